```python
import jax, jax.numpy as jnp
from jax import lax
import numpy as np

D_MODEL = 2048
BATCH = 8
SEQ = 4096
DEPTH = 1

N_META = 16
CHUNK = 128
Q_BLOCK = 128
RET_HEADS = 8
RET_QK_DIM = 128
RET_V_DIM = 128
MLA_HEADS = 8
MLA_NOPE = 128
MLA_ROPE = 64
MLA_V = 128
MLA_Q_RANK = 512
MLA_KV_RANK = 256
D_MIX = RET_HEADS * RET_V_DIM + MLA_HEADS * MLA_V
IN_SIZES = (RET_HEADS * RET_QK_DIM, RET_HEADS * RET_QK_DIM, RET_HEADS * RET_V_DIM,
            RET_HEADS * RET_V_DIM, MLA_Q_RANK, MLA_KV_RANK, MLA_ROPE)
D_IN = sum(IN_SIZES)
D_FF = 5632
ROPE_THETA = 10000.0
EPS = 1e-6

kernel_name = "hymba_retnet_mla_macaron_sandwich"


def rmsnorm(x, w):
    x32 = x.astype(jnp.float32)
    y = x32 * lax.rsqrt(jnp.mean(x32 * x32, axis=-1, keepdims=True) + EPS)
    return (y * w.astype(jnp.float32)).astype(x.dtype)


def swiglu(h, w_gate, w_up, w_down):
    return (jax.nn.silu(h @ w_gate) * (h @ w_up)) @ w_down


def rope_tables(pos, dim):
    inv = ROPE_THETA ** (-jnp.arange(0, dim, 2, dtype=jnp.float32) / dim)
    ang = pos[:, None] * inv[None, :]
    return jnp.cos(ang), jnp.sin(ang)


def apply_rope(x, cos, sin):
    x32 = x.astype(jnp.float32)
    x1, x2 = jnp.split(x32, 2, axis=-1)
    out = jnp.concatenate([x1 * cos - x2 * sin, x2 * cos + x1 * sin], axis=-1)
    return out.astype(x.dtype)


def retention(q, k, v):
    b, seq_len, n_heads, dv = v.shape
    dk = q.shape[-1]
    log_g = jnp.log(1.0 - 2.0 ** (-5.0 - jnp.arange(n_heads, dtype=jnp.float32)))

    def decay_mask(n):
        idx = jnp.arange(n, dtype=jnp.float32)
        diff = idx[:, None] - idx[None, :]
        return jnp.where(diff[None] >= 0,
                         jnp.exp(jnp.maximum(diff, 0.0)[None] * log_g[:, None, None]), 0.0)

    def intra(qc, kc, vc, dmask):
        s = jnp.einsum('bnhd,bmhd->bhnm', qc, kc) * dmask[None]
        return jnp.einsum('bhnm,bmhv->bnhv', s, vc)

    qm, km, vm = q[:, :N_META], k[:, :N_META], v[:, :N_META]
    o_meta = intra(qm, km, vm, decay_mask(N_META))
    pos_m = jnp.arange(N_META, dtype=jnp.float32)
    zeta_m = jnp.exp((N_META - 1 - pos_m)[:, None] * log_g[None, :])
    state0 = jnp.einsum('bmhd,bmhv,mh->bhdv', km, vm, zeta_m)

    n_chunks = (seq_len - N_META) // CHUNK

    def to_chunks(t):
        return t[:, N_META:].reshape(b, n_chunks, CHUNK, n_heads, t.shape[-1]).transpose(1, 0, 2, 3, 4)

    dmask_c = decay_mask(CHUNK)
    pos_c = jnp.arange(CHUNK, dtype=jnp.float32)
    xi = jnp.exp((pos_c + 1.0)[:, None] * log_g[None, :])
    zeta = jnp.exp((CHUNK - 1 - pos_c)[:, None] * log_g[None, :])
    g_chunk = jnp.exp(CHUNK * log_g)

    def step(state, qkv):
        qc, kc, vc = qkv
        o = intra(qc, kc, vc, dmask_c) + \
            jnp.einsum('bnhd,bhdv->bnhv', qc, state) * xi[None, :, :, None]
        state = state * g_chunk[None, :, None, None] + \
            jnp.einsum('bmhd,bmhv,mh->bhdv', kc, vc, zeta)
        return state, o

    _, o_real = lax.scan(step, state0, (to_chunks(q), to_chunks(k), to_chunks(v)))
    o_real = o_real.transpose(1, 0, 2, 3, 4).reshape(b, n_chunks * CHUNK, n_heads, dv)
    return jnp.concatenate([o_meta, o_real], axis=1)


def mla_attention(q_nope, q_rope, k_nope, k_rope, v):
    b, seq_len, n_heads, dv = v.shape
    scale = (MLA_NOPE + MLA_ROPE) ** -0.5
    kpos = jnp.arange(seq_len, dtype=jnp.int32)

    def attend(qn, qr, qpos, kn, kr, vv, kp):
        s = (jnp.einsum('bqhd,bkhd->bhqk', qn, kn) +
             jnp.einsum('bqhr,bkr->bhqk', qr, kr)).astype(jnp.float32) * scale
        s = jnp.where(kp[None, :] <= qpos[:, None], s, -1e30)
        p = jax.nn.softmax(s, axis=-1)
        return jnp.einsum('bhqk,bkhv->bqhv', p.astype(vv.dtype), vv)

    o_meta = attend(q_nope[:, :N_META], q_rope[:, :N_META], kpos[:N_META],
                    k_nope[:, :N_META], k_rope[:, :N_META], v[:, :N_META], kpos[:N_META])

    n_blocks = (seq_len - N_META) // Q_BLOCK

    def blocks(t):
        return t[:, N_META:].reshape(b, n_blocks, Q_BLOCK, *t.shape[2:]).swapaxes(0, 1)

    qpos_blocks = (N_META + jnp.arange(seq_len - N_META, dtype=jnp.int32)).reshape(n_blocks, Q_BLOCK)
    o_real = lax.map(lambda a: attend(a[0], a[1], a[2], k_nope, k_rope, v, kpos),
                     (blocks(q_nope), blocks(q_rope), qpos_blocks))
    o_real = o_real.swapaxes(0, 1).reshape(b, seq_len - N_META, n_heads, dv)
    return jnp.concatenate([o_meta, o_real], axis=1)


def hybrid_mixer(u, w_in, ret_group_norm, mla_q_norm, mla_w_uq, mla_kv_norm,
                 mla_w_uk, mla_w_uv, w_out, cos_r, sin_r, cos_m, sin_m):
    b, seq_len, _ = u.shape
    proj = u @ w_in
    offsets = np.cumsum(IN_SIZES)[:-1].tolist()
    rq, rk, rv, rg, cq, ckv, kr = jnp.split(proj, offsets, axis=-1)

    rq = apply_rope(rq.reshape(b, seq_len, RET_HEADS, RET_QK_DIM), cos_r[:, None], sin_r[:, None])
    rk = apply_rope(rk.reshape(b, seq_len, RET_HEADS, RET_QK_DIM), cos_r[:, None], sin_r[:, None])
    rv = rv.reshape(b, seq_len, RET_HEADS, RET_V_DIM)
    ret = retention(rq.astype(jnp.float32),
                    rk.astype(jnp.float32) * (RET_QK_DIM ** -0.5),
                    rv.astype(jnp.float32))
    ret = ret * lax.rsqrt(jnp.mean(ret * ret, axis=-1, keepdims=True) + EPS)
    ret = ret.reshape(b, seq_len, RET_HEADS * RET_V_DIM) * ret_group_norm.astype(jnp.float32)
    ret = (jax.nn.silu(rg.astype(jnp.float32)) * ret).astype(u.dtype)

    cq = rmsnorm(cq, mla_q_norm)
    q = (cq @ mla_w_uq).reshape(b, seq_len, MLA_HEADS, MLA_NOPE + MLA_ROPE)
    q_nope, q_rope = q[..., :MLA_NOPE], q[..., MLA_NOPE:]
    q_rope = apply_rope(q_rope, cos_m[:, None], sin_m[:, None])
    ckv = rmsnorm(ckv, mla_kv_norm)
    k_nope = (ckv @ mla_w_uk).reshape(b, seq_len, MLA_HEADS, MLA_NOPE)
    v = (ckv @ mla_w_uv).reshape(b, seq_len, MLA_HEADS, MLA_V)
    k_rope = apply_rope(kr, cos_m, sin_m)
    mla = mla_attention(q_nope, q_rope, k_nope, k_rope, v).reshape(b, seq_len, MLA_HEADS * MLA_V)

    return jnp.concatenate([ret, mla.astype(u.dtype)], axis=-1) @ w_out


def _normal(k, shape, fan_in):
    return jax.random.normal(k, shape, jnp.float32) * (fan_in ** -0.5)


def _gain(k, shape):
    return 1.0 + 0.02 * jax.random.normal(k, shape, jnp.float32)


def _fwd_setup_inputs(seed: int = 0) -> dict:
    key = jax.random.key(seed)
    ks = jax.random.split(key, 24)
    L = DEPTH
    return {
        "x": jax.random.normal(ks[0], (BATCH, SEQ, D_MODEL), jnp.float32),
        "meta_tokens": jax.random.normal(ks[1], (N_META, D_MODEL), jnp.float32),
        "ffn1_pre_norm": _gain(ks[2], (L, D_MODEL)),
        "ffn1_w_gate": _normal(ks[3], (L, D_MODEL, D_FF), D_MODEL),
        "ffn1_w_up": _normal(ks[4], (L, D_MODEL, D_FF), D_MODEL),
        "ffn1_w_down": _normal(ks[5], (L, D_FF, D_MODEL), D_FF),
        "ffn1_post_norm": _gain(ks[6], (L, D_MODEL)),
        "mix_pre_norm": _gain(ks[7], (L, D_MODEL)),
        "w_in": _normal(ks[8], (L, D_MODEL, D_IN), D_MODEL),
        "ret_group_norm": _gain(ks[9], (L, RET_HEADS * RET_V_DIM)),
        "mla_q_norm": _gain(ks[10], (L, MLA_Q_RANK)),
        "mla_w_uq": _normal(ks[11], (L, MLA_Q_RANK, MLA_HEADS * (MLA_NOPE + MLA_ROPE)), MLA_Q_RANK),
        "mla_kv_norm": _gain(ks[12], (L, MLA_KV_RANK)),
        "mla_w_uk": _normal(ks[13], (L, MLA_KV_RANK, MLA_HEADS * MLA_NOPE), MLA_KV_RANK),
        "mla_w_uv": _normal(ks[14], (L, MLA_KV_RANK, MLA_HEADS * MLA_V), MLA_KV_RANK),
        "w_out": _normal(ks[15], (L, D_MIX, D_MODEL), D_MIX),
        "mix_post_norm": _gain(ks[16], (L, D_MODEL)),
        "ffn2_pre_norm": _gain(ks[17], (L, D_MODEL)),
        "ffn2_w_gate": _normal(ks[18], (L, D_MODEL, D_FF), D_MODEL),
        "ffn2_w_up": _normal(ks[19], (L, D_MODEL, D_FF), D_MODEL),
        "ffn2_w_down": _normal(ks[20], (L, D_FF, D_MODEL), D_FF),
        "ffn2_post_norm": _gain(ks[21], (L, D_MODEL)),
    }


def _fwd_reference(x, meta_tokens, ffn1_pre_norm, ffn1_w_gate, ffn1_w_up, ffn1_w_down,
              ffn1_post_norm, mix_pre_norm, w_in, ret_group_norm, mla_q_norm, mla_w_uq,
              mla_kv_norm, mla_w_uk, mla_w_uv, w_out, mix_post_norm, ffn2_pre_norm,
              ffn2_w_gate, ffn2_w_up, ffn2_w_down, ffn2_post_norm):
    b = x.shape[0]
    meta = jnp.broadcast_to(meta_tokens.astype(x.dtype)[None], (b, N_META, x.shape[-1]))
    h = jnp.concatenate([meta, x], axis=1)
    seq_len = h.shape[1]
    pos = jnp.arange(seq_len, dtype=jnp.float32)
    cos_r, sin_r = rope_tables(pos, RET_QK_DIM)
    cos_m, sin_m = rope_tables(pos, MLA_ROPE)
    for l in range(DEPTH):
        f = swiglu(rmsnorm(h, ffn1_pre_norm[l]), ffn1_w_gate[l], ffn1_w_up[l], ffn1_w_down[l])
        h = h + 0.5 * rmsnorm(f, ffn1_post_norm[l])
        m = hybrid_mixer(rmsnorm(h, mix_pre_norm[l]), w_in[l], ret_group_norm[l], mla_q_norm[l],
                         mla_w_uq[l], mla_kv_norm[l], mla_w_uk[l], mla_w_uv[l], w_out[l],
                         cos_r, sin_r, cos_m, sin_m)
        h = h + rmsnorm(m, mix_post_norm[l])
        f = swiglu(rmsnorm(h, ffn2_pre_norm[l]), ffn2_w_gate[l], ffn2_w_up[l], ffn2_w_down[l])
        h = h + 0.5 * rmsnorm(f, ffn2_post_norm[l])
    return h[:, N_META:]


import jax as _jax
import jax.numpy as _jnp

TWIN_FORMAT = 'train_step'
FWD_PARAMS = ['x', 'meta_tokens', 'ffn1_pre_norm', 'ffn1_w_gate', 'ffn1_w_up', 'ffn1_w_down', 'ffn1_post_norm', 'mix_pre_norm', 'w_in', 'ret_group_norm', 'mla_q_norm', 'mla_w_uq', 'mla_kv_norm', 'mla_w_uk', 'mla_w_uv', 'w_out', 'mix_post_norm', 'ffn2_pre_norm', 'ffn2_w_gate', 'ffn2_w_up', 'ffn2_w_down', 'ffn2_post_norm']
TWIN_WEIGHTS = ['meta_tokens', 'ffn1_pre_norm', 'ffn1_w_gate', 'ffn1_w_up', 'ffn1_w_down', 'ffn1_post_norm', 'mix_pre_norm', 'w_in', 'ret_group_norm', 'mla_q_norm', 'mla_w_uq', 'mla_kv_norm', 'mla_w_uk', 'mla_w_uv', 'w_out', 'mix_post_norm', 'ffn2_pre_norm', 'ffn2_w_gate', 'ffn2_w_up', 'ffn2_w_down', 'ffn2_post_norm']
TWIN_DIFF_INPUT = 'x'
TWIN_INPUTS = ['x', 'meta_tokens', 'ffn1_pre_norm', 'ffn1_w_gate', 'ffn1_w_up', 'ffn1_w_down', 'ffn1_post_norm', 'mix_pre_norm', 'w_in', 'ret_group_norm', 'mla_q_norm', 'mla_w_uq', 'mla_kv_norm', 'mla_w_uk', 'mla_w_uv', 'w_out', 'mix_post_norm', 'ffn2_pre_norm', 'ffn2_w_gate', 'ffn2_w_up', 'ffn2_w_down', 'ffn2_post_norm', 'loss_target', 'm_meta_tokens', 'm_ffn1_pre_norm', 'm_ffn1_w_gate', 'm_ffn1_w_up', 'm_ffn1_w_down', 'm_ffn1_post_norm', 'm_mix_pre_norm', 'm_w_in', 'm_ret_group_norm', 'm_mla_q_norm', 'm_mla_w_uq', 'm_mla_kv_norm', 'm_mla_w_uk', 'm_mla_w_uv', 'm_w_out', 'm_mix_post_norm', 'm_ffn2_pre_norm', 'm_ffn2_w_gate', 'm_ffn2_w_up', 'm_ffn2_w_down', 'm_ffn2_post_norm', 'v_meta_tokens', 'v_ffn1_pre_norm', 'v_ffn1_w_gate', 'v_ffn1_w_up', 'v_ffn1_w_down', 'v_ffn1_post_norm', 'v_mix_pre_norm', 'v_w_in', 'v_ret_group_norm', 'v_mla_q_norm', 'v_mla_w_uq', 'v_mla_kv_norm', 'v_mla_w_uk', 'v_mla_w_uv', 'v_w_out', 'v_mix_post_norm', 'v_ffn2_pre_norm', 'v_ffn2_w_gate', 'v_ffn2_w_up', 'v_ffn2_w_down', 'v_ffn2_post_norm']
TWIN_OUTPUTS = ['loss', 'grad_x', 'grad_meta_tokens', 'grad_ffn1_pre_norm', 'grad_ffn1_w_gate', 'grad_ffn1_w_up', 'grad_ffn1_w_down', 'grad_ffn1_post_norm', 'grad_mix_pre_norm', 'grad_w_in', 'grad_ret_group_norm', 'grad_mla_q_norm', 'grad_mla_w_uq', 'grad_mla_kv_norm', 'grad_mla_w_uk', 'grad_mla_w_uv', 'grad_w_out', 'grad_mix_post_norm', 'grad_ffn2_pre_norm', 'grad_ffn2_w_gate', 'grad_ffn2_w_up', 'grad_ffn2_w_down', 'grad_ffn2_post_norm', 'delta_meta_tokens', 'delta_ffn1_pre_norm', 'delta_ffn1_w_gate', 'delta_ffn1_w_up', 'delta_ffn1_w_down', 'delta_ffn1_post_norm', 'delta_mix_pre_norm', 'delta_w_in', 'delta_ret_group_norm', 'delta_mla_q_norm', 'delta_mla_w_uq', 'delta_mla_kv_norm', 'delta_mla_w_uk', 'delta_mla_w_uv', 'delta_w_out', 'delta_mix_post_norm', 'delta_ffn2_pre_norm', 'delta_ffn2_w_gate', 'delta_ffn2_w_up', 'delta_ffn2_w_down', 'delta_ffn2_post_norm', 'new_m_meta_tokens', 'new_m_ffn1_pre_norm', 'new_m_ffn1_w_gate', 'new_m_ffn1_w_up', 'new_m_ffn1_w_down', 'new_m_ffn1_post_norm', 'new_m_mix_pre_norm', 'new_m_w_in', 'new_m_ret_group_norm', 'new_m_mla_q_norm', 'new_m_mla_w_uq', 'new_m_mla_kv_norm', 'new_m_mla_w_uk', 'new_m_mla_w_uv', 'new_m_w_out', 'new_m_mix_post_norm', 'new_m_ffn2_pre_norm', 'new_m_ffn2_w_gate', 'new_m_ffn2_w_up', 'new_m_ffn2_w_down', 'new_m_ffn2_post_norm', 'new_v_meta_tokens', 'new_v_ffn1_pre_norm', 'new_v_ffn1_w_gate', 'new_v_ffn1_w_up', 'new_v_ffn1_w_down', 'new_v_ffn1_post_norm', 'new_v_mix_pre_norm', 'new_v_w_in', 'new_v_ret_group_norm', 'new_v_mla_q_norm', 'new_v_mla_w_uq', 'new_v_mla_kv_norm', 'new_v_mla_w_uk', 'new_v_mla_w_uv', 'new_v_w_out', 'new_v_mix_post_norm', 'new_v_ffn2_pre_norm', 'new_v_ffn2_w_gate', 'new_v_ffn2_w_up', 'new_v_ffn2_w_down', 'new_v_ffn2_post_norm']
TWIN_LEAF_KINDS = {'loss': 'loss', 'grad_x': 'grad_x', 'grad_meta_tokens': 'grad_w', 'grad_ffn1_pre_norm': 'grad_w', 'grad_ffn1_w_gate': 'grad_w', 'grad_ffn1_w_up': 'grad_w', 'grad_ffn1_w_down': 'grad_w', 'grad_ffn1_post_norm': 'grad_w', 'grad_mix_pre_norm': 'grad_w', 'grad_w_in': 'grad_w', 'grad_ret_group_norm': 'grad_w', 'grad_mla_q_norm': 'grad_w', 'grad_mla_w_uq': 'grad_w', 'grad_mla_kv_norm': 'grad_w', 'grad_mla_w_uk': 'grad_w', 'grad_mla_w_uv': 'grad_w', 'grad_w_out': 'grad_w', 'grad_mix_post_norm': 'grad_w', 'grad_ffn2_pre_norm': 'grad_w', 'grad_ffn2_w_gate': 'grad_w', 'grad_ffn2_w_up': 'grad_w', 'grad_ffn2_w_down': 'grad_w', 'grad_ffn2_post_norm': 'grad_w', 'delta_meta_tokens': 'delta_w', 'delta_ffn1_pre_norm': 'delta_w', 'delta_ffn1_w_gate': 'delta_w', 'delta_ffn1_w_up': 'delta_w', 'delta_ffn1_w_down': 'delta_w', 'delta_ffn1_post_norm': 'delta_w', 'delta_mix_pre_norm': 'delta_w', 'delta_w_in': 'delta_w', 'delta_ret_group_norm': 'delta_w', 'delta_mla_q_norm': 'delta_w', 'delta_mla_w_uq': 'delta_w', 'delta_mla_kv_norm': 'delta_w', 'delta_mla_w_uk': 'delta_w', 'delta_mla_w_uv': 'delta_w', 'delta_w_out': 'delta_w', 'delta_mix_post_norm': 'delta_w', 'delta_ffn2_pre_norm': 'delta_w', 'delta_ffn2_w_gate': 'delta_w', 'delta_ffn2_w_up': 'delta_w', 'delta_ffn2_w_down': 'delta_w', 'delta_ffn2_post_norm': 'delta_w', 'new_m_meta_tokens': 'new_m', 'new_m_ffn1_pre_norm': 'new_m', 'new_m_ffn1_w_gate': 'new_m', 'new_m_ffn1_w_up': 'new_m', 'new_m_ffn1_w_down': 'new_m', 'new_m_ffn1_post_norm': 'new_m', 'new_m_mix_pre_norm': 'new_m', 'new_m_w_in': 'new_m', 'new_m_ret_group_norm': 'new_m', 'new_m_mla_q_norm': 'new_m', 'new_m_mla_w_uq': 'new_m', 'new_m_mla_kv_norm': 'new_m', 'new_m_mla_w_uk': 'new_m', 'new_m_mla_w_uv': 'new_m', 'new_m_w_out': 'new_m', 'new_m_mix_post_norm': 'new_m', 'new_m_ffn2_pre_norm': 'new_m', 'new_m_ffn2_w_gate': 'new_m', 'new_m_ffn2_w_up': 'new_m', 'new_m_ffn2_w_down': 'new_m', 'new_m_ffn2_post_norm': 'new_m', 'new_v_meta_tokens': 'new_v', 'new_v_ffn1_pre_norm': 'new_v', 'new_v_ffn1_w_gate': 'new_v', 'new_v_ffn1_w_up': 'new_v', 'new_v_ffn1_w_down': 'new_v', 'new_v_ffn1_post_norm': 'new_v', 'new_v_mix_pre_norm': 'new_v', 'new_v_w_in': 'new_v', 'new_v_ret_group_norm': 'new_v', 'new_v_mla_q_norm': 'new_v', 'new_v_mla_w_uq': 'new_v', 'new_v_mla_kv_norm': 'new_v', 'new_v_mla_w_uk': 'new_v', 'new_v_mla_w_uv': 'new_v', 'new_v_w_out': 'new_v', 'new_v_mix_post_norm': 'new_v', 'new_v_ffn2_pre_norm': 'new_v', 'new_v_ffn2_w_gate': 'new_v', 'new_v_ffn2_w_up': 'new_v', 'new_v_ffn2_w_down': 'new_v', 'new_v_ffn2_post_norm': 'new_v'}


def _forward(args):
    return _fwd_reference(*[args[k] for k in FWD_PARAMS])


def _output_shape():
    def fwd():
        inp = _fwd_setup_inputs(0)
        return _fwd_reference(*[inp[k] for k in FWD_PARAMS])
    out = _jax.eval_shape(fwd)
    return out.shape, out.dtype

N_MICROBATCH = 1
ADAM_LR = 0.001
ADAM_B1 = 0.9
ADAM_B2 = 0.999
ADAM_EPS = 1e-08
ADAM_WD = 0.01
ADAM_STEP = 10
PER_EXAMPLE_BATCH_AXIS = {'x': 0, 'loss_target': 0}
SHARED_INPUTS = []
_WEIGHT_DTYPES = {'meta_tokens': _jnp.float32, 'ffn1_pre_norm': _jnp.float32, 'ffn1_w_gate': _jnp.float32, 'ffn1_w_up': _jnp.float32, 'ffn1_w_down': _jnp.float32, 'ffn1_post_norm': _jnp.float32, 'mix_pre_norm': _jnp.float32, 'w_in': _jnp.float32, 'ret_group_norm': _jnp.float32, 'mla_q_norm': _jnp.float32, 'mla_w_uq': _jnp.float32, 'mla_kv_norm': _jnp.float32, 'mla_w_uk': _jnp.float32, 'mla_w_uv': _jnp.float32, 'w_out': _jnp.float32, 'mix_post_norm': _jnp.float32, 'ffn2_pre_norm': _jnp.float32, 'ffn2_w_gate': _jnp.float32, 'ffn2_w_up': _jnp.float32, 'ffn2_w_down': _jnp.float32, 'ffn2_post_norm': _jnp.float32}
MOMENT_SCALE = {'meta_tokens': 2.294751e-02, 'ffn1_pre_norm': 2.452410e-01, 'ffn1_w_gate': 1.045420e-01, 'ffn1_w_up': 1.044845e-01, 'ffn1_w_down': 1.736696e-01, 'ffn1_post_norm': 3.984537e+00, 'mix_pre_norm': 3.438477e-01, 'w_in': 2.126938e-01, 'ret_group_norm': 2.260006e-01, 'mla_q_norm': 8.661843e-02, 'mla_w_uq': 4.911983e-02, 'mla_kv_norm': 1.857647e-01, 'mla_w_uk': 5.095502e-02, 'mla_w_uv': 6.724142e-02, 'w_out': 1.727564e-01, 'mix_post_norm': 1.603846e+01, 'ffn2_pre_norm': 1.423706e-01, 'ffn2_w_gate': 4.994725e-02, 'ffn2_w_up': 6.692243e-02, 'ffn2_w_down': 1.109307e-01, 'ffn2_post_norm': 4.006323e+00}


def _to_microbatches(a, axis):
    t = _jnp.moveaxis(a, axis, 0)
    t = t.reshape((N_MICROBATCH, t.shape[0] // N_MICROBATCH) + t.shape[1:])
    return _jnp.moveaxis(t, 1, axis + 1)


def setup_inputs(seed: int = 0) -> dict:
    inp = _fwd_setup_inputs(seed)
    key = _jax.random.fold_in(_jax.random.key(seed), 7919)
    shape, _ = _output_shape()
    out = dict(inp)
    out["loss_target"] = _jax.random.normal(_jax.random.fold_in(key, 0), shape, _jnp.float32)
    for i, name in enumerate(TWIN_WEIGHTS):
        w = inp[name].astype(_jnp.float32)
        if MOMENT_SCALE is None:
            s = _jnp.sqrt(_jnp.mean(_jnp.square(w)) + 1e-30)
        else:
            s = MOMENT_SCALE[name]
        km, kv = _jax.random.split(_jax.random.fold_in(key, i + 1))
        out[name] = w
        out["m_" + name] = s * _jax.random.normal(km, w.shape, _jnp.float32)
        out["v_" + name] = (s * s) * _jax.random.uniform(kv, w.shape, _jnp.float32, 0.5, 1.5)
    if N_MICROBATCH > 1:
        for name, axis in PER_EXAMPLE_BATCH_AXIS.items():
            out[name] = _to_microbatches(out[name], axis)
    return {'x': out['x'], 'meta_tokens': out['meta_tokens'], 'ffn1_pre_norm': out['ffn1_pre_norm'], 'ffn1_w_gate': out['ffn1_w_gate'], 'ffn1_w_up': out['ffn1_w_up'], 'ffn1_w_down': out['ffn1_w_down'], 'ffn1_post_norm': out['ffn1_post_norm'], 'mix_pre_norm': out['mix_pre_norm'], 'w_in': out['w_in'], 'ret_group_norm': out['ret_group_norm'], 'mla_q_norm': out['mla_q_norm'], 'mla_w_uq': out['mla_w_uq'], 'mla_kv_norm': out['mla_kv_norm'], 'mla_w_uk': out['mla_w_uk'], 'mla_w_uv': out['mla_w_uv'], 'w_out': out['w_out'], 'mix_post_norm': out['mix_post_norm'], 'ffn2_pre_norm': out['ffn2_pre_norm'], 'ffn2_w_gate': out['ffn2_w_gate'], 'ffn2_w_up': out['ffn2_w_up'], 'ffn2_w_down': out['ffn2_w_down'], 'ffn2_post_norm': out['ffn2_post_norm'], 'loss_target': out['loss_target'], 'm_meta_tokens': out['m_meta_tokens'], 'm_ffn1_pre_norm': out['m_ffn1_pre_norm'], 'm_ffn1_w_gate': out['m_ffn1_w_gate'], 'm_ffn1_w_up': out['m_ffn1_w_up'], 'm_ffn1_w_down': out['m_ffn1_w_down'], 'm_ffn1_post_norm': out['m_ffn1_post_norm'], 'm_mix_pre_norm': out['m_mix_pre_norm'], 'm_w_in': out['m_w_in'], 'm_ret_group_norm': out['m_ret_group_norm'], 'm_mla_q_norm': out['m_mla_q_norm'], 'm_mla_w_uq': out['m_mla_w_uq'], 'm_mla_kv_norm': out['m_mla_kv_norm'], 'm_mla_w_uk': out['m_mla_w_uk'], 'm_mla_w_uv': out['m_mla_w_uv'], 'm_w_out': out['m_w_out'], 'm_mix_post_norm': out['m_mix_post_norm'], 'm_ffn2_pre_norm': out['m_ffn2_pre_norm'], 'm_ffn2_w_gate': out['m_ffn2_w_gate'], 'm_ffn2_w_up': out['m_ffn2_w_up'], 'm_ffn2_w_down': out['m_ffn2_w_down'], 'm_ffn2_post_norm': out['m_ffn2_post_norm'], 'v_meta_tokens': out['v_meta_tokens'], 'v_ffn1_pre_norm': out['v_ffn1_pre_norm'], 'v_ffn1_w_gate': out['v_ffn1_w_gate'], 'v_ffn1_w_up': out['v_ffn1_w_up'], 'v_ffn1_w_down': out['v_ffn1_w_down'], 'v_ffn1_post_norm': out['v_ffn1_post_norm'], 'v_mix_pre_norm': out['v_mix_pre_norm'], 'v_w_in': out['v_w_in'], 'v_ret_group_norm': out['v_ret_group_norm'], 'v_mla_q_norm': out['v_mla_q_norm'], 'v_mla_w_uq': out['v_mla_w_uq'], 'v_mla_kv_norm': out['v_mla_kv_norm'], 'v_mla_w_uk': out['v_mla_w_uk'], 'v_mla_w_uv': out['v_mla_w_uv'], 'v_w_out': out['v_w_out'], 'v_mix_post_norm': out['v_mix_post_norm'], 'v_ffn2_pre_norm': out['v_ffn2_pre_norm'], 'v_ffn2_w_gate': out['v_ffn2_w_gate'], 'v_ffn2_w_up': out['v_ffn2_w_up'], 'v_ffn2_w_down': out['v_ffn2_w_down'], 'v_ffn2_post_norm': out['v_ffn2_post_norm']}


def _loss(weights, diff, rest, loss_target):
    with _jax.named_scope("forward"):
        args = {**rest, TWIN_DIFF_INPUT: diff, **{k: w.astype(_WEIGHT_DTYPES[k]) for k, w in weights.items()}}
        y = _forward(args)
    with _jax.named_scope("loss_head"):
        err = _jnp.square(y.astype(_jnp.float32) - loss_target)
        return 0.5 * _jnp.sum(_jnp.mean(err, axis=-1)) if err.ndim else 0.5 * err


def _adamw(w, g, m, v):
    m = ADAM_B1 * m + (1.0 - ADAM_B1) * g
    v = ADAM_B2 * v + (1.0 - ADAM_B2) * _jnp.square(g)
    m_hat = m / (1.0 - ADAM_B1 ** ADAM_STEP)
    v_hat = v / (1.0 - ADAM_B2 ** ADAM_STEP)
    delta = -ADAM_LR * (m_hat / (_jnp.sqrt(v_hat) + ADAM_EPS) + ADAM_WD * w)
    return delta, m, v


def reference(x, meta_tokens, ffn1_pre_norm, ffn1_w_gate, ffn1_w_up, ffn1_w_down, ffn1_post_norm, mix_pre_norm, w_in, ret_group_norm, mla_q_norm, mla_w_uq, mla_kv_norm, mla_w_uk, mla_w_uv, w_out, mix_post_norm, ffn2_pre_norm, ffn2_w_gate, ffn2_w_up, ffn2_w_down, ffn2_post_norm, loss_target, m_meta_tokens, m_ffn1_pre_norm, m_ffn1_w_gate, m_ffn1_w_up, m_ffn1_w_down, m_ffn1_post_norm, m_mix_pre_norm, m_w_in, m_ret_group_norm, m_mla_q_norm, m_mla_w_uq, m_mla_kv_norm, m_mla_w_uk, m_mla_w_uv, m_w_out, m_mix_post_norm, m_ffn2_pre_norm, m_ffn2_w_gate, m_ffn2_w_up, m_ffn2_w_down, m_ffn2_post_norm, v_meta_tokens, v_ffn1_pre_norm, v_ffn1_w_gate, v_ffn1_w_up, v_ffn1_w_down, v_ffn1_post_norm, v_mix_pre_norm, v_w_in, v_ret_group_norm, v_mla_q_norm, v_mla_w_uq, v_mla_kv_norm, v_mla_w_uk, v_mla_w_uv, v_w_out, v_mix_post_norm, v_ffn2_pre_norm, v_ffn2_w_gate, v_ffn2_w_up, v_ffn2_w_down, v_ffn2_post_norm):
    given = dict(x=x, meta_tokens=meta_tokens, ffn1_pre_norm=ffn1_pre_norm, ffn1_w_gate=ffn1_w_gate, ffn1_w_up=ffn1_w_up, ffn1_w_down=ffn1_w_down, ffn1_post_norm=ffn1_post_norm, mix_pre_norm=mix_pre_norm, w_in=w_in, ret_group_norm=ret_group_norm, mla_q_norm=mla_q_norm, mla_w_uq=mla_w_uq, mla_kv_norm=mla_kv_norm, mla_w_uk=mla_w_uk, mla_w_uv=mla_w_uv, w_out=w_out, mix_post_norm=mix_post_norm, ffn2_pre_norm=ffn2_pre_norm, ffn2_w_gate=ffn2_w_gate, ffn2_w_up=ffn2_w_up, ffn2_w_down=ffn2_w_down, ffn2_post_norm=ffn2_post_norm, loss_target=loss_target, m_meta_tokens=m_meta_tokens, m_ffn1_pre_norm=m_ffn1_pre_norm, m_ffn1_w_gate=m_ffn1_w_gate, m_ffn1_w_up=m_ffn1_w_up, m_ffn1_w_down=m_ffn1_w_down, m_ffn1_post_norm=m_ffn1_post_norm, m_mix_pre_norm=m_mix_pre_norm, m_w_in=m_w_in, m_ret_group_norm=m_ret_group_norm, m_mla_q_norm=m_mla_q_norm, m_mla_w_uq=m_mla_w_uq, m_mla_kv_norm=m_mla_kv_norm, m_mla_w_uk=m_mla_w_uk, m_mla_w_uv=m_mla_w_uv, m_w_out=m_w_out, m_mix_post_norm=m_mix_post_norm, m_ffn2_pre_norm=m_ffn2_pre_norm, m_ffn2_w_gate=m_ffn2_w_gate, m_ffn2_w_up=m_ffn2_w_up, m_ffn2_w_down=m_ffn2_w_down, m_ffn2_post_norm=m_ffn2_post_norm, v_meta_tokens=v_meta_tokens, v_ffn1_pre_norm=v_ffn1_pre_norm, v_ffn1_w_gate=v_ffn1_w_gate, v_ffn1_w_up=v_ffn1_w_up, v_ffn1_w_down=v_ffn1_w_down, v_ffn1_post_norm=v_ffn1_post_norm, v_mix_pre_norm=v_mix_pre_norm, v_w_in=v_w_in, v_ret_group_norm=v_ret_group_norm, v_mla_q_norm=v_mla_q_norm, v_mla_w_uq=v_mla_w_uq, v_mla_kv_norm=v_mla_kv_norm, v_mla_w_uk=v_mla_w_uk, v_mla_w_uv=v_mla_w_uv, v_w_out=v_w_out, v_mix_post_norm=v_mix_post_norm, v_ffn2_pre_norm=v_ffn2_pre_norm, v_ffn2_w_gate=v_ffn2_w_gate, v_ffn2_w_up=v_ffn2_w_up, v_ffn2_w_down=v_ffn2_w_down, v_ffn2_post_norm=v_ffn2_post_norm)
    weights = {n: given[n] for n in TWIN_WEIGHTS}
    shared = {n: given[n] for n in SHARED_INPUTS}
    per_example = {n: given[n] for n in ['x']}
    grad_fn = _jax.value_and_grad(_loss, argnums=(0, 1))

    def one_microbatch(ex, loss_target):
        ex = dict(ex)
        diff = ex.pop(TWIN_DIFF_INPUT)
        return grad_fn(weights, diff, {**shared, **ex}, loss_target)

    if N_MICROBATCH == 1:
        loss, (grad_w, grad_x) = one_microbatch(per_example, given["loss_target"])
    else:
        def body(carry, xs):
            loss_sum, grad_sum = carry
            l_k, (gw_k, gx_k) = one_microbatch(xs[0], xs[1])
            with _jax.named_scope("update"):
                return (loss_sum + l_k, _jax.tree.map(_jnp.add, grad_sum, gw_k)), gx_k

        init = (_jnp.zeros((), _jnp.float32), _jax.tree.map(_jnp.zeros_like, weights))
        (loss, grad_w), grad_x = _jax.lax.scan(body, init, (per_example, given["loss_target"]))
    with _jax.named_scope("update"):
        delta_w, new_m, new_v = {}, {}, {}
        for n in TWIN_WEIGHTS:
            delta_w[n], new_m[n], new_v[n] = _adamw(weights[n], grad_w[n], given["m_" + n], given["v_" + n])
    return (loss, grad_x, *[grad_w[n] for n in TWIN_WEIGHTS], *[delta_w[n] for n in TWIN_WEIGHTS],
            *[new_m[n] for n in TWIN_WEIGHTS], *[new_v[n] for n in TWIN_WEIGHTS])
```

```python
import functools
import math

import jax
import jax.numpy as jnp
from jax import lax
from jax.experimental import pallas as pl
from jax.experimental.pallas import tpu as pltpu

F32 = jnp.float32
BF16 = jnp.bfloat16

EPS = 1e-6
N_META = 16
HEADS = 8
HEAD_DIM = 128
MLA_ROPE = 64
MLA_QK_PAD = 256
MLA_Q_RANK = 512
MLA_KV_RANK = 256
ROPE_THETA = 10000.0
N_CHIPS = 4
LANES = 128
VMEM_LIMIT = 60 * 2 ** 20

ADAM_LR = 0.001
ADAM_B1 = 0.9
ADAM_B2 = 0.999
ADAM_EPS = 1e-08
ADAM_WD = 0.01
ADAM_STEP = 10

NN = (((1,), (0,)), ((), ()))
NT = (((1,), (1,)), ((), ()))
TN = (((0,), (0,)), ((), ()))
MESH = pl.DeviceIdType.MESH


def _tile(n, pref, align=LANES):
    if n <= pref:
        return n
    best = 0
    for t in range(align, pref + 1, align):
        if n % t == 0:
            best = t
    assert best, (n, pref)
    return best


def _params(*sem):
    return pltpu.CompilerParams(dimension_semantics=sem, vmem_limit_bytes=VMEM_LIMIT)


def _sds(shape, dtype):
    return jax.ShapeDtypeStruct(tuple(shape), dtype)


def _rowwise(name, fn, rows, consts, outs, accs, tr):
    rows = [r if isinstance(r, tuple) else (r, r.shape[1], 0) for r in rows]
    t = rows[0][0].shape[0]
    assert t % tr == 0
    n_r, n_c, n_o = len(rows), len(consts), len(outs)

    def body(*refs):
        i = pl.program_id(0)
        r = [x[...] for x in refs[:n_r]]
        c = [x[...] for x in refs[n_r:n_r + n_c]]
        o_refs = refs[n_r + n_c:n_r + n_c + n_o]
        a_refs = refs[n_r + n_c + n_o:]
        o_vals, a_vals = fn(i * tr, *r, *c)
        for ref, v in zip(o_refs, o_vals):
            ref[...] = v.astype(ref.dtype)
        if a_refs:
            @pl.when(i == 0)
            def _():
                for ref, v in zip(a_refs, a_vals):
                    ref[...] = v

            @pl.when(i > 0)
            def _():
                for ref, v in zip(a_refs, a_vals):
                    ref[...] += v

    in_specs = [pl.BlockSpec((tr, w), functools.partial(lambda cb, i: (i, cb), cb)) for _, w, cb in rows]
    in_specs += [pl.BlockSpec(a.shape, lambda i: (0, 0)) for a in consts]
    out_specs = [pl.BlockSpec((tr, w), lambda i: (i, 0)) for w, _ in outs]
    out_specs += [pl.BlockSpec((1, w), lambda i: (0, 0)) for w in accs]
    out_shape = [_sds((t, w), dt) for w, dt in outs] + [_sds((1, w), F32) for w in accs]
    res = pl.pallas_call(
        body, name=name, grid=(t // tr,), in_specs=in_specs, out_specs=out_specs, out_shape=out_shape,
        compiler_params=_params("arbitrary"),
    )(*[a for a, _, _ in rows], *consts)
    return res[:n_o], res[n_o:]


def _rms(x, w):
    r = lax.rsqrt(jnp.mean(x * x, axis=-1, keepdims=True) + EPS)
    return x * r * w


def _rms_bwd(x, w, dy):
    r = lax.rsqrt(jnp.mean(x * x, axis=-1, keepdims=True) + EPS)
    xh = x * r
    gy = dy * w
    dx = r * (gy - xh * jnp.mean(gy * xh, axis=-1, keepdims=True))
    return dx, dy * xh


def _colsum(v):
    return jnp.sum(v, axis=0, keepdims=True)


def _silu(x):
    return x * jax.nn.sigmoid(x)


def _dsilu(x):
    s = jax.nn.sigmoid(x)
    return s * (1.0 + x * (1.0 - s))


def _rope_r(x, cos, sin):
    return x * cos + pltpu.roll(x, 64, 1) * sin


def _rope_r_t(dy, cos, sin):
    return dy * cos + pltpu.roll(dy * sin, 64, 1)


def _rope_m(x, cos, sa, sb):
    return x * cos + pltpu.roll(x, 32, 1) * sa + pltpu.roll(x, 96, 1) * sb


def _rope_m_t(dy, cos, sa, sb):
    return dy * cos + pltpu.roll(dy * sa, 96, 1) + pltpu.roll(dy * sb, 32, 1)


def _rope_tables(t):
    pos = jnp.arange(t, dtype=F32)

    def cs(dim):
        inv = ROPE_THETA ** (-jnp.arange(0, dim, 2, dtype=F32) / dim)
        ang = pos[:, None] * inv[None, :]
        return jnp.cos(ang), jnp.sin(ang)

    c, s = cs(HEAD_DIM)
    cos_r = jnp.concatenate([c, c], axis=1)
    sin_r = jnp.concatenate([-s, s], axis=1)
    c, s = cs(MLA_ROPE)
    z32, z64 = jnp.zeros_like(s), jnp.zeros((t, 64), F32)
    cos_m = jnp.concatenate([c, c, z64], axis=1)
    sa = jnp.concatenate([z32, s, z64], axis=1)
    sb = jnp.concatenate([-s, z32, z64], axis=1)
    return cos_r, sin_r, cos_m, sa, sb


def _mm(name, grid, operands, in_specs, dns, out_specs, out_shape, epilogue=None, extras=(), extra_specs=(),
        acc_shape=None):
    n_p, n_e = len(dns), len(extras)
    nk = grid[2]
    n_o = len(out_shape)

    def body(*refs):
        ab = refs[:2 * n_p]
        ex = refs[2 * n_p:2 * n_p + n_e]
        outs = refs[2 * n_p + n_e:2 * n_p + n_e + n_o]
        part = None
        for p in range(n_p):
            d = lax.dot_general(ab[2 * p][...], ab[2 * p + 1][...], dns[p], preferred_element_type=F32)
            part = d if part is None else part + d

        def finish(acc):
            vals = (acc,) if epilogue is None else epilogue(acc, *[e[...] for e in ex])
            for o, v in zip(outs, vals):
                o[...] = v.astype(o.dtype)

        if nk == 1:
            finish(part)
        else:
            acc_ref = refs[-1]
            k = pl.program_id(2)

            @pl.when(k == 0)
            def _():
                acc_ref[...] = part

            @pl.when(k > 0)
            def _():
                acc_ref[...] += part

            @pl.when(k == nk - 1)
            def _():
                finish(acc_ref[...])

    scratch = [] if nk == 1 else [pltpu.VMEM(acc_shape, F32)]
    return pl.pallas_call(
        body, name=name, grid=grid, in_specs=list(in_specs) + list(extra_specs), out_specs=out_specs,
        out_shape=out_shape, scratch_shapes=scratch,
        compiler_params=_params("arbitrary", "arbitrary", "arbitrary"),
    )(*operands, *extras)


def mm_nn(name, x, w, tm, out_dtype=F32, epilogue=None, outs=None, w2=None, n_block=None):
    t, kdim = x.shape
    if w.ndim == 3:
        s, _, ns = w.shape
        n, tn, nb = s * ns, ns, s
        wspec = pl.BlockSpec((None, kdim, ns), lambda j, i, k: (j, 0, 0))
    else:
        n = w.shape[1]
        tn = n_block or n
        nb = n // tn
        wspec = pl.BlockSpec((kdim, tn), lambda j, i, k: (0, j))
    xspec = pl.BlockSpec((tm, kdim), lambda j, i, k: (i, 0))
    ospec = pl.BlockSpec((tm, tn), lambda j, i, k: (i, j))
    outs = outs or [out_dtype]
    grid = (nb, t // tm, 1)
    if w2 is None:
        return _mm(name, grid, [x, w], [xspec, wspec], [NN], [ospec] * len(outs), [_sds((t, n), d) for d in outs],
                   epilogue=epilogue)

    def body(x_ref, w_ref, w2_ref, *o_refs):
        xv = x_ref[...]
        a = jnp.dot(xv, w_ref[...], preferred_element_type=F32)
        b = jnp.dot(xv, w2_ref[...], preferred_element_type=F32)
        for o, v in zip(o_refs, epilogue(a, b)):
            o[...] = v.astype(o.dtype)

    return pl.pallas_call(
        body, name=name, grid=grid[:2],
        in_specs=[pl.BlockSpec((tm, kdim), lambda j, i: (i, 0)),
                  pl.BlockSpec((None, kdim, tn), lambda j, i: (j, 0, 0)),
                  pl.BlockSpec((None, kdim, tn), lambda j, i: (j, 0, 0))],
        out_specs=[pl.BlockSpec((tm, tn), lambda j, i: (i, j))] * len(outs),
        out_shape=[_sds((t, n), d) for d in outs],
        compiler_params=_params("arbitrary", "arbitrary"),
    )(x, w, w2)


def mm_nn_k(name, x, w, tm, tk, out_dtype=F32):
    t, kdim = x.shape
    n = w.shape[1]
    grid = (t // tm, 1, kdim // tk)
    return _mm(name, grid, [x, w],
               [pl.BlockSpec((tm, tk), lambda i, j, k: (i, k)), pl.BlockSpec((tk, n), lambda i, j, k: (k, 0))],
               [NN], [pl.BlockSpec((tm, n), lambda i, j, k: (i, 0))], [_sds((t, n), out_dtype)],
               acc_shape=(tm, n))[0]


def mm_nt(name, xs, ws, tm, tn, outs=(F32,), epilogue=None, extras=(), extra_blocked=True):
    t = xs[0].shape[0]
    specs, ops = [], []
    for x, w in zip(xs, ws):
        kdim = x.shape[1]
        specs.append(pl.BlockSpec((tm, kdim), lambda j, i, k: (i, 0)))
        if w.ndim == 3:
            assert tn == w.shape[1]
            n = w.shape[0] * w.shape[1]
            specs.append(pl.BlockSpec((None, tn, kdim), lambda j, i, k: (j, 0, 0)))
        else:
            n = w.shape[0]
            specs.append(pl.BlockSpec((tn, kdim), lambda j, i, k: (j, 0)))
        ops += [x, w]
    ospec = pl.BlockSpec((tm, tn), lambda j, i, k: (i, j))
    return _mm(name, (n // tn, t // tm, 1), ops, specs, [NT] * len(xs), [ospec] * len(outs),
               [_sds((t, n), d) for d in outs], epilogue=epilogue, extras=extras,
               extra_specs=[ospec] * len(extras))


def mm_nt_k(name, xs, ws, tm, tn, out_dtype=F32):
    t = xs[0].shape[0]
    s, n, ns = ws[0].shape
    specs, ops = [], []
    for x, w in zip(xs, ws):
        specs.append(pl.BlockSpec((tm, ns), lambda i, j, k: (i, k)))
        specs.append(pl.BlockSpec((None, tn, ns), lambda i, j, k: (k, j, 0)))
        ops += [x, w]
    return _mm(name, (t // tm, n // tn, s), ops, specs, [NT] * len(xs),
               [pl.BlockSpec((tm, tn), lambda i, j, k: (i, j))], [_sds((t, n), out_dtype)], acc_shape=(tm, tn))[0]


def mm_tn(name, x, y, tm, tn, tk, out_dtype, shard_rows=False, shard_cols=False):
    t, m = x.shape
    n = y.shape[1]
    grid = (m // tm, n // tn, t // tk)
    if shard_cols:
        ospec = pl.BlockSpec((None, tm, tn), lambda i, j, k: (j, i, 0))
        oshape = _sds((n // tn, m, tn), out_dtype)
    elif shard_rows:
        ospec = pl.BlockSpec((None, tm, tn), lambda i, j, k: (i, 0, j))
        oshape = _sds((m // tm, tm, n), out_dtype)
    else:
        ospec = pl.BlockSpec((tm, tn), lambda i, j, k: (i, j))
        oshape = _sds((m, n), out_dtype)
    return _mm(name, grid, [x, y],
               [pl.BlockSpec((tk, tm), lambda i, j, k: (k, i)), pl.BlockSpec((tk, tn), lambda i, j, k: (k, j))],
               [TN], [ospec], [oshape], acc_shape=(tm, tn))[0]


def _decay_logs():
    return [math.log(1.0 - 2.0 ** (-5.0 - h)) for h in range(HEADS)]


def _log_decay(h):
    lg = jnp.float32(_decay_logs()[0])
    for i in range(1, HEADS):
        lg = jnp.where(h == i, jnp.float32(_decay_logs()[i]), lg)
    return lg


def _scores(q, k, scale, qi, ki, blk, softmax, lg, diag):
    s = lax.dot_general(q, k, NT, preferred_element_type=F32)
    row = lax.broadcasted_iota(jnp.int32, s.shape, 0)
    col = lax.broadcasted_iota(jnp.int32, s.shape, 1)
    if softmax:
        s = s * scale
        if diag:
            s = jnp.where(col <= row, s, -1e30)
        return s
    dist = ((qi - ki) * blk + row - col).astype(F32)
    dec = jnp.exp(jnp.maximum(dist, 0.0) * lg)
    if diag:
        dec = jnp.where(col <= row, dec, 0.0)
    return s * dec, dec


def attn_fwd(name, q, k, v, blk, softmax, scale=1.0):
    t = q.shape[0]
    dq = q.shape[1] // HEADS
    nq = t // blk

    def body(q_ref, k_ref, v_ref, o_ref, *lse_ref):
        h, qi = pl.program_id(0), pl.program_id(1)
        lg = _log_decay(h)
        qv = q_ref[...]

        def block(ki, carry, diag):
            start = pl.multiple_of(ki * blk, blk)
            kv = k_ref[pl.ds(start, blk), :]
            vv = v_ref[pl.ds(start, blk), :]
            if softmax:
                m, l, acc = carry
                s = _scores(qv, kv, scale, qi, ki, blk, True, lg, diag)
                m_new = jnp.maximum(m, jnp.max(s, axis=-1, keepdims=True))
                p = jnp.exp(s - m_new)
                alpha = jnp.exp(m - m_new)
                l = alpha * l + jnp.sum(p, axis=-1, keepdims=True)
                acc = alpha * acc + jnp.dot(p.astype(BF16), vv, preferred_element_type=F32)
                return m_new, l, acc
            p, _ = _scores(qv, kv, scale, qi, ki, blk, False, lg, diag)
            return (carry[0] + jnp.dot(p.astype(BF16), vv, preferred_element_type=F32),)

        if softmax:
            init = (jnp.full((blk, 1), -1e30, F32), jnp.zeros((blk, 1), F32), jnp.zeros((blk, HEAD_DIM), F32))
        else:
            init = (jnp.zeros((blk, HEAD_DIM), F32),)
        carry = lax.fori_loop(0, qi, lambda ki, c: block(ki, c, False), init)
        carry = block(qi, carry, True)
        if softmax:
            m, l, acc = carry
            o_ref[...] = acc / l
            lse_ref[0][...] = jnp.broadcast_to(m + jnp.log(l), (blk, HEAD_DIM))
        else:
            o_ref[...] = carry[0]

    hspec = pl.BlockSpec((blk, HEAD_DIM), lambda h, i: (i, h))
    out_specs = [hspec, hspec] if softmax else [hspec]
    out_shape = [_sds((t, HEADS * HEAD_DIM), F32)] * (2 if softmax else 1)
    return pl.pallas_call(
        body, name=name, grid=(HEADS, nq),
        in_specs=[pl.BlockSpec((blk, dq), lambda h, i: (i, h)),
                  pl.BlockSpec((t, dq), lambda h, i: (0, h)),
                  pl.BlockSpec((t, HEAD_DIM), lambda h, i: (0, h))],
        out_specs=out_specs, out_shape=out_shape, compiler_params=_params("arbitrary", "arbitrary"),
    )(q, k, v)


def attn_bwd(name, q, k, v, do, blk, softmax, scale=1.0, o=None, lse=None):
    t = q.shape[0]
    dq_w = q.shape[1] // HEADS
    nb = t // blk

    def body(q_ref, k_ref, v_ref, do_ref, *rest):
        if softmax:
            o_ref, lse_ref, dq_ref, dk_ref, dv_ref = rest
        else:
            dq_ref, dk_ref, dv_ref = rest
        h, ki = pl.program_id(0), pl.program_id(1)
        lg = _log_decay(h)
        kv = k_ref[...]
        vv = v_ref[...]

        @pl.when(ki == 0)
        def _():
            dq_ref[...] = jnp.zeros_like(dq_ref)

        def block(qi, carry, diag):
            dk, dv = carry
            start = pl.multiple_of(qi * blk, blk)
            rows = pl.ds(start, blk)
            qv = q_ref[rows, :]
            dov = do_ref[rows, :]
            dp = lax.dot_general(dov, vv, NT, preferred_element_type=F32)
            if softmax:
                s = _scores(qv, kv, scale, qi, ki, blk, True, lg, diag)
                p = jnp.exp(s - lse_ref[rows, :][:, :1])
                delta = jnp.sum(dov.astype(F32) * o_ref[rows, :], axis=-1, keepdims=True)
                ds = p * (dp - delta) * scale
            else:
                p, dec = _scores(qv, kv, scale, qi, ki, blk, False, lg, diag)
                ds = dp * dec
            pb, dsb = p.astype(BF16), ds.astype(BF16)
            dv = dv + lax.dot_general(pb, dov, TN, preferred_element_type=F32)
            dk = dk + lax.dot_general(dsb, qv, TN, preferred_element_type=F32)
            dq_ref[rows, :] += jnp.dot(dsb, kv, preferred_element_type=F32)
            return dk, dv

        carry = block(ki, (jnp.zeros((blk, dq_w), F32), jnp.zeros((blk, HEAD_DIM), F32)), True)
        dk, dv = lax.fori_loop(ki + 1, nb, lambda qi, c: block(qi, c, False), carry)
        dk_ref[...] = dk
        dv_ref[...] = dv

    full = lambda w: pl.BlockSpec((t, w), lambda h, j: (0, h))
    blkd = lambda w: pl.BlockSpec((blk, w), lambda h, j: (j, h))
    ins = [q, k, v, do]
    in_specs = [full(dq_w), blkd(dq_w), blkd(HEAD_DIM), full(HEAD_DIM)]
    if softmax:
        ins += [o, lse]
        in_specs += [full(HEAD_DIM), full(HEAD_DIM)]
    return pl.pallas_call(
        body, name=name, grid=(HEADS, nb), in_specs=in_specs,
        out_specs=[full(dq_w), blkd(dq_w), blkd(HEAD_DIM)],
        out_shape=[_sds(q.shape, F32), _sds(k.shape, F32), _sds(v.shape, F32)],
        compiler_params=_params("arbitrary", "arbitrary"),
    )(*ins)


CHIP_FLIPS = ((1, 0), (0, 1), (1, 1))


def _position():
    return lax.axis_index("x"), lax.axis_index("y"), lax.axis_index("c")


def _exchange(name, ins, out_shape, plan, n_remote, n_local):
    n_in, n_out = len(ins), len(out_shape)

    def body(*refs):
        in_refs, out_refs = refs[:n_in], refs[n_in:n_in + n_out]
        send_sems, recv_sems, local_sems = refs[n_in + n_out:]
        pos = _position()
        p = plan(pos, in_refs, out_refs)
        sends, recvs = p["sends"], p["recvs"]
        fwd, recvs2 = p.get("fwd", []), p.get("recvs2", [])
        n1 = len(sends)

        def remote(k, src, dst, dev):
            return pltpu.make_async_remote_copy(src_ref=src, dst_ref=dst, send_sem=send_sems.at[k],
                                                recv_sem=recv_sems.at[k], device_id=dev, device_id_type=MESH)

        local = [pltpu.make_async_copy(s, d, local_sems.at[i]) for i, (s, d) in enumerate(p["local"])]
        for cp in local:
            cp.start()
        out = [remote(k, s, d, dev) for k, (s, d, dev) in enumerate(sends)]
        for cp in out:
            cp.start()
        passed = [remote(n1 + k, s, d, dev) for k, (s, d, dev) in enumerate(fwd)]
        for k, dst in enumerate(recvs):
            remote(k, dst, dst, pos).wait_recv()
            if fwd:
                passed[k].start()
        for k, dst in enumerate(recvs2):
            remote(n1 + k, dst, dst, pos).wait_recv()
        for cp in out + passed:
            cp.wait_send()
        for cp in local:
            cp.wait()

    hbm = pl.BlockSpec(memory_space=pl.ANY)
    return pl.pallas_call(
        body, name=name, in_specs=[hbm] * n_in, out_specs=[hbm] * n_out, out_shape=out_shape,
        scratch_shapes=[pltpu.SemaphoreType.DMA((n_remote,)), pltpu.SemaphoreType.DMA((n_remote,)),
                        pltpu.SemaphoreType.DMA((n_local,))],
    )(*ins)


def _half_rows(c, rows):
    r2 = rows // 2
    return pl.ds(pl.multiple_of(c * r2, math.gcd(r2, LANES)), r2)


def _half(ref, c, rows, lead=()):
    return ref.at[(*lead, _half_rows(c, rows))]


def all_gather_chips(shards):
    def plan(pos, ins, outs):
        x, y, c = pos
        me = 2 * x + y
        p = dict(local=[], sends=[], recvs=[], fwd=[], recvs2=[])
        for a, (src, dst) in enumerate(zip(ins, outs)):
            rows = shards[a].shape[0]
            p["local"].append((src, dst.at[me]))
            for fx, fy in CHIP_FLIPS:
                px, py = x ^ fx, y ^ fy
                peer = 2 * px + py
                p["sends"].append((_half(src, c, rows), _half(dst, c, rows, (me,)), (px, py, c)))
                p["recvs"].append(_half(dst, c, rows, (peer,)))
                p["fwd"].append((_half(dst, c, rows, (peer,)), _half(dst, c, rows, (peer,)), (x, y, 1 - c)))
                p["recvs2"].append(_half(dst, 1 - c, rows, (peer,)))
        return p

    return _exchange("all_gather_chips", shards, [_sds((N_CHIPS, *s.shape), s.dtype) for s in shards], plan,
                     n_remote=6 * len(shards), n_local=len(shards))


def pair_exchange(grads):
    def plan(pos, ins, outs):
        x, y, c = pos
        p = dict(local=[], sends=[], recvs=[])
        for a, (src, dst) in enumerate(zip(ins, outs)):
            rows = grads[a].shape[1]
            p["local"].append((src.at[:, _half_rows(c, rows)], dst.at[c]))
            p["sends"].append((src.at[:, _half_rows(1 - c, rows)], dst.at[c], (x, y, 1 - c)))
            p["recvs"].append(dst.at[1 - c])
        return p

    return _exchange("grad_pair_exchange", grads,
                     [_sds((2, g.shape[0], g.shape[1] // 2, g.shape[2]), g.dtype) for g in grads], plan,
                     n_remote=len(grads), n_local=len(grads))


def chip_exchange(parts):
    def plan(pos, ins, outs):
        x, y, c = pos
        me = 2 * x + y
        p = dict(local=[], sends=[], recvs=[])
        for src, dst in zip(ins, outs):
            p["local"].append((src.at[me], dst.at[me]))
            for fx, fy in CHIP_FLIPS:
                px, py = x ^ fx, y ^ fy
                peer = 2 * px + py
                p["sends"].append((src.at[peer], dst.at[me], (px, py, c)))
                p["recvs"].append(dst.at[peer])
        return p

    return _exchange("grad_chip_exchange", parts, [_sds(g.shape, g.dtype) for g in parts], plan,
                     n_remote=3 * len(parts), n_local=len(parts))


def sibling_join(halves):
    def plan(pos, ins, outs):
        x, y, c = pos
        p = dict(local=[], sends=[], recvs=[])
        for a, (src, dst) in enumerate(zip(ins, outs)):
            rows = 2 * halves[a].shape[0]
            p["local"].append((src, _half(dst, c, rows)))
            p["sends"].append((src, _half(dst, c, rows), (x, y, 1 - c)))
            p["recvs"].append(_half(dst, 1 - c, rows))
        return p

    return _exchange("grad_sibling_join", halves, [_sds((2 * h.shape[0], h.shape[1]), h.dtype) for h in halves], plan,
                     n_remote=len(halves), n_local=len(halves))


def all_gather_devices(v):
    flips = [(fx, fy, fc) for fx in (0, 1) for fy in (0, 1) for fc in (0, 1)][1:]

    def plan(pos, ins, outs):
        x, y, c = pos
        me = 4 * x + 2 * y + c
        p = dict(local=[(ins[0], outs[0].at[me])], sends=[], recvs=[])
        for fx, fy, fc in flips:
            px, py, pc = x ^ fx, y ^ fy, c ^ fc
            p["sends"].append((ins[0], outs[0].at[me], (px, py, pc)))
            p["recvs"].append(outs[0].at[4 * px + 2 * py + pc])
        return p

    return _exchange("small_all_gather", [v], [_sds((8, *v.shape), v.dtype)], plan, n_remote=7, n_local=1)[0]


def sum_slots(name, p, out_dtype):
    s, r, c = p.shape
    tr = _tile(r, 256, 16)

    def body(p_ref, o_ref):
        acc = p_ref[0].astype(F32)
        for i in range(1, s):
            acc = acc + p_ref[i].astype(F32)
        o_ref[...] = acc.astype(o_ref.dtype)

    return pl.pallas_call(
        body, name=name, grid=(r // tr,), in_specs=[pl.BlockSpec((s, tr, c), lambda i: (0, i, 0))],
        out_specs=pl.BlockSpec((tr, c), lambda i: (i, 0)), out_shape=_sds((r, c), out_dtype),
        compiler_params=_params("arbitrary"),
    )(p)


def adamw(name, w, g, m, v):
    r, c = w.shape
    tr = _tile(r, 256, 8)

    def fn(_, w, g, m, v):
        m = ADAM_B1 * m + (1.0 - ADAM_B1) * g
        v = ADAM_B2 * v + (1.0 - ADAM_B2) * (g * g)
        m_hat = m / (1.0 - ADAM_B1 ** ADAM_STEP)
        v_hat = v / (1.0 - ADAM_B2 ** ADAM_STEP)
        delta = -ADAM_LR * (m_hat / (jnp.sqrt(v_hat) + ADAM_EPS) + ADAM_WD * w)
        return [delta, m, v], []

    outs, _ = _rowwise(name, fn, [w, g, m, v], [], [(c, F32)] * 3, [], tr)
    return outs


RET_SCALE = HEAD_DIM ** -0.5
MLA_SCALE = (HEAD_DIM + MLA_ROPE) ** -0.5
GRAD_DT = BF16
IN_RET = 4 * HEADS * HEAD_DIM
IN_MLA = MLA_Q_RANK + MLA_KV_RANK + MLA_ROPE
IN_MLA_PAD = IN_MLA + 64


def _heads(fn):
    return jnp.concatenate([fn(h) for h in range(HEADS)], axis=1)


def _head(a, h, stride=HEAD_DIM, off=0):
    return a[:, h * stride + off:h * stride + off + HEAD_DIM]


def _group_norm(o):
    rs = [lax.rsqrt(jnp.mean(_head(o, h) * _head(o, h), axis=-1, keepdims=True) + EPS) for h in range(HEADS)]
    return _heads(lambda h: _head(o, h) * rs[h]), rs


def _ffn_fwd(tag, n, wg, wu, wd, tm, tmk):
    g, u, a = mm_nn(tag + "_up", n, wg, tm, outs=[BF16] * 3, w2=wu, epilogue=lambda g, u: (g, u, _silu(g) * u))
    ff = a.shape[1]
    f = mm_nn_k(tag + "_down", a, wd.reshape(ff, wd.shape[2]), tmk, _tile(ff, 512))
    return g, u, a, f


def _ffn_bwd(tag, df, n, g, u, a, wg, wu, wd, tm, tmk, tk, dt):
    ns = wg.shape[2]

    def gate_grads(da, g, u):
        g, u = g.astype(F32), u.astype(F32)
        return da * u * _dsilu(g), da * _silu(g)

    dg, du = mm_nt(tag + "_da", [df], [wd], tm, ns, outs=(BF16, BF16), epilogue=gate_grads, extras=(g, u))
    dwd = mm_tn(tag + "_dwd", a, df, ns, dt, tk, GRAD_DT, shard_rows=True)
    dn = mm_nt_k(tag + "_dn", [dg, du], [wg, wu], tmk, dt)
    dwg = mm_tn(tag + "_dwg", n, dg, dt, ns, tk, GRAD_DT, shard_cols=True)
    dwu = mm_tn(tag + "_dwu", n, du, dt, ns, tk, GRAD_DT, shard_cols=True)
    return dn, dwg, dwu, dwd


def _local_step(x, tgt, meta, w, nw):
    seq, d = x.shape
    t_real = N_META + seq
    tp = -(-t_real // LANES) * LANES
    zpad = jnp.zeros((tp - t_real, d), F32)
    h0 = jnp.concatenate([meta, x, zpad], axis=0)
    tgt_p = jnp.concatenate([jnp.zeros((N_META, d), F32), tgt, zpad], axis=0)
    cos_r, sin_r, cos_m, sa, sb = _rope_tables(tp)
    tm = _tile(tp, 512)
    tmk = _tile(tp, 1408)
    tk = tmk
    blk = tm
    dt = _tile(d, 1024)
    hw = HEADS * HEAD_DIM
    qw = HEADS * MLA_QK_PAD

    (n1,), _ = _rowwise("ffn1_norm", lambda r0, h, g: ([_rms(h, g)], []), [h0], [nw["ffn1_pre_norm"]],
                        [(d, BF16)], [], tm)
    g1, u1, a1, f1 = _ffn_fwd("ffn1", n1, w["wg1"], w["wu1"], w["wd1"], tm, tmk)

    def post_ffn1(r0, h, f, post, pre):
        h1 = h + 0.5 * _rms(f, post)
        return [h1, _rms(h1, pre)], []

    (h1, un), _ = _rowwise("mix_norm", post_ffn1, [h0, f1], [nw["ffn1_post_norm"], nw["mix_pre_norm"]],
                           [(d, F32), (d, BF16)], [], tm)
    (proj_r,) = mm_nn("proj_r", un, w["w_r"], tm, n_block=_tile(IN_RET, 1024))
    (proj_c,) = mm_nn("proj_c", un, w["w_c"], tm)

    def split_proj(r0, pr, pc, cr, sr, cm, ta, tb, qn, kvn):
        rq = _heads(lambda h: _rope_r(_head(pr, h), cr, sr))
        rk = _heads(lambda h: _rope_r(_head(pr, h, off=hw), cr, sr) * RET_SCALE)
        rv = pr[:, 2 * hw:3 * hw]
        cqn = _rms(pc[:, :MLA_Q_RANK], qn)
        ckvn = _rms(pc[:, MLA_Q_RANK:MLA_Q_RANK + MLA_KV_RANK], kvn)
        krr = _rope_m(pc[:, MLA_Q_RANK + MLA_KV_RANK:], cm, ta, tb)
        return [rq, rk, rv, cqn, ckvn, krr], []

    (rq, rk, rv, cqn, ckvn, krr), _ = _rowwise(
        "split_proj", split_proj, [proj_r, proj_c, cos_r, sin_r, cos_m, sa, sb],
        [nw["mla_q_norm"], nw["mla_kv_norm"]],
        [(hw, BF16), (hw, BF16), (hw, BF16), (MLA_Q_RANK, BF16), (MLA_KV_RANK, BF16), (LANES, F32)], [], tm)
    (qp,) = mm_nn("q_up", cqn, w["wuq"], tm)
    (kn,) = mm_nn("k_up", ckvn, w["wuk"], tm)
    (vv,) = mm_nn("v_up", ckvn, w["wuv"], tm, out_dtype=BF16)

    def build_qk(r0, qp, kn, krr, cm, ta, tb):
        qc = jnp.concatenate(
            [part for h in range(HEADS)
             for part in (_head(qp, h, MLA_QK_PAD), _rope_m(_head(qp, h, MLA_QK_PAD, HEAD_DIM), cm, ta, tb))], axis=1)
        kc = jnp.concatenate([part for h in range(HEADS) for part in (_head(kn, h), krr)], axis=1)
        return [qc, kc], []

    (qc, kc), _ = _rowwise("build_qk", build_qk, [qp, kn, krr, cos_m, sa, sb], [], [(qw, BF16), (qw, BF16)], [], tm)
    o_m, lse = attn_fwd("mla_fwd", qc, kc, vv, blk, True, MLA_SCALE)
    (o_r,) = attn_fwd("ret_fwd", rq, rk, rv, blk, False)

    def gate_mix(r0, rg, o_r, o_m, gn):
        y, _ = _group_norm(o_r)
        return [jnp.concatenate([_silu(rg) * (y * gn), o_m], axis=1)], []

    (mixcat,), _ = _rowwise("gate_mix", gate_mix, [(proj_r, hw, 3), o_r, o_m], [nw["ret_group_norm"]],
                            [(2 * hw, BF16)], [], tm)
    (mix,) = mm_nn("mix_out", mixcat, w["w_out"], tm, n_block=dt)

    def post_mix(r0, h, m, post, pre):
        h2 = h + _rms(m, post)
        return [h2, _rms(h2, pre)], []

    (h2, n3), _ = _rowwise("ffn2_norm", post_mix, [h1, mix], [nw["mix_post_norm"], nw["ffn2_pre_norm"]],
                           [(d, F32), (d, BF16)], [], tm)
    g2, u2, a2, f2 = _ffn_fwd("ffn2", n3, w["wg2"], w["wu2"], w["wd2"], tm, tmk)

    def loss_head(r0, h, f, t, post):
        h3 = h + 0.5 * _rms(f, post)
        row = r0 + lax.broadcasted_iota(jnp.int32, h3.shape, 0)
        err = jnp.where(row >= N_META, jnp.where(row < t_real, h3 - t, 0.0), 0.0)
        dh3 = err / d
        df, dpost = _rms_bwd(f, post, 0.5 * dh3)
        return [dh3, df], [_colsum(err * err), _colsum(dpost)]

    (dh3, df2), (loss_vec, d_post2) = _rowwise("loss_head", loss_head, [h2, f2, tgt_p], [nw["ffn2_post_norm"]],
                                               [(d, F32), (d, BF16)], [d, d], tm)
    loss = 0.5 * jnp.sum(loss_vec) / d
    dn3, dwg2, dwu2, dwd2 = _ffn_bwd("ffn2", df2, n3, g2, u2, a2, w["wg2"], w["wu2"], w["wd2"], tm, tmk, tk, dt)

    def back_mix_norm(r0, h, m, dh3, dn, pre, post):
        dx, dpre = _rms_bwd(h, pre, dn)
        dh2 = dh3 + dx
        dm, dpost = _rms_bwd(m, post, dh2)
        return [dh2, dm], [_colsum(dpre), _colsum(dpost)]

    (dh2, dmix), (d_pre2, d_mix_post) = _rowwise(
        "back_mix_norm", back_mix_norm, [h2, mix, dh3, dn3], [nw["ffn2_pre_norm"], nw["mix_post_norm"]],
        [(d, F32), (d, BF16)], [d, d], tm)
    (dmixcat,) = mm_nt("mix_dx", [dmix], [w["w_out"]], tm, _tile(2 * hw, 512))
    dw_out = mm_tn("mix_dw", mixcat, dmix, 2 * hw // N_CHIPS, dt, tk, GRAD_DT, shard_rows=True)

    def back_gate(r0, dmc, rg, o_r, gn):
        d_ret, d_om = dmc[:, :hw], dmc[:, hw:]
        yh, rs = _group_norm(o_r)
        d_rg = d_ret * (yh * gn) * _dsilu(rg)
        dy = d_ret * _silu(rg)
        gyh = dy * gn
        d_or = _heads(lambda h: rs[h] * (_head(gyh, h) - _head(yh, h) * jnp.mean(_head(gyh, h) * _head(yh, h),
                                                                                    axis=-1, keepdims=True)))
        return [d_or, d_rg, d_om], [_colsum(dy * yh)]

    (d_or, d_rg, d_om), (d_gn,) = _rowwise("back_gate", back_gate, [dmixcat, (proj_r, hw, 3), o_r],
                                           [nw["ret_group_norm"]], [(hw, BF16), (hw, F32), (hw, BF16)], [hw], tm)
    dqc, dkc, dvv = attn_bwd("mla_bwd", qc, kc, vv, d_om, blk, True, MLA_SCALE, o_m, lse)
    drq, drk, drv = attn_bwd("ret_bwd", rq, rk, rv, d_or, blk, False)

    def back_qk(r0, dqc, dkc, dvv, cm, ta, tb):
        dqp = jnp.concatenate(
            [part for h in range(HEADS)
             for part in (_head(dqc, h, MLA_QK_PAD), _rope_m_t(_head(dqc, h, MLA_QK_PAD, HEAD_DIM), cm, ta, tb))],
            axis=1)
        dkn = _heads(lambda h: _head(dkc, h, MLA_QK_PAD))
        dkr = _head(dkc, 0, MLA_QK_PAD, HEAD_DIM)
        for h in range(1, HEADS):
            dkr = dkr + _head(dkc, h, MLA_QK_PAD, HEAD_DIM)
        return [dqp, dkn, _rope_m_t(dkr, cm, ta, tb), dvv], []

    (dqp, dkn, dkr, dvb), _ = _rowwise("back_qk", back_qk, [dqc, dkc, dvv, cos_m, sa, sb], [],
                                       [(qw, BF16), (hw, BF16), (LANES, F32), (hw, BF16)], [], tm)
    (dcqn,) = mm_nt("q_dx", [dqp], [w["wuq"]], tm, MLA_Q_RANK)
    dwuq = mm_tn("q_dw", cqn, dqp, MLA_Q_RANK, _tile(qw, 1024), tk, GRAD_DT)
    (dckvn,) = mm_nt("kv_dx", [dkn, dvb], [w["wuk"], w["wuv"]], tm, MLA_KV_RANK)
    dwuk = mm_tn("k_dw", ckvn, dkn, MLA_KV_RANK, hw, tk, GRAD_DT)
    dwuv = mm_tn("v_dw", ckvn, dvb, MLA_KV_RANK, hw, tk, GRAD_DT)

    def back_proj(r0, drq, drk, drv, d_rg, pc, dcqn, dckvn, dkr, cr, sr, qn, kvn):
        d_q = _heads(lambda h: _rope_r_t(_head(drq, h), cr, sr))
        d_k = _heads(lambda h: _rope_r_t(_head(drk, h), cr, sr) * RET_SCALE)
        dcq, a_q = _rms_bwd(pc[:, :MLA_Q_RANK], qn, dcqn)
        dckv, a_kv = _rms_bwd(pc[:, MLA_Q_RANK:MLA_Q_RANK + MLA_KV_RANK], kvn, dckvn)
        return ([jnp.concatenate([d_q, d_k, drv, d_rg], axis=1), jnp.concatenate([dcq, dckv, dkr], axis=1)],
                [_colsum(a_q), _colsum(a_kv)])

    (dproj_r, dproj_c), (d_qn, d_kvn) = _rowwise(
        "back_proj", back_proj, [drq, drk, drv, d_rg, proj_c, dcqn, dckvn, dkr, cos_r, sin_r],
        [nw["mla_q_norm"], nw["mla_kv_norm"]], [(IN_RET, BF16), (IN_MLA_PAD, BF16)],
        [MLA_Q_RANK, MLA_KV_RANK], tm)
    (dun,) = mm_nt("proj_dx", [dproj_r, dproj_c], [w["w_r"], w["w_c"]], tm, _tile(d, 512))
    dw_r = mm_tn("proj_dw_r", un, dproj_r, dt, _tile(IN_RET, 1024), tk, GRAD_DT)
    dw_c = mm_tn("proj_dw_c", un, dproj_c, dt, IN_MLA_PAD, tk, GRAD_DT)

    def back_ffn1_norm(r0, h, f, dh2, dn, pre, post):
        dx, dpre = _rms_bwd(h, pre, dn)
        dh1 = dh2 + dx
        df, dpost = _rms_bwd(f, post, 0.5 * dh1)
        return [dh1, df], [_colsum(dpre), _colsum(dpost)]

    (dh1, df1), (d_mix_pre, d_post1) = _rowwise(
        "back_ffn1_norm", back_ffn1_norm, [h1, f1, dh2, dun], [nw["mix_pre_norm"], nw["ffn1_post_norm"]],
        [(d, F32), (d, BF16)], [d, d], tm)
    dn1, dwg1, dwu1, dwd1 = _ffn_bwd("ffn1", df1, n1, g1, u1, a1, w["wg1"], w["wu1"], w["wd1"], tm, tmk, tk, dt)

    def back_input(r0, h, dh1, dn, pre):
        dx, dpre = _rms_bwd(h, pre, dn)
        return [dh1 + dx], [_colsum(dpre)]

    (dh0,), (d_pre1,) = _rowwise("back_input", back_input, [h0, dh1, dn1], [nw["ffn1_pre_norm"]], [(d, F32)], [d], tm)

    big = dict(wg1=dwg1, wu1=dwu1, wd1=dwd1, wg2=dwg2, wu2=dwu2, wd2=dwd2, w_r=dw_r, w_c=dw_c, wuq=dwuq,
               wuk=dwuk, wuv=dwuv, w_out=dw_out)
    small = dict(ffn1_pre_norm=d_pre1, ffn1_post_norm=d_post1, mix_pre_norm=d_mix_pre, ret_group_norm=d_gn,
                 mla_q_norm=d_qn, mla_kv_norm=d_kvn, mix_post_norm=d_mix_post, ffn2_pre_norm=d_pre2,
                 ffn2_post_norm=d_post2)
    return loss, dh0[N_META:t_real], big, small, dh0[:N_META]


WEIGHTS = ("meta_tokens", "ffn1_pre_norm", "ffn1_w_gate", "ffn1_w_up", "ffn1_w_down", "ffn1_post_norm",
           "mix_pre_norm", "w_in", "ret_group_norm", "mla_q_norm", "mla_w_uq", "mla_kv_norm", "mla_w_uk",
           "mla_w_uv", "w_out", "mix_post_norm", "ffn2_pre_norm", "ffn2_w_gate", "ffn2_w_up", "ffn2_w_down",
           "ffn2_post_norm")
BIG = ("ffn1_w_gate", "ffn1_w_up", "ffn1_w_down", "w_in", "mla_w_uq", "mla_w_uk", "mla_w_uv", "w_out",
       "ffn2_w_gate", "ffn2_w_up", "ffn2_w_down")
NORMS = ("ffn1_pre_norm", "ffn1_post_norm", "mix_pre_norm", "ret_group_norm", "mla_q_norm", "mla_kv_norm",
         "mix_post_norm", "ffn2_pre_norm", "ffn2_post_norm")


def _unshard_cols(g):
    return g.transpose(1, 0, 2).reshape(g.shape[1], -1)


def _shard_cols(a):
    return a.reshape(a.shape[0], N_CHIPS, -1).transpose(1, 0, 2)


def _pack_rows(rows, width):
    rows = [jnp.pad(r, ((0, 0), (0, width - r.shape[1]))) for r in rows]
    n = sum(r.shape[0] for r in rows)
    return jnp.pad(jnp.concatenate(rows, axis=0), ((0, -n % 8), (0, 0)))


def _step(p, m, v, x, loss_target):
    d = x.shape[2]
    local = {n: p[n][0] for n in BIG}
    names = list(BIG)
    gathered = all_gather_chips([local[n].astype(BF16) for n in names] + [p["meta_tokens"]])
    full = dict(zip(names, gathered[:-1]))
    meta = _unshard_cols(gathered[-1])

    w_in = _unshard_cols(full["w_in"])
    wuq = _unshard_cols(full["mla_w_uq"]).reshape(MLA_Q_RANK, HEADS, HEAD_DIM + MLA_ROPE)
    wuq = jnp.pad(wuq, ((0, 0), (0, 0), (0, MLA_QK_PAD - HEAD_DIM - MLA_ROPE))).reshape(MLA_Q_RANK, HEADS * MLA_QK_PAD)
    w = dict(wg1=full["ffn1_w_gate"], wu1=full["ffn1_w_up"], wd1=full["ffn1_w_down"],
             wg2=full["ffn2_w_gate"], wu2=full["ffn2_w_up"], wd2=full["ffn2_w_down"],
             w_r=w_in[:, :IN_RET], w_c=jnp.pad(w_in[:, IN_RET:], ((0, 0), (0, IN_MLA_PAD - IN_MLA))),
             wuq=wuq, wuk=_unshard_cols(full["mla_w_uk"]), wuv=_unshard_cols(full["mla_w_uv"]),
             w_out=full["w_out"].reshape(-1, d))
    nw = {n: p[n] for n in NORMS}
    loss, grad_x, big, small, d_meta = _local_step(x[0], loss_target[0], meta, w, nw)

    dw_in = _shard_cols(jnp.concatenate([big["w_r"], big["w_c"][:, :IN_MLA]], axis=1))
    dwuq = big["wuq"].reshape(MLA_Q_RANK, HEADS, MLA_QK_PAD)[:, :, :HEAD_DIM + MLA_ROPE]
    contrib = dict(ffn1_w_gate=big["wg1"], ffn1_w_up=big["wu1"], ffn1_w_down=big["wd1"], w_in=dw_in,
                   mla_w_uq=_shard_cols(dwuq.reshape(MLA_Q_RANK, -1)), mla_w_uk=_shard_cols(big["wuk"]),
                   mla_w_uv=_shard_cols(big["wuv"]), w_out=big["w_out"],
                   ffn2_w_gate=big["wg2"], ffn2_w_up=big["wu2"], ffn2_w_down=big["wd2"])
    pairs = pair_exchange([contrib[n] for n in names])
    chip_sums = [sum_slots("pair_sum_" + n, q.reshape(2, -1, q.shape[3]), GRAD_DT).reshape(q.shape[1:])
                 for n, q in zip(names, pairs)]
    parts = chip_exchange(chip_sums)
    halves = [sum_slots("chip_sum_" + n, q, F32) for n, q in zip(names, parts)]
    grads = dict(zip(names, sibling_join(halves)))

    width = max(d, HEADS * HEAD_DIM)
    packed = _pack_rows([small[n] for n in NORMS] + [d_meta], width)
    total = sum_slots("small_sum", all_gather_devices(packed), F32)
    for i, n in enumerate(NORMS):
        grads[n] = total[i:i + 1, :p[n].shape[1]]
    cols = p["meta_tokens"].shape[1]
    chip = 2 * lax.axis_index("x") + lax.axis_index("y")
    grads["meta_tokens"] = lax.dynamic_slice(total[len(NORMS):len(NORMS) + N_META, :d], (0, chip * cols), (N_META, cols))

    delta, new_m, new_v = {}, {}, {}
    for n in names + ["meta_tokens"]:
        shape = p[n].shape
        flat = lambda a: a.reshape(-1, shape[-1])
        grads[n] = grads[n].reshape(shape)
        out = adamw("adamw_" + n, flat(p[n]), flat(grads[n]), flat(m[n]), flat(v[n]))
        delta[n], new_m[n], new_v[n] = (o.reshape(shape) for o in out)
    pk = lambda src: _pack_rows([src[n] for n in NORMS], width)
    out = adamw("adamw_norms", pk(p), pk(grads), pk(m), pk(v))
    for i, n in enumerate(NORMS):
        delta[n], new_m[n], new_v[n] = (o[i:i + 1, :p[n].shape[1]] for o in out)

    loss = lax.psum(loss, ("x", "y", "c"))
    return loss, grad_x[None], grads, delta, new_m, new_v


def kernel(x, meta_tokens, ffn1_pre_norm, ffn1_w_gate, ffn1_w_up, ffn1_w_down, ffn1_post_norm, mix_pre_norm, w_in, ret_group_norm, mla_q_norm, mla_w_uq, mla_kv_norm, mla_w_uk, mla_w_uv, w_out, mix_post_norm, ffn2_pre_norm, ffn2_w_gate, ffn2_w_up, ffn2_w_down, ffn2_post_norm, loss_target, m_meta_tokens, m_ffn1_pre_norm, m_ffn1_w_gate, m_ffn1_w_up, m_ffn1_w_down, m_ffn1_post_norm, m_mix_pre_norm, m_w_in, m_ret_group_norm, m_mla_q_norm, m_mla_w_uq, m_mla_kv_norm, m_mla_w_uk, m_mla_w_uv, m_w_out, m_mix_post_norm, m_ffn2_pre_norm, m_ffn2_w_gate, m_ffn2_w_up, m_ffn2_w_down, m_ffn2_post_norm, v_meta_tokens, v_ffn1_pre_norm, v_ffn1_w_gate, v_ffn1_w_up, v_ffn1_w_down, v_ffn1_post_norm, v_mix_pre_norm, v_w_in, v_ret_group_norm, v_mla_q_norm, v_mla_w_uq, v_mla_kv_norm, v_mla_w_uk, v_mla_w_uv, v_w_out, v_mix_post_norm, v_ffn2_pre_norm, v_ffn2_w_gate, v_ffn2_w_up, v_ffn2_w_down, v_ffn2_post_norm):
    args = locals()
    p = {n: args[n] for n in WEIGHTS}
    m = {n: args["m_" + n] for n in WEIGHTS}
    v = {n: args["v_" + n] for n in WEIGHTS}
    loss, grad_x, grads, delta, new_m, new_v = _step(p, m, v, x, loss_target)
    return (loss, grad_x, *[grads[n] for n in WEIGHTS], *[delta[n] for n in WEIGHTS],
            *[new_m[n] for n in WEIGHTS], *[new_v[n] for n in WEIGHTS])
```

```python
import functools
import math

import jax
import jax.numpy as jnp
from jax import lax
from jax.experimental import pallas as pl
from jax.experimental.pallas import tpu as pltpu

F32 = jnp.float32
BF16 = jnp.bfloat16

EPS = 1e-6
N_META = 16
HEADS = 8
HEAD_DIM = 128
MLA_ROPE = 64
MLA_QK_PAD = 256
MLA_Q_RANK = 512
MLA_KV_RANK = 256
ROPE_THETA = 10000.0
N_CHIPS = 4
LANES = 128
VMEM_LIMIT = 60 * 2 ** 20

ADAM_LR = 0.001
ADAM_B1 = 0.9
ADAM_B2 = 0.999
ADAM_EPS = 1e-08
ADAM_WD = 0.01
ADAM_STEP = 10

NN = (((1,), (0,)), ((), ()))
NT = (((1,), (1,)), ((), ()))
TN = (((0,), (0,)), ((), ()))
MESH = pl.DeviceIdType.MESH


def _tile(n, pref, align=LANES):
    if n <= pref:
        return n
    best = 0
    for t in range(align, pref + 1, align):
        if n % t == 0:
            best = t
    assert best, (n, pref)
    return best


def _params(*sem):
    return pltpu.CompilerParams(dimension_semantics=sem, vmem_limit_bytes=VMEM_LIMIT)


def _sds(shape, dtype):
    return jax.ShapeDtypeStruct(tuple(shape), dtype)


def _rowwise(name, fn, rows, consts, outs, accs, tr):
    rows = [r if isinstance(r, tuple) else (r, r.shape[1], 0) for r in rows]
    t = rows[0][0].shape[0]
    assert t % tr == 0
    n_r, n_c, n_o = len(rows), len(consts), len(outs)

    def body(*refs):
        i = pl.program_id(0)
        r = [x[...] for x in refs[:n_r]]
        c = [x[...] for x in refs[n_r:n_r + n_c]]
        o_refs = refs[n_r + n_c:n_r + n_c + n_o]
        a_refs = refs[n_r + n_c + n_o:]
        o_vals, a_vals = fn(i * tr, *r, *c)
        for ref, v in zip(o_refs, o_vals):
            ref[...] = v.astype(ref.dtype)
        if a_refs:
            @pl.when(i == 0)
            def _():
                for ref, v in zip(a_refs, a_vals):
                    ref[...] = v

            @pl.when(i > 0)
            def _():
                for ref, v in zip(a_refs, a_vals):
                    ref[...] += v

    in_specs = [pl.BlockSpec((tr, w), functools.partial(lambda cb, i: (i, cb), cb)) for _, w, cb in rows]
    in_specs += [pl.BlockSpec(a.shape, lambda i: (0, 0)) for a in consts]
    out_specs = [pl.BlockSpec((tr, w), lambda i: (i, 0)) for w, _ in outs]
    out_specs += [pl.BlockSpec((1, w), lambda i: (0, 0)) for w in accs]
    out_shape = [_sds((t, w), dt) for w, dt in outs] + [_sds((1, w), F32) for w in accs]
    res = pl.pallas_call(
        body, name=name, grid=(t // tr,), in_specs=in_specs, out_specs=out_specs, out_shape=out_shape,
        compiler_params=_params("arbitrary"),
    )(*[a for a, _, _ in rows], *consts)
    return res[:n_o], res[n_o:]


def _rms(x, w):
    r = lax.rsqrt(jnp.mean(x * x, axis=-1, keepdims=True) + EPS)
    return x * r * w


def _rms_bwd(x, w, dy):
    r = lax.rsqrt(jnp.mean(x * x, axis=-1, keepdims=True) + EPS)
    xh = x * r
    gy = dy * w
    dx = r * (gy - xh * jnp.mean(gy * xh, axis=-1, keepdims=True))
    return dx, dy * xh


def _colsum(v):
    return jnp.sum(v, axis=0, keepdims=True)


def _silu(x):
    return x * jax.nn.sigmoid(x)


def _dsilu(x):
    s = jax.nn.sigmoid(x)
    return s * (1.0 + x * (1.0 - s))


def _rope_r(x, cos, sin):
    return x * cos + pltpu.roll(x, 64, 1) * sin


def _rope_r_t(dy, cos, sin):
    return dy * cos + pltpu.roll(dy * sin, 64, 1)


def _rope_m(x, cos, sa, sb):
    return x * cos + pltpu.roll(x, 32, 1) * sa + pltpu.roll(x, 96, 1) * sb


def _rope_m_t(dy, cos, sa, sb):
    return dy * cos + pltpu.roll(dy * sa, 96, 1) + pltpu.roll(dy * sb, 32, 1)


def _rope_tables(t):
    pos = jnp.arange(t, dtype=F32)

    def cs(dim):
        inv = ROPE_THETA ** (-jnp.arange(0, dim, 2, dtype=F32) / dim)
        ang = pos[:, None] * inv[None, :]
        return jnp.cos(ang), jnp.sin(ang)

    c, s = cs(HEAD_DIM)
    cos_r = jnp.concatenate([c, c], axis=1)
    sin_r = jnp.concatenate([-s, s], axis=1)
    c, s = cs(MLA_ROPE)
    z32, z64 = jnp.zeros_like(s), jnp.zeros((t, 64), F32)
    cos_m = jnp.concatenate([c, c, z64], axis=1)
    sa = jnp.concatenate([z32, s, z64], axis=1)
    sb = jnp.concatenate([-s, z32, z64], axis=1)
    return cos_r, sin_r, cos_m, sa, sb


def _mm(name, grid, operands, in_specs, dns, out_specs, out_shape, epilogue=None, extras=(), extra_specs=(),
        acc_shape=None):
    n_p, n_e = len(dns), len(extras)
    nk = grid[2]
    n_o = len(out_shape)

    def body(*refs):
        ab = refs[:2 * n_p]
        ex = refs[2 * n_p:2 * n_p + n_e]
        outs = refs[2 * n_p + n_e:2 * n_p + n_e + n_o]
        part = None
        for p in range(n_p):
            d = lax.dot_general(ab[2 * p][...], ab[2 * p + 1][...], dns[p], preferred_element_type=F32)
            part = d if part is None else part + d

        def finish(acc):
            vals = (acc,) if epilogue is None else epilogue(acc, *[e[...] for e in ex])
            for o, v in zip(outs, vals):
                o[...] = v.astype(o.dtype)

        if nk == 1:
            finish(part)
        else:
            acc_ref = refs[-1]
            k = pl.program_id(2)

            @pl.when(k == 0)
            def _():
                acc_ref[...] = part

            @pl.when(k > 0)
            def _():
                acc_ref[...] += part

            @pl.when(k == nk - 1)
            def _():
                finish(acc_ref[...])

    scratch = [] if nk == 1 else [pltpu.VMEM(acc_shape, F32)]
    return pl.pallas_call(
        body, name=name, grid=grid, in_specs=list(in_specs) + list(extra_specs), out_specs=out_specs,
        out_shape=out_shape, scratch_shapes=scratch,
        compiler_params=_params("arbitrary", "arbitrary", "arbitrary"),
    )(*operands, *extras)


def mm_nn(name, x, w, tm, out_dtype=F32, epilogue=None, outs=None, w2=None, n_block=None):
    t, kdim = x.shape
    if w.ndim == 3:
        s, _, ns = w.shape
        n, tn, nb = s * ns, ns, s
        wspec = pl.BlockSpec((None, kdim, ns), lambda j, i, k: (j, 0, 0))
    else:
        n = w.shape[1]
        tn = n_block or n
        nb = n // tn
        wspec = pl.BlockSpec((kdim, tn), lambda j, i, k: (0, j))
    xspec = pl.BlockSpec((tm, kdim), lambda j, i, k: (i, 0))
    ospec = pl.BlockSpec((tm, tn), lambda j, i, k: (i, j))
    outs = outs or [out_dtype]
    grid = (nb, t // tm, 1)
    if w2 is None:
        return _mm(name, grid, [x, w], [xspec, wspec], [NN], [ospec] * len(outs), [_sds((t, n), d) for d in outs],
                   epilogue=epilogue)

    def body(x_ref, w_ref, w2_ref, *o_refs):
        xv = x_ref[...]
        a = jnp.dot(xv, w_ref[...], preferred_element_type=F32)
        b = jnp.dot(xv, w2_ref[...], preferred_element_type=F32)
        for o, v in zip(o_refs, epilogue(a, b)):
            o[...] = v.astype(o.dtype)

    return pl.pallas_call(
        body, name=name, grid=grid[:2],
        in_specs=[pl.BlockSpec((tm, kdim), lambda j, i: (i, 0)),
                  pl.BlockSpec((None, kdim, tn), lambda j, i: (j, 0, 0)),
                  pl.BlockSpec((None, kdim, tn), lambda j, i: (j, 0, 0))],
        out_specs=[pl.BlockSpec((tm, tn), lambda j, i: (i, j))] * len(outs),
        out_shape=[_sds((t, n), d) for d in outs],
        compiler_params=_params("arbitrary", "arbitrary"),
    )(x, w, w2)


def mm_nn_k(name, x, w, tm, tk, out_dtype=F32):
    t, kdim = x.shape
    n = w.shape[1]
    grid = (t // tm, 1, kdim // tk)
    return _mm(name, grid, [x, w],
               [pl.BlockSpec((tm, tk), lambda i, j, k: (i, k)), pl.BlockSpec((tk, n), lambda i, j, k: (k, 0))],
               [NN], [pl.BlockSpec((tm, n), lambda i, j, k: (i, 0))], [_sds((t, n), out_dtype)],
               acc_shape=(tm, n))[0]


def mm_nt(name, xs, ws, tm, tn, outs=(F32,), epilogue=None, extras=(), extra_blocked=True):
    t = xs[0].shape[0]
    specs, ops = [], []
    for x, w in zip(xs, ws):
        kdim = x.shape[1]
        specs.append(pl.BlockSpec((tm, kdim), lambda j, i, k: (i, 0)))
        if w.ndim == 3:
            assert tn == w.shape[1]
            n = w.shape[0] * w.shape[1]
            specs.append(pl.BlockSpec((None, tn, kdim), lambda j, i, k: (j, 0, 0)))
        else:
            n = w.shape[0]
            specs.append(pl.BlockSpec((tn, kdim), lambda j, i, k: (j, 0)))
        ops += [x, w]
    ospec = pl.BlockSpec((tm, tn), lambda j, i, k: (i, j))
    return _mm(name, (n // tn, t // tm, 1), ops, specs, [NT] * len(xs), [ospec] * len(outs),
               [_sds((t, n), d) for d in outs], epilogue=epilogue, extras=extras,
               extra_specs=[ospec] * len(extras))


def mm_nt_k(name, xs, ws, tm, tn, out_dtype=F32):
    t = xs[0].shape[0]
    s, n, ns = ws[0].shape
    specs, ops = [], []
    for x, w in zip(xs, ws):
        specs.append(pl.BlockSpec((tm, ns), lambda i, j, k: (i, k)))
        specs.append(pl.BlockSpec((None, tn, ns), lambda i, j, k: (k, j, 0)))
        ops += [x, w]
    return _mm(name, (t // tm, n // tn, s), ops, specs, [NT] * len(xs),
               [pl.BlockSpec((tm, tn), lambda i, j, k: (i, j))], [_sds((t, n), out_dtype)], acc_shape=(tm, tn))[0]


def mm_tn(name, x, y, tm, tn, tk, out_dtype, shard_rows=False, shard_cols=False):
    t, m = x.shape
    n = y.shape[1]
    grid = (m // tm, n // tn, t // tk)
    if shard_cols:
        ospec = pl.BlockSpec((None, tm, tn), lambda i, j, k: (j, i, 0))
        oshape = _sds((n // tn, m, tn), out_dtype)
    elif shard_rows:
        ospec = pl.BlockSpec((None, tm, tn), lambda i, j, k: (i, 0, j))
        oshape = _sds((m // tm, tm, n), out_dtype)
    else:
        ospec = pl.BlockSpec((tm, tn), lambda i, j, k: (i, j))
        oshape = _sds((m, n), out_dtype)
    return _mm(name, grid, [x, y],
               [pl.BlockSpec((tk, tm), lambda i, j, k: (k, i)), pl.BlockSpec((tk, tn), lambda i, j, k: (k, j))],
               [TN], [ospec], [oshape], acc_shape=(tm, tn))[0]


def _decay_logs():
    return [math.log(1.0 - 2.0 ** (-5.0 - h)) for h in range(HEADS)]


def _log_decay(h):
    lg = jnp.float32(_decay_logs()[0])
    for i in range(1, HEADS):
        lg = jnp.where(h == i, jnp.float32(_decay_logs()[i]), lg)
    return lg


def _scores(q, k, scale, qi, ki, blk, softmax, lg, diag):
    s = lax.dot_general(q, k, NT, preferred_element_type=F32)
    row = lax.broadcasted_iota(jnp.int32, s.shape, 0)
    col = lax.broadcasted_iota(jnp.int32, s.shape, 1)
    if softmax:
        s = s * scale
        if diag:
            s = jnp.where(col <= row, s, -1e30)
        return s
    dist = ((qi - ki) * blk + row - col).astype(F32)
    dec = jnp.exp(jnp.maximum(dist, 0.0) * lg)
    if diag:
        dec = jnp.where(col <= row, dec, 0.0)
    return s * dec, dec


def attn_fwd(name, q, k, v, blk, softmax, scale=1.0):
    t = q.shape[0]
    dq = q.shape[1] // HEADS
    nq = t // blk

    def body(q_ref, k_ref, v_ref, o_ref, *lse_ref):
        h, qi = pl.program_id(0), pl.program_id(1)
        lg = _log_decay(h)
        qv = q_ref[...]

        def block(ki, carry, diag):
            start = pl.multiple_of(ki * blk, blk)
            kv = k_ref[pl.ds(start, blk), :]
            vv = v_ref[pl.ds(start, blk), :]
            if softmax:
                m, l, acc = carry
                s = _scores(qv, kv, scale, qi, ki, blk, True, lg, diag)
                m_new = jnp.maximum(m, jnp.max(s, axis=-1, keepdims=True))
                p = jnp.exp(s - m_new)
                alpha = jnp.exp(m - m_new)
                l = alpha * l + jnp.sum(p, axis=-1, keepdims=True)
                acc = alpha * acc + jnp.dot(p.astype(BF16), vv, preferred_element_type=F32)
                return m_new, l, acc
            p, _ = _scores(qv, kv, scale, qi, ki, blk, False, lg, diag)
            return (carry[0] + jnp.dot(p.astype(BF16), vv, preferred_element_type=F32),)

        if softmax:
            init = (jnp.full((blk, 1), -1e30, F32), jnp.zeros((blk, 1), F32), jnp.zeros((blk, HEAD_DIM), F32))
        else:
            init = (jnp.zeros((blk, HEAD_DIM), F32),)
        carry = lax.fori_loop(0, qi, lambda ki, c: block(ki, c, False), init)
        carry = block(qi, carry, True)
        if softmax:
            m, l, acc = carry
            o_ref[...] = acc / l
            lse_ref[0][...] = jnp.broadcast_to(m + jnp.log(l), (blk, HEAD_DIM))
        else:
            o_ref[...] = carry[0]

    hspec = pl.BlockSpec((blk, HEAD_DIM), lambda h, i: (i, h))
    out_specs = [hspec, hspec] if softmax else [hspec]
    out_shape = [_sds((t, HEADS * HEAD_DIM), F32)] * (2 if softmax else 1)
    return pl.pallas_call(
        body, name=name, grid=(HEADS, nq),
        in_specs=[pl.BlockSpec((blk, dq), lambda h, i: (i, h)),
                  pl.BlockSpec((t, dq), lambda h, i: (0, h)),
                  pl.BlockSpec((t, HEAD_DIM), lambda h, i: (0, h))],
        out_specs=out_specs, out_shape=out_shape, compiler_params=_params("arbitrary", "arbitrary"),
    )(q, k, v)


def attn_bwd(name, q, k, v, do, blk, softmax, scale=1.0, o=None, lse=None):
    t = q.shape[0]
    dq_w = q.shape[1] // HEADS
    nb = t // blk

    def body(q_ref, k_ref, v_ref, do_ref, *rest):
        if softmax:
            o_ref, lse_ref, dq_ref, dk_ref, dv_ref = rest
        else:
            dq_ref, dk_ref, dv_ref = rest
        h, ki = pl.program_id(0), pl.program_id(1)
        lg = _log_decay(h)
        kv = k_ref[...]
        vv = v_ref[...]

        @pl.when(ki == 0)
        def _():
            dq_ref[...] = jnp.zeros_like(dq_ref)

        def block(qi, carry, diag):
            dk, dv = carry
            start = pl.multiple_of(qi * blk, blk)
            rows = pl.ds(start, blk)
            qv = q_ref[rows, :]
            dov = do_ref[rows, :]
            dp = lax.dot_general(dov, vv, NT, preferred_element_type=F32)
            if softmax:
                s = _scores(qv, kv, scale, qi, ki, blk, True, lg, diag)
                p = jnp.exp(s - lse_ref[rows, :][:, :1])
                delta = jnp.sum(dov.astype(F32) * o_ref[rows, :], axis=-1, keepdims=True)
                ds = p * (dp - delta) * scale
            else:
                p, dec = _scores(qv, kv, scale, qi, ki, blk, False, lg, diag)
                ds = dp * dec
            pb, dsb = p.astype(BF16), ds.astype(BF16)
            dv = dv + lax.dot_general(pb, dov, TN, preferred_element_type=F32)
            dk = dk + lax.dot_general(dsb, qv, TN, preferred_element_type=F32)
            dq_ref[rows, :] += jnp.dot(dsb, kv, preferred_element_type=F32)
            return dk, dv

        carry = block(ki, (jnp.zeros((blk, dq_w), F32), jnp.zeros((blk, HEAD_DIM), F32)), True)
        dk, dv = lax.fori_loop(ki + 1, nb, lambda qi, c: block(qi, c, False), carry)
        dk_ref[...] = dk
        dv_ref[...] = dv

    full = lambda w: pl.BlockSpec((t, w), lambda h, j: (0, h))
    blkd = lambda w: pl.BlockSpec((blk, w), lambda h, j: (j, h))
    ins = [q, k, v, do]
    in_specs = [full(dq_w), blkd(dq_w), blkd(HEAD_DIM), full(HEAD_DIM)]
    if softmax:
        ins += [o, lse]
        in_specs += [full(HEAD_DIM), full(HEAD_DIM)]
    return pl.pallas_call(
        body, name=name, grid=(HEADS, nb), in_specs=in_specs,
        out_specs=[full(dq_w), blkd(dq_w), blkd(HEAD_DIM)],
        out_shape=[_sds(q.shape, F32), _sds(k.shape, F32), _sds(v.shape, F32)],
        compiler_params=_params("arbitrary", "arbitrary"),
    )(*ins)


CHIP_FLIPS = ((1, 0), (0, 1), (1, 1))


def _position():
    return lax.axis_index("x"), lax.axis_index("y"), lax.axis_index("c")


def _exchange(name, ins, out_shape, plan, n_remote, n_local):
    n_in, n_out = len(ins), len(out_shape)

    def body(*refs):
        in_refs, out_refs = refs[:n_in], refs[n_in:n_in + n_out]
        send_sems, recv_sems, local_sems = refs[n_in + n_out:]
        pos = _position()
        p = plan(pos, in_refs, out_refs)
        sends, recvs = p["sends"], p["recvs"]
        fwd, recvs2 = p.get("fwd", []), p.get("recvs2", [])
        n1 = len(sends)

        def remote(k, src, dst, dev):
            return pltpu.make_async_remote_copy(src_ref=src, dst_ref=dst, send_sem=send_sems.at[k],
                                                recv_sem=recv_sems.at[k], device_id=dev, device_id_type=MESH)

        local = [pltpu.make_async_copy(s, d, local_sems.at[i]) for i, (s, d) in enumerate(p["local"])]
        for cp in local:
            cp.start()
        out = [remote(k, s, d, dev) for k, (s, d, dev) in enumerate(sends)]
        for cp in out:
            cp.start()
        passed = [remote(n1 + k, s, d, dev) for k, (s, d, dev) in enumerate(fwd)]
        for k, dst in enumerate(recvs):
            remote(k, dst, dst, pos).wait_recv()
            if fwd:
                passed[k].start()
        for k, dst in enumerate(recvs2):
            remote(n1 + k, dst, dst, pos).wait_recv()
        for cp in out + passed:
            cp.wait_send()
        for cp in local:
            cp.wait()

    hbm = pl.BlockSpec(memory_space=pl.ANY)
    return pl.pallas_call(
        body, name=name, in_specs=[hbm] * n_in, out_specs=[hbm] * n_out, out_shape=out_shape,
        scratch_shapes=[pltpu.SemaphoreType.DMA((n_remote,)), pltpu.SemaphoreType.DMA((n_remote,)),
                        pltpu.SemaphoreType.DMA((n_local,))],
    )(*ins)


def _half_rows(c, rows):
    r2 = rows // 2
    return pl.ds(pl.multiple_of(c * r2, math.gcd(r2, LANES)), r2)


def _half(ref, c, rows, lead=()):
    return ref.at[(*lead, _half_rows(c, rows))]


def all_gather_chips(shards):
    def plan(pos, ins, outs):
        x, y, c = pos
        me = 2 * x + y
        p = dict(local=[], sends=[], recvs=[], fwd=[], recvs2=[])
        for a, (src, dst) in enumerate(zip(ins, outs)):
            rows = shards[a].shape[0]
            p["local"].append((src, dst.at[me]))
            for fx, fy in CHIP_FLIPS:
                px, py = x ^ fx, y ^ fy
                peer = 2 * px + py
                p["sends"].append((_half(src, c, rows), _half(dst, c, rows, (me,)), (px, py, c)))
                p["recvs"].append(_half(dst, c, rows, (peer,)))
                p["fwd"].append((_half(dst, c, rows, (peer,)), _half(dst, c, rows, (peer,)), (x, y, 1 - c)))
                p["recvs2"].append(_half(dst, 1 - c, rows, (peer,)))
        return p

    return _exchange("all_gather_chips", shards, [_sds((N_CHIPS, *s.shape), s.dtype) for s in shards], plan,
                     n_remote=6 * len(shards), n_local=len(shards))


def chip_exchange(parts):
    def plan(pos, ins, outs):
        x, y, c = pos
        me = 2 * x + y
        p = dict(local=[], sends=[], recvs=[])
        for src, dst in zip(ins, outs):
            p["local"].append((src.at[me], dst.at[me]))
            for fx, fy in CHIP_FLIPS:
                px, py = x ^ fx, y ^ fy
                peer = 2 * px + py
                p["sends"].append((src.at[peer], dst.at[me], (px, py, c)))
                p["recvs"].append(dst.at[peer])
        return p

    return _exchange("grad_chip_exchange", parts, [_sds(g.shape, g.dtype) for g in parts], plan,
                     n_remote=3 * len(parts), n_local=len(parts))


def all_gather_devices(v):
    flips = [(fx, fy, fc) for fx in (0, 1) for fy in (0, 1) for fc in (0, 1)][1:]

    def plan(pos, ins, outs):
        x, y, c = pos
        me = 4 * x + 2 * y + c
        p = dict(local=[(ins[0], outs[0].at[me])], sends=[], recvs=[])
        for fx, fy, fc in flips:
            px, py, pc = x ^ fx, y ^ fy, c ^ fc
            p["sends"].append((ins[0], outs[0].at[me], (px, py, pc)))
            p["recvs"].append(outs[0].at[4 * px + 2 * py + pc])
        return p

    return _exchange("small_all_gather", [v], [_sds((8, *v.shape), v.dtype)], plan, n_remote=7, n_local=1)[0]


SWAP_CHUNK_BYTES = 3 * 2 ** 19


def _sibling_swap(t, n_steps, value, sbuf, rbuf, send_sems, recv_sems, credits):
    x, y, c = _position()
    sib = (x, y, 1 - c)
    slot = t % 2
    copy = pltpu.make_async_remote_copy(src_ref=sbuf.at[slot], dst_ref=rbuf.at[slot], send_sem=send_sems.at[slot],
                                        recv_sem=recv_sems.at[slot], device_id=sib, device_id_type=MESH)

    @pl.when(t >= 2)
    def _():
        copy.wait_send()
        pl.semaphore_wait(credits.at[slot], 1)

    sbuf[slot] = value
    copy.start()
    copy.wait_recv()
    got = rbuf[slot]

    @pl.when(t + 2 < n_steps)
    def _():
        pl.semaphore_signal(credits.at[slot], inc=1, device_id=sib, device_id_type=MESH)

    @pl.when(t == n_steps - 1)
    def _():
        copy.wait_send()
        if n_steps > 1:
            other = pltpu.make_async_remote_copy(
                src_ref=sbuf.at[1 - slot], dst_ref=rbuf.at[1 - slot], send_sem=send_sems.at[1 - slot],
                recv_sem=recv_sems.at[1 - slot], device_id=sib, device_id_type=MESH)
            other.wait_send()

    return got


def _swap_scratch(rows, cols, dtype):
    return [pltpu.VMEM((2, rows, cols), dtype), pltpu.VMEM((2, rows, cols), dtype),
            pltpu.SemaphoreType.DMA((2,)), pltpu.SemaphoreType.DMA((2,)), pltpu.SemaphoreType.REGULAR((2,))]


def _chunk_rows(rows, cols, dtype):
    return _tile(rows, max(16, SWAP_CHUNK_BYTES // (cols * jnp.dtype(dtype).itemsize)), 16)


def pair_add(name, g):
    s, r, c_ = g.shape
    r2 = r // 2
    cr = _chunk_rows(r2, c_, g.dtype)
    nj = r2 // cr

    def body(core, mine_ref, theirs_ref, o_ref, *scratch):
        t = pl.program_id(0) * nj + pl.program_id(1)
        got = _sibling_swap(t, s * nj, theirs_ref[...], *scratch)
        o_ref[...] = (mine_ref[...].astype(F32) + got.astype(F32)).astype(o_ref.dtype)

    grid_spec = pltpu.PrefetchScalarGridSpec(
        num_scalar_prefetch=1, grid=(s, nj),
        in_specs=[pl.BlockSpec((None, cr, c_), lambda i, j, core: (i, core[0] * nj + j, 0)),
                  pl.BlockSpec((None, cr, c_), lambda i, j, core: (i, (1 - core[0]) * nj + j, 0))],
        out_specs=pl.BlockSpec((None, cr, c_), lambda i, j, core: (i, j, 0)),
        scratch_shapes=_swap_scratch(cr, c_, g.dtype))
    core = lax.axis_index("c").astype(jnp.int32).reshape(1)
    return pl.pallas_call(body, name=name, grid_spec=grid_spec, out_shape=_sds((s, r2, c_), g.dtype),
                          compiler_params=_params("arbitrary", "arbitrary"))(core, g, g)


def reduce_join(name, p):
    s, r2, c_ = p.shape
    cr = _chunk_rows(r2, c_, F32)
    nj = r2 // cr

    def body(core, p_ref, o_ref, *scratch):
        acc = p_ref[0].astype(F32)
        for i in range(1, s):
            acc = acc + p_ref[i].astype(F32)
        got = _sibling_swap(pl.program_id(0), nj, acc, *scratch)
        c = core[0]
        o_ref[c] = acc
        o_ref[1 - c] = got

    grid_spec = pltpu.PrefetchScalarGridSpec(
        num_scalar_prefetch=1, grid=(nj,),
        in_specs=[pl.BlockSpec((s, cr, c_), lambda j, core: (0, j, 0))],
        out_specs=pl.BlockSpec((2, cr, c_), lambda j, core: (0, j, 0)),
        scratch_shapes=_swap_scratch(cr, c_, F32))
    core = lax.axis_index("c").astype(jnp.int32).reshape(1)
    out = pl.pallas_call(body, name=name, grid_spec=grid_spec, out_shape=_sds((2, r2, c_), F32),
                         compiler_params=_params("arbitrary"))(core, p)
    return out.reshape(2 * r2, c_)


def sum_slots(name, p, out_dtype):
    s, r, c = p.shape
    tr = _tile(r, 256, 16)

    def body(p_ref, o_ref):
        acc = p_ref[0].astype(F32)
        for i in range(1, s):
            acc = acc + p_ref[i].astype(F32)
        o_ref[...] = acc.astype(o_ref.dtype)

    return pl.pallas_call(
        body, name=name, grid=(r // tr,), in_specs=[pl.BlockSpec((s, tr, c), lambda i: (0, i, 0))],
        out_specs=pl.BlockSpec((tr, c), lambda i: (i, 0)), out_shape=_sds((r, c), out_dtype),
        compiler_params=_params("arbitrary"),
    )(p)


def adamw(name, w, g, m, v):
    r, c = w.shape
    tr = _tile(r, 256, 8)

    def fn(_, w, g, m, v):
        m = ADAM_B1 * m + (1.0 - ADAM_B1) * g
        v = ADAM_B2 * v + (1.0 - ADAM_B2) * (g * g)
        m_hat = m / (1.0 - ADAM_B1 ** ADAM_STEP)
        v_hat = v / (1.0 - ADAM_B2 ** ADAM_STEP)
        delta = -ADAM_LR * (m_hat / (jnp.sqrt(v_hat) + ADAM_EPS) + ADAM_WD * w)
        return [delta, m, v], []

    outs, _ = _rowwise(name, fn, [w, g, m, v], [], [(c, F32)] * 3, [], tr)
    return outs


RET_SCALE = HEAD_DIM ** -0.5
MLA_SCALE = (HEAD_DIM + MLA_ROPE) ** -0.5
GRAD_DT = BF16
IN_RET = 4 * HEADS * HEAD_DIM
IN_MLA = MLA_Q_RANK + MLA_KV_RANK + MLA_ROPE
IN_MLA_PAD = IN_MLA + 64


def _heads(fn):
    return jnp.concatenate([fn(h) for h in range(HEADS)], axis=1)


def _head(a, h, stride=HEAD_DIM, off=0):
    return a[:, h * stride + off:h * stride + off + HEAD_DIM]


def _group_norm(o):
    rs = [lax.rsqrt(jnp.mean(_head(o, h) * _head(o, h), axis=-1, keepdims=True) + EPS) for h in range(HEADS)]
    return _heads(lambda h: _head(o, h) * rs[h]), rs


def _ffn_fwd(tag, n, wg, wu, wd, tm, tmk):
    g, u, a = mm_nn(tag + "_up", n, wg, tm, outs=[BF16] * 3, w2=wu, epilogue=lambda g, u: (g, u, _silu(g) * u))
    ff = a.shape[1]
    f = mm_nn_k(tag + "_down", a, wd.reshape(ff, wd.shape[2]), tmk, _tile(ff, 512))
    return g, u, a, f


def _ffn_bwd(tag, df, n, g, u, a, wg, wu, wd, tm, tmk, tk, dt):
    ns = wg.shape[2]

    def gate_grads(da, g, u):
        g, u = g.astype(F32), u.astype(F32)
        return da * u * _dsilu(g), da * _silu(g)

    dg, du = mm_nt(tag + "_da", [df], [wd], tm, ns, outs=(BF16, BF16), epilogue=gate_grads, extras=(g, u))
    dwd = mm_tn(tag + "_dwd", a, df, ns, dt, tk, GRAD_DT, shard_rows=True)
    dn = mm_nt_k(tag + "_dn", [dg, du], [wg, wu], tmk, dt)
    dwg = mm_tn(tag + "_dwg", n, dg, dt, ns, tk, GRAD_DT, shard_cols=True)
    dwu = mm_tn(tag + "_dwu", n, du, dt, ns, tk, GRAD_DT, shard_cols=True)
    return dn, dwg, dwu, dwd


def _local_step(x, tgt, meta, w, nw):
    seq, d = x.shape
    t_real = N_META + seq
    tp = -(-t_real // LANES) * LANES
    zpad = jnp.zeros((tp - t_real, d), F32)
    h0 = jnp.concatenate([meta, x, zpad], axis=0)
    tgt_p = jnp.concatenate([jnp.zeros((N_META, d), F32), tgt, zpad], axis=0)
    cos_r, sin_r, cos_m, sa, sb = _rope_tables(tp)
    tm = _tile(tp, 512)
    tmk = _tile(tp, 1408)
    tk = tmk
    blk = tm
    dt = _tile(d, 1024)
    hw = HEADS * HEAD_DIM
    qw = HEADS * MLA_QK_PAD

    (n1,), _ = _rowwise("ffn1_norm", lambda r0, h, g: ([_rms(h, g)], []), [h0], [nw["ffn1_pre_norm"]],
                        [(d, BF16)], [], tm)
    g1, u1, a1, f1 = _ffn_fwd("ffn1", n1, w["wg1"], w["wu1"], w["wd1"], tm, tmk)

    def post_ffn1(r0, h, f, post, pre):
        h1 = h + 0.5 * _rms(f, post)
        return [h1, _rms(h1, pre)], []

    (h1, un), _ = _rowwise("mix_norm", post_ffn1, [h0, f1], [nw["ffn1_post_norm"], nw["mix_pre_norm"]],
                           [(d, F32), (d, BF16)], [], tm)
    (proj_r,) = mm_nn("proj_r", un, w["w_r"], tm, n_block=_tile(IN_RET, 1024))
    (proj_c,) = mm_nn("proj_c", un, w["w_c"], tm)

    def split_proj(r0, pr, pc, cr, sr, cm, ta, tb, qn, kvn):
        rq = _heads(lambda h: _rope_r(_head(pr, h), cr, sr))
        rk = _heads(lambda h: _rope_r(_head(pr, h, off=hw), cr, sr) * RET_SCALE)
        rv = pr[:, 2 * hw:3 * hw]
        cqn = _rms(pc[:, :MLA_Q_RANK], qn)
        ckvn = _rms(pc[:, MLA_Q_RANK:MLA_Q_RANK + MLA_KV_RANK], kvn)
        krr = _rope_m(pc[:, MLA_Q_RANK + MLA_KV_RANK:], cm, ta, tb)
        return [rq, rk, rv, cqn, ckvn, krr], []

    (rq, rk, rv, cqn, ckvn, krr), _ = _rowwise(
        "split_proj", split_proj, [proj_r, proj_c, cos_r, sin_r, cos_m, sa, sb],
        [nw["mla_q_norm"], nw["mla_kv_norm"]],
        [(hw, BF16), (hw, BF16), (hw, BF16), (MLA_Q_RANK, BF16), (MLA_KV_RANK, BF16), (LANES, F32)], [], tm)
    (qp,) = mm_nn("q_up", cqn, w["wuq"], tm)
    (kn,) = mm_nn("k_up", ckvn, w["wuk"], tm)
    (vv,) = mm_nn("v_up", ckvn, w["wuv"], tm, out_dtype=BF16)

    def build_qk(r0, qp, kn, krr, cm, ta, tb):
        qc = jnp.concatenate(
            [part for h in range(HEADS)
             for part in (_head(qp, h, MLA_QK_PAD), _rope_m(_head(qp, h, MLA_QK_PAD, HEAD_DIM), cm, ta, tb))], axis=1)
        kc = jnp.concatenate([part for h in range(HEADS) for part in (_head(kn, h), krr)], axis=1)
        return [qc, kc], []

    (qc, kc), _ = _rowwise("build_qk", build_qk, [qp, kn, krr, cos_m, sa, sb], [], [(qw, BF16), (qw, BF16)], [], tm)
    o_m, lse = attn_fwd("mla_fwd", qc, kc, vv, blk, True, MLA_SCALE)
    (o_r,) = attn_fwd("ret_fwd", rq, rk, rv, blk, False)

    def gate_mix(r0, rg, o_r, o_m, gn):
        y, _ = _group_norm(o_r)
        return [jnp.concatenate([_silu(rg) * (y * gn), o_m], axis=1)], []

    (mixcat,), _ = _rowwise("gate_mix", gate_mix, [(proj_r, hw, 3), o_r, o_m], [nw["ret_group_norm"]],
                            [(2 * hw, BF16)], [], tm)
    (mix,) = mm_nn("mix_out", mixcat, w["w_out"], tm, n_block=dt)

    def post_mix(r0, h, m, post, pre):
        h2 = h + _rms(m, post)
        return [h2, _rms(h2, pre)], []

    (h2, n3), _ = _rowwise("ffn2_norm", post_mix, [h1, mix], [nw["mix_post_norm"], nw["ffn2_pre_norm"]],
                           [(d, F32), (d, BF16)], [], tm)
    g2, u2, a2, f2 = _ffn_fwd("ffn2", n3, w["wg2"], w["wu2"], w["wd2"], tm, tmk)

    def loss_head(r0, h, f, t, post):
        h3 = h + 0.5 * _rms(f, post)
        row = r0 + lax.broadcasted_iota(jnp.int32, h3.shape, 0)
        err = jnp.where(row >= N_META, jnp.where(row < t_real, h3 - t, 0.0), 0.0)
        dh3 = err / d
        df, dpost = _rms_bwd(f, post, 0.5 * dh3)
        return [dh3, df], [_colsum(err * err), _colsum(dpost)]

    (dh3, df2), (loss_vec, d_post2) = _rowwise("loss_head", loss_head, [h2, f2, tgt_p], [nw["ffn2_post_norm"]],
                                               [(d, F32), (d, BF16)], [d, d], tm)
    loss = 0.5 * jnp.sum(loss_vec) / d
    dn3, dwg2, dwu2, dwd2 = _ffn_bwd("ffn2", df2, n3, g2, u2, a2, w["wg2"], w["wu2"], w["wd2"], tm, tmk, tk, dt)

    def back_mix_norm(r0, h, m, dh3, dn, pre, post):
        dx, dpre = _rms_bwd(h, pre, dn)
        dh2 = dh3 + dx
        dm, dpost = _rms_bwd(m, post, dh2)
        return [dh2, dm], [_colsum(dpre), _colsum(dpost)]

    (dh2, dmix), (d_pre2, d_mix_post) = _rowwise(
        "back_mix_norm", back_mix_norm, [h2, mix, dh3, dn3], [nw["ffn2_pre_norm"], nw["mix_post_norm"]],
        [(d, F32), (d, BF16)], [d, d], tm)
    (dmixcat,) = mm_nt("mix_dx", [dmix], [w["w_out"]], tm, _tile(2 * hw, 512))
    dw_out = mm_tn("mix_dw", mixcat, dmix, 2 * hw // N_CHIPS, dt, tk, GRAD_DT, shard_rows=True)

    def back_gate(r0, dmc, rg, o_r, gn):
        d_ret, d_om = dmc[:, :hw], dmc[:, hw:]
        yh, rs = _group_norm(o_r)
        d_rg = d_ret * (yh * gn) * _dsilu(rg)
        dy = d_ret * _silu(rg)
        gyh = dy * gn
        d_or = _heads(lambda h: rs[h] * (_head(gyh, h) - _head(yh, h) * jnp.mean(_head(gyh, h) * _head(yh, h),
                                                                                    axis=-1, keepdims=True)))
        return [d_or, d_rg, d_om], [_colsum(dy * yh)]

    (d_or, d_rg, d_om), (d_gn,) = _rowwise("back_gate", back_gate, [dmixcat, (proj_r, hw, 3), o_r],
                                           [nw["ret_group_norm"]], [(hw, BF16), (hw, F32), (hw, BF16)], [hw], tm)
    dqc, dkc, dvv = attn_bwd("mla_bwd", qc, kc, vv, d_om, blk, True, MLA_SCALE, o_m, lse)
    drq, drk, drv = attn_bwd("ret_bwd", rq, rk, rv, d_or, blk, False)

    def back_qk(r0, dqc, dkc, dvv, cm, ta, tb):
        dqp = jnp.concatenate(
            [part for h in range(HEADS)
             for part in (_head(dqc, h, MLA_QK_PAD), _rope_m_t(_head(dqc, h, MLA_QK_PAD, HEAD_DIM), cm, ta, tb))],
            axis=1)
        dkn = _heads(lambda h: _head(dkc, h, MLA_QK_PAD))
        dkr = _head(dkc, 0, MLA_QK_PAD, HEAD_DIM)
        for h in range(1, HEADS):
            dkr = dkr + _head(dkc, h, MLA_QK_PAD, HEAD_DIM)
        return [dqp, dkn, _rope_m_t(dkr, cm, ta, tb), dvv], []

    (dqp, dkn, dkr, dvb), _ = _rowwise("back_qk", back_qk, [dqc, dkc, dvv, cos_m, sa, sb], [],
                                       [(qw, BF16), (hw, BF16), (LANES, F32), (hw, BF16)], [], tm)
    (dcqn,) = mm_nt("q_dx", [dqp], [w["wuq"]], tm, MLA_Q_RANK)
    dwuq = mm_tn("q_dw", cqn, dqp, MLA_Q_RANK, _tile(qw, 1024), tk, GRAD_DT)
    (dckvn,) = mm_nt("kv_dx", [dkn, dvb], [w["wuk"], w["wuv"]], tm, MLA_KV_RANK)
    dwuk = mm_tn("k_dw", ckvn, dkn, MLA_KV_RANK, hw, tk, GRAD_DT)
    dwuv = mm_tn("v_dw", ckvn, dvb, MLA_KV_RANK, hw, tk, GRAD_DT)

    def back_proj(r0, drq, drk, drv, d_rg, pc, dcqn, dckvn, dkr, cr, sr, qn, kvn):
        d_q = _heads(lambda h: _rope_r_t(_head(drq, h), cr, sr))
        d_k = _heads(lambda h: _rope_r_t(_head(drk, h), cr, sr) * RET_SCALE)
        dcq, a_q = _rms_bwd(pc[:, :MLA_Q_RANK], qn, dcqn)
        dckv, a_kv = _rms_bwd(pc[:, MLA_Q_RANK:MLA_Q_RANK + MLA_KV_RANK], kvn, dckvn)
        return ([jnp.concatenate([d_q, d_k, drv, d_rg], axis=1), jnp.concatenate([dcq, dckv, dkr], axis=1)],
                [_colsum(a_q), _colsum(a_kv)])

    (dproj_r, dproj_c), (d_qn, d_kvn) = _rowwise(
        "back_proj", back_proj, [drq, drk, drv, d_rg, proj_c, dcqn, dckvn, dkr, cos_r, sin_r],
        [nw["mla_q_norm"], nw["mla_kv_norm"]], [(IN_RET, BF16), (IN_MLA_PAD, BF16)],
        [MLA_Q_RANK, MLA_KV_RANK], tm)
    (dun,) = mm_nt("proj_dx", [dproj_r, dproj_c], [w["w_r"], w["w_c"]], tm, _tile(d, 512))
    dw_r = mm_tn("proj_dw_r", un, dproj_r, dt, _tile(IN_RET, 1024), tk, GRAD_DT)
    dw_c = mm_tn("proj_dw_c", un, dproj_c, dt, IN_MLA_PAD, tk, GRAD_DT)

    def back_ffn1_norm(r0, h, f, dh2, dn, pre, post):
        dx, dpre = _rms_bwd(h, pre, dn)
        dh1 = dh2 + dx
        df, dpost = _rms_bwd(f, post, 0.5 * dh1)
        return [dh1, df], [_colsum(dpre), _colsum(dpost)]

    (dh1, df1), (d_mix_pre, d_post1) = _rowwise(
        "back_ffn1_norm", back_ffn1_norm, [h1, f1, dh2, dun], [nw["mix_pre_norm"], nw["ffn1_post_norm"]],
        [(d, F32), (d, BF16)], [d, d], tm)
    dn1, dwg1, dwu1, dwd1 = _ffn_bwd("ffn1", df1, n1, g1, u1, a1, w["wg1"], w["wu1"], w["wd1"], tm, tmk, tk, dt)

    def back_input(r0, h, dh1, dn, pre):
        dx, dpre = _rms_bwd(h, pre, dn)
        return [dh1 + dx], [_colsum(dpre)]

    (dh0,), (d_pre1,) = _rowwise("back_input", back_input, [h0, dh1, dn1], [nw["ffn1_pre_norm"]], [(d, F32)], [d], tm)

    big = dict(wg1=dwg1, wu1=dwu1, wd1=dwd1, wg2=dwg2, wu2=dwu2, wd2=dwd2, w_r=dw_r, w_c=dw_c, wuq=dwuq,
               wuk=dwuk, wuv=dwuv, w_out=dw_out)
    small = dict(ffn1_pre_norm=d_pre1, ffn1_post_norm=d_post1, mix_pre_norm=d_mix_pre, ret_group_norm=d_gn,
                 mla_q_norm=d_qn, mla_kv_norm=d_kvn, mix_post_norm=d_mix_post, ffn2_pre_norm=d_pre2,
                 ffn2_post_norm=d_post2)
    return loss, dh0[N_META:t_real], big, small, dh0[:N_META]


WEIGHTS = ("meta_tokens", "ffn1_pre_norm", "ffn1_w_gate", "ffn1_w_up", "ffn1_w_down", "ffn1_post_norm",
           "mix_pre_norm", "w_in", "ret_group_norm", "mla_q_norm", "mla_w_uq", "mla_kv_norm", "mla_w_uk",
           "mla_w_uv", "w_out", "mix_post_norm", "ffn2_pre_norm", "ffn2_w_gate", "ffn2_w_up", "ffn2_w_down",
           "ffn2_post_norm")
BIG = ("ffn1_w_gate", "ffn1_w_up", "ffn1_w_down", "w_in", "mla_w_uq", "mla_w_uk", "mla_w_uv", "w_out",
       "ffn2_w_gate", "ffn2_w_up", "ffn2_w_down")
NORMS = ("ffn1_pre_norm", "ffn1_post_norm", "mix_pre_norm", "ret_group_norm", "mla_q_norm", "mla_kv_norm",
         "mix_post_norm", "ffn2_pre_norm", "ffn2_post_norm")


def _unshard_cols(g):
    return g.transpose(1, 0, 2).reshape(g.shape[1], -1)


def _shard_cols(a):
    return a.reshape(a.shape[0], N_CHIPS, -1).transpose(1, 0, 2)


def _pack_rows(rows, width):
    rows = [jnp.pad(r, ((0, 0), (0, width - r.shape[1]))) for r in rows]
    n = sum(r.shape[0] for r in rows)
    return jnp.pad(jnp.concatenate(rows, axis=0), ((0, -n % 8), (0, 0)))


def _step(p, m, v, x, loss_target):
    d = x.shape[2]
    local = {n: p[n][0] for n in BIG}
    names = list(BIG)
    gathered = all_gather_chips([local[n].astype(BF16) for n in names] + [p["meta_tokens"]])
    full = dict(zip(names, gathered[:-1]))
    meta = _unshard_cols(gathered[-1])

    w_in = _unshard_cols(full["w_in"])
    wuq = _unshard_cols(full["mla_w_uq"]).reshape(MLA_Q_RANK, HEADS, HEAD_DIM + MLA_ROPE)
    wuq = jnp.pad(wuq, ((0, 0), (0, 0), (0, MLA_QK_PAD - HEAD_DIM - MLA_ROPE))).reshape(MLA_Q_RANK, HEADS * MLA_QK_PAD)
    w = dict(wg1=full["ffn1_w_gate"], wu1=full["ffn1_w_up"], wd1=full["ffn1_w_down"],
             wg2=full["ffn2_w_gate"], wu2=full["ffn2_w_up"], wd2=full["ffn2_w_down"],
             w_r=w_in[:, :IN_RET], w_c=jnp.pad(w_in[:, IN_RET:], ((0, 0), (0, IN_MLA_PAD - IN_MLA))),
             wuq=wuq, wuk=_unshard_cols(full["mla_w_uk"]), wuv=_unshard_cols(full["mla_w_uv"]),
             w_out=full["w_out"].reshape(-1, d))
    nw = {n: p[n] for n in NORMS}
    loss, grad_x, big, small, d_meta = _local_step(x[0], loss_target[0], meta, w, nw)

    dw_in = _shard_cols(jnp.concatenate([big["w_r"], big["w_c"][:, :IN_MLA]], axis=1))
    dwuq = big["wuq"].reshape(MLA_Q_RANK, HEADS, MLA_QK_PAD)[:, :, :HEAD_DIM + MLA_ROPE]
    contrib = dict(ffn1_w_gate=big["wg1"], ffn1_w_up=big["wu1"], ffn1_w_down=big["wd1"], w_in=dw_in,
                   mla_w_uq=_shard_cols(dwuq.reshape(MLA_Q_RANK, -1)), mla_w_uk=_shard_cols(big["wuk"]),
                   mla_w_uv=_shard_cols(big["wuv"]), w_out=big["w_out"],
                   ffn2_w_gate=big["wg2"], ffn2_w_up=big["wu2"], ffn2_w_down=big["wd2"])
    chip_sums = [pair_add("pair_add_" + n, contrib[n]) for n in names]
    parts = chip_exchange(chip_sums)
    grads = {n: reduce_join("reduce_join_" + n, q) for n, q in zip(names, parts)}

    width = max(d, HEADS * HEAD_DIM)
    packed = _pack_rows([small[n] for n in NORMS] + [d_meta], width)
    total = sum_slots("small_sum", all_gather_devices(packed), F32)
    for i, n in enumerate(NORMS):
        grads[n] = total[i:i + 1, :p[n].shape[1]]
    cols = p["meta_tokens"].shape[1]
    chip = 2 * lax.axis_index("x") + lax.axis_index("y")
    grads["meta_tokens"] = lax.dynamic_slice(total[len(NORMS):len(NORMS) + N_META, :d], (0, chip * cols), (N_META, cols))

    delta, new_m, new_v = {}, {}, {}
    for n in names + ["meta_tokens"]:
        shape = p[n].shape
        flat = lambda a: a.reshape(-1, shape[-1])
        grads[n] = grads[n].reshape(shape)
        out = adamw("adamw_" + n, flat(p[n]), flat(grads[n]), flat(m[n]), flat(v[n]))
        delta[n], new_m[n], new_v[n] = (o.reshape(shape) for o in out)
    pk = lambda src: _pack_rows([src[n] for n in NORMS], width)
    out = adamw("adamw_norms", pk(p), pk(grads), pk(m), pk(v))
    for i, n in enumerate(NORMS):
        delta[n], new_m[n], new_v[n] = (o[i:i + 1, :p[n].shape[1]] for o in out)

    loss = lax.psum(loss, ("x", "y", "c"))
    return loss, grad_x[None], grads, delta, new_m, new_v


def kernel(x, meta_tokens, ffn1_pre_norm, ffn1_w_gate, ffn1_w_up, ffn1_w_down, ffn1_post_norm, mix_pre_norm, w_in, ret_group_norm, mla_q_norm, mla_w_uq, mla_kv_norm, mla_w_uk, mla_w_uv, w_out, mix_post_norm, ffn2_pre_norm, ffn2_w_gate, ffn2_w_up, ffn2_w_down, ffn2_post_norm, loss_target, m_meta_tokens, m_ffn1_pre_norm, m_ffn1_w_gate, m_ffn1_w_up, m_ffn1_w_down, m_ffn1_post_norm, m_mix_pre_norm, m_w_in, m_ret_group_norm, m_mla_q_norm, m_mla_w_uq, m_mla_kv_norm, m_mla_w_uk, m_mla_w_uv, m_w_out, m_mix_post_norm, m_ffn2_pre_norm, m_ffn2_w_gate, m_ffn2_w_up, m_ffn2_w_down, m_ffn2_post_norm, v_meta_tokens, v_ffn1_pre_norm, v_ffn1_w_gate, v_ffn1_w_up, v_ffn1_w_down, v_ffn1_post_norm, v_mix_pre_norm, v_w_in, v_ret_group_norm, v_mla_q_norm, v_mla_w_uq, v_mla_kv_norm, v_mla_w_uk, v_mla_w_uv, v_w_out, v_mix_post_norm, v_ffn2_pre_norm, v_ffn2_w_gate, v_ffn2_w_up, v_ffn2_w_down, v_ffn2_post_norm):
    args = locals()
    p = {n: args[n] for n in WEIGHTS}
    m = {n: args["m_" + n] for n in WEIGHTS}
    v = {n: args["v_" + n] for n in WEIGHTS}
    loss, grad_x, grads, delta, new_m, new_v = _step(p, m, v, x, loss_target)
    return (loss, grad_x, *[grads[n] for n in WEIGHTS], *[delta[n] for n in WEIGHTS],
            *[new_m[n] for n in WEIGHTS], *[new_v[n] for n in WEIGHTS])
```

```python
import functools
import math

import jax
import jax.numpy as jnp
from jax import lax
from jax.experimental import pallas as pl
from jax.experimental.pallas import tpu as pltpu

F32 = jnp.float32
BF16 = jnp.bfloat16

EPS = 1e-6
N_META = 16
HEADS = 8
HEAD_DIM = 128
MLA_ROPE = 64
MLA_QK_PAD = 256
MLA_Q_RANK = 512
MLA_KV_RANK = 256
ROPE_THETA = 10000.0
N_CHIPS = 4
LANES = 128
VMEM_LIMIT = 60 * 2 ** 20

ADAM_LR = 0.001
ADAM_B1 = 0.9
ADAM_B2 = 0.999
ADAM_EPS = 1e-08
ADAM_WD = 0.01
ADAM_STEP = 10

NN = (((1,), (0,)), ((), ()))
NT = (((1,), (1,)), ((), ()))
TN = (((0,), (0,)), ((), ()))
MESH = pl.DeviceIdType.MESH


def _tile(n, pref, align=LANES):
    if n <= pref:
        return n
    best = 0
    for t in range(align, pref + 1, align):
        if n % t == 0:
            best = t
    assert best, (n, pref)
    return best


def _params(*sem):
    return pltpu.CompilerParams(dimension_semantics=sem, vmem_limit_bytes=VMEM_LIMIT)


def _sds(shape, dtype):
    return jax.ShapeDtypeStruct(tuple(shape), dtype)


def _rowwise(name, fn, rows, consts, outs, accs, tr):
    rows = [r if isinstance(r, tuple) else (r, r.shape[1], 0) for r in rows]
    t = rows[0][0].shape[0]
    assert t % tr == 0
    n_r, n_c, n_o = len(rows), len(consts), len(outs)

    def body(*refs):
        i = pl.program_id(0)
        r = [x[...] for x in refs[:n_r]]
        c = [x[...] for x in refs[n_r:n_r + n_c]]
        o_refs = refs[n_r + n_c:n_r + n_c + n_o]
        a_refs = refs[n_r + n_c + n_o:]
        o_vals, a_vals = fn(i * tr, *r, *c)
        for ref, v in zip(o_refs, o_vals):
            ref[...] = v.astype(ref.dtype)
        if a_refs:
            @pl.when(i == 0)
            def _():
                for ref, v in zip(a_refs, a_vals):
                    ref[...] = v

            @pl.when(i > 0)
            def _():
                for ref, v in zip(a_refs, a_vals):
                    ref[...] += v

    in_specs = [pl.BlockSpec((tr, w), functools.partial(lambda cb, i: (i, cb), cb)) for _, w, cb in rows]
    in_specs += [pl.BlockSpec(a.shape, lambda i: (0, 0)) for a in consts]
    out_specs = [pl.BlockSpec((tr, w), lambda i: (i, 0)) for w, _ in outs]
    out_specs += [pl.BlockSpec((1, w), lambda i: (0, 0)) for w in accs]
    out_shape = [_sds((t, w), dt) for w, dt in outs] + [_sds((1, w), F32) for w in accs]
    res = pl.pallas_call(
        body, name=name, grid=(t // tr,), in_specs=in_specs, out_specs=out_specs, out_shape=out_shape,
        compiler_params=_params("arbitrary"),
    )(*[a for a, _, _ in rows], *consts)
    return res[:n_o], res[n_o:]


def _rms(x, w):
    r = lax.rsqrt(jnp.mean(x * x, axis=-1, keepdims=True) + EPS)
    return x * r * w


def _rms_bwd(x, w, dy):
    r = lax.rsqrt(jnp.mean(x * x, axis=-1, keepdims=True) + EPS)
    xh = x * r
    gy = dy * w
    dx = r * (gy - xh * jnp.mean(gy * xh, axis=-1, keepdims=True))
    return dx, dy * xh


def _colsum(v):
    return jnp.sum(v, axis=0, keepdims=True)


def _silu(x):
    return x * jax.nn.sigmoid(x)


def _dsilu(x):
    s = jax.nn.sigmoid(x)
    return s * (1.0 + x * (1.0 - s))


def _rope_r(x, cos, sin):
    return x * cos + pltpu.roll(x, 64, 1) * sin


def _rope_r_t(dy, cos, sin):
    return dy * cos + pltpu.roll(dy * sin, 64, 1)


def _rope_m(x, cos, sa, sb):
    return x * cos + pltpu.roll(x, 32, 1) * sa + pltpu.roll(x, 96, 1) * sb


def _rope_m_t(dy, cos, sa, sb):
    return dy * cos + pltpu.roll(dy * sa, 96, 1) + pltpu.roll(dy * sb, 32, 1)


def _rope_tables(t):
    pos = jnp.arange(t, dtype=F32)

    def cs(dim):
        inv = ROPE_THETA ** (-jnp.arange(0, dim, 2, dtype=F32) / dim)
        ang = pos[:, None] * inv[None, :]
        return jnp.cos(ang), jnp.sin(ang)

    c, s = cs(HEAD_DIM)
    cos_r = jnp.concatenate([c, c], axis=1)
    sin_r = jnp.concatenate([-s, s], axis=1)
    c, s = cs(MLA_ROPE)
    z32, z64 = jnp.zeros_like(s), jnp.zeros((t, 64), F32)
    cos_m = jnp.concatenate([c, c, z64], axis=1)
    sa = jnp.concatenate([z32, s, z64], axis=1)
    sb = jnp.concatenate([-s, z32, z64], axis=1)
    return cos_r, sin_r, cos_m, sa, sb


def _call(body, name, grid, in_specs, out_specs, out_shape, scratch, operands, host=None):
    sem = ("arbitrary",) * len(grid)
    if host is None:
        return pl.pallas_call(body, name=name, grid=grid, in_specs=in_specs, out_specs=out_specs, out_shape=out_shape,
                              scratch_shapes=scratch, compiler_params=_params(*sem))(*operands)
    n_in, n_out, n_s = len(in_specs), len(out_shape), len(scratch)
    h_in, h_out = len(host.ins), len(host.out_shape)

    def hosted(*refs):
        a = n_in
        b = a + h_in
        c = b + n_out
        d = c + h_out
        e = d + n_s
        ids = [pl.program_id(i) for i in range(len(grid))]
        first = functools.reduce(jnp.logical_and, [i == 0 for i in ids])
        last = functools.reduce(jnp.logical_and, [i == g - 1 for i, g in zip(ids, grid)])
        comm = (refs[a:b], refs[c:d], refs[e:])

        @pl.when(first)
        def _():
            host.start(*comm)

        body(*refs[:a], *refs[b:c], *refs[d:e])

        @pl.when(last)
        def _():
            host.finish(*comm)

    hbm = pl.BlockSpec(memory_space=pl.ANY)
    res = pl.pallas_call(
        hosted, name=name, grid=grid, in_specs=list(in_specs) + [hbm] * h_in, out_specs=list(out_specs) + [hbm] * h_out,
        out_shape=list(out_shape) + list(host.out_shape), scratch_shapes=list(scratch) + host.scratch(),
        compiler_params=_params(*sem))(*operands, *host.ins)
    host.result = res[n_out:]
    return res[:n_out]


def _mm(name, grid, operands, in_specs, dns, out_specs, out_shape, epilogue=None, extras=(), extra_specs=(),
        acc_shape=None, host=None):
    n_p, n_e = len(dns), len(extras)
    nk = grid[2]
    n_o = len(out_shape)

    def body(*refs):
        ab = refs[:2 * n_p]
        ex = refs[2 * n_p:2 * n_p + n_e]
        outs = refs[2 * n_p + n_e:2 * n_p + n_e + n_o]
        part = None
        for p in range(n_p):
            d = lax.dot_general(ab[2 * p][...], ab[2 * p + 1][...], dns[p], preferred_element_type=F32)
            part = d if part is None else part + d

        def finish(acc):
            vals = (acc,) if epilogue is None else epilogue(acc, *[e[...] for e in ex])
            for o, v in zip(outs, vals):
                o[...] = v.astype(o.dtype)

        if nk == 1:
            finish(part)
        else:
            acc_ref = refs[2 * n_p + n_e + n_o]
            k = pl.program_id(2)

            @pl.when(k == 0)
            def _():
                acc_ref[...] = part

            @pl.when(k > 0)
            def _():
                acc_ref[...] += part

            @pl.when(k == nk - 1)
            def _():
                finish(acc_ref[...])

    scratch = [] if nk == 1 else [pltpu.VMEM(acc_shape, F32)]
    return _call(body, name, grid, list(in_specs) + list(extra_specs), out_specs, out_shape, scratch,
                 [*operands, *extras], host)


def mm_nn(name, x, w, tm, out_dtype=F32, epilogue=None, outs=None, w2=None, n_block=None, host=None):
    t, kdim = x.shape
    if w.ndim == 3:
        s, _, ns = w.shape
        n, tn, nb = s * ns, ns, s
        wspec = pl.BlockSpec((None, kdim, ns), lambda j, i, k: (j, 0, 0))
    else:
        n = w.shape[1]
        tn = n_block or n
        nb = n // tn
        wspec = pl.BlockSpec((kdim, tn), lambda j, i, k: (0, j))
    xspec = pl.BlockSpec((tm, kdim), lambda j, i, k: (i, 0))
    ospec = pl.BlockSpec((tm, tn), lambda j, i, k: (i, j))
    outs = outs or [out_dtype]
    grid = (nb, t // tm, 1)
    if w2 is None:
        return _mm(name, grid, [x, w], [xspec, wspec], [NN], [ospec] * len(outs), [_sds((t, n), d) for d in outs],
                   epilogue=epilogue, host=host)

    def body(x_ref, w_ref, w2_ref, *o_refs):
        xv = x_ref[...]
        a = jnp.dot(xv, w_ref[...], preferred_element_type=F32)
        b = jnp.dot(xv, w2_ref[...], preferred_element_type=F32)
        for o, v in zip(o_refs, epilogue(a, b)):
            o[...] = v.astype(o.dtype)

    return _call(body, name, grid[:2],
                 [pl.BlockSpec((tm, kdim), lambda j, i: (i, 0)),
                  pl.BlockSpec((None, kdim, tn), lambda j, i: (j, 0, 0)),
                  pl.BlockSpec((None, kdim, tn), lambda j, i: (j, 0, 0))],
                 [pl.BlockSpec((tm, tn), lambda j, i: (i, j))] * len(outs), [_sds((t, n), d) for d in outs], [],
                 [x, w, w2], host)


def mm_nn_k(name, x, w, tm, tk, out_dtype=F32, host=None):
    t, kdim = x.shape
    n = w.shape[1]
    grid = (t // tm, 1, kdim // tk)
    return _mm(name, grid, [x, w],
               [pl.BlockSpec((tm, tk), lambda i, j, k: (i, k)), pl.BlockSpec((tk, n), lambda i, j, k: (k, 0))],
               [NN], [pl.BlockSpec((tm, n), lambda i, j, k: (i, 0))], [_sds((t, n), out_dtype)],
               acc_shape=(tm, n), host=host)[0]


def mm_nt(name, xs, ws, tm, tn, outs=(F32,), epilogue=None, extras=(), extra_blocked=True):
    t = xs[0].shape[0]
    specs, ops = [], []
    for x, w in zip(xs, ws):
        kdim = x.shape[1]
        specs.append(pl.BlockSpec((tm, kdim), lambda j, i, k: (i, 0)))
        if w.ndim == 3:
            assert tn == w.shape[1]
            n = w.shape[0] * w.shape[1]
            specs.append(pl.BlockSpec((None, tn, kdim), lambda j, i, k: (j, 0, 0)))
        else:
            n = w.shape[0]
            specs.append(pl.BlockSpec((tn, kdim), lambda j, i, k: (j, 0)))
        ops += [x, w]
    ospec = pl.BlockSpec((tm, tn), lambda j, i, k: (i, j))
    return _mm(name, (n // tn, t // tm, 1), ops, specs, [NT] * len(xs), [ospec] * len(outs),
               [_sds((t, n), d) for d in outs], epilogue=epilogue, extras=extras,
               extra_specs=[ospec] * len(extras))


def mm_nt_k(name, xs, ws, tm, tn, out_dtype=F32, host=None):
    t = xs[0].shape[0]
    s, n, ns = ws[0].shape
    specs, ops = [], []
    for x, w in zip(xs, ws):
        specs.append(pl.BlockSpec((tm, ns), lambda i, j, k: (i, k)))
        specs.append(pl.BlockSpec((None, tn, ns), lambda i, j, k: (k, j, 0)))
        ops += [x, w]
    return _mm(name, (t // tm, n // tn, s), ops, specs, [NT] * len(xs),
               [pl.BlockSpec((tm, tn), lambda i, j, k: (i, j))], [_sds((t, n), out_dtype)], acc_shape=(tm, tn),
               host=host)[0]


def mm_tn(name, x, y, tm, tn, tk, out_dtype, shard_rows=False, shard_cols=False):
    t, m = x.shape
    n = y.shape[1]
    grid = (m // tm, n // tn, t // tk)
    if shard_cols:
        ospec = pl.BlockSpec((None, tm, tn), lambda i, j, k: (j, i, 0))
        oshape = _sds((n // tn, m, tn), out_dtype)
    elif shard_rows:
        ospec = pl.BlockSpec((None, tm, tn), lambda i, j, k: (i, 0, j))
        oshape = _sds((m // tm, tm, n), out_dtype)
    else:
        ospec = pl.BlockSpec((tm, tn), lambda i, j, k: (i, j))
        oshape = _sds((m, n), out_dtype)
    return _mm(name, grid, [x, y],
               [pl.BlockSpec((tk, tm), lambda i, j, k: (k, i)), pl.BlockSpec((tk, tn), lambda i, j, k: (k, j))],
               [TN], [ospec], [oshape], acc_shape=(tm, tn))[0]


def _decay_logs():
    return [math.log(1.0 - 2.0 ** (-5.0 - h)) for h in range(HEADS)]


def _log_decay(h):
    lg = jnp.float32(_decay_logs()[0])
    for i in range(1, HEADS):
        lg = jnp.where(h == i, jnp.float32(_decay_logs()[i]), lg)
    return lg


def _scores(q, k, scale, qi, ki, blk, softmax, lg, diag):
    s = lax.dot_general(q, k, NT, preferred_element_type=F32)
    row = lax.broadcasted_iota(jnp.int32, s.shape, 0)
    col = lax.broadcasted_iota(jnp.int32, s.shape, 1)
    if softmax:
        s = s * scale
        if diag:
            s = jnp.where(col <= row, s, -1e30)
        return s
    dist = ((qi - ki) * blk + row - col).astype(F32)
    dec = jnp.exp(jnp.maximum(dist, 0.0) * lg)
    if diag:
        dec = jnp.where(col <= row, dec, 0.0)
    return s * dec, dec


def attn_fwd(name, q, k, v, blk, softmax, scale=1.0, host=None):
    t = q.shape[0]
    dq = q.shape[1] // HEADS
    nq = t // blk

    def body(q_ref, k_ref, v_ref, o_ref, *lse_ref):
        h, qi = pl.program_id(0), pl.program_id(1)
        lg = _log_decay(h)
        qv = q_ref[...]

        def block(ki, carry, diag):
            start = pl.multiple_of(ki * blk, blk)
            kv = k_ref[pl.ds(start, blk), :]
            vv = v_ref[pl.ds(start, blk), :]
            if softmax:
                m, l, acc = carry
                s = _scores(qv, kv, scale, qi, ki, blk, True, lg, diag)
                m_new = jnp.maximum(m, jnp.max(s, axis=-1, keepdims=True))
                p = jnp.exp(s - m_new)
                alpha = jnp.exp(m - m_new)
                l = alpha * l + jnp.sum(p, axis=-1, keepdims=True)
                acc = alpha * acc + jnp.dot(p.astype(BF16), vv, preferred_element_type=F32)
                return m_new, l, acc
            p, _ = _scores(qv, kv, scale, qi, ki, blk, False, lg, diag)
            return (carry[0] + jnp.dot(p.astype(BF16), vv, preferred_element_type=F32),)

        if softmax:
            init = (jnp.full((blk, 1), -1e30, F32), jnp.zeros((blk, 1), F32), jnp.zeros((blk, HEAD_DIM), F32))
        else:
            init = (jnp.zeros((blk, HEAD_DIM), F32),)
        carry = lax.fori_loop(0, qi, lambda ki, c: block(ki, c, False), init)
        carry = block(qi, carry, True)
        if softmax:
            m, l, acc = carry
            o_ref[...] = acc / l
            lse_ref[0][...] = jnp.broadcast_to(m + jnp.log(l), (blk, HEAD_DIM))
        else:
            o_ref[...] = carry[0]

    hspec = pl.BlockSpec((blk, HEAD_DIM), lambda h, i: (i, h))
    out_specs = [hspec, hspec] if softmax else [hspec]
    out_shape = [_sds((t, HEADS * HEAD_DIM), F32)] * (2 if softmax else 1)
    return _call(body, name, (HEADS, nq),
                 [pl.BlockSpec((blk, dq), lambda h, i: (i, h)),
                  pl.BlockSpec((t, dq), lambda h, i: (0, h)),
                  pl.BlockSpec((t, HEAD_DIM), lambda h, i: (0, h))],
                 out_specs, out_shape, [], [q, k, v], host)


def attn_bwd(name, q, k, v, do, blk, softmax, scale=1.0, o=None, lse=None, host=None):
    t = q.shape[0]
    dq_w = q.shape[1] // HEADS
    nb = t // blk

    def body(q_ref, k_ref, v_ref, do_ref, *rest):
        if softmax:
            o_ref, lse_ref, dq_ref, dk_ref, dv_ref = rest
        else:
            dq_ref, dk_ref, dv_ref = rest
        h, ki = pl.program_id(0), pl.program_id(1)
        lg = _log_decay(h)
        kv = k_ref[...]
        vv = v_ref[...]

        @pl.when(ki == 0)
        def _():
            dq_ref[...] = jnp.zeros_like(dq_ref)

        def block(qi, carry, diag):
            dk, dv = carry
            start = pl.multiple_of(qi * blk, blk)
            rows = pl.ds(start, blk)
            qv = q_ref[rows, :]
            dov = do_ref[rows, :]
            dp = lax.dot_general(dov, vv, NT, preferred_element_type=F32)
            if softmax:
                s = _scores(qv, kv, scale, qi, ki, blk, True, lg, diag)
                p = jnp.exp(s - lse_ref[rows, :][:, :1])
                delta = jnp.sum(dov.astype(F32) * o_ref[rows, :], axis=-1, keepdims=True)
                ds = p * (dp - delta) * scale
            else:
                p, dec = _scores(qv, kv, scale, qi, ki, blk, False, lg, diag)
                ds = dp * dec
            pb, dsb = p.astype(BF16), ds.astype(BF16)
            dv = dv + lax.dot_general(pb, dov, TN, preferred_element_type=F32)
            dk = dk + lax.dot_general(dsb, qv, TN, preferred_element_type=F32)
            dq_ref[rows, :] += jnp.dot(dsb, kv, preferred_element_type=F32)
            return dk, dv

        carry = block(ki, (jnp.zeros((blk, dq_w), F32), jnp.zeros((blk, HEAD_DIM), F32)), True)
        dk, dv = lax.fori_loop(ki + 1, nb, lambda qi, c: block(qi, c, False), carry)
        dk_ref[...] = dk
        dv_ref[...] = dv

    full = lambda w: pl.BlockSpec((t, w), lambda h, j: (0, h))
    blkd = lambda w: pl.BlockSpec((blk, w), lambda h, j: (j, h))
    ins = [q, k, v, do]
    in_specs = [full(dq_w), blkd(dq_w), blkd(HEAD_DIM), full(HEAD_DIM)]
    if softmax:
        ins += [o, lse]
        in_specs += [full(HEAD_DIM), full(HEAD_DIM)]
    return _call(body, name, (HEADS, nb), in_specs, [full(dq_w), blkd(dq_w), blkd(HEAD_DIM)],
                 [_sds(q.shape, F32), _sds(k.shape, F32), _sds(v.shape, F32)], [], ins, host)


CHIP_FLIPS = ((1, 0), (0, 1), (1, 1))


def _position():
    return lax.axis_index("x"), lax.axis_index("y"), lax.axis_index("c")


class _Comm:
    def __init__(self, ins, out_shape, plan, n_remote, n_local):
        self.ins, self.out_shape, self.plan = list(ins), list(out_shape), plan
        self.n_remote, self.n_local = n_remote, n_local
        self.result = None

    def scratch(self):
        return [pltpu.SemaphoreType.DMA((self.n_remote,)), pltpu.SemaphoreType.DMA((self.n_remote,)),
                pltpu.SemaphoreType.DMA((self.n_local,))]

    def _copies(self, in_refs, out_refs, sems):
        send_sems, recv_sems, local_sems = sems
        pos = _position()
        p = self.plan(pos, in_refs, out_refs)

        def remote(k, src, dst, dev):
            return pltpu.make_async_remote_copy(src_ref=src, dst_ref=dst, send_sem=send_sems.at[k],
                                                recv_sem=recv_sems.at[k], device_id=dev, device_id_type=MESH)

        local = [pltpu.make_async_copy(s, d, local_sems.at[i]) for i, (s, d) in enumerate(p["local"])]
        out = [remote(k, s, d, dev) for k, (s, d, dev) in enumerate(p["sends"])]
        arrivals = [remote(k, d, d, pos) for k, d in enumerate(p["recvs"])]
        return local, out, arrivals

    def start(self, in_refs, out_refs, sems):
        local, out, _ = self._copies(in_refs, out_refs, sems)
        for cp in local + out:
            cp.start()

    def finish(self, in_refs, out_refs, sems):
        local, out, arrivals = self._copies(in_refs, out_refs, sems)
        for cp in arrivals:
            cp.wait_recv()
        for cp in out:
            cp.wait_send()
        for cp in local:
            cp.wait()

    def run(self, name):
        n_in, n_out = len(self.ins), len(self.out_shape)

        def body(*refs):
            comm = (refs[:n_in], refs[n_in:n_in + n_out], refs[n_in + n_out:])
            self.start(*comm)
            self.finish(*comm)

        hbm = pl.BlockSpec(memory_space=pl.ANY)
        self.result = pl.pallas_call(body, name=name, in_specs=[hbm] * n_in, out_specs=[hbm] * n_out,
                                     out_shape=self.out_shape, scratch_shapes=self.scratch())(*self.ins)
        return self.result


def _half_rows(c, rows):
    r2 = rows // 2
    return pl.ds(pl.multiple_of(c * r2, math.gcd(r2, LANES)), r2)


def _half(ref, c, rows, lead=()):
    return ref.at[(*lead, _half_rows(c, rows))]


def gather_halves(shards):
    def plan(pos, ins, outs):
        x, y, c = pos
        me = 2 * x + y
        p = dict(local=[], sends=[], recvs=[])
        for a, (src, dst) in enumerate(zip(ins, outs)):
            rows = shards[a].shape[0]
            p["local"].append((src, dst.at[me]))
            for fx, fy in CHIP_FLIPS:
                px, py = x ^ fx, y ^ fy
                p["sends"].append((_half(src, c, rows), _half(dst, c, rows, (me,)), (px, py, c)))
                p["recvs"].append(_half(dst, c, rows, (2 * px + py,)))
        return p

    return _Comm(shards, [_sds((N_CHIPS, *s.shape), s.dtype) for s in shards], plan,
                 n_remote=3 * len(shards), n_local=len(shards))


def chip_exchange(parts):
    def plan(pos, ins, outs):
        x, y, c = pos
        me = 2 * x + y
        p = dict(local=[], sends=[], recvs=[])
        for src, dst in zip(ins, outs):
            p["local"].append((src.at[me], dst.at[me]))
            for fx, fy in CHIP_FLIPS:
                px, py = x ^ fx, y ^ fy
                peer = 2 * px + py
                p["sends"].append((src.at[peer], dst.at[me], (px, py, c)))
                p["recvs"].append(dst.at[peer])
        return p

    return _Comm(parts, [_sds(g.shape, g.dtype) for g in parts], plan, n_remote=3 * len(parts), n_local=len(parts))


def all_gather_devices(v):
    flips = [(fx, fy, fc) for fx in (0, 1) for fy in (0, 1) for fc in (0, 1)][1:]

    def plan(pos, ins, outs):
        x, y, c = pos
        me = 4 * x + 2 * y + c
        p = dict(local=[(ins[0], outs[0].at[me])], sends=[], recvs=[])
        for fx, fy, fc in flips:
            px, py, pc = x ^ fx, y ^ fy, c ^ fc
            p["sends"].append((ins[0], outs[0].at[me], (px, py, pc)))
            p["recvs"].append(outs[0].at[4 * px + 2 * py + pc])
        return p

    return _Comm([v], [_sds((8, *v.shape), v.dtype)], plan, n_remote=7, n_local=1).run("small_all_gather")[0]


SWAP_CHUNK_BYTES = 3 * 2 ** 19


def _sibling_swap(t, n_steps, value, sbuf, rbuf, send_sems, recv_sems, credits):
    x, y, c = _position()
    sib = (x, y, 1 - c)
    slot = t % 2
    copy = pltpu.make_async_remote_copy(src_ref=sbuf.at[slot], dst_ref=rbuf.at[slot], send_sem=send_sems.at[slot],
                                        recv_sem=recv_sems.at[slot], device_id=sib, device_id_type=MESH)

    @pl.when(t >= 2)
    def _():
        copy.wait_send()
        pl.semaphore_wait(credits.at[slot], 1)

    sbuf[slot] = value
    copy.start()
    copy.wait_recv()
    got = rbuf[slot]

    @pl.when(t + 2 < n_steps)
    def _():
        pl.semaphore_signal(credits.at[slot], inc=1, device_id=sib, device_id_type=MESH)

    @pl.when(t == n_steps - 1)
    def _():
        copy.wait_send()
        if n_steps > 1:
            other = pltpu.make_async_remote_copy(
                src_ref=sbuf.at[1 - slot], dst_ref=rbuf.at[1 - slot], send_sem=send_sems.at[1 - slot],
                recv_sem=recv_sems.at[1 - slot], device_id=sib, device_id_type=MESH)
            other.wait_send()

    return got


def _swap_scratch(rows, cols, dtype):
    return [pltpu.VMEM((2, rows, cols), dtype), pltpu.VMEM((2, rows, cols), dtype),
            pltpu.SemaphoreType.DMA((2,)), pltpu.SemaphoreType.DMA((2,)), pltpu.SemaphoreType.REGULAR((2,))]


def _chunk_rows(rows, cols, dtype):
    return _tile(rows, max(16, SWAP_CHUNK_BYTES // (cols * jnp.dtype(dtype).itemsize)), 16)


def pair_add(name, g):
    s, r, c_ = g.shape
    r2 = r // 2
    cr = _chunk_rows(r2, c_, g.dtype)
    nj = r2 // cr

    def body(core, mine_ref, theirs_ref, o_ref, *scratch):
        t = pl.program_id(0) * nj + pl.program_id(1)
        got = _sibling_swap(t, s * nj, theirs_ref[...], *scratch)
        o_ref[...] = (mine_ref[...].astype(F32) + got.astype(F32)).astype(o_ref.dtype)

    grid_spec = pltpu.PrefetchScalarGridSpec(
        num_scalar_prefetch=1, grid=(s, nj),
        in_specs=[pl.BlockSpec((None, cr, c_), lambda i, j, core: (i, core[0] * nj + j, 0)),
                  pl.BlockSpec((None, cr, c_), lambda i, j, core: (i, (1 - core[0]) * nj + j, 0))],
        out_specs=pl.BlockSpec((None, cr, c_), lambda i, j, core: (i, j, 0)),
        scratch_shapes=_swap_scratch(cr, c_, g.dtype))
    core = lax.axis_index("c").astype(jnp.int32).reshape(1)
    return pl.pallas_call(body, name=name, grid_spec=grid_spec, out_shape=_sds((s, r2, c_), g.dtype),
                          compiler_params=_params("arbitrary", "arbitrary"))(core, g, g)


def reduce_join(name, p):
    s, r2, c_ = p.shape
    cr = _chunk_rows(r2, c_, F32)
    nj = r2 // cr

    def body(core, p_ref, o_ref, *scratch):
        acc = p_ref[0].astype(F32)
        for i in range(1, s):
            acc = acc + p_ref[i].astype(F32)
        got = _sibling_swap(pl.program_id(0), nj, acc, *scratch)
        c = core[0]
        o_ref[c] = acc
        o_ref[1 - c] = got

    grid_spec = pltpu.PrefetchScalarGridSpec(
        num_scalar_prefetch=1, grid=(nj,),
        in_specs=[pl.BlockSpec((s, cr, c_), lambda j, core: (0, j, 0))],
        out_specs=pl.BlockSpec((2, cr, c_), lambda j, core: (0, j, 0)),
        scratch_shapes=_swap_scratch(cr, c_, F32))
    core = lax.axis_index("c").astype(jnp.int32).reshape(1)
    out = pl.pallas_call(body, name=name, grid_spec=grid_spec, out_shape=_sds((2, r2, c_), F32),
                         compiler_params=_params("arbitrary"))(core, p)
    return out.reshape(2 * r2, c_)


def sibling_fill(name, buf):
    s, r, c_ = buf.shape
    r2 = r // 2
    cr = _chunk_rows(r2, c_, buf.dtype)
    nj = r2 // cr
    n_peers = len(CHIP_FLIPS)

    def body(where, in_ref, o_ref, *scratch):
        t = pl.program_id(0) * nj + pl.program_id(1)
        o_ref[...] = _sibling_swap(t, n_peers * nj, in_ref[...], *scratch)

    grid_spec = pltpu.PrefetchScalarGridSpec(
        num_scalar_prefetch=1, grid=(n_peers, nj),
        in_specs=[pl.BlockSpec((None, cr, c_), lambda k, j, where: (where[k], where[n_peers] * nj + j, 0))],
        out_specs=pl.BlockSpec((None, cr, c_), lambda k, j, where: (where[k], (1 - where[n_peers]) * nj + j, 0)),
        scratch_shapes=_swap_scratch(cr, c_, buf.dtype))
    x, y, c = _position()
    where = jnp.stack([2 * (x ^ fx) + (y ^ fy) for fx, fy in CHIP_FLIPS] + [c]).astype(jnp.int32)
    return pl.pallas_call(body, name=name, grid_spec=grid_spec, out_shape=_sds(buf.shape, buf.dtype),
                          input_output_aliases={1: 0}, compiler_params=_params("arbitrary", "arbitrary"))(where, buf)


def sum_slots(name, p, out_dtype):
    s, r, c = p.shape
    tr = _tile(r, 256, 16)

    def body(p_ref, o_ref):
        acc = p_ref[0].astype(F32)
        for i in range(1, s):
            acc = acc + p_ref[i].astype(F32)
        o_ref[...] = acc.astype(o_ref.dtype)

    return pl.pallas_call(
        body, name=name, grid=(r // tr,), in_specs=[pl.BlockSpec((s, tr, c), lambda i: (0, i, 0))],
        out_specs=pl.BlockSpec((tr, c), lambda i: (i, 0)), out_shape=_sds((r, c), out_dtype),
        compiler_params=_params("arbitrary"),
    )(p)


def adamw(name, w, g, m, v):
    r, c = w.shape
    tr = _tile(r, 256, 8)

    def fn(_, w, g, m, v):
        m = ADAM_B1 * m + (1.0 - ADAM_B1) * g
        v = ADAM_B2 * v + (1.0 - ADAM_B2) * (g * g)
        m_hat = m / (1.0 - ADAM_B1 ** ADAM_STEP)
        v_hat = v / (1.0 - ADAM_B2 ** ADAM_STEP)
        delta = -ADAM_LR * (m_hat / (jnp.sqrt(v_hat) + ADAM_EPS) + ADAM_WD * w)
        return [delta, m, v], []

    outs, _ = _rowwise(name, fn, [w, g, m, v], [], [(c, F32)] * 3, [], tr)
    return outs


RET_SCALE = HEAD_DIM ** -0.5
MLA_SCALE = (HEAD_DIM + MLA_ROPE) ** -0.5
GRAD_DT = BF16
IN_RET = 4 * HEADS * HEAD_DIM
IN_MLA = MLA_Q_RANK + MLA_KV_RANK + MLA_ROPE
IN_MLA_PAD = IN_MLA + 64


def _heads(fn):
    return jnp.concatenate([fn(h) for h in range(HEADS)], axis=1)


def _head(a, h, stride=HEAD_DIM, off=0):
    return a[:, h * stride + off:h * stride + off + HEAD_DIM]


def _group_norm(o):
    rs = [lax.rsqrt(jnp.mean(_head(o, h) * _head(o, h), axis=-1, keepdims=True) + EPS) for h in range(HEADS)]
    return _heads(lambda h: _head(o, h) * rs[h]), rs


class _Alone:
    def host(self, kernel_name):
        return None

    def done(self, kernel_name, w):
        pass

    def grads(self, group, g):
        pass


def _ffn_fwd(tag, n, w, k, tm, tmk, plan):
    up, down = tag + "_up", tag + "_down"
    g, u, a = mm_nn(up, n, w["wg" + k], tm, outs=[BF16] * 3, w2=w["wu" + k],
                    epilogue=lambda g, u: (g, u, _silu(g) * u), host=plan.host(up))
    plan.done(up, w)
    ff = a.shape[1]
    wd = w["wd" + k]
    f = mm_nn_k(down, a, wd.reshape(ff, wd.shape[2]), tmk, _tile(ff, 512), host=plan.host(down))
    plan.done(down, w)
    return g, u, a, f


def _ffn_bwd(tag, df, n, g, u, a, wg, wu, wd, tm, tmk, tk, dt, plan):
    ns = wg.shape[2]

    def gate_grads(da, g, u):
        g, u = g.astype(F32), u.astype(F32)
        return da * u * _dsilu(g), da * _silu(g)

    dg, du = mm_nt(tag + "_da", [df], [wd], tm, ns, outs=(BF16, BF16), epilogue=gate_grads, extras=(g, u))
    dwd = mm_tn(tag + "_dwd", a, df, ns, dt, tk, GRAD_DT, shard_rows=True)
    dn = mm_nt_k(tag + "_dn", [dg, du], [wg, wu], tmk, dt, host=plan.host(tag + "_dn"))
    plan.done(tag + "_dn", None)
    dwg = mm_tn(tag + "_dwg", n, dg, dt, ns, tk, GRAD_DT, shard_cols=True)
    dwu = mm_tn(tag + "_dwu", n, du, dt, ns, tk, GRAD_DT, shard_cols=True)
    return dn, dwg, dwu, dwd


def _local_step(x, tgt, meta, w, nw, plan):
    seq, d = x.shape
    t_real = N_META + seq
    tp = -(-t_real // LANES) * LANES
    zpad = jnp.zeros((tp - t_real, d), F32)
    h0 = jnp.concatenate([meta, x, zpad], axis=0)
    tgt_p = jnp.concatenate([jnp.zeros((N_META, d), F32), tgt, zpad], axis=0)
    cos_r, sin_r, cos_m, sa, sb = _rope_tables(tp)
    tm = _tile(tp, 512)
    tmk = _tile(tp, 1408)
    tk = tmk
    blk = tm
    dt = _tile(d, 1024)
    hw = HEADS * HEAD_DIM
    qw = HEADS * MLA_QK_PAD

    (n1,), _ = _rowwise("ffn1_norm", lambda r0, h, g: ([_rms(h, g)], []), [h0], [nw["ffn1_pre_norm"]],
                        [(d, BF16)], [], tm)
    g1, u1, a1, f1 = _ffn_fwd("ffn1", n1, w, "1", tm, tmk, plan)

    def post_ffn1(r0, h, f, post, pre):
        h1 = h + 0.5 * _rms(f, post)
        return [h1, _rms(h1, pre)], []

    (h1, un), _ = _rowwise("mix_norm", post_ffn1, [h0, f1], [nw["ffn1_post_norm"], nw["mix_pre_norm"]],
                           [(d, F32), (d, BF16)], [], tm)
    (proj_r,) = mm_nn("proj_r", un, w["w_r"], tm, n_block=_tile(IN_RET, 1024))
    (proj_c,) = mm_nn("proj_c", un, w["w_c"], tm)

    def split_proj(r0, pr, pc, cr, sr, cm, ta, tb, qn, kvn):
        rq = _heads(lambda h: _rope_r(_head(pr, h), cr, sr))
        rk = _heads(lambda h: _rope_r(_head(pr, h, off=hw), cr, sr) * RET_SCALE)
        rv = pr[:, 2 * hw:3 * hw]
        cqn = _rms(pc[:, :MLA_Q_RANK], qn)
        ckvn = _rms(pc[:, MLA_Q_RANK:MLA_Q_RANK + MLA_KV_RANK], kvn)
        krr = _rope_m(pc[:, MLA_Q_RANK + MLA_KV_RANK:], cm, ta, tb)
        return [rq, rk, rv, cqn, ckvn, krr], []

    (rq, rk, rv, cqn, ckvn, krr), _ = _rowwise(
        "split_proj", split_proj, [proj_r, proj_c, cos_r, sin_r, cos_m, sa, sb],
        [nw["mla_q_norm"], nw["mla_kv_norm"]],
        [(hw, BF16), (hw, BF16), (hw, BF16), (MLA_Q_RANK, BF16), (MLA_KV_RANK, BF16), (LANES, F32)], [], tm)
    (qp,) = mm_nn("q_up", cqn, w["wuq"], tm)
    (kn,) = mm_nn("k_up", ckvn, w["wuk"], tm)
    (vv,) = mm_nn("v_up", ckvn, w["wuv"], tm, out_dtype=BF16)

    def build_qk(r0, qp, kn, krr, cm, ta, tb):
        qc = jnp.concatenate(
            [part for h in range(HEADS)
             for part in (_head(qp, h, MLA_QK_PAD), _rope_m(_head(qp, h, MLA_QK_PAD, HEAD_DIM), cm, ta, tb))], axis=1)
        kc = jnp.concatenate([part for h in range(HEADS) for part in (_head(kn, h), krr)], axis=1)
        return [qc, kc], []

    (qc, kc), _ = _rowwise("build_qk", build_qk, [qp, kn, krr, cos_m, sa, sb], [], [(qw, BF16), (qw, BF16)], [], tm)
    o_m, lse = attn_fwd("mla_fwd", qc, kc, vv, blk, True, MLA_SCALE, host=plan.host("mla_fwd"))
    plan.done("mla_fwd", w)
    (o_r,) = attn_fwd("ret_fwd", rq, rk, rv, blk, False)

    def gate_mix(r0, rg, o_r, o_m, gn):
        y, _ = _group_norm(o_r)
        return [jnp.concatenate([_silu(rg) * (y * gn), o_m], axis=1)], []

    (mixcat,), _ = _rowwise("gate_mix", gate_mix, [(proj_r, hw, 3), o_r, o_m], [nw["ret_group_norm"]],
                            [(2 * hw, BF16)], [], tm)
    (mix,) = mm_nn("mix_out", mixcat, w["w_out"], tm, n_block=dt)

    def post_mix(r0, h, m, post, pre):
        h2 = h + _rms(m, post)
        return [h2, _rms(h2, pre)], []

    (h2, n3), _ = _rowwise("ffn2_norm", post_mix, [h1, mix], [nw["mix_post_norm"], nw["ffn2_pre_norm"]],
                           [(d, F32), (d, BF16)], [], tm)
    g2, u2, a2, f2 = _ffn_fwd("ffn2", n3, w, "2", tm, tmk, plan)

    def loss_head(r0, h, f, t, post):
        h3 = h + 0.5 * _rms(f, post)
        row = r0 + lax.broadcasted_iota(jnp.int32, h3.shape, 0)
        err = jnp.where(row >= N_META, jnp.where(row < t_real, h3 - t, 0.0), 0.0)
        dh3 = err / d
        df, dpost = _rms_bwd(f, post, 0.5 * dh3)
        return [dh3, df], [_colsum(err * err), _colsum(dpost)]

    (dh3, df2), (loss_vec, d_post2) = _rowwise("loss_head", loss_head, [h2, f2, tgt_p], [nw["ffn2_post_norm"]],
                                               [(d, F32), (d, BF16)], [d, d], tm)
    loss = 0.5 * jnp.sum(loss_vec) / d
    dn3, dwg2, dwu2, dwd2 = _ffn_bwd("ffn2", df2, n3, g2, u2, a2, w["wg2"], w["wu2"], w["wd2"], tm, tmk, tk, dt, plan)
    plan.grads("ffn2", dict(wg2=dwg2, wu2=dwu2, wd2=dwd2))

    def back_mix_norm(r0, h, m, dh3, dn, pre, post):
        dx, dpre = _rms_bwd(h, pre, dn)
        dh2 = dh3 + dx
        dm, dpost = _rms_bwd(m, post, dh2)
        return [dh2, dm], [_colsum(dpre), _colsum(dpost)]

    (dh2, dmix), (d_pre2, d_mix_post) = _rowwise(
        "back_mix_norm", back_mix_norm, [h2, mix, dh3, dn3], [nw["ffn2_pre_norm"], nw["mix_post_norm"]],
        [(d, F32), (d, BF16)], [d, d], tm)
    (dmixcat,) = mm_nt("mix_dx", [dmix], [w["w_out"]], tm, _tile(2 * hw, 512))
    dw_out = mm_tn("mix_dw", mixcat, dmix, 2 * hw // N_CHIPS, dt, tk, GRAD_DT, shard_rows=True)

    def back_gate(r0, dmc, rg, o_r, gn):
        d_ret, d_om = dmc[:, :hw], dmc[:, hw:]
        yh, rs = _group_norm(o_r)
        d_rg = d_ret * (yh * gn) * _dsilu(rg)
        dy = d_ret * _silu(rg)
        gyh = dy * gn
        d_or = _heads(lambda h: rs[h] * (_head(gyh, h) - _head(yh, h) * jnp.mean(_head(gyh, h) * _head(yh, h),
                                                                                    axis=-1, keepdims=True)))
        return [d_or, d_rg, d_om], [_colsum(dy * yh)]

    (d_or, d_rg, d_om), (d_gn,) = _rowwise("back_gate", back_gate, [dmixcat, (proj_r, hw, 3), o_r],
                                           [nw["ret_group_norm"]], [(hw, BF16), (hw, F32), (hw, BF16)], [hw], tm)
    dqc, dkc, dvv = attn_bwd("mla_bwd", qc, kc, vv, d_om, blk, True, MLA_SCALE, o_m, lse, host=plan.host("mla_bwd"))
    plan.done("mla_bwd", None)
    drq, drk, drv = attn_bwd("ret_bwd", rq, rk, rv, d_or, blk, False)

    def back_qk(r0, dqc, dkc, dvv, cm, ta, tb):
        dqp = jnp.concatenate(
            [part for h in range(HEADS)
             for part in (_head(dqc, h, MLA_QK_PAD), _rope_m_t(_head(dqc, h, MLA_QK_PAD, HEAD_DIM), cm, ta, tb))],
            axis=1)
        dkn = _heads(lambda h: _head(dkc, h, MLA_QK_PAD))
        dkr = _head(dkc, 0, MLA_QK_PAD, HEAD_DIM)
        for h in range(1, HEADS):
            dkr = dkr + _head(dkc, h, MLA_QK_PAD, HEAD_DIM)
        return [dqp, dkn, _rope_m_t(dkr, cm, ta, tb), dvv], []

    (dqp, dkn, dkr, dvb), _ = _rowwise("back_qk", back_qk, [dqc, dkc, dvv, cos_m, sa, sb], [],
                                       [(qw, BF16), (hw, BF16), (LANES, F32), (hw, BF16)], [], tm)
    (dcqn,) = mm_nt("q_dx", [dqp], [w["wuq"]], tm, MLA_Q_RANK)
    dwuq = mm_tn("q_dw", cqn, dqp, MLA_Q_RANK, _tile(qw, 1024), tk, GRAD_DT)
    (dckvn,) = mm_nt("kv_dx", [dkn, dvb], [w["wuk"], w["wuv"]], tm, MLA_KV_RANK)
    dwuk = mm_tn("k_dw", ckvn, dkn, MLA_KV_RANK, hw, tk, GRAD_DT)
    dwuv = mm_tn("v_dw", ckvn, dvb, MLA_KV_RANK, hw, tk, GRAD_DT)

    def back_proj(r0, drq, drk, drv, d_rg, pc, dcqn, dckvn, dkr, cr, sr, qn, kvn):
        d_q = _heads(lambda h: _rope_r_t(_head(drq, h), cr, sr))
        d_k = _heads(lambda h: _rope_r_t(_head(drk, h), cr, sr) * RET_SCALE)
        dcq, a_q = _rms_bwd(pc[:, :MLA_Q_RANK], qn, dcqn)
        dckv, a_kv = _rms_bwd(pc[:, MLA_Q_RANK:MLA_Q_RANK + MLA_KV_RANK], kvn, dckvn)
        return ([jnp.concatenate([d_q, d_k, drv, d_rg], axis=1), jnp.concatenate([dcq, dckv, dkr], axis=1)],
                [_colsum(a_q), _colsum(a_kv)])

    (dproj_r, dproj_c), (d_qn, d_kvn) = _rowwise(
        "back_proj", back_proj, [drq, drk, drv, d_rg, proj_c, dcqn, dckvn, dkr, cos_r, sin_r],
        [nw["mla_q_norm"], nw["mla_kv_norm"]], [(IN_RET, BF16), (IN_MLA_PAD, BF16)],
        [MLA_Q_RANK, MLA_KV_RANK], tm)
    (dun,) = mm_nt("proj_dx", [dproj_r, dproj_c], [w["w_r"], w["w_c"]], tm, _tile(d, 512))
    dw_r = mm_tn("proj_dw_r", un, dproj_r, dt, _tile(IN_RET, 1024), tk, GRAD_DT)
    dw_c = mm_tn("proj_dw_c", un, dproj_c, dt, IN_MLA_PAD, tk, GRAD_DT)
    plan.grads("mixer", dict(w_r=dw_r, w_c=dw_c, wuq=dwuq, wuk=dwuk, wuv=dwuv, w_out=dw_out))

    def back_ffn1_norm(r0, h, f, dh2, dn, pre, post):
        dx, dpre = _rms_bwd(h, pre, dn)
        dh1 = dh2 + dx
        df, dpost = _rms_bwd(f, post, 0.5 * dh1)
        return [dh1, df], [_colsum(dpre), _colsum(dpost)]

    (dh1, df1), (d_mix_pre, d_post1) = _rowwise(
        "back_ffn1_norm", back_ffn1_norm, [h1, f1, dh2, dun], [nw["mix_pre_norm"], nw["ffn1_post_norm"]],
        [(d, F32), (d, BF16)], [d, d], tm)
    dn1, dwg1, dwu1, dwd1 = _ffn_bwd("ffn1", df1, n1, g1, u1, a1, w["wg1"], w["wu1"], w["wd1"], tm, tmk, tk, dt, plan)
    plan.grads("ffn1", dict(wg1=dwg1, wu1=dwu1, wd1=dwd1))

    def back_input(r0, h, dh1, dn, pre):
        dx, dpre = _rms_bwd(h, pre, dn)
        return [dh1 + dx], [_colsum(dpre)]

    (dh0,), (d_pre1,) = _rowwise("back_input", back_input, [h0, dh1, dn1], [nw["ffn1_pre_norm"]], [(d, F32)], [d], tm)

    small = dict(ffn1_pre_norm=d_pre1, ffn1_post_norm=d_post1, mix_pre_norm=d_mix_pre, ret_group_norm=d_gn,
                 mla_q_norm=d_qn, mla_kv_norm=d_kvn, mix_post_norm=d_mix_post, ffn2_pre_norm=d_pre2,
                 ffn2_post_norm=d_post2)
    return loss, dh0[N_META:t_real], small, dh0[:N_META]


WEIGHTS = ("meta_tokens", "ffn1_pre_norm", "ffn1_w_gate", "ffn1_w_up", "ffn1_w_down", "ffn1_post_norm",
           "mix_pre_norm", "w_in", "ret_group_norm", "mla_q_norm", "mla_w_uq", "mla_kv_norm", "mla_w_uk",
           "mla_w_uv", "w_out", "mix_post_norm", "ffn2_pre_norm", "ffn2_w_gate", "ffn2_w_up", "ffn2_w_down",
           "ffn2_post_norm")
BIG = ("ffn1_w_gate", "ffn1_w_up", "ffn1_w_down", "w_in", "mla_w_uq", "mla_w_uk", "mla_w_uv", "w_out",
       "ffn2_w_gate", "ffn2_w_up", "ffn2_w_down")
NORMS = ("ffn1_pre_norm", "ffn1_post_norm", "mix_pre_norm", "ret_group_norm", "mla_q_norm", "mla_kv_norm",
         "mix_post_norm", "ffn2_pre_norm", "ffn2_post_norm")


def _unshard_cols(g):
    return g.transpose(1, 0, 2).reshape(g.shape[1], -1)


def _shard_cols(a):
    return a.reshape(a.shape[0], N_CHIPS, -1).transpose(1, 0, 2)


def _pack_rows(rows, width):
    rows = [jnp.pad(r, ((0, 0), (0, width - r.shape[1]))) for r in rows]
    n = sum(r.shape[0] for r in rows)
    return jnp.pad(jnp.concatenate(rows, axis=0), ((0, -n % 8), (0, 0)))


def _weight_views(full):
    w = {}
    for n, g in full.items():
        if n == "w_in":
            w_in = _unshard_cols(g)
            w["w_r"] = w_in[:, :IN_RET]
            w["w_c"] = jnp.pad(w_in[:, IN_RET:], ((0, 0), (0, IN_MLA_PAD - IN_MLA)))
        elif n == "mla_w_uq":
            q = _unshard_cols(g).reshape(MLA_Q_RANK, HEADS, HEAD_DIM + MLA_ROPE)
            q = jnp.pad(q, ((0, 0), (0, 0), (0, MLA_QK_PAD - HEAD_DIM - MLA_ROPE)))
            w["wuq"] = q.reshape(MLA_Q_RANK, HEADS * MLA_QK_PAD)
        elif n in ("mla_w_uk", "mla_w_uv"):
            w["wu" + n[-1]] = _unshard_cols(g)
        elif n == "w_out":
            w["w_out"] = g.reshape(-1, g.shape[2])
        else:
            w["w" + n[7] + n[3]] = g
    return w


def _contributions(g):
    c = {}
    for k in ("1", "2"):
        if "wg" + k in g:
            c.update({f"ffn{k}_w_gate": g["wg" + k], f"ffn{k}_w_up": g["wu" + k], f"ffn{k}_w_down": g["wd" + k]})
    if "w_r" in g:
        dwuq = g["wuq"].reshape(MLA_Q_RANK, HEADS, MLA_QK_PAD)[:, :, :HEAD_DIM + MLA_ROPE]
        c.update(w_in=_shard_cols(jnp.concatenate([g["w_r"], g["w_c"][:, :IN_MLA]], axis=1)),
                 mla_w_uq=_shard_cols(dwuq.reshape(MLA_Q_RANK, -1)), mla_w_uk=_shard_cols(g["wuk"]),
                 mla_w_uv=_shard_cols(g["wuv"]), w_out=g["w_out"])
    return c


class _Schedule(_Alone):
    CARRIED = {"ffn1_up": ("w_in", "mla_w_uq", "mla_w_uk", "mla_w_uv", "w_out"),
               "ffn1_down": ("ffn2_w_down",), "mla_fwd": ("ffn2_w_gate", "ffn2_w_up")}
    GRAD_HOST = {"ffn2": "mla_bwd", "mixer": "ffn1_dn"}

    def __init__(self, shards):
        self.gathers = {k: (gather_halves([shards[n] for n in names]), names) for k, names in self.CARRIED.items()}
        self.exchanges = {}
        self.grad = {}

    def host(self, kernel_name):
        for table in (self.gathers, self.exchanges):
            if kernel_name in table:
                return table[kernel_name][0]
        return None

    def done(self, kernel_name, w):
        if kernel_name in self.gathers:
            comm, names = self.gathers[kernel_name]
            w.update(_weight_views({n: sibling_fill("fill_" + n, b) for n, b in zip(names, comm.result)}))
        elif kernel_name in self.exchanges:
            comm, names = self.exchanges[kernel_name]
            self._reduce(names, comm.result)

    def grads(self, group, g):
        contrib = _contributions(g)
        names = list(contrib)
        comm = chip_exchange([pair_add("pair_add_" + n, contrib[n]) for n in names])
        if group in self.GRAD_HOST:
            self.exchanges[self.GRAD_HOST[group]] = (comm, names)
        else:
            self._reduce(names, comm.run("grad_exchange_" + group))

    def _reduce(self, names, parts):
        for n, q in zip(names, parts):
            self.grad[n] = reduce_join("reduce_join_" + n, q)


def _step(p, m, v, x, loss_target):
    d = x.shape[2]
    names = list(BIG)
    shards = {n: p[n][0].astype(BF16) for n in names}
    first = ("ffn1_w_gate", "ffn1_w_up", "ffn1_w_down")
    gathered = gather_halves([shards[n] for n in first] + [p["meta_tokens"]]).run("gather_ffn1")
    filled = [sibling_fill("fill_" + n, b) for n, b in zip(first + ("meta_tokens",), gathered)]
    w = _weight_views(dict(zip(first, filled[:-1])))
    meta = _unshard_cols(filled[-1])
    nw = {n: p[n] for n in NORMS}
    plan = _Schedule(shards)
    loss, grad_x, small, d_meta = _local_step(x[0], loss_target[0], meta, w, nw, plan)
    grads = dict(plan.grad)

    width = max(d, HEADS * HEAD_DIM)
    packed = _pack_rows([small[n] for n in NORMS] + [d_meta], width)
    total = sum_slots("small_sum", all_gather_devices(packed), F32)
    for i, n in enumerate(NORMS):
        grads[n] = total[i:i + 1, :p[n].shape[1]]
    cols = p["meta_tokens"].shape[1]
    chip = 2 * lax.axis_index("x") + lax.axis_index("y")
    grads["meta_tokens"] = lax.dynamic_slice(total[len(NORMS):len(NORMS) + N_META, :d], (0, chip * cols), (N_META, cols))

    delta, new_m, new_v = {}, {}, {}
    for n in names + ["meta_tokens"]:
        shape = p[n].shape
        flat = lambda a: a.reshape(-1, shape[-1])
        grads[n] = grads[n].reshape(shape)
        out = adamw("adamw_" + n, flat(p[n]), flat(grads[n]), flat(m[n]), flat(v[n]))
        delta[n], new_m[n], new_v[n] = (o.reshape(shape) for o in out)
    pk = lambda src: _pack_rows([src[n] for n in NORMS], width)
    out = adamw("adamw_norms", pk(p), pk(grads), pk(m), pk(v))
    for i, n in enumerate(NORMS):
        delta[n], new_m[n], new_v[n] = (o[i:i + 1, :p[n].shape[1]] for o in out)

    loss = lax.psum(loss, ("x", "y", "c"))
    return loss, grad_x[None], grads, delta, new_m, new_v


def kernel(x, meta_tokens, ffn1_pre_norm, ffn1_w_gate, ffn1_w_up, ffn1_w_down, ffn1_post_norm, mix_pre_norm, w_in, ret_group_norm, mla_q_norm, mla_w_uq, mla_kv_norm, mla_w_uk, mla_w_uv, w_out, mix_post_norm, ffn2_pre_norm, ffn2_w_gate, ffn2_w_up, ffn2_w_down, ffn2_post_norm, loss_target, m_meta_tokens, m_ffn1_pre_norm, m_ffn1_w_gate, m_ffn1_w_up, m_ffn1_w_down, m_ffn1_post_norm, m_mix_pre_norm, m_w_in, m_ret_group_norm, m_mla_q_norm, m_mla_w_uq, m_mla_kv_norm, m_mla_w_uk, m_mla_w_uv, m_w_out, m_mix_post_norm, m_ffn2_pre_norm, m_ffn2_w_gate, m_ffn2_w_up, m_ffn2_w_down, m_ffn2_post_norm, v_meta_tokens, v_ffn1_pre_norm, v_ffn1_w_gate, v_ffn1_w_up, v_ffn1_w_down, v_ffn1_post_norm, v_mix_pre_norm, v_w_in, v_ret_group_norm, v_mla_q_norm, v_mla_w_uq, v_mla_kv_norm, v_mla_w_uk, v_mla_w_uv, v_w_out, v_mix_post_norm, v_ffn2_pre_norm, v_ffn2_w_gate, v_ffn2_w_up, v_ffn2_w_down, v_ffn2_post_norm):
    args = locals()
    p = {n: args[n] for n in WEIGHTS}
    m = {n: args["m_" + n] for n in WEIGHTS}
    v = {n: args["v_" + n] for n in WEIGHTS}
    loss, grad_x, grads, delta, new_m, new_v = _step(p, m, v, x, loss_target)
    return (loss, grad_x, *[grads[n] for n in WEIGHTS], *[delta[n] for n in WEIGHTS],
            *[new_m[n] for n in WEIGHTS], *[new_v[n] for n in WEIGHTS])
```

```python
import functools
import math

import jax
import jax.numpy as jnp
from jax import lax
from jax.experimental import pallas as pl
from jax.experimental.pallas import tpu as pltpu

F32 = jnp.float32
BF16 = jnp.bfloat16

EPS = 1e-6
N_META = 16
HEADS = 8
HEAD_DIM = 128
MLA_ROPE = 64
MLA_QK_PAD = 256
MLA_Q_RANK = 512
MLA_KV_RANK = 256
ROPE_THETA = 10000.0
N_CHIPS = 4
LANES = 128
VMEM_LIMIT = 60 * 2 ** 20

ADAM_LR = 0.001
ADAM_B1 = 0.9
ADAM_B2 = 0.999
ADAM_EPS = 1e-08
ADAM_WD = 0.01
ADAM_STEP = 10

NN = (((1,), (0,)), ((), ()))
NT = (((1,), (1,)), ((), ()))
TN = (((0,), (0,)), ((), ()))
MESH = pl.DeviceIdType.MESH


def _tile(n, pref, align=LANES):
    if n <= pref:
        return n
    best = 0
    for t in range(align, pref + 1, align):
        if n % t == 0:
            best = t
    assert best, (n, pref)
    return best


def _params(*sem):
    return pltpu.CompilerParams(dimension_semantics=sem, vmem_limit_bytes=VMEM_LIMIT)


def _sds(shape, dtype):
    return jax.ShapeDtypeStruct(tuple(shape), dtype)


def _rowwise(name, fn, rows, consts, outs, accs, tr):
    rows = [r if isinstance(r, tuple) else (r, r.shape[1], 0) for r in rows]
    t = rows[0][0].shape[0]
    assert t % tr == 0
    n_r, n_c, n_o = len(rows), len(consts), len(outs)

    def body(*refs):
        i = pl.program_id(0)
        r = [x[...] for x in refs[:n_r]]
        c = [x[...] for x in refs[n_r:n_r + n_c]]
        o_refs = refs[n_r + n_c:n_r + n_c + n_o]
        a_refs = refs[n_r + n_c + n_o:]
        o_vals, a_vals = fn(i * tr, *r, *c)
        for ref, v in zip(o_refs, o_vals):
            ref[...] = v.astype(ref.dtype)
        if a_refs:
            @pl.when(i == 0)
            def _():
                for ref, v in zip(a_refs, a_vals):
                    ref[...] = v

            @pl.when(i > 0)
            def _():
                for ref, v in zip(a_refs, a_vals):
                    ref[...] += v

    in_specs = [pl.BlockSpec((tr, w), functools.partial(lambda cb, i: (i, cb), cb)) for _, w, cb in rows]
    in_specs += [pl.BlockSpec(a.shape, lambda i: (0, 0)) for a in consts]
    out_specs = [pl.BlockSpec((tr, w), lambda i: (i, 0)) for w, _ in outs]
    out_specs += [pl.BlockSpec((1, w), lambda i: (0, 0)) for w in accs]
    out_shape = [_sds((t, w), dt) for w, dt in outs] + [_sds((1, w), F32) for w in accs]
    res = pl.pallas_call(
        body, name=name, grid=(t // tr,), in_specs=in_specs, out_specs=out_specs, out_shape=out_shape,
        compiler_params=_params("arbitrary"),
    )(*[a for a, _, _ in rows], *consts)
    return res[:n_o], res[n_o:]


def _rms(x, w):
    r = lax.rsqrt(jnp.mean(x * x, axis=-1, keepdims=True) + EPS)
    return x * r * w


def _rms_bwd(x, w, dy):
    r = lax.rsqrt(jnp.mean(x * x, axis=-1, keepdims=True) + EPS)
    xh = x * r
    gy = dy * w
    dx = r * (gy - xh * jnp.mean(gy * xh, axis=-1, keepdims=True))
    return dx, dy * xh


def _colsum(v):
    return jnp.sum(v, axis=0, keepdims=True)


def _silu(x):
    return x * jax.nn.sigmoid(x)


def _dsilu(x):
    s = jax.nn.sigmoid(x)
    return s * (1.0 + x * (1.0 - s))


def _rope_r(x, cos, sin):
    return x * cos + pltpu.roll(x, 64, 1) * sin


def _rope_r_t(dy, cos, sin):
    return dy * cos + pltpu.roll(dy * sin, 64, 1)


def _rope_m(x, cos, sa, sb):
    return x * cos + pltpu.roll(x, 32, 1) * sa + pltpu.roll(x, 96, 1) * sb


def _rope_m_t(dy, cos, sa, sb):
    return dy * cos + pltpu.roll(dy * sa, 96, 1) + pltpu.roll(dy * sb, 32, 1)


def _rope_tables(t):
    pos = jnp.arange(t, dtype=F32)

    def cs(dim):
        inv = ROPE_THETA ** (-jnp.arange(0, dim, 2, dtype=F32) / dim)
        ang = pos[:, None] * inv[None, :]
        return jnp.cos(ang), jnp.sin(ang)

    c, s = cs(HEAD_DIM)
    cos_r = jnp.concatenate([c, c], axis=1)
    sin_r = jnp.concatenate([-s, s], axis=1)
    c, s = cs(MLA_ROPE)
    z32, z64 = jnp.zeros_like(s), jnp.zeros((t, 64), F32)
    cos_m = jnp.concatenate([c, c, z64], axis=1)
    sa = jnp.concatenate([z32, s, z64], axis=1)
    sb = jnp.concatenate([-s, z32, z64], axis=1)
    return cos_r, sin_r, cos_m, sa, sb


def _call(body, name, grid, in_specs, out_specs, out_shape, scratch, operands, host=None):
    sem = ("arbitrary",) * len(grid)
    if host is None:
        return pl.pallas_call(body, name=name, grid=grid, in_specs=in_specs, out_specs=out_specs, out_shape=out_shape,
                              scratch_shapes=scratch, compiler_params=_params(*sem))(*operands)
    n_in, n_out, n_s = len(in_specs), len(out_shape), len(scratch)
    h_in, h_out = len(host.ins), len(host.out_shape)

    def hosted(*refs):
        a = n_in
        b = a + h_in
        c = b + n_out
        d = c + h_out
        e = d + n_s
        ids = [pl.program_id(i) for i in range(len(grid))]
        first = functools.reduce(jnp.logical_and, [i == 0 for i in ids])
        last = functools.reduce(jnp.logical_and, [i == g - 1 for i, g in zip(ids, grid)])
        comm = (refs[a:b], refs[c:d], refs[e:])

        @pl.when(first)
        def _():
            host.start(*comm)

        body(*refs[:a], *refs[b:c], *refs[d:e])

        @pl.when(last)
        def _():
            host.finish(*comm)

    hbm = pl.BlockSpec(memory_space=pl.ANY)
    res = pl.pallas_call(
        hosted, name=name, grid=grid, in_specs=list(in_specs) + [hbm] * h_in, out_specs=list(out_specs) + [hbm] * h_out,
        out_shape=list(out_shape) + list(host.out_shape), scratch_shapes=list(scratch) + host.scratch(),
        compiler_params=_params(*sem))(*operands, *host.ins)
    host.result = res[n_out:]
    return res[:n_out]


def _mm(name, grid, operands, in_specs, dns, out_specs, out_shape, epilogue=None, extras=(), extra_specs=(),
        acc_shape=None, host=None):
    n_p, n_e = len(dns), len(extras)
    nk = grid[2]
    n_o = len(out_shape)

    def body(*refs):
        ab = refs[:2 * n_p]
        ex = refs[2 * n_p:2 * n_p + n_e]
        outs = refs[2 * n_p + n_e:2 * n_p + n_e + n_o]
        part = None
        for p in range(n_p):
            d = lax.dot_general(ab[2 * p][...], ab[2 * p + 1][...], dns[p], preferred_element_type=F32)
            part = d if part is None else part + d

        def finish(acc):
            vals = (acc,) if epilogue is None else epilogue(acc, *[e[...] for e in ex])
            for o, v in zip(outs, vals):
                o[...] = v.astype(o.dtype)

        if nk == 1:
            finish(part)
        else:
            acc_ref = refs[2 * n_p + n_e + n_o]
            k = pl.program_id(2)

            @pl.when(k == 0)
            def _():
                acc_ref[...] = part

            @pl.when(k > 0)
            def _():
                acc_ref[...] += part

            @pl.when(k == nk - 1)
            def _():
                finish(acc_ref[...])

    scratch = [] if nk == 1 else [pltpu.VMEM(acc_shape, F32)]
    return _call(body, name, grid, list(in_specs) + list(extra_specs), out_specs, out_shape, scratch,
                 [*operands, *extras], host)


def mm_nn(name, x, w, tm, out_dtype=F32, epilogue=None, outs=None, w2=None, n_block=None, host=None):
    t, kdim = x.shape
    if w.ndim == 3:
        s, _, ns = w.shape
        n, tn, nb = s * ns, ns, s
        wspec = pl.BlockSpec((None, kdim, ns), lambda j, i, k: (j, 0, 0))
    else:
        n = w.shape[1]
        tn = n_block or n
        nb = n // tn
        wspec = pl.BlockSpec((kdim, tn), lambda j, i, k: (0, j))
    xspec = pl.BlockSpec((tm, kdim), lambda j, i, k: (i, 0))
    ospec = pl.BlockSpec((tm, tn), lambda j, i, k: (i, j))
    outs = outs or [out_dtype]
    grid = (nb, t // tm, 1)
    if w2 is None:
        return _mm(name, grid, [x, w], [xspec, wspec], [NN], [ospec] * len(outs), [_sds((t, n), d) for d in outs],
                   epilogue=epilogue, host=host)

    def body(x_ref, w_ref, w2_ref, *o_refs):
        xv = x_ref[...]
        a = jnp.dot(xv, w_ref[...], preferred_element_type=F32)
        b = jnp.dot(xv, w2_ref[...], preferred_element_type=F32)
        for o, v in zip(o_refs, epilogue(a, b)):
            o[...] = v.astype(o.dtype)

    return _call(body, name, grid[:2],
                 [pl.BlockSpec((tm, kdim), lambda j, i: (i, 0)),
                  pl.BlockSpec((None, kdim, tn), lambda j, i: (j, 0, 0)),
                  pl.BlockSpec((None, kdim, tn), lambda j, i: (j, 0, 0))],
                 [pl.BlockSpec((tm, tn), lambda j, i: (i, j))] * len(outs), [_sds((t, n), d) for d in outs], [],
                 [x, w, w2], host)


def mm_nn_k(name, x, w, tm, tk, out_dtype=F32, host=None):
    t, kdim = x.shape
    n = w.shape[1]
    grid = (t // tm, 1, kdim // tk)
    return _mm(name, grid, [x, w],
               [pl.BlockSpec((tm, tk), lambda i, j, k: (i, k)), pl.BlockSpec((tk, n), lambda i, j, k: (k, 0))],
               [NN], [pl.BlockSpec((tm, n), lambda i, j, k: (i, 0))], [_sds((t, n), out_dtype)],
               acc_shape=(tm, n), host=host)[0]


def mm_nt(name, xs, ws, tm, tn, outs=(F32,), epilogue=None, extras=(), host=None):
    t = xs[0].shape[0]
    specs, ops = [], []
    for x, w in zip(xs, ws):
        kdim = x.shape[1]
        specs.append(pl.BlockSpec((tm, kdim), lambda j, i, k: (i, 0)))
        if w.ndim == 3:
            assert tn == w.shape[1]
            n = w.shape[0] * w.shape[1]
            specs.append(pl.BlockSpec((None, tn, kdim), lambda j, i, k: (j, 0, 0)))
        else:
            n = w.shape[0]
            specs.append(pl.BlockSpec((tn, kdim), lambda j, i, k: (j, 0)))
        ops += [x, w]
    ospec = pl.BlockSpec((tm, tn), lambda j, i, k: (i, j))
    return _mm(name, (n // tn, t // tm, 1), ops, specs, [NT] * len(xs), [ospec] * len(outs),
               [_sds((t, n), d) for d in outs], epilogue=epilogue, extras=extras,
               extra_specs=[ospec] * len(extras), host=host)


def mm_nt_k(name, xs, ws, tm, tn, out_dtype=F32, host=None):
    t = xs[0].shape[0]
    s, n, ns = ws[0].shape
    specs, ops = [], []
    for x, w in zip(xs, ws):
        specs.append(pl.BlockSpec((tm, ns), lambda i, j, k: (i, k)))
        specs.append(pl.BlockSpec((None, tn, ns), lambda i, j, k: (k, j, 0)))
        ops += [x, w]
    return _mm(name, (t // tm, n // tn, s), ops, specs, [NT] * len(xs),
               [pl.BlockSpec((tm, tn), lambda i, j, k: (i, j))], [_sds((t, n), out_dtype)], acc_shape=(tm, tn),
               host=host)[0]


def mm_tn(name, x, y, tm, tn, tk, out_dtype, shard_rows=False, shard_cols=False, host=None):
    t, m = x.shape
    n = y.shape[1]
    grid = (m // tm, n // tn, t // tk)
    if shard_cols:
        ospec = pl.BlockSpec((None, tm, tn), lambda i, j, k: (j, i, 0))
        oshape = _sds((n // tn, m, tn), out_dtype)
    elif shard_rows:
        ospec = pl.BlockSpec((None, tm, tn), lambda i, j, k: (i, 0, j))
        oshape = _sds((m // tm, tm, n), out_dtype)
    else:
        ospec = pl.BlockSpec((tm, tn), lambda i, j, k: (i, j))
        oshape = _sds((m, n), out_dtype)
    return _mm(name, grid, [x, y],
               [pl.BlockSpec((tk, tm), lambda i, j, k: (k, i)), pl.BlockSpec((tk, tn), lambda i, j, k: (k, j))],
               [TN], [ospec], [oshape], acc_shape=(tm, tn), host=host)[0]


def _decay_logs():
    return [math.log(1.0 - 2.0 ** (-5.0 - h)) for h in range(HEADS)]


def _log_decay(h):
    lg = jnp.float32(_decay_logs()[0])
    for i in range(1, HEADS):
        lg = jnp.where(h == i, jnp.float32(_decay_logs()[i]), lg)
    return lg


def _scores(q, k, scale, qi, ki, blk, softmax, lg, diag):
    s = lax.dot_general(q, k, NT, preferred_element_type=F32)
    row = lax.broadcasted_iota(jnp.int32, s.shape, 0)
    col = lax.broadcasted_iota(jnp.int32, s.shape, 1)
    if softmax:
        s = s * scale
        if diag:
            s = jnp.where(col <= row, s, -1e30)
        return s
    dist = ((qi - ki) * blk + row - col).astype(F32)
    dec = jnp.exp(jnp.maximum(dist, 0.0) * lg)
    if diag:
        dec = jnp.where(col <= row, dec, 0.0)
    return s * dec, dec


def attn_fwd(name, q, k, v, blk, softmax, scale=1.0, host=None):
    t = q.shape[0]
    dq = q.shape[1] // HEADS
    nq = t // blk

    def body(q_ref, k_ref, v_ref, o_ref, *lse_ref):
        h, qi = pl.program_id(0), pl.program_id(1)
        lg = _log_decay(h)
        qv = q_ref[...]

        def block(ki, carry, diag):
            start = pl.multiple_of(ki * blk, blk)
            kv = k_ref[pl.ds(start, blk), :]
            vv = v_ref[pl.ds(start, blk), :]
            if softmax:
                m, l, acc = carry
                s = _scores(qv, kv, scale, qi, ki, blk, True, lg, diag)
                m_new = jnp.maximum(m, jnp.max(s, axis=-1, keepdims=True))
                p = jnp.exp(s - m_new)
                alpha = jnp.exp(m - m_new)
                l = alpha * l + jnp.sum(p, axis=-1, keepdims=True)
                acc = alpha * acc + jnp.dot(p.astype(BF16), vv, preferred_element_type=F32)
                return m_new, l, acc
            p, _ = _scores(qv, kv, scale, qi, ki, blk, False, lg, diag)
            return (carry[0] + jnp.dot(p.astype(BF16), vv, preferred_element_type=F32),)

        if softmax:
            init = (jnp.full((blk, 1), -1e30, F32), jnp.zeros((blk, 1), F32), jnp.zeros((blk, HEAD_DIM), F32))
        else:
            init = (jnp.zeros((blk, HEAD_DIM), F32),)
        carry = lax.fori_loop(0, qi, lambda ki, c: block(ki, c, False), init)
        carry = block(qi, carry, True)
        if softmax:
            m, l, acc = carry
            o_ref[...] = acc / l
            lse_ref[0][...] = jnp.broadcast_to(m + jnp.log(l), (blk, HEAD_DIM))
        else:
            o_ref[...] = carry[0]

    hspec = pl.BlockSpec((blk, HEAD_DIM), lambda h, i: (i, h))
    out_specs = [hspec, hspec] if softmax else [hspec]
    out_shape = [_sds((t, HEADS * HEAD_DIM), F32)] * (2 if softmax else 1)
    return _call(body, name, (HEADS, nq),
                 [pl.BlockSpec((blk, dq), lambda h, i: (i, h)),
                  pl.BlockSpec((t, dq), lambda h, i: (0, h)),
                  pl.BlockSpec((t, HEAD_DIM), lambda h, i: (0, h))],
                 out_specs, out_shape, [], [q, k, v], host)


def attn_bwd(name, q, k, v, do, blk, softmax, scale=1.0, o=None, lse=None, host=None):
    t = q.shape[0]
    dq_w = q.shape[1] // HEADS
    nb = t // blk

    def body(q_ref, k_ref, v_ref, do_ref, *rest):
        if softmax:
            o_ref, lse_ref, dq_ref, dk_ref, dv_ref = rest
        else:
            dq_ref, dk_ref, dv_ref = rest
        h, ki = pl.program_id(0), pl.program_id(1)
        lg = _log_decay(h)
        kv = k_ref[...]
        vv = v_ref[...]

        @pl.when(ki == 0)
        def _():
            dq_ref[...] = jnp.zeros_like(dq_ref)

        def block(qi, carry, diag):
            dk, dv = carry
            start = pl.multiple_of(qi * blk, blk)
            rows = pl.ds(start, blk)
            qv = q_ref[rows, :]
            dov = do_ref[rows, :]
            dp = lax.dot_general(dov, vv, NT, preferred_element_type=F32)
            if softmax:
                s = _scores(qv, kv, scale, qi, ki, blk, True, lg, diag)
                p = jnp.exp(s - lse_ref[rows, :][:, :1])
                delta = jnp.sum(dov.astype(F32) * o_ref[rows, :], axis=-1, keepdims=True)
                ds = p * (dp - delta) * scale
            else:
                p, dec = _scores(qv, kv, scale, qi, ki, blk, False, lg, diag)
                ds = dp * dec
            pb, dsb = p.astype(BF16), ds.astype(BF16)
            dv = dv + lax.dot_general(pb, dov, TN, preferred_element_type=F32)
            dk = dk + lax.dot_general(dsb, qv, TN, preferred_element_type=F32)
            dq_ref[rows, :] += jnp.dot(dsb, kv, preferred_element_type=F32)
            return dk, dv

        carry = block(ki, (jnp.zeros((blk, dq_w), F32), jnp.zeros((blk, HEAD_DIM), F32)), True)
        dk, dv = lax.fori_loop(ki + 1, nb, lambda qi, c: block(qi, c, False), carry)
        dk_ref[...] = dk
        dv_ref[...] = dv

    full = lambda w: pl.BlockSpec((t, w), lambda h, j: (0, h))
    blkd = lambda w: pl.BlockSpec((blk, w), lambda h, j: (j, h))
    ins = [q, k, v, do]
    in_specs = [full(dq_w), blkd(dq_w), blkd(HEAD_DIM), full(HEAD_DIM)]
    if softmax:
        ins += [o, lse]
        in_specs += [full(HEAD_DIM), full(HEAD_DIM)]
    return _call(body, name, (HEADS, nb), in_specs, [full(dq_w), blkd(dq_w), blkd(HEAD_DIM)],
                 [_sds(q.shape, F32), _sds(k.shape, F32), _sds(v.shape, F32)], [], ins, host)


CHIP_FLIPS = ((1, 0), (0, 1), (1, 1))


def _position():
    return lax.axis_index("x"), lax.axis_index("y"), lax.axis_index("c")


class _Comm:
    def __init__(self, ins, out_shape, plan, n_remote, n_local):
        self.ins, self.out_shape, self.plan = list(ins), list(out_shape), plan
        self.n_remote, self.n_local = n_remote, n_local
        self.result = None

    def scratch(self):
        return [pltpu.SemaphoreType.DMA((self.n_remote,)), pltpu.SemaphoreType.DMA((self.n_remote,)),
                pltpu.SemaphoreType.DMA((self.n_local,))]

    def _copies(self, in_refs, out_refs, sems):
        send_sems, recv_sems, local_sems = sems
        pos = _position()
        p = self.plan(pos, in_refs, out_refs)

        def remote(k, src, dst, dev):
            return pltpu.make_async_remote_copy(src_ref=src, dst_ref=dst, send_sem=send_sems.at[k],
                                                recv_sem=recv_sems.at[k], device_id=dev, device_id_type=MESH)

        local = [pltpu.make_async_copy(s, d, local_sems.at[i]) for i, (s, d) in enumerate(p["local"])]
        out = [remote(k, s, d, dev) for k, (s, d, dev) in enumerate(p["sends"])]
        arrivals = [functools.partial(remote, k, d, d, pos) for k, d in enumerate(p["recvs"])]
        return local, out, arrivals

    def start(self, in_refs, out_refs, sems):
        local, out, _ = self._copies(in_refs, out_refs, sems)
        for cp in local + out:
            cp.start()

    def finish(self, in_refs, out_refs, sems):
        local, out, arrivals = self._copies(in_refs, out_refs, sems)
        for make in arrivals:
            make().wait_recv()
        for cp in out:
            cp.wait_send()
        for cp in local:
            cp.wait()

    def run(self, name):
        n_in, n_out = len(self.ins), len(self.out_shape)

        def body(*refs):
            comm = (refs[:n_in], refs[n_in:n_in + n_out], refs[n_in + n_out:])
            self.start(*comm)
            self.finish(*comm)

        hbm = pl.BlockSpec(memory_space=pl.ANY)
        self.result = pl.pallas_call(body, name=name, in_specs=[hbm] * n_in, out_specs=[hbm] * n_out,
                                     out_shape=self.out_shape, scratch_shapes=self.scratch())(*self.ins)
        return self.result


def _half_rows(c, rows):
    r2 = rows // 2
    return pl.ds(pl.multiple_of(c * r2, math.gcd(r2, LANES)), r2)


def _half(ref, c, rows, lead=()):
    return ref.at[(*lead, _half_rows(c, rows))]


def gather_halves(shards):
    def plan(pos, ins, outs):
        x, y, c = pos
        me = 2 * x + y
        p = dict(local=[], sends=[], recvs=[])
        for a, (src, dst) in enumerate(zip(ins, outs)):
            rows = shards[a].shape[0]
            p["local"].append((src, dst.at[me]))
            for fx, fy in CHIP_FLIPS:
                px, py = x ^ fx, y ^ fy
                p["sends"].append((_half(src, c, rows), _half(dst, c, rows, (me,)), (px, py, c)))
                p["recvs"].append(_half(dst, c, rows, (2 * px + py,)))
        return p

    return _Comm(shards, [_sds((N_CHIPS, *s.shape), s.dtype) for s in shards], plan,
                 n_remote=3 * len(shards), n_local=len(shards))


def chip_exchange(parts):
    def plan(pos, ins, outs):
        x, y, c = pos
        me = 2 * x + y
        p = dict(local=[], sends=[], recvs=[])
        for src, dst in zip(ins, outs):
            p["local"].append((src.at[me], dst.at[me]))
            for fx, fy in CHIP_FLIPS:
                px, py = x ^ fx, y ^ fy
                peer = 2 * px + py
                p["sends"].append((src.at[peer], dst.at[me], (px, py, c)))
                p["recvs"].append(dst.at[peer])
        return p

    return _Comm(parts, [_sds(g.shape, g.dtype) for g in parts], plan, n_remote=3 * len(parts), n_local=len(parts))


def all_gather_devices(v):
    flips = [(fx, fy, fc) for fx in (0, 1) for fy in (0, 1) for fc in (0, 1)][1:]

    def plan(pos, ins, outs):
        x, y, c = pos
        me = 4 * x + 2 * y + c
        p = dict(local=[(ins[0], outs[0].at[me])], sends=[], recvs=[])
        for fx, fy, fc in flips:
            px, py, pc = x ^ fx, y ^ fy, c ^ fc
            p["sends"].append((ins[0], outs[0].at[me], (px, py, pc)))
            p["recvs"].append(outs[0].at[4 * px + 2 * py + pc])
        return p

    return _Comm([v], [_sds((8, *v.shape), v.dtype)], plan, n_remote=7, n_local=1).run("small_all_gather")[0]


SWAP_CHUNK_BYTES = 3 * 2 ** 19


def _sibling_swap(t, n_steps, value, sbuf, rbuf, send_sems, recv_sems, credits):
    x, y, c = _position()
    sib = (x, y, 1 - c)
    slot = t % 2
    copy = pltpu.make_async_remote_copy(src_ref=sbuf.at[slot], dst_ref=rbuf.at[slot], send_sem=send_sems.at[slot],
                                        recv_sem=recv_sems.at[slot], device_id=sib, device_id_type=MESH)

    @pl.when(t >= 2)
    def _():
        copy.wait_send()
        pl.semaphore_wait(credits.at[slot], 1)

    sbuf[slot] = value
    copy.start()
    copy.wait_recv()
    got = rbuf[slot]

    @pl.when(t + 2 < n_steps)
    def _():
        pl.semaphore_signal(credits.at[slot], inc=1, device_id=sib, device_id_type=MESH)

    @pl.when(t == n_steps - 1)
    def _():
        copy.wait_send()
        if n_steps > 1:
            other = pltpu.make_async_remote_copy(
                src_ref=sbuf.at[1 - slot], dst_ref=rbuf.at[1 - slot], send_sem=send_sems.at[1 - slot],
                recv_sem=recv_sems.at[1 - slot], device_id=sib, device_id_type=MESH)
            other.wait_send()

    return got


def _swap_scratch(rows, cols, dtype):
    return [pltpu.VMEM((2, rows, cols), dtype), pltpu.VMEM((2, rows, cols), dtype),
            pltpu.SemaphoreType.DMA((2,)), pltpu.SemaphoreType.DMA((2,)), pltpu.SemaphoreType.REGULAR((2,))]


def _chunk_rows(rows, cols, dtype):
    return _tile(rows, max(16, SWAP_CHUNK_BYTES // (cols * jnp.dtype(dtype).itemsize)), 16)


def pair_add(name, g):
    s, r, c_ = g.shape
    r2 = r // 2
    cr = _chunk_rows(r2, c_, g.dtype)
    nj = r2 // cr

    def body(core, mine_ref, theirs_ref, o_ref, *scratch):
        t = pl.program_id(0) * nj + pl.program_id(1)
        got = _sibling_swap(t, s * nj, theirs_ref[...], *scratch)
        o_ref[...] = (mine_ref[...].astype(F32) + got.astype(F32)).astype(o_ref.dtype)

    grid_spec = pltpu.PrefetchScalarGridSpec(
        num_scalar_prefetch=1, grid=(s, nj),
        in_specs=[pl.BlockSpec((None, cr, c_), lambda i, j, core: (i, core[0] * nj + j, 0)),
                  pl.BlockSpec((None, cr, c_), lambda i, j, core: (i, (1 - core[0]) * nj + j, 0))],
        out_specs=pl.BlockSpec((None, cr, c_), lambda i, j, core: (i, j, 0)),
        scratch_shapes=_swap_scratch(cr, c_, g.dtype))
    core = lax.axis_index("c").astype(jnp.int32).reshape(1)
    return pl.pallas_call(body, name=name, grid_spec=grid_spec, out_shape=_sds((s, r2, c_), g.dtype),
                          compiler_params=_params("arbitrary", "arbitrary"))(core, g, g)


def reduce_join(name, p):
    s, r2, c_ = p.shape
    cr = _chunk_rows(r2, c_, F32)
    nj = r2 // cr

    def body(core, p_ref, o_ref, *scratch):
        acc = p_ref[0].astype(F32)
        for i in range(1, s):
            acc = acc + p_ref[i].astype(F32)
        got = _sibling_swap(pl.program_id(0), nj, acc, *scratch)
        c = core[0]
        o_ref[c] = acc
        o_ref[1 - c] = got

    grid_spec = pltpu.PrefetchScalarGridSpec(
        num_scalar_prefetch=1, grid=(nj,),
        in_specs=[pl.BlockSpec((s, cr, c_), lambda j, core: (0, j, 0))],
        out_specs=pl.BlockSpec((2, cr, c_), lambda j, core: (0, j, 0)),
        scratch_shapes=_swap_scratch(cr, c_, F32))
    core = lax.axis_index("c").astype(jnp.int32).reshape(1)
    out = pl.pallas_call(body, name=name, grid_spec=grid_spec, out_shape=_sds((2, r2, c_), F32),
                         compiler_params=_params("arbitrary"))(core, p)
    return out.reshape(2 * r2, c_)


def sibling_fill(name, buf):
    s, r, c_ = buf.shape
    r2 = r // 2
    cr = _chunk_rows(r2, c_, buf.dtype)
    nj = r2 // cr
    n_peers = len(CHIP_FLIPS)

    def body(where, in_ref, o_ref, *scratch):
        t = pl.program_id(0) * nj + pl.program_id(1)
        o_ref[...] = _sibling_swap(t, n_peers * nj, in_ref[...], *scratch)

    grid_spec = pltpu.PrefetchScalarGridSpec(
        num_scalar_prefetch=1, grid=(n_peers, nj),
        in_specs=[pl.BlockSpec((None, cr, c_), lambda k, j, where: (where[k], where[n_peers] * nj + j, 0))],
        out_specs=pl.BlockSpec((None, cr, c_), lambda k, j, where: (where[k], (1 - where[n_peers]) * nj + j, 0)),
        scratch_shapes=_swap_scratch(cr, c_, buf.dtype))
    x, y, c = _position()
    where = jnp.stack([2 * (x ^ fx) + (y ^ fy) for fx, fy in CHIP_FLIPS] + [c]).astype(jnp.int32)
    return pl.pallas_call(body, name=name, grid_spec=grid_spec, out_shape=_sds(buf.shape, buf.dtype),
                          input_output_aliases={1: 0}, compiler_params=_params("arbitrary", "arbitrary"))(where, buf)


def sum_slots(name, p, out_dtype):
    s, r, c = p.shape
    tr = _tile(r, 256, 16)

    def body(p_ref, o_ref):
        acc = p_ref[0].astype(F32)
        for i in range(1, s):
            acc = acc + p_ref[i].astype(F32)
        o_ref[...] = acc.astype(o_ref.dtype)

    return pl.pallas_call(
        body, name=name, grid=(r // tr,), in_specs=[pl.BlockSpec((s, tr, c), lambda i: (0, i, 0))],
        out_specs=pl.BlockSpec((tr, c), lambda i: (i, 0)), out_shape=_sds((r, c), out_dtype),
        compiler_params=_params("arbitrary"),
    )(p)


def adamw(name, w, g, m, v):
    r, c = w.shape
    tr = _tile(r, 256, 8)

    def fn(_, w, g, m, v):
        m = ADAM_B1 * m + (1.0 - ADAM_B1) * g
        v = ADAM_B2 * v + (1.0 - ADAM_B2) * (g * g)
        m_hat = m / (1.0 - ADAM_B1 ** ADAM_STEP)
        v_hat = v / (1.0 - ADAM_B2 ** ADAM_STEP)
        delta = -ADAM_LR * (m_hat / (jnp.sqrt(v_hat) + ADAM_EPS) + ADAM_WD * w)
        return [delta, m, v], []

    outs, _ = _rowwise(name, fn, [w, g, m, v], [], [(c, F32)] * 3, [], tr)
    return outs


RET_SCALE = HEAD_DIM ** -0.5
MLA_SCALE = (HEAD_DIM + MLA_ROPE) ** -0.5
GRAD_DT = BF16
IN_RET = 4 * HEADS * HEAD_DIM
IN_MLA = MLA_Q_RANK + MLA_KV_RANK + MLA_ROPE
IN_MLA_PAD = IN_MLA + 64


def _heads(fn):
    return jnp.concatenate([fn(h) for h in range(HEADS)], axis=1)


def _head(a, h, stride=HEAD_DIM, off=0):
    return a[:, h * stride + off:h * stride + off + HEAD_DIM]


def _group_norm(o):
    rs = [lax.rsqrt(jnp.mean(_head(o, h) * _head(o, h), axis=-1, keepdims=True) + EPS) for h in range(HEADS)]
    return _heads(lambda h: _head(o, h) * rs[h]), rs


class _Alone:
    def host(self, kernel_name):
        return None

    def done(self, kernel_name, w):
        pass

    def grads(self, g):
        pass


def _ffn_fwd(tag, n, w, k, tm, tmk, plan):
    up, down = tag + "_up", tag + "_down"
    g, u, a = mm_nn(up, n, w["wg" + k], tm, outs=[BF16] * 3, w2=w["wu" + k],
                    epilogue=lambda g, u: (g, u, _silu(g) * u), host=plan.host(up))
    plan.done(up, w)
    ff = a.shape[1]
    wd = w["wd" + k]
    f = mm_nn_k(down, a, wd.reshape(ff, wd.shape[2]), tmk, _tile(ff, 512), host=plan.host(down))
    plan.done(down, w)
    return g, u, a, f


def _ffn_bwd(tag, df, n, g, u, a, wg, wu, wd, tm, tmk, tk, dt, plan):
    ns = wg.shape[2]

    def gate_grads(da, g, u):
        g, u = g.astype(F32), u.astype(F32)
        return da * u * _dsilu(g), da * _silu(g)

    def hosted(kernel_name, call):
        out = call(plan.host(kernel_name))
        plan.done(kernel_name, None)
        return out

    k = tag[-1]
    dg, du = hosted(tag + "_da", lambda h: mm_nt(tag + "_da", [df], [wd], tm, ns, outs=(BF16, BF16),
                                                 epilogue=gate_grads, extras=(g, u), host=h))
    dwg = hosted(tag + "_dwg", lambda h: mm_tn(tag + "_dwg", n, dg, dt, ns, tk, GRAD_DT, shard_cols=True, host=h))
    plan.grads({"wg" + k: dwg})
    dwu = hosted(tag + "_dwu", lambda h: mm_tn(tag + "_dwu", n, du, dt, ns, tk, GRAD_DT, shard_cols=True, host=h))
    plan.grads({"wu" + k: dwu})
    dwd = hosted(tag + "_dwd", lambda h: mm_tn(tag + "_dwd", a, df, ns, dt, tk, GRAD_DT, shard_rows=True, host=h))
    plan.grads({"wd" + k: dwd})
    dn = hosted(tag + "_dn", lambda h: mm_nt_k(tag + "_dn", [dg, du], [wg, wu], tmk, dt, host=h))
    return dn


def _local_step(x, tgt, meta, w, nw, plan):
    seq, d = x.shape
    t_real = N_META + seq
    tp = -(-t_real // LANES) * LANES
    zpad = jnp.zeros((tp - t_real, d), F32)
    h0 = jnp.concatenate([meta, x, zpad], axis=0)
    tgt_p = jnp.concatenate([jnp.zeros((N_META, d), F32), tgt, zpad], axis=0)
    cos_r, sin_r, cos_m, sa, sb = _rope_tables(tp)
    tm = _tile(tp, 512)
    tmk = _tile(tp, 1408)
    tk = tmk
    blk = tm
    dt = _tile(d, 1024)
    hw = HEADS * HEAD_DIM
    qw = HEADS * MLA_QK_PAD

    (n1,), _ = _rowwise("ffn1_norm", lambda r0, h, g: ([_rms(h, g)], []), [h0], [nw["ffn1_pre_norm"]],
                        [(d, BF16)], [], tm)
    g1, u1, a1, f1 = _ffn_fwd("ffn1", n1, w, "1", tm, tmk, plan)

    def post_ffn1(r0, h, f, post, pre):
        h1 = h + 0.5 * _rms(f, post)
        return [h1, _rms(h1, pre)], []

    (h1, un), _ = _rowwise("mix_norm", post_ffn1, [h0, f1], [nw["ffn1_post_norm"], nw["mix_pre_norm"]],
                           [(d, F32), (d, BF16)], [], tm)
    (proj_r,) = mm_nn("proj_r", un, w["w_r"], tm, n_block=_tile(IN_RET, 1024))
    (proj_c,) = mm_nn("proj_c", un, w["w_c"], tm)

    def split_proj(r0, pr, pc, cr, sr, cm, ta, tb, qn, kvn):
        rq = _heads(lambda h: _rope_r(_head(pr, h), cr, sr))
        rk = _heads(lambda h: _rope_r(_head(pr, h, off=hw), cr, sr) * RET_SCALE)
        rv = pr[:, 2 * hw:3 * hw]
        cqn = _rms(pc[:, :MLA_Q_RANK], qn)
        ckvn = _rms(pc[:, MLA_Q_RANK:MLA_Q_RANK + MLA_KV_RANK], kvn)
        krr = _rope_m(pc[:, MLA_Q_RANK + MLA_KV_RANK:], cm, ta, tb)
        return [rq, rk, rv, cqn, ckvn, krr], []

    (rq, rk, rv, cqn, ckvn, krr), _ = _rowwise(
        "split_proj", split_proj, [proj_r, proj_c, cos_r, sin_r, cos_m, sa, sb],
        [nw["mla_q_norm"], nw["mla_kv_norm"]],
        [(hw, BF16), (hw, BF16), (hw, BF16), (MLA_Q_RANK, BF16), (MLA_KV_RANK, BF16), (LANES, F32)], [], tm)
    (qp,) = mm_nn("q_up", cqn, w["wuq"], tm)
    (kn,) = mm_nn("k_up", ckvn, w["wuk"], tm)
    (vv,) = mm_nn("v_up", ckvn, w["wuv"], tm, out_dtype=BF16)

    def build_qk(r0, qp, kn, krr, cm, ta, tb):
        qc = jnp.concatenate(
            [part for h in range(HEADS)
             for part in (_head(qp, h, MLA_QK_PAD), _rope_m(_head(qp, h, MLA_QK_PAD, HEAD_DIM), cm, ta, tb))], axis=1)
        kc = jnp.concatenate([part for h in range(HEADS) for part in (_head(kn, h), krr)], axis=1)
        return [qc, kc], []

    (qc, kc), _ = _rowwise("build_qk", build_qk, [qp, kn, krr, cos_m, sa, sb], [], [(qw, BF16), (qw, BF16)], [], tm)
    o_m, lse = attn_fwd("mla_fwd", qc, kc, vv, blk, True, MLA_SCALE, host=plan.host("mla_fwd"))
    plan.done("mla_fwd", w)
    (o_r,) = attn_fwd("ret_fwd", rq, rk, rv, blk, False)

    def gate_mix(r0, rg, o_r, o_m, gn):
        y, _ = _group_norm(o_r)
        return [jnp.concatenate([_silu(rg) * (y * gn), o_m], axis=1)], []

    (mixcat,), _ = _rowwise("gate_mix", gate_mix, [(proj_r, hw, 3), o_r, o_m], [nw["ret_group_norm"]],
                            [(2 * hw, BF16)], [], tm)
    (mix,) = mm_nn("mix_out", mixcat, w["w_out"], tm, n_block=dt)

    def post_mix(r0, h, m, post, pre):
        h2 = h + _rms(m, post)
        return [h2, _rms(h2, pre)], []

    (h2, n3), _ = _rowwise("ffn2_norm", post_mix, [h1, mix], [nw["mix_post_norm"], nw["ffn2_pre_norm"]],
                           [(d, F32), (d, BF16)], [], tm)
    g2, u2, a2, f2 = _ffn_fwd("ffn2", n3, w, "2", tm, tmk, plan)

    def loss_head(r0, h, f, t, post):
        h3 = h + 0.5 * _rms(f, post)
        row = r0 + lax.broadcasted_iota(jnp.int32, h3.shape, 0)
        err = jnp.where(row >= N_META, jnp.where(row < t_real, h3 - t, 0.0), 0.0)
        dh3 = err / d
        df, dpost = _rms_bwd(f, post, 0.5 * dh3)
        return [dh3, df], [_colsum(err * err), _colsum(dpost)]

    (dh3, df2), (loss_vec, d_post2) = _rowwise("loss_head", loss_head, [h2, f2, tgt_p], [nw["ffn2_post_norm"]],
                                               [(d, F32), (d, BF16)], [d, d], tm)
    loss = 0.5 * jnp.sum(loss_vec) / d
    dn3 = _ffn_bwd("ffn2", df2, n3, g2, u2, a2, w["wg2"], w["wu2"], w["wd2"], tm, tmk, tk, dt, plan)

    def back_mix_norm(r0, h, m, dh3, dn, pre, post):
        dx, dpre = _rms_bwd(h, pre, dn)
        dh2 = dh3 + dx
        dm, dpost = _rms_bwd(m, post, dh2)
        return [dh2, dm], [_colsum(dpre), _colsum(dpost)]

    (dh2, dmix), (d_pre2, d_mix_post) = _rowwise(
        "back_mix_norm", back_mix_norm, [h2, mix, dh3, dn3], [nw["ffn2_pre_norm"], nw["mix_post_norm"]],
        [(d, F32), (d, BF16)], [d, d], tm)
    (dmixcat,) = mm_nt("mix_dx", [dmix], [w["w_out"]], tm, _tile(2 * hw, 512))
    dw_out = mm_tn("mix_dw", mixcat, dmix, 2 * hw // N_CHIPS, dt, tk, GRAD_DT, shard_rows=True)

    def back_gate(r0, dmc, rg, o_r, gn):
        d_ret, d_om = dmc[:, :hw], dmc[:, hw:]
        yh, rs = _group_norm(o_r)
        d_rg = d_ret * (yh * gn) * _dsilu(rg)
        dy = d_ret * _silu(rg)
        gyh = dy * gn
        d_or = _heads(lambda h: rs[h] * (_head(gyh, h) - _head(yh, h) * jnp.mean(_head(gyh, h) * _head(yh, h),
                                                                                    axis=-1, keepdims=True)))
        return [d_or, d_rg, d_om], [_colsum(dy * yh)]

    (d_or, d_rg, d_om), (d_gn,) = _rowwise("back_gate", back_gate, [dmixcat, (proj_r, hw, 3), o_r],
                                           [nw["ret_group_norm"]], [(hw, BF16), (hw, F32), (hw, BF16)], [hw], tm)
    dqc, dkc, dvv = attn_bwd("mla_bwd", qc, kc, vv, d_om, blk, True, MLA_SCALE, o_m, lse, host=plan.host("mla_bwd"))
    plan.done("mla_bwd", None)
    drq, drk, drv = attn_bwd("ret_bwd", rq, rk, rv, d_or, blk, False)

    def back_qk(r0, dqc, dkc, dvv, cm, ta, tb):
        dqp = jnp.concatenate(
            [part for h in range(HEADS)
             for part in (_head(dqc, h, MLA_QK_PAD), _rope_m_t(_head(dqc, h, MLA_QK_PAD, HEAD_DIM), cm, ta, tb))],
            axis=1)
        dkn = _heads(lambda h: _head(dkc, h, MLA_QK_PAD))
        dkr = _head(dkc, 0, MLA_QK_PAD, HEAD_DIM)
        for h in range(1, HEADS):
            dkr = dkr + _head(dkc, h, MLA_QK_PAD, HEAD_DIM)
        return [dqp, dkn, _rope_m_t(dkr, cm, ta, tb), dvv], []

    (dqp, dkn, dkr, dvb), _ = _rowwise("back_qk", back_qk, [dqc, dkc, dvv, cos_m, sa, sb], [],
                                       [(qw, BF16), (hw, BF16), (LANES, F32), (hw, BF16)], [], tm)
    (dcqn,) = mm_nt("q_dx", [dqp], [w["wuq"]], tm, MLA_Q_RANK)
    dwuq = mm_tn("q_dw", cqn, dqp, MLA_Q_RANK, _tile(qw, 1024), tk, GRAD_DT)
    (dckvn,) = mm_nt("kv_dx", [dkn, dvb], [w["wuk"], w["wuv"]], tm, MLA_KV_RANK)
    dwuk = mm_tn("k_dw", ckvn, dkn, MLA_KV_RANK, hw, tk, GRAD_DT)
    dwuv = mm_tn("v_dw", ckvn, dvb, MLA_KV_RANK, hw, tk, GRAD_DT)

    def back_proj(r0, drq, drk, drv, d_rg, pc, dcqn, dckvn, dkr, cr, sr, qn, kvn):
        d_q = _heads(lambda h: _rope_r_t(_head(drq, h), cr, sr))
        d_k = _heads(lambda h: _rope_r_t(_head(drk, h), cr, sr) * RET_SCALE)
        dcq, a_q = _rms_bwd(pc[:, :MLA_Q_RANK], qn, dcqn)
        dckv, a_kv = _rms_bwd(pc[:, MLA_Q_RANK:MLA_Q_RANK + MLA_KV_RANK], kvn, dckvn)
        return ([jnp.concatenate([d_q, d_k, drv, d_rg], axis=1), jnp.concatenate([dcq, dckv, dkr], axis=1)],
                [_colsum(a_q), _colsum(a_kv)])

    (dproj_r, dproj_c), (d_qn, d_kvn) = _rowwise(
        "back_proj", back_proj, [drq, drk, drv, d_rg, proj_c, dcqn, dckvn, dkr, cos_r, sin_r],
        [nw["mla_q_norm"], nw["mla_kv_norm"]], [(IN_RET, BF16), (IN_MLA_PAD, BF16)],
        [MLA_Q_RANK, MLA_KV_RANK], tm)
    (dun,) = mm_nt("proj_dx", [dproj_r, dproj_c], [w["w_r"], w["w_c"]], tm, _tile(d, 512))
    dw_r = mm_tn("proj_dw_r", un, dproj_r, dt, _tile(IN_RET, 1024), tk, GRAD_DT)
    dw_c = mm_tn("proj_dw_c", un, dproj_c, dt, IN_MLA_PAD, tk, GRAD_DT)
    plan.grads(dict(w_r=dw_r, w_c=dw_c, wuq=dwuq, wuk=dwuk, wuv=dwuv, w_out=dw_out))

    def back_ffn1_norm(r0, h, f, dh2, dn, pre, post):
        dx, dpre = _rms_bwd(h, pre, dn)
        dh1 = dh2 + dx
        df, dpost = _rms_bwd(f, post, 0.5 * dh1)
        return [dh1, df], [_colsum(dpre), _colsum(dpost)]

    (dh1, df1), (d_mix_pre, d_post1) = _rowwise(
        "back_ffn1_norm", back_ffn1_norm, [h1, f1, dh2, dun], [nw["mix_pre_norm"], nw["ffn1_post_norm"]],
        [(d, F32), (d, BF16)], [d, d], tm)
    dn1 = _ffn_bwd("ffn1", df1, n1, g1, u1, a1, w["wg1"], w["wu1"], w["wd1"], tm, tmk, tk, dt, plan)

    def back_input(r0, h, dh1, dn, pre):
        dx, dpre = _rms_bwd(h, pre, dn)
        return [dh1 + dx], [_colsum(dpre)]

    (dh0,), (d_pre1,) = _rowwise("back_input", back_input, [h0, dh1, dn1], [nw["ffn1_pre_norm"]], [(d, F32)], [d], tm)

    small = dict(ffn1_pre_norm=d_pre1, ffn1_post_norm=d_post1, mix_pre_norm=d_mix_pre, ret_group_norm=d_gn,
                 mla_q_norm=d_qn, mla_kv_norm=d_kvn, mix_post_norm=d_mix_post, ffn2_pre_norm=d_pre2,
                 ffn2_post_norm=d_post2)
    return loss, dh0[N_META:t_real], small, dh0[:N_META]


WEIGHTS = ("meta_tokens", "ffn1_pre_norm", "ffn1_w_gate", "ffn1_w_up", "ffn1_w_down", "ffn1_post_norm",
           "mix_pre_norm", "w_in", "ret_group_norm", "mla_q_norm", "mla_w_uq", "mla_kv_norm", "mla_w_uk",
           "mla_w_uv", "w_out", "mix_post_norm", "ffn2_pre_norm", "ffn2_w_gate", "ffn2_w_up", "ffn2_w_down",
           "ffn2_post_norm")
BIG = ("ffn1_w_gate", "ffn1_w_up", "ffn1_w_down", "w_in", "mla_w_uq", "mla_w_uk", "mla_w_uv", "w_out",
       "ffn2_w_gate", "ffn2_w_up", "ffn2_w_down")
NORMS = ("ffn1_pre_norm", "ffn1_post_norm", "mix_pre_norm", "ret_group_norm", "mla_q_norm", "mla_kv_norm",
         "mix_post_norm", "ffn2_pre_norm", "ffn2_post_norm")


def _unshard_cols(g):
    return g.transpose(1, 0, 2).reshape(g.shape[1], -1)


def _shard_cols(a):
    return a.reshape(a.shape[0], N_CHIPS, -1).transpose(1, 0, 2)


def _pack_rows(rows, width):
    rows = [jnp.pad(r, ((0, 0), (0, width - r.shape[1]))) for r in rows]
    n = sum(r.shape[0] for r in rows)
    return jnp.pad(jnp.concatenate(rows, axis=0), ((0, -n % 8), (0, 0)))


def _weight_views(full):
    w = {}
    for n, g in full.items():
        if n == "w_in":
            w_in = _unshard_cols(g)
            w["w_r"] = w_in[:, :IN_RET]
            w["w_c"] = jnp.pad(w_in[:, IN_RET:], ((0, 0), (0, IN_MLA_PAD - IN_MLA)))
        elif n == "mla_w_uq":
            q = _unshard_cols(g).reshape(MLA_Q_RANK, HEADS, HEAD_DIM + MLA_ROPE)
            q = jnp.pad(q, ((0, 0), (0, 0), (0, MLA_QK_PAD - HEAD_DIM - MLA_ROPE)))
            w["wuq"] = q.reshape(MLA_Q_RANK, HEADS * MLA_QK_PAD)
        elif n in ("mla_w_uk", "mla_w_uv"):
            w["wu" + n[-1]] = _unshard_cols(g)
        elif n == "w_out":
            w["w_out"] = g.reshape(-1, g.shape[2])
        else:
            w["w" + n[7] + n[3]] = g
    return w


def _contributions(g):
    ffn = {"g": "gate", "u": "up", "d": "down"}
    c = {f"ffn{n[2]}_w_{ffn[n[1]]}": a for n, a in g.items() if len(n) == 3 and n[2] in "12"}
    if "w_r" in g:
        dwuq = g["wuq"].reshape(MLA_Q_RANK, HEADS, MLA_QK_PAD)[:, :, :HEAD_DIM + MLA_ROPE]
        c.update(w_in=_shard_cols(jnp.concatenate([g["w_r"], g["w_c"][:, :IN_MLA]], axis=1)),
                 mla_w_uq=_shard_cols(dwuq.reshape(MLA_Q_RANK, -1)), mla_w_uk=_shard_cols(g["wuk"]),
                 mla_w_uv=_shard_cols(g["wuv"]), w_out=g["w_out"])
    return c


class _Schedule(_Alone):
    FIRST = ("ffn1_w_gate", "ffn1_w_up")
    CARRIED = {"ffn1_up": ("ffn1_w_down", "w_in", "mla_w_uq", "mla_w_uk", "mla_w_uv"),
               "ffn1_down": ("w_out", "ffn2_w_down"), "mla_fwd": ("ffn2_w_gate", "ffn2_w_up")}
    GRAD_HOST = dict(ffn2_w_gate="mla_bwd", ffn2_w_up="mla_bwd", ffn2_w_down="mla_bwd",
                     w_in="ffn1_da", mla_w_uq="ffn1_da", mla_w_uk="ffn1_da", mla_w_uv="ffn1_da", w_out="ffn1_da",
                     ffn1_w_gate="ffn1_dwu", ffn1_w_up="ffn1_dwd", ffn1_w_down="ffn1_dn")

    def __init__(self, shards):
        self.gathers = {k: (gather_halves([shards[n] for n in names]), names) for k, names in self.CARRIED.items()}
        self.waiting = {}
        self.exchanges = {}
        self.grad = {}

    def host(self, kernel_name):
        if kernel_name in self.gathers:
            return self.gathers[kernel_name][0]
        if kernel_name in self.waiting:
            names, sums = zip(*self.waiting.pop(kernel_name))
            self.exchanges[kernel_name] = (chip_exchange(list(sums)), names)
            return self.exchanges[kernel_name][0]
        return None

    def done(self, kernel_name, w):
        if kernel_name in self.gathers:
            comm, names = self.gathers[kernel_name]
            w.update(_weight_views({n: sibling_fill("fill_" + n, b) for n, b in zip(names, comm.result)}))
        elif kernel_name in self.exchanges:
            comm, names = self.exchanges[kernel_name]
            for n, q in zip(names, comm.result):
                self.grad[n] = reduce_join("reduce_join_" + n, q)

    def grads(self, g):
        for n, a in _contributions(g).items():
            self.waiting.setdefault(self.GRAD_HOST[n], []).append((n, pair_add("pair_add_" + n, a)))


def _step(p, m, v, x, loss_target):
    d = x.shape[2]
    names = list(BIG)
    shards = {n: p[n][0].astype(BF16) for n in names}
    first = _Schedule.FIRST
    gathered = gather_halves([shards[n] for n in first] + [p["meta_tokens"]]).run("gather_first")
    filled = [sibling_fill("fill_" + n, b) for n, b in zip(first + ("meta_tokens",), gathered)]
    w = _weight_views(dict(zip(first, filled[:-1])))
    meta = _unshard_cols(filled[-1])
    nw = {n: p[n] for n in NORMS}
    plan = _Schedule(shards)
    loss, grad_x, small, d_meta = _local_step(x[0], loss_target[0], meta, w, nw, plan)
    grads = dict(plan.grad)

    width = max(d, HEADS * HEAD_DIM)
    packed = _pack_rows([small[n] for n in NORMS] + [d_meta], width)
    total = sum_slots("small_sum", all_gather_devices(packed), F32)
    for i, n in enumerate(NORMS):
        grads[n] = total[i:i + 1, :p[n].shape[1]]
    cols = p["meta_tokens"].shape[1]
    chip = 2 * lax.axis_index("x") + lax.axis_index("y")
    grads["meta_tokens"] = lax.dynamic_slice(total[len(NORMS):len(NORMS) + N_META, :d], (0, chip * cols), (N_META, cols))

    delta, new_m, new_v = {}, {}, {}
    for n in names + ["meta_tokens"]:
        shape = p[n].shape
        flat = lambda a: a.reshape(-1, shape[-1])
        grads[n] = grads[n].reshape(shape)
        out = adamw("adamw_" + n, flat(p[n]), flat(grads[n]), flat(m[n]), flat(v[n]))
        delta[n], new_m[n], new_v[n] = (o.reshape(shape) for o in out)
    pk = lambda src: _pack_rows([src[n] for n in NORMS], width)
    out = adamw("adamw_norms", pk(p), pk(grads), pk(m), pk(v))
    for i, n in enumerate(NORMS):
        delta[n], new_m[n], new_v[n] = (o[i:i + 1, :p[n].shape[1]] for o in out)

    loss = lax.psum(loss, ("x", "y", "c"))
    return loss, grad_x[None], grads, delta, new_m, new_v


def kernel(x, meta_tokens, ffn1_pre_norm, ffn1_w_gate, ffn1_w_up, ffn1_w_down, ffn1_post_norm, mix_pre_norm, w_in, ret_group_norm, mla_q_norm, mla_w_uq, mla_kv_norm, mla_w_uk, mla_w_uv, w_out, mix_post_norm, ffn2_pre_norm, ffn2_w_gate, ffn2_w_up, ffn2_w_down, ffn2_post_norm, loss_target, m_meta_tokens, m_ffn1_pre_norm, m_ffn1_w_gate, m_ffn1_w_up, m_ffn1_w_down, m_ffn1_post_norm, m_mix_pre_norm, m_w_in, m_ret_group_norm, m_mla_q_norm, m_mla_w_uq, m_mla_kv_norm, m_mla_w_uk, m_mla_w_uv, m_w_out, m_mix_post_norm, m_ffn2_pre_norm, m_ffn2_w_gate, m_ffn2_w_up, m_ffn2_w_down, m_ffn2_post_norm, v_meta_tokens, v_ffn1_pre_norm, v_ffn1_w_gate, v_ffn1_w_up, v_ffn1_w_down, v_ffn1_post_norm, v_mix_pre_norm, v_w_in, v_ret_group_norm, v_mla_q_norm, v_mla_w_uq, v_mla_kv_norm, v_mla_w_uk, v_mla_w_uv, v_w_out, v_mix_post_norm, v_ffn2_pre_norm, v_ffn2_w_gate, v_ffn2_w_up, v_ffn2_w_down, v_ffn2_post_norm):
    args = locals()
    p = {n: args[n] for n in WEIGHTS}
    m = {n: args["m_" + n] for n in WEIGHTS}
    v = {n: args["v_" + n] for n in WEIGHTS}
    loss, grad_x, grads, delta, new_m, new_v = _step(p, m, v, x, loss_target)
    return (loss, grad_x, *[grads[n] for n in WEIGHTS], *[delta[n] for n in WEIGHTS],
            *[new_m[n] for n in WEIGHTS], *[new_v[n] for n in WEIGHTS])
```

```python
import functools
import math

import jax
import jax.numpy as jnp
from jax import lax
from jax.experimental import pallas as pl
from jax.experimental.pallas import tpu as pltpu

F32 = jnp.float32
BF16 = jnp.bfloat16

EPS = 1e-6
N_META = 16
HEADS = 8
HEAD_DIM = 128
MLA_ROPE = 64
MLA_QK_PAD = 256
MLA_Q_RANK = 512
MLA_KV_RANK = 256
ROPE_THETA = 10000.0
N_CHIPS = 4
LANES = 128
VMEM_LIMIT = 60 * 2 ** 20

ADAM_LR = 0.001
ADAM_B1 = 0.9
ADAM_B2 = 0.999
ADAM_EPS = 1e-08
ADAM_WD = 0.01
ADAM_STEP = 10

NN = (((1,), (0,)), ((), ()))
NT = (((1,), (1,)), ((), ()))
TN = (((0,), (0,)), ((), ()))
MESH = pl.DeviceIdType.MESH


def _tile(n, pref, align=LANES):
    if n <= pref:
        return n
    best = 0
    for t in range(align, pref + 1, align):
        if n % t == 0:
            best = t
    assert best, (n, pref)
    return best


def _params(*sem):
    return pltpu.CompilerParams(dimension_semantics=sem, vmem_limit_bytes=VMEM_LIMIT)


def _sds(shape, dtype):
    return jax.ShapeDtypeStruct(tuple(shape), dtype)


def _rowwise(name, fn, rows, consts, outs, accs, tr):
    rows = [r if isinstance(r, tuple) else (r, r.shape[1], 0) for r in rows]
    t = rows[0][0].shape[0]
    assert t % tr == 0
    n_r, n_c, n_o = len(rows), len(consts), len(outs)

    def body(*refs):
        i = pl.program_id(0)
        r = [x[...] for x in refs[:n_r]]
        c = [x[...] for x in refs[n_r:n_r + n_c]]
        o_refs = refs[n_r + n_c:n_r + n_c + n_o]
        a_refs = refs[n_r + n_c + n_o:]
        o_vals, a_vals = fn(i * tr, *r, *c)
        for ref, v in zip(o_refs, o_vals):
            ref[...] = v.astype(ref.dtype)
        if a_refs:
            @pl.when(i == 0)
            def _():
                for ref, v in zip(a_refs, a_vals):
                    ref[...] = v

            @pl.when(i > 0)
            def _():
                for ref, v in zip(a_refs, a_vals):
                    ref[...] += v

    in_specs = [pl.BlockSpec((tr, w), functools.partial(lambda cb, i: (i, cb), cb)) for _, w, cb in rows]
    in_specs += [pl.BlockSpec(a.shape, lambda i: (0, 0)) for a in consts]
    out_specs = [pl.BlockSpec((tr, w), lambda i: (i, 0)) for w, _ in outs]
    out_specs += [pl.BlockSpec((1, w), lambda i: (0, 0)) for w in accs]
    out_shape = [_sds((t, w), dt) for w, dt in outs] + [_sds((1, w), F32) for w in accs]
    res = pl.pallas_call(
        body, name=name, grid=(t // tr,), in_specs=in_specs, out_specs=out_specs, out_shape=out_shape,
        compiler_params=_params("arbitrary"),
    )(*[a for a, _, _ in rows], *consts)
    return res[:n_o], res[n_o:]


def _rms(x, w):
    r = lax.rsqrt(jnp.mean(x * x, axis=-1, keepdims=True) + EPS)
    return x * r * w


def _rms_bwd(x, w, dy):
    r = lax.rsqrt(jnp.mean(x * x, axis=-1, keepdims=True) + EPS)
    xh = x * r
    gy = dy * w
    dx = r * (gy - xh * jnp.mean(gy * xh, axis=-1, keepdims=True))
    return dx, dy * xh


def _colsum(v):
    return jnp.sum(v, axis=0, keepdims=True)


def _silu(x):
    return x * jax.nn.sigmoid(x)


def _dsilu(x):
    s = jax.nn.sigmoid(x)
    return s * (1.0 + x * (1.0 - s))


def _rope_r(x, cos, sin):
    return x * cos + pltpu.roll(x, 64, 1) * sin


def _rope_r_t(dy, cos, sin):
    return dy * cos + pltpu.roll(dy * sin, 64, 1)


def _rope_m(x, cos, sa, sb):
    return x * cos + pltpu.roll(x, 32, 1) * sa + pltpu.roll(x, 96, 1) * sb


def _rope_m_t(dy, cos, sa, sb):
    return dy * cos + pltpu.roll(dy * sa, 96, 1) + pltpu.roll(dy * sb, 32, 1)


def _rope_tables(t):
    pos = jnp.arange(t, dtype=F32)

    def cs(dim):
        inv = ROPE_THETA ** (-jnp.arange(0, dim, 2, dtype=F32) / dim)
        ang = pos[:, None] * inv[None, :]
        return jnp.cos(ang), jnp.sin(ang)

    c, s = cs(HEAD_DIM)
    cos_r = jnp.concatenate([c, c], axis=1)
    sin_r = jnp.concatenate([-s, s], axis=1)
    c, s = cs(MLA_ROPE)
    z32, z64 = jnp.zeros_like(s), jnp.zeros((t, 64), F32)
    cos_m = jnp.concatenate([c, c, z64], axis=1)
    sa = jnp.concatenate([z32, s, z64], axis=1)
    sb = jnp.concatenate([-s, z32, z64], axis=1)
    return cos_r, sin_r, cos_m, sa, sb


def _call(body, name, grid, in_specs, out_specs, out_shape, scratch, operands, host=None):
    sem = ("arbitrary",) * len(grid)
    if host is None:
        return pl.pallas_call(body, name=name, grid=grid, in_specs=in_specs, out_specs=out_specs, out_shape=out_shape,
                              scratch_shapes=scratch, compiler_params=_params(*sem))(*operands)
    n_in, n_out, n_s = len(in_specs), len(out_shape), len(scratch)
    h_in, h_out = len(host.ins), len(host.out_shape)

    def hosted(*refs):
        a = n_in
        b = a + h_in
        c = b + n_out
        d = c + h_out
        e = d + n_s
        ids = [pl.program_id(i) for i in range(len(grid))]
        first = functools.reduce(jnp.logical_and, [i == 0 for i in ids])
        last = functools.reduce(jnp.logical_and, [i == g - 1 for i, g in zip(ids, grid)])
        comm = (refs[a:b], refs[c:d], refs[e:])

        @pl.when(first)
        def _():
            host.start(*comm)

        body(*refs[:a], *refs[b:c], *refs[d:e])

        @pl.when(last)
        def _():
            host.finish(*comm)

    hbm = pl.BlockSpec(memory_space=pl.ANY)
    res = pl.pallas_call(
        hosted, name=name, grid=grid, in_specs=list(in_specs) + [hbm] * h_in, out_specs=list(out_specs) + [hbm] * h_out,
        out_shape=list(out_shape) + list(host.out_shape), scratch_shapes=list(scratch) + host.scratch(),
        compiler_params=_params(*sem))(*operands, *host.ins)
    host.result = res[n_out:]
    return res[:n_out]


def _mm(name, grid, operands, in_specs, dns, out_specs, out_shape, epilogue=None, extras=(), extra_specs=(),
        acc_shape=None, host=None):
    n_p, n_e = len(dns), len(extras)
    nk = grid[2]
    n_o = len(out_shape)

    def body(*refs):
        ab = refs[:2 * n_p]
        ex = refs[2 * n_p:2 * n_p + n_e]
        outs = refs[2 * n_p + n_e:2 * n_p + n_e + n_o]
        part = None
        for p in range(n_p):
            d = lax.dot_general(ab[2 * p][...], ab[2 * p + 1][...], dns[p], preferred_element_type=F32)
            part = d if part is None else part + d

        def finish(acc):
            vals = (acc,) if epilogue is None else epilogue(acc, *[e[...] for e in ex])
            for o, v in zip(outs, vals):
                o[...] = v.astype(o.dtype)

        if nk == 1:
            finish(part)
        else:
            acc_ref = refs[2 * n_p + n_e + n_o]
            k = pl.program_id(2)

            @pl.when(k == 0)
            def _():
                acc_ref[...] = part

            @pl.when(k > 0)
            def _():
                acc_ref[...] += part

            @pl.when(k == nk - 1)
            def _():
                finish(acc_ref[...])

    scratch = [] if nk == 1 else [pltpu.VMEM(acc_shape, F32)]
    return _call(body, name, grid, list(in_specs) + list(extra_specs), out_specs, out_shape, scratch,
                 [*operands, *extras], host)


def mm_nn(name, x, w, tm, out_dtype=F32, epilogue=None, outs=None, w2=None, n_block=None, host=None):
    t, kdim = x.shape
    if w.ndim == 3:
        s, _, ns = w.shape
        n, tn, nb = s * ns, ns, s
        wspec = pl.BlockSpec((None, kdim, ns), lambda j, i, k: (j, 0, 0))
    else:
        n = w.shape[1]
        tn = n_block or n
        nb = n // tn
        wspec = pl.BlockSpec((kdim, tn), lambda j, i, k: (0, j))
    xspec = pl.BlockSpec((tm, kdim), lambda j, i, k: (i, 0))
    ospec = pl.BlockSpec((tm, tn), lambda j, i, k: (i, j))
    outs = outs or [out_dtype]
    grid = (nb, t // tm, 1)
    if w2 is None:
        return _mm(name, grid, [x, w], [xspec, wspec], [NN], [ospec] * len(outs), [_sds((t, n), d) for d in outs],
                   epilogue=epilogue, host=host)

    def body(x_ref, w_ref, w2_ref, *o_refs):
        xv = x_ref[...]
        a = jnp.dot(xv, w_ref[...], preferred_element_type=F32)
        b = jnp.dot(xv, w2_ref[...], preferred_element_type=F32)
        for o, v in zip(o_refs, epilogue(a, b)):
            o[...] = v.astype(o.dtype)

    return _call(body, name, grid[:2],
                 [pl.BlockSpec((tm, kdim), lambda j, i: (i, 0)),
                  pl.BlockSpec((None, kdim, tn), lambda j, i: (j, 0, 0)),
                  pl.BlockSpec((None, kdim, tn), lambda j, i: (j, 0, 0))],
                 [pl.BlockSpec((tm, tn), lambda j, i: (i, j))] * len(outs), [_sds((t, n), d) for d in outs], [],
                 [x, w, w2], host)


def mm_nn_k(name, x, w, tm, tk, out_dtype=F32, host=None):
    t, kdim = x.shape
    n = w.shape[1]
    grid = (t // tm, 1, kdim // tk)
    return _mm(name, grid, [x, w],
               [pl.BlockSpec((tm, tk), lambda i, j, k: (i, k)), pl.BlockSpec((tk, n), lambda i, j, k: (k, 0))],
               [NN], [pl.BlockSpec((tm, n), lambda i, j, k: (i, 0))], [_sds((t, n), out_dtype)],
               acc_shape=(tm, n), host=host)[0]


def mm_nt(name, xs, ws, tm, tn, outs=(F32,), epilogue=None, extras=(), host=None):
    t = xs[0].shape[0]
    specs, ops = [], []
    for x, w in zip(xs, ws):
        kdim = x.shape[1]
        specs.append(pl.BlockSpec((tm, kdim), lambda j, i, k: (i, 0)))
        if w.ndim == 3:
            assert tn == w.shape[1]
            n = w.shape[0] * w.shape[1]
            specs.append(pl.BlockSpec((None, tn, kdim), lambda j, i, k: (j, 0, 0)))
        else:
            n = w.shape[0]
            specs.append(pl.BlockSpec((tn, kdim), lambda j, i, k: (j, 0)))
        ops += [x, w]
    ospec = pl.BlockSpec((tm, tn), lambda j, i, k: (i, j))
    return _mm(name, (n // tn, t // tm, 1), ops, specs, [NT] * len(xs), [ospec] * len(outs),
               [_sds((t, n), d) for d in outs], epilogue=epilogue, extras=extras,
               extra_specs=[ospec] * len(extras), host=host)


def mm_nt_k(name, xs, ws, tm, tn, out_dtype=F32, host=None):
    t = xs[0].shape[0]
    s, n, ns = ws[0].shape
    specs, ops = [], []
    for x, w in zip(xs, ws):
        specs.append(pl.BlockSpec((tm, ns), lambda i, j, k: (i, k)))
        specs.append(pl.BlockSpec((None, tn, ns), lambda i, j, k: (k, j, 0)))
        ops += [x, w]
    return _mm(name, (t // tm, n // tn, s), ops, specs, [NT] * len(xs),
               [pl.BlockSpec((tm, tn), lambda i, j, k: (i, j))], [_sds((t, n), out_dtype)], acc_shape=(tm, tn),
               host=host)[0]


def mm_tn(name, x, y, tm, tn, tk, out_dtype, shard_rows=False, shard_cols=False, host=None):
    t, m = x.shape
    n = y.shape[1]
    grid = (m // tm, n // tn, t // tk)
    if shard_cols:
        ospec = pl.BlockSpec((None, tm, tn), lambda i, j, k: (j, i, 0))
        oshape = _sds((n // tn, m, tn), out_dtype)
    elif shard_rows:
        ospec = pl.BlockSpec((None, tm, tn), lambda i, j, k: (i, 0, j))
        oshape = _sds((m // tm, tm, n), out_dtype)
    else:
        ospec = pl.BlockSpec((tm, tn), lambda i, j, k: (i, j))
        oshape = _sds((m, n), out_dtype)
    return _mm(name, grid, [x, y],
               [pl.BlockSpec((tk, tm), lambda i, j, k: (k, i)), pl.BlockSpec((tk, tn), lambda i, j, k: (k, j))],
               [TN], [ospec], [oshape], acc_shape=(tm, tn), host=host)[0]


def _decay_logs():
    return [math.log(1.0 - 2.0 ** (-5.0 - h)) for h in range(HEADS)]


def _log_decay(h):
    lg = jnp.float32(_decay_logs()[0])
    for i in range(1, HEADS):
        lg = jnp.where(h == i, jnp.float32(_decay_logs()[i]), lg)
    return lg


def _scores(q, k, scale, qi, ki, blk, softmax, lg, diag):
    s = lax.dot_general(q, k, NT, preferred_element_type=F32)
    row = lax.broadcasted_iota(jnp.int32, s.shape, 0)
    col = lax.broadcasted_iota(jnp.int32, s.shape, 1)
    if softmax:
        s = s * scale
        if diag:
            s = jnp.where(col <= row, s, -1e30)
        return s
    dist = ((qi - ki) * blk + row - col).astype(F32)
    dec = jnp.exp(jnp.maximum(dist, 0.0) * lg)
    if diag:
        dec = jnp.where(col <= row, dec, 0.0)
    return s * dec, dec


def attn_fwd(name, q, k, v, blk, softmax, scale=1.0, host=None):
    t = q.shape[0]
    dq = q.shape[1] // HEADS
    nq = t // blk

    def body(q_ref, k_ref, v_ref, o_ref, *lse_ref):
        h, qi = pl.program_id(0), pl.program_id(1)
        lg = _log_decay(h)
        qv = q_ref[...]

        def block(ki, carry, diag):
            start = pl.multiple_of(ki * blk, blk)
            kv = k_ref[pl.ds(start, blk), :]
            vv = v_ref[pl.ds(start, blk), :]
            if softmax:
                m, l, acc = carry
                s = _scores(qv, kv, scale, qi, ki, blk, True, lg, diag)
                m_new = jnp.maximum(m, jnp.max(s, axis=-1, keepdims=True))
                p = jnp.exp(s - m_new)
                alpha = jnp.exp(m - m_new)
                l = alpha * l + jnp.sum(p, axis=-1, keepdims=True)
                acc = alpha * acc + jnp.dot(p.astype(BF16), vv, preferred_element_type=F32)
                return m_new, l, acc
            p, _ = _scores(qv, kv, scale, qi, ki, blk, False, lg, diag)
            return (carry[0] + jnp.dot(p.astype(BF16), vv, preferred_element_type=F32),)

        if softmax:
            init = (jnp.full((blk, 1), -1e30, F32), jnp.zeros((blk, 1), F32), jnp.zeros((blk, HEAD_DIM), F32))
        else:
            init = (jnp.zeros((blk, HEAD_DIM), F32),)
        carry = lax.fori_loop(0, qi, lambda ki, c: block(ki, c, False), init)
        carry = block(qi, carry, True)
        if softmax:
            m, l, acc = carry
            o_ref[...] = acc / l
            lse_ref[0][...] = jnp.broadcast_to(m + jnp.log(l), (blk, HEAD_DIM))
        else:
            o_ref[...] = carry[0]

    hspec = pl.BlockSpec((blk, HEAD_DIM), lambda h, i: (i, h))
    out_specs = [hspec, hspec] if softmax else [hspec]
    out_shape = [_sds((t, HEADS * HEAD_DIM), F32)] * (2 if softmax else 1)
    return _call(body, name, (HEADS, nq),
                 [pl.BlockSpec((blk, dq), lambda h, i: (i, h)),
                  pl.BlockSpec((t, dq), lambda h, i: (0, h)),
                  pl.BlockSpec((t, HEAD_DIM), lambda h, i: (0, h))],
                 out_specs, out_shape, [], [q, k, v], host)


def attn_bwd(name, q, k, v, do, blk, softmax, scale=1.0, o=None, lse=None, host=None):
    t = q.shape[0]
    dq_w = q.shape[1] // HEADS
    nb = t // blk

    def body(q_ref, k_ref, v_ref, do_ref, *rest):
        if softmax:
            o_ref, lse_ref, dq_ref, dk_ref, dv_ref = rest
        else:
            dq_ref, dk_ref, dv_ref = rest
        h, ki = pl.program_id(0), pl.program_id(1)
        lg = _log_decay(h)
        kv = k_ref[...]
        vv = v_ref[...]

        @pl.when(ki == 0)
        def _():
            dq_ref[...] = jnp.zeros_like(dq_ref)

        def block(qi, carry, diag):
            dk, dv = carry
            start = pl.multiple_of(qi * blk, blk)
            rows = pl.ds(start, blk)
            qv = q_ref[rows, :]
            dov = do_ref[rows, :]
            dp = lax.dot_general(dov, vv, NT, preferred_element_type=F32)
            if softmax:
                s = _scores(qv, kv, scale, qi, ki, blk, True, lg, diag)
                p = jnp.exp(s - lse_ref[rows, :][:, :1])
                delta = jnp.sum(dov.astype(F32) * o_ref[rows, :], axis=-1, keepdims=True)
                ds = p * (dp - delta) * scale
            else:
                p, dec = _scores(qv, kv, scale, qi, ki, blk, False, lg, diag)
                ds = dp * dec
            pb, dsb = p.astype(BF16), ds.astype(BF16)
            dv = dv + lax.dot_general(pb, dov, TN, preferred_element_type=F32)
            dk = dk + lax.dot_general(dsb, qv, TN, preferred_element_type=F32)
            dq_ref[rows, :] += jnp.dot(dsb, kv, preferred_element_type=F32)
            return dk, dv

        carry = block(ki, (jnp.zeros((blk, dq_w), F32), jnp.zeros((blk, HEAD_DIM), F32)), True)
        dk, dv = lax.fori_loop(ki + 1, nb, lambda qi, c: block(qi, c, False), carry)
        dk_ref[...] = dk
        dv_ref[...] = dv

    full = lambda w: pl.BlockSpec((t, w), lambda h, j: (0, h))
    blkd = lambda w: pl.BlockSpec((blk, w), lambda h, j: (j, h))
    ins = [q, k, v, do]
    in_specs = [full(dq_w), blkd(dq_w), blkd(HEAD_DIM), full(HEAD_DIM)]
    if softmax:
        ins += [o, lse]
        in_specs += [full(HEAD_DIM), full(HEAD_DIM)]
    return _call(body, name, (HEADS, nb), in_specs, [full(dq_w), blkd(dq_w), blkd(HEAD_DIM)],
                 [_sds(q.shape, F32), _sds(k.shape, F32), _sds(v.shape, F32)], [], ins, host)


def _chunk_decays(lg, blk):
    row = lax.broadcasted_iota(jnp.int32, (blk, HEAD_DIM), 0).astype(F32)
    return jnp.exp(lg * (row + 1.0)), jnp.exp(lg * (blk - 1.0 - row)), jnp.exp(lg * blk * jnp.ones((1, HEAD_DIM), F32))


def ret_fwd(name, q, k, v, blk, host=None):
    t = q.shape[0]
    nb = t // blk

    def body(q_ref, k_ref, v_ref, o_ref, st_ref, state):
        h, i = pl.program_id(0), pl.program_id(1)
        lg = _log_decay(h)

        @pl.when(i == 0)
        def _():
            state[...] = jnp.zeros_like(state)

        qv, kv, vv = q_ref[...], k_ref[...], v_ref[...]
        before = state[...]
        st_ref[...] = before
        p, _ = _scores(qv, kv, 1.0, 0, 0, blk, False, lg, True)
        xi, zeta, g_blk = _chunk_decays(lg, blk)
        o_ref[...] = (jnp.dot(p.astype(BF16), vv, preferred_element_type=F32)
                      + jnp.dot(qv, before.astype(BF16), preferred_element_type=F32) * xi)
        kz = (kv.astype(F32) * zeta).astype(BF16)
        state[...] = before * g_blk + lax.dot_general(kz, vv, TN, preferred_element_type=F32)

    hspec = pl.BlockSpec((blk, HEAD_DIM), lambda h, i: (i, h))
    return _call(body, name, (HEADS, nb), [hspec] * 3,
                 [hspec, pl.BlockSpec((HEAD_DIM, HEAD_DIM), lambda h, i: (i, h))],
                 [_sds((t, HEADS * HEAD_DIM), F32), _sds((nb * HEAD_DIM, HEADS * HEAD_DIM), F32)],
                 [pltpu.VMEM((HEAD_DIM, HEAD_DIM), F32)], [q, k, v], host)


def ret_bwd(name, q, k, v, do, states, blk, host=None):
    t = q.shape[0]
    nb = t // blk

    def body(q_ref, k_ref, v_ref, do_ref, st_ref, dq_ref, dk_ref, dv_ref, dstate):
        h, i = pl.program_id(0), pl.program_id(1)
        lg = _log_decay(h)

        @pl.when(i == 0)
        def _():
            dstate[...] = jnp.zeros_like(dstate)

        qv, kv, vv, dov = q_ref[...], k_ref[...], v_ref[...], do_ref[...]
        before = st_ref[...].astype(BF16)
        after_grad = dstate[...]
        p, dec = _scores(qv, kv, 1.0, 0, 0, blk, False, lg, True)
        ds = lax.dot_general(dov, vv, NT, preferred_element_type=F32) * dec
        pb, dsb = p.astype(BF16), ds.astype(BF16)
        xi, zeta, g_blk = _chunk_decays(lg, blk)
        dox = (dov.astype(F32) * xi).astype(BF16)
        kz = (kv.astype(F32) * zeta).astype(BF16)
        agb = after_grad.astype(BF16)
        dv_ref[...] = (lax.dot_general(pb, dov, TN, preferred_element_type=F32)
                       + jnp.dot(kz, agb, preferred_element_type=F32))
        dq_ref[...] = (jnp.dot(dsb, kv, preferred_element_type=F32)
                       + lax.dot_general(dox, before, NT, preferred_element_type=F32))
        dk_ref[...] = (lax.dot_general(dsb, qv, TN, preferred_element_type=F32)
                       + lax.dot_general(vv, agb, NT, preferred_element_type=F32) * zeta)
        dstate[...] = after_grad * g_blk + lax.dot_general(qv, dox, TN, preferred_element_type=F32)

    hspec = pl.BlockSpec((blk, HEAD_DIM), lambda h, i: (nb - 1 - i, h))
    return _call(body, name, (HEADS, nb),
                 [hspec] * 4 + [pl.BlockSpec((HEAD_DIM, HEAD_DIM), lambda h, i: (nb - 1 - i, h))],
                 [hspec] * 3, [_sds(q.shape, F32)] * 3, [pltpu.VMEM((HEAD_DIM, HEAD_DIM), F32)],
                 [q, k, v, do, states], host)


CHIP_FLIPS = ((1, 0), (0, 1), (1, 1))


def _position():
    return lax.axis_index("x"), lax.axis_index("y"), lax.axis_index("c")


class _Comm:
    def __init__(self, ins, out_shape, plan, n_remote, n_local):
        self.ins, self.out_shape, self.plan = list(ins), list(out_shape), plan
        self.n_remote, self.n_local = n_remote, n_local
        self.result = None

    def scratch(self):
        return [pltpu.SemaphoreType.DMA((self.n_remote,)), pltpu.SemaphoreType.DMA((self.n_remote,)),
                pltpu.SemaphoreType.DMA((self.n_local,))]

    def _copies(self, in_refs, out_refs, sems):
        send_sems, recv_sems, local_sems = sems
        pos = _position()
        p = self.plan(pos, in_refs, out_refs)

        def remote(k, src, dst, dev):
            return pltpu.make_async_remote_copy(src_ref=src, dst_ref=dst, send_sem=send_sems.at[k],
                                                recv_sem=recv_sems.at[k], device_id=dev, device_id_type=MESH)

        local = [pltpu.make_async_copy(s, d, local_sems.at[i]) for i, (s, d) in enumerate(p["local"])]
        out = [remote(k, s, d, dev) for k, (s, d, dev) in enumerate(p["sends"])]
        arrivals = [functools.partial(remote, k, d, d, pos) for k, d in enumerate(p["recvs"])]
        return local, out, arrivals

    def start(self, in_refs, out_refs, sems):
        local, out, _ = self._copies(in_refs, out_refs, sems)
        for cp in local + out:
            cp.start()

    def finish(self, in_refs, out_refs, sems):
        local, out, arrivals = self._copies(in_refs, out_refs, sems)
        for make in arrivals:
            make().wait_recv()
        for cp in out:
            cp.wait_send()
        for cp in local:
            cp.wait()

    def run(self, name):
        n_in, n_out = len(self.ins), len(self.out_shape)

        def body(*refs):
            comm = (refs[:n_in], refs[n_in:n_in + n_out], refs[n_in + n_out:])
            self.start(*comm)
            self.finish(*comm)

        hbm = pl.BlockSpec(memory_space=pl.ANY)
        self.result = pl.pallas_call(body, name=name, in_specs=[hbm] * n_in, out_specs=[hbm] * n_out,
                                     out_shape=self.out_shape, scratch_shapes=self.scratch())(*self.ins)
        return self.result


def _half_rows(c, rows):
    r2 = rows // 2
    return pl.ds(pl.multiple_of(c * r2, math.gcd(r2, LANES)), r2)


def _half(ref, c, rows, lead=()):
    return ref.at[(*lead, _half_rows(c, rows))]


def gather_halves(shards):
    def plan(pos, ins, outs):
        x, y, c = pos
        me = 2 * x + y
        p = dict(local=[], sends=[], recvs=[])
        for a, (src, dst) in enumerate(zip(ins, outs)):
            rows = shards[a].shape[0]
            p["local"].append((src, dst.at[me]))
            for fx, fy in CHIP_FLIPS:
                px, py = x ^ fx, y ^ fy
                p["sends"].append((_half(src, c, rows), _half(dst, c, rows, (me,)), (px, py, c)))
                p["recvs"].append(_half(dst, c, rows, (2 * px + py,)))
        return p

    return _Comm(shards, [_sds((N_CHIPS, *s.shape), s.dtype) for s in shards], plan,
                 n_remote=3 * len(shards), n_local=len(shards))


def chip_exchange(parts):
    def plan(pos, ins, outs):
        x, y, c = pos
        me = 2 * x + y
        p = dict(local=[], sends=[], recvs=[])
        for src, dst in zip(ins, outs):
            p["local"].append((src.at[me], dst.at[me]))
            for fx, fy in CHIP_FLIPS:
                px, py = x ^ fx, y ^ fy
                peer = 2 * px + py
                p["sends"].append((src.at[peer], dst.at[me], (px, py, c)))
                p["recvs"].append(dst.at[peer])
        return p

    return _Comm(parts, [_sds(g.shape, g.dtype) for g in parts], plan, n_remote=3 * len(parts), n_local=len(parts))


def all_gather_devices(v):
    flips = [(fx, fy, fc) for fx in (0, 1) for fy in (0, 1) for fc in (0, 1)][1:]

    def plan(pos, ins, outs):
        x, y, c = pos
        me = 4 * x + 2 * y + c
        p = dict(local=[(ins[0], outs[0].at[me])], sends=[], recvs=[])
        for fx, fy, fc in flips:
            px, py, pc = x ^ fx, y ^ fy, c ^ fc
            p["sends"].append((ins[0], outs[0].at[me], (px, py, pc)))
            p["recvs"].append(outs[0].at[4 * px + 2 * py + pc])
        return p

    return _Comm([v], [_sds((8, *v.shape), v.dtype)], plan, n_remote=7, n_local=1).run("small_all_gather")[0]


SWAP_CHUNK_BYTES = 3 * 2 ** 19


def _sibling_stream(t, n, value, consume, sbuf, rbuf, send_sems, recv_sems, credits):
    x, y, c = _position()
    sib = (x, y, 1 - c)

    def copy(slot):
        return pltpu.make_async_remote_copy(src_ref=sbuf.at[slot], dst_ref=rbuf.at[slot], send_sem=send_sems.at[slot],
                                            recv_sem=recv_sems.at[slot], device_id=sib, device_id_type=MESH)

    slot = t % 2

    @pl.when(jnp.logical_and(t >= 2, t < n))
    def _():
        copy(slot).wait_send()
        pl.semaphore_wait(credits.at[slot], 1)

    @pl.when(t < n)
    def _():
        sbuf[slot] = value
        copy(slot).start()

    @pl.when(t >= 1)
    def _():
        prev = 1 - slot
        copy(prev).wait_recv()
        consume(sbuf[prev], rbuf[prev])

        @pl.when(t + 1 < n)
        def _():
            pl.semaphore_signal(credits.at[prev], inc=1, device_id=sib, device_id_type=MESH)

    @pl.when(t == n)
    def _():
        copy(1 - slot).wait_send()
        if n > 1:
            copy(slot).wait_send()


def _swap_scratch(rows, cols, dtype):
    return [pltpu.VMEM((2, rows, cols), dtype), pltpu.VMEM((2, rows, cols), dtype),
            pltpu.SemaphoreType.DMA((2,)), pltpu.SemaphoreType.DMA((2,)), pltpu.SemaphoreType.REGULAR((2,))]


def _chunk_rows(rows, cols, dtype):
    return _tile(rows, max(16, SWAP_CHUNK_BYTES // (cols * jnp.dtype(dtype).itemsize)), 16)


def pair_add(name, g):
    s, r, c_ = g.shape
    r2 = r // 2
    cr = _chunk_rows(r2, c_, g.dtype)
    nj = r2 // cr

    n = s * nj

    def body(core, mine_ref, theirs_ref, o_ref, *scratch):
        def consume(_, got):
            o_ref[...] = (mine_ref[...].astype(F32) + got.astype(F32)).astype(o_ref.dtype)

        _sibling_stream(pl.program_id(0), n, theirs_ref[...], consume, *scratch)

    sent = lambda t: jnp.minimum(t, n - 1)
    used = lambda t: jnp.maximum(t - 1, 0)
    grid_spec = pltpu.PrefetchScalarGridSpec(
        num_scalar_prefetch=1, grid=(n + 1,),
        in_specs=[pl.BlockSpec((None, cr, c_), lambda t, core: (used(t) // nj, core[0] * nj + used(t) % nj, 0)),
                  pl.BlockSpec((None, cr, c_), lambda t, core: (sent(t) // nj, (1 - core[0]) * nj + sent(t) % nj, 0))],
        out_specs=pl.BlockSpec((None, cr, c_), lambda t, core: (used(t) // nj, used(t) % nj, 0)),
        scratch_shapes=_swap_scratch(cr, c_, g.dtype))
    core = lax.axis_index("c").astype(jnp.int32).reshape(1)
    return pl.pallas_call(body, name=name, grid_spec=grid_spec, out_shape=_sds((s, r2, c_), g.dtype),
                          compiler_params=_params("arbitrary"))(core, g, g)


def reduce_join(name, p):
    s, r2, c_ = p.shape
    cr = _chunk_rows(r2, c_, F32)
    nj = r2 // cr

    def body(core, p_ref, o_ref, *scratch):
        acc = p_ref[0].astype(F32)
        for i in range(1, s):
            acc = acc + p_ref[i].astype(F32)

        def consume(own, got):
            c = core[0]
            o_ref[c] = own
            o_ref[1 - c] = got

        _sibling_stream(pl.program_id(0), nj, acc, consume, *scratch)

    grid_spec = pltpu.PrefetchScalarGridSpec(
        num_scalar_prefetch=1, grid=(nj + 1,),
        in_specs=[pl.BlockSpec((s, cr, c_), lambda t, core: (0, jnp.minimum(t, nj - 1), 0))],
        out_specs=pl.BlockSpec((2, cr, c_), lambda t, core: (0, jnp.maximum(t - 1, 0), 0)),
        scratch_shapes=_swap_scratch(cr, c_, F32))
    core = lax.axis_index("c").astype(jnp.int32).reshape(1)
    out = pl.pallas_call(body, name=name, grid_spec=grid_spec, out_shape=_sds((2, r2, c_), F32),
                         compiler_params=_params("arbitrary"))(core, p)
    return out.reshape(2 * r2, c_)


def sibling_fill(name, buf):
    s, r, c_ = buf.shape
    r2 = r // 2
    cr = _chunk_rows(r2, c_, buf.dtype)
    nj = r2 // cr
    n_peers = len(CHIP_FLIPS)

    n = n_peers * nj

    def body(where, in_ref, o_ref, *scratch):
        def consume(_, got):
            o_ref[...] = got

        _sibling_stream(pl.program_id(0), n, in_ref[...], consume, *scratch)

    sent = lambda t: jnp.minimum(t, n - 1)
    used = lambda t: jnp.maximum(t - 1, 0)
    grid_spec = pltpu.PrefetchScalarGridSpec(
        num_scalar_prefetch=1, grid=(n + 1,),
        in_specs=[pl.BlockSpec((None, cr, c_),
                               lambda t, where: (where[sent(t) // nj], where[n_peers] * nj + sent(t) % nj, 0))],
        out_specs=pl.BlockSpec((None, cr, c_),
                               lambda t, where: (where[used(t) // nj], (1 - where[n_peers]) * nj + used(t) % nj, 0)),
        scratch_shapes=_swap_scratch(cr, c_, buf.dtype))
    x, y, c = _position()
    where = jnp.stack([2 * (x ^ fx) + (y ^ fy) for fx, fy in CHIP_FLIPS] + [c]).astype(jnp.int32)
    return pl.pallas_call(body, name=name, grid_spec=grid_spec, out_shape=_sds(buf.shape, buf.dtype),
                          input_output_aliases={1: 0}, compiler_params=_params("arbitrary"))(where, buf)


def sum_slots(name, p, out_dtype):
    s, r, c = p.shape
    tr = _tile(r, 256, 16)

    def body(p_ref, o_ref):
        acc = p_ref[0].astype(F32)
        for i in range(1, s):
            acc = acc + p_ref[i].astype(F32)
        o_ref[...] = acc.astype(o_ref.dtype)

    return pl.pallas_call(
        body, name=name, grid=(r // tr,), in_specs=[pl.BlockSpec((s, tr, c), lambda i: (0, i, 0))],
        out_specs=pl.BlockSpec((tr, c), lambda i: (i, 0)), out_shape=_sds((r, c), out_dtype),
        compiler_params=_params("arbitrary"),
    )(p)


def adamw(name, w, g, m, v):
    r, c = w.shape
    tr = _tile(r, 256, 8)

    def fn(_, w, g, m, v):
        m = ADAM_B1 * m + (1.0 - ADAM_B1) * g
        v = ADAM_B2 * v + (1.0 - ADAM_B2) * (g * g)
        m_hat = m / (1.0 - ADAM_B1 ** ADAM_STEP)
        v_hat = v / (1.0 - ADAM_B2 ** ADAM_STEP)
        delta = -ADAM_LR * (m_hat / (jnp.sqrt(v_hat) + ADAM_EPS) + ADAM_WD * w)
        return [delta, m, v], []

    outs, _ = _rowwise(name, fn, [w, g, m, v], [], [(c, F32)] * 3, [], tr)
    return outs


RET_SCALE = HEAD_DIM ** -0.5
MLA_SCALE = (HEAD_DIM + MLA_ROPE) ** -0.5
GRAD_DT = BF16
IN_RET = 4 * HEADS * HEAD_DIM
IN_MLA = MLA_Q_RANK + MLA_KV_RANK + MLA_ROPE
IN_MLA_PAD = IN_MLA + 64


def _heads(fn):
    return jnp.concatenate([fn(h) for h in range(HEADS)], axis=1)


def _head(a, h, stride=HEAD_DIM, off=0):
    return a[:, h * stride + off:h * stride + off + HEAD_DIM]


def _group_norm(o):
    rs = [lax.rsqrt(jnp.mean(_head(o, h) * _head(o, h), axis=-1, keepdims=True) + EPS) for h in range(HEADS)]
    return _heads(lambda h: _head(o, h) * rs[h]), rs


class _Alone:
    def host(self, kernel_name):
        return None

    def done(self, kernel_name, w):
        pass

    def grads(self, g):
        pass


def _ffn_fwd(tag, n, w, k, tm, tmk, plan):
    up, down = tag + "_up", tag + "_down"
    g, u, a = mm_nn(up, n, w["wg" + k], tm, outs=[BF16] * 3, w2=w["wu" + k],
                    epilogue=lambda g, u: (g, u, _silu(g) * u), host=plan.host(up))
    plan.done(up, w)
    ff = a.shape[1]
    wd = w["wd" + k]
    f = mm_nn_k(down, a, wd.reshape(ff, wd.shape[2]), tmk, _tile(ff, 512), host=plan.host(down))
    plan.done(down, w)
    return g, u, a, f


def _ffn_bwd(tag, df, n, g, u, a, wg, wu, wd, tm, tmk, tk, dt, plan):
    ns = wg.shape[2]

    def gate_grads(da, g, u):
        g, u = g.astype(F32), u.astype(F32)
        return da * u * _dsilu(g), da * _silu(g)

    def hosted(kernel_name, call):
        out = call(plan.host(kernel_name))
        plan.done(kernel_name, None)
        return out

    k = tag[-1]
    dg, du = hosted(tag + "_da", lambda h: mm_nt(tag + "_da", [df], [wd], tm, ns, outs=(BF16, BF16),
                                                 epilogue=gate_grads, extras=(g, u), host=h))
    dwg = hosted(tag + "_dwg", lambda h: mm_tn(tag + "_dwg", n, dg, dt, ns, tk, GRAD_DT, shard_cols=True, host=h))
    plan.grads({"wg" + k: dwg})
    dwu = hosted(tag + "_dwu", lambda h: mm_tn(tag + "_dwu", n, du, dt, ns, tk, GRAD_DT, shard_cols=True, host=h))
    plan.grads({"wu" + k: dwu})
    dwd = hosted(tag + "_dwd", lambda h: mm_tn(tag + "_dwd", a, df, ns, dt, tk, GRAD_DT, shard_rows=True, host=h))
    plan.grads({"wd" + k: dwd})
    dn = hosted(tag + "_dn", lambda h: mm_nt_k(tag + "_dn", [dg, du], [wg, wu], tmk, dt, host=h))
    return dn


def _local_step(x, tgt, meta, w, nw, plan):
    seq, d = x.shape
    t_real = N_META + seq
    tp = -(-t_real // LANES) * LANES
    zpad = jnp.zeros((tp - t_real, d), F32)
    h0 = jnp.concatenate([meta, x, zpad], axis=0)
    tgt_p = jnp.concatenate([jnp.zeros((N_META, d), F32), tgt, zpad], axis=0)
    cos_r, sin_r, cos_m, sa, sb = _rope_tables(tp)
    tm = _tile(tp, 512)
    tmk = _tile(tp, 1408)
    tk = tmk
    blk = tm
    dt = _tile(d, 1024)
    hw = HEADS * HEAD_DIM
    qw = HEADS * MLA_QK_PAD

    (n1,), _ = _rowwise("ffn1_norm", lambda r0, h, g: ([_rms(h, g)], []), [h0], [nw["ffn1_pre_norm"]],
                        [(d, BF16)], [], tm)
    g1, u1, a1, f1 = _ffn_fwd("ffn1", n1, w, "1", tm, tmk, plan)

    def post_ffn1(r0, h, f, post, pre):
        h1 = h + 0.5 * _rms(f, post)
        return [h1, _rms(h1, pre)], []

    (h1, un), _ = _rowwise("mix_norm", post_ffn1, [h0, f1], [nw["ffn1_post_norm"], nw["mix_pre_norm"]],
                           [(d, F32), (d, BF16)], [], tm)
    (proj_r,) = mm_nn("proj_r", un, w["w_r"], tm, n_block=_tile(IN_RET, 1024), host=plan.host("proj_r"))
    plan.done("proj_r", w)
    (proj_c,) = mm_nn("proj_c", un, w["w_c"], tm)

    def split_proj(r0, pr, pc, cr, sr, cm, ta, tb, qn, kvn):
        rq = _heads(lambda h: _rope_r(_head(pr, h), cr, sr))
        rk = _heads(lambda h: _rope_r(_head(pr, h, off=hw), cr, sr) * RET_SCALE)
        rv = pr[:, 2 * hw:3 * hw]
        cqn = _rms(pc[:, :MLA_Q_RANK], qn)
        ckvn = _rms(pc[:, MLA_Q_RANK:MLA_Q_RANK + MLA_KV_RANK], kvn)
        krr = _rope_m(pc[:, MLA_Q_RANK + MLA_KV_RANK:], cm, ta, tb)
        return [rq, rk, rv, cqn, ckvn, krr], []

    (rq, rk, rv, cqn, ckvn, krr), _ = _rowwise(
        "split_proj", split_proj, [proj_r, proj_c, cos_r, sin_r, cos_m, sa, sb],
        [nw["mla_q_norm"], nw["mla_kv_norm"]],
        [(hw, BF16), (hw, BF16), (hw, BF16), (MLA_Q_RANK, BF16), (MLA_KV_RANK, BF16), (LANES, F32)], [], tm)
    (qp,) = mm_nn("q_up", cqn, w["wuq"], tm)
    (kn,) = mm_nn("k_up", ckvn, w["wuk"], tm)
    (vv,) = mm_nn("v_up", ckvn, w["wuv"], tm, out_dtype=BF16)

    def build_qk(r0, qp, kn, krr, cm, ta, tb):
        qc = jnp.concatenate(
            [part for h in range(HEADS)
             for part in (_head(qp, h, MLA_QK_PAD), _rope_m(_head(qp, h, MLA_QK_PAD, HEAD_DIM), cm, ta, tb))], axis=1)
        kc = jnp.concatenate([part for h in range(HEADS) for part in (_head(kn, h), krr)], axis=1)
        return [qc, kc], []

    (qc, kc), _ = _rowwise("build_qk", build_qk, [qp, kn, krr, cos_m, sa, sb], [], [(qw, BF16), (qw, BF16)], [], tm)
    o_m, lse = attn_fwd("mla_fwd", qc, kc, vv, blk, True, MLA_SCALE, host=plan.host("mla_fwd"))
    plan.done("mla_fwd", w)
    o_r, ret_states = ret_fwd("ret_fwd", rq, rk, rv, blk, host=plan.host("ret_fwd"))
    plan.done("ret_fwd", w)

    def gate_mix(r0, rg, o_r, o_m, gn):
        y, _ = _group_norm(o_r)
        return [jnp.concatenate([_silu(rg) * (y * gn), o_m], axis=1)], []

    (mixcat,), _ = _rowwise("gate_mix", gate_mix, [(proj_r, hw, 3), o_r, o_m], [nw["ret_group_norm"]],
                            [(2 * hw, BF16)], [], tm)
    (mix,) = mm_nn("mix_out", mixcat, w["w_out"], tm, n_block=dt)

    def post_mix(r0, h, m, post, pre):
        h2 = h + _rms(m, post)
        return [h2, _rms(h2, pre)], []

    (h2, n3), _ = _rowwise("ffn2_norm", post_mix, [h1, mix], [nw["mix_post_norm"], nw["ffn2_pre_norm"]],
                           [(d, F32), (d, BF16)], [], tm)
    g2, u2, a2, f2 = _ffn_fwd("ffn2", n3, w, "2", tm, tmk, plan)

    def loss_head(r0, h, f, t, post):
        h3 = h + 0.5 * _rms(f, post)
        row = r0 + lax.broadcasted_iota(jnp.int32, h3.shape, 0)
        err = jnp.where(row >= N_META, jnp.where(row < t_real, h3 - t, 0.0), 0.0)
        dh3 = err / d
        df, dpost = _rms_bwd(f, post, 0.5 * dh3)
        return [dh3, df], [_colsum(err * err), _colsum(dpost)]

    (dh3, df2), (loss_vec, d_post2) = _rowwise("loss_head", loss_head, [h2, f2, tgt_p], [nw["ffn2_post_norm"]],
                                               [(d, F32), (d, BF16)], [d, d], tm)
    loss = 0.5 * jnp.sum(loss_vec) / d
    dn3 = _ffn_bwd("ffn2", df2, n3, g2, u2, a2, w["wg2"], w["wu2"], w["wd2"], tm, tmk, tk, dt, plan)

    def back_mix_norm(r0, h, m, dh3, dn, pre, post):
        dx, dpre = _rms_bwd(h, pre, dn)
        dh2 = dh3 + dx
        dm, dpost = _rms_bwd(m, post, dh2)
        return [dh2, dm], [_colsum(dpre), _colsum(dpost)]

    (dh2, dmix), (d_pre2, d_mix_post) = _rowwise(
        "back_mix_norm", back_mix_norm, [h2, mix, dh3, dn3], [nw["ffn2_pre_norm"], nw["mix_post_norm"]],
        [(d, F32), (d, BF16)], [d, d], tm)
    (dmixcat,) = mm_nt("mix_dx", [dmix], [w["w_out"]], tm, _tile(2 * hw, 512))
    dw_out = mm_tn("mix_dw", mixcat, dmix, 2 * hw // N_CHIPS, dt, tk, GRAD_DT, shard_rows=True)

    def back_gate(r0, dmc, rg, o_r, gn):
        d_ret, d_om = dmc[:, :hw], dmc[:, hw:]
        yh, rs = _group_norm(o_r)
        d_rg = d_ret * (yh * gn) * _dsilu(rg)
        dy = d_ret * _silu(rg)
        gyh = dy * gn
        d_or = _heads(lambda h: rs[h] * (_head(gyh, h) - _head(yh, h) * jnp.mean(_head(gyh, h) * _head(yh, h),
                                                                                    axis=-1, keepdims=True)))
        return [d_or, d_rg, d_om], [_colsum(dy * yh)]

    (d_or, d_rg, d_om), (d_gn,) = _rowwise("back_gate", back_gate, [dmixcat, (proj_r, hw, 3), o_r],
                                           [nw["ret_group_norm"]], [(hw, BF16), (hw, F32), (hw, BF16)], [hw], tm)
    dqc, dkc, dvv = attn_bwd("mla_bwd", qc, kc, vv, d_om, blk, True, MLA_SCALE, o_m, lse, host=plan.host("mla_bwd"))
    plan.done("mla_bwd", None)
    drq, drk, drv = ret_bwd("ret_bwd", rq, rk, rv, d_or, ret_states, blk)

    def back_qk(r0, dqc, dkc, dvv, cm, ta, tb):
        dqp = jnp.concatenate(
            [part for h in range(HEADS)
             for part in (_head(dqc, h, MLA_QK_PAD), _rope_m_t(_head(dqc, h, MLA_QK_PAD, HEAD_DIM), cm, ta, tb))],
            axis=1)
        dkn = _heads(lambda h: _head(dkc, h, MLA_QK_PAD))
        dkr = _head(dkc, 0, MLA_QK_PAD, HEAD_DIM)
        for h in range(1, HEADS):
            dkr = dkr + _head(dkc, h, MLA_QK_PAD, HEAD_DIM)
        return [dqp, dkn, _rope_m_t(dkr, cm, ta, tb), dvv], []

    (dqp, dkn, dkr, dvb), _ = _rowwise("back_qk", back_qk, [dqc, dkc, dvv, cos_m, sa, sb], [],
                                       [(qw, BF16), (hw, BF16), (LANES, F32), (hw, BF16)], [], tm)
    (dcqn,) = mm_nt("q_dx", [dqp], [w["wuq"]], tm, MLA_Q_RANK)
    dwuq = mm_tn("q_dw", cqn, dqp, MLA_Q_RANK, _tile(qw, 1024), tk, GRAD_DT)
    (dckvn,) = mm_nt("kv_dx", [dkn, dvb], [w["wuk"], w["wuv"]], tm, MLA_KV_RANK)
    dwuk = mm_tn("k_dw", ckvn, dkn, MLA_KV_RANK, hw, tk, GRAD_DT)
    dwuv = mm_tn("v_dw", ckvn, dvb, MLA_KV_RANK, hw, tk, GRAD_DT)

    def back_proj(r0, drq, drk, drv, d_rg, pc, dcqn, dckvn, dkr, cr, sr, qn, kvn):
        d_q = _heads(lambda h: _rope_r_t(_head(drq, h), cr, sr))
        d_k = _heads(lambda h: _rope_r_t(_head(drk, h), cr, sr) * RET_SCALE)
        dcq, a_q = _rms_bwd(pc[:, :MLA_Q_RANK], qn, dcqn)
        dckv, a_kv = _rms_bwd(pc[:, MLA_Q_RANK:MLA_Q_RANK + MLA_KV_RANK], kvn, dckvn)
        return ([jnp.concatenate([d_q, d_k, drv, d_rg], axis=1), jnp.concatenate([dcq, dckv, dkr], axis=1)],
                [_colsum(a_q), _colsum(a_kv)])

    (dproj_r, dproj_c), (d_qn, d_kvn) = _rowwise(
        "back_proj", back_proj, [drq, drk, drv, d_rg, proj_c, dcqn, dckvn, dkr, cos_r, sin_r],
        [nw["mla_q_norm"], nw["mla_kv_norm"]], [(IN_RET, BF16), (IN_MLA_PAD, BF16)],
        [MLA_Q_RANK, MLA_KV_RANK], tm)
    (dun,) = mm_nt("proj_dx", [dproj_r, dproj_c], [w["w_r"], w["w_c"]], tm, _tile(d, 512))
    dw_r = mm_tn("proj_dw_r", un, dproj_r, dt, _tile(IN_RET, 1024), tk, GRAD_DT)
    dw_c = mm_tn("proj_dw_c", un, dproj_c, dt, IN_MLA_PAD, tk, GRAD_DT)
    plan.grads(dict(w_r=dw_r, w_c=dw_c, wuq=dwuq, wuk=dwuk, wuv=dwuv, w_out=dw_out))

    def back_ffn1_norm(r0, h, f, dh2, dn, pre, post):
        dx, dpre = _rms_bwd(h, pre, dn)
        dh1 = dh2 + dx
        df, dpost = _rms_bwd(f, post, 0.5 * dh1)
        return [dh1, df], [_colsum(dpre), _colsum(dpost)]

    (dh1, df1), (d_mix_pre, d_post1) = _rowwise(
        "back_ffn1_norm", back_ffn1_norm, [h1, f1, dh2, dun], [nw["mix_pre_norm"], nw["ffn1_post_norm"]],
        [(d, F32), (d, BF16)], [d, d], tm)
    dn1 = _ffn_bwd("ffn1", df1, n1, g1, u1, a1, w["wg1"], w["wu1"], w["wd1"], tm, tmk, tk, dt, plan)

    def back_input(r0, h, dh1, dn, pre):
        dx, dpre = _rms_bwd(h, pre, dn)
        return [dh1 + dx], [_colsum(dpre)]

    (dh0,), (d_pre1,) = _rowwise("back_input", back_input, [h0, dh1, dn1], [nw["ffn1_pre_norm"]], [(d, F32)], [d], tm)

    small = dict(ffn1_pre_norm=d_pre1, ffn1_post_norm=d_post1, mix_pre_norm=d_mix_pre, ret_group_norm=d_gn,
                 mla_q_norm=d_qn, mla_kv_norm=d_kvn, mix_post_norm=d_mix_post, ffn2_pre_norm=d_pre2,
                 ffn2_post_norm=d_post2)
    return loss, dh0[N_META:t_real], small, dh0[:N_META]


WEIGHTS = ("meta_tokens", "ffn1_pre_norm", "ffn1_w_gate", "ffn1_w_up", "ffn1_w_down", "ffn1_post_norm",
           "mix_pre_norm", "w_in", "ret_group_norm", "mla_q_norm", "mla_w_uq", "mla_kv_norm", "mla_w_uk",
           "mla_w_uv", "w_out", "mix_post_norm", "ffn2_pre_norm", "ffn2_w_gate", "ffn2_w_up", "ffn2_w_down",
           "ffn2_post_norm")
BIG = ("ffn1_w_gate", "ffn1_w_up", "ffn1_w_down", "w_in", "mla_w_uq", "mla_w_uk", "mla_w_uv", "w_out",
       "ffn2_w_gate", "ffn2_w_up", "ffn2_w_down")
NORMS = ("ffn1_pre_norm", "ffn1_post_norm", "mix_pre_norm", "ret_group_norm", "mla_q_norm", "mla_kv_norm",
         "mix_post_norm", "ffn2_pre_norm", "ffn2_post_norm")


def _unshard_cols(g):
    return g.transpose(1, 0, 2).reshape(g.shape[1], -1)


def _shard_cols(a):
    return a.reshape(a.shape[0], N_CHIPS, -1).transpose(1, 0, 2)


def _pack_rows(rows, width):
    rows = [jnp.pad(r, ((0, 0), (0, width - r.shape[1]))) for r in rows]
    n = sum(r.shape[0] for r in rows)
    return jnp.pad(jnp.concatenate(rows, axis=0), ((0, -n % 8), (0, 0)))


def _weight_views(full):
    w = {}
    for n, g in full.items():
        if n == "w_in":
            w_in = _unshard_cols(g)
            w["w_r"] = w_in[:, :IN_RET]
            w["w_c"] = jnp.pad(w_in[:, IN_RET:], ((0, 0), (0, IN_MLA_PAD - IN_MLA)))
        elif n == "mla_w_uq":
            q = _unshard_cols(g).reshape(MLA_Q_RANK, HEADS, HEAD_DIM + MLA_ROPE)
            q = jnp.pad(q, ((0, 0), (0, 0), (0, MLA_QK_PAD - HEAD_DIM - MLA_ROPE)))
            w["wuq"] = q.reshape(MLA_Q_RANK, HEADS * MLA_QK_PAD)
        elif n in ("mla_w_uk", "mla_w_uv"):
            w["wu" + n[-1]] = _unshard_cols(g)
        elif n == "w_out":
            w["w_out"] = g.reshape(-1, g.shape[2])
        else:
            w["w" + n[7] + n[3]] = g
    return w


def _contributions(g):
    ffn = {"g": "gate", "u": "up", "d": "down"}
    c = {f"ffn{n[2]}_w_{ffn[n[1]]}": a for n, a in g.items() if len(n) == 3 and n[2] in "12"}
    if "w_r" in g:
        dwuq = g["wuq"].reshape(MLA_Q_RANK, HEADS, MLA_QK_PAD)[:, :, :HEAD_DIM + MLA_ROPE]
        c.update(w_in=_shard_cols(jnp.concatenate([g["w_r"], g["w_c"][:, :IN_MLA]], axis=1)),
                 mla_w_uq=_shard_cols(dwuq.reshape(MLA_Q_RANK, -1)), mla_w_uk=_shard_cols(g["wuk"]),
                 mla_w_uv=_shard_cols(g["wuv"]), w_out=g["w_out"])
    return c


class _Schedule(_Alone):
    FIRST = ("ffn1_w_gate", "ffn1_w_up")
    CARRIED = {"ffn1_up": ("ffn1_w_down",), "ffn1_down": ("w_in", "mla_w_uq", "mla_w_uk", "mla_w_uv"),
               "proj_r": ("w_out",), "mla_fwd": ("ffn2_w_gate", "ffn2_w_up"), "ret_fwd": ("ffn2_w_down",)}
    GRAD_HOST = dict(ffn2_w_gate="mla_bwd", ffn2_w_up="mla_bwd", ffn2_w_down="mla_bwd",
                     w_in="ffn1_da", mla_w_uq="ffn1_da", mla_w_uk="ffn1_da", mla_w_uv="ffn1_da", w_out="ffn1_da",
                     ffn1_w_gate="ffn1_dwu", ffn1_w_up="ffn1_dwd", ffn1_w_down="ffn1_dn")

    def __init__(self, shards):
        self.gathers = {k: (gather_halves([shards[n] for n in names]), names) for k, names in self.CARRIED.items()}
        self.waiting = {}
        self.exchanges = {}
        self.grad = {}

    def host(self, kernel_name):
        if kernel_name in self.gathers:
            return self.gathers[kernel_name][0]
        if kernel_name in self.waiting:
            names, sums = zip(*self.waiting.pop(kernel_name))
            self.exchanges[kernel_name] = (chip_exchange(list(sums)), names)
            return self.exchanges[kernel_name][0]
        return None

    def done(self, kernel_name, w):
        if kernel_name in self.gathers:
            comm, names = self.gathers[kernel_name]
            w.update(_weight_views({n: sibling_fill("fill_" + n, b) for n, b in zip(names, comm.result)}))
        elif kernel_name in self.exchanges:
            comm, names = self.exchanges[kernel_name]
            for n, q in zip(names, comm.result):
                self.grad[n] = reduce_join("reduce_join_" + n, q)

    def grads(self, g):
        for n, a in _contributions(g).items():
            self.waiting.setdefault(self.GRAD_HOST[n], []).append((n, pair_add("pair_add_" + n, a)))


def _step(p, m, v, x, loss_target):
    d = x.shape[2]
    names = list(BIG)
    shards = {n: p[n][0].astype(BF16) for n in names}
    first = _Schedule.FIRST
    gathered = gather_halves([shards[n] for n in first] + [p["meta_tokens"]]).run("gather_first")
    filled = [sibling_fill("fill_" + n, b) for n, b in zip(first + ("meta_tokens",), gathered)]
    w = _weight_views(dict(zip(first, filled[:-1])))
    meta = _unshard_cols(filled[-1])
    nw = {n: p[n] for n in NORMS}
    plan = _Schedule(shards)
    loss, grad_x, small, d_meta = _local_step(x[0], loss_target[0], meta, w, nw, plan)
    grads = dict(plan.grad)

    width = max(d, HEADS * HEAD_DIM)
    packed = _pack_rows([small[n] for n in NORMS] + [d_meta], width)
    total = sum_slots("small_sum", all_gather_devices(packed), F32)
    for i, n in enumerate(NORMS):
        grads[n] = total[i:i + 1, :p[n].shape[1]]
    cols = p["meta_tokens"].shape[1]
    chip = 2 * lax.axis_index("x") + lax.axis_index("y")
    grads["meta_tokens"] = lax.dynamic_slice(total[len(NORMS):len(NORMS) + N_META, :d], (0, chip * cols), (N_META, cols))

    delta, new_m, new_v = {}, {}, {}
    for n in names + ["meta_tokens"]:
        shape = p[n].shape
        flat = lambda a: a.reshape(-1, shape[-1])
        grads[n] = grads[n].reshape(shape)
        out = adamw("adamw_" + n, flat(p[n]), flat(grads[n]), flat(m[n]), flat(v[n]))
        delta[n], new_m[n], new_v[n] = (o.reshape(shape) for o in out)
    pk = lambda src: _pack_rows([src[n] for n in NORMS], width)
    out = adamw("adamw_norms", pk(p), pk(grads), pk(m), pk(v))
    for i, n in enumerate(NORMS):
        delta[n], new_m[n], new_v[n] = (o[i:i + 1, :p[n].shape[1]] for o in out)

    loss = lax.psum(loss, ("x", "y", "c"))
    return loss, grad_x[None], grads, delta, new_m, new_v


def kernel(x, meta_tokens, ffn1_pre_norm, ffn1_w_gate, ffn1_w_up, ffn1_w_down, ffn1_post_norm, mix_pre_norm, w_in, ret_group_norm, mla_q_norm, mla_w_uq, mla_kv_norm, mla_w_uk, mla_w_uv, w_out, mix_post_norm, ffn2_pre_norm, ffn2_w_gate, ffn2_w_up, ffn2_w_down, ffn2_post_norm, loss_target, m_meta_tokens, m_ffn1_pre_norm, m_ffn1_w_gate, m_ffn1_w_up, m_ffn1_w_down, m_ffn1_post_norm, m_mix_pre_norm, m_w_in, m_ret_group_norm, m_mla_q_norm, m_mla_w_uq, m_mla_kv_norm, m_mla_w_uk, m_mla_w_uv, m_w_out, m_mix_post_norm, m_ffn2_pre_norm, m_ffn2_w_gate, m_ffn2_w_up, m_ffn2_w_down, m_ffn2_post_norm, v_meta_tokens, v_ffn1_pre_norm, v_ffn1_w_gate, v_ffn1_w_up, v_ffn1_w_down, v_ffn1_post_norm, v_mix_pre_norm, v_w_in, v_ret_group_norm, v_mla_q_norm, v_mla_w_uq, v_mla_kv_norm, v_mla_w_uk, v_mla_w_uv, v_w_out, v_mix_post_norm, v_ffn2_pre_norm, v_ffn2_w_gate, v_ffn2_w_up, v_ffn2_w_down, v_ffn2_post_norm):
    args = locals()
    p = {n: args[n] for n in WEIGHTS}
    m = {n: args["m_" + n] for n in WEIGHTS}
    v = {n: args["v_" + n] for n in WEIGHTS}
    loss, grad_x, grads, delta, new_m, new_v = _step(p, m, v, x, loss_target)
    return (loss, grad_x, *[grads[n] for n in WEIGHTS], *[delta[n] for n in WEIGHTS],
            *[new_m[n] for n in WEIGHTS], *[new_v[n] for n in WEIGHTS])
```

```python
import functools
import math

import jax
import jax.numpy as jnp
from jax import lax
from jax.experimental import pallas as pl
from jax.experimental.pallas import tpu as pltpu

F32 = jnp.float32
BF16 = jnp.bfloat16

EPS = 1e-6
N_META = 16
HEADS = 8
HEAD_DIM = 128
MLA_ROPE = 64
MLA_QK_PAD = 256
MLA_Q_RANK = 512
MLA_KV_RANK = 256
ROPE_THETA = 10000.0
N_CHIPS = 4
LANES = 128
VMEM_LIMIT = 60 * 2 ** 20

ADAM_LR = 0.001
ADAM_B1 = 0.9
ADAM_B2 = 0.999
ADAM_EPS = 1e-08
ADAM_WD = 0.01
ADAM_STEP = 10

NN = (((1,), (0,)), ((), ()))
NT = (((1,), (1,)), ((), ()))
TN = (((0,), (0,)), ((), ()))
MESH = pl.DeviceIdType.MESH


def _tile(n, pref, align=LANES):
    if n <= pref:
        return n
    best = 0
    for t in range(align, pref + 1, align):
        if n % t == 0:
            best = t
    assert best, (n, pref)
    return best


def _params(*sem):
    return pltpu.CompilerParams(dimension_semantics=sem, vmem_limit_bytes=VMEM_LIMIT)


def _sds(shape, dtype):
    return jax.ShapeDtypeStruct(tuple(shape), dtype)


def _rowwise(name, fn, rows, consts, outs, accs, tr):
    rows = [r if isinstance(r, tuple) else (r, r.shape[1], 0) for r in rows]
    t = rows[0][0].shape[0]
    assert t % tr == 0
    n_r, n_c, n_o = len(rows), len(consts), len(outs)

    def body(*refs):
        i = pl.program_id(0)
        r = [x[...] for x in refs[:n_r]]
        c = [x[...] for x in refs[n_r:n_r + n_c]]
        o_refs = refs[n_r + n_c:n_r + n_c + n_o]
        a_refs = refs[n_r + n_c + n_o:]
        o_vals, a_vals = fn(i * tr, *r, *c)
        for ref, v in zip(o_refs, o_vals):
            ref[...] = v.astype(ref.dtype)
        if a_refs:
            @pl.when(i == 0)
            def _():
                for ref, v in zip(a_refs, a_vals):
                    ref[...] = v

            @pl.when(i > 0)
            def _():
                for ref, v in zip(a_refs, a_vals):
                    ref[...] += v

    in_specs = [pl.BlockSpec((tr, w), functools.partial(lambda cb, i: (i, cb), cb)) for _, w, cb in rows]
    in_specs += [pl.BlockSpec(a.shape, lambda i: (0, 0)) for a in consts]
    out_specs = [pl.BlockSpec((tr, w), lambda i: (i, 0)) for w, _ in outs]
    out_specs += [pl.BlockSpec((1, w), lambda i: (0, 0)) for w in accs]
    out_shape = [_sds((t, w), dt) for w, dt in outs] + [_sds((1, w), F32) for w in accs]
    res = pl.pallas_call(
        body, name=name, grid=(t // tr,), in_specs=in_specs, out_specs=out_specs, out_shape=out_shape,
        compiler_params=_params("arbitrary"),
    )(*[a for a, _, _ in rows], *consts)
    return res[:n_o], res[n_o:]


def _rms(x, w):
    r = lax.rsqrt(jnp.mean(x * x, axis=-1, keepdims=True) + EPS)
    return x * r * w


def _rms_bwd(x, w, dy):
    r = lax.rsqrt(jnp.mean(x * x, axis=-1, keepdims=True) + EPS)
    xh = x * r
    gy = dy * w
    dx = r * (gy - xh * jnp.mean(gy * xh, axis=-1, keepdims=True))
    return dx, dy * xh


def _colsum(v):
    return jnp.sum(v, axis=0, keepdims=True)


def _silu(x):
    return x * jax.nn.sigmoid(x)


def _dsilu(x):
    s = jax.nn.sigmoid(x)
    return s * (1.0 + x * (1.0 - s))


def _rope_r(x, cos, sin):
    return x * cos + pltpu.roll(x, 64, 1) * sin


def _rope_r_t(dy, cos, sin):
    return dy * cos + pltpu.roll(dy * sin, 64, 1)


def _rope_m(x, cos, sa, sb):
    return x * cos + pltpu.roll(x, 32, 1) * sa + pltpu.roll(x, 96, 1) * sb


def _rope_m_t(dy, cos, sa, sb):
    return dy * cos + pltpu.roll(dy * sa, 96, 1) + pltpu.roll(dy * sb, 32, 1)


def _rope_tables(t):
    pos = jnp.arange(t, dtype=F32)

    def cs(dim):
        inv = ROPE_THETA ** (-jnp.arange(0, dim, 2, dtype=F32) / dim)
        ang = pos[:, None] * inv[None, :]
        return jnp.cos(ang), jnp.sin(ang)

    c, s = cs(HEAD_DIM)
    cos_r = jnp.concatenate([c, c], axis=1)
    sin_r = jnp.concatenate([-s, s], axis=1)
    c, s = cs(MLA_ROPE)
    z32, z64 = jnp.zeros_like(s), jnp.zeros((t, 64), F32)
    cos_m = jnp.concatenate([c, c, z64], axis=1)
    sa = jnp.concatenate([z32, s, z64], axis=1)
    sb = jnp.concatenate([-s, z32, z64], axis=1)
    return cos_r, sin_r, cos_m, sa, sb


def _call(body, name, grid, in_specs, out_specs, out_shape, scratch, operands, host=None):
    sem = ("arbitrary",) * len(grid)
    if host is None:
        return pl.pallas_call(body, name=name, grid=grid, in_specs=in_specs, out_specs=out_specs, out_shape=out_shape,
                              scratch_shapes=scratch, compiler_params=_params(*sem))(*operands)
    n_in, n_out, n_s = len(in_specs), len(out_shape), len(scratch)
    h_in, h_out = len(host.ins), len(host.out_shape)

    def hosted(*refs):
        a = n_in
        b = a + h_in
        c = b + n_out
        d = c + h_out
        e = d + n_s
        ids = [pl.program_id(i) for i in range(len(grid))]
        first = functools.reduce(jnp.logical_and, [i == 0 for i in ids])
        last = functools.reduce(jnp.logical_and, [i == g - 1 for i, g in zip(ids, grid)])
        comm = (refs[a:b], refs[c:d], refs[e:])

        @pl.when(first)
        def _():
            host.start(*comm)

        body(*refs[:a], *refs[b:c], *refs[d:e])

        @pl.when(last)
        def _():
            host.finish(*comm)

    hbm = pl.BlockSpec(memory_space=pl.ANY)
    res = pl.pallas_call(
        hosted, name=name, grid=grid, in_specs=list(in_specs) + [hbm] * h_in, out_specs=list(out_specs) + [hbm] * h_out,
        out_shape=list(out_shape) + list(host.out_shape), scratch_shapes=list(scratch) + host.scratch(),
        compiler_params=_params(*sem))(*operands, *host.ins)
    host.result = res[n_out:]
    return res[:n_out]


def _mm(name, grid, operands, in_specs, dns, out_specs, out_shape, epilogue=None, extras=(), extra_specs=(),
        acc_shape=None, host=None):
    n_p, n_e = len(dns), len(extras)
    nk = grid[2]
    n_o = len(out_shape)
    in_place = nk > 1 and epilogue is None and n_o == 1 and out_shape[0].dtype == F32

    def body(*refs):
        ab = refs[:2 * n_p]
        ex = refs[2 * n_p:2 * n_p + n_e]
        outs = refs[2 * n_p + n_e:2 * n_p + n_e + n_o]

        part = None
        for p in range(n_p):
            d = lax.dot_general(ab[2 * p][...], ab[2 * p + 1][...], dns[p], preferred_element_type=F32)
            part = d if part is None else part + d

        def finish(acc):
            vals = (acc,) if epilogue is None else epilogue(acc, *[e[...] for e in ex])
            for o, v in zip(outs, vals):
                o[...] = v.astype(o.dtype)

        if nk == 1:
            finish(part)
        else:
            acc_ref = outs[0] if in_place else refs[2 * n_p + n_e + n_o]
            k = pl.program_id(2)

            @pl.when(k == 0)
            def _():
                acc_ref[...] = part

            @pl.when(k > 0)
            def _():
                acc_ref[...] += part

            if not in_place:
                @pl.when(k == nk - 1)
                def _():
                    finish(acc_ref[...])

    scratch = [] if nk == 1 or in_place else [pltpu.VMEM(acc_shape, F32)]
    return _call(body, name, grid, list(in_specs) + list(extra_specs), out_specs, out_shape, scratch,
                 [*operands, *extras], host)


def mm_nn(name, x, w, tm, out_dtype=F32, epilogue=None, outs=None, w2=None, n_block=None, host=None):
    t, kdim = x.shape
    if w.ndim == 3:
        s, _, ns = w.shape
        n, tn, nb = s * ns, ns, s
        wspec = pl.BlockSpec((None, kdim, ns), lambda j, i, k: (j, 0, 0))
    else:
        n = w.shape[1]
        tn = n_block or n
        nb = n // tn
        wspec = pl.BlockSpec((kdim, tn), lambda j, i, k: (0, j))
    xspec = pl.BlockSpec((tm, kdim), lambda j, i, k: (i, 0))
    ospec = pl.BlockSpec((tm, tn), lambda j, i, k: (i, j))
    outs = outs or [out_dtype]
    grid = (nb, t // tm, 1)
    if w2 is None:
        return _mm(name, grid, [x, w], [xspec, wspec], [NN], [ospec] * len(outs), [_sds((t, n), d) for d in outs],
                   epilogue=epilogue, host=host)

    def body(x_ref, w_ref, w2_ref, *o_refs):
        xv = x_ref[...]
        a = jnp.dot(xv, w_ref[...], preferred_element_type=F32)
        b = jnp.dot(xv, w2_ref[...], preferred_element_type=F32)
        for o, v in zip(o_refs, epilogue(a, b)):
            o[...] = v.astype(o.dtype)

    return _call(body, name, grid[:2],
                 [pl.BlockSpec((tm, kdim), lambda j, i: (i, 0)),
                  pl.BlockSpec((None, kdim, tn), lambda j, i: (j, 0, 0)),
                  pl.BlockSpec((None, kdim, tn), lambda j, i: (j, 0, 0))],
                 [pl.BlockSpec((tm, tn), lambda j, i: (i, j))] * len(outs), [_sds((t, n), d) for d in outs], [],
                 [x, w, w2], host)


def mm_nn_k(name, x, w, tm, tk, out_dtype=F32, host=None):
    t, kdim = x.shape
    n = w.shape[1]
    grid = (t // tm, 1, kdim // tk)
    return _mm(name, grid, [x, w],
               [pl.BlockSpec((tm, tk), lambda i, j, k: (i, k)), pl.BlockSpec((tk, n), lambda i, j, k: (k, 0))],
               [NN], [pl.BlockSpec((tm, n), lambda i, j, k: (i, 0))], [_sds((t, n), out_dtype)],
               acc_shape=(tm, n), host=host)[0]


def mm_nt(name, xs, ws, tm, tn, outs=(F32,), epilogue=None, extras=(), host=None):
    t = xs[0].shape[0]
    specs, ops = [], []
    for x, w in zip(xs, ws):
        kdim = x.shape[1]
        specs.append(pl.BlockSpec((tm, kdim), lambda j, i, k: (i, 0)))
        if w.ndim == 3:
            assert tn == w.shape[1]
            n = w.shape[0] * w.shape[1]
            specs.append(pl.BlockSpec((None, tn, kdim), lambda j, i, k: (j, 0, 0)))
        else:
            n = w.shape[0]
            specs.append(pl.BlockSpec((tn, kdim), lambda j, i, k: (j, 0)))
        ops += [x, w]
    ospec = pl.BlockSpec((tm, tn), lambda j, i, k: (i, j))
    return _mm(name, (n // tn, t // tm, 1), ops, specs, [NT] * len(xs), [ospec] * len(outs),
               [_sds((t, n), d) for d in outs], epilogue=epilogue, extras=extras,
               extra_specs=[ospec] * len(extras), host=host)


def mm_nt_k(name, xs, ws, tm, tn, out_dtype=F32, host=None):
    t = xs[0].shape[0]
    s, n, ns = ws[0].shape
    specs, ops = [], []
    for x, w in zip(xs, ws):
        specs.append(pl.BlockSpec((tm, ns), lambda i, j, k: (i, k)))
        specs.append(pl.BlockSpec((None, tn, ns), lambda i, j, k: (k, j, 0)))
        ops += [x, w]
    return _mm(name, (t // tm, n // tn, s), ops, specs, [NT] * len(xs),
               [pl.BlockSpec((tm, tn), lambda i, j, k: (i, j))], [_sds((t, n), out_dtype)], acc_shape=(tm, tn),
               host=host)[0]


def mm_tn(name, x, y, tm, tn, tk, out_dtype, shard_rows=False, shard_cols=False, host=None):
    t, m = x.shape
    n = y.shape[1]
    grid = (m // tm, n // tn, t // tk)
    if shard_cols:
        ospec = pl.BlockSpec((None, tm, tn), lambda i, j, k: (j, i, 0))
        oshape = _sds((n // tn, m, tn), out_dtype)
    elif shard_rows:
        ospec = pl.BlockSpec((None, tm, tn), lambda i, j, k: (i, 0, j))
        oshape = _sds((m // tm, tm, n), out_dtype)
    else:
        ospec = pl.BlockSpec((tm, tn), lambda i, j, k: (i, j))
        oshape = _sds((m, n), out_dtype)
    return _mm(name, grid, [x, y],
               [pl.BlockSpec((tk, tm), lambda i, j, k: (k, i)), pl.BlockSpec((tk, tn), lambda i, j, k: (k, j))],
               [TN], [ospec], [oshape], acc_shape=(tm, tn), host=host)[0]


def _decay_logs():
    return [math.log(1.0 - 2.0 ** (-5.0 - h)) for h in range(HEADS)]


def _log_decay(h):
    lg = jnp.float32(_decay_logs()[0])
    for i in range(1, HEADS):
        lg = jnp.where(h == i, jnp.float32(_decay_logs()[i]), lg)
    return lg


def _decayed_scores(q, k, lg):
    s = lax.dot_general(q, k, NT, preferred_element_type=F32)
    row = lax.broadcasted_iota(jnp.int32, s.shape, 0)
    col = lax.broadcasted_iota(jnp.int32, s.shape, 1)
    dec = jnp.where(col <= row, jnp.exp(jnp.maximum(row - col, 0).astype(F32) * lg), 0.0)
    return s * dec, dec


def _causal(s):
    row = lax.broadcasted_iota(jnp.int32, s.shape, 0)
    col = lax.broadcasted_iota(jnp.int32, s.shape, 1)
    return jnp.where(col <= row, s, -1e30)


def attn_fwd(name, q, k, v, blk, scale, host=None):
    t = q.shape[0]
    dq = q.shape[1] // HEADS
    nq = t // blk

    def body(q_ref, k_ref, v_ref, o_ref, lse_ref):
        qi = pl.program_id(1)
        qv = q_ref[...]

        def block(ki, carry, diag):
            m, l, acc = carry
            rows = pl.ds(pl.multiple_of(ki * blk, blk), blk)
            s = lax.dot_general(qv, k_ref[rows, :], NT, preferred_element_type=F32) * scale
            if diag:
                s = _causal(s)
            m_new = jnp.maximum(m, jnp.max(s, axis=-1, keepdims=True))
            p = jnp.exp(s - m_new)
            alpha = jnp.exp(m - m_new)
            return (m_new, alpha * l + jnp.sum(p, axis=-1, keepdims=True),
                    alpha * acc + jnp.dot(p.astype(BF16), v_ref[rows, :], preferred_element_type=F32))

        init = (jnp.full((blk, 1), -1e30, F32), jnp.zeros((blk, 1), F32), jnp.zeros((blk, HEAD_DIM), F32))
        carry = lax.fori_loop(0, qi, lambda ki, c: block(ki, c, False), init)
        m, l, acc = block(qi, carry, True)
        o_ref[...] = acc / l
        lse_ref[...] = jnp.broadcast_to(m + jnp.log(l), (blk, HEAD_DIM))

    hspec = pl.BlockSpec((blk, HEAD_DIM), lambda h, i: (i, h))
    return _call(body, name, (HEADS, nq),
                 [pl.BlockSpec((blk, dq), lambda h, i: (i, h)),
                  pl.BlockSpec((t, dq), lambda h, i: (0, h)),
                  pl.BlockSpec((t, HEAD_DIM), lambda h, i: (0, h))],
                 [hspec, hspec], [_sds((t, HEADS * HEAD_DIM), F32)] * 2, [], [q, k, v], host)


def attn_bwd(name, q, k, v, do, o, lse, blk, scale, host=None):
    t = q.shape[0]
    dq_w = q.shape[1] // HEADS
    nb = t // blk

    def body(q_ref, k_ref, v_ref, do_ref, o_ref, lse_ref, dq_ref, dk_ref, dv_ref):
        ki = pl.program_id(1)
        kv = k_ref[...]
        vv = v_ref[...]

        @pl.when(ki == 0)
        def _():
            dq_ref[...] = jnp.zeros_like(dq_ref)

        def block(qi, carry, diag):
            dk, dv = carry
            rows = pl.ds(pl.multiple_of(qi * blk, blk), blk)
            qv, dov = q_ref[rows, :], do_ref[rows, :]
            s = lax.dot_general(qv, kv, NT, preferred_element_type=F32) * scale
            if diag:
                s = _causal(s)
            p = jnp.exp(s - lse_ref[rows, :][:, :1])
            dp = lax.dot_general(dov, vv, NT, preferred_element_type=F32)
            delta = jnp.sum(dov.astype(F32) * o_ref[rows, :], axis=-1, keepdims=True)
            ds = p * (dp - delta) * scale
            pb, dsb = p.astype(BF16), ds.astype(BF16)
            dv = dv + lax.dot_general(pb, dov, TN, preferred_element_type=F32)
            dk = dk + lax.dot_general(dsb, qv, TN, preferred_element_type=F32)
            dq_ref[rows, :] += jnp.dot(dsb, kv, preferred_element_type=F32)
            return dk, dv

        carry = block(ki, (jnp.zeros((blk, dq_w), F32), jnp.zeros((blk, HEAD_DIM), F32)), True)
        dk, dv = lax.fori_loop(ki + 1, nb, lambda qi, c: block(qi, c, False), carry)
        dk_ref[...] = dk
        dv_ref[...] = dv

    full = lambda w: pl.BlockSpec((t, w), lambda h, j: (0, h))
    blkd = lambda w: pl.BlockSpec((blk, w), lambda h, j: (j, h))
    return _call(body, name, (HEADS, nb),
                 [full(dq_w), blkd(dq_w), blkd(HEAD_DIM), full(HEAD_DIM), full(HEAD_DIM), full(HEAD_DIM)],
                 [full(dq_w), blkd(dq_w), blkd(HEAD_DIM)],
                 [_sds(q.shape, F32), _sds(k.shape, F32), _sds(v.shape, F32)], [], [q, k, v, do, o, lse], host)


def _chunk_decays(lg, blk):
    row = lax.broadcasted_iota(jnp.int32, (blk, HEAD_DIM), 0).astype(F32)
    return jnp.exp(lg * (row + 1.0)), jnp.exp(lg * (blk - 1.0 - row)), jnp.exp(lg * blk * jnp.ones((1, HEAD_DIM), F32))


def ret_fwd(name, q, k, v, blk, host=None):
    t = q.shape[0]
    nb = t // blk

    def body(q_ref, k_ref, v_ref, o_ref, st_ref, state):
        h, i = pl.program_id(0), pl.program_id(1)
        lg = _log_decay(h)

        @pl.when(i == 0)
        def _():
            state[...] = jnp.zeros_like(state)

        qv, kv, vv = q_ref[...], k_ref[...], v_ref[...]
        before = state[...]
        st_ref[...] = before
        p, _ = _decayed_scores(qv, kv, lg)
        xi, zeta, g_blk = _chunk_decays(lg, blk)
        o_ref[...] = (jnp.dot(p.astype(BF16), vv, preferred_element_type=F32)
                      + jnp.dot(qv, before.astype(BF16), preferred_element_type=F32) * xi)
        kz = (kv.astype(F32) * zeta).astype(BF16)
        state[...] = before * g_blk + lax.dot_general(kz, vv, TN, preferred_element_type=F32)

    hspec = pl.BlockSpec((blk, HEAD_DIM), lambda h, i: (i, h))
    return _call(body, name, (HEADS, nb), [hspec] * 3,
                 [hspec, pl.BlockSpec((HEAD_DIM, HEAD_DIM), lambda h, i: (i, h))],
                 [_sds((t, HEADS * HEAD_DIM), F32), _sds((nb * HEAD_DIM, HEADS * HEAD_DIM), F32)],
                 [pltpu.VMEM((HEAD_DIM, HEAD_DIM), F32)], [q, k, v], host)


def ret_bwd(name, q, k, v, do, states, blk, host=None):
    t = q.shape[0]
    nb = t // blk

    def body(q_ref, k_ref, v_ref, do_ref, st_ref, dq_ref, dk_ref, dv_ref, dstate):
        h, i = pl.program_id(0), pl.program_id(1)
        lg = _log_decay(h)

        @pl.when(i == 0)
        def _():
            dstate[...] = jnp.zeros_like(dstate)

        qv, kv, vv, dov = q_ref[...], k_ref[...], v_ref[...], do_ref[...]
        before = st_ref[...].astype(BF16)
        after_grad = dstate[...]
        p, dec = _decayed_scores(qv, kv, lg)
        ds = lax.dot_general(dov, vv, NT, preferred_element_type=F32) * dec
        pb, dsb = p.astype(BF16), ds.astype(BF16)
        xi, zeta, g_blk = _chunk_decays(lg, blk)
        dox = (dov.astype(F32) * xi).astype(BF16)
        kz = (kv.astype(F32) * zeta).astype(BF16)
        agb = after_grad.astype(BF16)
        dv_ref[...] = (lax.dot_general(pb, dov, TN, preferred_element_type=F32)
                       + jnp.dot(kz, agb, preferred_element_type=F32))
        dq_ref[...] = (jnp.dot(dsb, kv, preferred_element_type=F32)
                       + lax.dot_general(dox, before, NT, preferred_element_type=F32))
        dk_ref[...] = (lax.dot_general(dsb, qv, TN, preferred_element_type=F32)
                       + lax.dot_general(vv, agb, NT, preferred_element_type=F32) * zeta)
        dstate[...] = after_grad * g_blk + lax.dot_general(qv, dox, TN, preferred_element_type=F32)

    hspec = pl.BlockSpec((blk, HEAD_DIM), lambda h, i: (nb - 1 - i, h))
    return _call(body, name, (HEADS, nb),
                 [hspec] * 4 + [pl.BlockSpec((HEAD_DIM, HEAD_DIM), lambda h, i: (nb - 1 - i, h))],
                 [hspec] * 3, [_sds(q.shape, F32)] * 3, [pltpu.VMEM((HEAD_DIM, HEAD_DIM), F32)],
                 [q, k, v, do, states], host)


CHIP_FLIPS = ((1, 0), (0, 1), (1, 1))


def _position():
    return lax.axis_index("x"), lax.axis_index("y"), lax.axis_index("c")


class _Comm:
    def __init__(self, ins, out_shape, plan, n_remote, n_local):
        self.ins, self.out_shape, self.plan = list(ins), list(out_shape), plan
        self.n_remote, self.n_local = n_remote, n_local
        self.result = None

    def scratch(self):
        return [pltpu.SemaphoreType.DMA((self.n_remote,)), pltpu.SemaphoreType.DMA((self.n_remote,)),
                pltpu.SemaphoreType.DMA((self.n_local,))]

    def _copies(self, in_refs, out_refs, sems):
        send_sems, recv_sems, local_sems = sems
        pos = _position()
        p = self.plan(pos, in_refs, out_refs)

        def remote(k, src, dst, dev):
            return pltpu.make_async_remote_copy(src_ref=src, dst_ref=dst, send_sem=send_sems.at[k],
                                                recv_sem=recv_sems.at[k], device_id=dev, device_id_type=MESH)

        local = [pltpu.make_async_copy(s, d, local_sems.at[i]) for i, (s, d) in enumerate(p["local"])]
        out = [remote(k, s, d, dev) for k, (s, d, dev) in enumerate(p["sends"])]
        arrivals = [functools.partial(remote, k, d, d, pos) for k, d in enumerate(p["recvs"])]
        return local, out, arrivals

    def start(self, in_refs, out_refs, sems):
        local, out, _ = self._copies(in_refs, out_refs, sems)
        for cp in local + out:
            cp.start()

    def finish(self, in_refs, out_refs, sems):
        local, out, arrivals = self._copies(in_refs, out_refs, sems)
        for make in arrivals:
            make().wait_recv()
        for cp in out:
            cp.wait_send()
        for cp in local:
            cp.wait()

    def run(self, name):
        n_in, n_out = len(self.ins), len(self.out_shape)

        def body(*refs):
            comm = (refs[:n_in], refs[n_in:n_in + n_out], refs[n_in + n_out:])
            self.start(*comm)
            self.finish(*comm)

        hbm = pl.BlockSpec(memory_space=pl.ANY)
        self.result = pl.pallas_call(body, name=name, in_specs=[hbm] * n_in, out_specs=[hbm] * n_out,
                                     out_shape=self.out_shape, scratch_shapes=self.scratch())(*self.ins)
        return self.result


def _half_rows(c, rows):
    r2 = rows // 2
    return pl.ds(pl.multiple_of(c * r2, math.gcd(r2, LANES)), r2)


def _half(ref, c, rows, lead=()):
    return ref.at[(*lead, _half_rows(c, rows))]


def gather_halves(shards):
    def plan(pos, ins, outs):
        x, y, c = pos
        me = 2 * x + y
        p = dict(local=[], sends=[], recvs=[])
        for a, (src, dst) in enumerate(zip(ins, outs)):
            rows = shards[a].shape[0]
            p["local"].append((src, dst.at[me]))
            for fx, fy in CHIP_FLIPS:
                px, py = x ^ fx, y ^ fy
                p["sends"].append((_half(src, c, rows), _half(dst, c, rows, (me,)), (px, py, c)))
                p["recvs"].append(_half(dst, c, rows, (2 * px + py,)))
        return p

    return _Comm(shards, [_sds((N_CHIPS, *s.shape), s.dtype) for s in shards], plan,
                 n_remote=3 * len(shards), n_local=len(shards))


def chip_exchange(parts):
    def plan(pos, ins, outs):
        x, y, c = pos
        me = 2 * x + y
        p = dict(local=[], sends=[], recvs=[])
        for src, dst in zip(ins, outs):
            p["local"].append((src.at[me], dst.at[me]))
            for fx, fy in CHIP_FLIPS:
                px, py = x ^ fx, y ^ fy
                peer = 2 * px + py
                p["sends"].append((src.at[peer], dst.at[me], (px, py, c)))
                p["recvs"].append(dst.at[peer])
        return p

    return _Comm(parts, [_sds(g.shape, g.dtype) for g in parts], plan, n_remote=3 * len(parts), n_local=len(parts))


def all_gather_devices(v):
    flips = [(fx, fy, fc) for fx in (0, 1) for fy in (0, 1) for fc in (0, 1)][1:]

    def plan(pos, ins, outs):
        x, y, c = pos
        me = 4 * x + 2 * y + c
        p = dict(local=[(ins[0], outs[0].at[me])], sends=[], recvs=[])
        for fx, fy, fc in flips:
            px, py, pc = x ^ fx, y ^ fy, c ^ fc
            p["sends"].append((ins[0], outs[0].at[me], (px, py, pc)))
            p["recvs"].append(outs[0].at[4 * px + 2 * py + pc])
        return p

    return _Comm([v], [_sds((8, *v.shape), v.dtype)], plan, n_remote=7, n_local=1).run("small_all_gather")[0]


SWAP_CHUNK_BYTES = 3 * 2 ** 19


def _sibling_stream(t, n, value, consume, sbuf, rbuf, send_sems, recv_sems, credits):
    x, y, c = _position()
    sib = (x, y, 1 - c)

    def copy(slot):
        return pltpu.make_async_remote_copy(src_ref=sbuf.at[slot], dst_ref=rbuf.at[slot], send_sem=send_sems.at[slot],
                                            recv_sem=recv_sems.at[slot], device_id=sib, device_id_type=MESH)

    slot = t % 2

    @pl.when(jnp.logical_and(t >= 2, t < n))
    def _():
        copy(slot).wait_send()
        pl.semaphore_wait(credits.at[slot], 1)

    @pl.when(t < n)
    def _():
        sbuf[slot] = value
        copy(slot).start()

    @pl.when(t >= 1)
    def _():
        prev = 1 - slot
        copy(prev).wait_recv()
        consume(sbuf[prev], rbuf[prev])

        @pl.when(t + 1 < n)
        def _():
            pl.semaphore_signal(credits.at[prev], inc=1, device_id=sib, device_id_type=MESH)

    @pl.when(t == n)
    def _():
        copy(1 - slot).wait_send()
        if n > 1:
            copy(slot).wait_send()


def _swap_scratch(rows, cols, dtype):
    return [pltpu.VMEM((2, rows, cols), dtype), pltpu.VMEM((2, rows, cols), dtype),
            pltpu.SemaphoreType.DMA((2,)), pltpu.SemaphoreType.DMA((2,)), pltpu.SemaphoreType.REGULAR((2,))]


def _chunk_rows(rows, cols, dtype):
    return _tile(rows, max(16, SWAP_CHUNK_BYTES // (cols * jnp.dtype(dtype).itemsize)), 16)


def pair_add(name, g):
    s, r, c_ = g.shape
    r2 = r // 2
    cr = _chunk_rows(r2, c_, g.dtype)
    nj = r2 // cr

    n = s * nj

    def body(core, mine_ref, theirs_ref, o_ref, *scratch):
        def consume(_, got):
            o_ref[...] = (mine_ref[...].astype(F32) + got.astype(F32)).astype(o_ref.dtype)

        _sibling_stream(pl.program_id(0), n, theirs_ref[...], consume, *scratch)

    sent = lambda t: jnp.minimum(t, n - 1)
    used = lambda t: jnp.maximum(t - 1, 0)
    grid_spec = pltpu.PrefetchScalarGridSpec(
        num_scalar_prefetch=1, grid=(n + 1,),
        in_specs=[pl.BlockSpec((None, cr, c_), lambda t, core: (used(t) // nj, core[0] * nj + used(t) % nj, 0)),
                  pl.BlockSpec((None, cr, c_), lambda t, core: (sent(t) // nj, (1 - core[0]) * nj + sent(t) % nj, 0))],
        out_specs=pl.BlockSpec((None, cr, c_), lambda t, core: (used(t) // nj, used(t) % nj, 0)),
        scratch_shapes=_swap_scratch(cr, c_, g.dtype))
    core = lax.axis_index("c").astype(jnp.int32).reshape(1)
    return pl.pallas_call(body, name=name, grid_spec=grid_spec, out_shape=_sds((s, r2, c_), g.dtype),
                          compiler_params=_params("arbitrary"))(core, g, g)


def reduce_join(name, p):
    s, r2, c_ = p.shape
    cr = _chunk_rows(r2, c_, F32)
    nj = r2 // cr

    def body(core, p_ref, o_ref, *scratch):
        acc = p_ref[0].astype(F32)
        for i in range(1, s):
            acc = acc + p_ref[i].astype(F32)

        def consume(own, got):
            c = core[0]
            o_ref[c] = own
            o_ref[1 - c] = got

        _sibling_stream(pl.program_id(0), nj, acc, consume, *scratch)

    grid_spec = pltpu.PrefetchScalarGridSpec(
        num_scalar_prefetch=1, grid=(nj + 1,),
        in_specs=[pl.BlockSpec((s, cr, c_), lambda t, core: (0, jnp.minimum(t, nj - 1), 0))],
        out_specs=pl.BlockSpec((2, cr, c_), lambda t, core: (0, jnp.maximum(t - 1, 0), 0)),
        scratch_shapes=_swap_scratch(cr, c_, F32))
    core = lax.axis_index("c").astype(jnp.int32).reshape(1)
    out = pl.pallas_call(body, name=name, grid_spec=grid_spec, out_shape=_sds((2, r2, c_), F32),
                         compiler_params=_params("arbitrary"))(core, p)
    return out.reshape(2 * r2, c_)


def sibling_fill(name, buf):
    s, r, c_ = buf.shape
    r2 = r // 2
    cr = _chunk_rows(r2, c_, buf.dtype)
    nj = r2 // cr
    n_peers = len(CHIP_FLIPS)

    n = n_peers * nj

    def body(where, in_ref, o_ref, *scratch):
        def consume(_, got):
            o_ref[...] = got

        _sibling_stream(pl.program_id(0), n, in_ref[...], consume, *scratch)

    sent = lambda t: jnp.minimum(t, n - 1)
    used = lambda t: jnp.maximum(t - 1, 0)
    grid_spec = pltpu.PrefetchScalarGridSpec(
        num_scalar_prefetch=1, grid=(n + 1,),
        in_specs=[pl.BlockSpec((None, cr, c_),
                               lambda t, where: (where[sent(t) // nj], where[n_peers] * nj + sent(t) % nj, 0))],
        out_specs=pl.BlockSpec((None, cr, c_),
                               lambda t, where: (where[used(t) // nj], (1 - where[n_peers]) * nj + used(t) % nj, 0)),
        scratch_shapes=_swap_scratch(cr, c_, buf.dtype))
    x, y, c = _position()
    where = jnp.stack([2 * (x ^ fx) + (y ^ fy) for fx, fy in CHIP_FLIPS] + [c]).astype(jnp.int32)
    return pl.pallas_call(body, name=name, grid_spec=grid_spec, out_shape=_sds(buf.shape, buf.dtype),
                          input_output_aliases={1: 0}, compiler_params=_params("arbitrary"))(where, buf)


def sum_slots(name, p, out_dtype):
    s, r, c = p.shape
    tr = _tile(r, 256, 16)

    def body(p_ref, o_ref):
        acc = p_ref[0].astype(F32)
        for i in range(1, s):
            acc = acc + p_ref[i].astype(F32)
        o_ref[...] = acc.astype(o_ref.dtype)

    return pl.pallas_call(
        body, name=name, grid=(r // tr,), in_specs=[pl.BlockSpec((s, tr, c), lambda i: (0, i, 0))],
        out_specs=pl.BlockSpec((tr, c), lambda i: (i, 0)), out_shape=_sds((r, c), out_dtype),
        compiler_params=_params("arbitrary"),
    )(p)


def adamw(name, w, g, m, v):
    r, c = w.shape
    tr = _tile(r, 256, 8)

    def fn(_, w, g, m, v):
        m = ADAM_B1 * m + (1.0 - ADAM_B1) * g
        v = ADAM_B2 * v + (1.0 - ADAM_B2) * (g * g)
        m_hat = m / (1.0 - ADAM_B1 ** ADAM_STEP)
        v_hat = v / (1.0 - ADAM_B2 ** ADAM_STEP)
        delta = -ADAM_LR * (m_hat / (jnp.sqrt(v_hat) + ADAM_EPS) + ADAM_WD * w)
        return [delta, m, v], []

    outs, _ = _rowwise(name, fn, [w, g, m, v], [], [(c, F32)] * 3, [], tr)
    return outs


RET_SCALE = HEAD_DIM ** -0.5
MLA_SCALE = (HEAD_DIM + MLA_ROPE) ** -0.5
GRAD_DT = BF16
IN_RET = 4 * HEADS * HEAD_DIM
IN_MLA = MLA_Q_RANK + MLA_KV_RANK + MLA_ROPE
IN_MLA_PAD = IN_MLA + 64


def _heads(fn):
    return jnp.concatenate([fn(h) for h in range(HEADS)], axis=1)


def _head(a, h, stride=HEAD_DIM, off=0):
    return a[:, h * stride + off:h * stride + off + HEAD_DIM]


def _group_norm(o):
    rs = [lax.rsqrt(jnp.mean(_head(o, h) * _head(o, h), axis=-1, keepdims=True) + EPS) for h in range(HEADS)]
    return _heads(lambda h: _head(o, h) * rs[h]), rs


class _Alone:
    def host(self, kernel_name):
        return None

    def done(self, kernel_name, w):
        pass

    def grads(self, g):
        pass


def _ffn_fwd(tag, n, w, k, tm, tmk, plan):
    up, down = tag + "_up", tag + "_down"
    g, u, a = mm_nn(up, n, w["wg" + k], tm, outs=[BF16] * 3, w2=w["wu" + k],
                    epilogue=lambda g, u: (g, u, _silu(g) * u), host=plan.host(up))
    plan.done(up, w)
    ff = a.shape[1]
    wd = w["wd" + k]
    f = mm_nn_k(down, a, wd.reshape(ff, wd.shape[2]), tmk, _tile(ff, 1408), host=plan.host(down))
    plan.done(down, w)
    return g, u, a, f


def _ffn_bwd(tag, df, n, g, u, a, wg, wu, wd, tm, tmk, tk, dt, plan):
    ns = wg.shape[2]

    def gate_grads(da, g, u):
        g, u = g.astype(F32), u.astype(F32)
        return da * u * _dsilu(g), da * _silu(g)

    def hosted(kernel_name, call):
        out = call(plan.host(kernel_name))
        plan.done(kernel_name, None)
        return out

    k = tag[-1]
    dg, du = hosted(tag + "_da", lambda h: mm_nt(tag + "_da", [df], [wd], tm, ns, outs=(BF16, BF16),
                                                 epilogue=gate_grads, extras=(g, u), host=h))
    dwg = hosted(tag + "_dwg", lambda h: mm_tn(tag + "_dwg", n, dg, dt, ns, tk, GRAD_DT, shard_cols=True, host=h))
    plan.grads({"wg" + k: dwg})
    dwu = hosted(tag + "_dwu", lambda h: mm_tn(tag + "_dwu", n, du, dt, ns, tk, GRAD_DT, shard_cols=True, host=h))
    plan.grads({"wu" + k: dwu})
    dwd = hosted(tag + "_dwd", lambda h: mm_tn(tag + "_dwd", a, df, ns, dt, tk, GRAD_DT, shard_rows=True, host=h))
    plan.grads({"wd" + k: dwd})
    dn = hosted(tag + "_dn", lambda h: mm_nt_k(tag + "_dn", [dg, du], [wg, wu], tmk, dt, host=h))
    return dn


def _local_step(x, tgt, meta, w, nw, plan):
    seq, d = x.shape
    t_real = N_META + seq
    tp = -(-t_real // LANES) * LANES
    zpad = jnp.zeros((tp - t_real, d), F32)
    h0 = jnp.concatenate([meta, x, zpad], axis=0)
    tgt_p = jnp.concatenate([jnp.zeros((N_META, d), F32), tgt, zpad], axis=0)
    cos_r, sin_r, cos_m, sa, sb = _rope_tables(tp)
    tm = _tile(tp, 512)
    tmk = _tile(tp, 1408)
    tk = tp
    blk = tm
    dt = _tile(d, 1024)
    hw = HEADS * HEAD_DIM
    qw = HEADS * MLA_QK_PAD

    (n1,), _ = _rowwise("ffn1_norm", lambda r0, h, g: ([_rms(h, g)], []), [h0], [nw["ffn1_pre_norm"]],
                        [(d, BF16)], [], tm)
    g1, u1, a1, f1 = _ffn_fwd("ffn1", n1, w, "1", tm, tmk, plan)

    def post_ffn1(r0, h, f, post, pre):
        h1 = h + 0.5 * _rms(f, post)
        return [h1, _rms(h1, pre)], []

    (h1, un), _ = _rowwise("mix_norm", post_ffn1, [h0, f1], [nw["ffn1_post_norm"], nw["mix_pre_norm"]],
                           [(d, F32), (d, BF16)], [], tm)
    (proj_r,) = mm_nn("proj_r", un, w["w_r"], tm, n_block=_tile(IN_RET, 1024), host=plan.host("proj_r"))
    plan.done("proj_r", w)
    (proj_c,) = mm_nn("proj_c", un, w["w_c"], tm)

    def split_proj(r0, pr, pc, cr, sr, cm, ta, tb, qn, kvn):
        rq = _heads(lambda h: _rope_r(_head(pr, h), cr, sr))
        rk = _heads(lambda h: _rope_r(_head(pr, h, off=hw), cr, sr) * RET_SCALE)
        rv = pr[:, 2 * hw:3 * hw]
        cqn = _rms(pc[:, :MLA_Q_RANK], qn)
        ckvn = _rms(pc[:, MLA_Q_RANK:MLA_Q_RANK + MLA_KV_RANK], kvn)
        krr = _rope_m(pc[:, MLA_Q_RANK + MLA_KV_RANK:], cm, ta, tb)
        return [rq, rk, rv, cqn, ckvn, krr], []

    (rq, rk, rv, cqn, ckvn, krr), _ = _rowwise(
        "split_proj", split_proj, [proj_r, proj_c, cos_r, sin_r, cos_m, sa, sb],
        [nw["mla_q_norm"], nw["mla_kv_norm"]],
        [(hw, BF16), (hw, BF16), (hw, BF16), (MLA_Q_RANK, BF16), (MLA_KV_RANK, BF16), (LANES, F32)], [], tm)
    (qp,) = mm_nn("q_up", cqn, w["wuq"], tm)
    (kn,) = mm_nn("k_up", ckvn, w["wuk"], tm)
    (vv,) = mm_nn("v_up", ckvn, w["wuv"], tm, out_dtype=BF16)

    def build_qk(r0, qp, kn, krr, cm, ta, tb):
        qc = jnp.concatenate(
            [part for h in range(HEADS)
             for part in (_head(qp, h, MLA_QK_PAD), _rope_m(_head(qp, h, MLA_QK_PAD, HEAD_DIM), cm, ta, tb))], axis=1)
        kc = jnp.concatenate([part for h in range(HEADS) for part in (_head(kn, h), krr)], axis=1)
        return [qc, kc], []

    (qc, kc), _ = _rowwise("build_qk", build_qk, [qp, kn, krr, cos_m, sa, sb], [], [(qw, BF16), (qw, BF16)], [], tm)
    o_m, lse = attn_fwd("mla_fwd", qc, kc, vv, blk, MLA_SCALE, host=plan.host("mla_fwd"))
    plan.done("mla_fwd", w)
    o_r, ret_states = ret_fwd("ret_fwd", rq, rk, rv, blk, host=plan.host("ret_fwd"))
    plan.done("ret_fwd", w)

    def gate_mix(r0, rg, o_r, o_m, gn):
        y, _ = _group_norm(o_r)
        return [jnp.concatenate([_silu(rg) * (y * gn), o_m], axis=1)], []

    (mixcat,), _ = _rowwise("gate_mix", gate_mix, [(proj_r, hw, 3), o_r, o_m], [nw["ret_group_norm"]],
                            [(2 * hw, BF16)], [], tm)
    (mix,) = mm_nn("mix_out", mixcat, w["w_out"], tm, n_block=dt)

    def post_mix(r0, h, m, post, pre):
        h2 = h + _rms(m, post)
        return [h2, _rms(h2, pre)], []

    (h2, n3), _ = _rowwise("ffn2_norm", post_mix, [h1, mix], [nw["mix_post_norm"], nw["ffn2_pre_norm"]],
                           [(d, F32), (d, BF16)], [], tm)
    g2, u2, a2, f2 = _ffn_fwd("ffn2", n3, w, "2", tm, tmk, plan)

    def loss_head(r0, h, f, t, post):
        h3 = h + 0.5 * _rms(f, post)
        row = r0 + lax.broadcasted_iota(jnp.int32, h3.shape, 0)
        err = jnp.where(row >= N_META, jnp.where(row < t_real, h3 - t, 0.0), 0.0)
        dh3 = err / d
        df, dpost = _rms_bwd(f, post, 0.5 * dh3)
        return [dh3, df], [_colsum(err * err), _colsum(dpost)]

    (dh3, df2), (loss_vec, d_post2) = _rowwise("loss_head", loss_head, [h2, f2, tgt_p], [nw["ffn2_post_norm"]],
                                               [(d, F32), (d, BF16)], [d, d], tm)
    loss = 0.5 * jnp.sum(loss_vec) / d
    dn3 = _ffn_bwd("ffn2", df2, n3, g2, u2, a2, w["wg2"], w["wu2"], w["wd2"], tm, tmk, tk, dt, plan)

    def back_mix_norm(r0, h, m, dh3, dn, pre, post):
        dx, dpre = _rms_bwd(h, pre, dn)
        dh2 = dh3 + dx
        dm, dpost = _rms_bwd(m, post, dh2)
        return [dh2, dm], [_colsum(dpre), _colsum(dpost)]

    (dh2, dmix), (d_pre2, d_mix_post) = _rowwise(
        "back_mix_norm", back_mix_norm, [h2, mix, dh3, dn3], [nw["ffn2_pre_norm"], nw["mix_post_norm"]],
        [(d, F32), (d, BF16)], [d, d], tm)
    (dmixcat,) = mm_nt("mix_dx", [dmix], [w["w_out"]], tm, _tile(2 * hw, 512))
    dw_out = mm_tn("mix_dw", mixcat, dmix, 2 * hw // N_CHIPS, dt, tk, GRAD_DT, shard_rows=True)

    def back_gate(r0, dmc, rg, o_r, gn):
        d_ret, d_om = dmc[:, :hw], dmc[:, hw:]
        yh, rs = _group_norm(o_r)
        d_rg = d_ret * (yh * gn) * _dsilu(rg)
        dy = d_ret * _silu(rg)
        gyh = dy * gn
        d_or = _heads(lambda h: rs[h] * (_head(gyh, h) - _head(yh, h) * jnp.mean(_head(gyh, h) * _head(yh, h),
                                                                                    axis=-1, keepdims=True)))
        return [d_or, d_rg, d_om], [_colsum(dy * yh)]

    (d_or, d_rg, d_om), (d_gn,) = _rowwise("back_gate", back_gate, [dmixcat, (proj_r, hw, 3), o_r],
                                           [nw["ret_group_norm"]], [(hw, BF16), (hw, F32), (hw, BF16)], [hw], tm)
    dqc, dkc, dvv = attn_bwd("mla_bwd", qc, kc, vv, d_om, o_m, lse, blk, MLA_SCALE, host=plan.host("mla_bwd"))
    plan.done("mla_bwd", None)
    drq, drk, drv = ret_bwd("ret_bwd", rq, rk, rv, d_or, ret_states, blk)

    def back_qk(r0, dqc, dkc, dvv, cm, ta, tb):
        dqp = jnp.concatenate(
            [part for h in range(HEADS)
             for part in (_head(dqc, h, MLA_QK_PAD), _rope_m_t(_head(dqc, h, MLA_QK_PAD, HEAD_DIM), cm, ta, tb))],
            axis=1)
        dkn = _heads(lambda h: _head(dkc, h, MLA_QK_PAD))
        dkr = _head(dkc, 0, MLA_QK_PAD, HEAD_DIM)
        for h in range(1, HEADS):
            dkr = dkr + _head(dkc, h, MLA_QK_PAD, HEAD_DIM)
        return [dqp, dkn, _rope_m_t(dkr, cm, ta, tb), dvv], []

    (dqp, dkn, dkr, dvb), _ = _rowwise("back_qk", back_qk, [dqc, dkc, dvv, cos_m, sa, sb], [],
                                       [(qw, BF16), (hw, BF16), (LANES, F32), (hw, BF16)], [], tm)
    (dcqn,) = mm_nt("q_dx", [dqp], [w["wuq"]], tm, MLA_Q_RANK)
    dwuq = mm_tn("q_dw", cqn, dqp, MLA_Q_RANK, _tile(qw, 1024), tk, GRAD_DT)
    (dckvn,) = mm_nt("kv_dx", [dkn, dvb], [w["wuk"], w["wuv"]], tm, MLA_KV_RANK)
    dwuk = mm_tn("k_dw", ckvn, dkn, MLA_KV_RANK, hw, tk, GRAD_DT)
    dwuv = mm_tn("v_dw", ckvn, dvb, MLA_KV_RANK, hw, tk, GRAD_DT)

    def back_proj(r0, drq, drk, drv, d_rg, pc, dcqn, dckvn, dkr, cr, sr, qn, kvn):
        d_q = _heads(lambda h: _rope_r_t(_head(drq, h), cr, sr))
        d_k = _heads(lambda h: _rope_r_t(_head(drk, h), cr, sr) * RET_SCALE)
        dcq, a_q = _rms_bwd(pc[:, :MLA_Q_RANK], qn, dcqn)
        dckv, a_kv = _rms_bwd(pc[:, MLA_Q_RANK:MLA_Q_RANK + MLA_KV_RANK], kvn, dckvn)
        return ([jnp.concatenate([d_q, d_k, drv, d_rg], axis=1), jnp.concatenate([dcq, dckv, dkr], axis=1)],
                [_colsum(a_q), _colsum(a_kv)])

    (dproj_r, dproj_c), (d_qn, d_kvn) = _rowwise(
        "back_proj", back_proj, [drq, drk, drv, d_rg, proj_c, dcqn, dckvn, dkr, cos_r, sin_r],
        [nw["mla_q_norm"], nw["mla_kv_norm"]], [(IN_RET, BF16), (IN_MLA_PAD, BF16)],
        [MLA_Q_RANK, MLA_KV_RANK], tm)
    (dun,) = mm_nt("proj_dx", [dproj_r, dproj_c], [w["w_r"], w["w_c"]], tm, _tile(d, 512))
    dw_r = mm_tn("proj_dw_r", un, dproj_r, dt, _tile(IN_RET, 1024), tk, GRAD_DT)
    dw_c = mm_tn("proj_dw_c", un, dproj_c, dt, IN_MLA_PAD, tk, GRAD_DT)
    plan.grads(dict(w_r=dw_r, w_c=dw_c, wuq=dwuq, wuk=dwuk, wuv=dwuv, w_out=dw_out))

    def back_ffn1_norm(r0, h, f, dh2, dn, pre, post):
        dx, dpre = _rms_bwd(h, pre, dn)
        dh1 = dh2 + dx
        df, dpost = _rms_bwd(f, post, 0.5 * dh1)
        return [dh1, df], [_colsum(dpre), _colsum(dpost)]

    (dh1, df1), (d_mix_pre, d_post1) = _rowwise(
        "back_ffn1_norm", back_ffn1_norm, [h1, f1, dh2, dun], [nw["mix_pre_norm"], nw["ffn1_post_norm"]],
        [(d, F32), (d, BF16)], [d, d], tm)
    dn1 = _ffn_bwd("ffn1", df1, n1, g1, u1, a1, w["wg1"], w["wu1"], w["wd1"], tm, tmk, tk, dt, plan)

    def back_input(r0, h, dh1, dn, pre):
        dx, dpre = _rms_bwd(h, pre, dn)
        return [dh1 + dx], [_colsum(dpre)]

    (dh0,), (d_pre1,) = _rowwise("back_input", back_input, [h0, dh1, dn1], [nw["ffn1_pre_norm"]], [(d, F32)], [d], tm)

    small = dict(ffn1_pre_norm=d_pre1, ffn1_post_norm=d_post1, mix_pre_norm=d_mix_pre, ret_group_norm=d_gn,
                 mla_q_norm=d_qn, mla_kv_norm=d_kvn, mix_post_norm=d_mix_post, ffn2_pre_norm=d_pre2,
                 ffn2_post_norm=d_post2)
    return loss, dh0[N_META:t_real], small, dh0[:N_META]


WEIGHTS = ("meta_tokens", "ffn1_pre_norm", "ffn1_w_gate", "ffn1_w_up", "ffn1_w_down", "ffn1_post_norm",
           "mix_pre_norm", "w_in", "ret_group_norm", "mla_q_norm", "mla_w_uq", "mla_kv_norm", "mla_w_uk",
           "mla_w_uv", "w_out", "mix_post_norm", "ffn2_pre_norm", "ffn2_w_gate", "ffn2_w_up", "ffn2_w_down",
           "ffn2_post_norm")
BIG = ("ffn1_w_gate", "ffn1_w_up", "ffn1_w_down", "w_in", "mla_w_uq", "mla_w_uk", "mla_w_uv", "w_out",
       "ffn2_w_gate", "ffn2_w_up", "ffn2_w_down")
NORMS = ("ffn1_pre_norm", "ffn1_post_norm", "mix_pre_norm", "ret_group_norm", "mla_q_norm", "mla_kv_norm",
         "mix_post_norm", "ffn2_pre_norm", "ffn2_post_norm")


def _unshard_cols(g):
    return g.transpose(1, 0, 2).reshape(g.shape[1], -1)


def _shard_cols(a):
    return a.reshape(a.shape[0], N_CHIPS, -1).transpose(1, 0, 2)


def _pack_rows(rows, width):
    rows = [jnp.pad(r, ((0, 0), (0, width - r.shape[1]))) for r in rows]
    n = sum(r.shape[0] for r in rows)
    return jnp.pad(jnp.concatenate(rows, axis=0), ((0, -n % 8), (0, 0)))


def _weight_views(full):
    w = {}
    for n, g in full.items():
        if n == "w_in":
            w_in = _unshard_cols(g)
            w["w_r"] = w_in[:, :IN_RET]
            w["w_c"] = jnp.pad(w_in[:, IN_RET:], ((0, 0), (0, IN_MLA_PAD - IN_MLA)))
        elif n == "mla_w_uq":
            q = _unshard_cols(g).reshape(MLA_Q_RANK, HEADS, HEAD_DIM + MLA_ROPE)
            q = jnp.pad(q, ((0, 0), (0, 0), (0, MLA_QK_PAD - HEAD_DIM - MLA_ROPE)))
            w["wuq"] = q.reshape(MLA_Q_RANK, HEADS * MLA_QK_PAD)
        elif n in ("mla_w_uk", "mla_w_uv"):
            w["wu" + n[-1]] = _unshard_cols(g)
        elif n == "w_out":
            w["w_out"] = g.reshape(-1, g.shape[2])
        else:
            w["w" + n[7] + n[3]] = g
    return w


def _contributions(g):
    ffn = {"g": "gate", "u": "up", "d": "down"}
    c = {f"ffn{n[2]}_w_{ffn[n[1]]}": a for n, a in g.items() if len(n) == 3 and n[2] in "12"}
    if "w_r" in g:
        dwuq = g["wuq"].reshape(MLA_Q_RANK, HEADS, MLA_QK_PAD)[:, :, :HEAD_DIM + MLA_ROPE]
        c.update(w_in=_shard_cols(jnp.concatenate([g["w_r"], g["w_c"][:, :IN_MLA]], axis=1)),
                 mla_w_uq=_shard_cols(dwuq.reshape(MLA_Q_RANK, -1)), mla_w_uk=_shard_cols(g["wuk"]),
                 mla_w_uv=_shard_cols(g["wuv"]), w_out=g["w_out"])
    return c


class _Schedule(_Alone):
    FIRST = ("ffn1_w_gate", "ffn1_w_up")
    CARRIED = {"ffn1_up": ("ffn1_w_down",), "ffn1_down": ("w_in", "mla_w_uq", "mla_w_uk", "mla_w_uv"),
               "proj_r": ("w_out",), "mla_fwd": ("ffn2_w_gate", "ffn2_w_up"), "ret_fwd": ("ffn2_w_down",)}
    GRAD_HOST = dict(ffn2_w_gate="mla_bwd", ffn2_w_up="mla_bwd", ffn2_w_down="mla_bwd",
                     w_in="ffn1_da", mla_w_uq="ffn1_da", mla_w_uk="ffn1_da", mla_w_uv="ffn1_da", w_out="ffn1_da",
                     ffn1_w_gate="ffn1_dwu", ffn1_w_up="ffn1_dwd", ffn1_w_down="ffn1_dn")

    def __init__(self, shards):
        self.gathers = {k: (gather_halves([shards[n] for n in names]), names) for k, names in self.CARRIED.items()}
        self.waiting = {}
        self.exchanges = {}
        self.grad = {}

    def host(self, kernel_name):
        if kernel_name in self.gathers:
            return self.gathers[kernel_name][0]
        if kernel_name in self.waiting:
            names, sums = zip(*self.waiting.pop(kernel_name))
            self.exchanges[kernel_name] = (chip_exchange(list(sums)), names)
            return self.exchanges[kernel_name][0]
        return None

    def done(self, kernel_name, w):
        if kernel_name in self.gathers:
            comm, names = self.gathers[kernel_name]
            w.update(_weight_views({n: sibling_fill("fill_" + n, b) for n, b in zip(names, comm.result)}))
        elif kernel_name in self.exchanges:
            comm, names = self.exchanges[kernel_name]
            for n, q in zip(names, comm.result):
                self.grad[n] = reduce_join("reduce_join_" + n, q)

    def grads(self, g):
        for n, a in _contributions(g).items():
            self.waiting.setdefault(self.GRAD_HOST[n], []).append((n, pair_add("pair_add_" + n, a)))


def _step(p, m, v, x, loss_target):
    d = x.shape[2]
    names = list(BIG)
    shards = {n: p[n][0].astype(BF16) for n in names}
    first = _Schedule.FIRST
    gathered = gather_halves([shards[n] for n in first] + [p["meta_tokens"]]).run("gather_first")
    filled = [sibling_fill("fill_" + n, b) for n, b in zip(first + ("meta_tokens",), gathered)]
    w = _weight_views(dict(zip(first, filled[:-1])))
    meta = _unshard_cols(filled[-1])
    nw = {n: p[n] for n in NORMS}
    plan = _Schedule(shards)
    loss, grad_x, small, d_meta = _local_step(x[0], loss_target[0], meta, w, nw, plan)
    grads = dict(plan.grad)

    width = max(d, HEADS * HEAD_DIM)
    packed = _pack_rows([small[n] for n in NORMS] + [d_meta], width)
    total = sum_slots("small_sum", all_gather_devices(packed), F32)
    for i, n in enumerate(NORMS):
        grads[n] = total[i:i + 1, :p[n].shape[1]]
    cols = p["meta_tokens"].shape[1]
    chip = 2 * lax.axis_index("x") + lax.axis_index("y")
    grads["meta_tokens"] = lax.dynamic_slice(total[len(NORMS):len(NORMS) + N_META, :d], (0, chip * cols), (N_META, cols))

    delta, new_m, new_v = {}, {}, {}
    for n in names + ["meta_tokens"]:
        shape = p[n].shape
        flat = lambda a: a.reshape(-1, shape[-1])
        grads[n] = grads[n].reshape(shape)
        out = adamw("adamw_" + n, flat(p[n]), flat(grads[n]), flat(m[n]), flat(v[n]))
        delta[n], new_m[n], new_v[n] = (o.reshape(shape) for o in out)
    pk = lambda src: _pack_rows([src[n] for n in NORMS], width)
    out = adamw("adamw_norms", pk(p), pk(grads), pk(m), pk(v))
    for i, n in enumerate(NORMS):
        delta[n], new_m[n], new_v[n] = (o[i:i + 1, :p[n].shape[1]] for o in out)

    loss = lax.psum(loss, ("x", "y", "c"))
    return loss, grad_x[None], grads, delta, new_m, new_v


def kernel(x, meta_tokens, ffn1_pre_norm, ffn1_w_gate, ffn1_w_up, ffn1_w_down, ffn1_post_norm, mix_pre_norm, w_in, ret_group_norm, mla_q_norm, mla_w_uq, mla_kv_norm, mla_w_uk, mla_w_uv, w_out, mix_post_norm, ffn2_pre_norm, ffn2_w_gate, ffn2_w_up, ffn2_w_down, ffn2_post_norm, loss_target, m_meta_tokens, m_ffn1_pre_norm, m_ffn1_w_gate, m_ffn1_w_up, m_ffn1_w_down, m_ffn1_post_norm, m_mix_pre_norm, m_w_in, m_ret_group_norm, m_mla_q_norm, m_mla_w_uq, m_mla_kv_norm, m_mla_w_uk, m_mla_w_uv, m_w_out, m_mix_post_norm, m_ffn2_pre_norm, m_ffn2_w_gate, m_ffn2_w_up, m_ffn2_w_down, m_ffn2_post_norm, v_meta_tokens, v_ffn1_pre_norm, v_ffn1_w_gate, v_ffn1_w_up, v_ffn1_w_down, v_ffn1_post_norm, v_mix_pre_norm, v_w_in, v_ret_group_norm, v_mla_q_norm, v_mla_w_uq, v_mla_kv_norm, v_mla_w_uk, v_mla_w_uv, v_w_out, v_mix_post_norm, v_ffn2_pre_norm, v_ffn2_w_gate, v_ffn2_w_up, v_ffn2_w_down, v_ffn2_post_norm):
    args = locals()
    p = {n: args[n] for n in WEIGHTS}
    m = {n: args["m_" + n] for n in WEIGHTS}
    v = {n: args["v_" + n] for n in WEIGHTS}
    loss, grad_x, grads, delta, new_m, new_v = _step(p, m, v, x, loss_target)
    return (loss, grad_x, *[grads[n] for n in WEIGHTS], *[delta[n] for n in WEIGHTS],
            *[new_m[n] for n in WEIGHTS], *[new_v[n] for n in WEIGHTS])
```

```python
import functools
import math

import jax
import jax.numpy as jnp
from jax import lax
from jax.experimental import pallas as pl
from jax.experimental.pallas import tpu as pltpu

F32 = jnp.float32
BF16 = jnp.bfloat16

EPS = 1e-6
N_META = 16
HEADS = 8
HEAD_DIM = 128
MLA_ROPE = 64
MLA_QK_PAD = 256
MLA_Q_RANK = 512
MLA_KV_RANK = 256
ROPE_THETA = 10000.0
N_CHIPS = 4
LANES = 128
VMEM_LIMIT = 60 * 2 ** 20

ADAM_LR = 0.001
ADAM_B1 = 0.9
ADAM_B2 = 0.999
ADAM_EPS = 1e-08
ADAM_WD = 0.01
ADAM_STEP = 10

NN = (((1,), (0,)), ((), ()))
NT = (((1,), (1,)), ((), ()))
TN = (((0,), (0,)), ((), ()))
MESH = pl.DeviceIdType.MESH


def _tile(n, pref, align=LANES):
    if n <= pref:
        return n
    best = 0
    for t in range(align, pref + 1, align):
        if n % t == 0:
            best = t
    assert best, (n, pref)
    return best


def _params(*sem):
    return pltpu.CompilerParams(dimension_semantics=sem, vmem_limit_bytes=VMEM_LIMIT)


def _sds(shape, dtype):
    return jax.ShapeDtypeStruct(tuple(shape), dtype)


def _rowwise(name, fn, rows, consts, outs, accs, tr):
    rows = [r if isinstance(r, tuple) else (r, r.shape[1], 0) for r in rows]
    t = rows[0][0].shape[0]
    assert t % tr == 0
    n_r, n_c, n_o = len(rows), len(consts), len(outs)

    def body(*refs):
        i = pl.program_id(0)
        r = [x[...] for x in refs[:n_r]]
        c = [x[...] for x in refs[n_r:n_r + n_c]]
        o_refs = refs[n_r + n_c:n_r + n_c + n_o]
        a_refs = refs[n_r + n_c + n_o:]
        o_vals, a_vals = fn(i * tr, *r, *c)
        for ref, v in zip(o_refs, o_vals):
            ref[...] = v.astype(ref.dtype)
        if a_refs:
            @pl.when(i == 0)
            def _():
                for ref, v in zip(a_refs, a_vals):
                    ref[...] = v

            @pl.when(i > 0)
            def _():
                for ref, v in zip(a_refs, a_vals):
                    ref[...] += v

    in_specs = [pl.BlockSpec((tr, w), functools.partial(lambda cb, i: (i, cb), cb)) for _, w, cb in rows]
    in_specs += [pl.BlockSpec(a.shape, lambda i: (0, 0)) for a in consts]
    out_specs = [pl.BlockSpec((tr, w), lambda i: (i, 0)) for w, _ in outs]
    out_specs += [pl.BlockSpec((1, w), lambda i: (0, 0)) for w in accs]
    out_shape = [_sds((t, w), dt) for w, dt in outs] + [_sds((1, w), F32) for w in accs]
    res = pl.pallas_call(
        body, name=name, grid=(t // tr,), in_specs=in_specs, out_specs=out_specs, out_shape=out_shape,
        compiler_params=_params("arbitrary"),
    )(*[a for a, _, _ in rows], *consts)
    return res[:n_o], res[n_o:]


def _rms(x, w):
    r = lax.rsqrt(jnp.mean(x * x, axis=-1, keepdims=True) + EPS)
    return x * r * w


def _rms_bwd(x, w, dy):
    r = lax.rsqrt(jnp.mean(x * x, axis=-1, keepdims=True) + EPS)
    xh = x * r
    gy = dy * w
    dx = r * (gy - xh * jnp.mean(gy * xh, axis=-1, keepdims=True))
    return dx, dy * xh


def _colsum(v):
    return jnp.sum(v, axis=0, keepdims=True)


def _silu(x):
    return x * jax.nn.sigmoid(x)


def _dsilu(x):
    s = jax.nn.sigmoid(x)
    return s * (1.0 + x * (1.0 - s))


def _rope_r(x, cos, sin):
    return x * cos + pltpu.roll(x, 64, 1) * sin


def _rope_r_t(dy, cos, sin):
    return dy * cos + pltpu.roll(dy * sin, 64, 1)


def _rope_m(x, cos, sa, sb):
    return x * cos + pltpu.roll(x, 32, 1) * sa + pltpu.roll(x, 96, 1) * sb


def _rope_m_t(dy, cos, sa, sb):
    return dy * cos + pltpu.roll(dy * sa, 96, 1) + pltpu.roll(dy * sb, 32, 1)


def _rope_tables(t):
    pos = jnp.arange(t, dtype=F32)

    def cs(dim):
        inv = ROPE_THETA ** (-jnp.arange(0, dim, 2, dtype=F32) / dim)
        ang = pos[:, None] * inv[None, :]
        return jnp.cos(ang), jnp.sin(ang)

    c, s = cs(HEAD_DIM)
    cos_r = jnp.concatenate([c, c], axis=1)
    sin_r = jnp.concatenate([-s, s], axis=1)
    c, s = cs(MLA_ROPE)
    z32, z64 = jnp.zeros_like(s), jnp.zeros((t, 64), F32)
    cos_m = jnp.concatenate([c, c, z64], axis=1)
    sa = jnp.concatenate([z32, s, z64], axis=1)
    sb = jnp.concatenate([-s, z32, z64], axis=1)
    return cos_r, sin_r, cos_m, sa, sb


def _call(body, name, grid, in_specs, out_specs, out_shape, scratch, operands, host=None):
    sem = ("arbitrary",) * len(grid)
    if host is None:
        return pl.pallas_call(body, name=name, grid=grid, in_specs=in_specs, out_specs=out_specs, out_shape=out_shape,
                              scratch_shapes=scratch, compiler_params=_params(*sem))(*operands)
    n_in, n_out, n_s = len(in_specs), len(out_shape), len(scratch)
    h_in, h_out = len(host.ins), len(host.out_shape)

    def hosted(*refs):
        a = n_in
        b = a + h_in
        c = b + n_out
        d = c + h_out
        e = d + n_s
        ids = [pl.program_id(i) for i in range(len(grid))]
        first = functools.reduce(jnp.logical_and, [i == 0 for i in ids])
        last = functools.reduce(jnp.logical_and, [i == g - 1 for i, g in zip(ids, grid)])
        comm = (refs[a:b], refs[c:d], refs[e:])

        @pl.when(first)
        def _():
            host.start(*comm)

        body(*refs[:a], *refs[b:c], *refs[d:e])

        @pl.when(last)
        def _():
            host.finish(*comm)

    hbm = pl.BlockSpec(memory_space=pl.ANY)
    res = pl.pallas_call(
        hosted, name=name, grid=grid, in_specs=list(in_specs) + [hbm] * h_in, out_specs=list(out_specs) + [hbm] * h_out,
        out_shape=list(out_shape) + list(host.out_shape), scratch_shapes=list(scratch) + host.scratch(),
        compiler_params=_params(*sem))(*operands, *host.ins)
    host.result = res[n_out:]
    return res[:n_out]


def _mm(name, grid, operands, in_specs, dns, out_specs, out_shape, epilogue=None, extras=(), extra_specs=(),
        acc_shape=None, host=None):
    n_p, n_e = len(dns), len(extras)
    nk = grid[2]
    n_o = len(out_shape)
    in_place = nk > 1 and epilogue is None and n_o == 1 and out_shape[0].dtype == F32

    def body(*refs):
        ab = refs[:2 * n_p]
        ex = refs[2 * n_p:2 * n_p + n_e]
        outs = refs[2 * n_p + n_e:2 * n_p + n_e + n_o]

        part = None
        for p in range(n_p):
            d = lax.dot_general(ab[2 * p][...], ab[2 * p + 1][...], dns[p], preferred_element_type=F32)
            part = d if part is None else part + d

        def finish(acc):
            vals = (acc,) if epilogue is None else epilogue(acc, *[e[...] for e in ex])
            for o, v in zip(outs, vals):
                o[...] = v.astype(o.dtype)

        if nk == 1:
            finish(part)
        else:
            acc_ref = outs[0] if in_place else refs[2 * n_p + n_e + n_o]
            k = pl.program_id(2)

            @pl.when(k == 0)
            def _():
                acc_ref[...] = part

            @pl.when(k > 0)
            def _():
                acc_ref[...] += part

            if not in_place:
                @pl.when(k == nk - 1)
                def _():
                    finish(acc_ref[...])

    scratch = [] if nk == 1 or in_place else [pltpu.VMEM(acc_shape, F32)]
    return _call(body, name, grid, list(in_specs) + list(extra_specs), out_specs, out_shape, scratch,
                 [*operands, *extras], host)


def mm_nn(name, x, w, tm, out_dtype=F32, epilogue=None, outs=None, w2=None, n_block=None, extras=(), host=None):
    t, kdim = x.shape
    if w.ndim == 3:
        s, _, ns = w.shape
        n, tn, nb = s * ns, ns, s
        wspec = pl.BlockSpec((None, kdim, ns), lambda j, i, k: (j, 0, 0))
    else:
        n = w.shape[1]
        tn = n_block or n
        nb = n // tn
        wspec = pl.BlockSpec((kdim, tn), lambda j, i, k: (0, j))
    xspec = pl.BlockSpec((tm, kdim), lambda j, i, k: (i, 0))
    ospec = pl.BlockSpec((tm, tn), lambda j, i, k: (i, j))
    outs = outs or [out_dtype]
    grid = (nb, t // tm, 1)
    if w2 is None:
        return _mm(name, grid, [x, w], [xspec, wspec], [NN], [ospec] * len(outs), [_sds((t, n), d) for d in outs],
                   epilogue=epilogue, extras=extras, extra_specs=[ospec] * len(extras), host=host)

    def body(x_ref, w_ref, w2_ref, *o_refs):
        xv = x_ref[...]
        a = jnp.dot(xv, w_ref[...], preferred_element_type=F32)
        b = jnp.dot(xv, w2_ref[...], preferred_element_type=F32)
        for o, v in zip(o_refs, epilogue(a, b)):
            o[...] = v.astype(o.dtype)

    return _call(body, name, grid[:2],
                 [pl.BlockSpec((tm, kdim), lambda j, i: (i, 0)),
                  pl.BlockSpec((None, kdim, tn), lambda j, i: (j, 0, 0)),
                  pl.BlockSpec((None, kdim, tn), lambda j, i: (j, 0, 0))],
                 [pl.BlockSpec((tm, tn), lambda j, i: (i, j))] * len(outs), [_sds((t, n), d) for d in outs], [],
                 [x, w, w2], host)


def mm_nn_k(name, x, w, tm, tk, out_dtype=F32, host=None):
    t, kdim = x.shape
    n = w.shape[1]
    grid = (t // tm, 1, kdim // tk)
    return _mm(name, grid, [x, w],
               [pl.BlockSpec((tm, tk), lambda i, j, k: (i, k)), pl.BlockSpec((tk, n), lambda i, j, k: (k, 0))],
               [NN], [pl.BlockSpec((tm, n), lambda i, j, k: (i, 0))], [_sds((t, n), out_dtype)],
               acc_shape=(tm, n), host=host)[0]


def mm_nt(name, xs, ws, tm, tn, outs=(F32,), epilogue=None, extras=(), host=None):
    t = xs[0].shape[0]
    specs, ops = [], []
    for x, w in zip(xs, ws):
        kdim = x.shape[1]
        specs.append(pl.BlockSpec((tm, kdim), lambda j, i, k: (i, 0)))
        if w.ndim == 3:
            assert tn == w.shape[1]
            n = w.shape[0] * w.shape[1]
            specs.append(pl.BlockSpec((None, tn, kdim), lambda j, i, k: (j, 0, 0)))
        else:
            n = w.shape[0]
            specs.append(pl.BlockSpec((tn, kdim), lambda j, i, k: (j, 0)))
        ops += [x, w]
    ospec = pl.BlockSpec((tm, tn), lambda j, i, k: (i, j))
    return _mm(name, (n // tn, t // tm, 1), ops, specs, [NT] * len(xs), [ospec] * len(outs),
               [_sds((t, n), d) for d in outs], epilogue=epilogue, extras=extras,
               extra_specs=[ospec] * len(extras), host=host)


def mm_nt_k(name, xs, ws, tm, tn, out_dtype=F32, host=None):
    t = xs[0].shape[0]
    s, n, ns = ws[0].shape
    specs, ops = [], []
    for x, w in zip(xs, ws):
        specs.append(pl.BlockSpec((tm, ns), lambda i, j, k: (i, k)))
        specs.append(pl.BlockSpec((None, tn, ns), lambda i, j, k: (k, j, 0)))
        ops += [x, w]
    return _mm(name, (t // tm, n // tn, s), ops, specs, [NT] * len(xs),
               [pl.BlockSpec((tm, tn), lambda i, j, k: (i, j))], [_sds((t, n), out_dtype)], acc_shape=(tm, tn),
               host=host)[0]


def mm_tn(name, x, y, tm, tn, tk, out_dtype, shard_rows=False, shard_cols=False, host=None):
    t, m = x.shape
    n = y.shape[1]
    grid = (m // tm, n // tn, t // tk)
    if shard_cols:
        ospec = pl.BlockSpec((None, tm, tn), lambda i, j, k: (j, i, 0))
        oshape = _sds((n // tn, m, tn), out_dtype)
    elif shard_rows:
        ospec = pl.BlockSpec((None, tm, tn), lambda i, j, k: (i, 0, j))
        oshape = _sds((m // tm, tm, n), out_dtype)
    else:
        ospec = pl.BlockSpec((tm, tn), lambda i, j, k: (i, j))
        oshape = _sds((m, n), out_dtype)
    return _mm(name, grid, [x, y],
               [pl.BlockSpec((tk, tm), lambda i, j, k: (k, i)), pl.BlockSpec((tk, tn), lambda i, j, k: (k, j))],
               [TN], [ospec], [oshape], acc_shape=(tm, tn), host=host)[0]


def _decay_logs():
    return [math.log(1.0 - 2.0 ** (-5.0 - h)) for h in range(HEADS)]


def _log_decay(h):
    lg = jnp.float32(_decay_logs()[0])
    for i in range(1, HEADS):
        lg = jnp.where(h == i, jnp.float32(_decay_logs()[i]), lg)
    return lg


def _decayed_scores(q, k, lg):
    s = lax.dot_general(q, k, NT, preferred_element_type=F32)
    row = lax.broadcasted_iota(jnp.int32, s.shape, 0)
    col = lax.broadcasted_iota(jnp.int32, s.shape, 1)
    dec = jnp.where(col <= row, jnp.exp(jnp.maximum(row - col, 0).astype(F32) * lg), 0.0)
    return s * dec, dec


def _causal(s):
    row = lax.broadcasted_iota(jnp.int32, s.shape, 0)
    col = lax.broadcasted_iota(jnp.int32, s.shape, 1)
    return jnp.where(col <= row, s, -1e30)


def attn_fwd(name, q, k, v, blk, scale, host=None):
    t = q.shape[0]
    dq = q.shape[1] // HEADS
    nq = t // blk

    def body(q_ref, k_ref, v_ref, o_ref, lse_ref):
        qi = pl.program_id(1)
        qv = q_ref[...]

        def block(ki, carry, diag):
            m, l, acc = carry
            rows = pl.ds(pl.multiple_of(ki * blk, blk), blk)
            s = lax.dot_general(qv, k_ref[rows, :], NT, preferred_element_type=F32) * scale
            if diag:
                s = _causal(s)
            m_new = jnp.maximum(m, jnp.max(s, axis=-1, keepdims=True))
            p = jnp.exp(s - m_new)
            alpha = jnp.exp(m - m_new)
            return (m_new, alpha * l + jnp.sum(p, axis=-1, keepdims=True),
                    alpha * acc + jnp.dot(p.astype(BF16), v_ref[rows, :], preferred_element_type=F32))

        init = (jnp.full((blk, 1), -1e30, F32), jnp.zeros((blk, 1), F32), jnp.zeros((blk, HEAD_DIM), F32))
        carry = lax.fori_loop(0, qi, lambda ki, c: block(ki, c, False), init)
        m, l, acc = block(qi, carry, True)
        o_ref[...] = acc / l
        lse_ref[...] = jnp.broadcast_to(m + jnp.log(l), (blk, HEAD_DIM))

    hspec = pl.BlockSpec((blk, HEAD_DIM), lambda h, i: (i, h))
    return _call(body, name, (HEADS, nq),
                 [pl.BlockSpec((blk, dq), lambda h, i: (i, h)),
                  pl.BlockSpec((t, dq), lambda h, i: (0, h)),
                  pl.BlockSpec((t, HEAD_DIM), lambda h, i: (0, h))],
                 [hspec, hspec], [_sds((t, HEADS * HEAD_DIM), F32)] * 2, [], [q, k, v], host)


def attn_bwd(name, q, k, v, do, o, lse, blk, scale, host=None):
    t = q.shape[0]
    dq_w = q.shape[1] // HEADS
    nb = t // blk

    def body(q_ref, k_ref, v_ref, do_ref, o_ref, lse_ref, dq_ref, dk_ref, dv_ref):
        ki = pl.program_id(1)
        kv = k_ref[...]
        vv = v_ref[...]

        @pl.when(ki == 0)
        def _():
            dq_ref[...] = jnp.zeros_like(dq_ref)

        def block(qi, carry, diag):
            dk, dv = carry
            rows = pl.ds(pl.multiple_of(qi * blk, blk), blk)
            qv, dov = q_ref[rows, :], do_ref[rows, :]
            s = lax.dot_general(qv, kv, NT, preferred_element_type=F32) * scale
            if diag:
                s = _causal(s)
            p = jnp.exp(s - lse_ref[rows, :][:, :1])
            dp = lax.dot_general(dov, vv, NT, preferred_element_type=F32)
            delta = jnp.sum(dov.astype(F32) * o_ref[rows, :], axis=-1, keepdims=True)
            ds = p * (dp - delta) * scale
            pb, dsb = p.astype(BF16), ds.astype(BF16)
            dv = dv + lax.dot_general(pb, dov, TN, preferred_element_type=F32)
            dk = dk + lax.dot_general(dsb, qv, TN, preferred_element_type=F32)
            dq_ref[rows, :] += jnp.dot(dsb, kv, preferred_element_type=F32)
            return dk, dv

        carry = block(ki, (jnp.zeros((blk, dq_w), F32), jnp.zeros((blk, HEAD_DIM), F32)), True)
        dk, dv = lax.fori_loop(ki + 1, nb, lambda qi, c: block(qi, c, False), carry)
        dk_ref[...] = dk
        dv_ref[...] = dv

    full = lambda w: pl.BlockSpec((t, w), lambda h, j: (0, h))
    blkd = lambda w: pl.BlockSpec((blk, w), lambda h, j: (j, h))
    return _call(body, name, (HEADS, nb),
                 [full(dq_w), blkd(dq_w), blkd(HEAD_DIM), full(HEAD_DIM), full(HEAD_DIM), full(HEAD_DIM)],
                 [full(dq_w), blkd(dq_w), blkd(HEAD_DIM)],
                 [_sds(q.shape, F32), _sds(k.shape, F32), _sds(v.shape, F32)], [], [q, k, v, do, o, lse], host)


def _chunk_decays(lg, blk):
    row = lax.broadcasted_iota(jnp.int32, (blk, HEAD_DIM), 0).astype(F32)
    return jnp.exp(lg * (row + 1.0)), jnp.exp(lg * (blk - 1.0 - row)), jnp.exp(lg * blk * jnp.ones((1, HEAD_DIM), F32))


def ret_fwd(name, q, k, v, blk, host=None):
    t = q.shape[0]
    nb = t // blk

    def body(q_ref, k_ref, v_ref, o_ref, st_ref, state):
        h, i = pl.program_id(0), pl.program_id(1)
        lg = _log_decay(h)

        @pl.when(i == 0)
        def _():
            state[...] = jnp.zeros_like(state)

        qv, kv, vv = q_ref[...], k_ref[...], v_ref[...]
        before = state[...]
        st_ref[...] = before
        p, _ = _decayed_scores(qv, kv, lg)
        xi, zeta, g_blk = _chunk_decays(lg, blk)
        o_ref[...] = (jnp.dot(p.astype(BF16), vv, preferred_element_type=F32)
                      + jnp.dot(qv, before.astype(BF16), preferred_element_type=F32) * xi)
        kz = (kv.astype(F32) * zeta).astype(BF16)
        state[...] = before * g_blk + lax.dot_general(kz, vv, TN, preferred_element_type=F32)

    hspec = pl.BlockSpec((blk, HEAD_DIM), lambda h, i: (i, h))
    return _call(body, name, (HEADS, nb), [hspec] * 3,
                 [hspec, pl.BlockSpec((HEAD_DIM, HEAD_DIM), lambda h, i: (i, h))],
                 [_sds((t, HEADS * HEAD_DIM), F32), _sds((nb * HEAD_DIM, HEADS * HEAD_DIM), F32)],
                 [pltpu.VMEM((HEAD_DIM, HEAD_DIM), F32)], [q, k, v], host)


def ret_bwd(name, q, k, v, do, states, blk, host=None):
    t = q.shape[0]
    nb = t // blk

    def body(q_ref, k_ref, v_ref, do_ref, st_ref, dq_ref, dk_ref, dv_ref, dstate):
        h, i = pl.program_id(0), pl.program_id(1)
        lg = _log_decay(h)

        @pl.when(i == 0)
        def _():
            dstate[...] = jnp.zeros_like(dstate)

        qv, kv, vv, dov = q_ref[...], k_ref[...], v_ref[...], do_ref[...]
        before = st_ref[...].astype(BF16)
        after_grad = dstate[...]
        p, dec = _decayed_scores(qv, kv, lg)
        ds = lax.dot_general(dov, vv, NT, preferred_element_type=F32) * dec
        pb, dsb = p.astype(BF16), ds.astype(BF16)
        xi, zeta, g_blk = _chunk_decays(lg, blk)
        dox = (dov.astype(F32) * xi).astype(BF16)
        kz = (kv.astype(F32) * zeta).astype(BF16)
        agb = after_grad.astype(BF16)
        dv_ref[...] = (lax.dot_general(pb, dov, TN, preferred_element_type=F32)
                       + jnp.dot(kz, agb, preferred_element_type=F32))
        dq_ref[...] = (jnp.dot(dsb, kv, preferred_element_type=F32)
                       + lax.dot_general(dox, before, NT, preferred_element_type=F32))
        dk_ref[...] = (lax.dot_general(dsb, qv, TN, preferred_element_type=F32)
                       + lax.dot_general(vv, agb, NT, preferred_element_type=F32) * zeta)
        dstate[...] = after_grad * g_blk + lax.dot_general(qv, dox, TN, preferred_element_type=F32)

    hspec = pl.BlockSpec((blk, HEAD_DIM), lambda h, i: (nb - 1 - i, h))
    return _call(body, name, (HEADS, nb),
                 [hspec] * 4 + [pl.BlockSpec((HEAD_DIM, HEAD_DIM), lambda h, i: (nb - 1 - i, h))],
                 [hspec] * 3, [_sds(q.shape, F32)] * 3, [pltpu.VMEM((HEAD_DIM, HEAD_DIM), F32)],
                 [q, k, v, do, states], host)


CHIP_FLIPS = ((1, 0), (0, 1), (1, 1))


def _position():
    return lax.axis_index("x"), lax.axis_index("y"), lax.axis_index("c")


class _Comm:
    def __init__(self, ins, out_shape, plan, n_remote, n_local):
        self.ins, self.out_shape, self.plan = list(ins), list(out_shape), plan
        self.n_remote, self.n_local = n_remote, n_local
        self.result = None

    def scratch(self):
        return [pltpu.SemaphoreType.DMA((self.n_remote,)), pltpu.SemaphoreType.DMA((self.n_remote,)),
                pltpu.SemaphoreType.DMA((self.n_local,))]

    def _copies(self, in_refs, out_refs, sems):
        send_sems, recv_sems, local_sems = sems
        pos = _position()
        p = self.plan(pos, in_refs, out_refs)

        def remote(k, src, dst, dev):
            return pltpu.make_async_remote_copy(src_ref=src, dst_ref=dst, send_sem=send_sems.at[k],
                                                recv_sem=recv_sems.at[k], device_id=dev, device_id_type=MESH)

        local = [pltpu.make_async_copy(s, d, local_sems.at[i]) for i, (s, d) in enumerate(p["local"])]
        out = [remote(k, s, d, dev) for k, (s, d, dev) in enumerate(p["sends"])]
        arrivals = [functools.partial(remote, k, d, d, pos) for k, d in enumerate(p["recvs"])]
        return local, out, arrivals

    def start(self, in_refs, out_refs, sems):
        local, out, _ = self._copies(in_refs, out_refs, sems)
        for cp in local + out:
            cp.start()

    def finish(self, in_refs, out_refs, sems):
        local, out, arrivals = self._copies(in_refs, out_refs, sems)
        for make in arrivals:
            make().wait_recv()
        for cp in out:
            cp.wait_send()
        for cp in local:
            cp.wait()

    def run(self, name):
        n_in, n_out = len(self.ins), len(self.out_shape)

        def body(*refs):
            comm = (refs[:n_in], refs[n_in:n_in + n_out], refs[n_in + n_out:])
            self.start(*comm)
            self.finish(*comm)

        hbm = pl.BlockSpec(memory_space=pl.ANY)
        self.result = pl.pallas_call(body, name=name, in_specs=[hbm] * n_in, out_specs=[hbm] * n_out,
                                     out_shape=self.out_shape, scratch_shapes=self.scratch())(*self.ins)
        return self.result


def _half_rows(c, rows):
    r2 = rows // 2
    return pl.ds(pl.multiple_of(c * r2, math.gcd(r2, LANES)), r2)


def _half(ref, c, rows, lead=()):
    return ref.at[(*lead, _half_rows(c, rows))]


def gather_halves(shards):
    def plan(pos, ins, outs):
        x, y, c = pos
        me = 2 * x + y
        p = dict(local=[], sends=[], recvs=[])
        for a, (src, dst) in enumerate(zip(ins, outs)):
            rows = shards[a].shape[0]
            p["local"].append((src, dst.at[me]))
            for fx, fy in CHIP_FLIPS:
                px, py = x ^ fx, y ^ fy
                p["sends"].append((_half(src, c, rows), _half(dst, c, rows, (me,)), (px, py, c)))
                p["recvs"].append(_half(dst, c, rows, (2 * px + py,)))
        return p

    return _Comm(shards, [_sds((N_CHIPS, *s.shape), s.dtype) for s in shards], plan,
                 n_remote=3 * len(shards), n_local=len(shards))


def chip_exchange(parts):
    def plan(pos, ins, outs):
        x, y, c = pos
        me = 2 * x + y
        p = dict(local=[], sends=[], recvs=[])
        for src, dst in zip(ins, outs):
            p["local"].append((src.at[me], dst.at[me]))
            for fx, fy in CHIP_FLIPS:
                px, py = x ^ fx, y ^ fy
                peer = 2 * px + py
                p["sends"].append((src.at[peer], dst.at[me], (px, py, c)))
                p["recvs"].append(dst.at[peer])
        return p

    return _Comm(parts, [_sds(g.shape, g.dtype) for g in parts], plan, n_remote=3 * len(parts), n_local=len(parts))


def all_gather_devices(v):
    flips = [(fx, fy, fc) for fx in (0, 1) for fy in (0, 1) for fc in (0, 1)][1:]

    def plan(pos, ins, outs):
        x, y, c = pos
        me = 4 * x + 2 * y + c
        p = dict(local=[(ins[0], outs[0].at[me])], sends=[], recvs=[])
        for fx, fy, fc in flips:
            px, py, pc = x ^ fx, y ^ fy, c ^ fc
            p["sends"].append((ins[0], outs[0].at[me], (px, py, pc)))
            p["recvs"].append(outs[0].at[4 * px + 2 * py + pc])
        return p

    return _Comm([v], [_sds((8, *v.shape), v.dtype)], plan, n_remote=7, n_local=1).run("small_all_gather")[0]


SWAP_CHUNK_BYTES = 3 * 2 ** 19


def _sibling_stream(t, n, value, consume, sbuf, rbuf, send_sems, recv_sems, credits):
    x, y, c = _position()
    sib = (x, y, 1 - c)

    def copy(slot):
        return pltpu.make_async_remote_copy(src_ref=sbuf.at[slot], dst_ref=rbuf.at[slot], send_sem=send_sems.at[slot],
                                            recv_sem=recv_sems.at[slot], device_id=sib, device_id_type=MESH)

    slot = t % 2

    @pl.when(jnp.logical_and(t >= 2, t < n))
    def _():
        copy(slot).wait_send()
        pl.semaphore_wait(credits.at[slot], 1)

    @pl.when(t < n)
    def _():
        sbuf[slot] = value
        copy(slot).start()

    @pl.when(t >= 1)
    def _():
        prev = 1 - slot
        copy(prev).wait_recv()
        consume(sbuf[prev], rbuf[prev])

        @pl.when(t + 1 < n)
        def _():
            pl.semaphore_signal(credits.at[prev], inc=1, device_id=sib, device_id_type=MESH)

    @pl.when(t == n)
    def _():
        copy(1 - slot).wait_send()
        if n > 1:
            copy(slot).wait_send()


def _swap_scratch(rows, cols, dtype):
    return [pltpu.VMEM((2, rows, cols), dtype), pltpu.VMEM((2, rows, cols), dtype),
            pltpu.SemaphoreType.DMA((2,)), pltpu.SemaphoreType.DMA((2,)), pltpu.SemaphoreType.REGULAR((2,))]


def _chunk_rows(rows, cols, dtype):
    return _tile(rows, max(16, SWAP_CHUNK_BYTES // (cols * jnp.dtype(dtype).itemsize)), 16)


def pair_add(name, g):
    s, r, c_ = g.shape
    r2 = r // 2
    cr = _chunk_rows(r2, c_, g.dtype)
    nj = r2 // cr

    n = s * nj

    def body(core, mine_ref, theirs_ref, o_ref, *scratch):
        def consume(_, got):
            o_ref[...] = (mine_ref[...].astype(F32) + got.astype(F32)).astype(o_ref.dtype)

        _sibling_stream(pl.program_id(0), n, theirs_ref[...], consume, *scratch)

    sent = lambda t: jnp.minimum(t, n - 1)
    used = lambda t: jnp.maximum(t - 1, 0)
    grid_spec = pltpu.PrefetchScalarGridSpec(
        num_scalar_prefetch=1, grid=(n + 1,),
        in_specs=[pl.BlockSpec((None, cr, c_), lambda t, core: (used(t) // nj, core[0] * nj + used(t) % nj, 0)),
                  pl.BlockSpec((None, cr, c_), lambda t, core: (sent(t) // nj, (1 - core[0]) * nj + sent(t) % nj, 0))],
        out_specs=pl.BlockSpec((None, cr, c_), lambda t, core: (used(t) // nj, used(t) % nj, 0)),
        scratch_shapes=_swap_scratch(cr, c_, g.dtype))
    core = lax.axis_index("c").astype(jnp.int32).reshape(1)
    return pl.pallas_call(body, name=name, grid_spec=grid_spec, out_shape=_sds((s, r2, c_), g.dtype),
                          compiler_params=_params("arbitrary"))(core, g, g)


def reduce_join(name, p):
    s, r2, c_ = p.shape
    cr = _chunk_rows(r2, c_, F32)
    nj = r2 // cr

    def body(core, p_ref, o_ref, *scratch):
        acc = p_ref[0].astype(F32)
        for i in range(1, s):
            acc = acc + p_ref[i].astype(F32)

        def consume(own, got):
            c = core[0]
            o_ref[c] = own
            o_ref[1 - c] = got

        _sibling_stream(pl.program_id(0), nj, acc, consume, *scratch)

    grid_spec = pltpu.PrefetchScalarGridSpec(
        num_scalar_prefetch=1, grid=(nj + 1,),
        in_specs=[pl.BlockSpec((s, cr, c_), lambda t, core: (0, jnp.minimum(t, nj - 1), 0))],
        out_specs=pl.BlockSpec((2, cr, c_), lambda t, core: (0, jnp.maximum(t - 1, 0), 0)),
        scratch_shapes=_swap_scratch(cr, c_, F32))
    core = lax.axis_index("c").astype(jnp.int32).reshape(1)
    out = pl.pallas_call(body, name=name, grid_spec=grid_spec, out_shape=_sds((2, r2, c_), F32),
                         compiler_params=_params("arbitrary"))(core, p)
    return out.reshape(2 * r2, c_)


def sibling_fill(name, buf):
    s, r, c_ = buf.shape
    r2 = r // 2
    cr = _chunk_rows(r2, c_, buf.dtype)
    nj = r2 // cr
    n_peers = len(CHIP_FLIPS)

    n = n_peers * nj

    def body(where, in_ref, o_ref, *scratch):
        def consume(_, got):
            o_ref[...] = got

        _sibling_stream(pl.program_id(0), n, in_ref[...], consume, *scratch)

    sent = lambda t: jnp.minimum(t, n - 1)
    used = lambda t: jnp.maximum(t - 1, 0)
    grid_spec = pltpu.PrefetchScalarGridSpec(
        num_scalar_prefetch=1, grid=(n + 1,),
        in_specs=[pl.BlockSpec((None, cr, c_),
                               lambda t, where: (where[sent(t) // nj], where[n_peers] * nj + sent(t) % nj, 0))],
        out_specs=pl.BlockSpec((None, cr, c_),
                               lambda t, where: (where[used(t) // nj], (1 - where[n_peers]) * nj + used(t) % nj, 0)),
        scratch_shapes=_swap_scratch(cr, c_, buf.dtype))
    x, y, c = _position()
    where = jnp.stack([2 * (x ^ fx) + (y ^ fy) for fx, fy in CHIP_FLIPS] + [c]).astype(jnp.int32)
    return pl.pallas_call(body, name=name, grid_spec=grid_spec, out_shape=_sds(buf.shape, buf.dtype),
                          input_output_aliases={1: 0}, compiler_params=_params("arbitrary"))(where, buf)


def sum_slots(name, p, out_dtype):
    s, r, c = p.shape
    tr = _tile(r, 256, 16)

    def body(p_ref, o_ref):
        acc = p_ref[0].astype(F32)
        for i in range(1, s):
            acc = acc + p_ref[i].astype(F32)
        o_ref[...] = acc.astype(o_ref.dtype)

    return pl.pallas_call(
        body, name=name, grid=(r // tr,), in_specs=[pl.BlockSpec((s, tr, c), lambda i: (0, i, 0))],
        out_specs=pl.BlockSpec((tr, c), lambda i: (i, 0)), out_shape=_sds((r, c), out_dtype),
        compiler_params=_params("arbitrary"),
    )(p)


def adamw(name, w, g, m, v):
    r, c = w.shape
    tr = _tile(r, 256, 8)

    def fn(_, w, g, m, v):
        m = ADAM_B1 * m + (1.0 - ADAM_B1) * g
        v = ADAM_B2 * v + (1.0 - ADAM_B2) * (g * g)
        m_hat = m / (1.0 - ADAM_B1 ** ADAM_STEP)
        v_hat = v / (1.0 - ADAM_B2 ** ADAM_STEP)
        delta = -ADAM_LR * (m_hat / (jnp.sqrt(v_hat) + ADAM_EPS) + ADAM_WD * w)
        return [delta, m, v, g], []

    outs, _ = _rowwise(name, fn, [w, g, m, v], [], [(c, F32)] * 4, [], tr)
    return outs


RET_SCALE = HEAD_DIM ** -0.5
MLA_SCALE = (HEAD_DIM + MLA_ROPE) ** -0.5
GRAD_DT = BF16
IN_RET = 4 * HEADS * HEAD_DIM
IN_MLA = MLA_Q_RANK + MLA_KV_RANK + MLA_ROPE
IN_MLA_PAD = IN_MLA + 64


def _heads(fn):
    return jnp.concatenate([fn(h) for h in range(HEADS)], axis=1)


def _head(a, h, stride=HEAD_DIM, off=0):
    return a[:, h * stride + off:h * stride + off + HEAD_DIM]


def _group_norm(o):
    rs = [lax.rsqrt(jnp.mean(_head(o, h) * _head(o, h), axis=-1, keepdims=True) + EPS) for h in range(HEADS)]
    return _heads(lambda h: _head(o, h) * rs[h]), rs


class _Alone:
    def host(self, kernel_name):
        return None

    def done(self, kernel_name, w):
        pass

    def grads(self, g):
        pass


def _ffn_fwd(tag, n, w, k, tm, tmk, plan):
    gate, up, down = tag + "_gate", tag + "_up", tag + "_down"
    if "wu" + k in w:
        g, u, a = mm_nn(up, n, w["wg" + k], tm, outs=[BF16] * 3, w2=w["wu" + k],
                        epilogue=lambda g, u: (g, u, _silu(g) * u), host=plan.host(up))
    else:
        (g,) = mm_nn(gate, n, w["wg" + k], tm, out_dtype=BF16, host=plan.host(gate))
        plan.done(gate, w)
        u, a = mm_nn(up, n, w["wu" + k], tm, outs=[BF16] * 2, extras=(g,),
                     epilogue=lambda u, g: (u, _silu(g.astype(F32)) * u), host=plan.host(up))
    plan.done(up, w)
    ff = a.shape[1]
    wd = w["wd" + k]
    f = mm_nn_k(down, a, wd.reshape(ff, wd.shape[2]), tmk, _tile(ff, 1408), host=plan.host(down))
    plan.done(down, w)
    return g, u, a, f


def _ffn_bwd(tag, df, n, g, u, a, wg, wu, wd, tm, tmk, tk, dt, plan):
    ns = wg.shape[2]

    def gate_grads(da, g, u):
        g, u = g.astype(F32), u.astype(F32)
        return da * u * _dsilu(g), da * _silu(g)

    def hosted(kernel_name, call):
        out = call(plan.host(kernel_name))
        plan.done(kernel_name, None)
        return out

    k = tag[-1]
    dg, du = hosted(tag + "_da", lambda h: mm_nt(tag + "_da", [df], [wd], tm, ns, outs=(BF16, BF16),
                                                 epilogue=gate_grads, extras=(g, u), host=h))
    dwg = hosted(tag + "_dwg", lambda h: mm_tn(tag + "_dwg", n, dg, dt, ns, tk, GRAD_DT, shard_cols=True, host=h))
    plan.grads({"wg" + k: dwg})
    dwu = hosted(tag + "_dwu", lambda h: mm_tn(tag + "_dwu", n, du, dt, ns, tk, GRAD_DT, shard_cols=True, host=h))
    plan.grads({"wu" + k: dwu})
    dwd = hosted(tag + "_dwd", lambda h: mm_tn(tag + "_dwd", a, df, ns, dt, tk, GRAD_DT, shard_rows=True, host=h))
    plan.grads({"wd" + k: dwd})
    dn = hosted(tag + "_dn", lambda h: mm_nt_k(tag + "_dn", [dg, du], [wg, wu], tmk, dt, host=h))
    return dn


def _local_step(x, tgt, meta, w, nw, plan):
    seq, d = x.shape
    t_real = N_META + seq
    tp = -(-t_real // LANES) * LANES
    zpad = jnp.zeros((tp - t_real, d), F32)
    h0 = jnp.concatenate([meta, x, zpad], axis=0)
    tgt_p = jnp.concatenate([jnp.zeros((N_META, d), F32), tgt, zpad], axis=0)
    cos_r, sin_r, cos_m, sa, sb = _rope_tables(tp)
    tm = _tile(tp, 512)
    tmk = _tile(tp, 1408)
    tk = tp
    blk = tm
    dt = _tile(d, 1024)
    hw = HEADS * HEAD_DIM
    qw = HEADS * MLA_QK_PAD

    (n1,), _ = _rowwise("ffn1_norm", lambda r0, h, g: ([_rms(h, g)], []), [h0], [nw["ffn1_pre_norm"]],
                        [(d, BF16)], [], tm)
    g1, u1, a1, f1 = _ffn_fwd("ffn1", n1, w, "1", tm, tmk, plan)

    def post_ffn1(r0, h, f, post, pre):
        h1 = h + 0.5 * _rms(f, post)
        return [h1, _rms(h1, pre)], []

    (h1, un), _ = _rowwise("mix_norm", post_ffn1, [h0, f1], [nw["ffn1_post_norm"], nw["mix_pre_norm"]],
                           [(d, F32), (d, BF16)], [], tm)
    (proj_r,) = mm_nn("proj_r", un, w["w_r"], tm, n_block=_tile(IN_RET, 1024), host=plan.host("proj_r"))
    plan.done("proj_r", w)
    (proj_c,) = mm_nn("proj_c", un, w["w_c"], tm)

    def split_proj(r0, pr, pc, cr, sr, cm, ta, tb, qn, kvn):
        rq = _heads(lambda h: _rope_r(_head(pr, h), cr, sr))
        rk = _heads(lambda h: _rope_r(_head(pr, h, off=hw), cr, sr) * RET_SCALE)
        rv = pr[:, 2 * hw:3 * hw]
        cqn = _rms(pc[:, :MLA_Q_RANK], qn)
        ckvn = _rms(pc[:, MLA_Q_RANK:MLA_Q_RANK + MLA_KV_RANK], kvn)
        krr = _rope_m(pc[:, MLA_Q_RANK + MLA_KV_RANK:], cm, ta, tb)
        return [rq, rk, rv, cqn, ckvn, krr], []

    (rq, rk, rv, cqn, ckvn, krr), _ = _rowwise(
        "split_proj", split_proj, [proj_r, proj_c, cos_r, sin_r, cos_m, sa, sb],
        [nw["mla_q_norm"], nw["mla_kv_norm"]],
        [(hw, BF16), (hw, BF16), (hw, BF16), (MLA_Q_RANK, BF16), (MLA_KV_RANK, BF16), (LANES, F32)], [], tm)
    (qp,) = mm_nn("q_up", cqn, w["wuq"], tm)
    (kn,) = mm_nn("k_up", ckvn, w["wuk"], tm)
    (vv,) = mm_nn("v_up", ckvn, w["wuv"], tm, out_dtype=BF16)

    def build_qk(r0, qp, kn, krr, cm, ta, tb):
        qc = jnp.concatenate(
            [part for h in range(HEADS)
             for part in (_head(qp, h, MLA_QK_PAD), _rope_m(_head(qp, h, MLA_QK_PAD, HEAD_DIM), cm, ta, tb))], axis=1)
        kc = jnp.concatenate([part for h in range(HEADS) for part in (_head(kn, h), krr)], axis=1)
        return [qc, kc], []

    (qc, kc), _ = _rowwise("build_qk", build_qk, [qp, kn, krr, cos_m, sa, sb], [], [(qw, BF16), (qw, BF16)], [], tm)
    o_m, lse = attn_fwd("mla_fwd", qc, kc, vv, blk, MLA_SCALE, host=plan.host("mla_fwd"))
    plan.done("mla_fwd", w)
    o_r, ret_states = ret_fwd("ret_fwd", rq, rk, rv, blk, host=plan.host("ret_fwd"))
    plan.done("ret_fwd", w)

    def gate_mix(r0, rg, o_r, o_m, gn):
        y, _ = _group_norm(o_r)
        return [jnp.concatenate([_silu(rg) * (y * gn), o_m], axis=1)], []

    (mixcat,), _ = _rowwise("gate_mix", gate_mix, [(proj_r, hw, 3), o_r, o_m], [nw["ret_group_norm"]],
                            [(2 * hw, BF16)], [], tm)
    (mix,) = mm_nn("mix_out", mixcat, w["w_out"], tm, n_block=dt)

    def post_mix(r0, h, m, post, pre):
        h2 = h + _rms(m, post)
        return [h2, _rms(h2, pre)], []

    (h2, n3), _ = _rowwise("ffn2_norm", post_mix, [h1, mix], [nw["mix_post_norm"], nw["ffn2_pre_norm"]],
                           [(d, F32), (d, BF16)], [], tm)
    g2, u2, a2, f2 = _ffn_fwd("ffn2", n3, w, "2", tm, tmk, plan)

    def loss_head(r0, h, f, t, post):
        h3 = h + 0.5 * _rms(f, post)
        row = r0 + lax.broadcasted_iota(jnp.int32, h3.shape, 0)
        err = jnp.where(row >= N_META, jnp.where(row < t_real, h3 - t, 0.0), 0.0)
        dh3 = err / d
        df, dpost = _rms_bwd(f, post, 0.5 * dh3)
        return [dh3, df], [_colsum(err * err), _colsum(dpost)]

    (dh3, df2), (loss_vec, d_post2) = _rowwise("loss_head", loss_head, [h2, f2, tgt_p], [nw["ffn2_post_norm"]],
                                               [(d, F32), (d, BF16)], [d, d], tm)
    loss = 0.5 * jnp.sum(loss_vec) / d
    dn3 = _ffn_bwd("ffn2", df2, n3, g2, u2, a2, w["wg2"], w["wu2"], w["wd2"], tm, tmk, tk, dt, plan)

    def back_mix_norm(r0, h, m, dh3, dn, pre, post):
        dx, dpre = _rms_bwd(h, pre, dn)
        dh2 = dh3 + dx
        dm, dpost = _rms_bwd(m, post, dh2)
        return [dh2, dm], [_colsum(dpre), _colsum(dpost)]

    (dh2, dmix), (d_pre2, d_mix_post) = _rowwise(
        "back_mix_norm", back_mix_norm, [h2, mix, dh3, dn3], [nw["ffn2_pre_norm"], nw["mix_post_norm"]],
        [(d, F32), (d, BF16)], [d, d], tm)
    (dmixcat,) = mm_nt("mix_dx", [dmix], [w["w_out"]], tm, _tile(2 * hw, 512))
    dw_out = mm_tn("mix_dw", mixcat, dmix, 2 * hw // N_CHIPS, dt, tk, GRAD_DT, shard_rows=True)

    def back_gate(r0, dmc, rg, o_r, gn):
        d_ret, d_om = dmc[:, :hw], dmc[:, hw:]
        yh, rs = _group_norm(o_r)
        d_rg = d_ret * (yh * gn) * _dsilu(rg)
        dy = d_ret * _silu(rg)
        gyh = dy * gn
        d_or = _heads(lambda h: rs[h] * (_head(gyh, h) - _head(yh, h) * jnp.mean(_head(gyh, h) * _head(yh, h),
                                                                                    axis=-1, keepdims=True)))
        return [d_or, d_rg, d_om], [_colsum(dy * yh)]

    (d_or, d_rg, d_om), (d_gn,) = _rowwise("back_gate", back_gate, [dmixcat, (proj_r, hw, 3), o_r],
                                           [nw["ret_group_norm"]], [(hw, BF16), (hw, F32), (hw, BF16)], [hw], tm)
    dqc, dkc, dvv = attn_bwd("mla_bwd", qc, kc, vv, d_om, o_m, lse, blk, MLA_SCALE, host=plan.host("mla_bwd"))
    plan.done("mla_bwd", None)
    drq, drk, drv = ret_bwd("ret_bwd", rq, rk, rv, d_or, ret_states, blk)

    def back_qk(r0, dqc, dkc, dvv, cm, ta, tb):
        dqp = jnp.concatenate(
            [part for h in range(HEADS)
             for part in (_head(dqc, h, MLA_QK_PAD), _rope_m_t(_head(dqc, h, MLA_QK_PAD, HEAD_DIM), cm, ta, tb))],
            axis=1)
        dkn = _heads(lambda h: _head(dkc, h, MLA_QK_PAD))
        dkr = _head(dkc, 0, MLA_QK_PAD, HEAD_DIM)
        for h in range(1, HEADS):
            dkr = dkr + _head(dkc, h, MLA_QK_PAD, HEAD_DIM)
        return [dqp, dkn, _rope_m_t(dkr, cm, ta, tb), dvv], []

    (dqp, dkn, dkr, dvb), _ = _rowwise("back_qk", back_qk, [dqc, dkc, dvv, cos_m, sa, sb], [],
                                       [(qw, BF16), (hw, BF16), (LANES, F32), (hw, BF16)], [], tm)
    (dcqn,) = mm_nt("q_dx", [dqp], [w["wuq"]], tm, MLA_Q_RANK)
    dwuq = mm_tn("q_dw", cqn, dqp, MLA_Q_RANK, _tile(qw, 1024), tk, GRAD_DT)
    (dckvn,) = mm_nt("kv_dx", [dkn, dvb], [w["wuk"], w["wuv"]], tm, MLA_KV_RANK)
    dwuk = mm_tn("k_dw", ckvn, dkn, MLA_KV_RANK, hw, tk, GRAD_DT)
    dwuv = mm_tn("v_dw", ckvn, dvb, MLA_KV_RANK, hw, tk, GRAD_DT)

    def back_proj(r0, drq, drk, drv, d_rg, pc, dcqn, dckvn, dkr, cr, sr, qn, kvn):
        d_q = _heads(lambda h: _rope_r_t(_head(drq, h), cr, sr))
        d_k = _heads(lambda h: _rope_r_t(_head(drk, h), cr, sr) * RET_SCALE)
        dcq, a_q = _rms_bwd(pc[:, :MLA_Q_RANK], qn, dcqn)
        dckv, a_kv = _rms_bwd(pc[:, MLA_Q_RANK:MLA_Q_RANK + MLA_KV_RANK], kvn, dckvn)
        return ([jnp.concatenate([d_q, d_k, drv, d_rg], axis=1), jnp.concatenate([dcq, dckv, dkr], axis=1)],
                [_colsum(a_q), _colsum(a_kv)])

    (dproj_r, dproj_c), (d_qn, d_kvn) = _rowwise(
        "back_proj", back_proj, [drq, drk, drv, d_rg, proj_c, dcqn, dckvn, dkr, cos_r, sin_r],
        [nw["mla_q_norm"], nw["mla_kv_norm"]], [(IN_RET, BF16), (IN_MLA_PAD, BF16)],
        [MLA_Q_RANK, MLA_KV_RANK], tm)
    (dun,) = mm_nt("proj_dx", [dproj_r, dproj_c], [w["w_r"], w["w_c"]], tm, _tile(d, 512))
    dw_r = mm_tn("proj_dw_r", un, dproj_r, dt, _tile(IN_RET, 1024), tk, GRAD_DT)
    dw_c = mm_tn("proj_dw_c", un, dproj_c, dt, IN_MLA_PAD, tk, GRAD_DT)
    plan.grads(dict(w_r=dw_r, w_c=dw_c, wuq=dwuq, wuk=dwuk, wuv=dwuv, w_out=dw_out))

    def back_ffn1_norm(r0, h, f, dh2, dn, pre, post):
        dx, dpre = _rms_bwd(h, pre, dn)
        dh1 = dh2 + dx
        df, dpost = _rms_bwd(f, post, 0.5 * dh1)
        return [dh1, df], [_colsum(dpre), _colsum(dpost)]

    (dh1, df1), (d_mix_pre, d_post1) = _rowwise(
        "back_ffn1_norm", back_ffn1_norm, [h1, f1, dh2, dun], [nw["mix_pre_norm"], nw["ffn1_post_norm"]],
        [(d, F32), (d, BF16)], [d, d], tm)
    dn1 = _ffn_bwd("ffn1", df1, n1, g1, u1, a1, w["wg1"], w["wu1"], w["wd1"], tm, tmk, tk, dt, plan)

    def back_input(r0, h, dh1, dn, pre):
        dx, dpre = _rms_bwd(h, pre, dn)
        return [dh1 + dx], [_colsum(dpre)]

    (dh0,), (d_pre1,) = _rowwise("back_input", back_input, [h0, dh1, dn1], [nw["ffn1_pre_norm"]], [(d, F32)], [d], tm)

    small = dict(ffn1_pre_norm=d_pre1, ffn1_post_norm=d_post1, mix_pre_norm=d_mix_pre, ret_group_norm=d_gn,
                 mla_q_norm=d_qn, mla_kv_norm=d_kvn, mix_post_norm=d_mix_post, ffn2_pre_norm=d_pre2,
                 ffn2_post_norm=d_post2)
    return loss, dh0[N_META:t_real], small, dh0[:N_META]


WEIGHTS = ("meta_tokens", "ffn1_pre_norm", "ffn1_w_gate", "ffn1_w_up", "ffn1_w_down", "ffn1_post_norm",
           "mix_pre_norm", "w_in", "ret_group_norm", "mla_q_norm", "mla_w_uq", "mla_kv_norm", "mla_w_uk",
           "mla_w_uv", "w_out", "mix_post_norm", "ffn2_pre_norm", "ffn2_w_gate", "ffn2_w_up", "ffn2_w_down",
           "ffn2_post_norm")
BIG = ("ffn1_w_gate", "ffn1_w_up", "ffn1_w_down", "w_in", "mla_w_uq", "mla_w_uk", "mla_w_uv", "w_out",
       "ffn2_w_gate", "ffn2_w_up", "ffn2_w_down")
NORMS = ("ffn1_pre_norm", "ffn1_post_norm", "mix_pre_norm", "ret_group_norm", "mla_q_norm", "mla_kv_norm",
         "mix_post_norm", "ffn2_pre_norm", "ffn2_post_norm")


def _unshard_cols(g):
    return g.transpose(1, 0, 2).reshape(g.shape[1], -1)


def _shard_cols(a):
    return a.reshape(a.shape[0], N_CHIPS, -1).transpose(1, 0, 2)


def _pack_rows(rows, width):
    rows = [jnp.pad(r, ((0, 0), (0, width - r.shape[1]))) for r in rows]
    n = sum(r.shape[0] for r in rows)
    return jnp.pad(jnp.concatenate(rows, axis=0), ((0, -n % 8), (0, 0)))


def _weight_views(full):
    w = {}
    for n, g in full.items():
        if n == "w_in":
            w_in = _unshard_cols(g)
            w["w_r"] = w_in[:, :IN_RET]
            w["w_c"] = jnp.pad(w_in[:, IN_RET:], ((0, 0), (0, IN_MLA_PAD - IN_MLA)))
        elif n == "mla_w_uq":
            q = _unshard_cols(g).reshape(MLA_Q_RANK, HEADS, HEAD_DIM + MLA_ROPE)
            q = jnp.pad(q, ((0, 0), (0, 0), (0, MLA_QK_PAD - HEAD_DIM - MLA_ROPE)))
            w["wuq"] = q.reshape(MLA_Q_RANK, HEADS * MLA_QK_PAD)
        elif n in ("mla_w_uk", "mla_w_uv"):
            w["wu" + n[-1]] = _unshard_cols(g)
        elif n == "w_out":
            w["w_out"] = g.reshape(-1, g.shape[2])
        else:
            w["w" + n[7] + n[3]] = g
    return w


def _contributions(g):
    ffn = {"g": "gate", "u": "up", "d": "down"}
    c = {f"ffn{n[2]}_w_{ffn[n[1]]}": a for n, a in g.items() if len(n) == 3 and n[2] in "12"}
    if "w_r" in g:
        dwuq = g["wuq"].reshape(MLA_Q_RANK, HEADS, MLA_QK_PAD)[:, :, :HEAD_DIM + MLA_ROPE]
        c.update(w_in=_shard_cols(jnp.concatenate([g["w_r"], g["w_c"][:, :IN_MLA]], axis=1)),
                 mla_w_uq=_shard_cols(dwuq.reshape(MLA_Q_RANK, -1)), mla_w_uk=_shard_cols(g["wuk"]),
                 mla_w_uv=_shard_cols(g["wuv"]), w_out=g["w_out"])
    return c


class _Schedule(_Alone):
    FIRST = ("ffn1_w_gate",)
    CARRIED = {"ffn1_gate": ("ffn1_w_up",), "ffn1_up": ("ffn1_w_down",),
               "ffn1_down": ("w_in", "mla_w_uq", "mla_w_uk", "mla_w_uv"),
               "proj_r": ("w_out",), "mla_fwd": ("ffn2_w_gate", "ffn2_w_up"), "ret_fwd": ("ffn2_w_down",)}
    GRAD_HOST = dict(ffn2_w_gate="mla_bwd", ffn2_w_up="mla_bwd", ffn2_w_down="mla_bwd",
                     w_in="ffn1_da", mla_w_uq="ffn1_da", mla_w_uk="ffn1_da", mla_w_uv="ffn1_da", w_out="ffn1_da",
                     ffn1_w_gate="ffn1_dwu", ffn1_w_up="ffn1_dwd", ffn1_w_down="ffn1_dn")

    def __init__(self, shards):
        self.gathers = {k: (gather_halves([shards[n] for n in names]), names) for k, names in self.CARRIED.items()}
        self.waiting = {}
        self.exchanges = {}
        self.grad = {}

    def host(self, kernel_name):
        if kernel_name in self.gathers:
            return self.gathers[kernel_name][0]
        if kernel_name in self.waiting:
            names, sums = zip(*self.waiting.pop(kernel_name))
            self.exchanges[kernel_name] = (chip_exchange(list(sums)), names)
            return self.exchanges[kernel_name][0]
        return None

    def done(self, kernel_name, w):
        if kernel_name in self.gathers:
            comm, names = self.gathers[kernel_name]
            w.update(_weight_views({n: sibling_fill("fill_" + n, b) for n, b in zip(names, comm.result)}))
        elif kernel_name in self.exchanges:
            comm, names = self.exchanges[kernel_name]
            for n, q in zip(names, comm.result):
                self.grad[n] = reduce_join("reduce_join_" + n, q)

    def grads(self, g):
        for n, a in _contributions(g).items():
            self.waiting.setdefault(self.GRAD_HOST[n], []).append((n, pair_add("pair_add_" + n, a)))


def _step(p, m, v, x, loss_target):
    d = x.shape[2]
    names = list(BIG)
    shards = {n: p[n][0].astype(BF16) for n in names}
    first = _Schedule.FIRST
    gathered = gather_halves([shards[n] for n in first] + [p["meta_tokens"]]).run("gather_first")
    filled = [sibling_fill("fill_" + n, b) for n, b in zip(first + ("meta_tokens",), gathered)]
    w = _weight_views(dict(zip(first, filled[:-1])))
    meta = _unshard_cols(filled[-1])
    nw = {n: p[n] for n in NORMS}
    plan = _Schedule(shards)
    loss, grad_x, small, d_meta = _local_step(x[0], loss_target[0], meta, w, nw, plan)
    grads = dict(plan.grad)

    width = max(d, HEADS * HEAD_DIM)
    packed = _pack_rows([small[n] for n in NORMS] + [d_meta], width)
    total = sum_slots("small_sum", all_gather_devices(packed), F32)
    for i, n in enumerate(NORMS):
        grads[n] = total[i:i + 1, :p[n].shape[1]]
    cols = p["meta_tokens"].shape[1]
    chip = 2 * lax.axis_index("x") + lax.axis_index("y")
    grads["meta_tokens"] = lax.dynamic_slice(total[len(NORMS):len(NORMS) + N_META, :d], (0, chip * cols), (N_META, cols))

    delta, new_m, new_v = {}, {}, {}
    for n in names + ["meta_tokens"]:
        shape = p[n].shape
        flat = lambda a: a.reshape(-1, shape[-1])
        out = adamw("adamw_" + n, flat(p[n]), flat(grads[n]), flat(m[n]), flat(v[n]))
        delta[n], new_m[n], new_v[n], grads[n] = (o.reshape(shape) for o in out)
    pk = lambda src: _pack_rows([src[n] for n in NORMS], width)
    out = adamw("adamw_norms", pk(p), pk(grads), pk(m), pk(v))
    for i, n in enumerate(NORMS):
        delta[n], new_m[n], new_v[n] = (o[i:i + 1, :p[n].shape[1]] for o in out[:3])

    loss = lax.psum(loss, ("x", "y", "c"))
    return loss, grad_x[None], grads, delta, new_m, new_v


def kernel(x, meta_tokens, ffn1_pre_norm, ffn1_w_gate, ffn1_w_up, ffn1_w_down, ffn1_post_norm, mix_pre_norm, w_in, ret_group_norm, mla_q_norm, mla_w_uq, mla_kv_norm, mla_w_uk, mla_w_uv, w_out, mix_post_norm, ffn2_pre_norm, ffn2_w_gate, ffn2_w_up, ffn2_w_down, ffn2_post_norm, loss_target, m_meta_tokens, m_ffn1_pre_norm, m_ffn1_w_gate, m_ffn1_w_up, m_ffn1_w_down, m_ffn1_post_norm, m_mix_pre_norm, m_w_in, m_ret_group_norm, m_mla_q_norm, m_mla_w_uq, m_mla_kv_norm, m_mla_w_uk, m_mla_w_uv, m_w_out, m_mix_post_norm, m_ffn2_pre_norm, m_ffn2_w_gate, m_ffn2_w_up, m_ffn2_w_down, m_ffn2_post_norm, v_meta_tokens, v_ffn1_pre_norm, v_ffn1_w_gate, v_ffn1_w_up, v_ffn1_w_down, v_ffn1_post_norm, v_mix_pre_norm, v_w_in, v_ret_group_norm, v_mla_q_norm, v_mla_w_uq, v_mla_kv_norm, v_mla_w_uk, v_mla_w_uv, v_w_out, v_mix_post_norm, v_ffn2_pre_norm, v_ffn2_w_gate, v_ffn2_w_up, v_ffn2_w_down, v_ffn2_post_norm):
    args = locals()
    p = {n: args[n] for n in WEIGHTS}
    m = {n: args["m_" + n] for n in WEIGHTS}
    v = {n: args["v_" + n] for n in WEIGHTS}
    loss, grad_x, grads, delta, new_m, new_v = _step(p, m, v, x, loss_target)
    return (loss, grad_x, *[grads[n] for n in WEIGHTS], *[delta[n] for n in WEIGHTS],
            *[new_m[n] for n in WEIGHTS], *[new_v[n] for n in WEIGHTS])
```

```python
import functools
import math

import jax
import jax.numpy as jnp
from jax import lax
from jax.experimental import pallas as pl
from jax.experimental.pallas import tpu as pltpu

F32 = jnp.float32
BF16 = jnp.bfloat16

EPS = 1e-6
N_META = 16
HEADS = 8
HEAD_DIM = 128
MLA_ROPE = 64
MLA_QK_PAD = 256
MLA_Q_RANK = 512
MLA_KV_RANK = 256
ROPE_THETA = 10000.0
N_CHIPS = 4
LANES = 128
VMEM_LIMIT = 60 * 2 ** 20

ADAM_LR = 0.001
ADAM_B1 = 0.9
ADAM_B2 = 0.999
ADAM_EPS = 1e-08
ADAM_WD = 0.01
ADAM_STEP = 10

NN = (((1,), (0,)), ((), ()))
NT = (((1,), (1,)), ((), ()))
TN = (((0,), (0,)), ((), ()))
MESH = pl.DeviceIdType.MESH


def _tile(n, pref, align=LANES):
    if n <= pref:
        return n
    best = 0
    for t in range(align, pref + 1, align):
        if n % t == 0:
            best = t
    assert best, (n, pref)
    return best


def _params(*sem):
    return pltpu.CompilerParams(dimension_semantics=sem, vmem_limit_bytes=VMEM_LIMIT)


def _sds(shape, dtype):
    return jax.ShapeDtypeStruct(tuple(shape), dtype)


def _rowwise(name, fn, rows, consts, outs, accs, tr):
    rows = [r if isinstance(r, tuple) else (r, r.shape[1], 0) for r in rows]
    t = rows[0][0].shape[0]
    assert t % tr == 0
    n_r, n_c, n_o = len(rows), len(consts), len(outs)

    def body(*refs):
        i = pl.program_id(0)
        r = [x[...] for x in refs[:n_r]]
        c = [x[...] for x in refs[n_r:n_r + n_c]]
        o_refs = refs[n_r + n_c:n_r + n_c + n_o]
        a_refs = refs[n_r + n_c + n_o:]
        o_vals, a_vals = fn(i * tr, *r, *c)
        for ref, v in zip(o_refs, o_vals):
            ref[...] = v.astype(ref.dtype)
        if a_refs:
            @pl.when(i == 0)
            def _():
                for ref, v in zip(a_refs, a_vals):
                    ref[...] = v

            @pl.when(i > 0)
            def _():
                for ref, v in zip(a_refs, a_vals):
                    ref[...] += v

    in_specs = [pl.BlockSpec((tr, w), functools.partial(lambda cb, i: (i, cb), cb)) for _, w, cb in rows]
    in_specs += [pl.BlockSpec(a.shape, lambda i: (0, 0)) for a in consts]
    out_specs = [pl.BlockSpec((tr, w), lambda i: (i, 0)) for w, _ in outs]
    out_specs += [pl.BlockSpec((1, w), lambda i: (0, 0)) for w in accs]
    out_shape = [_sds((t, w), dt) for w, dt in outs] + [_sds((1, w), F32) for w in accs]
    res = pl.pallas_call(
        body, name=name, grid=(t // tr,), in_specs=in_specs, out_specs=out_specs, out_shape=out_shape,
        compiler_params=_params("arbitrary"),
    )(*[a for a, _, _ in rows], *consts)
    return res[:n_o], res[n_o:]


def _rms(x, w):
    r = lax.rsqrt(jnp.mean(x * x, axis=-1, keepdims=True) + EPS)
    return x * r * w


def _rms_bwd(x, w, dy):
    r = lax.rsqrt(jnp.mean(x * x, axis=-1, keepdims=True) + EPS)
    xh = x * r
    gy = dy * w
    dx = r * (gy - xh * jnp.mean(gy * xh, axis=-1, keepdims=True))
    return dx, dy * xh


def _colsum(v):
    return jnp.sum(v, axis=0, keepdims=True)


def _silu(x):
    return x * jax.nn.sigmoid(x)


def _dsilu(x):
    s = jax.nn.sigmoid(x)
    return s * (1.0 + x * (1.0 - s))


def _rope_r(x, cos, sin):
    return x * cos + pltpu.roll(x, 64, 1) * sin


def _rope_r_t(dy, cos, sin):
    return dy * cos + pltpu.roll(dy * sin, 64, 1)


def _rope_m(x, cos, sa, sb):
    return x * cos + pltpu.roll(x, 32, 1) * sa + pltpu.roll(x, 96, 1) * sb


def _rope_m_t(dy, cos, sa, sb):
    return dy * cos + pltpu.roll(dy * sa, 96, 1) + pltpu.roll(dy * sb, 32, 1)


def _rope_tables(t):
    pos = jnp.arange(t, dtype=F32)

    def cs(dim):
        inv = ROPE_THETA ** (-jnp.arange(0, dim, 2, dtype=F32) / dim)
        ang = pos[:, None] * inv[None, :]
        return jnp.cos(ang), jnp.sin(ang)

    c, s = cs(HEAD_DIM)
    cos_r = jnp.concatenate([c, c], axis=1)
    sin_r = jnp.concatenate([-s, s], axis=1)
    c, s = cs(MLA_ROPE)
    z32, z64 = jnp.zeros_like(s), jnp.zeros((t, 64), F32)
    cos_m = jnp.concatenate([c, c, z64], axis=1)
    sa = jnp.concatenate([z32, s, z64], axis=1)
    sb = jnp.concatenate([-s, z32, z64], axis=1)
    return cos_r, sin_r, cos_m, sa, sb


def _call(body, name, grid, in_specs, out_specs, out_shape, scratch, operands, host=None):
    sem = ("arbitrary",) * len(grid)
    if host is None:
        return pl.pallas_call(body, name=name, grid=grid, in_specs=in_specs, out_specs=out_specs, out_shape=out_shape,
                              scratch_shapes=scratch, compiler_params=_params(*sem))(*operands)
    n_in, n_out, n_s = len(in_specs), len(out_shape), len(scratch)
    h_in, h_out = len(host.ins), len(host.out_shape)

    def hosted(*refs):
        a = n_in
        b = a + h_in
        c = b + n_out
        d = c + h_out
        e = d + n_s
        ids = [pl.program_id(i) for i in range(len(grid))]
        first = functools.reduce(jnp.logical_and, [i == 0 for i in ids])
        last = functools.reduce(jnp.logical_and, [i == g - 1 for i, g in zip(ids, grid)])
        comm = (refs[a:b], refs[c:d], refs[e:])

        @pl.when(first)
        def _():
            host.start(*comm)

        body(*refs[:a], *refs[b:c], *refs[d:e])

        @pl.when(last)
        def _():
            host.finish(*comm)

    hbm = pl.BlockSpec(memory_space=pl.ANY)
    res = pl.pallas_call(
        hosted, name=name, grid=grid, in_specs=list(in_specs) + [hbm] * h_in, out_specs=list(out_specs) + [hbm] * h_out,
        out_shape=list(out_shape) + list(host.out_shape), scratch_shapes=list(scratch) + host.scratch(),
        compiler_params=_params(*sem))(*operands, *host.ins)
    host.result = res[n_out:]
    return res[:n_out]


def _mm(name, grid, operands, in_specs, dns, out_specs, out_shape, epilogue=None, extras=(), extra_specs=(),
        acc_shape=None, host=None):
    n_p, n_e = len(dns), len(extras)
    nk = grid[2]
    n_o = len(out_shape)
    in_place = nk > 1 and epilogue is None and n_o == 1 and out_shape[0].dtype == F32

    def body(*refs):
        ab = refs[:2 * n_p]
        ex = refs[2 * n_p:2 * n_p + n_e]
        outs = refs[2 * n_p + n_e:2 * n_p + n_e + n_o]

        part = None
        for p in range(n_p):
            d = lax.dot_general(ab[2 * p][...], ab[2 * p + 1][...], dns[p], preferred_element_type=F32)
            part = d if part is None else part + d

        def finish(acc):
            vals = (acc,) if epilogue is None else epilogue(acc, *[e[...] for e in ex])
            for o, v in zip(outs, vals):
                o[...] = v.astype(o.dtype)

        if nk == 1:
            finish(part)
        else:
            acc_ref = outs[0] if in_place else refs[2 * n_p + n_e + n_o]
            k = pl.program_id(2)

            @pl.when(k == 0)
            def _():
                acc_ref[...] = part

            @pl.when(k > 0)
            def _():
                acc_ref[...] += part

            if not in_place:
                @pl.when(k == nk - 1)
                def _():
                    finish(acc_ref[...])

    scratch = [] if nk == 1 or in_place else [pltpu.VMEM(acc_shape, F32)]
    return _call(body, name, grid, list(in_specs) + list(extra_specs), out_specs, out_shape, scratch,
                 [*operands, *extras], host)


def mm_nn(name, x, w, tm, out_dtype=F32, epilogue=None, outs=None, w2=None, n_block=None, extras=(), host=None):
    t, kdim = x.shape
    if w.ndim == 3:
        s, _, ns = w.shape
        n, tn, nb = s * ns, ns, s
        wspec = pl.BlockSpec((None, kdim, ns), lambda j, i, k: (j, 0, 0))
    else:
        n = w.shape[1]
        tn = n_block or n
        nb = n // tn
        wspec = pl.BlockSpec((kdim, tn), lambda j, i, k: (0, j))
    xspec = pl.BlockSpec((tm, kdim), lambda j, i, k: (i, 0))
    ospec = pl.BlockSpec((tm, tn), lambda j, i, k: (i, j))
    outs = outs or [out_dtype]
    grid = (nb, t // tm, 1)
    if w2 is None:
        return _mm(name, grid, [x, w], [xspec, wspec], [NN], [ospec] * len(outs), [_sds((t, n), d) for d in outs],
                   epilogue=epilogue, extras=extras, extra_specs=[ospec] * len(extras), host=host)

    def body(x_ref, w_ref, w2_ref, *o_refs):
        xv = x_ref[...]
        a = jnp.dot(xv, w_ref[...], preferred_element_type=F32)
        b = jnp.dot(xv, w2_ref[...], preferred_element_type=F32)
        for o, v in zip(o_refs, epilogue(a, b)):
            o[...] = v.astype(o.dtype)

    return _call(body, name, grid[:2],
                 [pl.BlockSpec((tm, kdim), lambda j, i: (i, 0)),
                  pl.BlockSpec((None, kdim, tn), lambda j, i: (j, 0, 0)),
                  pl.BlockSpec((None, kdim, tn), lambda j, i: (j, 0, 0))],
                 [pl.BlockSpec((tm, tn), lambda j, i: (i, j))] * len(outs), [_sds((t, n), d) for d in outs], [],
                 [x, w, w2], host)


def mm_nn_k(name, x, w, tm, tk, out_dtype=F32, host=None):
    t, kdim = x.shape
    n = w.shape[1]
    grid = (t // tm, 1, kdim // tk)
    return _mm(name, grid, [x, w],
               [pl.BlockSpec((tm, tk), lambda i, j, k: (i, k)), pl.BlockSpec((tk, n), lambda i, j, k: (k, 0))],
               [NN], [pl.BlockSpec((tm, n), lambda i, j, k: (i, 0))], [_sds((t, n), out_dtype)],
               acc_shape=(tm, n), host=host)[0]


def mm_nt(name, xs, ws, tm, tn, outs=(F32,), epilogue=None, extras=(), host=None):
    t = xs[0].shape[0]
    specs, ops = [], []
    for x, w in zip(xs, ws):
        kdim = x.shape[1]
        specs.append(pl.BlockSpec((tm, kdim), lambda j, i, k: (i, 0)))
        if w.ndim == 3:
            assert tn == w.shape[1]
            n = w.shape[0] * w.shape[1]
            specs.append(pl.BlockSpec((None, tn, kdim), lambda j, i, k: (j, 0, 0)))
        else:
            n = w.shape[0]
            specs.append(pl.BlockSpec((tn, kdim), lambda j, i, k: (j, 0)))
        ops += [x, w]
    ospec = pl.BlockSpec((tm, tn), lambda j, i, k: (i, j))
    return _mm(name, (n // tn, t // tm, 1), ops, specs, [NT] * len(xs), [ospec] * len(outs),
               [_sds((t, n), d) for d in outs], epilogue=epilogue, extras=extras,
               extra_specs=[ospec] * len(extras), host=host)


def mm_nt_k(name, xs, ws, tm, tn, out_dtype=F32, host=None):
    t = xs[0].shape[0]
    s, n, ns = ws[0].shape
    specs, ops = [], []
    for x, w in zip(xs, ws):
        specs.append(pl.BlockSpec((tm, ns), lambda i, j, k: (i, k)))
        specs.append(pl.BlockSpec((None, tn, ns), lambda i, j, k: (k, j, 0)))
        ops += [x, w]
    return _mm(name, (t // tm, n // tn, s), ops, specs, [NT] * len(xs),
               [pl.BlockSpec((tm, tn), lambda i, j, k: (i, j))], [_sds((t, n), out_dtype)], acc_shape=(tm, tn),
               host=host)[0]


def mm_tn(name, x, y, tm, tn, tk, out_dtype, shard_rows=False, shard_cols=False, host=None):
    t, m = x.shape
    n = y.shape[1]
    grid = (m // tm, n // tn, t // tk)
    if shard_cols:
        ospec = pl.BlockSpec((None, tm, tn), lambda i, j, k: (j, i, 0))
        oshape = _sds((n // tn, m, tn), out_dtype)
    elif shard_rows:
        ospec = pl.BlockSpec((None, tm, tn), lambda i, j, k: (i, 0, j))
        oshape = _sds((m // tm, tm, n), out_dtype)
    else:
        ospec = pl.BlockSpec((tm, tn), lambda i, j, k: (i, j))
        oshape = _sds((m, n), out_dtype)
    return _mm(name, grid, [x, y],
               [pl.BlockSpec((tk, tm), lambda i, j, k: (k, i)), pl.BlockSpec((tk, tn), lambda i, j, k: (k, j))],
               [TN], [ospec], [oshape], acc_shape=(tm, tn), host=host)[0]


def _decay_logs():
    return [math.log(1.0 - 2.0 ** (-5.0 - h)) for h in range(HEADS)]


def _log_decay(h):
    lg = jnp.float32(_decay_logs()[0])
    for i in range(1, HEADS):
        lg = jnp.where(h == i, jnp.float32(_decay_logs()[i]), lg)
    return lg


def _decayed_scores(q, k, lg):
    s = lax.dot_general(q, k, NT, preferred_element_type=F32)
    row = lax.broadcasted_iota(jnp.int32, s.shape, 0)
    col = lax.broadcasted_iota(jnp.int32, s.shape, 1)
    dec = jnp.where(col <= row, jnp.exp(jnp.maximum(row - col, 0).astype(F32) * lg), 0.0)
    return s * dec, dec


def _causal(s):
    row = lax.broadcasted_iota(jnp.int32, s.shape, 0)
    col = lax.broadcasted_iota(jnp.int32, s.shape, 1)
    return jnp.where(col <= row, s, -1e30)


def attn_fwd(name, q, k, v, blk, scale, host=None):
    t = q.shape[0]
    dq = q.shape[1] // HEADS
    nq = t // blk

    def body(q_ref, k_ref, v_ref, o_ref, lse_ref):
        qi = pl.program_id(1)
        qv = q_ref[...]

        def block(ki, carry, diag):
            m, l, acc = carry
            rows = pl.ds(pl.multiple_of(ki * blk, blk), blk)
            s = lax.dot_general(qv, k_ref[rows, :], NT, preferred_element_type=F32) * scale
            if diag:
                s = _causal(s)
            m_new = jnp.maximum(m, jnp.max(s, axis=-1, keepdims=True))
            p = jnp.exp(s - m_new)
            alpha = jnp.exp(m - m_new)
            return (m_new, alpha * l + jnp.sum(p, axis=-1, keepdims=True),
                    alpha * acc + jnp.dot(p.astype(BF16), v_ref[rows, :], preferred_element_type=F32))

        init = (jnp.full((blk, 1), -1e30, F32), jnp.zeros((blk, 1), F32), jnp.zeros((blk, HEAD_DIM), F32))
        carry = lax.fori_loop(0, qi, lambda ki, c: block(ki, c, False), init)
        m, l, acc = block(qi, carry, True)
        o_ref[...] = acc / l
        lse_ref[...] = jnp.broadcast_to(m + jnp.log(l), (blk, HEAD_DIM))

    hspec = pl.BlockSpec((blk, HEAD_DIM), lambda h, i: (i, h))
    return _call(body, name, (HEADS, nq),
                 [pl.BlockSpec((blk, dq), lambda h, i: (i, h)),
                  pl.BlockSpec((t, dq), lambda h, i: (0, h)),
                  pl.BlockSpec((t, HEAD_DIM), lambda h, i: (0, h))],
                 [hspec, hspec], [_sds((t, HEADS * HEAD_DIM), F32)] * 2, [], [q, k, v], host)


def attn_bwd(name, q, k, v, do, o, lse, blk, scale, host=None):
    t = q.shape[0]
    dq_w = q.shape[1] // HEADS
    nb = t // blk

    def body(q_ref, k_ref, v_ref, do_ref, o_ref, lse_ref, dq_ref, dk_ref, dv_ref):
        ki = pl.program_id(1)
        kv = k_ref[...]
        vv = v_ref[...]

        @pl.when(ki == 0)
        def _():
            dq_ref[...] = jnp.zeros_like(dq_ref)

        def block(qi, carry, diag):
            dk, dv = carry
            rows = pl.ds(pl.multiple_of(qi * blk, blk), blk)
            qv, dov = q_ref[rows, :], do_ref[rows, :]
            s = lax.dot_general(qv, kv, NT, preferred_element_type=F32) * scale
            if diag:
                s = _causal(s)
            p = jnp.exp(s - lse_ref[rows, :][:, :1])
            dp = lax.dot_general(dov, vv, NT, preferred_element_type=F32)
            delta = jnp.sum(dov.astype(F32) * o_ref[rows, :], axis=-1, keepdims=True)
            ds = p * (dp - delta) * scale
            pb, dsb = p.astype(BF16), ds.astype(BF16)
            dv = dv + lax.dot_general(pb, dov, TN, preferred_element_type=F32)
            dk = dk + lax.dot_general(dsb, qv, TN, preferred_element_type=F32)
            dq_ref[rows, :] += jnp.dot(dsb, kv, preferred_element_type=F32)
            return dk, dv

        carry = block(ki, (jnp.zeros((blk, dq_w), F32), jnp.zeros((blk, HEAD_DIM), F32)), True)
        dk, dv = lax.fori_loop(ki + 1, nb, lambda qi, c: block(qi, c, False), carry)
        dk_ref[...] = dk
        dv_ref[...] = dv

    full = lambda w: pl.BlockSpec((t, w), lambda h, j: (0, h))
    blkd = lambda w: pl.BlockSpec((blk, w), lambda h, j: (j, h))
    return _call(body, name, (HEADS, nb),
                 [full(dq_w), blkd(dq_w), blkd(HEAD_DIM), full(HEAD_DIM), full(HEAD_DIM), full(HEAD_DIM)],
                 [full(dq_w), blkd(dq_w), blkd(HEAD_DIM)],
                 [_sds(q.shape, F32), _sds(k.shape, F32), _sds(v.shape, F32)], [], [q, k, v, do, o, lse], host)


def _chunk_decays(lg, blk):
    row = lax.broadcasted_iota(jnp.int32, (blk, HEAD_DIM), 0).astype(F32)
    return jnp.exp(lg * (row + 1.0)), jnp.exp(lg * (blk - 1.0 - row)), jnp.exp(lg * blk * jnp.ones((1, HEAD_DIM), F32))


def ret_fwd(name, q, k, v, blk, host=None):
    t = q.shape[0]
    nb = t // blk

    def body(q_ref, k_ref, v_ref, o_ref, st_ref, state):
        h, i = pl.program_id(0), pl.program_id(1)
        lg = _log_decay(h)

        @pl.when(i == 0)
        def _():
            state[...] = jnp.zeros_like(state)

        qv, kv, vv = q_ref[...], k_ref[...], v_ref[...]
        before = state[...]
        st_ref[...] = before
        p, _ = _decayed_scores(qv, kv, lg)
        xi, zeta, g_blk = _chunk_decays(lg, blk)
        o_ref[...] = (jnp.dot(p.astype(BF16), vv, preferred_element_type=F32)
                      + jnp.dot(qv, before.astype(BF16), preferred_element_type=F32) * xi)
        kz = (kv.astype(F32) * zeta).astype(BF16)
        state[...] = before * g_blk + lax.dot_general(kz, vv, TN, preferred_element_type=F32)

    hspec = pl.BlockSpec((blk, HEAD_DIM), lambda h, i: (i, h))
    return _call(body, name, (HEADS, nb), [hspec] * 3,
                 [hspec, pl.BlockSpec((HEAD_DIM, HEAD_DIM), lambda h, i: (i, h))],
                 [_sds((t, HEADS * HEAD_DIM), F32), _sds((nb * HEAD_DIM, HEADS * HEAD_DIM), F32)],
                 [pltpu.VMEM((HEAD_DIM, HEAD_DIM), F32)], [q, k, v], host)


def ret_bwd(name, q, k, v, do, states, blk, host=None):
    t = q.shape[0]
    nb = t // blk

    def body(q_ref, k_ref, v_ref, do_ref, st_ref, dq_ref, dk_ref, dv_ref, dstate):
        h, i = pl.program_id(0), pl.program_id(1)
        lg = _log_decay(h)

        @pl.when(i == 0)
        def _():
            dstate[...] = jnp.zeros_like(dstate)

        qv, kv, vv, dov = q_ref[...], k_ref[...], v_ref[...], do_ref[...]
        before = st_ref[...].astype(BF16)
        after_grad = dstate[...]
        p, dec = _decayed_scores(qv, kv, lg)
        ds = lax.dot_general(dov, vv, NT, preferred_element_type=F32) * dec
        pb, dsb = p.astype(BF16), ds.astype(BF16)
        xi, zeta, g_blk = _chunk_decays(lg, blk)
        dox = (dov.astype(F32) * xi).astype(BF16)
        kz = (kv.astype(F32) * zeta).astype(BF16)
        agb = after_grad.astype(BF16)
        dv_ref[...] = (lax.dot_general(pb, dov, TN, preferred_element_type=F32)
                       + jnp.dot(kz, agb, preferred_element_type=F32))
        dq_ref[...] = (jnp.dot(dsb, kv, preferred_element_type=F32)
                       + lax.dot_general(dox, before, NT, preferred_element_type=F32))
        dk_ref[...] = (lax.dot_general(dsb, qv, TN, preferred_element_type=F32)
                       + lax.dot_general(vv, agb, NT, preferred_element_type=F32) * zeta)
        dstate[...] = after_grad * g_blk + lax.dot_general(qv, dox, TN, preferred_element_type=F32)

    hspec = pl.BlockSpec((blk, HEAD_DIM), lambda h, i: (nb - 1 - i, h))
    return _call(body, name, (HEADS, nb),
                 [hspec] * 4 + [pl.BlockSpec((HEAD_DIM, HEAD_DIM), lambda h, i: (nb - 1 - i, h))],
                 [hspec] * 3, [_sds(q.shape, F32)] * 3, [pltpu.VMEM((HEAD_DIM, HEAD_DIM), F32)],
                 [q, k, v, do, states], host)


CHIP_FLIPS = ((1, 0), (0, 1), (1, 1))


def _position():
    return lax.axis_index("x"), lax.axis_index("y"), lax.axis_index("c")


class _Comm:
    def __init__(self, ins, out_shape, plan, n_remote, n_local):
        self.ins, self.out_shape, self.plan = list(ins), list(out_shape), plan
        self.n_remote, self.n_local = n_remote, n_local
        self.result = None

    def scratch(self):
        return [pltpu.SemaphoreType.DMA((self.n_remote,)), pltpu.SemaphoreType.DMA((self.n_remote,)),
                pltpu.SemaphoreType.DMA((self.n_local,))]

    def _copies(self, in_refs, out_refs, sems):
        send_sems, recv_sems, local_sems = sems
        pos = _position()
        p = self.plan(pos, in_refs, out_refs)

        def remote(k, src, dst, dev):
            return pltpu.make_async_remote_copy(src_ref=src, dst_ref=dst, send_sem=send_sems.at[k],
                                                recv_sem=recv_sems.at[k], device_id=dev, device_id_type=MESH)

        local = [pltpu.make_async_copy(s, d, local_sems.at[i]) for i, (s, d) in enumerate(p["local"])]
        out = [remote(k, s, d, dev) for k, (s, d, dev) in enumerate(p["sends"])]
        arrivals = [functools.partial(remote, k, d, d, pos) for k, d in enumerate(p["recvs"])]
        return local, out, arrivals

    def start(self, in_refs, out_refs, sems):
        local, out, _ = self._copies(in_refs, out_refs, sems)
        for cp in local + out:
            cp.start()

    def finish(self, in_refs, out_refs, sems):
        local, out, arrivals = self._copies(in_refs, out_refs, sems)
        for make in arrivals:
            make().wait_recv()
        for cp in out:
            cp.wait_send()
        for cp in local:
            cp.wait()

    def run(self, name):
        n_in, n_out = len(self.ins), len(self.out_shape)

        def body(*refs):
            comm = (refs[:n_in], refs[n_in:n_in + n_out], refs[n_in + n_out:])
            self.start(*comm)
            self.finish(*comm)

        hbm = pl.BlockSpec(memory_space=pl.ANY)
        self.result = pl.pallas_call(body, name=name, in_specs=[hbm] * n_in, out_specs=[hbm] * n_out,
                                     out_shape=self.out_shape, scratch_shapes=self.scratch())(*self.ins)
        return self.result


def _half_rows(c, rows):
    r2 = rows // 2
    return pl.ds(pl.multiple_of(c * r2, math.gcd(r2, LANES)), r2)


def _half(ref, c, rows, lead=()):
    return ref.at[(*lead, _half_rows(c, rows))]


def gather_halves(shards):
    def plan(pos, ins, outs):
        x, y, c = pos
        me = 2 * x + y
        p = dict(local=[], sends=[], recvs=[])
        for a, (src, dst) in enumerate(zip(ins, outs)):
            rows = shards[a].shape[0]
            p["local"].append((src, dst.at[me]))
            for fx, fy in CHIP_FLIPS:
                px, py = x ^ fx, y ^ fy
                p["sends"].append((_half(src, c, rows), _half(dst, c, rows, (me,)), (px, py, c)))
                p["recvs"].append(_half(dst, c, rows, (2 * px + py,)))
        return p

    return _Comm(shards, [_sds((N_CHIPS, *s.shape), s.dtype) for s in shards], plan,
                 n_remote=3 * len(shards), n_local=len(shards))


def chip_exchange(parts):
    def plan(pos, ins, outs):
        x, y, c = pos
        me = 2 * x + y
        p = dict(local=[], sends=[], recvs=[])
        for src, dst in zip(ins, outs):
            p["local"].append((src.at[me], dst.at[me]))
            for fx, fy in CHIP_FLIPS:
                px, py = x ^ fx, y ^ fy
                peer = 2 * px + py
                p["sends"].append((src.at[peer], dst.at[me], (px, py, c)))
                p["recvs"].append(dst.at[peer])
        return p

    return _Comm(parts, [_sds(g.shape, g.dtype) for g in parts], plan, n_remote=3 * len(parts), n_local=len(parts))


def all_gather_devices(v):
    flips = [(fx, fy, fc) for fx in (0, 1) for fy in (0, 1) for fc in (0, 1)][1:]

    def plan(pos, ins, outs):
        x, y, c = pos
        me = 4 * x + 2 * y + c
        p = dict(local=[(ins[0], outs[0].at[me])], sends=[], recvs=[])
        for fx, fy, fc in flips:
            px, py, pc = x ^ fx, y ^ fy, c ^ fc
            p["sends"].append((ins[0], outs[0].at[me], (px, py, pc)))
            p["recvs"].append(outs[0].at[4 * px + 2 * py + pc])
        return p

    return _Comm([v], [_sds((8, *v.shape), v.dtype)], plan, n_remote=7, n_local=1).run("small_all_gather")[0]


SWAP_CHUNK_BYTES = 3 * 2 ** 19


def _sibling_stream(t, n, value, consume, sbuf, rbuf, send_sems, recv_sems, credits):
    x, y, c = _position()
    sib = (x, y, 1 - c)

    def copy(slot):
        return pltpu.make_async_remote_copy(src_ref=sbuf.at[slot], dst_ref=rbuf.at[slot], send_sem=send_sems.at[slot],
                                            recv_sem=recv_sems.at[slot], device_id=sib, device_id_type=MESH)

    slot = t % 2

    @pl.when(jnp.logical_and(t >= 2, t < n))
    def _():
        copy(slot).wait_send()
        pl.semaphore_wait(credits.at[slot], 1)

    @pl.when(t < n)
    def _():
        sbuf[slot] = value
        copy(slot).start()

    @pl.when(t >= 1)
    def _():
        prev = 1 - slot
        copy(prev).wait_recv()
        consume(sbuf[prev], rbuf[prev])

        @pl.when(t + 1 < n)
        def _():
            pl.semaphore_signal(credits.at[prev], inc=1, device_id=sib, device_id_type=MESH)

    @pl.when(t == n)
    def _():
        copy(1 - slot).wait_send()
        if n > 1:
            copy(slot).wait_send()


def _swap_scratch(rows, cols, dtype):
    return [pltpu.VMEM((2, rows, cols), dtype), pltpu.VMEM((2, rows, cols), dtype),
            pltpu.SemaphoreType.DMA((2,)), pltpu.SemaphoreType.DMA((2,)), pltpu.SemaphoreType.REGULAR((2,))]


def _chunk_rows(rows, cols, dtype):
    return _tile(rows, max(16, SWAP_CHUNK_BYTES // (cols * jnp.dtype(dtype).itemsize)), 16)


def pair_add(name, g):
    s, r, c_ = g.shape
    r2 = r // 2
    cr = _chunk_rows(r2, c_, g.dtype)
    nj = r2 // cr

    n = s * nj

    def body(core, mine_ref, theirs_ref, o_ref, *scratch):
        def consume(_, got):
            o_ref[...] = (mine_ref[...].astype(F32) + got.astype(F32)).astype(o_ref.dtype)

        _sibling_stream(pl.program_id(0), n, theirs_ref[...], consume, *scratch)

    sent = lambda t: jnp.minimum(t, n - 1)
    used = lambda t: jnp.maximum(t - 1, 0)
    grid_spec = pltpu.PrefetchScalarGridSpec(
        num_scalar_prefetch=1, grid=(n + 1,),
        in_specs=[pl.BlockSpec((None, cr, c_), lambda t, core: (used(t) // nj, core[0] * nj + used(t) % nj, 0)),
                  pl.BlockSpec((None, cr, c_), lambda t, core: (sent(t) // nj, (1 - core[0]) * nj + sent(t) % nj, 0))],
        out_specs=pl.BlockSpec((None, cr, c_), lambda t, core: (used(t) // nj, used(t) % nj, 0)),
        scratch_shapes=_swap_scratch(cr, c_, g.dtype))
    core = lax.axis_index("c").astype(jnp.int32).reshape(1)
    return pl.pallas_call(body, name=name, grid_spec=grid_spec, out_shape=_sds((s, r2, c_), g.dtype),
                          compiler_params=_params("arbitrary"))(core, g, g)


def reduce_join(name, p):
    s, r2, c_ = p.shape
    cr = _chunk_rows(r2, c_, F32)
    nj = r2 // cr

    def body(core, p_ref, o_ref, *scratch):
        acc = p_ref[0].astype(F32)
        for i in range(1, s):
            acc = acc + p_ref[i].astype(F32)

        def consume(own, got):
            c = core[0]
            o_ref[c] = own
            o_ref[1 - c] = got

        _sibling_stream(pl.program_id(0), nj, acc, consume, *scratch)

    grid_spec = pltpu.PrefetchScalarGridSpec(
        num_scalar_prefetch=1, grid=(nj + 1,),
        in_specs=[pl.BlockSpec((s, cr, c_), lambda t, core: (0, jnp.minimum(t, nj - 1), 0))],
        out_specs=pl.BlockSpec((2, cr, c_), lambda t, core: (0, jnp.maximum(t - 1, 0), 0)),
        scratch_shapes=_swap_scratch(cr, c_, F32))
    core = lax.axis_index("c").astype(jnp.int32).reshape(1)
    out = pl.pallas_call(body, name=name, grid_spec=grid_spec, out_shape=_sds((2, r2, c_), F32),
                         compiler_params=_params("arbitrary"))(core, p)
    return out.reshape(2 * r2, c_)


def sibling_fill(name, buf):
    s, r, c_ = buf.shape
    r2 = r // 2
    cr = _chunk_rows(r2, c_, buf.dtype)
    nj = r2 // cr
    n_peers = len(CHIP_FLIPS)

    n = n_peers * nj

    def body(where, in_ref, o_ref, *scratch):
        def consume(_, got):
            o_ref[...] = got

        _sibling_stream(pl.program_id(0), n, in_ref[...], consume, *scratch)

    sent = lambda t: jnp.minimum(t, n - 1)
    used = lambda t: jnp.maximum(t - 1, 0)
    grid_spec = pltpu.PrefetchScalarGridSpec(
        num_scalar_prefetch=1, grid=(n + 1,),
        in_specs=[pl.BlockSpec((None, cr, c_),
                               lambda t, where: (where[sent(t) // nj], where[n_peers] * nj + sent(t) % nj, 0))],
        out_specs=pl.BlockSpec((None, cr, c_),
                               lambda t, where: (where[used(t) // nj], (1 - where[n_peers]) * nj + used(t) % nj, 0)),
        scratch_shapes=_swap_scratch(cr, c_, buf.dtype))
    x, y, c = _position()
    where = jnp.stack([2 * (x ^ fx) + (y ^ fy) for fx, fy in CHIP_FLIPS] + [c]).astype(jnp.int32)
    return pl.pallas_call(body, name=name, grid_spec=grid_spec, out_shape=_sds(buf.shape, buf.dtype),
                          input_output_aliases={1: 0}, compiler_params=_params("arbitrary"))(where, buf)


def sum_slots(name, p, out_dtype):
    s, r, c = p.shape
    tr = _tile(r, 256, 16)

    def body(p_ref, o_ref):
        acc = p_ref[0].astype(F32)
        for i in range(1, s):
            acc = acc + p_ref[i].astype(F32)
        o_ref[...] = acc.astype(o_ref.dtype)

    return pl.pallas_call(
        body, name=name, grid=(r // tr,), in_specs=[pl.BlockSpec((s, tr, c), lambda i: (0, i, 0))],
        out_specs=pl.BlockSpec((tr, c), lambda i: (i, 0)), out_shape=_sds((r, c), out_dtype),
        compiler_params=_params("arbitrary"),
    )(p)


def adamw(name, w, g, m, v):
    r, c = w.shape
    tr = _tile(r, 256, 8)

    def fn(_, w, g, m, v):
        m = ADAM_B1 * m + (1.0 - ADAM_B1) * g
        v = ADAM_B2 * v + (1.0 - ADAM_B2) * (g * g)
        m_hat = m / (1.0 - ADAM_B1 ** ADAM_STEP)
        v_hat = v / (1.0 - ADAM_B2 ** ADAM_STEP)
        delta = -ADAM_LR * (m_hat / (jnp.sqrt(v_hat) + ADAM_EPS) + ADAM_WD * w)
        return [delta, m, v, g], []

    outs, _ = _rowwise(name, fn, [w, g, m, v], [], [(c, F32)] * 4, [], tr)
    return outs


RET_SCALE = HEAD_DIM ** -0.5
MLA_SCALE = (HEAD_DIM + MLA_ROPE) ** -0.5
GRAD_DT = BF16
IN_RET = 4 * HEADS * HEAD_DIM
IN_MLA = MLA_Q_RANK + MLA_KV_RANK + MLA_ROPE
IN_MLA_PAD = IN_MLA + 64


def _heads(fn):
    return jnp.concatenate([fn(h) for h in range(HEADS)], axis=1)


def _head(a, h, stride=HEAD_DIM, off=0):
    return a[:, h * stride + off:h * stride + off + HEAD_DIM]


def _group_norm(o):
    rs = [lax.rsqrt(jnp.mean(_head(o, h) * _head(o, h), axis=-1, keepdims=True) + EPS) for h in range(HEADS)]
    return _heads(lambda h: _head(o, h) * rs[h]), rs


class _Alone:
    def host(self, kernel_name):
        return None

    def done(self, kernel_name, w):
        pass

    def grads(self, g):
        pass


def _ffn_fwd(tag, n, w, k, tm, tmk, plan):
    gate, up, down = tag + "_gate", tag + "_up", tag + "_down"
    if "wu" + k in w:
        g, u, a = mm_nn(up, n, w["wg" + k], tm, outs=[BF16] * 3, w2=w["wu" + k],
                        epilogue=lambda g, u: (g, u, _silu(g) * u), host=plan.host(up))
    else:
        (g,) = mm_nn(gate, n, w["wg" + k], tm, out_dtype=BF16, host=plan.host(gate))
        plan.done(gate, w)
        u, a = mm_nn(up, n, w["wu" + k], tm, outs=[BF16] * 2, extras=(g,),
                     epilogue=lambda u, g: (u, _silu(g.astype(F32)) * u), host=plan.host(up))
    plan.done(up, w)
    ff = a.shape[1]
    wd = w["wd" + k]
    f = mm_nn_k(down, a, wd.reshape(ff, wd.shape[2]), tmk, _tile(ff, 1408), host=plan.host(down))
    plan.done(down, w)
    return g, u, a, f


def _ffn_bwd(tag, df, n, g, u, a, wg, wu, wd, tm, tmk, tk, dt, plan):
    ns = wg.shape[2]

    def gate_grads(da, g, u):
        g, u = g.astype(F32), u.astype(F32)
        return da * u * _dsilu(g), da * _silu(g)

    def hosted(kernel_name, call):
        out = call(plan.host(kernel_name))
        plan.done(kernel_name, None)
        return out

    k = tag[-1]
    dg, du = hosted(tag + "_da", lambda h: mm_nt(tag + "_da", [df], [wd], tm, ns, outs=(BF16, BF16),
                                                 epilogue=gate_grads, extras=(g, u), host=h))
    dwg = hosted(tag + "_dwg", lambda h: mm_tn(tag + "_dwg", n, dg, dt, ns, tk, GRAD_DT, shard_cols=True, host=h))
    plan.grads({"wg" + k: dwg})
    dwu = hosted(tag + "_dwu", lambda h: mm_tn(tag + "_dwu", n, du, dt, ns, tk, GRAD_DT, shard_cols=True, host=h))
    plan.grads({"wu" + k: dwu})
    dwd = hosted(tag + "_dwd", lambda h: mm_tn(tag + "_dwd", a, df, ns, dt, tk, GRAD_DT, shard_rows=True, host=h))
    plan.grads({"wd" + k: dwd})
    dn = hosted(tag + "_dn", lambda h: mm_nt_k(tag + "_dn", [dg, du], [wg, wu], tmk, dt, host=h))
    return dn


def _local_step(x, tgt, meta, w, nw, plan):
    seq, d = x.shape
    t_real = N_META + seq
    tp = -(-t_real // LANES) * LANES
    zpad = jnp.zeros((tp - t_real, d), F32)
    h0 = jnp.concatenate([meta, x, zpad], axis=0)
    tgt_p = jnp.concatenate([jnp.zeros((N_META, d), F32), tgt, zpad], axis=0)
    cos_r, sin_r, cos_m, sa, sb = _rope_tables(tp)
    tm = _tile(tp, 512)
    tmt = _tile(tp, 768, 16)
    tmk = _tile(tp, 1408)
    tk = tp
    blk = tm
    dt = _tile(d, 1024)
    hw = HEADS * HEAD_DIM
    qw = HEADS * MLA_QK_PAD

    (n1,), _ = _rowwise("ffn1_norm", lambda r0, h, g: ([_rms(h, g)], []), [h0], [nw["ffn1_pre_norm"]],
                        [(d, BF16)], [], tm)
    g1, u1, a1, f1 = _ffn_fwd("ffn1", n1, w, "1", tm, tmk, plan)

    def post_ffn1(r0, h, f, post, pre):
        h1 = h + 0.5 * _rms(f, post)
        return [h1, _rms(h1, pre)], []

    (h1, un), _ = _rowwise("mix_norm", post_ffn1, [h0, f1], [nw["ffn1_post_norm"], nw["mix_pre_norm"]],
                           [(d, F32), (d, BF16)], [], tm)
    (proj_r,) = mm_nn("proj_r", un, w["w_r"], tm, n_block=_tile(IN_RET, 1024), host=plan.host("proj_r"))
    plan.done("proj_r", w)
    (proj_c,) = mm_nn("proj_c", un, w["w_c"], tm)

    def split_proj(r0, pr, pc, cr, sr, cm, ta, tb, qn, kvn):
        rq = _heads(lambda h: _rope_r(_head(pr, h), cr, sr))
        rk = _heads(lambda h: _rope_r(_head(pr, h, off=hw), cr, sr) * RET_SCALE)
        rv = pr[:, 2 * hw:3 * hw]
        cqn = _rms(pc[:, :MLA_Q_RANK], qn)
        ckvn = _rms(pc[:, MLA_Q_RANK:MLA_Q_RANK + MLA_KV_RANK], kvn)
        krr = _rope_m(pc[:, MLA_Q_RANK + MLA_KV_RANK:], cm, ta, tb)
        return [rq, rk, rv, cqn, ckvn, krr], []

    (rq, rk, rv, cqn, ckvn, krr), _ = _rowwise(
        "split_proj", split_proj, [proj_r, proj_c, cos_r, sin_r, cos_m, sa, sb],
        [nw["mla_q_norm"], nw["mla_kv_norm"]],
        [(hw, BF16), (hw, BF16), (hw, BF16), (MLA_Q_RANK, BF16), (MLA_KV_RANK, BF16), (LANES, F32)], [], tm)
    (qp,) = mm_nn("q_up", cqn, w["wuq"], tm)
    (kn,) = mm_nn("k_up", ckvn, w["wuk"], tm)
    (vv,) = mm_nn("v_up", ckvn, w["wuv"], tm, out_dtype=BF16)

    def build_qk(r0, qp, kn, krr, cm, ta, tb):
        qc = jnp.concatenate(
            [part for h in range(HEADS)
             for part in (_head(qp, h, MLA_QK_PAD), _rope_m(_head(qp, h, MLA_QK_PAD, HEAD_DIM), cm, ta, tb))], axis=1)
        kc = jnp.concatenate([part for h in range(HEADS) for part in (_head(kn, h), krr)], axis=1)
        return [qc, kc], []

    (qc, kc), _ = _rowwise("build_qk", build_qk, [qp, kn, krr, cos_m, sa, sb], [], [(qw, BF16), (qw, BF16)], [], tm)
    o_m, lse = attn_fwd("mla_fwd", qc, kc, vv, blk, MLA_SCALE, host=plan.host("mla_fwd"))
    plan.done("mla_fwd", w)
    o_r, ret_states = ret_fwd("ret_fwd", rq, rk, rv, blk, host=plan.host("ret_fwd"))
    plan.done("ret_fwd", w)

    def gate_mix(r0, rg, o_r, o_m, gn):
        y, _ = _group_norm(o_r)
        return [jnp.concatenate([_silu(rg) * (y * gn), o_m], axis=1)], []

    (mixcat,), _ = _rowwise("gate_mix", gate_mix, [(proj_r, hw, 3), o_r, o_m], [nw["ret_group_norm"]],
                            [(2 * hw, BF16)], [], tm)
    (mix,) = mm_nn("mix_out", mixcat, w["w_out"], tm, n_block=dt)

    def post_mix(r0, h, m, post, pre):
        h2 = h + _rms(m, post)
        return [h2, _rms(h2, pre)], []

    (h2, n3), _ = _rowwise("ffn2_norm", post_mix, [h1, mix], [nw["mix_post_norm"], nw["ffn2_pre_norm"]],
                           [(d, F32), (d, BF16)], [], tm)
    g2, u2, a2, f2 = _ffn_fwd("ffn2", n3, w, "2", tm, tmk, plan)

    def loss_head(r0, h, f, t, post):
        h3 = h + 0.5 * _rms(f, post)
        row = r0 + lax.broadcasted_iota(jnp.int32, h3.shape, 0)
        err = jnp.where(row >= N_META, jnp.where(row < t_real, h3 - t, 0.0), 0.0)
        dh3 = err / d
        df, dpost = _rms_bwd(f, post, 0.5 * dh3)
        return [dh3, df], [_colsum(err * err), _colsum(dpost)]

    (dh3, df2), (loss_vec, d_post2) = _rowwise("loss_head", loss_head, [h2, f2, tgt_p], [nw["ffn2_post_norm"]],
                                               [(d, F32), (d, BF16)], [d, d], tm)
    loss = 0.5 * jnp.sum(loss_vec) / d
    dn3 = _ffn_bwd("ffn2", df2, n3, g2, u2, a2, w["wg2"], w["wu2"], w["wd2"], tmt, tmk, tk, dt, plan)

    def back_mix_norm(r0, h, m, dh3, dn, pre, post):
        dx, dpre = _rms_bwd(h, pre, dn)
        dh2 = dh3 + dx
        dm, dpost = _rms_bwd(m, post, dh2)
        return [dh2, dm], [_colsum(dpre), _colsum(dpost)]

    (dh2, dmix), (d_pre2, d_mix_post) = _rowwise(
        "back_mix_norm", back_mix_norm, [h2, mix, dh3, dn3], [nw["ffn2_pre_norm"], nw["mix_post_norm"]],
        [(d, F32), (d, BF16)], [d, d], tm)
    (dmixcat,) = mm_nt("mix_dx", [dmix], [w["w_out"]], tmt, _tile(2 * hw, 512))
    dw_out = mm_tn("mix_dw", mixcat, dmix, 2 * hw // N_CHIPS, dt, tk, GRAD_DT, shard_rows=True)

    def back_gate(r0, dmc, rg, o_r, gn):
        d_ret, d_om = dmc[:, :hw], dmc[:, hw:]
        yh, rs = _group_norm(o_r)
        d_rg = d_ret * (yh * gn) * _dsilu(rg)
        dy = d_ret * _silu(rg)
        gyh = dy * gn
        d_or = _heads(lambda h: rs[h] * (_head(gyh, h) - _head(yh, h) * jnp.mean(_head(gyh, h) * _head(yh, h),
                                                                                    axis=-1, keepdims=True)))
        return [d_or, d_rg, d_om], [_colsum(dy * yh)]

    (d_or, d_rg, d_om), (d_gn,) = _rowwise("back_gate", back_gate, [dmixcat, (proj_r, hw, 3), o_r],
                                           [nw["ret_group_norm"]], [(hw, BF16), (hw, F32), (hw, BF16)], [hw], tm)
    dqc, dkc, dvv = attn_bwd("mla_bwd", qc, kc, vv, d_om, o_m, lse, blk, MLA_SCALE, host=plan.host("mla_bwd"))
    plan.done("mla_bwd", None)
    drq, drk, drv = ret_bwd("ret_bwd", rq, rk, rv, d_or, ret_states, blk)

    def back_qk(r0, dqc, dkc, dvv, cm, ta, tb):
        dqp = jnp.concatenate(
            [part for h in range(HEADS)
             for part in (_head(dqc, h, MLA_QK_PAD), _rope_m_t(_head(dqc, h, MLA_QK_PAD, HEAD_DIM), cm, ta, tb))],
            axis=1)
        dkn = _heads(lambda h: _head(dkc, h, MLA_QK_PAD))
        dkr = _head(dkc, 0, MLA_QK_PAD, HEAD_DIM)
        for h in range(1, HEADS):
            dkr = dkr + _head(dkc, h, MLA_QK_PAD, HEAD_DIM)
        return [dqp, dkn, _rope_m_t(dkr, cm, ta, tb), dvv], []

    (dqp, dkn, dkr, dvb), _ = _rowwise("back_qk", back_qk, [dqc, dkc, dvv, cos_m, sa, sb], [],
                                       [(qw, BF16), (hw, BF16), (LANES, F32), (hw, BF16)], [], tm)
    (dcqn,) = mm_nt("q_dx", [dqp], [w["wuq"]], tm, MLA_Q_RANK)
    dwuq = mm_tn("q_dw", cqn, dqp, MLA_Q_RANK, _tile(qw, 1024), tk, GRAD_DT)
    (dckvn,) = mm_nt("kv_dx", [dkn, dvb], [w["wuk"], w["wuv"]], tm, MLA_KV_RANK)
    dwuk = mm_tn("k_dw", ckvn, dkn, MLA_KV_RANK, hw, tk, GRAD_DT)
    dwuv = mm_tn("v_dw", ckvn, dvb, MLA_KV_RANK, hw, tk, GRAD_DT)

    def back_proj(r0, drq, drk, drv, d_rg, pc, dcqn, dckvn, dkr, cr, sr, qn, kvn):
        d_q = _heads(lambda h: _rope_r_t(_head(drq, h), cr, sr))
        d_k = _heads(lambda h: _rope_r_t(_head(drk, h), cr, sr) * RET_SCALE)
        dcq, a_q = _rms_bwd(pc[:, :MLA_Q_RANK], qn, dcqn)
        dckv, a_kv = _rms_bwd(pc[:, MLA_Q_RANK:MLA_Q_RANK + MLA_KV_RANK], kvn, dckvn)
        return ([jnp.concatenate([d_q, d_k, drv, d_rg], axis=1), jnp.concatenate([dcq, dckv, dkr], axis=1)],
                [_colsum(a_q), _colsum(a_kv)])

    (dproj_r, dproj_c), (d_qn, d_kvn) = _rowwise(
        "back_proj", back_proj, [drq, drk, drv, d_rg, proj_c, dcqn, dckvn, dkr, cos_r, sin_r],
        [nw["mla_q_norm"], nw["mla_kv_norm"]], [(IN_RET, BF16), (IN_MLA_PAD, BF16)],
        [MLA_Q_RANK, MLA_KV_RANK], tm)
    (dun,) = mm_nt("proj_dx", [dproj_r, dproj_c], [w["w_r"], w["w_c"]], tmt, _tile(d, 512))
    dw_r = mm_tn("proj_dw_r", un, dproj_r, dt, _tile(IN_RET, 1024), tk, GRAD_DT)
    dw_c = mm_tn("proj_dw_c", un, dproj_c, dt, IN_MLA_PAD, tk, GRAD_DT)
    plan.grads(dict(w_r=dw_r, w_c=dw_c, wuq=dwuq, wuk=dwuk, wuv=dwuv, w_out=dw_out))

    def back_ffn1_norm(r0, h, f, dh2, dn, pre, post):
        dx, dpre = _rms_bwd(h, pre, dn)
        dh1 = dh2 + dx
        df, dpost = _rms_bwd(f, post, 0.5 * dh1)
        return [dh1, df], [_colsum(dpre), _colsum(dpost)]

    (dh1, df1), (d_mix_pre, d_post1) = _rowwise(
        "back_ffn1_norm", back_ffn1_norm, [h1, f1, dh2, dun], [nw["mix_pre_norm"], nw["ffn1_post_norm"]],
        [(d, F32), (d, BF16)], [d, d], tm)
    dn1 = _ffn_bwd("ffn1", df1, n1, g1, u1, a1, w["wg1"], w["wu1"], w["wd1"], tmt, tmk, tk, dt, plan)

    def back_input(r0, h, dh1, dn, pre):
        dx, dpre = _rms_bwd(h, pre, dn)
        return [dh1 + dx], [_colsum(dpre)]

    (dh0,), (d_pre1,) = _rowwise("back_input", back_input, [h0, dh1, dn1], [nw["ffn1_pre_norm"]], [(d, F32)], [d], tm)

    small = dict(ffn1_pre_norm=d_pre1, ffn1_post_norm=d_post1, mix_pre_norm=d_mix_pre, ret_group_norm=d_gn,
                 mla_q_norm=d_qn, mla_kv_norm=d_kvn, mix_post_norm=d_mix_post, ffn2_pre_norm=d_pre2,
                 ffn2_post_norm=d_post2)
    return loss, dh0[N_META:t_real], small, dh0[:N_META]


WEIGHTS = ("meta_tokens", "ffn1_pre_norm", "ffn1_w_gate", "ffn1_w_up", "ffn1_w_down", "ffn1_post_norm",
           "mix_pre_norm", "w_in", "ret_group_norm", "mla_q_norm", "mla_w_uq", "mla_kv_norm", "mla_w_uk",
           "mla_w_uv", "w_out", "mix_post_norm", "ffn2_pre_norm", "ffn2_w_gate", "ffn2_w_up", "ffn2_w_down",
           "ffn2_post_norm")
BIG = ("ffn1_w_gate", "ffn1_w_up", "ffn1_w_down", "w_in", "mla_w_uq", "mla_w_uk", "mla_w_uv", "w_out",
       "ffn2_w_gate", "ffn2_w_up", "ffn2_w_down")
NORMS = ("ffn1_pre_norm", "ffn1_post_norm", "mix_pre_norm", "ret_group_norm", "mla_q_norm", "mla_kv_norm",
         "mix_post_norm", "ffn2_pre_norm", "ffn2_post_norm")


def _unshard_cols(g):
    return g.transpose(1, 0, 2).reshape(g.shape[1], -1)


def _shard_cols(a):
    return a.reshape(a.shape[0], N_CHIPS, -1).transpose(1, 0, 2)


def _pack_rows(rows, width):
    rows = [jnp.pad(r, ((0, 0), (0, width - r.shape[1]))) for r in rows]
    n = sum(r.shape[0] for r in rows)
    return jnp.pad(jnp.concatenate(rows, axis=0), ((0, -n % 8), (0, 0)))


def _weight_views(full):
    w = {}
    for n, g in full.items():
        if n == "w_in":
            w_in = _unshard_cols(g)
            w["w_r"] = w_in[:, :IN_RET]
            w["w_c"] = jnp.pad(w_in[:, IN_RET:], ((0, 0), (0, IN_MLA_PAD - IN_MLA)))
        elif n == "mla_w_uq":
            q = _unshard_cols(g).reshape(MLA_Q_RANK, HEADS, HEAD_DIM + MLA_ROPE)
            q = jnp.pad(q, ((0, 0), (0, 0), (0, MLA_QK_PAD - HEAD_DIM - MLA_ROPE)))
            w["wuq"] = q.reshape(MLA_Q_RANK, HEADS * MLA_QK_PAD)
        elif n in ("mla_w_uk", "mla_w_uv"):
            w["wu" + n[-1]] = _unshard_cols(g)
        elif n == "w_out":
            w["w_out"] = g.reshape(-1, g.shape[2])
        else:
            w["w" + n[7] + n[3]] = g
    return w


def _contributions(g):
    ffn = {"g": "gate", "u": "up", "d": "down"}
    c = {f"ffn{n[2]}_w_{ffn[n[1]]}": a for n, a in g.items() if len(n) == 3 and n[2] in "12"}
    if "w_r" in g:
        dwuq = g["wuq"].reshape(MLA_Q_RANK, HEADS, MLA_QK_PAD)[:, :, :HEAD_DIM + MLA_ROPE]
        c.update(w_in=_shard_cols(jnp.concatenate([g["w_r"], g["w_c"][:, :IN_MLA]], axis=1)),
                 mla_w_uq=_shard_cols(dwuq.reshape(MLA_Q_RANK, -1)), mla_w_uk=_shard_cols(g["wuk"]),
                 mla_w_uv=_shard_cols(g["wuv"]), w_out=g["w_out"])
    return c


class _Schedule(_Alone):
    FIRST = ("ffn1_w_gate",)
    CARRIED = {"ffn1_gate": ("ffn1_w_up",), "ffn1_up": ("ffn1_w_down",),
               "ffn1_down": ("w_in", "mla_w_uq", "mla_w_uk", "mla_w_uv"),
               "proj_r": ("w_out",), "mla_fwd": ("ffn2_w_gate", "ffn2_w_up"), "ffn2_up": ("ffn2_w_down",)}
    GRAD_HOST = dict(ffn2_w_gate="mla_bwd", ffn2_w_up="mla_bwd", ffn2_w_down="mla_bwd",
                     w_in="ffn1_da", mla_w_uq="ffn1_da", mla_w_uk="ffn1_da", mla_w_uv="ffn1_da", w_out="ffn1_da",
                     ffn1_w_gate="ffn1_dwu", ffn1_w_up="ffn1_dwd", ffn1_w_down="ffn1_dn")

    def __init__(self, shards):
        self.gathers = {k: (gather_halves([shards[n] for n in names]), names) for k, names in self.CARRIED.items()}
        self.waiting = {}
        self.exchanges = {}
        self.grad = {}

    def host(self, kernel_name):
        if kernel_name in self.gathers:
            return self.gathers[kernel_name][0]
        if kernel_name in self.waiting:
            names, sums = zip(*self.waiting.pop(kernel_name))
            self.exchanges[kernel_name] = (chip_exchange(list(sums)), names)
            return self.exchanges[kernel_name][0]
        return None

    def done(self, kernel_name, w):
        if kernel_name in self.gathers:
            comm, names = self.gathers[kernel_name]
            w.update(_weight_views({n: sibling_fill("fill_" + n, b) for n, b in zip(names, comm.result)}))
        elif kernel_name in self.exchanges:
            comm, names = self.exchanges[kernel_name]
            for n, q in zip(names, comm.result):
                self.grad[n] = reduce_join("reduce_join_" + n, q)

    def grads(self, g):
        for n, a in _contributions(g).items():
            self.waiting.setdefault(self.GRAD_HOST[n], []).append((n, pair_add("pair_add_" + n, a)))


def _step(p, m, v, x, loss_target):
    d = x.shape[2]
    names = list(BIG)
    shards = {n: p[n][0].astype(BF16) for n in names}
    first = _Schedule.FIRST
    gathered = gather_halves([shards[n] for n in first] + [p["meta_tokens"]]).run("gather_first")
    filled = [sibling_fill("fill_" + n, b) for n, b in zip(first + ("meta_tokens",), gathered)]
    w = _weight_views(dict(zip(first, filled[:-1])))
    meta = _unshard_cols(filled[-1])
    nw = {n: p[n] for n in NORMS}
    plan = _Schedule(shards)
    loss, grad_x, small, d_meta = _local_step(x[0], loss_target[0], meta, w, nw, plan)
    grads = dict(plan.grad)

    width = max(d, HEADS * HEAD_DIM)
    packed = _pack_rows([small[n] for n in NORMS] + [d_meta], width)
    total = sum_slots("small_sum", all_gather_devices(packed), F32)
    for i, n in enumerate(NORMS):
        grads[n] = total[i:i + 1, :p[n].shape[1]]
    cols = p["meta_tokens"].shape[1]
    chip = 2 * lax.axis_index("x") + lax.axis_index("y")
    grads["meta_tokens"] = lax.dynamic_slice(total[len(NORMS):len(NORMS) + N_META, :d], (0, chip * cols), (N_META, cols))

    delta, new_m, new_v = {}, {}, {}
    for n in names + ["meta_tokens"]:
        shape = p[n].shape
        flat = lambda a: a.reshape(-1, shape[-1])
        out = adamw("adamw_" + n, flat(p[n]), flat(grads[n]), flat(m[n]), flat(v[n]))
        delta[n], new_m[n], new_v[n], grads[n] = (o.reshape(shape) for o in out)
    pk = lambda src: _pack_rows([src[n] for n in NORMS], width)
    out = adamw("adamw_norms", pk(p), pk(grads), pk(m), pk(v))
    for i, n in enumerate(NORMS):
        delta[n], new_m[n], new_v[n] = (o[i:i + 1, :p[n].shape[1]] for o in out[:3])

    loss = lax.psum(loss, ("x", "y", "c"))
    return loss, grad_x[None], grads, delta, new_m, new_v


def kernel(x, meta_tokens, ffn1_pre_norm, ffn1_w_gate, ffn1_w_up, ffn1_w_down, ffn1_post_norm, mix_pre_norm, w_in, ret_group_norm, mla_q_norm, mla_w_uq, mla_kv_norm, mla_w_uk, mla_w_uv, w_out, mix_post_norm, ffn2_pre_norm, ffn2_w_gate, ffn2_w_up, ffn2_w_down, ffn2_post_norm, loss_target, m_meta_tokens, m_ffn1_pre_norm, m_ffn1_w_gate, m_ffn1_w_up, m_ffn1_w_down, m_ffn1_post_norm, m_mix_pre_norm, m_w_in, m_ret_group_norm, m_mla_q_norm, m_mla_w_uq, m_mla_kv_norm, m_mla_w_uk, m_mla_w_uv, m_w_out, m_mix_post_norm, m_ffn2_pre_norm, m_ffn2_w_gate, m_ffn2_w_up, m_ffn2_w_down, m_ffn2_post_norm, v_meta_tokens, v_ffn1_pre_norm, v_ffn1_w_gate, v_ffn1_w_up, v_ffn1_w_down, v_ffn1_post_norm, v_mix_pre_norm, v_w_in, v_ret_group_norm, v_mla_q_norm, v_mla_w_uq, v_mla_kv_norm, v_mla_w_uk, v_mla_w_uv, v_w_out, v_mix_post_norm, v_ffn2_pre_norm, v_ffn2_w_gate, v_ffn2_w_up, v_ffn2_w_down, v_ffn2_post_norm):
    args = locals()
    p = {n: args[n] for n in WEIGHTS}
    m = {n: args["m_" + n] for n in WEIGHTS}
    v = {n: args["v_" + n] for n in WEIGHTS}
    loss, grad_x, grads, delta, new_m, new_v = _step(p, m, v, x, loss_target)
    return (loss, grad_x, *[grads[n] for n in WEIGHTS], *[delta[n] for n in WEIGHTS],
            *[new_m[n] for n in WEIGHTS], *[new_v[n] for n in WEIGHTS])
```

```python
import functools
import math

import jax
import jax.numpy as jnp
from jax import lax
from jax.experimental import pallas as pl
from jax.experimental.pallas import tpu as pltpu

F32 = jnp.float32
BF16 = jnp.bfloat16

EPS = 1e-6
N_META = 16
HEADS = 8
HEAD_DIM = 128
MLA_ROPE = 64
MLA_QK_PAD = 256
MLA_Q_RANK = 512
MLA_KV_RANK = 256
ROPE_THETA = 10000.0
N_CHIPS = 4
LANES = 128
VMEM_LIMIT = 60 * 2 ** 20

ADAM_LR = 0.001
ADAM_B1 = 0.9
ADAM_B2 = 0.999
ADAM_EPS = 1e-08
ADAM_WD = 0.01
ADAM_STEP = 10

NN = (((1,), (0,)), ((), ()))
NT = (((1,), (1,)), ((), ()))
TN = (((0,), (0,)), ((), ()))
MESH = pl.DeviceIdType.MESH


def _tile(n, pref, align=LANES):
    if n <= pref:
        return n
    best = 0
    for t in range(align, pref + 1, align):
        if n % t == 0:
            best = t
    assert best, (n, pref)
    return best


def _params(*sem):
    return pltpu.CompilerParams(dimension_semantics=sem, vmem_limit_bytes=VMEM_LIMIT)


def _sds(shape, dtype):
    return jax.ShapeDtypeStruct(tuple(shape), dtype)


def _rowwise(name, fn, rows, consts, outs, accs, tr):
    rows = [r if isinstance(r, tuple) else (r, r.shape[1], 0) for r in rows]
    t = rows[0][0].shape[0]
    assert t % tr == 0
    n_r, n_c, n_o = len(rows), len(consts), len(outs)

    def body(*refs):
        i = pl.program_id(0)
        r = [x[...] for x in refs[:n_r]]
        c = [x[...] for x in refs[n_r:n_r + n_c]]
        o_refs = refs[n_r + n_c:n_r + n_c + n_o]
        a_refs = refs[n_r + n_c + n_o:]
        o_vals, a_vals = fn(i * tr, *r, *c)
        for ref, v in zip(o_refs, o_vals):
            ref[...] = v.astype(ref.dtype)
        if a_refs:
            @pl.when(i == 0)
            def _():
                for ref, v in zip(a_refs, a_vals):
                    ref[...] = v

            @pl.when(i > 0)
            def _():
                for ref, v in zip(a_refs, a_vals):
                    ref[...] += v

    in_specs = [pl.BlockSpec((tr, w), functools.partial(lambda cb, i: (i, cb), cb)) for _, w, cb in rows]
    in_specs += [pl.BlockSpec(a.shape, lambda i: (0, 0)) for a in consts]
    out_specs = [pl.BlockSpec((tr, w), lambda i: (i, 0)) for w, _ in outs]
    out_specs += [pl.BlockSpec((1, w), lambda i: (0, 0)) for w in accs]
    out_shape = [_sds((t, w), dt) for w, dt in outs] + [_sds((1, w), F32) for w in accs]
    res = pl.pallas_call(
        body, name=name, grid=(t // tr,), in_specs=in_specs, out_specs=out_specs, out_shape=out_shape,
        compiler_params=_params("arbitrary"),
    )(*[a for a, _, _ in rows], *consts)
    return res[:n_o], res[n_o:]


def _rms(x, w):
    r = lax.rsqrt(jnp.mean(x * x, axis=-1, keepdims=True) + EPS)
    return x * r * w


def _rms_bwd(x, w, dy):
    r = lax.rsqrt(jnp.mean(x * x, axis=-1, keepdims=True) + EPS)
    xh = x * r
    gy = dy * w
    dx = r * (gy - xh * jnp.mean(gy * xh, axis=-1, keepdims=True))
    return dx, dy * xh


def _colsum(v):
    return jnp.sum(v, axis=0, keepdims=True)


def _silu(x):
    return x * jax.nn.sigmoid(x)


def _dsilu(x):
    s = jax.nn.sigmoid(x)
    return s * (1.0 + x * (1.0 - s))


def _rope_r(x, cos, sin):
    return x * cos + pltpu.roll(x, 64, 1) * sin


def _rope_r_t(dy, cos, sin):
    return dy * cos + pltpu.roll(dy * sin, 64, 1)


def _rope_m(x, cos, sa, sb):
    return x * cos + pltpu.roll(x, 32, 1) * sa + pltpu.roll(x, 96, 1) * sb


def _rope_m_t(dy, cos, sa, sb):
    return dy * cos + pltpu.roll(dy * sa, 96, 1) + pltpu.roll(dy * sb, 32, 1)


def _rope_tables(t):
    pos = jnp.arange(t, dtype=F32)

    def cs(dim):
        inv = ROPE_THETA ** (-jnp.arange(0, dim, 2, dtype=F32) / dim)
        ang = pos[:, None] * inv[None, :]
        return jnp.cos(ang), jnp.sin(ang)

    c, s = cs(HEAD_DIM)
    cos_r = jnp.concatenate([c, c], axis=1)
    sin_r = jnp.concatenate([-s, s], axis=1)
    c, s = cs(MLA_ROPE)
    z32, z64 = jnp.zeros_like(s), jnp.zeros((t, 64), F32)
    cos_m = jnp.concatenate([c, c, z64], axis=1)
    sa = jnp.concatenate([z32, s, z64], axis=1)
    sb = jnp.concatenate([-s, z32, z64], axis=1)
    return cos_r, sin_r, cos_m, sa, sb


def _call(body, name, grid, in_specs, out_specs, out_shape, scratch, operands, host=None):
    sem = ("arbitrary",) * len(grid)
    if host is None:
        return pl.pallas_call(body, name=name, grid=grid, in_specs=in_specs, out_specs=out_specs, out_shape=out_shape,
                              scratch_shapes=scratch, compiler_params=_params(*sem))(*operands)
    n_in, n_out, n_s = len(in_specs), len(out_shape), len(scratch)
    h_in, h_out = len(host.ins), len(host.out_shape)

    def hosted(*refs):
        a = n_in
        b = a + h_in
        c = b + n_out
        d = c + h_out
        e = d + n_s
        ids = [pl.program_id(i) for i in range(len(grid))]
        first = functools.reduce(jnp.logical_and, [i == 0 for i in ids])
        last = functools.reduce(jnp.logical_and, [i == g - 1 for i, g in zip(ids, grid)])
        comm = (refs[a:b], refs[c:d], refs[e:])

        @pl.when(first)
        def _():
            host.start(*comm)

        body(*refs[:a], *refs[b:c], *refs[d:e])

        @pl.when(last)
        def _():
            host.finish(*comm)

    hbm = pl.BlockSpec(memory_space=pl.ANY)
    res = pl.pallas_call(
        hosted, name=name, grid=grid, in_specs=list(in_specs) + [hbm] * h_in, out_specs=list(out_specs) + [hbm] * h_out,
        out_shape=list(out_shape) + list(host.out_shape), scratch_shapes=list(scratch) + host.scratch(),
        compiler_params=_params(*sem))(*operands, *host.ins)
    host.result = res[n_out:]
    return res[:n_out]


def _mm(name, grid, operands, in_specs, dns, out_specs, out_shape, epilogue=None, extras=(), extra_specs=(),
        acc_shape=None, host=None):
    n_p, n_e = len(dns), len(extras)
    nk = grid[2]
    n_o = len(out_shape)
    in_place = nk > 1 and epilogue is None and n_o == 1 and out_shape[0].dtype == F32

    def body(*refs):
        ab = refs[:2 * n_p]
        ex = refs[2 * n_p:2 * n_p + n_e]
        outs = refs[2 * n_p + n_e:2 * n_p + n_e + n_o]

        part = None
        for p in range(n_p):
            d = lax.dot_general(ab[2 * p][...], ab[2 * p + 1][...], dns[p], preferred_element_type=F32)
            part = d if part is None else part + d

        def finish(acc):
            vals = (acc,) if epilogue is None else epilogue(acc, *[e[...] for e in ex])
            for o, v in zip(outs, vals):
                o[...] = v.astype(o.dtype)

        if nk == 1:
            finish(part)
        else:
            acc_ref = outs[0] if in_place else refs[2 * n_p + n_e + n_o]
            k = pl.program_id(2)

            @pl.when(k == 0)
            def _():
                acc_ref[...] = part

            @pl.when(k > 0)
            def _():
                acc_ref[...] += part

            if not in_place:
                @pl.when(k == nk - 1)
                def _():
                    finish(acc_ref[...])

    scratch = [] if nk == 1 or in_place else [pltpu.VMEM(acc_shape, F32)]
    return _call(body, name, grid, list(in_specs) + list(extra_specs), out_specs, out_shape, scratch,
                 [*operands, *extras], host)


def mm_nn(name, x, w, tm, out_dtype=F32, epilogue=None, outs=None, w2=None, n_block=None, extras=(), host=None):
    t, kdim = x.shape
    if w.ndim == 3:
        s, _, ns = w.shape
        n, tn, nb = s * ns, ns, s
        wspec = pl.BlockSpec((None, kdim, ns), lambda j, i, k: (j, 0, 0))
    else:
        n = w.shape[1]
        tn = n_block or n
        nb = n // tn
        wspec = pl.BlockSpec((kdim, tn), lambda j, i, k: (0, j))
    xspec = pl.BlockSpec((tm, kdim), lambda j, i, k: (i, 0))
    ospec = pl.BlockSpec((tm, tn), lambda j, i, k: (i, j))
    outs = outs or [out_dtype]
    grid = (nb, t // tm, 1)
    if w2 is None:
        return _mm(name, grid, [x, w], [xspec, wspec], [NN], [ospec] * len(outs), [_sds((t, n), d) for d in outs],
                   epilogue=epilogue, extras=extras, extra_specs=[ospec] * len(extras), host=host)

    def body(x_ref, w_ref, w2_ref, *o_refs):
        xv = x_ref[...]
        a = jnp.dot(xv, w_ref[...], preferred_element_type=F32)
        b = jnp.dot(xv, w2_ref[...], preferred_element_type=F32)
        for o, v in zip(o_refs, epilogue(a, b)):
            o[...] = v.astype(o.dtype)

    return _call(body, name, grid[:2],
                 [pl.BlockSpec((tm, kdim), lambda j, i: (i, 0)),
                  pl.BlockSpec((None, kdim, tn), lambda j, i: (j, 0, 0)),
                  pl.BlockSpec((None, kdim, tn), lambda j, i: (j, 0, 0))],
                 [pl.BlockSpec((tm, tn), lambda j, i: (i, j))] * len(outs), [_sds((t, n), d) for d in outs], [],
                 [x, w, w2], host)


def mm_nn_k(name, x, w, tm, tk, out_dtype=F32, host=None):
    t, kdim = x.shape
    n = w.shape[1]
    grid = (t // tm, 1, kdim // tk)
    return _mm(name, grid, [x, w],
               [pl.BlockSpec((tm, tk), lambda i, j, k: (i, k)), pl.BlockSpec((tk, n), lambda i, j, k: (k, 0))],
               [NN], [pl.BlockSpec((tm, n), lambda i, j, k: (i, 0))], [_sds((t, n), out_dtype)],
               acc_shape=(tm, n), host=host)[0]


def mm_nt(name, xs, ws, tm, tn, outs=(F32,), epilogue=None, extras=(), host=None):
    t = xs[0].shape[0]
    specs, ops = [], []
    for x, w in zip(xs, ws):
        kdim = x.shape[1]
        specs.append(pl.BlockSpec((tm, kdim), lambda j, i, k: (i, 0)))
        if w.ndim == 3:
            assert tn == w.shape[1]
            n = w.shape[0] * w.shape[1]
            specs.append(pl.BlockSpec((None, tn, kdim), lambda j, i, k: (j, 0, 0)))
        else:
            n = w.shape[0]
            specs.append(pl.BlockSpec((tn, kdim), lambda j, i, k: (j, 0)))
        ops += [x, w]
    ospec = pl.BlockSpec((tm, tn), lambda j, i, k: (i, j))
    return _mm(name, (n // tn, t // tm, 1), ops, specs, [NT] * len(xs), [ospec] * len(outs),
               [_sds((t, n), d) for d in outs], epilogue=epilogue, extras=extras,
               extra_specs=[ospec] * len(extras), host=host)


def mm_nt_k(name, xs, ws, tm, tn, out_dtype=F32, host=None):
    t = xs[0].shape[0]
    s, n, ns = ws[0].shape
    specs, ops = [], []
    for x, w in zip(xs, ws):
        specs.append(pl.BlockSpec((tm, ns), lambda i, j, k: (i, k)))
        specs.append(pl.BlockSpec((None, tn, ns), lambda i, j, k: (k, j, 0)))
        ops += [x, w]
    return _mm(name, (t // tm, n // tn, s), ops, specs, [NT] * len(xs),
               [pl.BlockSpec((tm, tn), lambda i, j, k: (i, j))], [_sds((t, n), out_dtype)], acc_shape=(tm, tn),
               host=host)[0]


def mm_tn(name, x, y, tm, tn, tk, out_dtype, shard_rows=False, shard_cols=False, host=None):
    t, m = x.shape
    n = y.shape[1]
    grid = (m // tm, n // tn, t // tk)
    if shard_cols:
        ospec = pl.BlockSpec((None, tm, tn), lambda i, j, k: (j, i, 0))
        oshape = _sds((n // tn, m, tn), out_dtype)
    elif shard_rows:
        ospec = pl.BlockSpec((None, tm, tn), lambda i, j, k: (i, 0, j))
        oshape = _sds((m // tm, tm, n), out_dtype)
    else:
        ospec = pl.BlockSpec((tm, tn), lambda i, j, k: (i, j))
        oshape = _sds((m, n), out_dtype)
    return _mm(name, grid, [x, y],
               [pl.BlockSpec((tk, tm), lambda i, j, k: (k, i)), pl.BlockSpec((tk, tn), lambda i, j, k: (k, j))],
               [TN], [ospec], [oshape], acc_shape=(tm, tn), host=host)[0]


def _decay_logs():
    return [math.log(1.0 - 2.0 ** (-5.0 - h)) for h in range(HEADS)]


def _log_decay(h):
    lg = jnp.float32(_decay_logs()[0])
    for i in range(1, HEADS):
        lg = jnp.where(h == i, jnp.float32(_decay_logs()[i]), lg)
    return lg


def _decayed_scores(q, k, lg):
    s = lax.dot_general(q, k, NT, preferred_element_type=F32)
    row = lax.broadcasted_iota(jnp.int32, s.shape, 0)
    col = lax.broadcasted_iota(jnp.int32, s.shape, 1)
    dec = jnp.where(col <= row, jnp.exp(jnp.maximum(row - col, 0).astype(F32) * lg), 0.0)
    return s * dec, dec


def _causal(s):
    row = lax.broadcasted_iota(jnp.int32, s.shape, 0)
    col = lax.broadcasted_iota(jnp.int32, s.shape, 1)
    return jnp.where(col <= row, s, -1e30)


def attn_fwd(name, q, k, v, blk, scale, host=None):
    t = q.shape[0]
    dq = q.shape[1] // HEADS
    nq = t // blk

    def body(q_ref, k_ref, v_ref, o_ref, lse_ref, m_ref, l_ref, acc_ref):
        qi = pl.program_id(1)
        qv = q_ref[...]
        m_ref[...] = jnp.full_like(m_ref, -1e30)
        l_ref[...] = jnp.zeros_like(l_ref)
        acc_ref[...] = jnp.zeros_like(acc_ref)

        def keys(start, n, diag_from):
            rows = pl.ds(pl.multiple_of(start, blk), n)
            s = lax.dot_general(qv, k_ref[rows, :], NT, preferred_element_type=F32) * scale
            if diag_from is not None:
                row = lax.broadcasted_iota(jnp.int32, s.shape, 0)
                col = lax.broadcasted_iota(jnp.int32, s.shape, 1)
                s = jnp.where(col - diag_from <= row, s, -1e30)
            m = m_ref[...]
            m_new = jnp.maximum(m, jnp.max(s, axis=-1, keepdims=True))
            p = jnp.exp(s - m_new)
            alpha = jnp.exp(m - m_new)
            m_ref[...] = m_new
            l_ref[...] = alpha * l_ref[...] + jnp.sum(p, axis=-1, keepdims=True)
            acc_ref[...] = alpha * acc_ref[...] + jnp.dot(p.astype(BF16), v_ref[rows, :], preferred_element_type=F32)

        @pl.loop(0, qi // 2)
        def _(j):
            keys(j * (2 * blk), 2 * blk, None)

        @pl.when(qi % 2 == 1)
        def _():
            keys((qi - 1) * blk, 2 * blk, blk)

        @pl.when(qi % 2 == 0)
        def _():
            keys(qi * blk, blk, 0)

        l = l_ref[...]
        o_ref[...] = acc_ref[...] / l
        lse_ref[...] = jnp.broadcast_to(m_ref[...] + jnp.log(l), (blk, HEAD_DIM))

    hspec = pl.BlockSpec((blk, HEAD_DIM), lambda h, i: (i, h))
    return _call(body, name, (HEADS, nq),
                 [pl.BlockSpec((blk, dq), lambda h, i: (i, h)),
                  pl.BlockSpec((t, dq), lambda h, i: (0, h)),
                  pl.BlockSpec((t, HEAD_DIM), lambda h, i: (0, h))],
                 [hspec, hspec], [_sds((t, HEADS * HEAD_DIM), F32)] * 2,
                 [pltpu.VMEM((blk, 1), F32), pltpu.VMEM((blk, 1), F32), pltpu.VMEM((blk, HEAD_DIM), F32)],
                 [q, k, v], host)


def attn_bwd(name, q, k, v, do, o, lse, blk, scale, host=None):
    t = q.shape[0]
    dq_w = q.shape[1] // HEADS
    nb = t // blk

    def body(q_ref, k_ref, v_ref, do_ref, o_ref, lse_ref, dq_ref, dk_ref, dv_ref):
        ki = pl.program_id(1)
        kv = k_ref[...]
        vv = v_ref[...]

        @pl.when(ki == 0)
        def _():
            dq_ref[...] = jnp.zeros_like(dq_ref)

        def queries(start, n, diag):
            rows = pl.ds(pl.multiple_of(start, blk), n)
            qv, dov = q_ref[rows, :], do_ref[rows, :]
            s = lax.dot_general(qv, kv, NT, preferred_element_type=F32) * scale
            if diag:
                s = _causal(s)
            p = jnp.exp(s - lse_ref[rows, :][:, :1])
            dp = lax.dot_general(dov, vv, NT, preferred_element_type=F32)
            delta = jnp.sum(dov.astype(F32) * o_ref[rows, :], axis=-1, keepdims=True)
            ds = p * (dp - delta) * scale
            pb, dsb = p.astype(BF16), ds.astype(BF16)
            dv_ref[...] += lax.dot_general(pb, dov, TN, preferred_element_type=F32)
            dk_ref[...] += lax.dot_general(dsb, qv, TN, preferred_element_type=F32)
            dq_ref[rows, :] += jnp.dot(dsb, kv, preferred_element_type=F32)

        dk_ref[...] = jnp.zeros_like(dk_ref)
        dv_ref[...] = jnp.zeros_like(dv_ref)
        queries(ki * blk, blk, True)
        later = nb - 1 - ki

        @pl.when(later % 2 == 1)
        def _():
            queries((ki + 1) * blk, blk, False)

        @pl.loop(0, later // 2)
        def _(j):
            queries((ki + 1 + later % 2 + 2 * j) * blk, 2 * blk, False)

    full = lambda w: pl.BlockSpec((t, w), lambda h, j: (0, h))
    blkd = lambda w: pl.BlockSpec((blk, w), lambda h, j: (j, h))
    return _call(body, name, (HEADS, nb),
                 [full(dq_w), blkd(dq_w), blkd(HEAD_DIM), full(HEAD_DIM), full(HEAD_DIM), full(HEAD_DIM)],
                 [full(dq_w), blkd(dq_w), blkd(HEAD_DIM)],
                 [_sds(q.shape, F32), _sds(k.shape, F32), _sds(v.shape, F32)], [], [q, k, v, do, o, lse], host)


def _chunk_decays(lg, blk):
    row = lax.broadcasted_iota(jnp.int32, (blk, HEAD_DIM), 0).astype(F32)
    return jnp.exp(lg * (row + 1.0)), jnp.exp(lg * (blk - 1.0 - row)), jnp.exp(lg * blk * jnp.ones((1, HEAD_DIM), F32))


def ret_fwd(name, q, k, v, blk, host=None):
    t = q.shape[0]
    nb = t // blk

    def body(q_ref, k_ref, v_ref, o_ref, st_ref, state):
        h, i = pl.program_id(0), pl.program_id(1)
        lg = _log_decay(h)

        @pl.when(i == 0)
        def _():
            state[...] = jnp.zeros_like(state)

        qv, kv, vv = q_ref[...], k_ref[...], v_ref[...]
        before = state[...]
        st_ref[...] = before
        p, _ = _decayed_scores(qv, kv, lg)
        xi, zeta, g_blk = _chunk_decays(lg, blk)
        o_ref[...] = (jnp.dot(p.astype(BF16), vv, preferred_element_type=F32)
                      + jnp.dot(qv, before.astype(BF16), preferred_element_type=F32) * xi)
        kz = (kv.astype(F32) * zeta).astype(BF16)
        state[...] = before * g_blk + lax.dot_general(kz, vv, TN, preferred_element_type=F32)

    hspec = pl.BlockSpec((blk, HEAD_DIM), lambda h, i: (i, h))
    return _call(body, name, (HEADS, nb), [hspec] * 3,
                 [hspec, pl.BlockSpec((HEAD_DIM, HEAD_DIM), lambda h, i: (i, h))],
                 [_sds((t, HEADS * HEAD_DIM), F32), _sds((nb * HEAD_DIM, HEADS * HEAD_DIM), F32)],
                 [pltpu.VMEM((HEAD_DIM, HEAD_DIM), F32)], [q, k, v], host)


def ret_bwd(name, q, k, v, do, states, blk, host=None):
    t = q.shape[0]
    nb = t // blk

    def body(q_ref, k_ref, v_ref, do_ref, st_ref, dq_ref, dk_ref, dv_ref, dstate):
        h, i = pl.program_id(0), pl.program_id(1)
        lg = _log_decay(h)

        @pl.when(i == 0)
        def _():
            dstate[...] = jnp.zeros_like(dstate)

        qv, kv, vv, dov = q_ref[...], k_ref[...], v_ref[...], do_ref[...]
        before = st_ref[...].astype(BF16)
        after_grad = dstate[...]
        p, dec = _decayed_scores(qv, kv, lg)
        ds = lax.dot_general(dov, vv, NT, preferred_element_type=F32) * dec
        pb, dsb = p.astype(BF16), ds.astype(BF16)
        xi, zeta, g_blk = _chunk_decays(lg, blk)
        dox = (dov.astype(F32) * xi).astype(BF16)
        kz = (kv.astype(F32) * zeta).astype(BF16)
        agb = after_grad.astype(BF16)
        dv_ref[...] = (lax.dot_general(pb, dov, TN, preferred_element_type=F32)
                       + jnp.dot(kz, agb, preferred_element_type=F32))
        dq_ref[...] = (jnp.dot(dsb, kv, preferred_element_type=F32)
                       + lax.dot_general(dox, before, NT, preferred_element_type=F32))
        dk_ref[...] = (lax.dot_general(dsb, qv, TN, preferred_element_type=F32)
                       + lax.dot_general(vv, agb, NT, preferred_element_type=F32) * zeta)
        dstate[...] = after_grad * g_blk + lax.dot_general(qv, dox, TN, preferred_element_type=F32)

    hspec = pl.BlockSpec((blk, HEAD_DIM), lambda h, i: (nb - 1 - i, h))
    return _call(body, name, (HEADS, nb),
                 [hspec] * 4 + [pl.BlockSpec((HEAD_DIM, HEAD_DIM), lambda h, i: (nb - 1 - i, h))],
                 [hspec] * 3, [_sds(q.shape, F32)] * 3, [pltpu.VMEM((HEAD_DIM, HEAD_DIM), F32)],
                 [q, k, v, do, states], host)


CHIP_FLIPS = ((1, 0), (0, 1), (1, 1))


def _position():
    return lax.axis_index("x"), lax.axis_index("y"), lax.axis_index("c")


class _Comm:
    def __init__(self, ins, out_shape, plan, n_remote, n_local):
        self.ins, self.out_shape, self.plan = list(ins), list(out_shape), plan
        self.n_remote, self.n_local = n_remote, n_local
        self.result = None

    def scratch(self):
        return [pltpu.SemaphoreType.DMA((self.n_remote,)), pltpu.SemaphoreType.DMA((self.n_remote,)),
                pltpu.SemaphoreType.DMA((self.n_local,))]

    def _copies(self, in_refs, out_refs, sems):
        send_sems, recv_sems, local_sems = sems
        pos = _position()
        p = self.plan(pos, in_refs, out_refs)

        def remote(k, src, dst, dev):
            return pltpu.make_async_remote_copy(src_ref=src, dst_ref=dst, send_sem=send_sems.at[k],
                                                recv_sem=recv_sems.at[k], device_id=dev, device_id_type=MESH)

        local = [pltpu.make_async_copy(s, d, local_sems.at[i]) for i, (s, d) in enumerate(p["local"])]
        out = [remote(k, s, d, dev) for k, (s, d, dev) in enumerate(p["sends"])]
        arrivals = [functools.partial(remote, k, d, d, pos) for k, d in enumerate(p["recvs"])]
        return local, out, arrivals

    def start(self, in_refs, out_refs, sems):
        local, out, _ = self._copies(in_refs, out_refs, sems)
        for cp in local + out:
            cp.start()

    def finish(self, in_refs, out_refs, sems):
        local, out, arrivals = self._copies(in_refs, out_refs, sems)
        for make in arrivals:
            make().wait_recv()
        for cp in out:
            cp.wait_send()
        for cp in local:
            cp.wait()

    def run(self, name):
        n_in, n_out = len(self.ins), len(self.out_shape)

        def body(*refs):
            comm = (refs[:n_in], refs[n_in:n_in + n_out], refs[n_in + n_out:])
            self.start(*comm)
            self.finish(*comm)

        hbm = pl.BlockSpec(memory_space=pl.ANY)
        self.result = pl.pallas_call(body, name=name, in_specs=[hbm] * n_in, out_specs=[hbm] * n_out,
                                     out_shape=self.out_shape, scratch_shapes=self.scratch())(*self.ins)
        return self.result


def _half_rows(c, rows):
    r2 = rows // 2
    return pl.ds(pl.multiple_of(c * r2, math.gcd(r2, LANES)), r2)


def _half(ref, c, rows, lead=()):
    return ref.at[(*lead, _half_rows(c, rows))]


def gather_halves(shards):
    def plan(pos, ins, outs):
        x, y, c = pos
        me = 2 * x + y
        p = dict(local=[], sends=[], recvs=[])
        for a, (src, dst) in enumerate(zip(ins, outs)):
            rows = shards[a].shape[0]
            p["local"].append((src, dst.at[me]))
            for fx, fy in CHIP_FLIPS:
                px, py = x ^ fx, y ^ fy
                p["sends"].append((_half(src, c, rows), _half(dst, c, rows, (me,)), (px, py, c)))
                p["recvs"].append(_half(dst, c, rows, (2 * px + py,)))
        return p

    return _Comm(shards, [_sds((N_CHIPS, *s.shape), s.dtype) for s in shards], plan,
                 n_remote=3 * len(shards), n_local=len(shards))


def chip_exchange(parts):
    def plan(pos, ins, outs):
        x, y, c = pos
        me = 2 * x + y
        p = dict(local=[], sends=[], recvs=[])
        for src, dst in zip(ins, outs):
            p["local"].append((src.at[me], dst.at[me]))
            for fx, fy in CHIP_FLIPS:
                px, py = x ^ fx, y ^ fy
                peer = 2 * px + py
                p["sends"].append((src.at[peer], dst.at[me], (px, py, c)))
                p["recvs"].append(dst.at[peer])
        return p

    return _Comm(parts, [_sds(g.shape, g.dtype) for g in parts], plan, n_remote=3 * len(parts), n_local=len(parts))


def all_gather_devices(v):
    flips = [(fx, fy, fc) for fx in (0, 1) for fy in (0, 1) for fc in (0, 1)][1:]

    def plan(pos, ins, outs):
        x, y, c = pos
        me = 4 * x + 2 * y + c
        p = dict(local=[(ins[0], outs[0].at[me])], sends=[], recvs=[])
        for fx, fy, fc in flips:
            px, py, pc = x ^ fx, y ^ fy, c ^ fc
            p["sends"].append((ins[0], outs[0].at[me], (px, py, pc)))
            p["recvs"].append(outs[0].at[4 * px + 2 * py + pc])
        return p

    return _Comm([v], [_sds((8, *v.shape), v.dtype)], plan, n_remote=7, n_local=1).run("small_all_gather")[0]


SWAP_CHUNK_BYTES = 3 * 2 ** 19


def _sibling_stream(t, n, value, consume, sbuf, rbuf, send_sems, recv_sems, credits):
    x, y, c = _position()
    sib = (x, y, 1 - c)

    def copy(slot):
        return pltpu.make_async_remote_copy(src_ref=sbuf.at[slot], dst_ref=rbuf.at[slot], send_sem=send_sems.at[slot],
                                            recv_sem=recv_sems.at[slot], device_id=sib, device_id_type=MESH)

    slot = t % 2

    @pl.when(jnp.logical_and(t >= 2, t < n))
    def _():
        copy(slot).wait_send()
        pl.semaphore_wait(credits.at[slot], 1)

    @pl.when(t < n)
    def _():
        sbuf[slot] = value
        copy(slot).start()

    @pl.when(t >= 1)
    def _():
        prev = 1 - slot
        copy(prev).wait_recv()
        consume(sbuf[prev], rbuf[prev])

        @pl.when(t + 1 < n)
        def _():
            pl.semaphore_signal(credits.at[prev], inc=1, device_id=sib, device_id_type=MESH)

    @pl.when(t == n)
    def _():
        copy(1 - slot).wait_send()
        if n > 1:
            copy(slot).wait_send()


def _swap_scratch(rows, cols, dtype):
    return [pltpu.VMEM((2, rows, cols), dtype), pltpu.VMEM((2, rows, cols), dtype),
            pltpu.SemaphoreType.DMA((2,)), pltpu.SemaphoreType.DMA((2,)), pltpu.SemaphoreType.REGULAR((2,))]


def _chunk_rows(rows, cols, dtype):
    return _tile(rows, max(16, SWAP_CHUNK_BYTES // (cols * jnp.dtype(dtype).itemsize)), 16)


def pair_add(name, g):
    s, r, c_ = g.shape
    r2 = r // 2
    cr = _chunk_rows(r2, c_, g.dtype)
    nj = r2 // cr

    n = s * nj

    def body(core, mine_ref, theirs_ref, o_ref, *scratch):
        def consume(_, got):
            o_ref[...] = (mine_ref[...].astype(F32) + got.astype(F32)).astype(o_ref.dtype)

        _sibling_stream(pl.program_id(0), n, theirs_ref[...], consume, *scratch)

    sent = lambda t: jnp.minimum(t, n - 1)
    used = lambda t: jnp.maximum(t - 1, 0)
    grid_spec = pltpu.PrefetchScalarGridSpec(
        num_scalar_prefetch=1, grid=(n + 1,),
        in_specs=[pl.BlockSpec((None, cr, c_), lambda t, core: (used(t) // nj, core[0] * nj + used(t) % nj, 0)),
                  pl.BlockSpec((None, cr, c_), lambda t, core: (sent(t) // nj, (1 - core[0]) * nj + sent(t) % nj, 0))],
        out_specs=pl.BlockSpec((None, cr, c_), lambda t, core: (used(t) // nj, used(t) % nj, 0)),
        scratch_shapes=_swap_scratch(cr, c_, g.dtype))
    core = lax.axis_index("c").astype(jnp.int32).reshape(1)
    return pl.pallas_call(body, name=name, grid_spec=grid_spec, out_shape=_sds((s, r2, c_), g.dtype),
                          compiler_params=_params("arbitrary"))(core, g, g)


def reduce_join(name, p):
    s, r2, c_ = p.shape
    cr = _chunk_rows(r2, c_, F32)
    nj = r2 // cr

    def body(core, p_ref, o_ref, *scratch):
        acc = p_ref[0].astype(F32)
        for i in range(1, s):
            acc = acc + p_ref[i].astype(F32)

        def consume(own, got):
            c = core[0]
            o_ref[c] = own
            o_ref[1 - c] = got

        _sibling_stream(pl.program_id(0), nj, acc, consume, *scratch)

    grid_spec = pltpu.PrefetchScalarGridSpec(
        num_scalar_prefetch=1, grid=(nj + 1,),
        in_specs=[pl.BlockSpec((s, cr, c_), lambda t, core: (0, jnp.minimum(t, nj - 1), 0))],
        out_specs=pl.BlockSpec((2, cr, c_), lambda t, core: (0, jnp.maximum(t - 1, 0), 0)),
        scratch_shapes=_swap_scratch(cr, c_, F32))
    core = lax.axis_index("c").astype(jnp.int32).reshape(1)
    out = pl.pallas_call(body, name=name, grid_spec=grid_spec, out_shape=_sds((2, r2, c_), F32),
                         compiler_params=_params("arbitrary"))(core, p)
    return out.reshape(2 * r2, c_)


def sibling_fill(name, buf):
    s, r, c_ = buf.shape
    r2 = r // 2
    cr = _chunk_rows(r2, c_, buf.dtype)
    nj = r2 // cr
    n_peers = len(CHIP_FLIPS)

    n = n_peers * nj

    def body(where, in_ref, o_ref, *scratch):
        def consume(_, got):
            o_ref[...] = got

        _sibling_stream(pl.program_id(0), n, in_ref[...], consume, *scratch)

    sent = lambda t: jnp.minimum(t, n - 1)
    used = lambda t: jnp.maximum(t - 1, 0)
    grid_spec = pltpu.PrefetchScalarGridSpec(
        num_scalar_prefetch=1, grid=(n + 1,),
        in_specs=[pl.BlockSpec((None, cr, c_),
                               lambda t, where: (where[sent(t) // nj], where[n_peers] * nj + sent(t) % nj, 0))],
        out_specs=pl.BlockSpec((None, cr, c_),
                               lambda t, where: (where[used(t) // nj], (1 - where[n_peers]) * nj + used(t) % nj, 0)),
        scratch_shapes=_swap_scratch(cr, c_, buf.dtype))
    x, y, c = _position()
    where = jnp.stack([2 * (x ^ fx) + (y ^ fy) for fx, fy in CHIP_FLIPS] + [c]).astype(jnp.int32)
    return pl.pallas_call(body, name=name, grid_spec=grid_spec, out_shape=_sds(buf.shape, buf.dtype),
                          input_output_aliases={1: 0}, compiler_params=_params("arbitrary"))(where, buf)


def sum_slots(name, p, out_dtype):
    s, r, c = p.shape
    tr = _tile(r, 256, 16)

    def body(p_ref, o_ref):
        acc = p_ref[0].astype(F32)
        for i in range(1, s):
            acc = acc + p_ref[i].astype(F32)
        o_ref[...] = acc.astype(o_ref.dtype)

    return pl.pallas_call(
        body, name=name, grid=(r // tr,), in_specs=[pl.BlockSpec((s, tr, c), lambda i: (0, i, 0))],
        out_specs=pl.BlockSpec((tr, c), lambda i: (i, 0)), out_shape=_sds((r, c), out_dtype),
        compiler_params=_params("arbitrary"),
    )(p)


def adamw(name, w, g, m, v):
    r, c = w.shape
    tr = _tile(r, 256, 8)

    def fn(_, w, g, m, v):
        m = ADAM_B1 * m + (1.0 - ADAM_B1) * g
        v = ADAM_B2 * v + (1.0 - ADAM_B2) * (g * g)
        m_hat = m / (1.0 - ADAM_B1 ** ADAM_STEP)
        v_hat = v / (1.0 - ADAM_B2 ** ADAM_STEP)
        delta = -ADAM_LR * (m_hat / (jnp.sqrt(v_hat) + ADAM_EPS) + ADAM_WD * w)
        return [delta, m, v, g], []

    outs, _ = _rowwise(name, fn, [w, g, m, v], [], [(c, F32)] * 4, [], tr)
    return outs


RET_SCALE = HEAD_DIM ** -0.5
MLA_SCALE = (HEAD_DIM + MLA_ROPE) ** -0.5
GRAD_DT = BF16
IN_RET = 4 * HEADS * HEAD_DIM
IN_MLA = MLA_Q_RANK + MLA_KV_RANK + MLA_ROPE
IN_MLA_PAD = IN_MLA + 64


def _heads(fn):
    return jnp.concatenate([fn(h) for h in range(HEADS)], axis=1)


def _head(a, h, stride=HEAD_DIM, off=0):
    return a[:, h * stride + off:h * stride + off + HEAD_DIM]


def _group_norm(o):
    rs = [lax.rsqrt(jnp.mean(_head(o, h) * _head(o, h), axis=-1, keepdims=True) + EPS) for h in range(HEADS)]
    return _heads(lambda h: _head(o, h) * rs[h]), rs


class _Alone:
    def host(self, kernel_name):
        return None

    def done(self, kernel_name, w):
        pass

    def grads(self, g):
        pass


def _ffn_fwd(tag, n, w, k, tm, tmk, plan):
    gate, up, down = tag + "_gate", tag + "_up", tag + "_down"
    if "wu" + k in w:
        g, u, a = mm_nn(up, n, w["wg" + k], tm, outs=[BF16] * 3, w2=w["wu" + k],
                        epilogue=lambda g, u: (g, u, _silu(g) * u), host=plan.host(up))
    else:
        (g,) = mm_nn(gate, n, w["wg" + k], tm, out_dtype=BF16, host=plan.host(gate))
        plan.done(gate, w)
        u, a = mm_nn(up, n, w["wu" + k], tm, outs=[BF16] * 2, extras=(g,),
                     epilogue=lambda u, g: (u, _silu(g.astype(F32)) * u), host=plan.host(up))
    plan.done(up, w)
    ff = a.shape[1]
    wd = w["wd" + k]
    f = mm_nn_k(down, a, wd.reshape(ff, wd.shape[2]), tmk, _tile(ff, 1408), host=plan.host(down))
    plan.done(down, w)
    return g, u, a, f


def _ffn_bwd(tag, df, n, g, u, a, wg, wu, wd, tm, tmk, tk, dt, plan):
    ns = wg.shape[2]

    def gate_grads(da, g, u):
        g, u = g.astype(F32), u.astype(F32)
        return da * u * _dsilu(g), da * _silu(g)

    def hosted(kernel_name, call):
        out = call(plan.host(kernel_name))
        plan.done(kernel_name, None)
        return out

    k = tag[-1]
    dg, du = hosted(tag + "_da", lambda h: mm_nt(tag + "_da", [df], [wd], tm, ns, outs=(BF16, BF16),
                                                 epilogue=gate_grads, extras=(g, u), host=h))
    dwg = hosted(tag + "_dwg", lambda h: mm_tn(tag + "_dwg", n, dg, dt, ns, tk, GRAD_DT, shard_cols=True, host=h))
    plan.grads({"wg" + k: dwg})
    dwu = hosted(tag + "_dwu", lambda h: mm_tn(tag + "_dwu", n, du, dt, ns, tk, GRAD_DT, shard_cols=True, host=h))
    plan.grads({"wu" + k: dwu})
    dwd = hosted(tag + "_dwd", lambda h: mm_tn(tag + "_dwd", a, df, ns, dt, tk, GRAD_DT, shard_rows=True, host=h))
    plan.grads({"wd" + k: dwd})
    dn = hosted(tag + "_dn", lambda h: mm_nt_k(tag + "_dn", [dg, du], [wg, wu], tmk, dt, host=h))
    return dn


def _local_step(x, tgt, meta, w, nw, plan):
    seq, d = x.shape
    t_real = N_META + seq
    tp = -(-t_real // LANES) * LANES
    zpad = jnp.zeros((tp - t_real, d), F32)
    h0 = jnp.concatenate([meta, x, zpad], axis=0)
    tgt_p = jnp.concatenate([jnp.zeros((N_META, d), F32), tgt, zpad], axis=0)
    cos_r, sin_r, cos_m, sa, sb = _rope_tables(tp)
    tm = _tile(tp, 512)
    tmt = _tile(tp, 768, 16)
    tmk = _tile(tp, 1408)
    tk = tp
    blk = tm
    dt = _tile(d, 1024)
    hw = HEADS * HEAD_DIM
    qw = HEADS * MLA_QK_PAD

    (n1,), _ = _rowwise("ffn1_norm", lambda r0, h, g: ([_rms(h, g)], []), [h0], [nw["ffn1_pre_norm"]],
                        [(d, BF16)], [], tm)
    g1, u1, a1, f1 = _ffn_fwd("ffn1", n1, w, "1", tm, tmk, plan)

    def post_ffn1(r0, h, f, post, pre):
        h1 = h + 0.5 * _rms(f, post)
        return [h1, _rms(h1, pre)], []

    (h1, un), _ = _rowwise("mix_norm", post_ffn1, [h0, f1], [nw["ffn1_post_norm"], nw["mix_pre_norm"]],
                           [(d, F32), (d, BF16)], [], tm)
    (proj_r,) = mm_nn("proj_r", un, w["w_r"], tm, n_block=_tile(IN_RET, 1024), host=plan.host("proj_r"))
    plan.done("proj_r", w)
    (proj_c,) = mm_nn("proj_c", un, w["w_c"], tm)

    def split_proj(r0, pr, pc, cr, sr, cm, ta, tb, qn, kvn):
        rq = _heads(lambda h: _rope_r(_head(pr, h), cr, sr))
        rk = _heads(lambda h: _rope_r(_head(pr, h, off=hw), cr, sr) * RET_SCALE)
        rv = pr[:, 2 * hw:3 * hw]
        cqn = _rms(pc[:, :MLA_Q_RANK], qn)
        ckvn = _rms(pc[:, MLA_Q_RANK:MLA_Q_RANK + MLA_KV_RANK], kvn)
        krr = _rope_m(pc[:, MLA_Q_RANK + MLA_KV_RANK:], cm, ta, tb)
        return [rq, rk, rv, cqn, ckvn, krr], []

    (rq, rk, rv, cqn, ckvn, krr), _ = _rowwise(
        "split_proj", split_proj, [proj_r, proj_c, cos_r, sin_r, cos_m, sa, sb],
        [nw["mla_q_norm"], nw["mla_kv_norm"]],
        [(hw, BF16), (hw, BF16), (hw, BF16), (MLA_Q_RANK, BF16), (MLA_KV_RANK, BF16), (LANES, F32)], [], tm)
    (qp,) = mm_nn("q_up", cqn, w["wuq"], tm)
    (kn,) = mm_nn("k_up", ckvn, w["wuk"], tm)
    (vv,) = mm_nn("v_up", ckvn, w["wuv"], tm, out_dtype=BF16)

    def build_qk(r0, qp, kn, krr, cm, ta, tb):
        qc = jnp.concatenate(
            [part for h in range(HEADS)
             for part in (_head(qp, h, MLA_QK_PAD), _rope_m(_head(qp, h, MLA_QK_PAD, HEAD_DIM), cm, ta, tb))], axis=1)
        kc = jnp.concatenate([part for h in range(HEADS) for part in (_head(kn, h), krr)], axis=1)
        return [qc, kc], []

    (qc, kc), _ = _rowwise("build_qk", build_qk, [qp, kn, krr, cos_m, sa, sb], [], [(qw, BF16), (qw, BF16)], [], tm)
    o_m, lse = attn_fwd("mla_fwd", qc, kc, vv, blk, MLA_SCALE, host=plan.host("mla_fwd"))
    plan.done("mla_fwd", w)
    o_r, ret_states = ret_fwd("ret_fwd", rq, rk, rv, blk, host=plan.host("ret_fwd"))
    plan.done("ret_fwd", w)

    def gate_mix(r0, rg, o_r, o_m, gn):
        y, _ = _group_norm(o_r)
        return [jnp.concatenate([_silu(rg) * (y * gn), o_m], axis=1)], []

    (mixcat,), _ = _rowwise("gate_mix", gate_mix, [(proj_r, hw, 3), o_r, o_m], [nw["ret_group_norm"]],
                            [(2 * hw, BF16)], [], tm)
    (mix,) = mm_nn("mix_out", mixcat, w["w_out"], tm, n_block=dt)

    def post_mix(r0, h, m, post, pre):
        h2 = h + _rms(m, post)
        return [h2, _rms(h2, pre)], []

    (h2, n3), _ = _rowwise("ffn2_norm", post_mix, [h1, mix], [nw["mix_post_norm"], nw["ffn2_pre_norm"]],
                           [(d, F32), (d, BF16)], [], tm)
    g2, u2, a2, f2 = _ffn_fwd("ffn2", n3, w, "2", tm, tmk, plan)

    def loss_head(r0, h, f, t, post):
        h3 = h + 0.5 * _rms(f, post)
        row = r0 + lax.broadcasted_iota(jnp.int32, h3.shape, 0)
        err = jnp.where(row >= N_META, jnp.where(row < t_real, h3 - t, 0.0), 0.0)
        dh3 = err / d
        df, dpost = _rms_bwd(f, post, 0.5 * dh3)
        return [dh3, df], [_colsum(err * err), _colsum(dpost)]

    (dh3, df2), (loss_vec, d_post2) = _rowwise("loss_head", loss_head, [h2, f2, tgt_p], [nw["ffn2_post_norm"]],
                                               [(d, F32), (d, BF16)], [d, d], tm)
    loss = 0.5 * jnp.sum(loss_vec) / d
    dn3 = _ffn_bwd("ffn2", df2, n3, g2, u2, a2, w["wg2"], w["wu2"], w["wd2"], tmt, tmk, tk, dt, plan)

    def back_mix_norm(r0, h, m, dh3, dn, pre, post):
        dx, dpre = _rms_bwd(h, pre, dn)
        dh2 = dh3 + dx
        dm, dpost = _rms_bwd(m, post, dh2)
        return [dh2, dm], [_colsum(dpre), _colsum(dpost)]

    (dh2, dmix), (d_pre2, d_mix_post) = _rowwise(
        "back_mix_norm", back_mix_norm, [h2, mix, dh3, dn3], [nw["ffn2_pre_norm"], nw["mix_post_norm"]],
        [(d, F32), (d, BF16)], [d, d], tm)
    (dmixcat,) = mm_nt("mix_dx", [dmix], [w["w_out"]], tmt, _tile(2 * hw, 512))
    dw_out = mm_tn("mix_dw", mixcat, dmix, 2 * hw // N_CHIPS, dt, tk, GRAD_DT, shard_rows=True)

    def back_gate(r0, dmc, rg, o_r, gn):
        d_ret, d_om = dmc[:, :hw], dmc[:, hw:]
        yh, rs = _group_norm(o_r)
        d_rg = d_ret * (yh * gn) * _dsilu(rg)
        dy = d_ret * _silu(rg)
        gyh = dy * gn
        d_or = _heads(lambda h: rs[h] * (_head(gyh, h) - _head(yh, h) * jnp.mean(_head(gyh, h) * _head(yh, h),
                                                                                    axis=-1, keepdims=True)))
        return [d_or, d_rg, d_om], [_colsum(dy * yh)]

    (d_or, d_rg, d_om), (d_gn,) = _rowwise("back_gate", back_gate, [dmixcat, (proj_r, hw, 3), o_r],
                                           [nw["ret_group_norm"]], [(hw, BF16), (hw, F32), (hw, BF16)], [hw], tm)
    dqc, dkc, dvv = attn_bwd("mla_bwd", qc, kc, vv, d_om, o_m, lse, blk, MLA_SCALE, host=plan.host("mla_bwd"))
    plan.done("mla_bwd", None)
    drq, drk, drv = ret_bwd("ret_bwd", rq, rk, rv, d_or, ret_states, blk)

    def back_qk(r0, dqc, dkc, dvv, cm, ta, tb):
        dqp = jnp.concatenate(
            [part for h in range(HEADS)
             for part in (_head(dqc, h, MLA_QK_PAD), _rope_m_t(_head(dqc, h, MLA_QK_PAD, HEAD_DIM), cm, ta, tb))],
            axis=1)
        dkn = _heads(lambda h: _head(dkc, h, MLA_QK_PAD))
        dkr = _head(dkc, 0, MLA_QK_PAD, HEAD_DIM)
        for h in range(1, HEADS):
            dkr = dkr + _head(dkc, h, MLA_QK_PAD, HEAD_DIM)
        return [dqp, dkn, _rope_m_t(dkr, cm, ta, tb), dvv], []

    (dqp, dkn, dkr, dvb), _ = _rowwise("back_qk", back_qk, [dqc, dkc, dvv, cos_m, sa, sb], [],
                                       [(qw, BF16), (hw, BF16), (LANES, F32), (hw, BF16)], [], tm)
    (dcqn,) = mm_nt("q_dx", [dqp], [w["wuq"]], tm, MLA_Q_RANK)
    dwuq = mm_tn("q_dw", cqn, dqp, MLA_Q_RANK, _tile(qw, 1024), tk, GRAD_DT)
    (dckvn,) = mm_nt("kv_dx", [dkn, dvb], [w["wuk"], w["wuv"]], tm, MLA_KV_RANK)
    dwuk = mm_tn("k_dw", ckvn, dkn, MLA_KV_RANK, hw, tk, GRAD_DT)
    dwuv = mm_tn("v_dw", ckvn, dvb, MLA_KV_RANK, hw, tk, GRAD_DT)

    def back_proj(r0, drq, drk, drv, d_rg, pc, dcqn, dckvn, dkr, cr, sr, qn, kvn):
        d_q = _heads(lambda h: _rope_r_t(_head(drq, h), cr, sr))
        d_k = _heads(lambda h: _rope_r_t(_head(drk, h), cr, sr) * RET_SCALE)
        dcq, a_q = _rms_bwd(pc[:, :MLA_Q_RANK], qn, dcqn)
        dckv, a_kv = _rms_bwd(pc[:, MLA_Q_RANK:MLA_Q_RANK + MLA_KV_RANK], kvn, dckvn)
        return ([jnp.concatenate([d_q, d_k, drv, d_rg], axis=1), jnp.concatenate([dcq, dckv, dkr], axis=1)],
                [_colsum(a_q), _colsum(a_kv)])

    (dproj_r, dproj_c), (d_qn, d_kvn) = _rowwise(
        "back_proj", back_proj, [drq, drk, drv, d_rg, proj_c, dcqn, dckvn, dkr, cos_r, sin_r],
        [nw["mla_q_norm"], nw["mla_kv_norm"]], [(IN_RET, BF16), (IN_MLA_PAD, BF16)],
        [MLA_Q_RANK, MLA_KV_RANK], tm)
    (dun,) = mm_nt("proj_dx", [dproj_r, dproj_c], [w["w_r"], w["w_c"]], tmt, _tile(d, 512))
    dw_r = mm_tn("proj_dw_r", un, dproj_r, dt, _tile(IN_RET, 1024), tk, GRAD_DT)
    dw_c = mm_tn("proj_dw_c", un, dproj_c, dt, IN_MLA_PAD, tk, GRAD_DT)
    plan.grads(dict(w_r=dw_r, w_c=dw_c, wuq=dwuq, wuk=dwuk, wuv=dwuv, w_out=dw_out))

    def back_ffn1_norm(r0, h, f, dh2, dn, pre, post):
        dx, dpre = _rms_bwd(h, pre, dn)
        dh1 = dh2 + dx
        df, dpost = _rms_bwd(f, post, 0.5 * dh1)
        return [dh1, df], [_colsum(dpre), _colsum(dpost)]

    (dh1, df1), (d_mix_pre, d_post1) = _rowwise(
        "back_ffn1_norm", back_ffn1_norm, [h1, f1, dh2, dun], [nw["mix_pre_norm"], nw["ffn1_post_norm"]],
        [(d, F32), (d, BF16)], [d, d], tm)
    dn1 = _ffn_bwd("ffn1", df1, n1, g1, u1, a1, w["wg1"], w["wu1"], w["wd1"], tmt, tmk, tk, dt, plan)

    def back_input(r0, h, dh1, dn, pre):
        dx, dpre = _rms_bwd(h, pre, dn)
        return [dh1 + dx], [_colsum(dpre)]

    (dh0,), (d_pre1,) = _rowwise("back_input", back_input, [h0, dh1, dn1], [nw["ffn1_pre_norm"]], [(d, F32)], [d], tm)

    small = dict(ffn1_pre_norm=d_pre1, ffn1_post_norm=d_post1, mix_pre_norm=d_mix_pre, ret_group_norm=d_gn,
                 mla_q_norm=d_qn, mla_kv_norm=d_kvn, mix_post_norm=d_mix_post, ffn2_pre_norm=d_pre2,
                 ffn2_post_norm=d_post2)
    return loss, dh0[N_META:t_real], small, dh0[:N_META]


WEIGHTS = ("meta_tokens", "ffn1_pre_norm", "ffn1_w_gate", "ffn1_w_up", "ffn1_w_down", "ffn1_post_norm",
           "mix_pre_norm", "w_in", "ret_group_norm", "mla_q_norm", "mla_w_uq", "mla_kv_norm", "mla_w_uk",
           "mla_w_uv", "w_out", "mix_post_norm", "ffn2_pre_norm", "ffn2_w_gate", "ffn2_w_up", "ffn2_w_down",
           "ffn2_post_norm")
BIG = ("ffn1_w_gate", "ffn1_w_up", "ffn1_w_down", "w_in", "mla_w_uq", "mla_w_uk", "mla_w_uv", "w_out",
       "ffn2_w_gate", "ffn2_w_up", "ffn2_w_down")
NORMS = ("ffn1_pre_norm", "ffn1_post_norm", "mix_pre_norm", "ret_group_norm", "mla_q_norm", "mla_kv_norm",
         "mix_post_norm", "ffn2_pre_norm", "ffn2_post_norm")


def _unshard_cols(g):
    return g.transpose(1, 0, 2).reshape(g.shape[1], -1)


def _shard_cols(a):
    return a.reshape(a.shape[0], N_CHIPS, -1).transpose(1, 0, 2)


def _pack_rows(rows, width):
    rows = [jnp.pad(r, ((0, 0), (0, width - r.shape[1]))) for r in rows]
    n = sum(r.shape[0] for r in rows)
    return jnp.pad(jnp.concatenate(rows, axis=0), ((0, -n % 8), (0, 0)))


def _weight_views(full):
    w = {}
    for n, g in full.items():
        if n == "w_in":
            w_in = _unshard_cols(g)
            w["w_r"] = w_in[:, :IN_RET]
            w["w_c"] = jnp.pad(w_in[:, IN_RET:], ((0, 0), (0, IN_MLA_PAD - IN_MLA)))
        elif n == "mla_w_uq":
            q = _unshard_cols(g).reshape(MLA_Q_RANK, HEADS, HEAD_DIM + MLA_ROPE)
            q = jnp.pad(q, ((0, 0), (0, 0), (0, MLA_QK_PAD - HEAD_DIM - MLA_ROPE)))
            w["wuq"] = q.reshape(MLA_Q_RANK, HEADS * MLA_QK_PAD)
        elif n in ("mla_w_uk", "mla_w_uv"):
            w["wu" + n[-1]] = _unshard_cols(g)
        elif n == "w_out":
            w["w_out"] = g.reshape(-1, g.shape[2])
        else:
            w["w" + n[7] + n[3]] = g
    return w


def _contributions(g):
    ffn = {"g": "gate", "u": "up", "d": "down"}
    c = {f"ffn{n[2]}_w_{ffn[n[1]]}": a for n, a in g.items() if len(n) == 3 and n[2] in "12"}
    if "w_r" in g:
        dwuq = g["wuq"].reshape(MLA_Q_RANK, HEADS, MLA_QK_PAD)[:, :, :HEAD_DIM + MLA_ROPE]
        c.update(w_in=_shard_cols(jnp.concatenate([g["w_r"], g["w_c"][:, :IN_MLA]], axis=1)),
                 mla_w_uq=_shard_cols(dwuq.reshape(MLA_Q_RANK, -1)), mla_w_uk=_shard_cols(g["wuk"]),
                 mla_w_uv=_shard_cols(g["wuv"]), w_out=g["w_out"])
    return c


class _Schedule(_Alone):
    FIRST = ("ffn1_w_gate",)
    CARRIED = {"ffn1_gate": ("ffn1_w_up",), "ffn1_up": ("ffn1_w_down",),
               "ffn1_down": ("w_in", "mla_w_uq", "mla_w_uk", "mla_w_uv"),
               "proj_r": ("w_out",), "mla_fwd": ("ffn2_w_gate", "ffn2_w_up"), "ffn2_up": ("ffn2_w_down",)}
    GRAD_HOST = dict(ffn2_w_gate="mla_bwd", ffn2_w_up="mla_bwd", ffn2_w_down="mla_bwd",
                     w_in="ffn1_da", mla_w_uq="ffn1_da", mla_w_uk="ffn1_da", mla_w_uv="ffn1_da", w_out="ffn1_da",
                     ffn1_w_gate="ffn1_dwu", ffn1_w_up="ffn1_dwd", ffn1_w_down="ffn1_dn")

    def __init__(self, shards):
        self.gathers = {k: (gather_halves([shards[n] for n in names]), names) for k, names in self.CARRIED.items()}
        self.waiting = {}
        self.exchanges = {}
        self.grad = {}

    def host(self, kernel_name):
        if kernel_name in self.gathers:
            return self.gathers[kernel_name][0]
        if kernel_name in self.waiting:
            names, sums = zip(*self.waiting.pop(kernel_name))
            self.exchanges[kernel_name] = (chip_exchange(list(sums)), names)
            return self.exchanges[kernel_name][0]
        return None

    def done(self, kernel_name, w):
        if kernel_name in self.gathers:
            comm, names = self.gathers[kernel_name]
            w.update(_weight_views({n: sibling_fill("fill_" + n, b) for n, b in zip(names, comm.result)}))
        elif kernel_name in self.exchanges:
            comm, names = self.exchanges[kernel_name]
            for n, q in zip(names, comm.result):
                self.grad[n] = reduce_join("reduce_join_" + n, q)

    def grads(self, g):
        for n, a in _contributions(g).items():
            self.waiting.setdefault(self.GRAD_HOST[n], []).append((n, pair_add("pair_add_" + n, a)))


def _step(p, m, v, x, loss_target):
    d = x.shape[2]
    names = list(BIG)
    shards = {n: p[n][0].astype(BF16) for n in names}
    first = _Schedule.FIRST
    gathered = gather_halves([shards[n] for n in first] + [p["meta_tokens"]]).run("gather_first")
    filled = [sibling_fill("fill_" + n, b) for n, b in zip(first + ("meta_tokens",), gathered)]
    w = _weight_views(dict(zip(first, filled[:-1])))
    meta = _unshard_cols(filled[-1])
    nw = {n: p[n] for n in NORMS}
    plan = _Schedule(shards)
    loss, grad_x, small, d_meta = _local_step(x[0], loss_target[0], meta, w, nw, plan)
    grads = dict(plan.grad)

    width = max(d, HEADS * HEAD_DIM)
    packed = _pack_rows([small[n] for n in NORMS] + [d_meta], width)
    total = sum_slots("small_sum", all_gather_devices(packed), F32)
    for i, n in enumerate(NORMS):
        grads[n] = total[i:i + 1, :p[n].shape[1]]
    cols = p["meta_tokens"].shape[1]
    chip = 2 * lax.axis_index("x") + lax.axis_index("y")
    grads["meta_tokens"] = lax.dynamic_slice(total[len(NORMS):len(NORMS) + N_META, :d], (0, chip * cols), (N_META, cols))

    delta, new_m, new_v = {}, {}, {}
    for n in names + ["meta_tokens"]:
        shape = p[n].shape
        flat = lambda a: a.reshape(-1, shape[-1])
        out = adamw("adamw_" + n, flat(p[n]), flat(grads[n]), flat(m[n]), flat(v[n]))
        delta[n], new_m[n], new_v[n], grads[n] = (o.reshape(shape) for o in out)
    pk = lambda src: _pack_rows([src[n] for n in NORMS], width)
    out = adamw("adamw_norms", pk(p), pk(grads), pk(m), pk(v))
    for i, n in enumerate(NORMS):
        delta[n], new_m[n], new_v[n] = (o[i:i + 1, :p[n].shape[1]] for o in out[:3])

    loss = lax.psum(loss, ("x", "y", "c"))
    return loss, grad_x[None], grads, delta, new_m, new_v


def kernel(x, meta_tokens, ffn1_pre_norm, ffn1_w_gate, ffn1_w_up, ffn1_w_down, ffn1_post_norm, mix_pre_norm, w_in, ret_group_norm, mla_q_norm, mla_w_uq, mla_kv_norm, mla_w_uk, mla_w_uv, w_out, mix_post_norm, ffn2_pre_norm, ffn2_w_gate, ffn2_w_up, ffn2_w_down, ffn2_post_norm, loss_target, m_meta_tokens, m_ffn1_pre_norm, m_ffn1_w_gate, m_ffn1_w_up, m_ffn1_w_down, m_ffn1_post_norm, m_mix_pre_norm, m_w_in, m_ret_group_norm, m_mla_q_norm, m_mla_w_uq, m_mla_kv_norm, m_mla_w_uk, m_mla_w_uv, m_w_out, m_mix_post_norm, m_ffn2_pre_norm, m_ffn2_w_gate, m_ffn2_w_up, m_ffn2_w_down, m_ffn2_post_norm, v_meta_tokens, v_ffn1_pre_norm, v_ffn1_w_gate, v_ffn1_w_up, v_ffn1_w_down, v_ffn1_post_norm, v_mix_pre_norm, v_w_in, v_ret_group_norm, v_mla_q_norm, v_mla_w_uq, v_mla_kv_norm, v_mla_w_uk, v_mla_w_uv, v_w_out, v_mix_post_norm, v_ffn2_pre_norm, v_ffn2_w_gate, v_ffn2_w_up, v_ffn2_w_down, v_ffn2_post_norm):
    args = locals()
    p = {n: args[n] for n in WEIGHTS}
    m = {n: args["m_" + n] for n in WEIGHTS}
    v = {n: args["v_" + n] for n in WEIGHTS}
    loss, grad_x, grads, delta, new_m, new_v = _step(p, m, v, x, loss_target)
    return (loss, grad_x, *[grads[n] for n in WEIGHTS], *[delta[n] for n in WEIGHTS],
            *[new_m[n] for n in WEIGHTS], *[new_v[n] for n in WEIGHTS])
```

```python
import functools
import math

import jax
import jax.numpy as jnp
from jax import lax
from jax.experimental import pallas as pl
from jax.experimental.pallas import tpu as pltpu

F32 = jnp.float32
BF16 = jnp.bfloat16

EPS = 1e-6
N_META = 16
HEADS = 8
HEAD_DIM = 128
MLA_ROPE = 64
MLA_QK_PAD = 256
MLA_Q_RANK = 512
MLA_KV_RANK = 256
ROPE_THETA = 10000.0
N_CHIPS = 4
LANES = 128
VMEM_LIMIT = 60 * 2 ** 20

ADAM_LR = 0.001
ADAM_B1 = 0.9
ADAM_B2 = 0.999
ADAM_EPS = 1e-08
ADAM_WD = 0.01
ADAM_STEP = 10

NN = (((1,), (0,)), ((), ()))
NT = (((1,), (1,)), ((), ()))
TN = (((0,), (0,)), ((), ()))
MESH = pl.DeviceIdType.MESH


def _tile(n, pref, align=LANES):
    if n <= pref:
        return n
    best = 0
    for t in range(align, pref + 1, align):
        if n % t == 0:
            best = t
    assert best, (n, pref)
    return best


def _params(*sem):
    return pltpu.CompilerParams(dimension_semantics=sem, vmem_limit_bytes=VMEM_LIMIT)


def _sds(shape, dtype):
    return jax.ShapeDtypeStruct(tuple(shape), dtype)


def _rowwise(name, fn, rows, consts, outs, accs, tr):
    rows = [r if isinstance(r, tuple) else (r, r.shape[1], 0) for r in rows]
    t = rows[0][0].shape[0]
    assert t % tr == 0
    n_r, n_c, n_o = len(rows), len(consts), len(outs)

    def body(*refs):
        i = pl.program_id(0)
        r = [x[...] for x in refs[:n_r]]
        c = [x[...] for x in refs[n_r:n_r + n_c]]
        o_refs = refs[n_r + n_c:n_r + n_c + n_o]
        a_refs = refs[n_r + n_c + n_o:]
        o_vals, a_vals = fn(i * tr, *r, *c)
        for ref, v in zip(o_refs, o_vals):
            ref[...] = v.astype(ref.dtype)
        if a_refs:
            @pl.when(i == 0)
            def _():
                for ref, v in zip(a_refs, a_vals):
                    ref[...] = v

            @pl.when(i > 0)
            def _():
                for ref, v in zip(a_refs, a_vals):
                    ref[...] += v

    in_specs = [pl.BlockSpec((tr, w), functools.partial(lambda cb, i: (i, cb), cb)) for _, w, cb in rows]
    in_specs += [pl.BlockSpec(a.shape, lambda i: (0, 0)) for a in consts]
    out_specs = [pl.BlockSpec((tr, w), lambda i: (i, 0)) for w, _ in outs]
    out_specs += [pl.BlockSpec((1, w), lambda i: (0, 0)) for w in accs]
    out_shape = [_sds((t, w), dt) for w, dt in outs] + [_sds((1, w), F32) for w in accs]
    res = pl.pallas_call(
        body, name=name, grid=(t // tr,), in_specs=in_specs, out_specs=out_specs, out_shape=out_shape,
        compiler_params=_params("arbitrary"),
    )(*[a for a, _, _ in rows], *consts)
    return res[:n_o], res[n_o:]


def _rms(x, w):
    r = lax.rsqrt(jnp.mean(x * x, axis=-1, keepdims=True) + EPS)
    return x * r * w


def _rms_bwd(x, w, dy):
    r = lax.rsqrt(jnp.mean(x * x, axis=-1, keepdims=True) + EPS)
    xh = x * r
    gy = dy * w
    dx = r * (gy - xh * jnp.mean(gy * xh, axis=-1, keepdims=True))
    return dx, dy * xh


def _colsum(v):
    return jnp.sum(v, axis=0, keepdims=True)


def _silu(x):
    return x * jax.nn.sigmoid(x)


def _dsilu(x):
    s = jax.nn.sigmoid(x)
    return s * (1.0 + x * (1.0 - s))


def _rope_r(x, cos, sin):
    return x * cos + pltpu.roll(x, 64, 1) * sin


def _rope_r_t(dy, cos, sin):
    return dy * cos + pltpu.roll(dy * sin, 64, 1)


def _rope_m(x, cos, sa, sb):
    return x * cos + pltpu.roll(x, 32, 1) * sa + pltpu.roll(x, 96, 1) * sb


def _rope_m_t(dy, cos, sa, sb):
    return dy * cos + pltpu.roll(dy * sa, 96, 1) + pltpu.roll(dy * sb, 32, 1)


def _rope_tables(t):
    pos = jnp.arange(t, dtype=F32)

    def cs(dim):
        inv = ROPE_THETA ** (-jnp.arange(0, dim, 2, dtype=F32) / dim)
        ang = pos[:, None] * inv[None, :]
        return jnp.cos(ang), jnp.sin(ang)

    c, s = cs(HEAD_DIM)
    cos_r = jnp.concatenate([c, c], axis=1)
    sin_r = jnp.concatenate([-s, s], axis=1)
    c, s = cs(MLA_ROPE)
    z32, z64 = jnp.zeros_like(s), jnp.zeros((t, 64), F32)
    cos_m = jnp.concatenate([c, c, z64], axis=1)
    sa = jnp.concatenate([z32, s, z64], axis=1)
    sb = jnp.concatenate([-s, z32, z64], axis=1)
    return cos_r, sin_r, cos_m, sa, sb


def _call(body, name, grid, in_specs, out_specs, out_shape, scratch, operands, host=None):
    sem = ("arbitrary",) * len(grid)
    if host is None:
        return pl.pallas_call(body, name=name, grid=grid, in_specs=in_specs, out_specs=out_specs, out_shape=out_shape,
                              scratch_shapes=scratch, compiler_params=_params(*sem))(*operands)
    n_in, n_out, n_s = len(in_specs), len(out_shape), len(scratch)
    h_in, h_out = len(host.ins), len(host.out_shape)

    def hosted(*refs):
        a = n_in
        b = a + h_in
        c = b + n_out
        d = c + h_out
        e = d + n_s
        ids = [pl.program_id(i) for i in range(len(grid))]
        first = functools.reduce(jnp.logical_and, [i == 0 for i in ids])
        last = functools.reduce(jnp.logical_and, [i == g - 1 for i, g in zip(ids, grid)])
        comm = (refs[a:b], refs[c:d], refs[e:])

        @pl.when(first)
        def _():
            host.start(*comm)

        body(*refs[:a], *refs[b:c], *refs[d:e])

        @pl.when(last)
        def _():
            host.finish(*comm)

    hbm = pl.BlockSpec(memory_space=pl.ANY)
    res = pl.pallas_call(
        hosted, name=name, grid=grid, in_specs=list(in_specs) + [hbm] * h_in, out_specs=list(out_specs) + [hbm] * h_out,
        out_shape=list(out_shape) + list(host.out_shape), scratch_shapes=list(scratch) + host.scratch(),
        compiler_params=_params(*sem))(*operands, *host.ins)
    host.result = res[n_out:]
    return res[:n_out]


def _mm(name, grid, operands, in_specs, dns, out_specs, out_shape, epilogue=None, extras=(), extra_specs=(),
        acc_shape=None, host=None):
    n_p, n_e = len(dns), len(extras)
    nk = grid[2]
    n_o = len(out_shape)
    in_place = nk > 1 and epilogue is None and n_o == 1 and out_shape[0].dtype == F32

    def body(*refs):
        ab = refs[:2 * n_p]
        ex = refs[2 * n_p:2 * n_p + n_e]
        outs = refs[2 * n_p + n_e:2 * n_p + n_e + n_o]

        part = None
        for p in range(n_p):
            d = lax.dot_general(ab[2 * p][...], ab[2 * p + 1][...], dns[p], preferred_element_type=F32)
            part = d if part is None else part + d

        def finish(acc):
            vals = (acc,) if epilogue is None else epilogue(acc, *[e[...] for e in ex])
            for o, v in zip(outs, vals):
                o[...] = v.astype(o.dtype)

        if nk == 1:
            finish(part)
        else:
            acc_ref = outs[0] if in_place else refs[2 * n_p + n_e + n_o]
            k = pl.program_id(2)

            @pl.when(k == 0)
            def _():
                acc_ref[...] = part

            @pl.when(k > 0)
            def _():
                acc_ref[...] += part

            if not in_place:
                @pl.when(k == nk - 1)
                def _():
                    finish(acc_ref[...])

    scratch = [] if nk == 1 or in_place else [pltpu.VMEM(acc_shape, F32)]
    return _call(body, name, grid, list(in_specs) + list(extra_specs), out_specs, out_shape, scratch,
                 [*operands, *extras], host)


def mm_nn(name, x, w, tm, out_dtype=F32, epilogue=None, outs=None, w2=None, n_block=None, extras=(), host=None):
    t, kdim = x.shape
    if w.ndim == 3:
        s, _, ns = w.shape
        n, tn, nb = s * ns, ns, s
        wspec = pl.BlockSpec((None, kdim, ns), lambda j, i, k: (j, 0, 0))
    else:
        n = w.shape[1]
        tn = n_block or n
        nb = n // tn
        wspec = pl.BlockSpec((kdim, tn), lambda j, i, k: (0, j))
    xspec = pl.BlockSpec((tm, kdim), lambda j, i, k: (i, 0))
    ospec = pl.BlockSpec((tm, tn), lambda j, i, k: (i, j))
    outs = outs or [out_dtype]
    grid = (nb, t // tm, 1)
    if w2 is None:
        return _mm(name, grid, [x, w], [xspec, wspec], [NN], [ospec] * len(outs), [_sds((t, n), d) for d in outs],
                   epilogue=epilogue, extras=extras, extra_specs=[ospec] * len(extras), host=host)

    def body(x_ref, w_ref, w2_ref, *o_refs):
        xv = x_ref[...]
        a = jnp.dot(xv, w_ref[...], preferred_element_type=F32)
        b = jnp.dot(xv, w2_ref[...], preferred_element_type=F32)
        for o, v in zip(o_refs, epilogue(a, b)):
            o[...] = v.astype(o.dtype)

    return _call(body, name, grid[:2],
                 [pl.BlockSpec((tm, kdim), lambda j, i: (i, 0)),
                  pl.BlockSpec((None, kdim, tn), lambda j, i: (j, 0, 0)),
                  pl.BlockSpec((None, kdim, tn), lambda j, i: (j, 0, 0))],
                 [pl.BlockSpec((tm, tn), lambda j, i: (i, j))] * len(outs), [_sds((t, n), d) for d in outs], [],
                 [x, w, w2], host)


def mm_nn_k(name, x, w, tm, tk, out_dtype=F32, host=None):
    t, kdim = x.shape
    n = w.shape[1]
    grid = (t // tm, 1, kdim // tk)
    return _mm(name, grid, [x, w],
               [pl.BlockSpec((tm, tk), lambda i, j, k: (i, k)), pl.BlockSpec((tk, n), lambda i, j, k: (k, 0))],
               [NN], [pl.BlockSpec((tm, n), lambda i, j, k: (i, 0))], [_sds((t, n), out_dtype)],
               acc_shape=(tm, n), host=host)[0]


def mm_nt(name, xs, ws, tm, tn, outs=(F32,), epilogue=None, extras=(), host=None):
    t = xs[0].shape[0]
    specs, ops = [], []
    for x, w in zip(xs, ws):
        kdim = x.shape[1]
        specs.append(pl.BlockSpec((tm, kdim), lambda j, i, k: (i, 0)))
        if w.ndim == 3:
            assert tn == w.shape[1]
            n = w.shape[0] * w.shape[1]
            specs.append(pl.BlockSpec((None, tn, kdim), lambda j, i, k: (j, 0, 0)))
        else:
            n = w.shape[0]
            specs.append(pl.BlockSpec((tn, kdim), lambda j, i, k: (j, 0)))
        ops += [x, w]
    ospec = pl.BlockSpec((tm, tn), lambda j, i, k: (i, j))
    return _mm(name, (n // tn, t // tm, 1), ops, specs, [NT] * len(xs), [ospec] * len(outs),
               [_sds((t, n), d) for d in outs], epilogue=epilogue, extras=extras,
               extra_specs=[ospec] * len(extras), host=host)


def mm_nt_k(name, xs, ws, tm, tn, out_dtype=F32, host=None):
    t = xs[0].shape[0]
    s, n, ns = ws[0].shape
    specs, ops = [], []
    for x, w in zip(xs, ws):
        specs.append(pl.BlockSpec((tm, ns), lambda i, j, k: (i, k)))
        specs.append(pl.BlockSpec((None, tn, ns), lambda i, j, k: (k, j, 0)))
        ops += [x, w]
    return _mm(name, (t // tm, n // tn, s), ops, specs, [NT] * len(xs),
               [pl.BlockSpec((tm, tn), lambda i, j, k: (i, j))], [_sds((t, n), out_dtype)], acc_shape=(tm, tn),
               host=host)[0]


def mm_tn(name, x, y, tm, tn, tk, out_dtype, shard_rows=False, shard_cols=False, host=None):
    t, m = x.shape
    n = y.shape[1]
    grid = (m // tm, n // tn, t // tk)
    if shard_cols:
        ospec = pl.BlockSpec((None, tm, tn), lambda i, j, k: (j, i, 0))
        oshape = _sds((n // tn, m, tn), out_dtype)
    elif shard_rows:
        ospec = pl.BlockSpec((None, tm, tn), lambda i, j, k: (i, 0, j))
        oshape = _sds((m // tm, tm, n), out_dtype)
    else:
        ospec = pl.BlockSpec((tm, tn), lambda i, j, k: (i, j))
        oshape = _sds((m, n), out_dtype)
    return _mm(name, grid, [x, y],
               [pl.BlockSpec((tk, tm), lambda i, j, k: (k, i)), pl.BlockSpec((tk, tn), lambda i, j, k: (k, j))],
               [TN], [ospec], [oshape], acc_shape=(tm, tn), host=host)[0]


def _decay_logs():
    return [math.log(1.0 - 2.0 ** (-5.0 - h)) for h in range(HEADS)]


def _log_decay(h):
    lg = jnp.float32(_decay_logs()[0])
    for i in range(1, HEADS):
        lg = jnp.where(h == i, jnp.float32(_decay_logs()[i]), lg)
    return lg


def _decayed_scores(q, k, lg):
    s = lax.dot_general(q, k, NT, preferred_element_type=F32)
    row = lax.broadcasted_iota(jnp.int32, s.shape, 0)
    col = lax.broadcasted_iota(jnp.int32, s.shape, 1)
    dec = jnp.where(col <= row, jnp.exp(jnp.maximum(row - col, 0).astype(F32) * lg), 0.0)
    return s * dec, dec


def _causal(s):
    row = lax.broadcasted_iota(jnp.int32, s.shape, 0)
    col = lax.broadcasted_iota(jnp.int32, s.shape, 1)
    return jnp.where(col <= row, s, -1e30)


def attn_fwd(name, q, k, v, blk, scale, host=None):
    t = q.shape[0]
    dq = q.shape[1] // HEADS
    nq = t // blk

    def body(q_ref, k_ref, v_ref, o_ref, lse_ref, m_ref, l_ref, acc_ref):
        qi = pl.program_id(1)
        qv = q_ref[...]
        m_ref[...] = jnp.full_like(m_ref, -1e30)
        l_ref[...] = jnp.zeros_like(l_ref)
        acc_ref[...] = jnp.zeros_like(acc_ref)

        def keys(start, n, diag_from):
            rows = pl.ds(pl.multiple_of(start, blk), n)
            s = lax.dot_general(qv, k_ref[rows, :], NT, preferred_element_type=F32) * scale
            if diag_from is not None:
                row = lax.broadcasted_iota(jnp.int32, s.shape, 0)
                col = lax.broadcasted_iota(jnp.int32, s.shape, 1)
                s = jnp.where(col - diag_from <= row, s, -1e30)
            m = m_ref[...]
            m_new = jnp.maximum(m, jnp.max(s, axis=-1, keepdims=True))
            p = jnp.exp(s - m_new)
            alpha = jnp.exp(m - m_new)
            m_ref[...] = m_new
            l_ref[...] = alpha * l_ref[...] + jnp.sum(p, axis=-1, keepdims=True)
            acc_ref[...] = alpha * acc_ref[...] + jnp.dot(p.astype(BF16), v_ref[rows, :], preferred_element_type=F32)

        @pl.loop(0, qi // 2)
        def _(j):
            keys(j * (2 * blk), 2 * blk, None)

        @pl.when(qi % 2 == 1)
        def _():
            keys((qi - 1) * blk, 2 * blk, blk)

        @pl.when(qi % 2 == 0)
        def _():
            keys(qi * blk, blk, 0)

        l = l_ref[...]
        o_ref[...] = acc_ref[...] / l
        lse_ref[...] = jnp.broadcast_to(m_ref[...] + jnp.log(l), (blk, HEAD_DIM))

    hspec = pl.BlockSpec((blk, HEAD_DIM), lambda h, i: (i, h))
    return _call(body, name, (HEADS, nq),
                 [pl.BlockSpec((blk, dq), lambda h, i: (i, h)),
                  pl.BlockSpec((t, dq), lambda h, i: (0, h)),
                  pl.BlockSpec((t, HEAD_DIM), lambda h, i: (0, h))],
                 [hspec, hspec], [_sds((t, HEADS * HEAD_DIM), F32)] * 2,
                 [pltpu.VMEM((blk, 1), F32), pltpu.VMEM((blk, 1), F32), pltpu.VMEM((blk, HEAD_DIM), F32)],
                 [q, k, v], host)


def attn_bwd(name, q, k, v, do, o, lse, blk, scale, host=None):
    t = q.shape[0]
    dq_w = q.shape[1] // HEADS
    nb = t // blk

    def body(q_ref, k_ref, v_ref, do_ref, o_ref, lse_ref, dq_ref, dk_ref, dv_ref):
        ki = pl.program_id(1)
        kv = k_ref[...]
        vv = v_ref[...]

        @pl.when(ki == 0)
        def _():
            dq_ref[...] = jnp.zeros_like(dq_ref)

        def queries(start, n, diag):
            rows = pl.ds(pl.multiple_of(start, blk), n)
            qv, dov = q_ref[rows, :], do_ref[rows, :]
            s = lax.dot_general(qv, kv, NT, preferred_element_type=F32) * scale
            if diag:
                s = _causal(s)
            p = jnp.exp(s - lse_ref[rows, :][:, :1])
            dp = lax.dot_general(dov, vv, NT, preferred_element_type=F32)
            delta = jnp.sum(dov.astype(F32) * o_ref[rows, :], axis=-1, keepdims=True)
            ds = p * (dp - delta) * scale
            pb, dsb = p.astype(BF16), ds.astype(BF16)
            dv_ref[...] += lax.dot_general(pb, dov, TN, preferred_element_type=F32)
            dk_ref[...] += lax.dot_general(dsb, qv, TN, preferred_element_type=F32)
            dq_ref[rows, :] += jnp.dot(dsb, kv, preferred_element_type=F32)

        dk_ref[...] = jnp.zeros_like(dk_ref)
        dv_ref[...] = jnp.zeros_like(dv_ref)
        queries(ki * blk, blk, True)
        later = nb - 1 - ki

        @pl.when(later % 2 == 1)
        def _():
            queries((ki + 1) * blk, blk, False)

        @pl.loop(0, later // 2)
        def _(j):
            queries((ki + 1 + later % 2 + 2 * j) * blk, 2 * blk, False)

    full = lambda w: pl.BlockSpec((t, w), lambda h, j: (0, h))
    blkd = lambda w: pl.BlockSpec((blk, w), lambda h, j: (j, h))
    return _call(body, name, (HEADS, nb),
                 [full(dq_w), blkd(dq_w), blkd(HEAD_DIM), full(HEAD_DIM), full(HEAD_DIM), full(HEAD_DIM)],
                 [full(dq_w), blkd(dq_w), blkd(HEAD_DIM)],
                 [_sds(q.shape, F32), _sds(k.shape, F32), _sds(v.shape, F32)], [], [q, k, v, do, o, lse], host)


def _chunk_decays(lg, blk):
    row = lax.broadcasted_iota(jnp.int32, (blk, HEAD_DIM), 0).astype(F32)
    return jnp.exp(lg * (row + 1.0)), jnp.exp(lg * (blk - 1.0 - row)), jnp.exp(lg * blk * jnp.ones((1, HEAD_DIM), F32))


def ret_fwd(name, q, k, v, blk, host=None):
    t = q.shape[0]
    nb = t // blk

    def body(q_ref, k_ref, v_ref, o_ref, st_ref, state):
        h, i = pl.program_id(0), pl.program_id(1)
        lg = _log_decay(h)

        @pl.when(i == 0)
        def _():
            state[...] = jnp.zeros_like(state)

        qv, kv, vv = q_ref[...], k_ref[...], v_ref[...]
        before = state[...]
        st_ref[...] = before
        p, _ = _decayed_scores(qv, kv, lg)
        xi, zeta, g_blk = _chunk_decays(lg, blk)
        o_ref[...] = (jnp.dot(p.astype(BF16), vv, preferred_element_type=F32)
                      + jnp.dot(qv, before.astype(BF16), preferred_element_type=F32) * xi)
        kz = (kv.astype(F32) * zeta).astype(BF16)
        state[...] = before * g_blk + lax.dot_general(kz, vv, TN, preferred_element_type=F32)

    hspec = pl.BlockSpec((blk, HEAD_DIM), lambda h, i: (i, h))
    return _call(body, name, (HEADS, nb), [hspec] * 3,
                 [hspec, pl.BlockSpec((HEAD_DIM, HEAD_DIM), lambda h, i: (i, h))],
                 [_sds((t, HEADS * HEAD_DIM), F32), _sds((nb * HEAD_DIM, HEADS * HEAD_DIM), F32)],
                 [pltpu.VMEM((HEAD_DIM, HEAD_DIM), F32)], [q, k, v], host)


def ret_bwd(name, q, k, v, do, states, blk, host=None):
    t = q.shape[0]
    nb = t // blk

    def body(q_ref, k_ref, v_ref, do_ref, st_ref, dq_ref, dk_ref, dv_ref, dstate):
        h, i = pl.program_id(0), pl.program_id(1)
        lg = _log_decay(h)

        @pl.when(i == 0)
        def _():
            dstate[...] = jnp.zeros_like(dstate)

        qv, kv, vv, dov = q_ref[...], k_ref[...], v_ref[...], do_ref[...]
        before = st_ref[...].astype(BF16)
        after_grad = dstate[...]
        p, dec = _decayed_scores(qv, kv, lg)
        ds = lax.dot_general(dov, vv, NT, preferred_element_type=F32) * dec
        pb, dsb = p.astype(BF16), ds.astype(BF16)
        xi, zeta, g_blk = _chunk_decays(lg, blk)
        dox = (dov.astype(F32) * xi).astype(BF16)
        kz = (kv.astype(F32) * zeta).astype(BF16)
        agb = after_grad.astype(BF16)
        dv_ref[...] = (lax.dot_general(pb, dov, TN, preferred_element_type=F32)
                       + jnp.dot(kz, agb, preferred_element_type=F32))
        dq_ref[...] = (jnp.dot(dsb, kv, preferred_element_type=F32)
                       + lax.dot_general(dox, before, NT, preferred_element_type=F32))
        dk_ref[...] = (lax.dot_general(dsb, qv, TN, preferred_element_type=F32)
                       + lax.dot_general(vv, agb, NT, preferred_element_type=F32) * zeta)
        dstate[...] = after_grad * g_blk + lax.dot_general(qv, dox, TN, preferred_element_type=F32)

    hspec = pl.BlockSpec((blk, HEAD_DIM), lambda h, i: (nb - 1 - i, h))
    return _call(body, name, (HEADS, nb),
                 [hspec] * 4 + [pl.BlockSpec((HEAD_DIM, HEAD_DIM), lambda h, i: (nb - 1 - i, h))],
                 [hspec] * 3, [_sds(q.shape, F32)] * 3, [pltpu.VMEM((HEAD_DIM, HEAD_DIM), F32)],
                 [q, k, v, do, states], host)


CHIP_FLIPS = ((1, 0), (0, 1), (1, 1))


def _position():
    return lax.axis_index("x"), lax.axis_index("y"), lax.axis_index("c")


class _Comm:
    def __init__(self, ins, out_shape, plan, n_remote, n_local):
        self.ins, self.out_shape, self.plan = list(ins), list(out_shape), plan
        self.n_remote, self.n_local = n_remote, n_local
        self.result = None

    def scratch(self):
        return [pltpu.SemaphoreType.DMA((self.n_remote,)), pltpu.SemaphoreType.DMA((self.n_remote,)),
                pltpu.SemaphoreType.DMA((self.n_local,))]

    def _copies(self, in_refs, out_refs, sems):
        send_sems, recv_sems, local_sems = sems
        pos = _position()
        p = self.plan(pos, in_refs, out_refs)

        def remote(k, src, dst, dev):
            return pltpu.make_async_remote_copy(src_ref=src, dst_ref=dst, send_sem=send_sems.at[k],
                                                recv_sem=recv_sems.at[k], device_id=dev, device_id_type=MESH)

        local = [pltpu.make_async_copy(s, d, local_sems.at[i]) for i, (s, d) in enumerate(p["local"])]
        out = [remote(k, s, d, dev) for k, (s, d, dev) in enumerate(p["sends"])]
        arrivals = [functools.partial(remote, k, d, d, pos) for k, d in enumerate(p["recvs"])]
        return local, out, arrivals

    def start(self, in_refs, out_refs, sems):
        local, out, _ = self._copies(in_refs, out_refs, sems)
        for cp in local + out:
            cp.start()

    def finish(self, in_refs, out_refs, sems):
        local, out, arrivals = self._copies(in_refs, out_refs, sems)
        for make in arrivals:
            make().wait_recv()
        for cp in out:
            cp.wait_send()
        for cp in local:
            cp.wait()

    def run(self, name):
        n_in, n_out = len(self.ins), len(self.out_shape)

        def body(*refs):
            comm = (refs[:n_in], refs[n_in:n_in + n_out], refs[n_in + n_out:])
            self.start(*comm)
            self.finish(*comm)

        hbm = pl.BlockSpec(memory_space=pl.ANY)
        self.result = pl.pallas_call(body, name=name, in_specs=[hbm] * n_in, out_specs=[hbm] * n_out,
                                     out_shape=self.out_shape, scratch_shapes=self.scratch())(*self.ins)
        return self.result


def _half_rows(c, rows):
    r2 = rows // 2
    return pl.ds(pl.multiple_of(c * r2, math.gcd(r2, LANES)), r2)


def _half(ref, c, rows, lead=()):
    return ref.at[(*lead, _half_rows(c, rows))]


def gather_halves(shards):
    def plan(pos, ins, outs):
        x, y, c = pos
        me = 2 * x + y
        p = dict(local=[], sends=[], recvs=[])
        for a, (src, dst) in enumerate(zip(ins, outs)):
            rows = shards[a].shape[0]
            p["local"].append((src, dst.at[me]))
            for fx, fy in CHIP_FLIPS:
                px, py = x ^ fx, y ^ fy
                p["sends"].append((_half(src, c, rows), _half(dst, c, rows, (me,)), (px, py, c)))
                p["recvs"].append(_half(dst, c, rows, (2 * px + py,)))
        return p

    return _Comm(shards, [_sds((N_CHIPS, *s.shape), s.dtype) for s in shards], plan,
                 n_remote=3 * len(shards), n_local=len(shards))


def chip_exchange(parts):
    def plan(pos, ins, outs):
        x, y, c = pos
        me = 2 * x + y
        p = dict(local=[], sends=[], recvs=[])
        for src, dst in zip(ins, outs):
            p["local"].append((src.at[me], dst.at[me]))
            for fx, fy in CHIP_FLIPS:
                px, py = x ^ fx, y ^ fy
                peer = 2 * px + py
                p["sends"].append((src.at[peer], dst.at[me], (px, py, c)))
                p["recvs"].append(dst.at[peer])
        return p

    return _Comm(parts, [_sds(g.shape, g.dtype) for g in parts], plan, n_remote=3 * len(parts), n_local=len(parts))


def all_gather_devices(v):
    flips = [(fx, fy, fc) for fx in (0, 1) for fy in (0, 1) for fc in (0, 1)][1:]

    def plan(pos, ins, outs):
        x, y, c = pos
        me = 4 * x + 2 * y + c
        p = dict(local=[(ins[0], outs[0].at[me])], sends=[], recvs=[])
        for fx, fy, fc in flips:
            px, py, pc = x ^ fx, y ^ fy, c ^ fc
            p["sends"].append((ins[0], outs[0].at[me], (px, py, pc)))
            p["recvs"].append(outs[0].at[4 * px + 2 * py + pc])
        return p

    return _Comm([v], [_sds((8, *v.shape), v.dtype)], plan, n_remote=7, n_local=1).run("small_all_gather")[0]


SWAP_CHUNK_BYTES = 3 * 2 ** 19


def _sibling_stream(t, n, value, consume, sbuf, rbuf, send_sems, recv_sems, credits):
    x, y, c = _position()
    sib = (x, y, 1 - c)

    def copy(slot):
        return pltpu.make_async_remote_copy(src_ref=sbuf.at[slot], dst_ref=rbuf.at[slot], send_sem=send_sems.at[slot],
                                            recv_sem=recv_sems.at[slot], device_id=sib, device_id_type=MESH)

    slot = t % 2

    @pl.when(jnp.logical_and(t >= 2, t < n))
    def _():
        copy(slot).wait_send()
        pl.semaphore_wait(credits.at[slot], 1)

    @pl.when(t < n)
    def _():
        sbuf[slot] = value
        copy(slot).start()

    @pl.when(t >= 1)
    def _():
        prev = 1 - slot
        copy(prev).wait_recv()
        consume(sbuf[prev], rbuf[prev])

        @pl.when(t + 1 < n)
        def _():
            pl.semaphore_signal(credits.at[prev], inc=1, device_id=sib, device_id_type=MESH)

    @pl.when(t == n)
    def _():
        copy(1 - slot).wait_send()
        if n > 1:
            copy(slot).wait_send()


def _swap_scratch(rows, cols, dtype):
    return [pltpu.VMEM((2, rows, cols), dtype), pltpu.VMEM((2, rows, cols), dtype),
            pltpu.SemaphoreType.DMA((2,)), pltpu.SemaphoreType.DMA((2,)), pltpu.SemaphoreType.REGULAR((2,))]


def _chunk_rows(rows, cols, dtype, nbytes=SWAP_CHUNK_BYTES):
    return _tile(rows, max(16, nbytes // (cols * jnp.dtype(dtype).itemsize)), 16)


def pair_add(name, g):
    s, r, c_ = g.shape
    r2 = r // 2
    cr = _chunk_rows(r2, c_, g.dtype)
    nj = r2 // cr

    n = s * nj

    def body(core, mine_ref, theirs_ref, o_ref, *scratch):
        def consume(_, got):
            o_ref[...] = (mine_ref[...].astype(F32) + got.astype(F32)).astype(o_ref.dtype)

        _sibling_stream(pl.program_id(0), n, theirs_ref[...], consume, *scratch)

    sent = lambda t: jnp.minimum(t, n - 1)
    used = lambda t: jnp.maximum(t - 1, 0)
    grid_spec = pltpu.PrefetchScalarGridSpec(
        num_scalar_prefetch=1, grid=(n + 1,),
        in_specs=[pl.BlockSpec((None, cr, c_), lambda t, core: (used(t) // nj, core[0] * nj + used(t) % nj, 0)),
                  pl.BlockSpec((None, cr, c_), lambda t, core: (sent(t) // nj, (1 - core[0]) * nj + sent(t) % nj, 0))],
        out_specs=pl.BlockSpec((None, cr, c_), lambda t, core: (used(t) // nj, used(t) % nj, 0)),
        scratch_shapes=_swap_scratch(cr, c_, g.dtype))
    core = lax.axis_index("c").astype(jnp.int32).reshape(1)
    return pl.pallas_call(body, name=name, grid_spec=grid_spec, out_shape=_sds((s, r2, c_), g.dtype),
                          compiler_params=_params("arbitrary"))(core, g, g)


def _adamw_math(w, g, m, v):
    m = ADAM_B1 * m + (1.0 - ADAM_B1) * g
    v = ADAM_B2 * v + (1.0 - ADAM_B2) * (g * g)
    m_hat = m / (1.0 - ADAM_B1 ** ADAM_STEP)
    v_hat = v / (1.0 - ADAM_B2 ** ADAM_STEP)
    return -ADAM_LR * (m_hat / (jnp.sqrt(v_hat) + ADAM_EPS) + ADAM_WD * w), m, v


def reduce_join(name, p):
    s, r2, c_ = p.shape
    cr = _chunk_rows(r2, c_, F32)
    nj = r2 // cr

    def body(core, p_ref, o_ref, *scratch):
        acc = p_ref[0].astype(F32)
        for i in range(1, s):
            acc = acc + p_ref[i].astype(F32)

        def consume(own, got):
            c = core[0]
            o_ref[c] = own
            o_ref[1 - c] = got

        _sibling_stream(pl.program_id(0), nj, acc, consume, *scratch)

    grid_spec = pltpu.PrefetchScalarGridSpec(
        num_scalar_prefetch=1, grid=(nj + 1,),
        in_specs=[pl.BlockSpec((s, cr, c_), lambda t, core: (0, jnp.minimum(t, nj - 1), 0))],
        out_specs=pl.BlockSpec((2, cr, c_), lambda t, core: (0, jnp.maximum(t - 1, 0), 0)),
        scratch_shapes=_swap_scratch(cr, c_, F32))
    core = lax.axis_index("c").astype(jnp.int32).reshape(1)
    out = pl.pallas_call(body, name=name, grid_spec=grid_spec, out_shape=_sds((2, r2, c_), F32),
                         compiler_params=_params("arbitrary"))(core, p)
    return out.reshape(2 * r2, c_)


def sibling_fill(name, buf):
    s, r, c_ = buf.shape
    r2 = r // 2
    cr = _chunk_rows(r2, c_, buf.dtype)
    nj = r2 // cr
    n_peers = len(CHIP_FLIPS)

    n = n_peers * nj

    def body(where, in_ref, o_ref, *scratch):
        def consume(_, got):
            o_ref[...] = got

        _sibling_stream(pl.program_id(0), n, in_ref[...], consume, *scratch)

    sent = lambda t: jnp.minimum(t, n - 1)
    used = lambda t: jnp.maximum(t - 1, 0)
    grid_spec = pltpu.PrefetchScalarGridSpec(
        num_scalar_prefetch=1, grid=(n + 1,),
        in_specs=[pl.BlockSpec((None, cr, c_),
                               lambda t, where: (where[sent(t) // nj], where[n_peers] * nj + sent(t) % nj, 0))],
        out_specs=pl.BlockSpec((None, cr, c_),
                               lambda t, where: (where[used(t) // nj], (1 - where[n_peers]) * nj + used(t) % nj, 0)),
        scratch_shapes=_swap_scratch(cr, c_, buf.dtype))
    x, y, c = _position()
    where = jnp.stack([2 * (x ^ fx) + (y ^ fy) for fx, fy in CHIP_FLIPS] + [c]).astype(jnp.int32)
    return pl.pallas_call(body, name=name, grid_spec=grid_spec, out_shape=_sds(buf.shape, buf.dtype),
                          input_output_aliases={1: 0}, compiler_params=_params("arbitrary"))(where, buf)


def sum_slots(name, p, out_dtype):
    s, r, c = p.shape
    tr = _tile(r, 256, 16)

    def body(p_ref, o_ref):
        acc = p_ref[0].astype(F32)
        for i in range(1, s):
            acc = acc + p_ref[i].astype(F32)
        o_ref[...] = acc.astype(o_ref.dtype)

    return pl.pallas_call(
        body, name=name, grid=(r // tr,), in_specs=[pl.BlockSpec((s, tr, c), lambda i: (0, i, 0))],
        out_specs=pl.BlockSpec((tr, c), lambda i: (i, 0)), out_shape=_sds((r, c), out_dtype),
        compiler_params=_params("arbitrary"),
    )(p)


def adamw(name, w, g, m, v):
    r, c = w.shape
    outs, _ = _rowwise(name, lambda _, w, g, m, v: ([*_adamw_math(w, g, m, v), g], []), [w, g, m, v], [],
                       [(c, F32)] * 4, [], _tile(r, 256, 8))
    return outs


RET_SCALE = HEAD_DIM ** -0.5
MLA_SCALE = (HEAD_DIM + MLA_ROPE) ** -0.5
GRAD_DT = BF16
IN_RET = 4 * HEADS * HEAD_DIM
IN_MLA = MLA_Q_RANK + MLA_KV_RANK + MLA_ROPE
IN_MLA_PAD = IN_MLA + 64


def _heads(fn):
    return jnp.concatenate([fn(h) for h in range(HEADS)], axis=1)


def _head(a, h, stride=HEAD_DIM, off=0):
    return a[:, h * stride + off:h * stride + off + HEAD_DIM]


def _group_norm(o):
    rs = [lax.rsqrt(jnp.mean(_head(o, h) * _head(o, h), axis=-1, keepdims=True) + EPS) for h in range(HEADS)]
    return _heads(lambda h: _head(o, h) * rs[h]), rs


class _Alone:
    def host(self, kernel_name):
        return None

    def done(self, kernel_name, w):
        pass

    def grads(self, g):
        pass


def _ffn_fwd(tag, n, w, k, tm, tmk, plan):
    gate, up, down = tag + "_gate", tag + "_up", tag + "_down"
    if "wu" + k in w:
        g, u, a = mm_nn(up, n, w["wg" + k], tm, outs=[BF16] * 3, w2=w["wu" + k],
                        epilogue=lambda g, u: (g, u, _silu(g) * u), host=plan.host(up))
    else:
        (g,) = mm_nn(gate, n, w["wg" + k], tm, out_dtype=BF16, host=plan.host(gate))
        plan.done(gate, w)
        u, a = mm_nn(up, n, w["wu" + k], tm, outs=[BF16] * 2, extras=(g,),
                     epilogue=lambda u, g: (u, _silu(g.astype(F32)) * u), host=plan.host(up))
    plan.done(up, w)
    ff = a.shape[1]
    wd = w["wd" + k]
    f = mm_nn_k(down, a, wd.reshape(ff, wd.shape[2]), tmk, _tile(ff, 1408), host=plan.host(down))
    plan.done(down, w)
    return g, u, a, f


def _ffn_bwd(tag, df, n, g, u, a, wg, wu, wd, tm, tmk, tk, dt, plan):
    ns = wg.shape[2]

    def gate_grads(da, g, u):
        g, u = g.astype(F32), u.astype(F32)
        return da * u * _dsilu(g), da * _silu(g)

    def hosted(kernel_name, call):
        out = call(plan.host(kernel_name))
        plan.done(kernel_name, None)
        return out

    k = tag[-1]
    dg, du = hosted(tag + "_da", lambda h: mm_nt(tag + "_da", [df], [wd], tm, ns, outs=(BF16, BF16),
                                                 epilogue=gate_grads, extras=(g, u), host=h))
    dwg = hosted(tag + "_dwg", lambda h: mm_tn(tag + "_dwg", n, dg, dt, ns, tk, GRAD_DT, shard_cols=True, host=h))
    plan.grads({"wg" + k: dwg})
    dwu = hosted(tag + "_dwu", lambda h: mm_tn(tag + "_dwu", n, du, dt, ns, tk, GRAD_DT, shard_cols=True, host=h))
    plan.grads({"wu" + k: dwu})
    dwd = hosted(tag + "_dwd", lambda h: mm_tn(tag + "_dwd", a, df, ns, dt, tk, GRAD_DT, shard_rows=True, host=h))
    plan.grads({"wd" + k: dwd})
    dn = hosted(tag + "_dn", lambda h: mm_nt_k(tag + "_dn", [dg, du], [wg, wu], tmk, dt, out_dtype=BF16, host=h))
    return dn


def _local_step(x, tgt, meta, w, nw, plan):
    seq, d = x.shape
    t_real = N_META + seq
    tp = -(-t_real // LANES) * LANES
    zpad = jnp.zeros((tp - t_real, d), F32)
    h0 = jnp.concatenate([meta, x, zpad], axis=0)
    tgt_p = jnp.concatenate([jnp.zeros((N_META, d), F32), tgt, zpad], axis=0)
    cos_r, sin_r, cos_m, sa, sb = _rope_tables(tp)
    tm = _tile(tp, 512)
    tmt = _tile(tp, 768, 16)
    tmk = _tile(tp, 1408)
    tk = tp
    blk = tm
    dt = _tile(d, 1024)
    hw = HEADS * HEAD_DIM
    qw = HEADS * MLA_QK_PAD

    (n1,), _ = _rowwise("ffn1_norm", lambda r0, h, g: ([_rms(h, g)], []), [h0], [nw["ffn1_pre_norm"]],
                        [(d, BF16)], [], tm)
    g1, u1, a1, f1 = _ffn_fwd("ffn1", n1, w, "1", tm, tmk, plan)

    def post_ffn1(r0, h, f, post, pre):
        h1 = h + 0.5 * _rms(f, post)
        return [h1, _rms(h1, pre)], []

    (h1, un), _ = _rowwise("mix_norm", post_ffn1, [h0, f1], [nw["ffn1_post_norm"], nw["mix_pre_norm"]],
                           [(d, F32), (d, BF16)], [], tm)
    (proj_r,) = mm_nn("proj_r", un, w["w_r"], tm, n_block=_tile(IN_RET, 1024), host=plan.host("proj_r"))
    plan.done("proj_r", w)
    (proj_c,) = mm_nn("proj_c", un, w["w_c"], tm)

    def split_proj(r0, pr, pc, cr, sr, cm, ta, tb, qn, kvn):
        rq = _heads(lambda h: _rope_r(_head(pr, h), cr, sr))
        rk = _heads(lambda h: _rope_r(_head(pr, h, off=hw), cr, sr) * RET_SCALE)
        rv = pr[:, 2 * hw:3 * hw]
        cqn = _rms(pc[:, :MLA_Q_RANK], qn)
        ckvn = _rms(pc[:, MLA_Q_RANK:MLA_Q_RANK + MLA_KV_RANK], kvn)
        krr = _rope_m(pc[:, MLA_Q_RANK + MLA_KV_RANK:], cm, ta, tb)
        return [rq, rk, rv, cqn, ckvn, krr], []

    (rq, rk, rv, cqn, ckvn, krr), _ = _rowwise(
        "split_proj", split_proj, [proj_r, proj_c, cos_r, sin_r, cos_m, sa, sb],
        [nw["mla_q_norm"], nw["mla_kv_norm"]],
        [(hw, BF16), (hw, BF16), (hw, BF16), (MLA_Q_RANK, BF16), (MLA_KV_RANK, BF16), (LANES, F32)], [], tm)
    (qp,) = mm_nn("q_up", cqn, w["wuq"], tm)
    (kn,) = mm_nn("k_up", ckvn, w["wuk"], tm)
    (vv,) = mm_nn("v_up", ckvn, w["wuv"], tm, out_dtype=BF16)

    def build_qk(r0, qp, kn, krr, cm, ta, tb):
        qc = jnp.concatenate(
            [part for h in range(HEADS)
             for part in (_head(qp, h, MLA_QK_PAD), _rope_m(_head(qp, h, MLA_QK_PAD, HEAD_DIM), cm, ta, tb))], axis=1)
        kc = jnp.concatenate([part for h in range(HEADS) for part in (_head(kn, h), krr)], axis=1)
        return [qc, kc], []

    (qc, kc), _ = _rowwise("build_qk", build_qk, [qp, kn, krr, cos_m, sa, sb], [], [(qw, BF16), (qw, BF16)], [], tm)
    o_m, lse = attn_fwd("mla_fwd", qc, kc, vv, blk, MLA_SCALE, host=plan.host("mla_fwd"))
    plan.done("mla_fwd", w)
    o_r, ret_states = ret_fwd("ret_fwd", rq, rk, rv, blk, host=plan.host("ret_fwd"))
    plan.done("ret_fwd", w)

    def gate_mix(r0, rg, o_r, o_m, gn):
        y, _ = _group_norm(o_r)
        return [jnp.concatenate([_silu(rg) * (y * gn), o_m], axis=1)], []

    (mixcat,), _ = _rowwise("gate_mix", gate_mix, [(proj_r, hw, 3), o_r, o_m], [nw["ret_group_norm"]],
                            [(2 * hw, BF16)], [], tm)
    (mix,) = mm_nn("mix_out", mixcat, w["w_out"], tm, n_block=dt)

    def post_mix(r0, h, m, post, pre):
        h2 = h + _rms(m, post)
        return [h2, _rms(h2, pre)], []

    (h2, n3), _ = _rowwise("ffn2_norm", post_mix, [h1, mix], [nw["mix_post_norm"], nw["ffn2_pre_norm"]],
                           [(d, F32), (d, BF16)], [], tm)
    g2, u2, a2, f2 = _ffn_fwd("ffn2", n3, w, "2", tm, tmk, plan)

    def loss_head(r0, h, f, t, post):
        h3 = h + 0.5 * _rms(f, post)
        row = r0 + lax.broadcasted_iota(jnp.int32, h3.shape, 0)
        err = jnp.where(row >= N_META, jnp.where(row < t_real, h3 - t, 0.0), 0.0)
        dh3 = err / d
        df, dpost = _rms_bwd(f, post, 0.5 * dh3)
        return [dh3, df], [_colsum(err * err), _colsum(dpost)]

    (dh3, df2), (loss_vec, d_post2) = _rowwise("loss_head", loss_head, [h2, f2, tgt_p], [nw["ffn2_post_norm"]],
                                               [(d, F32), (d, BF16)], [d, d], tm)
    loss = 0.5 * jnp.sum(loss_vec) / d
    dn3 = _ffn_bwd("ffn2", df2, n3, g2, u2, a2, w["wg2"], w["wu2"], w["wd2"], tmt, tmk, tk, dt, plan)

    def back_mix_norm(r0, h, m, dh3, dn, pre, post):
        dx, dpre = _rms_bwd(h, pre, dn)
        dh2 = dh3 + dx
        dm, dpost = _rms_bwd(m, post, dh2)
        return [dh2, dm], [_colsum(dpre), _colsum(dpost)]

    (dh2, dmix), (d_pre2, d_mix_post) = _rowwise(
        "back_mix_norm", back_mix_norm, [h2, mix, dh3, dn3], [nw["ffn2_pre_norm"], nw["mix_post_norm"]],
        [(d, F32), (d, BF16)], [d, d], tm)
    (dmixcat,) = mm_nt("mix_dx", [dmix], [w["w_out"]], tmt, _tile(2 * hw, 512), outs=(BF16,))
    plan.grads(dict(w_out=mm_tn("mix_dw", mixcat, dmix, 2 * hw // N_CHIPS, dt, tk, GRAD_DT, shard_rows=True)))

    def back_gate(r0, dmc, rg, o_r, gn):
        d_ret, d_om = dmc[:, :hw], dmc[:, hw:]
        yh, rs = _group_norm(o_r)
        d_rg = d_ret * (yh * gn) * _dsilu(rg)
        dy = d_ret * _silu(rg)
        gyh = dy * gn
        d_or = _heads(lambda h: rs[h] * (_head(gyh, h) - _head(yh, h) * jnp.mean(_head(gyh, h) * _head(yh, h),
                                                                                    axis=-1, keepdims=True)))
        return [d_or, d_rg, d_om], [_colsum(dy * yh)]

    (d_or, d_rg, d_om), (d_gn,) = _rowwise("back_gate", back_gate, [dmixcat, (proj_r, hw, 3), o_r],
                                           [nw["ret_group_norm"]], [(hw, BF16), (hw, F32), (hw, BF16)], [hw], tm)
    dqc, dkc, dvv = attn_bwd("mla_bwd", qc, kc, vv, d_om, o_m, lse, blk, MLA_SCALE, host=plan.host("mla_bwd"))
    plan.done("mla_bwd", None)
    drq, drk, drv = ret_bwd("ret_bwd", rq, rk, rv, d_or, ret_states, blk)

    def back_qk(r0, dqc, dkc, dvv, cm, ta, tb):
        dqp = jnp.concatenate(
            [part for h in range(HEADS)
             for part in (_head(dqc, h, MLA_QK_PAD), _rope_m_t(_head(dqc, h, MLA_QK_PAD, HEAD_DIM), cm, ta, tb))],
            axis=1)
        dkn = _heads(lambda h: _head(dkc, h, MLA_QK_PAD))
        dkr = _head(dkc, 0, MLA_QK_PAD, HEAD_DIM)
        for h in range(1, HEADS):
            dkr = dkr + _head(dkc, h, MLA_QK_PAD, HEAD_DIM)
        return [dqp, dkn, _rope_m_t(dkr, cm, ta, tb), dvv], []

    (dqp, dkn, dkr, dvb), _ = _rowwise("back_qk", back_qk, [dqc, dkc, dvv, cos_m, sa, sb], [],
                                       [(qw, BF16), (hw, BF16), (LANES, F32), (hw, BF16)], [], tm)
    (dcqn,) = mm_nt("q_dx", [dqp], [w["wuq"]], tm, MLA_Q_RANK)
    dwuq = mm_tn("q_dw", cqn, dqp, MLA_Q_RANK, _tile(qw, 1024), tk, GRAD_DT)
    (dckvn,) = mm_nt("kv_dx", [dkn, dvb], [w["wuk"], w["wuv"]], tm, MLA_KV_RANK)
    dwuk = mm_tn("k_dw", ckvn, dkn, MLA_KV_RANK, hw, tk, GRAD_DT)
    dwuv = mm_tn("v_dw", ckvn, dvb, MLA_KV_RANK, hw, tk, GRAD_DT)

    def back_proj(r0, drq, drk, drv, d_rg, pc, dcqn, dckvn, dkr, cr, sr, qn, kvn):
        d_q = _heads(lambda h: _rope_r_t(_head(drq, h), cr, sr))
        d_k = _heads(lambda h: _rope_r_t(_head(drk, h), cr, sr) * RET_SCALE)
        dcq, a_q = _rms_bwd(pc[:, :MLA_Q_RANK], qn, dcqn)
        dckv, a_kv = _rms_bwd(pc[:, MLA_Q_RANK:MLA_Q_RANK + MLA_KV_RANK], kvn, dckvn)
        return ([jnp.concatenate([d_q, d_k, drv, d_rg], axis=1), jnp.concatenate([dcq, dckv, dkr], axis=1)],
                [_colsum(a_q), _colsum(a_kv)])

    (dproj_r, dproj_c), (d_qn, d_kvn) = _rowwise(
        "back_proj", back_proj, [drq, drk, drv, d_rg, proj_c, dcqn, dckvn, dkr, cos_r, sin_r],
        [nw["mla_q_norm"], nw["mla_kv_norm"]], [(IN_RET, BF16), (IN_MLA_PAD, BF16)],
        [MLA_Q_RANK, MLA_KV_RANK], tm)
    (dun,) = mm_nt("proj_dx", [dproj_r, dproj_c], [w["w_r"], w["w_c"]], tmt, _tile(d, 512), outs=(BF16,))
    dw_r = mm_tn("proj_dw_r", un, dproj_r, dt, _tile(IN_RET, 1024), tk, GRAD_DT)
    dw_c = mm_tn("proj_dw_c", un, dproj_c, dt, IN_MLA_PAD, tk, GRAD_DT)
    plan.grads(dict(w_r=dw_r, w_c=dw_c, wuq=dwuq, wuk=dwuk, wuv=dwuv))

    def back_ffn1_norm(r0, h, f, dh2, dn, pre, post):
        dx, dpre = _rms_bwd(h, pre, dn)
        dh1 = dh2 + dx
        df, dpost = _rms_bwd(f, post, 0.5 * dh1)
        return [dh1, df], [_colsum(dpre), _colsum(dpost)]

    (dh1, df1), (d_mix_pre, d_post1) = _rowwise(
        "back_ffn1_norm", back_ffn1_norm, [h1, f1, dh2, dun], [nw["mix_pre_norm"], nw["ffn1_post_norm"]],
        [(d, F32), (d, BF16)], [d, d], tm)
    dn1 = _ffn_bwd("ffn1", df1, n1, g1, u1, a1, w["wg1"], w["wu1"], w["wd1"], tmt, tmk, tk, dt, plan)

    def back_input(r0, h, dh1, dn, pre):
        dx, dpre = _rms_bwd(h, pre, dn)
        return [dh1 + dx], [_colsum(dpre)]

    (dh0,), (d_pre1,) = _rowwise("back_input", back_input, [h0, dh1, dn1], [nw["ffn1_pre_norm"]], [(d, F32)], [d], tm)

    small = dict(ffn1_pre_norm=d_pre1, ffn1_post_norm=d_post1, mix_pre_norm=d_mix_pre, ret_group_norm=d_gn,
                 mla_q_norm=d_qn, mla_kv_norm=d_kvn, mix_post_norm=d_mix_post, ffn2_pre_norm=d_pre2,
                 ffn2_post_norm=d_post2)
    return loss, dh0[N_META:t_real], small, dh0[:N_META]


WEIGHTS = ("meta_tokens", "ffn1_pre_norm", "ffn1_w_gate", "ffn1_w_up", "ffn1_w_down", "ffn1_post_norm",
           "mix_pre_norm", "w_in", "ret_group_norm", "mla_q_norm", "mla_w_uq", "mla_kv_norm", "mla_w_uk",
           "mla_w_uv", "w_out", "mix_post_norm", "ffn2_pre_norm", "ffn2_w_gate", "ffn2_w_up", "ffn2_w_down",
           "ffn2_post_norm")
BIG = ("ffn1_w_gate", "ffn1_w_up", "ffn1_w_down", "w_in", "mla_w_uq", "mla_w_uk", "mla_w_uv", "w_out",
       "ffn2_w_gate", "ffn2_w_up", "ffn2_w_down")
NORMS = ("ffn1_pre_norm", "ffn1_post_norm", "mix_pre_norm", "ret_group_norm", "mla_q_norm", "mla_kv_norm",
         "mix_post_norm", "ffn2_pre_norm", "ffn2_post_norm")


def _unshard_cols(g):
    return g.transpose(1, 0, 2).reshape(g.shape[1], -1)


def _shard_cols(a):
    return a.reshape(a.shape[0], N_CHIPS, -1).transpose(1, 0, 2)


def _pack_rows(rows, width):
    rows = [jnp.pad(r, ((0, 0), (0, width - r.shape[1]))) for r in rows]
    n = sum(r.shape[0] for r in rows)
    return jnp.pad(jnp.concatenate(rows, axis=0), ((0, -n % 8), (0, 0)))


def _weight_views(full):
    w = {}
    for n, g in full.items():
        if n == "w_in":
            w_in = _unshard_cols(g)
            w["w_r"] = w_in[:, :IN_RET]
            w["w_c"] = jnp.pad(w_in[:, IN_RET:], ((0, 0), (0, IN_MLA_PAD - IN_MLA)))
        elif n == "mla_w_uq":
            q = _unshard_cols(g).reshape(MLA_Q_RANK, HEADS, HEAD_DIM + MLA_ROPE)
            q = jnp.pad(q, ((0, 0), (0, 0), (0, MLA_QK_PAD - HEAD_DIM - MLA_ROPE)))
            w["wuq"] = q.reshape(MLA_Q_RANK, HEADS * MLA_QK_PAD)
        elif n in ("mla_w_uk", "mla_w_uv"):
            w["wu" + n[-1]] = _unshard_cols(g)
        elif n == "w_out":
            w["w_out"] = g.reshape(-1, g.shape[2])
        else:
            w["w" + n[7] + n[3]] = g
    return w


def _contributions(g):
    ffn = {"g": "gate", "u": "up", "d": "down"}
    c = {f"ffn{n[2]}_w_{ffn[n[1]]}": a for n, a in g.items() if len(n) == 3 and n[2] in "12"}
    if "w_out" in g:
        c["w_out"] = g["w_out"]
    if "w_r" in g:
        dwuq = g["wuq"].reshape(MLA_Q_RANK, HEADS, MLA_QK_PAD)[:, :, :HEAD_DIM + MLA_ROPE]
        c.update(w_in=_shard_cols(jnp.concatenate([g["w_r"], g["w_c"][:, :IN_MLA]], axis=1)),
                 mla_w_uq=_shard_cols(dwuq.reshape(MLA_Q_RANK, -1)), mla_w_uk=_shard_cols(g["wuk"]),
                 mla_w_uv=_shard_cols(g["wuv"]))
    return c


class _Schedule(_Alone):
    FIRST = ("ffn1_w_gate",)
    CARRIED = {"ffn1_gate": ("ffn1_w_up",), "ffn1_up": ("ffn1_w_down",),
               "ffn1_down": ("w_in", "mla_w_uq", "mla_w_uk", "mla_w_uv"),
               "proj_r": ("w_out",), "mla_fwd": ("ffn2_w_gate", "ffn2_w_up"), "ffn2_up": ("ffn2_w_down",)}
    GRAD_HOST = dict(ffn2_w_gate="mla_bwd", ffn2_w_up="mla_bwd", ffn2_w_down="mla_bwd",
                     w_out="mla_bwd", w_in="ffn1_da", mla_w_uq="ffn1_da", mla_w_uk="ffn1_da", mla_w_uv="ffn1_da",
                     ffn1_w_gate="ffn1_dwu", ffn1_w_up="ffn1_dwd", ffn1_w_down="ffn1_dn")

    def __init__(self, shards):
        self.gathers = {k: (gather_halves([shards[n] for n in names]), names) for k, names in self.CARRIED.items()}
        self.waiting = {}
        self.exchanges = {}
        self.grad = {}

    def host(self, kernel_name):
        if kernel_name in self.gathers:
            return self.gathers[kernel_name][0]
        if kernel_name in self.waiting:
            names, sums = zip(*self.waiting.pop(kernel_name))
            self.exchanges[kernel_name] = (chip_exchange(list(sums)), names)
            return self.exchanges[kernel_name][0]
        return None

    def done(self, kernel_name, w):
        if kernel_name in self.gathers:
            comm, names = self.gathers[kernel_name]
            w.update(_weight_views({n: sibling_fill("fill_" + n, b) for n, b in zip(names, comm.result)}))
        elif kernel_name in self.exchanges:
            comm, names = self.exchanges[kernel_name]
            for n, q in zip(names, comm.result):
                self.grad[n] = reduce_join("reduce_join_" + n, q)

    def grads(self, g):
        for n, a in _contributions(g).items():
            self.waiting.setdefault(self.GRAD_HOST[n], []).append((n, pair_add("pair_add_" + n, a)))


def _step(p, m, v, x, loss_target):
    d = x.shape[2]
    names = list(BIG)
    shards = {n: p[n][0].astype(BF16) for n in names}
    first = _Schedule.FIRST
    gathered = gather_halves([shards[n] for n in first] + [p["meta_tokens"]]).run("gather_first")
    filled = [sibling_fill("fill_" + n, b) for n, b in zip(first + ("meta_tokens",), gathered)]
    w = _weight_views(dict(zip(first, filled[:-1])))
    meta = _unshard_cols(filled[-1])
    nw = {n: p[n] for n in NORMS}
    plan = _Schedule(shards)
    loss, grad_x, small, d_meta = _local_step(x[0], loss_target[0], meta, w, nw, plan)
    grads = dict(plan.grad)

    width = max(d, HEADS * HEAD_DIM)
    packed = _pack_rows([small[n] for n in NORMS] + [d_meta], width)
    total = sum_slots("small_sum", all_gather_devices(packed), F32)
    for i, n in enumerate(NORMS):
        grads[n] = total[i:i + 1, :p[n].shape[1]]
    cols = p["meta_tokens"].shape[1]
    chip = 2 * lax.axis_index("x") + lax.axis_index("y")
    grads["meta_tokens"] = lax.dynamic_slice(total[len(NORMS):len(NORMS) + N_META, :d], (0, chip * cols), (N_META, cols))

    delta, new_m, new_v = {}, {}, {}
    for n in names + ["meta_tokens"]:
        shape = p[n].shape
        flat = lambda a: a.reshape(-1, shape[-1])
        out = adamw("adamw_" + n, flat(p[n]), flat(grads[n]), flat(m[n]), flat(v[n]))
        delta[n], new_m[n], new_v[n], grads[n] = (o.reshape(shape) for o in out)
    pk = lambda src: _pack_rows([src[n] for n in NORMS], width)
    out = adamw("adamw_norms", pk(p), pk(grads), pk(m), pk(v))
    for i, n in enumerate(NORMS):
        delta[n], new_m[n], new_v[n] = (o[i:i + 1, :p[n].shape[1]] for o in out[:3])

    loss = lax.psum(loss, ("x", "y", "c"))
    return loss, grad_x[None], grads, delta, new_m, new_v


def kernel(x, meta_tokens, ffn1_pre_norm, ffn1_w_gate, ffn1_w_up, ffn1_w_down, ffn1_post_norm, mix_pre_norm, w_in, ret_group_norm, mla_q_norm, mla_w_uq, mla_kv_norm, mla_w_uk, mla_w_uv, w_out, mix_post_norm, ffn2_pre_norm, ffn2_w_gate, ffn2_w_up, ffn2_w_down, ffn2_post_norm, loss_target, m_meta_tokens, m_ffn1_pre_norm, m_ffn1_w_gate, m_ffn1_w_up, m_ffn1_w_down, m_ffn1_post_norm, m_mix_pre_norm, m_w_in, m_ret_group_norm, m_mla_q_norm, m_mla_w_uq, m_mla_kv_norm, m_mla_w_uk, m_mla_w_uv, m_w_out, m_mix_post_norm, m_ffn2_pre_norm, m_ffn2_w_gate, m_ffn2_w_up, m_ffn2_w_down, m_ffn2_post_norm, v_meta_tokens, v_ffn1_pre_norm, v_ffn1_w_gate, v_ffn1_w_up, v_ffn1_w_down, v_ffn1_post_norm, v_mix_pre_norm, v_w_in, v_ret_group_norm, v_mla_q_norm, v_mla_w_uq, v_mla_kv_norm, v_mla_w_uk, v_mla_w_uv, v_w_out, v_mix_post_norm, v_ffn2_pre_norm, v_ffn2_w_gate, v_ffn2_w_up, v_ffn2_w_down, v_ffn2_post_norm):
    args = locals()
    p = {n: args[n] for n in WEIGHTS}
    m = {n: args["m_" + n] for n in WEIGHTS}
    v = {n: args["v_" + n] for n in WEIGHTS}
    loss, grad_x, grads, delta, new_m, new_v = _step(p, m, v, x, loss_target)
    return (loss, grad_x, *[grads[n] for n in WEIGHTS], *[delta[n] for n in WEIGHTS],
            *[new_m[n] for n in WEIGHTS], *[new_v[n] for n in WEIGHTS])
```

```python
import functools
import math

import jax
import jax.numpy as jnp
from jax import lax
from jax.experimental import pallas as pl
from jax.experimental.pallas import tpu as pltpu

F32 = jnp.float32
BF16 = jnp.bfloat16

EPS = 1e-6
N_META = 16
HEADS = 8
HEAD_DIM = 128
MLA_ROPE = 64
MLA_QK_PAD = 256
MLA_Q_RANK = 512
MLA_KV_RANK = 256
ROPE_THETA = 10000.0
N_CHIPS = 4
LANES = 128
VMEM_LIMIT = 60 * 2 ** 20

ADAM_LR = 0.001
ADAM_B1 = 0.9
ADAM_B2 = 0.999
ADAM_EPS = 1e-08
ADAM_WD = 0.01
ADAM_STEP = 10

NN = (((1,), (0,)), ((), ()))
NT = (((1,), (1,)), ((), ()))
TN = (((0,), (0,)), ((), ()))
MESH = pl.DeviceIdType.MESH


def _tile(n, pref, align=LANES):
    if n <= pref:
        return n
    best = 0
    for t in range(align, pref + 1, align):
        if n % t == 0:
            best = t
    assert best, (n, pref)
    return best


def _params(*sem, collective_id=None):
    return pltpu.CompilerParams(dimension_semantics=sem, vmem_limit_bytes=VMEM_LIMIT, collective_id=collective_id)


SIBLING_BARRIER = 0


def _sds(shape, dtype):
    return jax.ShapeDtypeStruct(tuple(shape), dtype)


def _rowwise(name, fn, rows, consts, outs, accs, tr):
    rows = [r if isinstance(r, tuple) else (r, r.shape[1], 0) for r in rows]
    t = rows[0][0].shape[0]
    assert t % tr == 0
    n_r, n_c, n_o = len(rows), len(consts), len(outs)

    def body(*refs):
        i = pl.program_id(0)
        r = [x[...] for x in refs[:n_r]]
        c = [x[...] for x in refs[n_r:n_r + n_c]]
        o_refs = refs[n_r + n_c:n_r + n_c + n_o]
        a_refs = refs[n_r + n_c + n_o:]
        o_vals, a_vals = fn(i * tr, *r, *c)
        for ref, v in zip(o_refs, o_vals):
            ref[...] = v.astype(ref.dtype)
        if a_refs:
            @pl.when(i == 0)
            def _():
                for ref, v in zip(a_refs, a_vals):
                    ref[...] = v

            @pl.when(i > 0)
            def _():
                for ref, v in zip(a_refs, a_vals):
                    ref[...] += v

    in_specs = [pl.BlockSpec((tr, w), functools.partial(lambda cb, i: (i, cb), cb)) for _, w, cb in rows]
    in_specs += [pl.BlockSpec(a.shape, lambda i: (0, 0)) for a in consts]
    out_specs = [pl.BlockSpec((tr, w), lambda i: (i, 0)) for w, _ in outs]
    out_specs += [pl.BlockSpec((1, w), lambda i: (0, 0)) for w in accs]
    out_shape = [_sds((t, w), dt) for w, dt in outs] + [_sds((1, w), F32) for w in accs]
    res = pl.pallas_call(
        body, name=name, grid=(t // tr,), in_specs=in_specs, out_specs=out_specs, out_shape=out_shape,
        compiler_params=_params("arbitrary"),
    )(*[a for a, _, _ in rows], *consts)
    return res[:n_o], res[n_o:]


def _rms(x, w):
    r = lax.rsqrt(jnp.mean(x * x, axis=-1, keepdims=True) + EPS)
    return x * r * w


def _rms_bwd(x, w, dy):
    r = lax.rsqrt(jnp.mean(x * x, axis=-1, keepdims=True) + EPS)
    xh = x * r
    gy = dy * w
    dx = r * (gy - xh * jnp.mean(gy * xh, axis=-1, keepdims=True))
    return dx, dy * xh


def _colsum(v):
    return jnp.sum(v, axis=0, keepdims=True)


def _silu(x):
    return x * jax.nn.sigmoid(x)


def _dsilu(x):
    s = jax.nn.sigmoid(x)
    return s * (1.0 + x * (1.0 - s))


def _rope_r(x, cos, sin):
    return x * cos + pltpu.roll(x, 64, 1) * sin


def _rope_r_t(dy, cos, sin):
    return dy * cos + pltpu.roll(dy * sin, 64, 1)


def _rope_m(x, cos, sa, sb):
    return x * cos + pltpu.roll(x, 32, 1) * sa + pltpu.roll(x, 96, 1) * sb


def _rope_m_t(dy, cos, sa, sb):
    return dy * cos + pltpu.roll(dy * sa, 96, 1) + pltpu.roll(dy * sb, 32, 1)


def _rope_tables(t):
    pos = jnp.arange(t, dtype=F32)

    def cs(dim):
        inv = ROPE_THETA ** (-jnp.arange(0, dim, 2, dtype=F32) / dim)
        ang = pos[:, None] * inv[None, :]
        return jnp.cos(ang), jnp.sin(ang)

    c, s = cs(HEAD_DIM)
    cos_r = jnp.concatenate([c, c], axis=1)
    sin_r = jnp.concatenate([-s, s], axis=1)
    c, s = cs(MLA_ROPE)
    z32, z64 = jnp.zeros_like(s), jnp.zeros((t, 64), F32)
    cos_m = jnp.concatenate([c, c, z64], axis=1)
    sa = jnp.concatenate([z32, s, z64], axis=1)
    sb = jnp.concatenate([-s, z32, z64], axis=1)
    return cos_r, sin_r, cos_m, sa, sb


def _call(body, name, grid, in_specs, out_specs, out_shape, scratch, operands, host=None):
    sem = ("arbitrary",) * len(grid)
    if host is None:
        return pl.pallas_call(body, name=name, grid=grid, in_specs=in_specs, out_specs=out_specs, out_shape=out_shape,
                              scratch_shapes=scratch, compiler_params=_params(*sem))(*operands)
    n_in, n_out, n_s = len(in_specs), len(out_shape), len(scratch)
    h_in, h_out = len(host.ins), len(host.out_shape)

    def hosted(*refs):
        a = n_in
        b = a + h_in
        c = b + n_out
        d = c + h_out
        e = d + n_s
        ids = [pl.program_id(i) for i in range(len(grid))]
        first = functools.reduce(jnp.logical_and, [i == 0 for i in ids])
        last = functools.reduce(jnp.logical_and, [i == g - 1 for i, g in zip(ids, grid)])
        comm = (refs[a:b], refs[c:d], refs[e:])

        @pl.when(first)
        def _():
            host.start(*comm)

        body(*refs[:a], *refs[b:c], *refs[d:e])

        @pl.when(last)
        def _():
            host.finish(*comm)

    hbm = pl.BlockSpec(memory_space=pl.ANY)
    res = pl.pallas_call(
        hosted, name=name, grid=grid, in_specs=list(in_specs) + [hbm] * h_in, out_specs=list(out_specs) + [hbm] * h_out,
        out_shape=list(out_shape) + list(host.out_shape), scratch_shapes=list(scratch) + host.scratch(),
        compiler_params=_params(*sem))(*operands, *host.ins)
    host.result = res[n_out:]
    return res[:n_out]


def _mm(name, grid, operands, in_specs, dns, out_specs, out_shape, epilogue=None, extras=(), extra_specs=(),
        acc_shape=None, host=None):
    n_p, n_e = len(dns), len(extras)
    nk = grid[2]
    n_o = len(out_shape)
    in_place = nk > 1 and epilogue is None and n_o == 1 and out_shape[0].dtype == F32

    def body(*refs):
        ab = refs[:2 * n_p]
        ex = refs[2 * n_p:2 * n_p + n_e]
        outs = refs[2 * n_p + n_e:2 * n_p + n_e + n_o]

        part = None
        for p in range(n_p):
            d = lax.dot_general(ab[2 * p][...], ab[2 * p + 1][...], dns[p], preferred_element_type=F32)
            part = d if part is None else part + d

        def finish(acc):
            vals = (acc,) if epilogue is None else epilogue(acc, *[e[...] for e in ex])
            for o, v in zip(outs, vals):
                o[...] = v.astype(o.dtype)

        if nk == 1:
            finish(part)
        else:
            acc_ref = outs[0] if in_place else refs[2 * n_p + n_e + n_o]
            k = pl.program_id(2)

            @pl.when(k == 0)
            def _():
                acc_ref[...] = part

            @pl.when(k > 0)
            def _():
                acc_ref[...] += part

            if not in_place:
                @pl.when(k == nk - 1)
                def _():
                    finish(acc_ref[...])

    scratch = [] if nk == 1 or in_place else [pltpu.VMEM(acc_shape, F32)]
    return _call(body, name, grid, list(in_specs) + list(extra_specs), out_specs, out_shape, scratch,
                 [*operands, *extras], host)


def mm_nn(name, x, w, tm, out_dtype=F32, epilogue=None, outs=None, w2=None, n_block=None, extras=(), host=None):
    t, kdim = x.shape
    if w.ndim == 3:
        s, _, ns = w.shape
        n, tn, nb = s * ns, ns, s
        wspec = pl.BlockSpec((None, kdim, ns), lambda j, i, k: (j, 0, 0))
    else:
        n = w.shape[1]
        tn = n_block or n
        nb = n // tn
        wspec = pl.BlockSpec((kdim, tn), lambda j, i, k: (0, j))
    xspec = pl.BlockSpec((tm, kdim), lambda j, i, k: (i, 0))
    ospec = pl.BlockSpec((tm, tn), lambda j, i, k: (i, j))
    outs = outs or [out_dtype]
    grid = (nb, t // tm, 1)
    if w2 is None:
        return _mm(name, grid, [x, w], [xspec, wspec], [NN], [ospec] * len(outs), [_sds((t, n), d) for d in outs],
                   epilogue=epilogue, extras=extras, extra_specs=[ospec] * len(extras), host=host)

    def body(x_ref, w_ref, w2_ref, *o_refs):
        xv = x_ref[...]
        a = jnp.dot(xv, w_ref[...], preferred_element_type=F32)
        b = jnp.dot(xv, w2_ref[...], preferred_element_type=F32)
        for o, v in zip(o_refs, epilogue(a, b)):
            o[...] = v.astype(o.dtype)

    return _call(body, name, grid[:2],
                 [pl.BlockSpec((tm, kdim), lambda j, i: (i, 0)),
                  pl.BlockSpec((None, kdim, tn), lambda j, i: (j, 0, 0)),
                  pl.BlockSpec((None, kdim, tn), lambda j, i: (j, 0, 0))],
                 [pl.BlockSpec((tm, tn), lambda j, i: (i, j))] * len(outs), [_sds((t, n), d) for d in outs], [],
                 [x, w, w2], host)


def mm_nn_k(name, x, w, tm, tk, out_dtype=F32, host=None):
    t, kdim = x.shape
    n = w.shape[1]
    grid = (t // tm, 1, kdim // tk)
    return _mm(name, grid, [x, w],
               [pl.BlockSpec((tm, tk), lambda i, j, k: (i, k)), pl.BlockSpec((tk, n), lambda i, j, k: (k, 0))],
               [NN], [pl.BlockSpec((tm, n), lambda i, j, k: (i, 0))], [_sds((t, n), out_dtype)],
               acc_shape=(tm, n), host=host)[0]


def mm_nt(name, xs, ws, tm, tn, outs=(F32,), epilogue=None, extras=(), host=None):
    t = xs[0].shape[0]
    specs, ops = [], []
    for x, w in zip(xs, ws):
        kdim = x.shape[1]
        specs.append(pl.BlockSpec((tm, kdim), lambda j, i, k: (i, 0)))
        if w.ndim == 3:
            assert tn == w.shape[1]
            n = w.shape[0] * w.shape[1]
            specs.append(pl.BlockSpec((None, tn, kdim), lambda j, i, k: (j, 0, 0)))
        else:
            n = w.shape[0]
            specs.append(pl.BlockSpec((tn, kdim), lambda j, i, k: (j, 0)))
        ops += [x, w]
    ospec = pl.BlockSpec((tm, tn), lambda j, i, k: (i, j))
    return _mm(name, (n // tn, t // tm, 1), ops, specs, [NT] * len(xs), [ospec] * len(outs),
               [_sds((t, n), d) for d in outs], epilogue=epilogue, extras=extras,
               extra_specs=[ospec] * len(extras), host=host)


def mm_nt_k(name, xs, ws, tm, tn, out_dtype=F32, host=None):
    t = xs[0].shape[0]
    s, n, ns = ws[0].shape
    specs, ops = [], []
    for x, w in zip(xs, ws):
        specs.append(pl.BlockSpec((tm, ns), lambda i, j, k: (i, k)))
        specs.append(pl.BlockSpec((None, tn, ns), lambda i, j, k: (k, j, 0)))
        ops += [x, w]
    return _mm(name, (t // tm, n // tn, s), ops, specs, [NT] * len(xs),
               [pl.BlockSpec((tm, tn), lambda i, j, k: (i, j))], [_sds((t, n), out_dtype)], acc_shape=(tm, tn),
               host=host)[0]


def mm_tn(name, x, y, tm, tn, tk, out_dtype, shard_rows=False, shard_cols=False, host=None):
    t, m = x.shape
    n = y.shape[1]
    grid = (m // tm, n // tn, t // tk)
    if shard_cols:
        ospec = pl.BlockSpec((None, tm, tn), lambda i, j, k: (j, i, 0))
        oshape = _sds((n // tn, m, tn), out_dtype)
    elif shard_rows:
        ospec = pl.BlockSpec((None, tm, tn), lambda i, j, k: (i, 0, j))
        oshape = _sds((m // tm, tm, n), out_dtype)
    else:
        ospec = pl.BlockSpec((tm, tn), lambda i, j, k: (i, j))
        oshape = _sds((m, n), out_dtype)
    return _mm(name, grid, [x, y],
               [pl.BlockSpec((tk, tm), lambda i, j, k: (k, i)), pl.BlockSpec((tk, tn), lambda i, j, k: (k, j))],
               [TN], [ospec], [oshape], acc_shape=(tm, tn), host=host)[0]


def _decay_logs():
    return [math.log(1.0 - 2.0 ** (-5.0 - h)) for h in range(HEADS)]


def _log_decay(h):
    lg = jnp.float32(_decay_logs()[0])
    for i in range(1, HEADS):
        lg = jnp.where(h == i, jnp.float32(_decay_logs()[i]), lg)
    return lg


def _decayed_scores(q, k, lg):
    s = lax.dot_general(q, k, NT, preferred_element_type=F32)
    row = lax.broadcasted_iota(jnp.int32, s.shape, 0)
    col = lax.broadcasted_iota(jnp.int32, s.shape, 1)
    dec = jnp.where(col <= row, jnp.exp(jnp.maximum(row - col, 0).astype(F32) * lg), 0.0)
    return s * dec, dec


def _causal(s):
    row = lax.broadcasted_iota(jnp.int32, s.shape, 0)
    col = lax.broadcasted_iota(jnp.int32, s.shape, 1)
    return jnp.where(col <= row, s, -1e30)


def attn_fwd(name, q, k, v, blk, scale, host=None):
    t = q.shape[0]
    dq = q.shape[1] // HEADS
    nq = t // blk

    def body(q_ref, k_ref, v_ref, o_ref, lse_ref, m_ref, l_ref, acc_ref):
        qi = pl.program_id(1)
        qv = q_ref[...]
        m_ref[...] = jnp.full_like(m_ref, -1e30)
        l_ref[...] = jnp.zeros_like(l_ref)
        acc_ref[...] = jnp.zeros_like(acc_ref)

        def keys(start, n, diag_from):
            rows = pl.ds(pl.multiple_of(start, blk), n)
            s = lax.dot_general(qv, k_ref[rows, :], NT, preferred_element_type=F32) * scale
            if diag_from is not None:
                row = lax.broadcasted_iota(jnp.int32, s.shape, 0)
                col = lax.broadcasted_iota(jnp.int32, s.shape, 1)
                s = jnp.where(col - diag_from <= row, s, -1e30)
            m = m_ref[...]
            m_new = jnp.maximum(m, jnp.max(s, axis=-1, keepdims=True))
            p = jnp.exp(s - m_new)
            alpha = jnp.exp(m - m_new)
            m_ref[...] = m_new
            l_ref[...] = alpha * l_ref[...] + jnp.sum(p, axis=-1, keepdims=True)
            acc_ref[...] = alpha * acc_ref[...] + jnp.dot(p.astype(BF16), v_ref[rows, :], preferred_element_type=F32)

        @pl.loop(0, qi // 2)
        def _(j):
            keys(j * (2 * blk), 2 * blk, None)

        @pl.when(qi % 2 == 1)
        def _():
            keys((qi - 1) * blk, 2 * blk, blk)

        @pl.when(qi % 2 == 0)
        def _():
            keys(qi * blk, blk, 0)

        l = l_ref[...]
        o_ref[...] = acc_ref[...] / l
        lse_ref[...] = jnp.broadcast_to(m_ref[...] + jnp.log(l), (blk, HEAD_DIM))

    hspec = pl.BlockSpec((blk, HEAD_DIM), lambda h, i: (i, h))
    return _call(body, name, (HEADS, nq),
                 [pl.BlockSpec((blk, dq), lambda h, i: (i, h)),
                  pl.BlockSpec((t, dq), lambda h, i: (0, h)),
                  pl.BlockSpec((t, HEAD_DIM), lambda h, i: (0, h))],
                 [hspec, hspec], [_sds((t, HEADS * HEAD_DIM), F32)] * 2,
                 [pltpu.VMEM((blk, 1), F32), pltpu.VMEM((blk, 1), F32), pltpu.VMEM((blk, HEAD_DIM), F32)],
                 [q, k, v], host)


def attn_bwd(name, q, k, v, do, o, lse, blk, scale, host=None):
    t = q.shape[0]
    dq_w = q.shape[1] // HEADS
    nb = t // blk

    def body(q_ref, k_ref, v_ref, do_ref, o_ref, lse_ref, dq_ref, dk_ref, dv_ref):
        ki = pl.program_id(1)
        kv = k_ref[...]
        vv = v_ref[...]

        @pl.when(ki == 0)
        def _():
            dq_ref[...] = jnp.zeros_like(dq_ref)

        def queries(start, n, diag):
            rows = pl.ds(pl.multiple_of(start, blk), n)
            qv, dov = q_ref[rows, :], do_ref[rows, :]
            s = lax.dot_general(qv, kv, NT, preferred_element_type=F32) * scale
            if diag:
                s = _causal(s)
            p = jnp.exp(s - lse_ref[rows, :][:, :1])
            dp = lax.dot_general(dov, vv, NT, preferred_element_type=F32)
            delta = jnp.sum(dov.astype(F32) * o_ref[rows, :], axis=-1, keepdims=True)
            ds = p * (dp - delta) * scale
            pb, dsb = p.astype(BF16), ds.astype(BF16)
            dv_ref[...] += lax.dot_general(pb, dov, TN, preferred_element_type=F32)
            dk_ref[...] += lax.dot_general(dsb, qv, TN, preferred_element_type=F32)
            dq_ref[rows, :] += jnp.dot(dsb, kv, preferred_element_type=F32)

        dk_ref[...] = jnp.zeros_like(dk_ref)
        dv_ref[...] = jnp.zeros_like(dv_ref)
        queries(ki * blk, blk, True)
        later = nb - 1 - ki

        @pl.when(later % 2 == 1)
        def _():
            queries((ki + 1) * blk, blk, False)

        @pl.loop(0, later // 2)
        def _(j):
            queries((ki + 1 + later % 2 + 2 * j) * blk, 2 * blk, False)

    full = lambda w: pl.BlockSpec((t, w), lambda h, j: (0, h))
    blkd = lambda w: pl.BlockSpec((blk, w), lambda h, j: (j, h))
    return _call(body, name, (HEADS, nb),
                 [full(dq_w), blkd(dq_w), blkd(HEAD_DIM), full(HEAD_DIM), full(HEAD_DIM), full(HEAD_DIM)],
                 [full(dq_w), blkd(dq_w), blkd(HEAD_DIM)],
                 [_sds(q.shape, F32), _sds(k.shape, F32), _sds(v.shape, F32)], [], [q, k, v, do, o, lse], host)


def _chunk_decays(lg, blk):
    row = lax.broadcasted_iota(jnp.int32, (blk, HEAD_DIM), 0).astype(F32)
    return jnp.exp(lg * (row + 1.0)), jnp.exp(lg * (blk - 1.0 - row)), jnp.exp(lg * blk * jnp.ones((1, HEAD_DIM), F32))


def ret_fwd(name, q, k, v, blk, host=None):
    t = q.shape[0]
    nb = t // blk

    def body(q_ref, k_ref, v_ref, o_ref, st_ref, state):
        h, i = pl.program_id(0), pl.program_id(1)
        lg = _log_decay(h)

        @pl.when(i == 0)
        def _():
            state[...] = jnp.zeros_like(state)

        qv, kv, vv = q_ref[...], k_ref[...], v_ref[...]
        before = state[...]
        st_ref[...] = before
        p, _ = _decayed_scores(qv, kv, lg)
        xi, zeta, g_blk = _chunk_decays(lg, blk)
        o_ref[...] = (jnp.dot(p.astype(BF16), vv, preferred_element_type=F32)
                      + jnp.dot(qv, before.astype(BF16), preferred_element_type=F32) * xi)
        kz = (kv.astype(F32) * zeta).astype(BF16)
        state[...] = before * g_blk + lax.dot_general(kz, vv, TN, preferred_element_type=F32)

    hspec = pl.BlockSpec((blk, HEAD_DIM), lambda h, i: (i, h))
    return _call(body, name, (HEADS, nb), [hspec] * 3,
                 [hspec, pl.BlockSpec((HEAD_DIM, HEAD_DIM), lambda h, i: (i, h))],
                 [_sds((t, HEADS * HEAD_DIM), F32), _sds((nb * HEAD_DIM, HEADS * HEAD_DIM), F32)],
                 [pltpu.VMEM((HEAD_DIM, HEAD_DIM), F32)], [q, k, v], host)


def ret_bwd(name, q, k, v, do, states, blk, host=None):
    t = q.shape[0]
    nb = t // blk

    def body(q_ref, k_ref, v_ref, do_ref, st_ref, dq_ref, dk_ref, dv_ref, dstate):
        h, i = pl.program_id(0), pl.program_id(1)
        lg = _log_decay(h)

        @pl.when(i == 0)
        def _():
            dstate[...] = jnp.zeros_like(dstate)

        qv, kv, vv, dov = q_ref[...], k_ref[...], v_ref[...], do_ref[...]
        before = st_ref[...].astype(BF16)
        after_grad = dstate[...]
        p, dec = _decayed_scores(qv, kv, lg)
        ds = lax.dot_general(dov, vv, NT, preferred_element_type=F32) * dec
        pb, dsb = p.astype(BF16), ds.astype(BF16)
        xi, zeta, g_blk = _chunk_decays(lg, blk)
        dox = (dov.astype(F32) * xi).astype(BF16)
        kz = (kv.astype(F32) * zeta).astype(BF16)
        agb = after_grad.astype(BF16)
        dv_ref[...] = (lax.dot_general(pb, dov, TN, preferred_element_type=F32)
                       + jnp.dot(kz, agb, preferred_element_type=F32))
        dq_ref[...] = (jnp.dot(dsb, kv, preferred_element_type=F32)
                       + lax.dot_general(dox, before, NT, preferred_element_type=F32))
        dk_ref[...] = (lax.dot_general(dsb, qv, TN, preferred_element_type=F32)
                       + lax.dot_general(vv, agb, NT, preferred_element_type=F32) * zeta)
        dstate[...] = after_grad * g_blk + lax.dot_general(qv, dox, TN, preferred_element_type=F32)

    hspec = pl.BlockSpec((blk, HEAD_DIM), lambda h, i: (nb - 1 - i, h))
    return _call(body, name, (HEADS, nb),
                 [hspec] * 4 + [pl.BlockSpec((HEAD_DIM, HEAD_DIM), lambda h, i: (nb - 1 - i, h))],
                 [hspec] * 3, [_sds(q.shape, F32)] * 3, [pltpu.VMEM((HEAD_DIM, HEAD_DIM), F32)],
                 [q, k, v, do, states], host)


CHIP_FLIPS = ((1, 0), (0, 1), (1, 1))


def _position():
    return lax.axis_index("x"), lax.axis_index("y"), lax.axis_index("c")


class _Comm:
    def __init__(self, ins, out_shape, plan, n_remote, n_local):
        self.ins, self.out_shape, self.plan = list(ins), list(out_shape), plan
        self.n_remote, self.n_local = n_remote, n_local
        self.result = None

    def scratch(self):
        return [pltpu.SemaphoreType.DMA((self.n_remote,)), pltpu.SemaphoreType.DMA((self.n_remote,)),
                pltpu.SemaphoreType.DMA((self.n_local,))]

    def _copies(self, in_refs, out_refs, sems):
        send_sems, recv_sems, local_sems = sems
        pos = _position()
        p = self.plan(pos, in_refs, out_refs)

        def remote(k, src, dst, dev):
            return pltpu.make_async_remote_copy(src_ref=src, dst_ref=dst, send_sem=send_sems.at[k],
                                                recv_sem=recv_sems.at[k], device_id=dev, device_id_type=MESH)

        local = [pltpu.make_async_copy(s, d, local_sems.at[i]) for i, (s, d) in enumerate(p["local"])]
        out = [remote(k, s, d, dev) for k, (s, d, dev) in enumerate(p["sends"])]
        arrivals = [functools.partial(remote, k, d, d, pos) for k, d in enumerate(p["recvs"])]
        return local, out, arrivals

    def start(self, in_refs, out_refs, sems):
        local, out, _ = self._copies(in_refs, out_refs, sems)
        for cp in local + out:
            cp.start()

    def finish(self, in_refs, out_refs, sems):
        local, out, arrivals = self._copies(in_refs, out_refs, sems)
        for make in arrivals:
            make().wait_recv()
        for cp in out:
            cp.wait_send()
        for cp in local:
            cp.wait()

    def run(self, name):
        n_in, n_out = len(self.ins), len(self.out_shape)

        def body(*refs):
            comm = (refs[:n_in], refs[n_in:n_in + n_out], refs[n_in + n_out:])
            self.start(*comm)
            self.finish(*comm)

        hbm = pl.BlockSpec(memory_space=pl.ANY)
        self.result = pl.pallas_call(body, name=name, in_specs=[hbm] * n_in, out_specs=[hbm] * n_out,
                                     out_shape=self.out_shape, scratch_shapes=self.scratch())(*self.ins)
        return self.result


def _half_rows(c, rows):
    r2 = rows // 2
    return pl.ds(pl.multiple_of(c * r2, math.gcd(r2, LANES)), r2)


def _half(ref, c, rows, lead=()):
    return ref.at[(*lead, _half_rows(c, rows))]


def gather_halves(shards):
    def plan(pos, ins, outs):
        x, y, c = pos
        me = 2 * x + y
        p = dict(local=[], sends=[], recvs=[])
        for a, (src, dst) in enumerate(zip(ins, outs)):
            rows = shards[a].shape[0]
            p["local"].append((src, dst.at[me]))
            for fx, fy in CHIP_FLIPS:
                px, py = x ^ fx, y ^ fy
                p["sends"].append((_half(src, c, rows), _half(dst, c, rows, (me,)), (px, py, c)))
                p["recvs"].append(_half(dst, c, rows, (2 * px + py,)))
        return p

    return _Comm(shards, [_sds((N_CHIPS, *s.shape), s.dtype) for s in shards], plan,
                 n_remote=3 * len(shards), n_local=len(shards))


def chip_exchange(parts):
    def plan(pos, ins, outs):
        x, y, c = pos
        me = 2 * x + y
        p = dict(local=[], sends=[], recvs=[])
        for src, dst in zip(ins, outs):
            p["local"].append((src.at[me], dst.at[me]))
            for fx, fy in CHIP_FLIPS:
                px, py = x ^ fx, y ^ fy
                peer = 2 * px + py
                p["sends"].append((src.at[peer], dst.at[me], (px, py, c)))
                p["recvs"].append(dst.at[peer])
        return p

    return _Comm(parts, [_sds(g.shape, g.dtype) for g in parts], plan, n_remote=3 * len(parts), n_local=len(parts))


def all_gather_devices(v):
    flips = [(fx, fy, fc) for fx in (0, 1) for fy in (0, 1) for fc in (0, 1)][1:]

    def plan(pos, ins, outs):
        x, y, c = pos
        me = 4 * x + 2 * y + c
        p = dict(local=[(ins[0], outs[0].at[me])], sends=[], recvs=[])
        for fx, fy, fc in flips:
            px, py, pc = x ^ fx, y ^ fy, c ^ fc
            p["sends"].append((ins[0], outs[0].at[me], (px, py, pc)))
            p["recvs"].append(outs[0].at[4 * px + 2 * py + pc])
        return p

    return _Comm([v], [_sds((8, *v.shape), v.dtype)], plan, n_remote=7, n_local=1).run("small_all_gather")[0]


SWAP_CHUNK_BYTES = 3 * 2 ** 19


def _sibling_stream(t, n, value, consume, sbuf, rbuf, send_sems, recv_sems, credits):
    x, y, c = _position()
    sib = (x, y, 1 - c)

    def copy(slot):
        return pltpu.make_async_remote_copy(src_ref=sbuf.at[slot], dst_ref=rbuf.at[slot], send_sem=send_sems.at[slot],
                                            recv_sem=recv_sems.at[slot], device_id=sib, device_id_type=MESH)

    slot = t % 2

    @pl.when(t == 0)
    def _():
        barrier = pltpu.get_barrier_semaphore()
        pl.semaphore_signal(barrier, inc=1, device_id=sib, device_id_type=MESH)
        pl.semaphore_wait(barrier, 1)

    @pl.when(jnp.logical_and(t >= 2, t < n))
    def _():
        copy(slot).wait_send()
        pl.semaphore_wait(credits.at[slot], 1)

    @pl.when(t < n)
    def _():
        sbuf[slot] = value
        copy(slot).start()

    @pl.when(t >= 1)
    def _():
        prev = 1 - slot
        copy(prev).wait_recv()
        consume(sbuf[prev], rbuf[prev])

        @pl.when(t + 1 < n)
        def _():
            pl.semaphore_signal(credits.at[prev], inc=1, device_id=sib, device_id_type=MESH)

    @pl.when(t == n)
    def _():
        copy(1 - slot).wait_send()
        if n > 1:
            copy(slot).wait_send()


def _swap_scratch(rows, cols, dtype):
    return [pltpu.VMEM((2, rows, cols), dtype), pltpu.VMEM((2, rows, cols), dtype),
            pltpu.SemaphoreType.DMA((2,)), pltpu.SemaphoreType.DMA((2,)), pltpu.SemaphoreType.REGULAR((2,))]


def _chunk_rows(rows, cols, dtype, nbytes=SWAP_CHUNK_BYTES):
    return _tile(rows, max(16, nbytes // (cols * jnp.dtype(dtype).itemsize)), 16)


def pair_add(name, g):
    s, r, c_ = g.shape
    r2 = r // 2
    cr = _chunk_rows(r2, c_, g.dtype)
    nj = r2 // cr

    n = s * nj

    def body(core, mine_ref, theirs_ref, o_ref, *scratch):
        def consume(_, got):
            o_ref[...] = (mine_ref[...].astype(F32) + got.astype(F32)).astype(o_ref.dtype)

        _sibling_stream(pl.program_id(0), n, theirs_ref[...], consume, *scratch)

    sent = lambda t: jnp.minimum(t, n - 1)
    used = lambda t: jnp.maximum(t - 1, 0)
    grid_spec = pltpu.PrefetchScalarGridSpec(
        num_scalar_prefetch=1, grid=(n + 1,),
        in_specs=[pl.BlockSpec((None, cr, c_), lambda t, core: (used(t) // nj, core[0] * nj + used(t) % nj, 0)),
                  pl.BlockSpec((None, cr, c_), lambda t, core: (sent(t) // nj, (1 - core[0]) * nj + sent(t) % nj, 0))],
        out_specs=pl.BlockSpec((None, cr, c_), lambda t, core: (used(t) // nj, used(t) % nj, 0)),
        scratch_shapes=_swap_scratch(cr, c_, g.dtype))
    core = lax.axis_index("c").astype(jnp.int32).reshape(1)
    return pl.pallas_call(body, name=name, grid_spec=grid_spec, out_shape=_sds((s, r2, c_), g.dtype),
                          compiler_params=_params("arbitrary", collective_id=SIBLING_BARRIER))(core, g, g)


def _adamw_math(w, g, m, v):
    m = ADAM_B1 * m + (1.0 - ADAM_B1) * g
    v = ADAM_B2 * v + (1.0 - ADAM_B2) * (g * g)
    m_hat = m / (1.0 - ADAM_B1 ** ADAM_STEP)
    v_hat = v / (1.0 - ADAM_B2 ** ADAM_STEP)
    return -ADAM_LR * (m_hat / (jnp.sqrt(v_hat) + ADAM_EPS) + ADAM_WD * w), m, v


def reduce_join(name, p):
    s, r2, c_ = p.shape
    cr = _chunk_rows(r2, c_, F32)
    nj = r2 // cr

    def body(core, p_ref, o_ref, *scratch):
        acc = p_ref[0].astype(F32)
        for i in range(1, s):
            acc = acc + p_ref[i].astype(F32)

        def consume(own, got):
            c = core[0]
            o_ref[c] = own
            o_ref[1 - c] = got

        _sibling_stream(pl.program_id(0), nj, acc, consume, *scratch)

    grid_spec = pltpu.PrefetchScalarGridSpec(
        num_scalar_prefetch=1, grid=(nj + 1,),
        in_specs=[pl.BlockSpec((s, cr, c_), lambda t, core: (0, jnp.minimum(t, nj - 1), 0))],
        out_specs=pl.BlockSpec((2, cr, c_), lambda t, core: (0, jnp.maximum(t - 1, 0), 0)),
        scratch_shapes=_swap_scratch(cr, c_, F32))
    core = lax.axis_index("c").astype(jnp.int32).reshape(1)
    out = pl.pallas_call(body, name=name, grid_spec=grid_spec, out_shape=_sds((2, r2, c_), F32),
                         compiler_params=_params("arbitrary", collective_id=SIBLING_BARRIER))(core, p)
    return out.reshape(2 * r2, c_)


def sibling_fill(name, buf):
    s, r, c_ = buf.shape
    r2 = r // 2
    cr = _chunk_rows(r2, c_, buf.dtype)
    nj = r2 // cr
    n_peers = len(CHIP_FLIPS)

    n = n_peers * nj

    def body(where, in_ref, o_ref, *scratch):
        def consume(_, got):
            o_ref[...] = got

        _sibling_stream(pl.program_id(0), n, in_ref[...], consume, *scratch)

    sent = lambda t: jnp.minimum(t, n - 1)
    used = lambda t: jnp.maximum(t - 1, 0)
    grid_spec = pltpu.PrefetchScalarGridSpec(
        num_scalar_prefetch=1, grid=(n + 1,),
        in_specs=[pl.BlockSpec((None, cr, c_),
                               lambda t, where: (where[sent(t) // nj], where[n_peers] * nj + sent(t) % nj, 0))],
        out_specs=pl.BlockSpec((None, cr, c_),
                               lambda t, where: (where[used(t) // nj], (1 - where[n_peers]) * nj + used(t) % nj, 0)),
        scratch_shapes=_swap_scratch(cr, c_, buf.dtype))
    x, y, c = _position()
    where = jnp.stack([2 * (x ^ fx) + (y ^ fy) for fx, fy in CHIP_FLIPS] + [c]).astype(jnp.int32)
    return pl.pallas_call(body, name=name, grid_spec=grid_spec, out_shape=_sds(buf.shape, buf.dtype),
                          input_output_aliases={1: 0},
                          compiler_params=_params("arbitrary", collective_id=SIBLING_BARRIER))(where, buf)


def sum_slots(name, p, out_dtype):
    s, r, c = p.shape
    tr = _tile(r, 256, 16)

    def body(p_ref, o_ref):
        acc = p_ref[0].astype(F32)
        for i in range(1, s):
            acc = acc + p_ref[i].astype(F32)
        o_ref[...] = acc.astype(o_ref.dtype)

    return pl.pallas_call(
        body, name=name, grid=(r // tr,), in_specs=[pl.BlockSpec((s, tr, c), lambda i: (0, i, 0))],
        out_specs=pl.BlockSpec((tr, c), lambda i: (i, 0)), out_shape=_sds((r, c), out_dtype),
        compiler_params=_params("arbitrary"),
    )(p)


def adamw(name, w, g, m, v):
    r, c = w.shape
    outs, _ = _rowwise(name, lambda _, w, g, m, v: ([*_adamw_math(w, g, m, v), g], []), [w, g, m, v], [],
                       [(c, F32)] * 4, [], _tile(r, 256, 8))
    return outs


RET_SCALE = HEAD_DIM ** -0.5
MLA_SCALE = (HEAD_DIM + MLA_ROPE) ** -0.5
GRAD_DT = BF16
IN_RET = 4 * HEADS * HEAD_DIM
IN_MLA = MLA_Q_RANK + MLA_KV_RANK + MLA_ROPE
IN_MLA_PAD = IN_MLA + 64


def _heads(fn):
    return jnp.concatenate([fn(h) for h in range(HEADS)], axis=1)


def _head(a, h, stride=HEAD_DIM, off=0):
    return a[:, h * stride + off:h * stride + off + HEAD_DIM]


def _group_norm(o):
    rs = [lax.rsqrt(jnp.mean(_head(o, h) * _head(o, h), axis=-1, keepdims=True) + EPS) for h in range(HEADS)]
    return _heads(lambda h: _head(o, h) * rs[h]), rs


class _Alone:
    def host(self, kernel_name):
        return None

    def done(self, kernel_name, w):
        pass

    def grads(self, g):
        pass


def _ffn_fwd(tag, n, w, k, tm, tmk, plan):
    gate, up, down = tag + "_gate", tag + "_up", tag + "_down"
    if "wu" + k in w:
        g, u, a = mm_nn(up, n, w["wg" + k], tm, outs=[BF16] * 3, w2=w["wu" + k],
                        epilogue=lambda g, u: (g, u, _silu(g) * u), host=plan.host(up))
    else:
        (g,) = mm_nn(gate, n, w["wg" + k], tm, out_dtype=BF16, host=plan.host(gate))
        plan.done(gate, w)
        u, a = mm_nn(up, n, w["wu" + k], tm, outs=[BF16] * 2, extras=(g,),
                     epilogue=lambda u, g: (u, _silu(g.astype(F32)) * u), host=plan.host(up))
    plan.done(up, w)
    ff = a.shape[1]
    wd = w["wd" + k]
    f = mm_nn_k(down, a, wd.reshape(ff, wd.shape[2]), tmk, _tile(ff, 1408), host=plan.host(down))
    plan.done(down, w)
    return g, u, a, f


def _ffn_bwd(tag, df, n, g, u, a, wg, wu, wd, tm, tmk, tk, dt, plan):
    ns = wg.shape[2]

    def gate_grads(da, g, u):
        g, u = g.astype(F32), u.astype(F32)
        return da * u * _dsilu(g), da * _silu(g)

    def hosted(kernel_name, call):
        out = call(plan.host(kernel_name))
        plan.done(kernel_name, None)
        return out

    k = tag[-1]
    dg, du = hosted(tag + "_da", lambda h: mm_nt(tag + "_da", [df], [wd], tm, ns, outs=(BF16, BF16),
                                                 epilogue=gate_grads, extras=(g, u), host=h))
    dwg = hosted(tag + "_dwg", lambda h: mm_tn(tag + "_dwg", n, dg, dt, ns, tk, GRAD_DT, shard_cols=True, host=h))
    plan.grads({"wg" + k: dwg})
    dwu = hosted(tag + "_dwu", lambda h: mm_tn(tag + "_dwu", n, du, dt, ns, tk, GRAD_DT, shard_cols=True, host=h))
    plan.grads({"wu" + k: dwu})
    dwd = hosted(tag + "_dwd", lambda h: mm_tn(tag + "_dwd", a, df, ns, dt, tk, GRAD_DT, shard_rows=True, host=h))
    plan.grads({"wd" + k: dwd})
    dn = hosted(tag + "_dn", lambda h: mm_nt_k(tag + "_dn", [dg, du], [wg, wu], tmk, dt, out_dtype=BF16, host=h))
    return dn


def _local_step(x, tgt, meta, w, nw, plan):
    seq, d = x.shape
    t_real = N_META + seq
    tp = -(-t_real // LANES) * LANES
    zpad = jnp.zeros((tp - t_real, d), F32)
    h0 = jnp.concatenate([meta, x, zpad], axis=0)
    tgt_p = jnp.concatenate([jnp.zeros((N_META, d), F32), tgt, zpad], axis=0)
    cos_r, sin_r, cos_m, sa, sb = _rope_tables(tp)
    tm = _tile(tp, 512)
    tmt = _tile(tp, 768, 16)
    tmk = _tile(tp, 1408)
    tk = tp
    blk = tm
    dt = _tile(d, 1024)
    hw = HEADS * HEAD_DIM
    qw = HEADS * MLA_QK_PAD

    (n1,), _ = _rowwise("ffn1_norm", lambda r0, h, g: ([_rms(h, g)], []), [h0], [nw["ffn1_pre_norm"]],
                        [(d, BF16)], [], tm)
    g1, u1, a1, f1 = _ffn_fwd("ffn1", n1, w, "1", tm, tmk, plan)

    def post_ffn1(r0, h, f, post, pre):
        h1 = h + 0.5 * _rms(f, post)
        return [h1, _rms(h1, pre)], []

    (h1, un), _ = _rowwise("mix_norm", post_ffn1, [h0, f1], [nw["ffn1_post_norm"], nw["mix_pre_norm"]],
                           [(d, F32), (d, BF16)], [], tm)
    (proj_r,) = mm_nn("proj_r", un, w["w_r"], tm, n_block=_tile(IN_RET, 1024), host=plan.host("proj_r"))
    plan.done("proj_r", w)
    (proj_c,) = mm_nn("proj_c", un, w["w_c"], tm)

    def split_proj(r0, pr, pc, cr, sr, cm, ta, tb, qn, kvn):
        rq = _heads(lambda h: _rope_r(_head(pr, h), cr, sr))
        rk = _heads(lambda h: _rope_r(_head(pr, h, off=hw), cr, sr) * RET_SCALE)
        rv = pr[:, 2 * hw:3 * hw]
        cqn = _rms(pc[:, :MLA_Q_RANK], qn)
        ckvn = _rms(pc[:, MLA_Q_RANK:MLA_Q_RANK + MLA_KV_RANK], kvn)
        krr = _rope_m(pc[:, MLA_Q_RANK + MLA_KV_RANK:], cm, ta, tb)
        return [rq, rk, rv, cqn, ckvn, krr], []

    (rq, rk, rv, cqn, ckvn, krr), _ = _rowwise(
        "split_proj", split_proj, [proj_r, proj_c, cos_r, sin_r, cos_m, sa, sb],
        [nw["mla_q_norm"], nw["mla_kv_norm"]],
        [(hw, BF16), (hw, BF16), (hw, BF16), (MLA_Q_RANK, BF16), (MLA_KV_RANK, BF16), (LANES, F32)], [], tm)
    (qp,) = mm_nn("q_up", cqn, w["wuq"], tm)
    (kn,) = mm_nn("k_up", ckvn, w["wuk"], tm)
    (vv,) = mm_nn("v_up", ckvn, w["wuv"], tm, out_dtype=BF16)

    def build_qk(r0, qp, kn, krr, cm, ta, tb):
        qc = jnp.concatenate(
            [part for h in range(HEADS)
             for part in (_head(qp, h, MLA_QK_PAD), _rope_m(_head(qp, h, MLA_QK_PAD, HEAD_DIM), cm, ta, tb))], axis=1)
        kc = jnp.concatenate([part for h in range(HEADS) for part in (_head(kn, h), krr)], axis=1)
        return [qc, kc], []

    (qc, kc), _ = _rowwise("build_qk", build_qk, [qp, kn, krr, cos_m, sa, sb], [], [(qw, BF16), (qw, BF16)], [], tm)
    o_m, lse = attn_fwd("mla_fwd", qc, kc, vv, blk, MLA_SCALE, host=plan.host("mla_fwd"))
    plan.done("mla_fwd", w)
    o_r, ret_states = ret_fwd("ret_fwd", rq, rk, rv, blk, host=plan.host("ret_fwd"))
    plan.done("ret_fwd", w)

    def gate_mix(r0, rg, o_r, o_m, gn):
        y, _ = _group_norm(o_r)
        return [jnp.concatenate([_silu(rg) * (y * gn), o_m], axis=1)], []

    (mixcat,), _ = _rowwise("gate_mix", gate_mix, [(proj_r, hw, 3), o_r, o_m], [nw["ret_group_norm"]],
                            [(2 * hw, BF16)], [], tm)
    (mix,) = mm_nn("mix_out", mixcat, w["w_out"], tm, n_block=dt)

    def post_mix(r0, h, m, post, pre):
        h2 = h + _rms(m, post)
        return [h2, _rms(h2, pre)], []

    (h2, n3), _ = _rowwise("ffn2_norm", post_mix, [h1, mix], [nw["mix_post_norm"], nw["ffn2_pre_norm"]],
                           [(d, F32), (d, BF16)], [], tm)
    g2, u2, a2, f2 = _ffn_fwd("ffn2", n3, w, "2", tm, tmk, plan)

    def loss_head(r0, h, f, t, post):
        h3 = h + 0.5 * _rms(f, post)
        row = r0 + lax.broadcasted_iota(jnp.int32, h3.shape, 0)
        err = jnp.where(row >= N_META, jnp.where(row < t_real, h3 - t, 0.0), 0.0)
        dh3 = err / d
        df, dpost = _rms_bwd(f, post, 0.5 * dh3)
        return [dh3, df], [_colsum(err * err), _colsum(dpost)]

    (dh3, df2), (loss_vec, d_post2) = _rowwise("loss_head", loss_head, [h2, f2, tgt_p], [nw["ffn2_post_norm"]],
                                               [(d, F32), (d, BF16)], [d, d], tm)
    loss = 0.5 * jnp.sum(loss_vec) / d
    dn3 = _ffn_bwd("ffn2", df2, n3, g2, u2, a2, w["wg2"], w["wu2"], w["wd2"], tmt, tmk, tk, dt, plan)

    def back_mix_norm(r0, h, m, dh3, dn, pre, post):
        dx, dpre = _rms_bwd(h, pre, dn)
        dh2 = dh3 + dx
        dm, dpost = _rms_bwd(m, post, dh2)
        return [dh2, dm], [_colsum(dpre), _colsum(dpost)]

    (dh2, dmix), (d_pre2, d_mix_post) = _rowwise(
        "back_mix_norm", back_mix_norm, [h2, mix, dh3, dn3], [nw["ffn2_pre_norm"], nw["mix_post_norm"]],
        [(d, F32), (d, BF16)], [d, d], tm)
    (dmixcat,) = mm_nt("mix_dx", [dmix], [w["w_out"]], tmt, _tile(2 * hw, 512), outs=(BF16,))
    plan.grads(dict(w_out=mm_tn("mix_dw", mixcat, dmix, 2 * hw // N_CHIPS, dt, tk, GRAD_DT, shard_rows=True)))

    def back_gate(r0, dmc, rg, o_r, gn):
        d_ret, d_om = dmc[:, :hw], dmc[:, hw:]
        yh, rs = _group_norm(o_r)
        d_rg = d_ret * (yh * gn) * _dsilu(rg)
        dy = d_ret * _silu(rg)
        gyh = dy * gn
        d_or = _heads(lambda h: rs[h] * (_head(gyh, h) - _head(yh, h) * jnp.mean(_head(gyh, h) * _head(yh, h),
                                                                                    axis=-1, keepdims=True)))
        return [d_or, d_rg, d_om], [_colsum(dy * yh)]

    (d_or, d_rg, d_om), (d_gn,) = _rowwise("back_gate", back_gate, [dmixcat, (proj_r, hw, 3), o_r],
                                           [nw["ret_group_norm"]], [(hw, BF16), (hw, F32), (hw, BF16)], [hw], tm)
    dqc, dkc, dvv = attn_bwd("mla_bwd", qc, kc, vv, d_om, o_m, lse, blk, MLA_SCALE, host=plan.host("mla_bwd"))
    plan.done("mla_bwd", None)
    drq, drk, drv = ret_bwd("ret_bwd", rq, rk, rv, d_or, ret_states, blk)

    def back_qk(r0, dqc, dkc, dvv, cm, ta, tb):
        dqp = jnp.concatenate(
            [part for h in range(HEADS)
             for part in (_head(dqc, h, MLA_QK_PAD), _rope_m_t(_head(dqc, h, MLA_QK_PAD, HEAD_DIM), cm, ta, tb))],
            axis=1)
        dkn = _heads(lambda h: _head(dkc, h, MLA_QK_PAD))
        dkr = _head(dkc, 0, MLA_QK_PAD, HEAD_DIM)
        for h in range(1, HEADS):
            dkr = dkr + _head(dkc, h, MLA_QK_PAD, HEAD_DIM)
        return [dqp, dkn, _rope_m_t(dkr, cm, ta, tb), dvv], []

    (dqp, dkn, dkr, dvb), _ = _rowwise("back_qk", back_qk, [dqc, dkc, dvv, cos_m, sa, sb], [],
                                       [(qw, BF16), (hw, BF16), (LANES, F32), (hw, BF16)], [], tm)
    (dcqn,) = mm_nt("q_dx", [dqp], [w["wuq"]], tm, MLA_Q_RANK)
    dwuq = mm_tn("q_dw", cqn, dqp, MLA_Q_RANK, _tile(qw, 1024), tk, GRAD_DT)
    (dckvn,) = mm_nt("kv_dx", [dkn, dvb], [w["wuk"], w["wuv"]], tm, MLA_KV_RANK)
    dwuk = mm_tn("k_dw", ckvn, dkn, MLA_KV_RANK, hw, tk, GRAD_DT)
    dwuv = mm_tn("v_dw", ckvn, dvb, MLA_KV_RANK, hw, tk, GRAD_DT)

    def back_proj(r0, drq, drk, drv, d_rg, pc, dcqn, dckvn, dkr, cr, sr, qn, kvn):
        d_q = _heads(lambda h: _rope_r_t(_head(drq, h), cr, sr))
        d_k = _heads(lambda h: _rope_r_t(_head(drk, h), cr, sr) * RET_SCALE)
        dcq, a_q = _rms_bwd(pc[:, :MLA_Q_RANK], qn, dcqn)
        dckv, a_kv = _rms_bwd(pc[:, MLA_Q_RANK:MLA_Q_RANK + MLA_KV_RANK], kvn, dckvn)
        return ([jnp.concatenate([d_q, d_k, drv, d_rg], axis=1), jnp.concatenate([dcq, dckv, dkr], axis=1)],
                [_colsum(a_q), _colsum(a_kv)])

    (dproj_r, dproj_c), (d_qn, d_kvn) = _rowwise(
        "back_proj", back_proj, [drq, drk, drv, d_rg, proj_c, dcqn, dckvn, dkr, cos_r, sin_r],
        [nw["mla_q_norm"], nw["mla_kv_norm"]], [(IN_RET, BF16), (IN_MLA_PAD, BF16)],
        [MLA_Q_RANK, MLA_KV_RANK], tm)
    (dun,) = mm_nt("proj_dx", [dproj_r, dproj_c], [w["w_r"], w["w_c"]], tmt, _tile(d, 512), outs=(BF16,))
    dw_r = mm_tn("proj_dw_r", un, dproj_r, dt, _tile(IN_RET, 1024), tk, GRAD_DT)
    dw_c = mm_tn("proj_dw_c", un, dproj_c, dt, IN_MLA_PAD, tk, GRAD_DT)
    plan.grads(dict(w_r=dw_r, w_c=dw_c, wuq=dwuq, wuk=dwuk, wuv=dwuv))

    def back_ffn1_norm(r0, h, f, dh2, dn, pre, post):
        dx, dpre = _rms_bwd(h, pre, dn)
        dh1 = dh2 + dx
        df, dpost = _rms_bwd(f, post, 0.5 * dh1)
        return [dh1, df], [_colsum(dpre), _colsum(dpost)]

    (dh1, df1), (d_mix_pre, d_post1) = _rowwise(
        "back_ffn1_norm", back_ffn1_norm, [h1, f1, dh2, dun], [nw["mix_pre_norm"], nw["ffn1_post_norm"]],
        [(d, F32), (d, BF16)], [d, d], tm)
    dn1 = _ffn_bwd("ffn1", df1, n1, g1, u1, a1, w["wg1"], w["wu1"], w["wd1"], tmt, tmk, tk, dt, plan)

    def back_input(r0, h, dh1, dn, pre):
        dx, dpre = _rms_bwd(h, pre, dn)
        return [dh1 + dx], [_colsum(dpre)]

    (dh0,), (d_pre1,) = _rowwise("back_input", back_input, [h0, dh1, dn1], [nw["ffn1_pre_norm"]], [(d, F32)], [d], tm)

    small = dict(ffn1_pre_norm=d_pre1, ffn1_post_norm=d_post1, mix_pre_norm=d_mix_pre, ret_group_norm=d_gn,
                 mla_q_norm=d_qn, mla_kv_norm=d_kvn, mix_post_norm=d_mix_post, ffn2_pre_norm=d_pre2,
                 ffn2_post_norm=d_post2)
    return loss, dh0[N_META:t_real], small, dh0[:N_META]


WEIGHTS = ("meta_tokens", "ffn1_pre_norm", "ffn1_w_gate", "ffn1_w_up", "ffn1_w_down", "ffn1_post_norm",
           "mix_pre_norm", "w_in", "ret_group_norm", "mla_q_norm", "mla_w_uq", "mla_kv_norm", "mla_w_uk",
           "mla_w_uv", "w_out", "mix_post_norm", "ffn2_pre_norm", "ffn2_w_gate", "ffn2_w_up", "ffn2_w_down",
           "ffn2_post_norm")
BIG = ("ffn1_w_gate", "ffn1_w_up", "ffn1_w_down", "w_in", "mla_w_uq", "mla_w_uk", "mla_w_uv", "w_out",
       "ffn2_w_gate", "ffn2_w_up", "ffn2_w_down")
NORMS = ("ffn1_pre_norm", "ffn1_post_norm", "mix_pre_norm", "ret_group_norm", "mla_q_norm", "mla_kv_norm",
         "mix_post_norm", "ffn2_pre_norm", "ffn2_post_norm")


def _unshard_cols(g):
    return g.transpose(1, 0, 2).reshape(g.shape[1], -1)


def _shard_cols(a):
    return a.reshape(a.shape[0], N_CHIPS, -1).transpose(1, 0, 2)


def _pack_rows(rows, width):
    rows = [jnp.pad(r, ((0, 0), (0, width - r.shape[1]))) for r in rows]
    n = sum(r.shape[0] for r in rows)
    return jnp.pad(jnp.concatenate(rows, axis=0), ((0, -n % 8), (0, 0)))


def _weight_views(full):
    w = {}
    for n, g in full.items():
        if n == "w_in":
            w_in = _unshard_cols(g)
            w["w_r"] = w_in[:, :IN_RET]
            w["w_c"] = jnp.pad(w_in[:, IN_RET:], ((0, 0), (0, IN_MLA_PAD - IN_MLA)))
        elif n == "mla_w_uq":
            q = _unshard_cols(g).reshape(MLA_Q_RANK, HEADS, HEAD_DIM + MLA_ROPE)
            q = jnp.pad(q, ((0, 0), (0, 0), (0, MLA_QK_PAD - HEAD_DIM - MLA_ROPE)))
            w["wuq"] = q.reshape(MLA_Q_RANK, HEADS * MLA_QK_PAD)
        elif n in ("mla_w_uk", "mla_w_uv"):
            w["wu" + n[-1]] = _unshard_cols(g)
        elif n == "w_out":
            w["w_out"] = g.reshape(-1, g.shape[2])
        else:
            w["w" + n[7] + n[3]] = g
    return w


def _contributions(g):
    ffn = {"g": "gate", "u": "up", "d": "down"}
    c = {f"ffn{n[2]}_w_{ffn[n[1]]}": a for n, a in g.items() if len(n) == 3 and n[2] in "12"}
    if "w_out" in g:
        c["w_out"] = g["w_out"]
    if "w_r" in g:
        dwuq = g["wuq"].reshape(MLA_Q_RANK, HEADS, MLA_QK_PAD)[:, :, :HEAD_DIM + MLA_ROPE]
        c.update(w_in=_shard_cols(jnp.concatenate([g["w_r"], g["w_c"][:, :IN_MLA]], axis=1)),
                 mla_w_uq=_shard_cols(dwuq.reshape(MLA_Q_RANK, -1)), mla_w_uk=_shard_cols(g["wuk"]),
                 mla_w_uv=_shard_cols(g["wuv"]))
    return c


class _Schedule(_Alone):
    FIRST = ("ffn1_w_gate",)
    CARRIED = {"ffn1_gate": ("ffn1_w_up",), "ffn1_up": ("ffn1_w_down",),
               "ffn1_down": ("w_in", "mla_w_uq", "mla_w_uk", "mla_w_uv"),
               "proj_r": ("w_out",), "mla_fwd": ("ffn2_w_gate", "ffn2_w_up"), "ffn2_up": ("ffn2_w_down",)}
    GRAD_HOST = dict(ffn2_w_gate="mla_bwd", ffn2_w_up="mla_bwd", ffn2_w_down="mla_bwd",
                     w_out="mla_bwd", w_in="ffn1_da", mla_w_uq="ffn1_da", mla_w_uk="ffn1_da", mla_w_uv="ffn1_da",
                     ffn1_w_gate="ffn1_dwu", ffn1_w_up="ffn1_dwd", ffn1_w_down="ffn1_dn")

    def __init__(self, shards):
        self.gathers = {k: (gather_halves([shards[n] for n in names]), names) for k, names in self.CARRIED.items()}
        self.waiting = {}
        self.exchanges = {}
        self.grad = {}

    def host(self, kernel_name):
        if kernel_name in self.gathers:
            return self.gathers[kernel_name][0]
        if kernel_name in self.waiting:
            names, sums = zip(*self.waiting.pop(kernel_name))
            self.exchanges[kernel_name] = (chip_exchange(list(sums)), names)
            return self.exchanges[kernel_name][0]
        return None

    def done(self, kernel_name, w):
        if kernel_name in self.gathers:
            comm, names = self.gathers[kernel_name]
            w.update(_weight_views({n: sibling_fill("fill_" + n, b) for n, b in zip(names, comm.result)}))
        elif kernel_name in self.exchanges:
            comm, names = self.exchanges[kernel_name]
            for n, q in zip(names, comm.result):
                self.grad[n] = reduce_join("reduce_join_" + n, q)

    def grads(self, g):
        for n, a in _contributions(g).items():
            self.waiting.setdefault(self.GRAD_HOST[n], []).append((n, pair_add("pair_add_" + n, a)))


def _step(p, m, v, x, loss_target):
    d = x.shape[2]
    names = list(BIG)
    shards = {n: p[n][0].astype(BF16) for n in names}
    first = _Schedule.FIRST
    gathered = gather_halves([shards[n] for n in first] + [p["meta_tokens"]]).run("gather_first")
    filled = [sibling_fill("fill_" + n, b) for n, b in zip(first + ("meta_tokens",), gathered)]
    w = _weight_views(dict(zip(first, filled[:-1])))
    meta = _unshard_cols(filled[-1])
    nw = {n: p[n] for n in NORMS}
    plan = _Schedule(shards)
    loss, grad_x, small, d_meta = _local_step(x[0], loss_target[0], meta, w, nw, plan)
    grads = dict(plan.grad)

    width = max(d, HEADS * HEAD_DIM)
    packed = _pack_rows([small[n] for n in NORMS] + [d_meta], width)
    total = sum_slots("small_sum", all_gather_devices(packed), F32)
    for i, n in enumerate(NORMS):
        grads[n] = total[i:i + 1, :p[n].shape[1]]
    cols = p["meta_tokens"].shape[1]
    chip = 2 * lax.axis_index("x") + lax.axis_index("y")
    grads["meta_tokens"] = lax.dynamic_slice(total[len(NORMS):len(NORMS) + N_META, :d], (0, chip * cols), (N_META, cols))

    delta, new_m, new_v = {}, {}, {}
    for n in names + ["meta_tokens"]:
        shape = p[n].shape
        flat = lambda a: a.reshape(-1, shape[-1])
        out = adamw("adamw_" + n, flat(p[n]), flat(grads[n]), flat(m[n]), flat(v[n]))
        delta[n], new_m[n], new_v[n], grads[n] = (o.reshape(shape) for o in out)
    pk = lambda src: _pack_rows([src[n] for n in NORMS], width)
    out = adamw("adamw_norms", pk(p), pk(grads), pk(m), pk(v))
    for i, n in enumerate(NORMS):
        delta[n], new_m[n], new_v[n] = (o[i:i + 1, :p[n].shape[1]] for o in out[:3])

    loss = lax.psum(loss, ("x", "y", "c"))
    return loss, grad_x[None], grads, delta, new_m, new_v


def kernel(x, meta_tokens, ffn1_pre_norm, ffn1_w_gate, ffn1_w_up, ffn1_w_down, ffn1_post_norm, mix_pre_norm, w_in, ret_group_norm, mla_q_norm, mla_w_uq, mla_kv_norm, mla_w_uk, mla_w_uv, w_out, mix_post_norm, ffn2_pre_norm, ffn2_w_gate, ffn2_w_up, ffn2_w_down, ffn2_post_norm, loss_target, m_meta_tokens, m_ffn1_pre_norm, m_ffn1_w_gate, m_ffn1_w_up, m_ffn1_w_down, m_ffn1_post_norm, m_mix_pre_norm, m_w_in, m_ret_group_norm, m_mla_q_norm, m_mla_w_uq, m_mla_kv_norm, m_mla_w_uk, m_mla_w_uv, m_w_out, m_mix_post_norm, m_ffn2_pre_norm, m_ffn2_w_gate, m_ffn2_w_up, m_ffn2_w_down, m_ffn2_post_norm, v_meta_tokens, v_ffn1_pre_norm, v_ffn1_w_gate, v_ffn1_w_up, v_ffn1_w_down, v_ffn1_post_norm, v_mix_pre_norm, v_w_in, v_ret_group_norm, v_mla_q_norm, v_mla_w_uq, v_mla_kv_norm, v_mla_w_uk, v_mla_w_uv, v_w_out, v_mix_post_norm, v_ffn2_pre_norm, v_ffn2_w_gate, v_ffn2_w_up, v_ffn2_w_down, v_ffn2_post_norm):
    args = locals()
    p = {n: args[n] for n in WEIGHTS}
    m = {n: args["m_" + n] for n in WEIGHTS}
    v = {n: args["v_" + n] for n in WEIGHTS}
    loss, grad_x, grads, delta, new_m, new_v = _step(p, m, v, x, loss_target)
    return (loss, grad_x, *[grads[n] for n in WEIGHTS], *[delta[n] for n in WEIGHTS],
            *[new_m[n] for n in WEIGHTS], *[new_v[n] for n in WEIGHTS])
```

```python
import functools
import math

import jax
import jax.numpy as jnp
from jax import lax
from jax.experimental import pallas as pl
from jax.experimental.pallas import tpu as pltpu

F32 = jnp.float32
BF16 = jnp.bfloat16

EPS = 1e-6
N_META = 16
HEADS = 8
HEAD_DIM = 128
MLA_ROPE = 64
MLA_QK_PAD = 256
MLA_Q_RANK = 512
MLA_KV_RANK = 256
ROPE_THETA = 10000.0
N_CHIPS = 4
LANES = 128
VMEM_LIMIT = 60 * 2 ** 20

ADAM_LR = 0.001
ADAM_B1 = 0.9
ADAM_B2 = 0.999
ADAM_EPS = 1e-08
ADAM_WD = 0.01
ADAM_STEP = 10

NN = (((1,), (0,)), ((), ()))
NT = (((1,), (1,)), ((), ()))
TN = (((0,), (0,)), ((), ()))
MESH = pl.DeviceIdType.MESH


def _tile(n, pref, align=LANES):
    if n <= pref:
        return n
    best = 0
    for t in range(align, pref + 1, align):
        if n % t == 0:
            best = t
    assert best, (n, pref)
    return best


def _params(*sem, collective_id=None):
    return pltpu.CompilerParams(dimension_semantics=sem, vmem_limit_bytes=VMEM_LIMIT, collective_id=collective_id)


SIBLING_BARRIER = 0


def _sds(shape, dtype):
    return jax.ShapeDtypeStruct(tuple(shape), dtype)


def _rowwise(name, fn, rows, consts, outs, accs, tr):
    rows = [r if isinstance(r, tuple) else (r, r.shape[1], 0) for r in rows]
    t = rows[0][0].shape[0]
    assert t % tr == 0
    n_r, n_c, n_o = len(rows), len(consts), len(outs)

    def body(*refs):
        i = pl.program_id(0)
        r = [x[...] for x in refs[:n_r]]
        c = [x[...] for x in refs[n_r:n_r + n_c]]
        o_refs = refs[n_r + n_c:n_r + n_c + n_o]
        a_refs = refs[n_r + n_c + n_o:]
        o_vals, a_vals = fn(i * tr, *r, *c)
        for ref, v in zip(o_refs, o_vals):
            ref[...] = v.astype(ref.dtype)
        if a_refs:
            @pl.when(i == 0)
            def _():
                for ref, v in zip(a_refs, a_vals):
                    ref[...] = v

            @pl.when(i > 0)
            def _():
                for ref, v in zip(a_refs, a_vals):
                    ref[...] += v

    in_specs = [pl.BlockSpec((tr, w), functools.partial(lambda cb, i: (i, cb), cb)) for _, w, cb in rows]
    in_specs += [pl.BlockSpec(a.shape, lambda i: (0, 0)) for a in consts]
    out_specs = [pl.BlockSpec((tr, w), lambda i: (i, 0)) for w, _ in outs]
    out_specs += [pl.BlockSpec((1, w), lambda i: (0, 0)) for w in accs]
    out_shape = [_sds((t, w), dt) for w, dt in outs] + [_sds((1, w), F32) for w in accs]
    res = pl.pallas_call(
        body, name=name, grid=(t // tr,), in_specs=in_specs, out_specs=out_specs, out_shape=out_shape,
        compiler_params=_params("arbitrary"),
    )(*[a for a, _, _ in rows], *consts)
    return res[:n_o], res[n_o:]


def _rms(x, w):
    r = lax.rsqrt(jnp.mean(x * x, axis=-1, keepdims=True) + EPS)
    return x * r * w


def _rms_bwd(x, w, dy):
    r = lax.rsqrt(jnp.mean(x * x, axis=-1, keepdims=True) + EPS)
    xh = x * r
    gy = dy * w
    dx = r * (gy - xh * jnp.mean(gy * xh, axis=-1, keepdims=True))
    return dx, dy * xh


def _colsum(v):
    return jnp.sum(v, axis=0, keepdims=True)


def _silu(x):
    return x * jax.nn.sigmoid(x)


def _dsilu(x):
    s = jax.nn.sigmoid(x)
    return s * (1.0 + x * (1.0 - s))


def _rope_r(x, cos, sin):
    return x * cos + pltpu.roll(x, 64, 1) * sin


def _rope_r_t(dy, cos, sin):
    return dy * cos + pltpu.roll(dy * sin, 64, 1)


def _rope_m(x, cos, sa, sb):
    return x * cos + pltpu.roll(x, 32, 1) * sa + pltpu.roll(x, 96, 1) * sb


def _rope_m_t(dy, cos, sa, sb):
    return dy * cos + pltpu.roll(dy * sa, 96, 1) + pltpu.roll(dy * sb, 32, 1)


def _rope_tables(t):
    pos = jnp.arange(t, dtype=F32)

    def cs(dim):
        inv = ROPE_THETA ** (-jnp.arange(0, dim, 2, dtype=F32) / dim)
        ang = pos[:, None] * inv[None, :]
        return jnp.cos(ang), jnp.sin(ang)

    c, s = cs(HEAD_DIM)
    cos_r = jnp.concatenate([c, c], axis=1)
    sin_r = jnp.concatenate([-s, s], axis=1)
    c, s = cs(MLA_ROPE)
    z32, z64 = jnp.zeros_like(s), jnp.zeros((t, 64), F32)
    cos_m = jnp.concatenate([c, c, z64], axis=1)
    sa = jnp.concatenate([z32, s, z64], axis=1)
    sb = jnp.concatenate([-s, z32, z64], axis=1)
    return cos_r, sin_r, cos_m, sa, sb


def _call(body, name, grid, in_specs, out_specs, out_shape, scratch, operands, host=None):
    sem = ("arbitrary",) * len(grid)
    if host is None:
        return pl.pallas_call(body, name=name, grid=grid, in_specs=in_specs, out_specs=out_specs, out_shape=out_shape,
                              scratch_shapes=scratch, compiler_params=_params(*sem))(*operands)
    n_in, n_out, n_s = len(in_specs), len(out_shape), len(scratch)
    h_in, h_out = len(host.ins), len(host.out_shape)

    def hosted(*refs):
        a = n_in
        b = a + h_in
        c = b + n_out
        d = c + h_out
        e = d + n_s
        ids = [pl.program_id(i) for i in range(len(grid))]
        first = functools.reduce(jnp.logical_and, [i == 0 for i in ids])
        last = functools.reduce(jnp.logical_and, [i == g - 1 for i, g in zip(ids, grid)])
        comm = (refs[a:b], refs[c:d], refs[e:])

        @pl.when(first)
        def _():
            host.start(*comm)

        body(*refs[:a], *refs[b:c], *refs[d:e])

        @pl.when(last)
        def _():
            host.finish(*comm)

    hbm = pl.BlockSpec(memory_space=pl.ANY)
    res = pl.pallas_call(
        hosted, name=name, grid=grid, in_specs=list(in_specs) + [hbm] * h_in, out_specs=list(out_specs) + [hbm] * h_out,
        out_shape=list(out_shape) + list(host.out_shape), scratch_shapes=list(scratch) + host.scratch(),
        compiler_params=_params(*sem, collective_id=host.collective_id))(*operands, *host.ins)
    host.result = res[n_out:]
    return res[:n_out]


def _mm(name, grid, operands, in_specs, dns, out_specs, out_shape, epilogue=None, extras=(), extra_specs=(),
        acc_shape=None, host=None):
    n_p, n_e = len(dns), len(extras)
    nk = grid[2]
    n_o = len(out_shape)
    in_place = nk > 1 and epilogue is None and n_o == 1 and out_shape[0].dtype == F32

    def body(*refs):
        ab = refs[:2 * n_p]
        ex = refs[2 * n_p:2 * n_p + n_e]
        outs = refs[2 * n_p + n_e:2 * n_p + n_e + n_o]

        part = None
        for p in range(n_p):
            d = lax.dot_general(ab[2 * p][...], ab[2 * p + 1][...], dns[p], preferred_element_type=F32)
            part = d if part is None else part + d

        def finish(acc):
            vals = (acc,) if epilogue is None else epilogue(acc, *[e[...] for e in ex])
            for o, v in zip(outs, vals):
                o[...] = v.astype(o.dtype)

        if nk == 1:
            finish(part)
        else:
            acc_ref = outs[0] if in_place else refs[2 * n_p + n_e + n_o]
            k = pl.program_id(2)

            @pl.when(k == 0)
            def _():
                acc_ref[...] = part

            @pl.when(k > 0)
            def _():
                acc_ref[...] += part

            if not in_place:
                @pl.when(k == nk - 1)
                def _():
                    finish(acc_ref[...])

    scratch = [] if nk == 1 or in_place else [pltpu.VMEM(acc_shape, F32)]
    return _call(body, name, grid, list(in_specs) + list(extra_specs), out_specs, out_shape, scratch,
                 [*operands, *extras], host)


def mm_nn(name, x, w, tm, out_dtype=F32, epilogue=None, outs=None, w2=None, n_block=None, extras=(), host=None):
    t, kdim = x.shape
    if w.ndim == 3:
        s, _, ns = w.shape
        n, tn, nb = s * ns, ns, s
        wspec = pl.BlockSpec((None, kdim, ns), lambda j, i, k: (j, 0, 0))
    else:
        n = w.shape[1]
        tn = n_block or n
        nb = n // tn
        wspec = pl.BlockSpec((kdim, tn), lambda j, i, k: (0, j))
    xspec = pl.BlockSpec((tm, kdim), lambda j, i, k: (i, 0))
    ospec = pl.BlockSpec((tm, tn), lambda j, i, k: (i, j))
    outs = outs or [out_dtype]
    grid = (nb, t // tm, 1)
    if w2 is None:
        return _mm(name, grid, [x, w], [xspec, wspec], [NN], [ospec] * len(outs), [_sds((t, n), d) for d in outs],
                   epilogue=epilogue, extras=extras, extra_specs=[ospec] * len(extras), host=host)

    def body(x_ref, w_ref, w2_ref, *o_refs):
        xv = x_ref[...]
        a = jnp.dot(xv, w_ref[...], preferred_element_type=F32)
        b = jnp.dot(xv, w2_ref[...], preferred_element_type=F32)
        for o, v in zip(o_refs, epilogue(a, b)):
            o[...] = v.astype(o.dtype)

    return _call(body, name, grid[:2],
                 [pl.BlockSpec((tm, kdim), lambda j, i: (i, 0)),
                  pl.BlockSpec((None, kdim, tn), lambda j, i: (j, 0, 0)),
                  pl.BlockSpec((None, kdim, tn), lambda j, i: (j, 0, 0))],
                 [pl.BlockSpec((tm, tn), lambda j, i: (i, j))] * len(outs), [_sds((t, n), d) for d in outs], [],
                 [x, w, w2], host)


def mm_nn_k(name, x, w, tm, tk, out_dtype=F32, host=None):
    t, kdim = x.shape
    n = w.shape[1]
    grid = (t // tm, 1, kdim // tk)
    return _mm(name, grid, [x, w],
               [pl.BlockSpec((tm, tk), lambda i, j, k: (i, k)), pl.BlockSpec((tk, n), lambda i, j, k: (k, 0))],
               [NN], [pl.BlockSpec((tm, n), lambda i, j, k: (i, 0))], [_sds((t, n), out_dtype)],
               acc_shape=(tm, n), host=host)[0]


def mm_nt(name, xs, ws, tm, tn, outs=(F32,), epilogue=None, extras=(), host=None):
    t = xs[0].shape[0]
    specs, ops = [], []
    for x, w in zip(xs, ws):
        kdim = x.shape[1]
        specs.append(pl.BlockSpec((tm, kdim), lambda j, i, k: (i, 0)))
        if w.ndim == 3:
            assert tn == w.shape[1]
            n = w.shape[0] * w.shape[1]
            specs.append(pl.BlockSpec((None, tn, kdim), lambda j, i, k: (j, 0, 0)))
        else:
            n = w.shape[0]
            specs.append(pl.BlockSpec((tn, kdim), lambda j, i, k: (j, 0)))
        ops += [x, w]
    ospec = pl.BlockSpec((tm, tn), lambda j, i, k: (i, j))
    return _mm(name, (n // tn, t // tm, 1), ops, specs, [NT] * len(xs), [ospec] * len(outs),
               [_sds((t, n), d) for d in outs], epilogue=epilogue, extras=extras,
               extra_specs=[ospec] * len(extras), host=host)


def mm_nt_k(name, xs, ws, tm, tn, out_dtype=F32, host=None):
    t = xs[0].shape[0]
    s, n, ns = ws[0].shape
    specs, ops = [], []
    for x, w in zip(xs, ws):
        specs.append(pl.BlockSpec((tm, ns), lambda i, j, k: (i, k)))
        specs.append(pl.BlockSpec((None, tn, ns), lambda i, j, k: (k, j, 0)))
        ops += [x, w]
    return _mm(name, (t // tm, n // tn, s), ops, specs, [NT] * len(xs),
               [pl.BlockSpec((tm, tn), lambda i, j, k: (i, j))], [_sds((t, n), out_dtype)], acc_shape=(tm, tn),
               host=host)[0]


def mm_tn(name, x, y, tm, tn, tk, out_dtype, shard_rows=False, shard_cols=False, host=None):
    t, m = x.shape
    n = y.shape[1]
    grid = (m // tm, n // tn, t // tk)
    if shard_cols:
        ospec = pl.BlockSpec((None, tm, tn), lambda i, j, k: (j, i, 0))
        oshape = _sds((n // tn, m, tn), out_dtype)
    elif shard_rows:
        ospec = pl.BlockSpec((None, tm, tn), lambda i, j, k: (i, 0, j))
        oshape = _sds((m // tm, tm, n), out_dtype)
    else:
        ospec = pl.BlockSpec((tm, tn), lambda i, j, k: (i, j))
        oshape = _sds((m, n), out_dtype)
    return _mm(name, grid, [x, y],
               [pl.BlockSpec((tk, tm), lambda i, j, k: (k, i)), pl.BlockSpec((tk, tn), lambda i, j, k: (k, j))],
               [TN], [ospec], [oshape], acc_shape=(tm, tn), host=host)[0]


def _decay_logs():
    return [math.log(1.0 - 2.0 ** (-5.0 - h)) for h in range(HEADS)]


def _log_decay(h):
    lg = jnp.float32(_decay_logs()[0])
    for i in range(1, HEADS):
        lg = jnp.where(h == i, jnp.float32(_decay_logs()[i]), lg)
    return lg


def _decayed_scores(q, k, lg):
    s = lax.dot_general(q, k, NT, preferred_element_type=F32)
    row = lax.broadcasted_iota(jnp.int32, s.shape, 0)
    col = lax.broadcasted_iota(jnp.int32, s.shape, 1)
    dec = jnp.where(col <= row, jnp.exp(jnp.maximum(row - col, 0).astype(F32) * lg), 0.0)
    return s * dec, dec


def _causal(s):
    row = lax.broadcasted_iota(jnp.int32, s.shape, 0)
    col = lax.broadcasted_iota(jnp.int32, s.shape, 1)
    return jnp.where(col <= row, s, -1e30)


def attn_fwd(name, q, k, v, blk, scale, host=None):
    t = q.shape[0]
    dq = q.shape[1] // HEADS
    nq = t // blk

    def body(q_ref, k_ref, v_ref, o_ref, lse_ref, m_ref, l_ref, acc_ref):
        qi = pl.program_id(1)
        qv = q_ref[...]
        m_ref[...] = jnp.full_like(m_ref, -1e30)
        l_ref[...] = jnp.zeros_like(l_ref)
        acc_ref[...] = jnp.zeros_like(acc_ref)

        def keys(start, n, diag_from):
            rows = pl.ds(pl.multiple_of(start, blk), n)
            s = lax.dot_general(qv, k_ref[rows, :], NT, preferred_element_type=F32) * scale
            if diag_from is not None:
                row = lax.broadcasted_iota(jnp.int32, s.shape, 0)
                col = lax.broadcasted_iota(jnp.int32, s.shape, 1)
                s = jnp.where(col - diag_from <= row, s, -1e30)
            m = m_ref[...]
            m_new = jnp.maximum(m, jnp.max(s, axis=-1, keepdims=True))
            p = jnp.exp(s - m_new)
            alpha = jnp.exp(m - m_new)
            m_ref[...] = m_new
            l_ref[...] = alpha * l_ref[...] + jnp.sum(p, axis=-1, keepdims=True)
            acc_ref[...] = alpha * acc_ref[...] + jnp.dot(p.astype(BF16), v_ref[rows, :], preferred_element_type=F32)

        @pl.loop(0, qi // 2)
        def _(j):
            keys(j * (2 * blk), 2 * blk, None)

        @pl.when(qi % 2 == 1)
        def _():
            keys((qi - 1) * blk, 2 * blk, blk)

        @pl.when(qi % 2 == 0)
        def _():
            keys(qi * blk, blk, 0)

        l = l_ref[...]
        o_ref[...] = acc_ref[...] / l
        lse_ref[...] = jnp.broadcast_to(m_ref[...] + jnp.log(l), (blk, HEAD_DIM))

    hspec = pl.BlockSpec((blk, HEAD_DIM), lambda h, i: (i, h))
    return _call(body, name, (HEADS, nq),
                 [pl.BlockSpec((blk, dq), lambda h, i: (i, h)),
                  pl.BlockSpec((t, dq), lambda h, i: (0, h)),
                  pl.BlockSpec((t, HEAD_DIM), lambda h, i: (0, h))],
                 [hspec, hspec], [_sds((t, HEADS * HEAD_DIM), F32)] * 2,
                 [pltpu.VMEM((blk, 1), F32), pltpu.VMEM((blk, 1), F32), pltpu.VMEM((blk, HEAD_DIM), F32)],
                 [q, k, v], host)


def attn_bwd(name, q, k, v, do, o, lse, blk, scale, host=None):
    t = q.shape[0]
    dq_w = q.shape[1] // HEADS
    nb = t // blk

    def body(q_ref, k_ref, v_ref, do_ref, o_ref, lse_ref, dq_ref, dk_ref, dv_ref):
        ki = pl.program_id(1)
        kv = k_ref[...]
        vv = v_ref[...]

        @pl.when(ki == 0)
        def _():
            dq_ref[...] = jnp.zeros_like(dq_ref)

        def queries(start, n, diag):
            rows = pl.ds(pl.multiple_of(start, blk), n)
            qv, dov = q_ref[rows, :], do_ref[rows, :]
            s = lax.dot_general(qv, kv, NT, preferred_element_type=F32) * scale
            if diag:
                s = _causal(s)
            p = jnp.exp(s - lse_ref[rows, :][:, :1])
            dp = lax.dot_general(dov, vv, NT, preferred_element_type=F32)
            delta = jnp.sum(dov.astype(F32) * o_ref[rows, :], axis=-1, keepdims=True)
            ds = p * (dp - delta) * scale
            pb, dsb = p.astype(BF16), ds.astype(BF16)
            dv_ref[...] += lax.dot_general(pb, dov, TN, preferred_element_type=F32)
            dk_ref[...] += lax.dot_general(dsb, qv, TN, preferred_element_type=F32)
            dq_ref[rows, :] += jnp.dot(dsb, kv, preferred_element_type=F32)

        dk_ref[...] = jnp.zeros_like(dk_ref)
        dv_ref[...] = jnp.zeros_like(dv_ref)
        queries(ki * blk, blk, True)
        later = nb - 1 - ki

        @pl.when(later % 2 == 1)
        def _():
            queries((ki + 1) * blk, blk, False)

        @pl.loop(0, later // 2)
        def _(j):
            queries((ki + 1 + later % 2 + 2 * j) * blk, 2 * blk, False)

    full = lambda w: pl.BlockSpec((t, w), lambda h, j: (0, h))
    blkd = lambda w: pl.BlockSpec((blk, w), lambda h, j: (j, h))
    return _call(body, name, (HEADS, nb),
                 [full(dq_w), blkd(dq_w), blkd(HEAD_DIM), full(HEAD_DIM), full(HEAD_DIM), full(HEAD_DIM)],
                 [full(dq_w), blkd(dq_w), blkd(HEAD_DIM)],
                 [_sds(q.shape, F32), _sds(k.shape, F32), _sds(v.shape, F32)], [], [q, k, v, do, o, lse], host)


def _chunk_decays(lg, blk):
    row = lax.broadcasted_iota(jnp.int32, (blk, HEAD_DIM), 0).astype(F32)
    return jnp.exp(lg * (row + 1.0)), jnp.exp(lg * (blk - 1.0 - row)), jnp.exp(lg * blk * jnp.ones((1, HEAD_DIM), F32))


def ret_fwd(name, q, k, v, blk, host=None):
    t = q.shape[0]
    nb = t // blk

    def body(q_ref, k_ref, v_ref, o_ref, st_ref, state):
        h, i = pl.program_id(0), pl.program_id(1)
        lg = _log_decay(h)

        @pl.when(i == 0)
        def _():
            state[...] = jnp.zeros_like(state)

        qv, kv, vv = q_ref[...], k_ref[...], v_ref[...]
        before = state[...]
        st_ref[...] = before
        p, _ = _decayed_scores(qv, kv, lg)
        xi, zeta, g_blk = _chunk_decays(lg, blk)
        o_ref[...] = (jnp.dot(p.astype(BF16), vv, preferred_element_type=F32)
                      + jnp.dot(qv, before.astype(BF16), preferred_element_type=F32) * xi)
        kz = (kv.astype(F32) * zeta).astype(BF16)
        state[...] = before * g_blk + lax.dot_general(kz, vv, TN, preferred_element_type=F32)

    hspec = pl.BlockSpec((blk, HEAD_DIM), lambda h, i: (i, h))
    return _call(body, name, (HEADS, nb), [hspec] * 3,
                 [hspec, pl.BlockSpec((HEAD_DIM, HEAD_DIM), lambda h, i: (i, h))],
                 [_sds((t, HEADS * HEAD_DIM), F32), _sds((nb * HEAD_DIM, HEADS * HEAD_DIM), F32)],
                 [pltpu.VMEM((HEAD_DIM, HEAD_DIM), F32)], [q, k, v], host)


def ret_bwd(name, q, k, v, do, states, blk, host=None):
    t = q.shape[0]
    nb = t // blk

    def body(q_ref, k_ref, v_ref, do_ref, st_ref, dq_ref, dk_ref, dv_ref, dstate):
        h, i = pl.program_id(0), pl.program_id(1)
        lg = _log_decay(h)

        @pl.when(i == 0)
        def _():
            dstate[...] = jnp.zeros_like(dstate)

        qv, kv, vv, dov = q_ref[...], k_ref[...], v_ref[...], do_ref[...]
        before = st_ref[...].astype(BF16)
        after_grad = dstate[...]
        p, dec = _decayed_scores(qv, kv, lg)
        ds = lax.dot_general(dov, vv, NT, preferred_element_type=F32) * dec
        pb, dsb = p.astype(BF16), ds.astype(BF16)
        xi, zeta, g_blk = _chunk_decays(lg, blk)
        dox = (dov.astype(F32) * xi).astype(BF16)
        kz = (kv.astype(F32) * zeta).astype(BF16)
        agb = after_grad.astype(BF16)
        dv_ref[...] = (lax.dot_general(pb, dov, TN, preferred_element_type=F32)
                       + jnp.dot(kz, agb, preferred_element_type=F32))
        dq_ref[...] = (jnp.dot(dsb, kv, preferred_element_type=F32)
                       + lax.dot_general(dox, before, NT, preferred_element_type=F32))
        dk_ref[...] = (lax.dot_general(dsb, qv, TN, preferred_element_type=F32)
                       + lax.dot_general(vv, agb, NT, preferred_element_type=F32) * zeta)
        dstate[...] = after_grad * g_blk + lax.dot_general(qv, dox, TN, preferred_element_type=F32)

    hspec = pl.BlockSpec((blk, HEAD_DIM), lambda h, i: (nb - 1 - i, h))
    return _call(body, name, (HEADS, nb),
                 [hspec] * 4 + [pl.BlockSpec((HEAD_DIM, HEAD_DIM), lambda h, i: (nb - 1 - i, h))],
                 [hspec] * 3, [_sds(q.shape, F32)] * 3, [pltpu.VMEM((HEAD_DIM, HEAD_DIM), F32)],
                 [q, k, v, do, states], host)


CHIP_FLIPS = ((1, 0), (0, 1), (1, 1))
CHIP_BARRIER = 1
ALL_BARRIER = 2


def _position():
    return lax.axis_index("x"), lax.axis_index("y"), lax.axis_index("c")


def _chip_peers(pos):
    x, y, c = pos
    return [(x ^ fx, y ^ fy, c) for fx, fy in CHIP_FLIPS]


class _Comm:
    def __init__(self, ins, out_shape, plan, n_remote, n_local, collective_id):
        self.ins, self.out_shape, self.plan = list(ins), list(out_shape), plan
        self.n_remote, self.n_local, self.collective_id = n_remote, n_local, collective_id
        self.result = None

    def scratch(self):
        return [pltpu.SemaphoreType.DMA((self.n_remote,)), pltpu.SemaphoreType.DMA((self.n_remote,)),
                pltpu.SemaphoreType.DMA((self.n_local,))]

    def _copies(self, in_refs, out_refs, sems):
        send_sems, recv_sems, local_sems = sems
        pos = _position()
        p = self.plan(pos, in_refs, out_refs)

        def remote(k, src, dst, dev):
            return pltpu.make_async_remote_copy(src_ref=src, dst_ref=dst, send_sem=send_sems.at[k],
                                                recv_sem=recv_sems.at[k], device_id=dev, device_id_type=MESH)

        local = [pltpu.make_async_copy(s, d, local_sems.at[i]) for i, (s, d) in enumerate(p["local"])]
        out = [remote(k, s, d, dev) for k, (s, d, dev) in enumerate(p["sends"])]
        arrivals = [functools.partial(remote, k, d, d, pos) for k, d in enumerate(p["recvs"])]
        return local, out, arrivals, p["peers"]

    def start(self, in_refs, out_refs, sems):
        local, out, _, peers = self._copies(in_refs, out_refs, sems)
        barrier = pltpu.get_barrier_semaphore()
        for peer in peers:
            pl.semaphore_signal(barrier, inc=1, device_id=peer, device_id_type=MESH)
        pl.semaphore_wait(barrier, len(peers))
        for cp in local + out:
            cp.start()

    def finish(self, in_refs, out_refs, sems):
        local, out, arrivals, _ = self._copies(in_refs, out_refs, sems)
        for make in arrivals:
            make().wait_recv()
        for cp in out:
            cp.wait_send()
        for cp in local:
            cp.wait()

    def run(self, name):
        n_in, n_out = len(self.ins), len(self.out_shape)

        def body(*refs):
            comm = (refs[:n_in], refs[n_in:n_in + n_out], refs[n_in + n_out:])
            self.start(*comm)
            self.finish(*comm)

        hbm = pl.BlockSpec(memory_space=pl.ANY)
        self.result = pl.pallas_call(
            body, name=name, in_specs=[hbm] * n_in, out_specs=[hbm] * n_out, out_shape=self.out_shape,
            scratch_shapes=self.scratch(),
            compiler_params=pltpu.CompilerParams(collective_id=self.collective_id))(*self.ins)
        return self.result


def _half_rows(c, rows):
    r2 = rows // 2
    return pl.ds(pl.multiple_of(c * r2, math.gcd(r2, LANES)), r2)


def _half(ref, c, rows, lead=()):
    return ref.at[(*lead, _half_rows(c, rows))]


def gather_halves(shards):
    def plan(pos, ins, outs):
        x, y, c = pos
        me = 2 * x + y
        p = dict(local=[], sends=[], recvs=[], peers=_chip_peers(pos))
        for a, (src, dst) in enumerate(zip(ins, outs)):
            rows = shards[a].shape[0]
            p["local"].append((src, dst.at[me]))
            for px, py, _ in p["peers"]:
                p["sends"].append((_half(src, c, rows), _half(dst, c, rows, (me,)), (px, py, c)))
                p["recvs"].append(_half(dst, c, rows, (2 * px + py,)))
        return p

    return _Comm(shards, [_sds((N_CHIPS, *s.shape), s.dtype) for s in shards], plan,
                 n_remote=3 * len(shards), n_local=len(shards), collective_id=CHIP_BARRIER)


def chip_exchange(parts):
    def plan(pos, ins, outs):
        x, y, c = pos
        me = 2 * x + y
        p = dict(local=[], sends=[], recvs=[], peers=_chip_peers(pos))
        for src, dst in zip(ins, outs):
            p["local"].append((src.at[me], dst.at[me]))
            for px, py, _ in p["peers"]:
                peer = 2 * px + py
                p["sends"].append((src.at[peer], dst.at[me], (px, py, c)))
                p["recvs"].append(dst.at[peer])
        return p

    return _Comm(parts, [_sds(g.shape, g.dtype) for g in parts], plan, n_remote=3 * len(parts), n_local=len(parts),
                 collective_id=CHIP_BARRIER)


def all_gather_devices(v):
    flips = [(fx, fy, fc) for fx in (0, 1) for fy in (0, 1) for fc in (0, 1)][1:]

    def plan(pos, ins, outs):
        x, y, c = pos
        me = 4 * x + 2 * y + c
        p = dict(local=[(ins[0], outs[0].at[me])], sends=[], recvs=[],
                 peers=[(x ^ fx, y ^ fy, c ^ fc) for fx, fy, fc in flips])
        for px, py, pc in p["peers"]:
            p["sends"].append((ins[0], outs[0].at[me], (px, py, pc)))
            p["recvs"].append(outs[0].at[4 * px + 2 * py + pc])
        return p

    return _Comm([v], [_sds((8, *v.shape), v.dtype)], plan, n_remote=7, n_local=1,
                 collective_id=ALL_BARRIER).run("small_all_gather")[0]


SWAP_CHUNK_BYTES = 3 * 2 ** 19


def _sibling_stream(t, n, value, consume, sbuf, rbuf, send_sems, recv_sems, credits):
    x, y, c = _position()
    sib = (x, y, 1 - c)

    def copy(slot):
        return pltpu.make_async_remote_copy(src_ref=sbuf.at[slot], dst_ref=rbuf.at[slot], send_sem=send_sems.at[slot],
                                            recv_sem=recv_sems.at[slot], device_id=sib, device_id_type=MESH)

    slot = t % 2

    @pl.when(t == 0)
    def _():
        barrier = pltpu.get_barrier_semaphore()
        pl.semaphore_signal(barrier, inc=1, device_id=sib, device_id_type=MESH)
        pl.semaphore_wait(barrier, 1)

    @pl.when(jnp.logical_and(t >= 2, t < n))
    def _():
        copy(slot).wait_send()
        pl.semaphore_wait(credits.at[slot], 1)

    @pl.when(t < n)
    def _():
        sbuf[slot] = value
        copy(slot).start()

    @pl.when(t >= 1)
    def _():
        prev = 1 - slot
        copy(prev).wait_recv()
        consume(sbuf[prev], rbuf[prev])

        @pl.when(t + 1 < n)
        def _():
            pl.semaphore_signal(credits.at[prev], inc=1, device_id=sib, device_id_type=MESH)

    @pl.when(t == n)
    def _():
        copy(1 - slot).wait_send()
        if n > 1:
            copy(slot).wait_send()


def _swap_scratch(rows, cols, dtype):
    return [pltpu.VMEM((2, rows, cols), dtype), pltpu.VMEM((2, rows, cols), dtype),
            pltpu.SemaphoreType.DMA((2,)), pltpu.SemaphoreType.DMA((2,)), pltpu.SemaphoreType.REGULAR((2,))]


def _chunk_rows(rows, cols, dtype, nbytes=SWAP_CHUNK_BYTES):
    return _tile(rows, max(16, nbytes // (cols * jnp.dtype(dtype).itemsize)), 16)


def pair_add(name, g):
    s, r, c_ = g.shape
    r2 = r // 2
    cr = _chunk_rows(r2, c_, g.dtype)
    nj = r2 // cr

    n = s * nj

    def body(core, mine_ref, theirs_ref, o_ref, *scratch):
        def consume(_, got):
            o_ref[...] = (mine_ref[...].astype(F32) + got.astype(F32)).astype(o_ref.dtype)

        _sibling_stream(pl.program_id(0), n, theirs_ref[...], consume, *scratch)

    sent = lambda t: jnp.minimum(t, n - 1)
    used = lambda t: jnp.maximum(t - 1, 0)
    grid_spec = pltpu.PrefetchScalarGridSpec(
        num_scalar_prefetch=1, grid=(n + 1,),
        in_specs=[pl.BlockSpec((None, cr, c_), lambda t, core: (used(t) // nj, core[0] * nj + used(t) % nj, 0)),
                  pl.BlockSpec((None, cr, c_), lambda t, core: (sent(t) // nj, (1 - core[0]) * nj + sent(t) % nj, 0))],
        out_specs=pl.BlockSpec((None, cr, c_), lambda t, core: (used(t) // nj, used(t) % nj, 0)),
        scratch_shapes=_swap_scratch(cr, c_, g.dtype))
    core = lax.axis_index("c").astype(jnp.int32).reshape(1)
    return pl.pallas_call(body, name=name, grid_spec=grid_spec, out_shape=_sds((s, r2, c_), g.dtype),
                          compiler_params=_params("arbitrary", collective_id=SIBLING_BARRIER))(core, g, g)


def _adamw_math(w, g, m, v):
    m = ADAM_B1 * m + (1.0 - ADAM_B1) * g
    v = ADAM_B2 * v + (1.0 - ADAM_B2) * (g * g)
    m_hat = m / (1.0 - ADAM_B1 ** ADAM_STEP)
    v_hat = v / (1.0 - ADAM_B2 ** ADAM_STEP)
    return -ADAM_LR * (m_hat / (jnp.sqrt(v_hat) + ADAM_EPS) + ADAM_WD * w), m, v


def reduce_join(name, p):
    s, r2, c_ = p.shape
    cr = _chunk_rows(r2, c_, F32)
    nj = r2 // cr

    def body(core, p_ref, o_ref, *scratch):
        acc = p_ref[0].astype(F32)
        for i in range(1, s):
            acc = acc + p_ref[i].astype(F32)

        def consume(own, got):
            c = core[0]
            o_ref[c] = own
            o_ref[1 - c] = got

        _sibling_stream(pl.program_id(0), nj, acc, consume, *scratch)

    grid_spec = pltpu.PrefetchScalarGridSpec(
        num_scalar_prefetch=1, grid=(nj + 1,),
        in_specs=[pl.BlockSpec((s, cr, c_), lambda t, core: (0, jnp.minimum(t, nj - 1), 0))],
        out_specs=pl.BlockSpec((2, cr, c_), lambda t, core: (0, jnp.maximum(t - 1, 0), 0)),
        scratch_shapes=_swap_scratch(cr, c_, F32))
    core = lax.axis_index("c").astype(jnp.int32).reshape(1)
    out = pl.pallas_call(body, name=name, grid_spec=grid_spec, out_shape=_sds((2, r2, c_), F32),
                         compiler_params=_params("arbitrary", collective_id=SIBLING_BARRIER))(core, p)
    return out.reshape(2 * r2, c_)


def sibling_fill(name, buf):
    s, r, c_ = buf.shape
    r2 = r // 2
    cr = _chunk_rows(r2, c_, buf.dtype)
    nj = r2 // cr
    n_peers = len(CHIP_FLIPS)

    n = n_peers * nj

    def body(where, in_ref, o_ref, *scratch):
        def consume(_, got):
            o_ref[...] = got

        _sibling_stream(pl.program_id(0), n, in_ref[...], consume, *scratch)

    sent = lambda t: jnp.minimum(t, n - 1)
    used = lambda t: jnp.maximum(t - 1, 0)
    grid_spec = pltpu.PrefetchScalarGridSpec(
        num_scalar_prefetch=1, grid=(n + 1,),
        in_specs=[pl.BlockSpec((None, cr, c_),
                               lambda t, where: (where[sent(t) // nj], where[n_peers] * nj + sent(t) % nj, 0))],
        out_specs=pl.BlockSpec((None, cr, c_),
                               lambda t, where: (where[used(t) // nj], (1 - where[n_peers]) * nj + used(t) % nj, 0)),
        scratch_shapes=_swap_scratch(cr, c_, buf.dtype))
    x, y, c = _position()
    where = jnp.stack([2 * (x ^ fx) + (y ^ fy) for fx, fy in CHIP_FLIPS] + [c]).astype(jnp.int32)
    return pl.pallas_call(body, name=name, grid_spec=grid_spec, out_shape=_sds(buf.shape, buf.dtype),
                          input_output_aliases={1: 0},
                          compiler_params=_params("arbitrary", collective_id=SIBLING_BARRIER))(where, buf)


def sum_slots(name, p, out_dtype):
    s, r, c = p.shape
    tr = _tile(r, 256, 16)

    def body(p_ref, o_ref):
        acc = p_ref[0].astype(F32)
        for i in range(1, s):
            acc = acc + p_ref[i].astype(F32)
        o_ref[...] = acc.astype(o_ref.dtype)

    return pl.pallas_call(
        body, name=name, grid=(r // tr,), in_specs=[pl.BlockSpec((s, tr, c), lambda i: (0, i, 0))],
        out_specs=pl.BlockSpec((tr, c), lambda i: (i, 0)), out_shape=_sds((r, c), out_dtype),
        compiler_params=_params("arbitrary"),
    )(p)


def adamw(name, w, g, m, v):
    r, c = w.shape
    outs, _ = _rowwise(name, lambda _, w, g, m, v: ([*_adamw_math(w, g, m, v), g], []), [w, g, m, v], [],
                       [(c, F32)] * 4, [], _tile(r, 256, 8))
    return outs


RET_SCALE = HEAD_DIM ** -0.5
MLA_SCALE = (HEAD_DIM + MLA_ROPE) ** -0.5
GRAD_DT = BF16
IN_RET = 4 * HEADS * HEAD_DIM
IN_MLA = MLA_Q_RANK + MLA_KV_RANK + MLA_ROPE
IN_MLA_PAD = IN_MLA + 64


def _heads(fn):
    return jnp.concatenate([fn(h) for h in range(HEADS)], axis=1)


def _head(a, h, stride=HEAD_DIM, off=0):
    return a[:, h * stride + off:h * stride + off + HEAD_DIM]


def _group_norm(o):
    rs = [lax.rsqrt(jnp.mean(_head(o, h) * _head(o, h), axis=-1, keepdims=True) + EPS) for h in range(HEADS)]
    return _heads(lambda h: _head(o, h) * rs[h]), rs


class _Alone:
    def host(self, kernel_name):
        return None

    def done(self, kernel_name, w):
        pass

    def grads(self, g):
        pass


def _ffn_fwd(tag, n, w, k, tm, tmk, plan):
    gate, up, down = tag + "_gate", tag + "_up", tag + "_down"
    if "wu" + k in w:
        g, u, a = mm_nn(up, n, w["wg" + k], tm, outs=[BF16] * 3, w2=w["wu" + k],
                        epilogue=lambda g, u: (g, u, _silu(g) * u), host=plan.host(up))
    else:
        (g,) = mm_nn(gate, n, w["wg" + k], tm, out_dtype=BF16, host=plan.host(gate))
        plan.done(gate, w)
        u, a = mm_nn(up, n, w["wu" + k], tm, outs=[BF16] * 2, extras=(g,),
                     epilogue=lambda u, g: (u, _silu(g.astype(F32)) * u), host=plan.host(up))
    plan.done(up, w)
    ff = a.shape[1]
    wd = w["wd" + k]
    f = mm_nn_k(down, a, wd.reshape(ff, wd.shape[2]), tmk, _tile(ff, 1408), host=plan.host(down))
    plan.done(down, w)
    return g, u, a, f


def _ffn_bwd(tag, df, n, g, u, a, wg, wu, wd, tm, tmk, tk, dt, plan):
    ns = wg.shape[2]

    def gate_grads(da, g, u):
        g, u = g.astype(F32), u.astype(F32)
        return da * u * _dsilu(g), da * _silu(g)

    def hosted(kernel_name, call):
        out = call(plan.host(kernel_name))
        plan.done(kernel_name, None)
        return out

    k = tag[-1]
    dg, du = hosted(tag + "_da", lambda h: mm_nt(tag + "_da", [df], [wd], tm, ns, outs=(BF16, BF16),
                                                 epilogue=gate_grads, extras=(g, u), host=h))
    dwg = hosted(tag + "_dwg", lambda h: mm_tn(tag + "_dwg", n, dg, dt, ns, tk, GRAD_DT, shard_cols=True, host=h))
    plan.grads({"wg" + k: dwg})
    dwu = hosted(tag + "_dwu", lambda h: mm_tn(tag + "_dwu", n, du, dt, ns, tk, GRAD_DT, shard_cols=True, host=h))
    plan.grads({"wu" + k: dwu})
    dwd = hosted(tag + "_dwd", lambda h: mm_tn(tag + "_dwd", a, df, ns, dt, tk, GRAD_DT, shard_rows=True, host=h))
    plan.grads({"wd" + k: dwd})
    dn = hosted(tag + "_dn", lambda h: mm_nt_k(tag + "_dn", [dg, du], [wg, wu], tmk, dt, out_dtype=BF16, host=h))
    return dn


def _local_step(x, tgt, meta, w, nw, plan):
    seq, d = x.shape
    t_real = N_META + seq
    tp = -(-t_real // LANES) * LANES
    zpad = jnp.zeros((tp - t_real, d), F32)
    h0 = jnp.concatenate([meta, x, zpad], axis=0)
    tgt_p = jnp.concatenate([jnp.zeros((N_META, d), F32), tgt, zpad], axis=0)
    cos_r, sin_r, cos_m, sa, sb = _rope_tables(tp)
    tm = _tile(tp, 512)
    tmt = _tile(tp, 768, 16)
    tmk = _tile(tp, 1408)
    tk = tp
    blk = tm
    dt = _tile(d, 1024)
    hw = HEADS * HEAD_DIM
    qw = HEADS * MLA_QK_PAD

    (n1,), _ = _rowwise("ffn1_norm", lambda r0, h, g: ([_rms(h, g)], []), [h0], [nw["ffn1_pre_norm"]],
                        [(d, BF16)], [], tm)
    g1, u1, a1, f1 = _ffn_fwd("ffn1", n1, w, "1", tm, tmk, plan)

    def post_ffn1(r0, h, f, post, pre):
        h1 = h + 0.5 * _rms(f, post)
        return [h1, _rms(h1, pre)], []

    (h1, un), _ = _rowwise("mix_norm", post_ffn1, [h0, f1], [nw["ffn1_post_norm"], nw["mix_pre_norm"]],
                           [(d, F32), (d, BF16)], [], tm)
    (proj_r,) = mm_nn("proj_r", un, w["w_r"], tm, n_block=_tile(IN_RET, 1024), host=plan.host("proj_r"))
    plan.done("proj_r", w)
    (proj_c,) = mm_nn("proj_c", un, w["w_c"], tm)

    def split_proj(r0, pr, pc, cr, sr, cm, ta, tb, qn, kvn):
        rq = _heads(lambda h: _rope_r(_head(pr, h), cr, sr))
        rk = _heads(lambda h: _rope_r(_head(pr, h, off=hw), cr, sr) * RET_SCALE)
        rv = pr[:, 2 * hw:3 * hw]
        cqn = _rms(pc[:, :MLA_Q_RANK], qn)
        ckvn = _rms(pc[:, MLA_Q_RANK:MLA_Q_RANK + MLA_KV_RANK], kvn)
        krr = _rope_m(pc[:, MLA_Q_RANK + MLA_KV_RANK:], cm, ta, tb)
        return [rq, rk, rv, cqn, ckvn, krr], []

    (rq, rk, rv, cqn, ckvn, krr), _ = _rowwise(
        "split_proj", split_proj, [proj_r, proj_c, cos_r, sin_r, cos_m, sa, sb],
        [nw["mla_q_norm"], nw["mla_kv_norm"]],
        [(hw, BF16), (hw, BF16), (hw, BF16), (MLA_Q_RANK, BF16), (MLA_KV_RANK, BF16), (LANES, F32)], [], tm)
    (qp,) = mm_nn("q_up", cqn, w["wuq"], tm)
    (kn,) = mm_nn("k_up", ckvn, w["wuk"], tm)
    (vv,) = mm_nn("v_up", ckvn, w["wuv"], tm, out_dtype=BF16)

    def build_qk(r0, qp, kn, krr, cm, ta, tb):
        qc = jnp.concatenate(
            [part for h in range(HEADS)
             for part in (_head(qp, h, MLA_QK_PAD), _rope_m(_head(qp, h, MLA_QK_PAD, HEAD_DIM), cm, ta, tb))], axis=1)
        kc = jnp.concatenate([part for h in range(HEADS) for part in (_head(kn, h), krr)], axis=1)
        return [qc, kc], []

    (qc, kc), _ = _rowwise("build_qk", build_qk, [qp, kn, krr, cos_m, sa, sb], [], [(qw, BF16), (qw, BF16)], [], tm)
    o_m, lse = attn_fwd("mla_fwd", qc, kc, vv, blk, MLA_SCALE, host=plan.host("mla_fwd"))
    plan.done("mla_fwd", w)
    o_r, ret_states = ret_fwd("ret_fwd", rq, rk, rv, blk, host=plan.host("ret_fwd"))
    plan.done("ret_fwd", w)

    def gate_mix(r0, rg, o_r, o_m, gn):
        y, _ = _group_norm(o_r)
        return [jnp.concatenate([_silu(rg) * (y * gn), o_m], axis=1)], []

    (mixcat,), _ = _rowwise("gate_mix", gate_mix, [(proj_r, hw, 3), o_r, o_m], [nw["ret_group_norm"]],
                            [(2 * hw, BF16)], [], tm)
    (mix,) = mm_nn("mix_out", mixcat, w["w_out"], tm, n_block=dt)

    def post_mix(r0, h, m, post, pre):
        h2 = h + _rms(m, post)
        return [h2, _rms(h2, pre)], []

    (h2, n3), _ = _rowwise("ffn2_norm", post_mix, [h1, mix], [nw["mix_post_norm"], nw["ffn2_pre_norm"]],
                           [(d, F32), (d, BF16)], [], tm)
    g2, u2, a2, f2 = _ffn_fwd("ffn2", n3, w, "2", tm, tmk, plan)

    def loss_head(r0, h, f, t, post):
        h3 = h + 0.5 * _rms(f, post)
        row = r0 + lax.broadcasted_iota(jnp.int32, h3.shape, 0)
        err = jnp.where(row >= N_META, jnp.where(row < t_real, h3 - t, 0.0), 0.0)
        dh3 = err / d
        df, dpost = _rms_bwd(f, post, 0.5 * dh3)
        return [dh3, df], [_colsum(err * err), _colsum(dpost)]

    (dh3, df2), (loss_vec, d_post2) = _rowwise("loss_head", loss_head, [h2, f2, tgt_p], [nw["ffn2_post_norm"]],
                                               [(d, F32), (d, BF16)], [d, d], tm)
    loss = 0.5 * jnp.sum(loss_vec) / d
    dn3 = _ffn_bwd("ffn2", df2, n3, g2, u2, a2, w["wg2"], w["wu2"], w["wd2"], tmt, tmk, tk, dt, plan)

    def back_mix_norm(r0, h, m, dh3, dn, pre, post):
        dx, dpre = _rms_bwd(h, pre, dn)
        dh2 = dh3 + dx
        dm, dpost = _rms_bwd(m, post, dh2)
        return [dh2, dm], [_colsum(dpre), _colsum(dpost)]

    (dh2, dmix), (d_pre2, d_mix_post) = _rowwise(
        "back_mix_norm", back_mix_norm, [h2, mix, dh3, dn3], [nw["ffn2_pre_norm"], nw["mix_post_norm"]],
        [(d, F32), (d, BF16)], [d, d], tm)
    (dmixcat,) = mm_nt("mix_dx", [dmix], [w["w_out"]], tmt, _tile(2 * hw, 512), outs=(BF16,))
    plan.grads(dict(w_out=mm_tn("mix_dw", mixcat, dmix, 2 * hw // N_CHIPS, dt, tk, GRAD_DT, shard_rows=True)))

    def back_gate(r0, dmc, rg, o_r, gn):
        d_ret, d_om = dmc[:, :hw], dmc[:, hw:]
        yh, rs = _group_norm(o_r)
        d_rg = d_ret * (yh * gn) * _dsilu(rg)
        dy = d_ret * _silu(rg)
        gyh = dy * gn
        d_or = _heads(lambda h: rs[h] * (_head(gyh, h) - _head(yh, h) * jnp.mean(_head(gyh, h) * _head(yh, h),
                                                                                    axis=-1, keepdims=True)))
        return [d_or, d_rg, d_om], [_colsum(dy * yh)]

    (d_or, d_rg, d_om), (d_gn,) = _rowwise("back_gate", back_gate, [dmixcat, (proj_r, hw, 3), o_r],
                                           [nw["ret_group_norm"]], [(hw, BF16), (hw, F32), (hw, BF16)], [hw], tm)
    dqc, dkc, dvv = attn_bwd("mla_bwd", qc, kc, vv, d_om, o_m, lse, blk, MLA_SCALE, host=plan.host("mla_bwd"))
    plan.done("mla_bwd", None)
    drq, drk, drv = ret_bwd("ret_bwd", rq, rk, rv, d_or, ret_states, blk)

    def back_qk(r0, dqc, dkc, dvv, cm, ta, tb):
        dqp = jnp.concatenate(
            [part for h in range(HEADS)
             for part in (_head(dqc, h, MLA_QK_PAD), _rope_m_t(_head(dqc, h, MLA_QK_PAD, HEAD_DIM), cm, ta, tb))],
            axis=1)
        dkn = _heads(lambda h: _head(dkc, h, MLA_QK_PAD))
        dkr = _head(dkc, 0, MLA_QK_PAD, HEAD_DIM)
        for h in range(1, HEADS):
            dkr = dkr + _head(dkc, h, MLA_QK_PAD, HEAD_DIM)
        return [dqp, dkn, _rope_m_t(dkr, cm, ta, tb), dvv], []

    (dqp, dkn, dkr, dvb), _ = _rowwise("back_qk", back_qk, [dqc, dkc, dvv, cos_m, sa, sb], [],
                                       [(qw, BF16), (hw, BF16), (LANES, F32), (hw, BF16)], [], tm)
    (dcqn,) = mm_nt("q_dx", [dqp], [w["wuq"]], tm, MLA_Q_RANK)
    dwuq = mm_tn("q_dw", cqn, dqp, MLA_Q_RANK, _tile(qw, 1024), tk, GRAD_DT)
    (dckvn,) = mm_nt("kv_dx", [dkn, dvb], [w["wuk"], w["wuv"]], tm, MLA_KV_RANK)
    dwuk = mm_tn("k_dw", ckvn, dkn, MLA_KV_RANK, hw, tk, GRAD_DT)
    dwuv = mm_tn("v_dw", ckvn, dvb, MLA_KV_RANK, hw, tk, GRAD_DT)

    def back_proj(r0, drq, drk, drv, d_rg, pc, dcqn, dckvn, dkr, cr, sr, qn, kvn):
        d_q = _heads(lambda h: _rope_r_t(_head(drq, h), cr, sr))
        d_k = _heads(lambda h: _rope_r_t(_head(drk, h), cr, sr) * RET_SCALE)
        dcq, a_q = _rms_bwd(pc[:, :MLA_Q_RANK], qn, dcqn)
        dckv, a_kv = _rms_bwd(pc[:, MLA_Q_RANK:MLA_Q_RANK + MLA_KV_RANK], kvn, dckvn)
        return ([jnp.concatenate([d_q, d_k, drv, d_rg], axis=1), jnp.concatenate([dcq, dckv, dkr], axis=1)],
                [_colsum(a_q), _colsum(a_kv)])

    (dproj_r, dproj_c), (d_qn, d_kvn) = _rowwise(
        "back_proj", back_proj, [drq, drk, drv, d_rg, proj_c, dcqn, dckvn, dkr, cos_r, sin_r],
        [nw["mla_q_norm"], nw["mla_kv_norm"]], [(IN_RET, BF16), (IN_MLA_PAD, BF16)],
        [MLA_Q_RANK, MLA_KV_RANK], tm)
    (dun,) = mm_nt("proj_dx", [dproj_r, dproj_c], [w["w_r"], w["w_c"]], tmt, _tile(d, 512), outs=(BF16,))
    dw_r = mm_tn("proj_dw_r", un, dproj_r, dt, _tile(IN_RET, 1024), tk, GRAD_DT)
    dw_c = mm_tn("proj_dw_c", un, dproj_c, dt, IN_MLA_PAD, tk, GRAD_DT)
    plan.grads(dict(w_r=dw_r, w_c=dw_c, wuq=dwuq, wuk=dwuk, wuv=dwuv))

    def back_ffn1_norm(r0, h, f, dh2, dn, pre, post):
        dx, dpre = _rms_bwd(h, pre, dn)
        dh1 = dh2 + dx
        df, dpost = _rms_bwd(f, post, 0.5 * dh1)
        return [dh1, df], [_colsum(dpre), _colsum(dpost)]

    (dh1, df1), (d_mix_pre, d_post1) = _rowwise(
        "back_ffn1_norm", back_ffn1_norm, [h1, f1, dh2, dun], [nw["mix_pre_norm"], nw["ffn1_post_norm"]],
        [(d, F32), (d, BF16)], [d, d], tm)
    dn1 = _ffn_bwd("ffn1", df1, n1, g1, u1, a1, w["wg1"], w["wu1"], w["wd1"], tmt, tmk, tk, dt, plan)

    def back_input(r0, h, dh1, dn, pre):
        dx, dpre = _rms_bwd(h, pre, dn)
        return [dh1 + dx], [_colsum(dpre)]

    (dh0,), (d_pre1,) = _rowwise("back_input", back_input, [h0, dh1, dn1], [nw["ffn1_pre_norm"]], [(d, F32)], [d], tm)

    small = dict(ffn1_pre_norm=d_pre1, ffn1_post_norm=d_post1, mix_pre_norm=d_mix_pre, ret_group_norm=d_gn,
                 mla_q_norm=d_qn, mla_kv_norm=d_kvn, mix_post_norm=d_mix_post, ffn2_pre_norm=d_pre2,
                 ffn2_post_norm=d_post2)
    return loss, dh0[N_META:t_real], small, dh0[:N_META]


WEIGHTS = ("meta_tokens", "ffn1_pre_norm", "ffn1_w_gate", "ffn1_w_up", "ffn1_w_down", "ffn1_post_norm",
           "mix_pre_norm", "w_in", "ret_group_norm", "mla_q_norm", "mla_w_uq", "mla_kv_norm", "mla_w_uk",
           "mla_w_uv", "w_out", "mix_post_norm", "ffn2_pre_norm", "ffn2_w_gate", "ffn2_w_up", "ffn2_w_down",
           "ffn2_post_norm")
BIG = ("ffn1_w_gate", "ffn1_w_up", "ffn1_w_down", "w_in", "mla_w_uq", "mla_w_uk", "mla_w_uv", "w_out",
       "ffn2_w_gate", "ffn2_w_up", "ffn2_w_down")
NORMS = ("ffn1_pre_norm", "ffn1_post_norm", "mix_pre_norm", "ret_group_norm", "mla_q_norm", "mla_kv_norm",
         "mix_post_norm", "ffn2_pre_norm", "ffn2_post_norm")


def _unshard_cols(g):
    return g.transpose(1, 0, 2).reshape(g.shape[1], -1)


def _shard_cols(a):
    return a.reshape(a.shape[0], N_CHIPS, -1).transpose(1, 0, 2)


def _pack_rows(rows, width):
    rows = [jnp.pad(r, ((0, 0), (0, width - r.shape[1]))) for r in rows]
    n = sum(r.shape[0] for r in rows)
    return jnp.pad(jnp.concatenate(rows, axis=0), ((0, -n % 8), (0, 0)))


def _weight_views(full):
    w = {}
    for n, g in full.items():
        if n == "w_in":
            w_in = _unshard_cols(g)
            w["w_r"] = w_in[:, :IN_RET]
            w["w_c"] = jnp.pad(w_in[:, IN_RET:], ((0, 0), (0, IN_MLA_PAD - IN_MLA)))
        elif n == "mla_w_uq":
            q = _unshard_cols(g).reshape(MLA_Q_RANK, HEADS, HEAD_DIM + MLA_ROPE)
            q = jnp.pad(q, ((0, 0), (0, 0), (0, MLA_QK_PAD - HEAD_DIM - MLA_ROPE)))
            w["wuq"] = q.reshape(MLA_Q_RANK, HEADS * MLA_QK_PAD)
        elif n in ("mla_w_uk", "mla_w_uv"):
            w["wu" + n[-1]] = _unshard_cols(g)
        elif n == "w_out":
            w["w_out"] = g.reshape(-1, g.shape[2])
        else:
            w["w" + n[7] + n[3]] = g
    return w


def _contributions(g):
    ffn = {"g": "gate", "u": "up", "d": "down"}
    c = {f"ffn{n[2]}_w_{ffn[n[1]]}": a for n, a in g.items() if len(n) == 3 and n[2] in "12"}
    if "w_out" in g:
        c["w_out"] = g["w_out"]
    if "w_r" in g:
        dwuq = g["wuq"].reshape(MLA_Q_RANK, HEADS, MLA_QK_PAD)[:, :, :HEAD_DIM + MLA_ROPE]
        c.update(w_in=_shard_cols(jnp.concatenate([g["w_r"], g["w_c"][:, :IN_MLA]], axis=1)),
                 mla_w_uq=_shard_cols(dwuq.reshape(MLA_Q_RANK, -1)), mla_w_uk=_shard_cols(g["wuk"]),
                 mla_w_uv=_shard_cols(g["wuv"]))
    return c


class _Schedule(_Alone):
    FIRST = ("ffn1_w_gate",)
    CARRIED = {"ffn1_gate": ("ffn1_w_up",), "ffn1_up": ("ffn1_w_down",),
               "ffn1_down": ("w_in", "mla_w_uq", "mla_w_uk", "mla_w_uv"),
               "proj_r": ("w_out",), "mla_fwd": ("ffn2_w_gate", "ffn2_w_up"), "ffn2_up": ("ffn2_w_down",)}
    GRAD_HOST = dict(ffn2_w_gate="mla_bwd", ffn2_w_up="mla_bwd", ffn2_w_down="mla_bwd",
                     w_out="mla_bwd", w_in="ffn1_da", mla_w_uq="ffn1_da", mla_w_uk="ffn1_da", mla_w_uv="ffn1_da",
                     ffn1_w_gate="ffn1_dwu", ffn1_w_up="ffn1_dwd", ffn1_w_down="ffn1_dn")

    def __init__(self, shards):
        self.gathers = {k: (gather_halves([shards[n] for n in names]), names) for k, names in self.CARRIED.items()}
        self.waiting = {}
        self.exchanges = {}
        self.grad = {}

    def host(self, kernel_name):
        if kernel_name in self.gathers:
            return self.gathers[kernel_name][0]
        if kernel_name in self.waiting:
            names, sums = zip(*self.waiting.pop(kernel_name))
            self.exchanges[kernel_name] = (chip_exchange(list(sums)), names)
            return self.exchanges[kernel_name][0]
        return None

    def done(self, kernel_name, w):
        if kernel_name in self.gathers:
            comm, names = self.gathers[kernel_name]
            w.update(_weight_views({n: sibling_fill("fill_" + n, b) for n, b in zip(names, comm.result)}))
        elif kernel_name in self.exchanges:
            comm, names = self.exchanges[kernel_name]
            for n, q in zip(names, comm.result):
                self.grad[n] = reduce_join("reduce_join_" + n, q)

    def grads(self, g):
        for n, a in _contributions(g).items():
            self.waiting.setdefault(self.GRAD_HOST[n], []).append((n, pair_add("pair_add_" + n, a)))


def _step(p, m, v, x, loss_target):
    d = x.shape[2]
    names = list(BIG)
    shards = {n: p[n][0].astype(BF16) for n in names}
    first = _Schedule.FIRST
    gathered = gather_halves([shards[n] for n in first] + [p["meta_tokens"]]).run("gather_first")
    filled = [sibling_fill("fill_" + n, b) for n, b in zip(first + ("meta_tokens",), gathered)]
    w = _weight_views(dict(zip(first, filled[:-1])))
    meta = _unshard_cols(filled[-1])
    nw = {n: p[n] for n in NORMS}
    plan = _Schedule(shards)
    loss, grad_x, small, d_meta = _local_step(x[0], loss_target[0], meta, w, nw, plan)
    grads = dict(plan.grad)

    width = max(d, HEADS * HEAD_DIM)
    packed = _pack_rows([small[n] for n in NORMS] + [d_meta], width)
    total = sum_slots("small_sum", all_gather_devices(packed), F32)
    for i, n in enumerate(NORMS):
        grads[n] = total[i:i + 1, :p[n].shape[1]]
    cols = p["meta_tokens"].shape[1]
    chip = 2 * lax.axis_index("x") + lax.axis_index("y")
    grads["meta_tokens"] = lax.dynamic_slice(total[len(NORMS):len(NORMS) + N_META, :d], (0, chip * cols), (N_META, cols))

    delta, new_m, new_v = {}, {}, {}
    for n in names + ["meta_tokens"]:
        shape = p[n].shape
        flat = lambda a: a.reshape(-1, shape[-1])
        out = adamw("adamw_" + n, flat(p[n]), flat(grads[n]), flat(m[n]), flat(v[n]))
        delta[n], new_m[n], new_v[n], grads[n] = (o.reshape(shape) for o in out)
    pk = lambda src: _pack_rows([src[n] for n in NORMS], width)
    out = adamw("adamw_norms", pk(p), pk(grads), pk(m), pk(v))
    for i, n in enumerate(NORMS):
        delta[n], new_m[n], new_v[n] = (o[i:i + 1, :p[n].shape[1]] for o in out[:3])

    loss = lax.psum(loss, ("x", "y", "c"))
    return loss, grad_x[None], grads, delta, new_m, new_v


def kernel(x, meta_tokens, ffn1_pre_norm, ffn1_w_gate, ffn1_w_up, ffn1_w_down, ffn1_post_norm, mix_pre_norm, w_in, ret_group_norm, mla_q_norm, mla_w_uq, mla_kv_norm, mla_w_uk, mla_w_uv, w_out, mix_post_norm, ffn2_pre_norm, ffn2_w_gate, ffn2_w_up, ffn2_w_down, ffn2_post_norm, loss_target, m_meta_tokens, m_ffn1_pre_norm, m_ffn1_w_gate, m_ffn1_w_up, m_ffn1_w_down, m_ffn1_post_norm, m_mix_pre_norm, m_w_in, m_ret_group_norm, m_mla_q_norm, m_mla_w_uq, m_mla_kv_norm, m_mla_w_uk, m_mla_w_uv, m_w_out, m_mix_post_norm, m_ffn2_pre_norm, m_ffn2_w_gate, m_ffn2_w_up, m_ffn2_w_down, m_ffn2_post_norm, v_meta_tokens, v_ffn1_pre_norm, v_ffn1_w_gate, v_ffn1_w_up, v_ffn1_w_down, v_ffn1_post_norm, v_mix_pre_norm, v_w_in, v_ret_group_norm, v_mla_q_norm, v_mla_w_uq, v_mla_kv_norm, v_mla_w_uk, v_mla_w_uv, v_w_out, v_mix_post_norm, v_ffn2_pre_norm, v_ffn2_w_gate, v_ffn2_w_up, v_ffn2_w_down, v_ffn2_post_norm):
    args = locals()
    p = {n: args[n] for n in WEIGHTS}
    m = {n: args["m_" + n] for n in WEIGHTS}
    v = {n: args["v_" + n] for n in WEIGHTS}
    loss, grad_x, grads, delta, new_m, new_v = _step(p, m, v, x, loss_target)
    return (loss, grad_x, *[grads[n] for n in WEIGHTS], *[delta[n] for n in WEIGHTS],
            *[new_m[n] for n in WEIGHTS], *[new_v[n] for n in WEIGHTS])
```

```python
import functools
import math

import jax
import jax.numpy as jnp
import numpy as np
from jax import lax
from jax.experimental import pallas as pl
from jax.experimental.pallas import tpu as pltpu

F32 = jnp.float32
BF16 = jnp.bfloat16

EPS = 1e-6
N_META = 16
HEADS = 8
HEAD_DIM = 128
MLA_ROPE = 64
MLA_QK_PAD = 256
MLA_Q_RANK = 512
MLA_KV_RANK = 256
ROPE_THETA = 10000.0
N_CHIPS = 4
LANES = 128
VMEM_LIMIT = 60 * 2 ** 20

ADAM_LR = 0.001
ADAM_B1 = 0.9
ADAM_B2 = 0.999
ADAM_EPS = 1e-08
ADAM_WD = 0.01
ADAM_STEP = 10

NN = (((1,), (0,)), ((), ()))
NT = (((1,), (1,)), ((), ()))
TN = (((0,), (0,)), ((), ()))
MESH = pl.DeviceIdType.MESH


def _tile(n, pref, align=LANES):
    if n <= pref:
        return n
    best = 0
    for t in range(align, pref + 1, align):
        if n % t == 0:
            best = t
    assert best, (n, pref)
    return best


def _params(*sem, collective_id=None):
    return pltpu.CompilerParams(dimension_semantics=sem, vmem_limit_bytes=VMEM_LIMIT, collective_id=collective_id)


SIBLING_BARRIER = 0


def _sds(shape, dtype):
    return jax.ShapeDtypeStruct(tuple(shape), dtype)


def _rowwise(name, fn, rows, consts, outs, accs, tr):
    rows = [r if isinstance(r, tuple) else (r, r.shape[1], 0) for r in rows]
    t = rows[0][0].shape[0]
    assert t % tr == 0
    n_r, n_c, n_o = len(rows), len(consts), len(outs)

    def body(*refs):
        i = pl.program_id(0)
        r = [x[...] for x in refs[:n_r]]
        c = [x[...] for x in refs[n_r:n_r + n_c]]
        o_refs = refs[n_r + n_c:n_r + n_c + n_o]
        a_refs = refs[n_r + n_c + n_o:]
        o_vals, a_vals = fn(i * tr, *r, *c)
        for ref, v in zip(o_refs, o_vals):
            ref[...] = v.astype(ref.dtype)
        if a_refs:
            @pl.when(i == 0)
            def _():
                for ref, v in zip(a_refs, a_vals):
                    ref[...] = v

            @pl.when(i > 0)
            def _():
                for ref, v in zip(a_refs, a_vals):
                    ref[...] += v

    in_specs = [pl.BlockSpec((tr, w), functools.partial(lambda cb, i: (i, cb), cb)) for _, w, cb in rows]
    in_specs += [pl.BlockSpec(a.shape, lambda i: (0, 0)) for a in consts]
    out_specs = [pl.BlockSpec((tr, w), lambda i: (i, 0)) for w, _ in outs]
    out_specs += [pl.BlockSpec((1, w), lambda i: (0, 0)) for w in accs]
    out_shape = [_sds((t, w), dt) for w, dt in outs] + [_sds((1, w), F32) for w in accs]
    res = pl.pallas_call(
        body, name=name, grid=(t // tr,), in_specs=in_specs, out_specs=out_specs, out_shape=out_shape,
        compiler_params=_params("arbitrary"),
    )(*[a for a, _, _ in rows], *consts)
    return res[:n_o], res[n_o:]


def _rms(x, w):
    r = lax.rsqrt(jnp.mean(x * x, axis=-1, keepdims=True) + EPS)
    return x * r * w


def _rms_bwd(x, w, dy):
    r = lax.rsqrt(jnp.mean(x * x, axis=-1, keepdims=True) + EPS)
    xh = x * r
    gy = dy * w
    dx = r * (gy - xh * jnp.mean(gy * xh, axis=-1, keepdims=True))
    return dx, dy * xh


def _colsum(v):
    return jnp.sum(v, axis=0, keepdims=True)


def _silu(x):
    return x * jax.nn.sigmoid(x)


def _dsilu(x):
    s = jax.nn.sigmoid(x)
    return s * (1.0 + x * (1.0 - s))


def _rope_r(x, cos, sin):
    return x * cos + pltpu.roll(x, 64, 1) * sin


def _rope_r_t(dy, cos, sin):
    return dy * cos + pltpu.roll(dy * sin, 64, 1)


def _rope_m(x, cos, sa, sb):
    return x * cos + pltpu.roll(x, 32, 1) * sa + pltpu.roll(x, 96, 1) * sb


def _rope_m_t(dy, cos, sa, sb):
    return dy * cos + pltpu.roll(dy * sa, 96, 1) + pltpu.roll(dy * sb, 32, 1)


def _rope_tables(t):
    pos = np.arange(t, dtype=np.float32)

    def cs(dim):
        inv = np.float32(ROPE_THETA) ** (-np.arange(0, dim, 2, dtype=np.float32) / np.float32(dim))
        ang = pos[:, None] * inv[None, :]
        return np.cos(ang), np.sin(ang)

    c, s = cs(HEAD_DIM)
    cos_r = np.concatenate([c, c], axis=1)
    sin_r = np.concatenate([-s, s], axis=1)
    c, s = cs(MLA_ROPE)
    z32, z64 = np.zeros_like(s), np.zeros((t, 64), np.float32)
    cos_m = np.concatenate([c, c, z64], axis=1)
    sa = np.concatenate([z32, s, z64], axis=1)
    sb = np.concatenate([-s, z32, z64], axis=1)
    return tuple(jnp.asarray(a, F32) for a in (cos_r, sin_r, cos_m, sa, sb))


def _call(body, name, grid, in_specs, out_specs, out_shape, scratch, operands, host=None):
    sem = ("arbitrary",) * len(grid)
    if host is None:
        return pl.pallas_call(body, name=name, grid=grid, in_specs=in_specs, out_specs=out_specs, out_shape=out_shape,
                              scratch_shapes=scratch, compiler_params=_params(*sem))(*operands)
    n_in, n_out, n_s = len(in_specs), len(out_shape), len(scratch)
    h_in, h_out = len(host.ins), len(host.out_shape)

    def hosted(*refs):
        a = n_in
        b = a + h_in
        c = b + n_out
        d = c + h_out
        e = d + n_s
        ids = [pl.program_id(i) for i in range(len(grid))]
        first = functools.reduce(jnp.logical_and, [i == 0 for i in ids])
        last = functools.reduce(jnp.logical_and, [i == g - 1 for i, g in zip(ids, grid)])
        comm = (refs[a:b], refs[c:d], refs[e:])

        @pl.when(first)
        def _():
            host.start(*comm)

        body(*refs[:a], *refs[b:c], *refs[d:e])

        @pl.when(last)
        def _():
            host.finish(*comm)

    hbm = pl.BlockSpec(memory_space=pl.ANY)
    res = pl.pallas_call(
        hosted, name=name, grid=grid, in_specs=list(in_specs) + [hbm] * h_in, out_specs=list(out_specs) + [hbm] * h_out,
        out_shape=list(out_shape) + list(host.out_shape), scratch_shapes=list(scratch) + host.scratch(),
        compiler_params=_params(*sem, collective_id=host.collective_id))(*operands, *host.ins)
    host.result = res[n_out:]
    return res[:n_out]


def _mm(name, grid, operands, in_specs, dns, out_specs, out_shape, epilogue=None, extras=(), extra_specs=(),
        acc_shape=None, host=None):
    n_p, n_e = len(dns), len(extras)
    nk = grid[2]
    n_o = len(out_shape)
    in_place = nk > 1 and epilogue is None and n_o == 1 and out_shape[0].dtype == F32

    def body(*refs):
        ab = refs[:2 * n_p]
        ex = refs[2 * n_p:2 * n_p + n_e]
        outs = refs[2 * n_p + n_e:2 * n_p + n_e + n_o]

        part = None
        for p in range(n_p):
            d = lax.dot_general(ab[2 * p][...], ab[2 * p + 1][...], dns[p], preferred_element_type=F32)
            part = d if part is None else part + d

        def finish(acc):
            vals = (acc,) if epilogue is None else epilogue(acc, *[e[...] for e in ex])
            for o, v in zip(outs, vals):
                o[...] = v.astype(o.dtype)

        if nk == 1:
            finish(part)
        else:
            acc_ref = outs[0] if in_place else refs[2 * n_p + n_e + n_o]
            k = pl.program_id(2)

            @pl.when(k == 0)
            def _():
                acc_ref[...] = part

            @pl.when(k > 0)
            def _():
                acc_ref[...] += part

            if not in_place:
                @pl.when(k == nk - 1)
                def _():
                    finish(acc_ref[...])

    scratch = [] if nk == 1 or in_place else [pltpu.VMEM(acc_shape, F32)]
    return _call(body, name, grid, list(in_specs) + list(extra_specs), out_specs, out_shape, scratch,
                 [*operands, *extras], host)


def mm_nn(name, x, w, tm, out_dtype=F32, epilogue=None, outs=None, w2=None, n_block=None, extras=(), host=None):
    t, kdim = x.shape
    if w.ndim == 3:
        s, _, ns = w.shape
        n, tn, nb = s * ns, ns, s
        wspec = pl.BlockSpec((None, kdim, ns), lambda j, i, k: (j, 0, 0))
    else:
        n = w.shape[1]
        tn = n_block or n
        nb = n // tn
        wspec = pl.BlockSpec((kdim, tn), lambda j, i, k: (0, j))
    xspec = pl.BlockSpec((tm, kdim), lambda j, i, k: (i, 0))
    ospec = pl.BlockSpec((tm, tn), lambda j, i, k: (i, j))
    outs = outs or [out_dtype]
    grid = (nb, t // tm, 1)
    if w2 is None:
        return _mm(name, grid, [x, w], [xspec, wspec], [NN], [ospec] * len(outs), [_sds((t, n), d) for d in outs],
                   epilogue=epilogue, extras=extras, extra_specs=[ospec] * len(extras), host=host)

    def body(x_ref, w_ref, w2_ref, *o_refs):
        xv = x_ref[...]
        a = jnp.dot(xv, w_ref[...], preferred_element_type=F32)
        b = jnp.dot(xv, w2_ref[...], preferred_element_type=F32)
        for o, v in zip(o_refs, epilogue(a, b)):
            o[...] = v.astype(o.dtype)

    return _call(body, name, grid[:2],
                 [pl.BlockSpec((tm, kdim), lambda j, i: (i, 0)),
                  pl.BlockSpec((None, kdim, tn), lambda j, i: (j, 0, 0)),
                  pl.BlockSpec((None, kdim, tn), lambda j, i: (j, 0, 0))],
                 [pl.BlockSpec((tm, tn), lambda j, i: (i, j))] * len(outs), [_sds((t, n), d) for d in outs], [],
                 [x, w, w2], host)


def mm_nn_k(name, x, w, tm, tk, out_dtype=F32, host=None):
    t, kdim = x.shape
    n = w.shape[1]
    grid = (t // tm, 1, kdim // tk)
    return _mm(name, grid, [x, w],
               [pl.BlockSpec((tm, tk), lambda i, j, k: (i, k)), pl.BlockSpec((tk, n), lambda i, j, k: (k, 0))],
               [NN], [pl.BlockSpec((tm, n), lambda i, j, k: (i, 0))], [_sds((t, n), out_dtype)],
               acc_shape=(tm, n), host=host)[0]


def mm_nt(name, xs, ws, tm, tn, outs=(F32,), epilogue=None, extras=(), host=None):
    t = xs[0].shape[0]
    specs, ops = [], []
    for x, w in zip(xs, ws):
        kdim = x.shape[1]
        specs.append(pl.BlockSpec((tm, kdim), lambda j, i, k: (i, 0)))
        if w.ndim == 3:
            assert tn == w.shape[1]
            n = w.shape[0] * w.shape[1]
            specs.append(pl.BlockSpec((None, tn, kdim), lambda j, i, k: (j, 0, 0)))
        else:
            n = w.shape[0]
            specs.append(pl.BlockSpec((tn, kdim), lambda j, i, k: (j, 0)))
        ops += [x, w]
    ospec = pl.BlockSpec((tm, tn), lambda j, i, k: (i, j))
    return _mm(name, (n // tn, t // tm, 1), ops, specs, [NT] * len(xs), [ospec] * len(outs),
               [_sds((t, n), d) for d in outs], epilogue=epilogue, extras=extras,
               extra_specs=[ospec] * len(extras), host=host)


def mm_nt_k(name, xs, ws, tm, tn, out_dtype=F32, host=None):
    t = xs[0].shape[0]
    s, n, ns = ws[0].shape
    specs, ops = [], []
    for x, w in zip(xs, ws):
        specs.append(pl.BlockSpec((tm, ns), lambda i, j, k: (i, k)))
        specs.append(pl.BlockSpec((None, tn, ns), lambda i, j, k: (k, j, 0)))
        ops += [x, w]
    return _mm(name, (t // tm, n // tn, s), ops, specs, [NT] * len(xs),
               [pl.BlockSpec((tm, tn), lambda i, j, k: (i, j))], [_sds((t, n), out_dtype)], acc_shape=(tm, tn),
               host=host)[0]


def mm_tn(name, x, y, tm, tn, tk, out_dtype, shard_rows=False, shard_cols=False, host=None):
    t, m = x.shape
    n = y.shape[1]
    grid = (m // tm, n // tn, t // tk)
    if shard_cols:
        ospec = pl.BlockSpec((None, tm, tn), lambda i, j, k: (j, i, 0))
        oshape = _sds((n // tn, m, tn), out_dtype)
    elif shard_rows:
        ospec = pl.BlockSpec((None, tm, tn), lambda i, j, k: (i, 0, j))
        oshape = _sds((m // tm, tm, n), out_dtype)
    else:
        ospec = pl.BlockSpec((tm, tn), lambda i, j, k: (i, j))
        oshape = _sds((m, n), out_dtype)
    return _mm(name, grid, [x, y],
               [pl.BlockSpec((tk, tm), lambda i, j, k: (k, i)), pl.BlockSpec((tk, tn), lambda i, j, k: (k, j))],
               [TN], [ospec], [oshape], acc_shape=(tm, tn), host=host)[0]


def _decay_logs():
    return [math.log(1.0 - 2.0 ** (-5.0 - h)) for h in range(HEADS)]


def _log_decay(h):
    lg = jnp.float32(_decay_logs()[0])
    for i in range(1, HEADS):
        lg = jnp.where(h == i, jnp.float32(_decay_logs()[i]), lg)
    return lg


def _decayed_scores(q, k, lg):
    s = lax.dot_general(q, k, NT, preferred_element_type=F32)
    row = lax.broadcasted_iota(jnp.int32, s.shape, 0)
    col = lax.broadcasted_iota(jnp.int32, s.shape, 1)
    dec = jnp.where(col <= row, jnp.exp(jnp.maximum(row - col, 0).astype(F32) * lg), 0.0)
    return s * dec, dec


def _causal(s):
    row = lax.broadcasted_iota(jnp.int32, s.shape, 0)
    col = lax.broadcasted_iota(jnp.int32, s.shape, 1)
    return jnp.where(col <= row, s, -1e30)


def attn_fwd(name, q, k, v, blk, scale, host=None):
    t = q.shape[0]
    dq = q.shape[1] // HEADS
    nq = t // blk

    def body(q_ref, k_ref, v_ref, o_ref, lse_ref, m_ref, l_ref, acc_ref):
        qi = pl.program_id(1)
        qv = q_ref[...]
        m_ref[...] = jnp.full_like(m_ref, -1e30)
        l_ref[...] = jnp.zeros_like(l_ref)
        acc_ref[...] = jnp.zeros_like(acc_ref)

        def keys(start, n, diag_from):
            rows = pl.ds(pl.multiple_of(start, blk), n)
            s = lax.dot_general(qv, k_ref[rows, :], NT, preferred_element_type=F32) * scale
            if diag_from is not None:
                row = lax.broadcasted_iota(jnp.int32, s.shape, 0)
                col = lax.broadcasted_iota(jnp.int32, s.shape, 1)
                s = jnp.where(col - diag_from <= row, s, -1e30)
            m = m_ref[...]
            m_new = jnp.maximum(m, jnp.max(s, axis=-1, keepdims=True))
            p = jnp.exp(s - m_new)
            alpha = jnp.exp(m - m_new)
            m_ref[...] = m_new
            l_ref[...] = alpha * l_ref[...] + jnp.sum(p, axis=-1, keepdims=True)
            acc_ref[...] = alpha * acc_ref[...] + jnp.dot(p.astype(BF16), v_ref[rows, :], preferred_element_type=F32)

        @pl.loop(0, qi // 2)
        def _(j):
            keys(j * (2 * blk), 2 * blk, None)

        @pl.when(qi % 2 == 1)
        def _():
            keys((qi - 1) * blk, 2 * blk, blk)

        @pl.when(qi % 2 == 0)
        def _():
            keys(qi * blk, blk, 0)

        l = l_ref[...]
        o_ref[...] = acc_ref[...] / l
        lse_ref[...] = jnp.broadcast_to(m_ref[...] + jnp.log(l), (blk, HEAD_DIM))

    hspec = pl.BlockSpec((blk, HEAD_DIM), lambda h, i: (i, h))
    return _call(body, name, (HEADS, nq),
                 [pl.BlockSpec((blk, dq), lambda h, i: (i, h)),
                  pl.BlockSpec((t, dq), lambda h, i: (0, h)),
                  pl.BlockSpec((t, HEAD_DIM), lambda h, i: (0, h))],
                 [hspec, hspec], [_sds((t, HEADS * HEAD_DIM), F32)] * 2,
                 [pltpu.VMEM((blk, 1), F32), pltpu.VMEM((blk, 1), F32), pltpu.VMEM((blk, HEAD_DIM), F32)],
                 [q, k, v], host)


def attn_bwd(name, q, k, v, do, o, lse, blk, scale, host=None):
    t = q.shape[0]
    dq_w = q.shape[1] // HEADS
    nb = t // blk

    def body(q_ref, k_ref, v_ref, do_ref, o_ref, lse_ref, dq_ref, dk_out, dv_out, dk_ref, dv_ref):
        ki = pl.program_id(1)
        kv = k_ref[...]
        vv = v_ref[...]

        @pl.when(ki == 0)
        def _():
            dq_ref[...] = jnp.zeros_like(dq_ref)

        def queries(start, n, diag):
            rows = pl.ds(pl.multiple_of(start, blk), n)
            qv, dov = q_ref[rows, :], do_ref[rows, :]
            s = lax.dot_general(qv, kv, NT, preferred_element_type=F32) * scale
            if diag:
                s = _causal(s)
            p = jnp.exp(s - lse_ref[rows, :][:, :1])
            dp = lax.dot_general(dov, vv, NT, preferred_element_type=F32)
            delta = jnp.sum(dov.astype(F32) * o_ref[rows, :], axis=-1, keepdims=True)
            ds = p * (dp - delta) * scale
            pb, dsb = p.astype(BF16), ds.astype(BF16)
            dv_ref[...] += lax.dot_general(pb, dov, TN, preferred_element_type=F32)
            dk_ref[...] += lax.dot_general(dsb, qv, TN, preferred_element_type=F32)
            dq_ref[rows, :] += jnp.dot(dsb, kv, preferred_element_type=F32)

        dk_ref[...] = jnp.zeros_like(dk_ref)
        dv_ref[...] = jnp.zeros_like(dv_ref)
        queries(ki * blk, blk, True)
        later = nb - 1 - ki

        @pl.when(later % 2 == 1)
        def _():
            queries((ki + 1) * blk, blk, False)

        @pl.loop(0, later // 2)
        def _(j):
            queries((ki + 1 + later % 2 + 2 * j) * blk, 2 * blk, False)

        dk_out[...] = dk_ref[...].astype(dk_out.dtype)
        dv_out[...] = dv_ref[...].astype(dv_out.dtype)

    full = lambda w: pl.BlockSpec((t, w), lambda h, j: (0, h))
    blkd = lambda w: pl.BlockSpec((blk, w), lambda h, j: (j, h))
    return _call(body, name, (HEADS, nb),
                 [full(dq_w), blkd(dq_w), blkd(HEAD_DIM), full(HEAD_DIM), full(HEAD_DIM), full(HEAD_DIM)],
                 [full(dq_w), blkd(dq_w), blkd(HEAD_DIM)],
                 [_sds(q.shape, F32), _sds(k.shape, BF16), _sds(v.shape, BF16)],
                 [pltpu.VMEM((blk, dq_w), F32), pltpu.VMEM((blk, HEAD_DIM), F32)], [q, k, v, do, o, lse], host)


def _chunk_decays(lg, blk):
    row = lax.broadcasted_iota(jnp.int32, (blk, HEAD_DIM), 0).astype(F32)
    return jnp.exp(lg * (row + 1.0)), jnp.exp(lg * (blk - 1.0 - row)), jnp.exp(lg * blk * jnp.ones((1, HEAD_DIM), F32))


def ret_fwd(name, q, k, v, blk, host=None):
    t = q.shape[0]
    nb = t // blk

    def body(q_ref, k_ref, v_ref, o_ref, st_ref, state):
        h, i = pl.program_id(0), pl.program_id(1)
        lg = _log_decay(h)

        @pl.when(i == 0)
        def _():
            state[...] = jnp.zeros_like(state)

        qv, kv, vv = q_ref[...], k_ref[...], v_ref[...]
        before = state[...]
        st_ref[...] = before
        p, _ = _decayed_scores(qv, kv, lg)
        xi, zeta, g_blk = _chunk_decays(lg, blk)
        o_ref[...] = (jnp.dot(p.astype(BF16), vv, preferred_element_type=F32)
                      + jnp.dot(qv, before.astype(BF16), preferred_element_type=F32) * xi)
        kz = (kv.astype(F32) * zeta).astype(BF16)
        state[...] = before * g_blk + lax.dot_general(kz, vv, TN, preferred_element_type=F32)

    hspec = pl.BlockSpec((blk, HEAD_DIM), lambda h, i: (i, h))
    return _call(body, name, (HEADS, nb), [hspec] * 3,
                 [hspec, pl.BlockSpec((HEAD_DIM, HEAD_DIM), lambda h, i: (i, h))],
                 [_sds((t, HEADS * HEAD_DIM), F32), _sds((nb * HEAD_DIM, HEADS * HEAD_DIM), F32)],
                 [pltpu.VMEM((HEAD_DIM, HEAD_DIM), F32)], [q, k, v], host)


def ret_bwd(name, q, k, v, do, states, blk, host=None):
    t = q.shape[0]
    nb = t // blk

    def body(q_ref, k_ref, v_ref, do_ref, st_ref, dq_ref, dk_ref, dv_ref, dstate):
        h, i = pl.program_id(0), pl.program_id(1)
        lg = _log_decay(h)

        @pl.when(i == 0)
        def _():
            dstate[...] = jnp.zeros_like(dstate)

        qv, kv, vv, dov = q_ref[...], k_ref[...], v_ref[...], do_ref[...]
        before = st_ref[...].astype(BF16)
        after_grad = dstate[...]
        p, dec = _decayed_scores(qv, kv, lg)
        ds = lax.dot_general(dov, vv, NT, preferred_element_type=F32) * dec
        pb, dsb = p.astype(BF16), ds.astype(BF16)
        xi, zeta, g_blk = _chunk_decays(lg, blk)
        dox = (dov.astype(F32) * xi).astype(BF16)
        kz = (kv.astype(F32) * zeta).astype(BF16)
        agb = after_grad.astype(BF16)
        dv = lax.dot_general(pb, dov, TN, preferred_element_type=F32) + jnp.dot(kz, agb, preferred_element_type=F32)
        dq = (jnp.dot(dsb, kv, preferred_element_type=F32)
              + lax.dot_general(dox, before, NT, preferred_element_type=F32))
        dk = (lax.dot_general(dsb, qv, TN, preferred_element_type=F32)
              + lax.dot_general(vv, agb, NT, preferred_element_type=F32) * zeta)
        for ref, val in ((dq_ref, dq), (dk_ref, dk), (dv_ref, dv)):
            ref[...] = val.astype(ref.dtype)
        dstate[...] = after_grad * g_blk + lax.dot_general(qv, dox, TN, preferred_element_type=F32)

    hspec = pl.BlockSpec((blk, HEAD_DIM), lambda h, i: (nb - 1 - i, h))
    return _call(body, name, (HEADS, nb),
                 [hspec] * 4 + [pl.BlockSpec((HEAD_DIM, HEAD_DIM), lambda h, i: (nb - 1 - i, h))],
                 [hspec] * 3, [_sds(q.shape, BF16)] * 3, [pltpu.VMEM((HEAD_DIM, HEAD_DIM), F32)],
                 [q, k, v, do, states], host)


CHIP_FLIPS = ((1, 0), (0, 1), (1, 1))
CHIP_BARRIER = 1
ALL_BARRIER = 2


def _position():
    return lax.axis_index("x"), lax.axis_index("y"), lax.axis_index("c")


def _chip_peers(pos):
    x, y, c = pos
    return [(x ^ fx, y ^ fy, c) for fx, fy in CHIP_FLIPS]


class _Comm:
    def __init__(self, ins, out_shape, plan, n_remote, n_local, collective_id):
        self.ins, self.out_shape, self.plan = list(ins), list(out_shape), plan
        self.n_remote, self.n_local, self.collective_id = n_remote, n_local, collective_id
        self.result = None

    def scratch(self):
        return [pltpu.SemaphoreType.DMA((self.n_remote,)), pltpu.SemaphoreType.DMA((self.n_remote,)),
                pltpu.SemaphoreType.DMA((self.n_local,))]

    def _copies(self, in_refs, out_refs, sems):
        send_sems, recv_sems, local_sems = sems
        pos = _position()
        p = self.plan(pos, in_refs, out_refs)

        def remote(k, src, dst, dev):
            return pltpu.make_async_remote_copy(src_ref=src, dst_ref=dst, send_sem=send_sems.at[k],
                                                recv_sem=recv_sems.at[k], device_id=dev, device_id_type=MESH)

        local = [pltpu.make_async_copy(s, d, local_sems.at[i]) for i, (s, d) in enumerate(p["local"])]
        out = [remote(k, s, d, dev) for k, (s, d, dev) in enumerate(p["sends"])]
        arrivals = [functools.partial(remote, k, d, d, pos) for k, d in enumerate(p["recvs"])]
        return local, out, arrivals, p["peers"]

    def start(self, in_refs, out_refs, sems):
        local, out, _, peers = self._copies(in_refs, out_refs, sems)
        barrier = pltpu.get_barrier_semaphore()
        for peer in peers:
            pl.semaphore_signal(barrier, inc=1, device_id=peer, device_id_type=MESH)
        pl.semaphore_wait(barrier, len(peers))
        for cp in local + out:
            cp.start()

    def finish(self, in_refs, out_refs, sems):
        local, out, arrivals, _ = self._copies(in_refs, out_refs, sems)
        for make in arrivals:
            make().wait_recv()
        for cp in out:
            cp.wait_send()
        for cp in local:
            cp.wait()

    def run(self, name):
        n_in, n_out = len(self.ins), len(self.out_shape)

        def body(*refs):
            comm = (refs[:n_in], refs[n_in:n_in + n_out], refs[n_in + n_out:])
            self.start(*comm)
            self.finish(*comm)

        hbm = pl.BlockSpec(memory_space=pl.ANY)
        self.result = pl.pallas_call(
            body, name=name, in_specs=[hbm] * n_in, out_specs=[hbm] * n_out, out_shape=self.out_shape,
            scratch_shapes=self.scratch(),
            compiler_params=pltpu.CompilerParams(collective_id=self.collective_id))(*self.ins)
        return self.result


def _half_rows(c, rows):
    r2 = rows // 2
    return pl.ds(pl.multiple_of(c * r2, math.gcd(r2, LANES)), r2)


def _half(ref, c, rows, lead=()):
    return ref.at[(*lead, _half_rows(c, rows))]


def gather_halves(shards):
    def plan(pos, ins, outs):
        x, y, c = pos
        me = 2 * x + y
        p = dict(local=[], sends=[], recvs=[], peers=_chip_peers(pos))
        for a, (src, dst) in enumerate(zip(ins, outs)):
            rows = shards[a].shape[0]
            p["local"].append((src, dst.at[me]))
            for px, py, _ in p["peers"]:
                p["sends"].append((_half(src, c, rows), _half(dst, c, rows, (me,)), (px, py, c)))
                p["recvs"].append(_half(dst, c, rows, (2 * px + py,)))
        return p

    return _Comm(shards, [_sds((N_CHIPS, *s.shape), s.dtype) for s in shards], plan,
                 n_remote=3 * len(shards), n_local=len(shards), collective_id=CHIP_BARRIER)


def chip_exchange(parts):
    def plan(pos, ins, outs):
        x, y, c = pos
        me = 2 * x + y
        p = dict(local=[], sends=[], recvs=[], peers=_chip_peers(pos))
        for src, dst in zip(ins, outs):
            p["local"].append((src.at[me], dst.at[me]))
            for px, py, _ in p["peers"]:
                peer = 2 * px + py
                p["sends"].append((src.at[peer], dst.at[me], (px, py, c)))
                p["recvs"].append(dst.at[peer])
        return p

    return _Comm(parts, [_sds(g.shape, g.dtype) for g in parts], plan, n_remote=3 * len(parts), n_local=len(parts),
                 collective_id=CHIP_BARRIER)


def all_gather_devices(v):
    flips = [(fx, fy, fc) for fx in (0, 1) for fy in (0, 1) for fc in (0, 1)][1:]

    def plan(pos, ins, outs):
        x, y, c = pos
        me = 4 * x + 2 * y + c
        p = dict(local=[(ins[0], outs[0].at[me])], sends=[], recvs=[],
                 peers=[(x ^ fx, y ^ fy, c ^ fc) for fx, fy, fc in flips])
        for px, py, pc in p["peers"]:
            p["sends"].append((ins[0], outs[0].at[me], (px, py, pc)))
            p["recvs"].append(outs[0].at[4 * px + 2 * py + pc])
        return p

    return _Comm([v], [_sds((8, *v.shape), v.dtype)], plan, n_remote=7, n_local=1,
                 collective_id=ALL_BARRIER).run("small_all_gather")[0]


SWAP_CHUNK_BYTES = 3 * 2 ** 19


def _sibling_stream(t, n, value, consume, sbuf, rbuf, send_sems, recv_sems, credits):
    x, y, c = _position()
    sib = (x, y, 1 - c)

    def copy(slot):
        return pltpu.make_async_remote_copy(src_ref=sbuf.at[slot], dst_ref=rbuf.at[slot], send_sem=send_sems.at[slot],
                                            recv_sem=recv_sems.at[slot], device_id=sib, device_id_type=MESH)

    slot = t % 2

    @pl.when(t == 0)
    def _():
        barrier = pltpu.get_barrier_semaphore()
        pl.semaphore_signal(barrier, inc=1, device_id=sib, device_id_type=MESH)
        pl.semaphore_wait(barrier, 1)

    @pl.when(jnp.logical_and(t >= 2, t < n))
    def _():
        copy(slot).wait_send()
        pl.semaphore_wait(credits.at[slot], 1)

    @pl.when(t < n)
    def _():
        sbuf[slot] = value
        copy(slot).start()

    @pl.when(t >= 1)
    def _():
        prev = 1 - slot
        copy(prev).wait_recv()
        consume(sbuf[prev], rbuf[prev])

        @pl.when(t + 1 < n)
        def _():
            pl.semaphore_signal(credits.at[prev], inc=1, device_id=sib, device_id_type=MESH)

    @pl.when(t == n)
    def _():
        copy(1 - slot).wait_send()
        if n > 1:
            copy(slot).wait_send()


def _swap_scratch(rows, cols, dtype):
    return [pltpu.VMEM((2, rows, cols), dtype), pltpu.VMEM((2, rows, cols), dtype),
            pltpu.SemaphoreType.DMA((2,)), pltpu.SemaphoreType.DMA((2,)), pltpu.SemaphoreType.REGULAR((2,))]


def _chunk_rows(rows, cols, dtype, nbytes=SWAP_CHUNK_BYTES):
    return _tile(rows, max(16, nbytes // (cols * jnp.dtype(dtype).itemsize)), 16)


def pair_add(name, g):
    s, r, c_ = g.shape
    r2 = r // 2
    cr = _chunk_rows(r2, c_, g.dtype)
    nj = r2 // cr

    n = s * nj

    def body(core, mine_ref, theirs_ref, o_ref, *scratch):
        def consume(_, got):
            o_ref[...] = (mine_ref[...].astype(F32) + got.astype(F32)).astype(o_ref.dtype)

        _sibling_stream(pl.program_id(0), n, theirs_ref[...], consume, *scratch)

    sent = lambda t: jnp.minimum(t, n - 1)
    used = lambda t: jnp.maximum(t - 1, 0)
    grid_spec = pltpu.PrefetchScalarGridSpec(
        num_scalar_prefetch=1, grid=(n + 1,),
        in_specs=[pl.BlockSpec((None, cr, c_), lambda t, core: (used(t) // nj, core[0] * nj + used(t) % nj, 0)),
                  pl.BlockSpec((None, cr, c_), lambda t, core: (sent(t) // nj, (1 - core[0]) * nj + sent(t) % nj, 0))],
        out_specs=pl.BlockSpec((None, cr, c_), lambda t, core: (used(t) // nj, used(t) % nj, 0)),
        scratch_shapes=_swap_scratch(cr, c_, g.dtype))
    core = lax.axis_index("c").astype(jnp.int32).reshape(1)
    return pl.pallas_call(body, name=name, grid_spec=grid_spec, out_shape=_sds((s, r2, c_), g.dtype),
                          compiler_params=_params("arbitrary", collective_id=SIBLING_BARRIER))(core, g, g)


def _adamw_math(w, g, m, v):
    m = ADAM_B1 * m + (1.0 - ADAM_B1) * g
    v = ADAM_B2 * v + (1.0 - ADAM_B2) * (g * g)
    m_hat = m / (1.0 - ADAM_B1 ** ADAM_STEP)
    v_hat = v / (1.0 - ADAM_B2 ** ADAM_STEP)
    return -ADAM_LR * (m_hat / (jnp.sqrt(v_hat) + ADAM_EPS) + ADAM_WD * w), m, v


def reduce_join(name, p):
    s, r2, c_ = p.shape
    cr = _chunk_rows(r2, c_, F32)
    nj = r2 // cr

    def body(core, p_ref, o_ref, *scratch):
        acc = p_ref[0].astype(F32)
        for i in range(1, s):
            acc = acc + p_ref[i].astype(F32)

        def consume(own, got):
            c = core[0]
            o_ref[c] = own
            o_ref[1 - c] = got

        _sibling_stream(pl.program_id(0), nj, acc, consume, *scratch)

    grid_spec = pltpu.PrefetchScalarGridSpec(
        num_scalar_prefetch=1, grid=(nj + 1,),
        in_specs=[pl.BlockSpec((s, cr, c_), lambda t, core: (0, jnp.minimum(t, nj - 1), 0))],
        out_specs=pl.BlockSpec((2, cr, c_), lambda t, core: (0, jnp.maximum(t - 1, 0), 0)),
        scratch_shapes=_swap_scratch(cr, c_, F32))
    core = lax.axis_index("c").astype(jnp.int32).reshape(1)
    out = pl.pallas_call(body, name=name, grid_spec=grid_spec, out_shape=_sds((2, r2, c_), F32),
                         compiler_params=_params("arbitrary", collective_id=SIBLING_BARRIER))(core, p)
    return out.reshape(2 * r2, c_)


def sibling_fill(name, buf):
    s, r, c_ = buf.shape
    r2 = r // 2
    cr = _chunk_rows(r2, c_, buf.dtype)
    nj = r2 // cr
    n_peers = len(CHIP_FLIPS)

    n = n_peers * nj

    def body(where, in_ref, o_ref, *scratch):
        def consume(_, got):
            o_ref[...] = got

        _sibling_stream(pl.program_id(0), n, in_ref[...], consume, *scratch)

    sent = lambda t: jnp.minimum(t, n - 1)
    used = lambda t: jnp.maximum(t - 1, 0)
    grid_spec = pltpu.PrefetchScalarGridSpec(
        num_scalar_prefetch=1, grid=(n + 1,),
        in_specs=[pl.BlockSpec((None, cr, c_),
                               lambda t, where: (where[sent(t) // nj], where[n_peers] * nj + sent(t) % nj, 0))],
        out_specs=pl.BlockSpec((None, cr, c_),
                               lambda t, where: (where[used(t) // nj], (1 - where[n_peers]) * nj + used(t) % nj, 0)),
        scratch_shapes=_swap_scratch(cr, c_, buf.dtype))
    x, y, c = _position()
    where = jnp.stack([2 * (x ^ fx) + (y ^ fy) for fx, fy in CHIP_FLIPS] + [c]).astype(jnp.int32)
    return pl.pallas_call(body, name=name, grid_spec=grid_spec, out_shape=_sds(buf.shape, buf.dtype),
                          input_output_aliases={1: 0},
                          compiler_params=_params("arbitrary", collective_id=SIBLING_BARRIER))(where, buf)


def sum_slots(name, p, out_dtype):
    s, r, c = p.shape
    tr = _tile(r, 256, 16)

    def body(p_ref, o_ref):
        acc = p_ref[0].astype(F32)
        for i in range(1, s):
            acc = acc + p_ref[i].astype(F32)
        o_ref[...] = acc.astype(o_ref.dtype)

    return pl.pallas_call(
        body, name=name, grid=(r // tr,), in_specs=[pl.BlockSpec((s, tr, c), lambda i: (0, i, 0))],
        out_specs=pl.BlockSpec((tr, c), lambda i: (i, 0)), out_shape=_sds((r, c), out_dtype),
        compiler_params=_params("arbitrary"),
    )(p)


def adamw(name, w, g, m, v):
    r, c = w.shape
    outs, _ = _rowwise(name, lambda _, w, g, m, v: ([*_adamw_math(w, g, m, v), g], []), [w, g, m, v], [],
                       [(c, F32)] * 4, [], _tile(r, 256, 8))
    return outs


RET_SCALE = HEAD_DIM ** -0.5
MLA_SCALE = (HEAD_DIM + MLA_ROPE) ** -0.5
GRAD_DT = BF16
IN_RET = 4 * HEADS * HEAD_DIM
IN_MLA = MLA_Q_RANK + MLA_KV_RANK + MLA_ROPE
IN_MLA_PAD = IN_MLA + 64


def _heads(fn):
    return jnp.concatenate([fn(h) for h in range(HEADS)], axis=1)


def _head(a, h, stride=HEAD_DIM, off=0):
    return a[:, h * stride + off:h * stride + off + HEAD_DIM]


def _group_norm(o):
    rs = [lax.rsqrt(jnp.mean(_head(o, h) * _head(o, h), axis=-1, keepdims=True) + EPS) for h in range(HEADS)]
    return _heads(lambda h: _head(o, h) * rs[h]), rs


class _Alone:
    def host(self, kernel_name):
        return None

    def done(self, kernel_name, w):
        pass

    def grads(self, g):
        pass


def _ffn_fwd(tag, n, w, k, tm, tmk, plan):
    gate, up, down = tag + "_gate", tag + "_up", tag + "_down"
    if "wu" + k in w:
        g, u, a = mm_nn(up, n, w["wg" + k], tm, outs=[BF16] * 3, w2=w["wu" + k],
                        epilogue=lambda g, u: (g, u, _silu(g) * u), host=plan.host(up))
    else:
        (g,) = mm_nn(gate, n, w["wg" + k], tm, out_dtype=BF16, host=plan.host(gate))
        plan.done(gate, w)
        u, a = mm_nn(up, n, w["wu" + k], tm, outs=[BF16] * 2, extras=(g,),
                     epilogue=lambda u, g: (u, _silu(g.astype(F32)) * u), host=plan.host(up))
    plan.done(up, w)
    ff = a.shape[1]
    wd = w["wd" + k]
    f = mm_nn_k(down, a, wd.reshape(ff, wd.shape[2]), tmk, _tile(ff, 1408), host=plan.host(down))
    plan.done(down, w)
    return g, u, a, f


def _ffn_bwd(tag, df, n, g, u, a, wg, wu, wd, tm, tmk, tk, dt, plan):
    ns = wg.shape[2]

    def gate_grads(da, g, u):
        g, u = g.astype(F32), u.astype(F32)
        return da * u * _dsilu(g), da * _silu(g)

    def hosted(kernel_name, call):
        out = call(plan.host(kernel_name))
        plan.done(kernel_name, None)
        return out

    k = tag[-1]
    dg, du = hosted(tag + "_da", lambda h: mm_nt(tag + "_da", [df], [wd], tm, ns, outs=(BF16, BF16),
                                                 epilogue=gate_grads, extras=(g, u), host=h))
    dwg = hosted(tag + "_dwg", lambda h: mm_tn(tag + "_dwg", n, dg, dt, ns, tk, GRAD_DT, shard_cols=True, host=h))
    plan.grads({"wg" + k: dwg})
    dwu = hosted(tag + "_dwu", lambda h: mm_tn(tag + "_dwu", n, du, dt, ns, tk, GRAD_DT, shard_cols=True, host=h))
    plan.grads({"wu" + k: dwu})
    dwd = hosted(tag + "_dwd", lambda h: mm_tn(tag + "_dwd", a, df, ns, dt, tk, GRAD_DT, shard_rows=True, host=h))
    plan.grads({"wd" + k: dwd})
    dn = hosted(tag + "_dn", lambda h: mm_nt_k(tag + "_dn", [dg, du], [wg, wu], tmk, dt, out_dtype=BF16, host=h))
    return dn


def _local_step(x, tgt, meta, w, nw, plan):
    seq, d = x.shape
    t_real = N_META + seq
    tp = -(-t_real // LANES) * LANES
    zpad = jnp.zeros((tp - t_real, d), F32)
    h0 = jnp.concatenate([meta, x, zpad], axis=0)
    tgt_p = jnp.concatenate([jnp.zeros((N_META, d), F32), tgt, zpad], axis=0)
    cos_r, sin_r, cos_m, sa, sb = _rope_tables(tp)
    tm = _tile(tp, 512)
    tmt = _tile(tp, 768, 16)
    tmk = _tile(tp, 1408)
    tk = tp
    blk = tm
    dt = _tile(d, 1024)
    hw = HEADS * HEAD_DIM
    qw = HEADS * MLA_QK_PAD

    (n1,), _ = _rowwise("ffn1_norm", lambda r0, h, g: ([_rms(h, g)], []), [h0], [nw["ffn1_pre_norm"]],
                        [(d, BF16)], [], tm)
    g1, u1, a1, f1 = _ffn_fwd("ffn1", n1, w, "1", tm, tmk, plan)

    def post_ffn1(r0, h, f, post, pre):
        h1 = h + 0.5 * _rms(f, post)
        return [h1, _rms(h1, pre)], []

    (h1, un), _ = _rowwise("mix_norm", post_ffn1, [h0, f1], [nw["ffn1_post_norm"], nw["mix_pre_norm"]],
                           [(d, F32), (d, BF16)], [], tm)
    (proj_r,) = mm_nn("proj_r", un, w["w_r"], tm, n_block=_tile(IN_RET, 1024), host=plan.host("proj_r"))
    plan.done("proj_r", w)
    (proj_c,) = mm_nn("proj_c", un, w["w_c"], tm)

    def split_proj(r0, pr, pc, cr, sr, cm, ta, tb, qn, kvn):
        rq = _heads(lambda h: _rope_r(_head(pr, h), cr, sr))
        rk = _heads(lambda h: _rope_r(_head(pr, h, off=hw), cr, sr) * RET_SCALE)
        rv = pr[:, 2 * hw:3 * hw]
        cqn = _rms(pc[:, :MLA_Q_RANK], qn)
        ckvn = _rms(pc[:, MLA_Q_RANK:MLA_Q_RANK + MLA_KV_RANK], kvn)
        krr = _rope_m(pc[:, MLA_Q_RANK + MLA_KV_RANK:], cm, ta, tb)
        return [rq, rk, rv, cqn, ckvn, krr], []

    (rq, rk, rv, cqn, ckvn, krr), _ = _rowwise(
        "split_proj", split_proj, [proj_r, proj_c, cos_r, sin_r, cos_m, sa, sb],
        [nw["mla_q_norm"], nw["mla_kv_norm"]],
        [(hw, BF16), (hw, BF16), (hw, BF16), (MLA_Q_RANK, BF16), (MLA_KV_RANK, BF16), (LANES, F32)], [], tm)
    (qp,) = mm_nn("q_up", cqn, w["wuq"], tm)
    (kn,) = mm_nn("k_up", ckvn, w["wuk"], tm)
    (vv,) = mm_nn("v_up", ckvn, w["wuv"], tm, out_dtype=BF16)

    def build_qk(r0, qp, kn, krr, cm, ta, tb):
        qc = jnp.concatenate(
            [part for h in range(HEADS)
             for part in (_head(qp, h, MLA_QK_PAD), _rope_m(_head(qp, h, MLA_QK_PAD, HEAD_DIM), cm, ta, tb))], axis=1)
        kc = jnp.concatenate([part for h in range(HEADS) for part in (_head(kn, h), krr)], axis=1)
        return [qc, kc], []

    (qc, kc), _ = _rowwise("build_qk", build_qk, [qp, kn, krr, cos_m, sa, sb], [], [(qw, BF16), (qw, BF16)], [], tm)
    o_m, lse = attn_fwd("mla_fwd", qc, kc, vv, blk, MLA_SCALE, host=plan.host("mla_fwd"))
    plan.done("mla_fwd", w)
    o_r, ret_states = ret_fwd("ret_fwd", rq, rk, rv, blk, host=plan.host("ret_fwd"))
    plan.done("ret_fwd", w)

    def gate_mix(r0, rg, o_r, o_m, gn):
        y, _ = _group_norm(o_r)
        return [jnp.concatenate([_silu(rg) * (y * gn), o_m], axis=1)], []

    (mixcat,), _ = _rowwise("gate_mix", gate_mix, [(proj_r, hw, 3), o_r, o_m], [nw["ret_group_norm"]],
                            [(2 * hw, BF16)], [], tm)
    (mix,) = mm_nn("mix_out", mixcat, w["w_out"], tm, n_block=dt)

    def post_mix(r0, h, m, post, pre):
        h2 = h + _rms(m, post)
        return [h2, _rms(h2, pre)], []

    (h2, n3), _ = _rowwise("ffn2_norm", post_mix, [h1, mix], [nw["mix_post_norm"], nw["ffn2_pre_norm"]],
                           [(d, F32), (d, BF16)], [], tm)
    g2, u2, a2, f2 = _ffn_fwd("ffn2", n3, w, "2", tm, tmk, plan)

    def loss_head(r0, h, f, t, post):
        h3 = h + 0.5 * _rms(f, post)
        row = r0 + lax.broadcasted_iota(jnp.int32, h3.shape, 0)
        err = jnp.where(row >= N_META, jnp.where(row < t_real, h3 - t, 0.0), 0.0)
        dh3 = err / d
        df, dpost = _rms_bwd(f, post, 0.5 * dh3)
        return [dh3, df], [_colsum(err * err), _colsum(dpost)]

    (dh3, df2), (loss_vec, d_post2) = _rowwise("loss_head", loss_head, [h2, f2, tgt_p], [nw["ffn2_post_norm"]],
                                               [(d, F32), (d, BF16)], [d, d], tm)
    loss = 0.5 * jnp.sum(loss_vec) / d
    dn3 = _ffn_bwd("ffn2", df2, n3, g2, u2, a2, w["wg2"], w["wu2"], w["wd2"], tmt, tmk, tk, dt, plan)

    def back_mix_norm(r0, h, m, dh3, dn, pre, post):
        dx, dpre = _rms_bwd(h, pre, dn)
        dh2 = dh3 + dx
        dm, dpost = _rms_bwd(m, post, dh2)
        return [dh2, dm], [_colsum(dpre), _colsum(dpost)]

    (dh2, dmix), (d_pre2, d_mix_post) = _rowwise(
        "back_mix_norm", back_mix_norm, [h2, mix, dh3, dn3], [nw["ffn2_pre_norm"], nw["mix_post_norm"]],
        [(d, F32), (d, BF16)], [d, d], tm)
    (dmixcat,) = mm_nt("mix_dx", [dmix], [w["w_out"]], tmt, _tile(2 * hw, 512), outs=(BF16,))
    plan.grads(dict(w_out=mm_tn("mix_dw", mixcat, dmix, 2 * hw // N_CHIPS, dt, tk, GRAD_DT, shard_rows=True)))

    def back_gate(r0, dmc, rg, o_r, gn):
        d_ret, d_om = dmc[:, :hw], dmc[:, hw:]
        yh, rs = _group_norm(o_r)
        d_rg = d_ret * (yh * gn) * _dsilu(rg)
        dy = d_ret * _silu(rg)
        gyh = dy * gn
        d_or = _heads(lambda h: rs[h] * (_head(gyh, h) - _head(yh, h) * jnp.mean(_head(gyh, h) * _head(yh, h),
                                                                                    axis=-1, keepdims=True)))
        return [d_or, d_rg, d_om], [_colsum(dy * yh)]

    (d_or, d_rg, d_om), (d_gn,) = _rowwise("back_gate", back_gate, [dmixcat, (proj_r, hw, 3), o_r],
                                           [nw["ret_group_norm"]], [(hw, BF16), (hw, F32), (hw, BF16)], [hw], tm)
    dqc, dkc, dvv = attn_bwd("mla_bwd", qc, kc, vv, d_om, o_m, lse, blk, MLA_SCALE, host=plan.host("mla_bwd"))
    plan.done("mla_bwd", None)
    drq, drk, drv = ret_bwd("ret_bwd", rq, rk, rv, d_or, ret_states, blk)

    def back_qk(r0, dqc, dkc, dvv, cm, ta, tb):
        dkc = dkc.astype(F32)
        dqp = jnp.concatenate(
            [part for h in range(HEADS)
             for part in (_head(dqc, h, MLA_QK_PAD), _rope_m_t(_head(dqc, h, MLA_QK_PAD, HEAD_DIM), cm, ta, tb))],
            axis=1)
        dkn = _heads(lambda h: _head(dkc, h, MLA_QK_PAD))
        dkr = _head(dkc, 0, MLA_QK_PAD, HEAD_DIM)
        for h in range(1, HEADS):
            dkr = dkr + _head(dkc, h, MLA_QK_PAD, HEAD_DIM)
        return [dqp, dkn, _rope_m_t(dkr, cm, ta, tb), dvv], []

    (dqp, dkn, dkr, dvb), _ = _rowwise("back_qk", back_qk, [dqc, dkc, dvv, cos_m, sa, sb], [],
                                       [(qw, BF16), (hw, BF16), (LANES, F32), (hw, BF16)], [], tm)
    (dcqn,) = mm_nt("q_dx", [dqp], [w["wuq"]], tm, MLA_Q_RANK)
    dwuq = mm_tn("q_dw", cqn, dqp, MLA_Q_RANK, _tile(qw, 1024), tk, GRAD_DT)
    (dckvn,) = mm_nt("kv_dx", [dkn, dvb], [w["wuk"], w["wuv"]], tm, MLA_KV_RANK)
    dwuk = mm_tn("k_dw", ckvn, dkn, MLA_KV_RANK, hw, tk, GRAD_DT)
    dwuv = mm_tn("v_dw", ckvn, dvb, MLA_KV_RANK, hw, tk, GRAD_DT)

    def back_proj(r0, drq, drk, drv, d_rg, pc, dcqn, dckvn, dkr, cr, sr, qn, kvn):
        d_q = _heads(lambda h: _rope_r_t(_head(drq, h), cr, sr))
        d_k = _heads(lambda h: _rope_r_t(_head(drk, h), cr, sr) * RET_SCALE)
        dcq, a_q = _rms_bwd(pc[:, :MLA_Q_RANK], qn, dcqn)
        dckv, a_kv = _rms_bwd(pc[:, MLA_Q_RANK:MLA_Q_RANK + MLA_KV_RANK], kvn, dckvn)
        return ([jnp.concatenate([d_q, d_k, drv, d_rg], axis=1), jnp.concatenate([dcq, dckv, dkr], axis=1)],
                [_colsum(a_q), _colsum(a_kv)])

    (dproj_r, dproj_c), (d_qn, d_kvn) = _rowwise(
        "back_proj", back_proj, [drq, drk, drv, d_rg, proj_c, dcqn, dckvn, dkr, cos_r, sin_r],
        [nw["mla_q_norm"], nw["mla_kv_norm"]], [(IN_RET, BF16), (IN_MLA_PAD, BF16)],
        [MLA_Q_RANK, MLA_KV_RANK], tm)
    (dun,) = mm_nt("proj_dx", [dproj_r, dproj_c], [w["w_r"], w["w_c"]], tmt, _tile(d, 512), outs=(BF16,))
    dw_r = mm_tn("proj_dw_r", un, dproj_r, dt, _tile(IN_RET, 1024), tk, GRAD_DT)
    dw_c = mm_tn("proj_dw_c", un, dproj_c, dt, IN_MLA_PAD, tk, GRAD_DT)
    plan.grads(dict(w_r=dw_r, w_c=dw_c, wuq=dwuq, wuk=dwuk, wuv=dwuv))

    def back_ffn1_norm(r0, h, f, dh2, dn, pre, post):
        dx, dpre = _rms_bwd(h, pre, dn)
        dh1 = dh2 + dx
        df, dpost = _rms_bwd(f, post, 0.5 * dh1)
        return [dh1, df], [_colsum(dpre), _colsum(dpost)]

    (dh1, df1), (d_mix_pre, d_post1) = _rowwise(
        "back_ffn1_norm", back_ffn1_norm, [h1, f1, dh2, dun], [nw["mix_pre_norm"], nw["ffn1_post_norm"]],
        [(d, F32), (d, BF16)], [d, d], tm)
    dn1 = _ffn_bwd("ffn1", df1, n1, g1, u1, a1, w["wg1"], w["wu1"], w["wd1"], tmt, tmk, tk, dt, plan)

    def back_input(r0, h, dh1, dn, pre):
        dx, dpre = _rms_bwd(h, pre, dn)
        return [dh1 + dx], [_colsum(dpre)]

    (dh0,), (d_pre1,) = _rowwise("back_input", back_input, [h0, dh1, dn1], [nw["ffn1_pre_norm"]], [(d, F32)], [d], tm)

    small = dict(ffn1_pre_norm=d_pre1, ffn1_post_norm=d_post1, mix_pre_norm=d_mix_pre, ret_group_norm=d_gn,
                 mla_q_norm=d_qn, mla_kv_norm=d_kvn, mix_post_norm=d_mix_post, ffn2_pre_norm=d_pre2,
                 ffn2_post_norm=d_post2)
    return loss, dh0[N_META:t_real], small, dh0[:N_META]


WEIGHTS = ("meta_tokens", "ffn1_pre_norm", "ffn1_w_gate", "ffn1_w_up", "ffn1_w_down", "ffn1_post_norm",
           "mix_pre_norm", "w_in", "ret_group_norm", "mla_q_norm", "mla_w_uq", "mla_kv_norm", "mla_w_uk",
           "mla_w_uv", "w_out", "mix_post_norm", "ffn2_pre_norm", "ffn2_w_gate", "ffn2_w_up", "ffn2_w_down",
           "ffn2_post_norm")
BIG = ("ffn1_w_gate", "ffn1_w_up", "ffn1_w_down", "w_in", "mla_w_uq", "mla_w_uk", "mla_w_uv", "w_out",
       "ffn2_w_gate", "ffn2_w_up", "ffn2_w_down")
NORMS = ("ffn1_pre_norm", "ffn1_post_norm", "mix_pre_norm", "ret_group_norm", "mla_q_norm", "mla_kv_norm",
         "mix_post_norm", "ffn2_pre_norm", "ffn2_post_norm")


def _unshard_cols(g):
    return g.transpose(1, 0, 2).reshape(g.shape[1], -1)


def _shard_cols(a):
    return a.reshape(a.shape[0], N_CHIPS, -1).transpose(1, 0, 2)


def _pack_rows(rows, width):
    rows = [jnp.pad(r, ((0, 0), (0, width - r.shape[1]))) for r in rows]
    n = sum(r.shape[0] for r in rows)
    return jnp.pad(jnp.concatenate(rows, axis=0), ((0, -n % 8), (0, 0)))


def _weight_views(full):
    w = {}
    for n, g in full.items():
        if n == "w_in":
            w_in = _unshard_cols(g)
            w["w_r"] = w_in[:, :IN_RET]
            w["w_c"] = jnp.pad(w_in[:, IN_RET:], ((0, 0), (0, IN_MLA_PAD - IN_MLA)))
        elif n == "mla_w_uq":
            q = _unshard_cols(g).reshape(MLA_Q_RANK, HEADS, HEAD_DIM + MLA_ROPE)
            q = jnp.pad(q, ((0, 0), (0, 0), (0, MLA_QK_PAD - HEAD_DIM - MLA_ROPE)))
            w["wuq"] = q.reshape(MLA_Q_RANK, HEADS * MLA_QK_PAD)
        elif n in ("mla_w_uk", "mla_w_uv"):
            w["wu" + n[-1]] = _unshard_cols(g)
        elif n == "w_out":
            w["w_out"] = g.reshape(-1, g.shape[2])
        else:
            w["w" + n[7] + n[3]] = g
    return w


def _contributions(g):
    ffn = {"g": "gate", "u": "up", "d": "down"}
    c = {f"ffn{n[2]}_w_{ffn[n[1]]}": a for n, a in g.items() if len(n) == 3 and n[2] in "12"}
    if "w_out" in g:
        c["w_out"] = g["w_out"]
    if "w_r" in g:
        dwuq = g["wuq"].reshape(MLA_Q_RANK, HEADS, MLA_QK_PAD)[:, :, :HEAD_DIM + MLA_ROPE]
        c.update(w_in=_shard_cols(jnp.concatenate([g["w_r"], g["w_c"][:, :IN_MLA]], axis=1)),
                 mla_w_uq=_shard_cols(dwuq.reshape(MLA_Q_RANK, -1)), mla_w_uk=_shard_cols(g["wuk"]),
                 mla_w_uv=_shard_cols(g["wuv"]))
    return c


class _Schedule(_Alone):
    FIRST = ("ffn1_w_gate",)
    CARRIED = {"ffn1_gate": ("ffn1_w_up",), "ffn1_up": ("ffn1_w_down",),
               "ffn1_down": ("w_in", "mla_w_uq", "mla_w_uk", "mla_w_uv"),
               "proj_r": ("w_out",), "mla_fwd": ("ffn2_w_gate", "ffn2_w_up"), "ffn2_up": ("ffn2_w_down",)}
    GRAD_HOST = dict(ffn2_w_gate="mla_bwd", ffn2_w_up="mla_bwd", ffn2_w_down="mla_bwd",
                     w_out="mla_bwd", w_in="ffn1_da", mla_w_uq="ffn1_da", mla_w_uk="ffn1_da", mla_w_uv="ffn1_da",
                     ffn1_w_gate="ffn1_dwu", ffn1_w_up="ffn1_dwd", ffn1_w_down="ffn1_dn")

    def __init__(self, shards):
        self.gathers = {k: (gather_halves([shards[n] for n in names]), names) for k, names in self.CARRIED.items()}
        self.waiting = {}
        self.exchanges = {}
        self.grad = {}

    def host(self, kernel_name):
        if kernel_name in self.gathers:
            return self.gathers[kernel_name][0]
        if kernel_name in self.waiting:
            names, sums = zip(*self.waiting.pop(kernel_name))
            self.exchanges[kernel_name] = (chip_exchange(list(sums)), names)
            return self.exchanges[kernel_name][0]
        return None

    def done(self, kernel_name, w):
        if kernel_name in self.gathers:
            comm, names = self.gathers[kernel_name]
            w.update(_weight_views({n: sibling_fill("fill_" + n, b) for n, b in zip(names, comm.result)}))
        elif kernel_name in self.exchanges:
            comm, names = self.exchanges[kernel_name]
            for n, q in zip(names, comm.result):
                self.grad[n] = reduce_join("reduce_join_" + n, q)

    def grads(self, g):
        for n, a in _contributions(g).items():
            self.waiting.setdefault(self.GRAD_HOST[n], []).append((n, pair_add("pair_add_" + n, a)))


def _step(p, m, v, x, loss_target):
    d = x.shape[2]
    names = list(BIG)
    shards = {n: p[n][0].astype(BF16) for n in names}
    first = _Schedule.FIRST
    gathered = gather_halves([shards[n] for n in first] + [p["meta_tokens"]]).run("gather_first")
    filled = [sibling_fill("fill_" + n, b) for n, b in zip(first + ("meta_tokens",), gathered)]
    w = _weight_views(dict(zip(first, filled[:-1])))
    meta = _unshard_cols(filled[-1])
    nw = {n: p[n] for n in NORMS}
    plan = _Schedule(shards)
    loss, grad_x, small, d_meta = _local_step(x[0], loss_target[0], meta, w, nw, plan)
    grads = dict(plan.grad)

    width = max(d, HEADS * HEAD_DIM)
    packed = _pack_rows([small[n] for n in NORMS] + [d_meta], width)
    total = sum_slots("small_sum", all_gather_devices(packed), F32)
    for i, n in enumerate(NORMS):
        grads[n] = total[i:i + 1, :p[n].shape[1]]
    cols = p["meta_tokens"].shape[1]
    chip = 2 * lax.axis_index("x") + lax.axis_index("y")
    grads["meta_tokens"] = lax.dynamic_slice(total[len(NORMS):len(NORMS) + N_META, :d], (0, chip * cols), (N_META, cols))

    delta, new_m, new_v = {}, {}, {}
    for n in names + ["meta_tokens"]:
        shape = p[n].shape
        flat = lambda a: a.reshape(-1, shape[-1])
        out = adamw("adamw_" + n, flat(p[n]), flat(grads[n]), flat(m[n]), flat(v[n]))
        delta[n], new_m[n], new_v[n], grads[n] = (o.reshape(shape) for o in out)
    pk = lambda src: _pack_rows([src[n] for n in NORMS], width)
    out = adamw("adamw_norms", pk(p), pk(grads), pk(m), pk(v))
    for i, n in enumerate(NORMS):
        delta[n], new_m[n], new_v[n] = (o[i:i + 1, :p[n].shape[1]] for o in out[:3])

    loss = lax.psum(loss, ("x", "y", "c"))
    return loss, grad_x[None], grads, delta, new_m, new_v


def kernel(x, meta_tokens, ffn1_pre_norm, ffn1_w_gate, ffn1_w_up, ffn1_w_down, ffn1_post_norm, mix_pre_norm, w_in, ret_group_norm, mla_q_norm, mla_w_uq, mla_kv_norm, mla_w_uk, mla_w_uv, w_out, mix_post_norm, ffn2_pre_norm, ffn2_w_gate, ffn2_w_up, ffn2_w_down, ffn2_post_norm, loss_target, m_meta_tokens, m_ffn1_pre_norm, m_ffn1_w_gate, m_ffn1_w_up, m_ffn1_w_down, m_ffn1_post_norm, m_mix_pre_norm, m_w_in, m_ret_group_norm, m_mla_q_norm, m_mla_w_uq, m_mla_kv_norm, m_mla_w_uk, m_mla_w_uv, m_w_out, m_mix_post_norm, m_ffn2_pre_norm, m_ffn2_w_gate, m_ffn2_w_up, m_ffn2_w_down, m_ffn2_post_norm, v_meta_tokens, v_ffn1_pre_norm, v_ffn1_w_gate, v_ffn1_w_up, v_ffn1_w_down, v_ffn1_post_norm, v_mix_pre_norm, v_w_in, v_ret_group_norm, v_mla_q_norm, v_mla_w_uq, v_mla_kv_norm, v_mla_w_uk, v_mla_w_uv, v_w_out, v_mix_post_norm, v_ffn2_pre_norm, v_ffn2_w_gate, v_ffn2_w_up, v_ffn2_w_down, v_ffn2_post_norm):
    args = locals()
    p = {n: args[n] for n in WEIGHTS}
    m = {n: args["m_" + n] for n in WEIGHTS}
    v = {n: args["v_" + n] for n in WEIGHTS}
    loss, grad_x, grads, delta, new_m, new_v = _step(p, m, v, x, loss_target)
    return (loss, grad_x, *[grads[n] for n in WEIGHTS], *[delta[n] for n in WEIGHTS],
            *[new_m[n] for n in WEIGHTS], *[new_v[n] for n in WEIGHTS])
```

```python
import functools
import math

import jax
import jax.numpy as jnp
import numpy as np
from jax import lax
from jax.experimental import pallas as pl
from jax.experimental.pallas import tpu as pltpu

F32 = jnp.float32
BF16 = jnp.bfloat16

EPS = 1e-6
N_META = 16
HEADS = 8
HEAD_DIM = 128
MLA_ROPE = 64
MLA_QK_PAD = 256
MLA_Q_RANK = 512
MLA_KV_RANK = 256
ROPE_THETA = 10000.0
N_CHIPS = 4
LANES = 128
VMEM_LIMIT = 60 * 2 ** 20

ADAM_LR = 0.001
ADAM_B1 = 0.9
ADAM_B2 = 0.999
ADAM_EPS = 1e-08
ADAM_WD = 0.01
ADAM_STEP = 10

NN = (((1,), (0,)), ((), ()))
NT = (((1,), (1,)), ((), ()))
TN = (((0,), (0,)), ((), ()))
MESH = pl.DeviceIdType.MESH


def _tile(n, pref, align=LANES):
    if n <= pref:
        return n
    best = 0
    for t in range(align, pref + 1, align):
        if n % t == 0:
            best = t
    assert best, (n, pref)
    return best


def _params(*sem, collective_id=None):
    return pltpu.CompilerParams(dimension_semantics=sem, vmem_limit_bytes=VMEM_LIMIT, collective_id=collective_id)


SIBLING_BARRIER = 0


def _sds(shape, dtype):
    return jax.ShapeDtypeStruct(tuple(shape), dtype)


def _rowwise(name, fn, rows, consts, outs, accs, tr):
    rows = [r if isinstance(r, tuple) else (r, r.shape[1], 0) for r in rows]
    t = rows[0][0].shape[0]
    assert t % tr == 0
    n_r, n_c, n_o = len(rows), len(consts), len(outs)

    def body(*refs):
        i = pl.program_id(0)
        r = [x[...] for x in refs[:n_r]]
        c = [x[...] for x in refs[n_r:n_r + n_c]]
        o_refs = refs[n_r + n_c:n_r + n_c + n_o]
        a_refs = refs[n_r + n_c + n_o:]
        o_vals, a_vals = fn(i * tr, *r, *c)
        for ref, v in zip(o_refs, o_vals):
            ref[...] = v.astype(ref.dtype)
        if a_refs:
            @pl.when(i == 0)
            def _():
                for ref, v in zip(a_refs, a_vals):
                    ref[...] = v

            @pl.when(i > 0)
            def _():
                for ref, v in zip(a_refs, a_vals):
                    ref[...] += v

    in_specs = [pl.BlockSpec((tr, w), functools.partial(lambda cb, i: (i, cb), cb)) for _, w, cb in rows]
    in_specs += [pl.BlockSpec(a.shape, lambda i: (0, 0)) for a in consts]
    out_specs = [pl.BlockSpec((tr, w), lambda i: (i, 0)) for w, _ in outs]
    out_specs += [pl.BlockSpec((1, w), lambda i: (0, 0)) for w in accs]
    out_shape = [_sds((t, w), dt) for w, dt in outs] + [_sds((1, w), F32) for w in accs]
    res = pl.pallas_call(
        body, name=name, grid=(t // tr,), in_specs=in_specs, out_specs=out_specs, out_shape=out_shape,
        compiler_params=_params("arbitrary"),
    )(*[a for a, _, _ in rows], *consts)
    return res[:n_o], res[n_o:]


def _rms(x, w):
    r = lax.rsqrt(jnp.mean(x * x, axis=-1, keepdims=True) + EPS)
    return x * r * w


def _rms_bwd(x, w, dy):
    r = lax.rsqrt(jnp.mean(x * x, axis=-1, keepdims=True) + EPS)
    xh = x * r
    gy = dy * w
    dx = r * (gy - xh * jnp.mean(gy * xh, axis=-1, keepdims=True))
    return dx, dy * xh


def _colsum(v):
    return jnp.sum(v, axis=0, keepdims=True)


def _silu(x):
    return x * jax.nn.sigmoid(x)


def _dsilu(x):
    s = jax.nn.sigmoid(x)
    return s * (1.0 + x * (1.0 - s))


def _rope_r(x, cos, sin):
    return x * cos + pltpu.roll(x, 64, 1) * sin


def _rope_r_t(dy, cos, sin):
    return dy * cos + pltpu.roll(dy * sin, 64, 1)


def _rope_m(x, cos, sa, sb):
    return x * cos + pltpu.roll(x, 32, 1) * sa + pltpu.roll(x, 96, 1) * sb


def _rope_m_t(dy, cos, sa, sb):
    return dy * cos + pltpu.roll(dy * sa, 96, 1) + pltpu.roll(dy * sb, 32, 1)


def _rope_tables(t):
    pos = np.arange(t, dtype=np.float32)

    def cs(dim):
        inv = np.float32(ROPE_THETA) ** (-np.arange(0, dim, 2, dtype=np.float32) / np.float32(dim))
        ang = pos[:, None] * inv[None, :]
        return np.cos(ang), np.sin(ang)

    c, s = cs(HEAD_DIM)
    cos_r = np.concatenate([c, c], axis=1)
    sin_r = np.concatenate([-s, s], axis=1)
    c, s = cs(MLA_ROPE)
    z32, z64 = np.zeros_like(s), np.zeros((t, 64), np.float32)
    cos_m = np.concatenate([c, c, z64], axis=1)
    sa = np.concatenate([z32, s, z64], axis=1)
    sb = np.concatenate([-s, z32, z64], axis=1)
    return tuple(jnp.asarray(a, F32) for a in (cos_r, sin_r, cos_m, sa, sb))


def _call(body, name, grid, in_specs, out_specs, out_shape, scratch, operands, host=None):
    sem = ("arbitrary",) * len(grid)
    if host is None:
        return pl.pallas_call(body, name=name, grid=grid, in_specs=in_specs, out_specs=out_specs, out_shape=out_shape,
                              scratch_shapes=scratch, compiler_params=_params(*sem))(*operands)
    n_in, n_out, n_s = len(in_specs), len(out_shape), len(scratch)
    h_in, h_out = len(host.ins), len(host.out_shape)

    def hosted(*refs):
        a = n_in
        b = a + h_in
        c = b + n_out
        d = c + h_out
        e = d + n_s
        ids = [pl.program_id(i) for i in range(len(grid))]
        first = functools.reduce(jnp.logical_and, [i == 0 for i in ids])
        last = functools.reduce(jnp.logical_and, [i == g - 1 for i, g in zip(ids, grid)])
        comm = (refs[a:b], refs[c:d], refs[e:])

        @pl.when(first)
        def _():
            host.start(*comm)

        body(*refs[:a], *refs[b:c], *refs[d:e])

        @pl.when(last)
        def _():
            host.finish(*comm)

    hbm = pl.BlockSpec(memory_space=pl.ANY)
    res = pl.pallas_call(
        hosted, name=name, grid=grid, in_specs=list(in_specs) + [hbm] * h_in, out_specs=list(out_specs) + [hbm] * h_out,
        out_shape=list(out_shape) + list(host.out_shape), scratch_shapes=list(scratch) + host.scratch(),
        compiler_params=_params(*sem, collective_id=host.collective_id))(*operands, *host.ins)
    host.result = res[n_out:]
    return res[:n_out]


def _mm(name, grid, operands, in_specs, dns, out_specs, out_shape, epilogue=None, extras=(), extra_specs=(),
        acc_shape=None, host=None):
    n_p, n_e = len(dns), len(extras)
    nk = grid[2]
    n_o = len(out_shape)
    in_place = nk > 1 and epilogue is None and n_o == 1 and out_shape[0].dtype == F32

    def body(*refs):
        ab = refs[:2 * n_p]
        ex = refs[2 * n_p:2 * n_p + n_e]
        outs = refs[2 * n_p + n_e:2 * n_p + n_e + n_o]

        part = None
        for p in range(n_p):
            d = lax.dot_general(ab[2 * p][...], ab[2 * p + 1][...], dns[p], preferred_element_type=F32)
            part = d if part is None else part + d

        def finish(acc):
            vals = (acc,) if epilogue is None else epilogue(acc, *[e[...] for e in ex])
            for o, v in zip(outs, vals):
                o[...] = v.astype(o.dtype)

        if nk == 1:
            finish(part)
        else:
            acc_ref = outs[0] if in_place else refs[2 * n_p + n_e + n_o]
            k = pl.program_id(2)

            @pl.when(k == 0)
            def _():
                acc_ref[...] = part

            @pl.when(k > 0)
            def _():
                acc_ref[...] += part

            if not in_place:
                @pl.when(k == nk - 1)
                def _():
                    finish(acc_ref[...])

    scratch = [] if nk == 1 or in_place else [pltpu.VMEM(acc_shape, F32)]
    return _call(body, name, grid, list(in_specs) + list(extra_specs), out_specs, out_shape, scratch,
                 [*operands, *extras], host)


def mm_nn(name, x, w, tm, out_dtype=F32, epilogue=None, outs=None, w2=None, n_block=None, extras=(), host=None):
    t, kdim = x.shape
    if w.ndim == 3:
        s, _, ns = w.shape
        n, tn, nb = s * ns, ns, s
        wspec = pl.BlockSpec((None, kdim, ns), lambda j, i, k: (j, 0, 0))
    else:
        n = w.shape[1]
        tn = n_block or n
        nb = n // tn
        wspec = pl.BlockSpec((kdim, tn), lambda j, i, k: (0, j))
    xspec = pl.BlockSpec((tm, kdim), lambda j, i, k: (i, 0))
    ospec = pl.BlockSpec((tm, tn), lambda j, i, k: (i, j))
    outs = outs or [out_dtype]
    grid = (nb, t // tm, 1)
    if w2 is None:
        return _mm(name, grid, [x, w], [xspec, wspec], [NN], [ospec] * len(outs), [_sds((t, n), d) for d in outs],
                   epilogue=epilogue, extras=extras, extra_specs=[ospec] * len(extras), host=host)

    def body(x_ref, w_ref, w2_ref, *o_refs):
        xv = x_ref[...]
        a = jnp.dot(xv, w_ref[...], preferred_element_type=F32)
        b = jnp.dot(xv, w2_ref[...], preferred_element_type=F32)
        for o, v in zip(o_refs, epilogue(a, b)):
            o[...] = v.astype(o.dtype)

    return _call(body, name, grid[:2],
                 [pl.BlockSpec((tm, kdim), lambda j, i: (i, 0)),
                  pl.BlockSpec((None, kdim, tn), lambda j, i: (j, 0, 0)),
                  pl.BlockSpec((None, kdim, tn), lambda j, i: (j, 0, 0))],
                 [pl.BlockSpec((tm, tn), lambda j, i: (i, j))] * len(outs), [_sds((t, n), d) for d in outs], [],
                 [x, w, w2], host)


def mm_nn_k(name, x, w, tm, tk, out_dtype=F32, host=None):
    t, kdim = x.shape
    n = w.shape[1]
    grid = (t // tm, 1, kdim // tk)
    return _mm(name, grid, [x, w],
               [pl.BlockSpec((tm, tk), lambda i, j, k: (i, k)), pl.BlockSpec((tk, n), lambda i, j, k: (k, 0))],
               [NN], [pl.BlockSpec((tm, n), lambda i, j, k: (i, 0))], [_sds((t, n), out_dtype)],
               acc_shape=(tm, n), host=host)[0]


def mm_nt(name, xs, ws, tm, tn, outs=(F32,), epilogue=None, extras=(), host=None):
    t = xs[0].shape[0]
    specs, ops = [], []
    for x, w in zip(xs, ws):
        kdim = x.shape[1]
        specs.append(pl.BlockSpec((tm, kdim), lambda j, i, k: (i, 0)))
        if w.ndim == 3:
            assert tn == w.shape[1]
            n = w.shape[0] * w.shape[1]
            specs.append(pl.BlockSpec((None, tn, kdim), lambda j, i, k: (j, 0, 0)))
        else:
            n = w.shape[0]
            specs.append(pl.BlockSpec((tn, kdim), lambda j, i, k: (j, 0)))
        ops += [x, w]
    ospec = pl.BlockSpec((tm, tn), lambda j, i, k: (i, j))
    return _mm(name, (n // tn, t // tm, 1), ops, specs, [NT] * len(xs), [ospec] * len(outs),
               [_sds((t, n), d) for d in outs], epilogue=epilogue, extras=extras,
               extra_specs=[ospec] * len(extras), host=host)


def mm_nt_k(name, xs, ws, tm, tn, out_dtype=F32, host=None):
    t = xs[0].shape[0]
    s, n, ns = ws[0].shape
    specs, ops = [], []
    for x, w in zip(xs, ws):
        specs.append(pl.BlockSpec((tm, ns), lambda i, j, k: (i, k)))
        specs.append(pl.BlockSpec((None, tn, ns), lambda i, j, k: (k, j, 0)))
        ops += [x, w]
    return _mm(name, (t // tm, n // tn, s), ops, specs, [NT] * len(xs),
               [pl.BlockSpec((tm, tn), lambda i, j, k: (i, j))], [_sds((t, n), out_dtype)], acc_shape=(tm, tn),
               host=host)[0]


def mm_tn(name, x, y, tm, tn, tk, out_dtype, shard_rows=False, shard_cols=False, host=None):
    t, m = x.shape
    n = y.shape[1]
    grid = (m // tm, n // tn, t // tk)
    if shard_cols:
        ospec = pl.BlockSpec((None, tm, tn), lambda i, j, k: (j, i, 0))
        oshape = _sds((n // tn, m, tn), out_dtype)
    elif shard_rows:
        ospec = pl.BlockSpec((None, tm, tn), lambda i, j, k: (i, 0, j))
        oshape = _sds((m // tm, tm, n), out_dtype)
    else:
        ospec = pl.BlockSpec((tm, tn), lambda i, j, k: (i, j))
        oshape = _sds((m, n), out_dtype)
    return _mm(name, grid, [x, y],
               [pl.BlockSpec((tk, tm), lambda i, j, k: (k, i)), pl.BlockSpec((tk, tn), lambda i, j, k: (k, j))],
               [TN], [ospec], [oshape], acc_shape=(tm, tn), host=host)[0]


def _decay_logs():
    return [math.log(1.0 - 2.0 ** (-5.0 - h)) for h in range(HEADS)]


def _log_decay(h):
    lg = jnp.float32(_decay_logs()[0])
    for i in range(1, HEADS):
        lg = jnp.where(h == i, jnp.float32(_decay_logs()[i]), lg)
    return lg


def _decayed_scores(q, k, lg):
    s = lax.dot_general(q, k, NT, preferred_element_type=F32)
    row = lax.broadcasted_iota(jnp.int32, s.shape, 0)
    col = lax.broadcasted_iota(jnp.int32, s.shape, 1)
    dec = jnp.where(col <= row, jnp.exp(jnp.maximum(row - col, 0).astype(F32) * lg), 0.0)
    return s * dec, dec


def _causal(s):
    row = lax.broadcasted_iota(jnp.int32, s.shape, 0)
    col = lax.broadcasted_iota(jnp.int32, s.shape, 1)
    return jnp.where(col <= row, s, -1e30)


def _attn_fwd_step(qi, blk, scale, q_ref, k_ref, v_ref, o_ref, lse_ref, m_ref, l_ref, acc_ref):
    qv = q_ref[...]
    m_ref[...] = jnp.full_like(m_ref, -1e30)
    l_ref[...] = jnp.zeros_like(l_ref)
    acc_ref[...] = jnp.zeros_like(acc_ref)

    def keys(start, n, diag_from):
        rows = pl.ds(pl.multiple_of(start, blk), n)
        s = lax.dot_general(qv, k_ref[rows, :], NT, preferred_element_type=F32) * scale
        if diag_from is not None:
            row = lax.broadcasted_iota(jnp.int32, s.shape, 0)
            col = lax.broadcasted_iota(jnp.int32, s.shape, 1)
            s = jnp.where(col - diag_from <= row, s, -1e30)
        m = m_ref[...]
        m_new = jnp.maximum(m, jnp.max(s, axis=-1, keepdims=True))
        p = jnp.exp(s - m_new)
        alpha = jnp.exp(m - m_new)
        m_ref[...] = m_new
        l_ref[...] = alpha * l_ref[...] + jnp.sum(p, axis=-1, keepdims=True)
        acc_ref[...] = alpha * acc_ref[...] + jnp.dot(p.astype(BF16), v_ref[rows, :], preferred_element_type=F32)

    @pl.loop(0, qi // 2)
    def _(j):
        keys(j * (2 * blk), 2 * blk, None)

    @pl.when(qi % 2 == 1)
    def _():
        keys((qi - 1) * blk, 2 * blk, blk)

    @pl.when(qi % 2 == 0)
    def _():
        keys(qi * blk, blk, 0)

    l = l_ref[...]
    o_ref[...] = acc_ref[...] / l
    lse_ref[...] = jnp.broadcast_to(m_ref[...] + jnp.log(l), (blk, HEAD_DIM))


def mix_fwd(name, q, k, v, rq, rk, rv, blk, scale, host=None):
    t = q.shape[0]
    dq = q.shape[1] // HEADS
    nq = t // blk

    def body(q_ref, k_ref, v_ref, rq_ref, rk_ref, rv_ref, o_ref, lse_ref, ro_ref, st_ref, m_ref, l_ref, acc_ref, state):
        h, i = pl.program_id(0), pl.program_id(1)

        @pl.when(h < HEADS)
        def _():
            _attn_fwd_step(i, blk, scale, q_ref, k_ref, v_ref, o_ref, lse_ref, m_ref, l_ref, acc_ref)

        @pl.when(h >= HEADS)
        def _():
            _ret_fwd_step(h - HEADS, i, blk, rq_ref, rk_ref, rv_ref, ro_ref, st_ref, state)

    mla = lambda w, whole=False: pl.BlockSpec(
        (t if whole else blk, w),
        lambda h, i: (0 if whole else jnp.where(h < HEADS, i, nq - 1), jnp.minimum(h, HEADS - 1)))
    ret = lambda rows: pl.BlockSpec(
        (rows, HEAD_DIM), lambda h, i: (jnp.where(h >= HEADS, i, 0), jnp.maximum(h - HEADS, 0)))
    wide = _sds((t, HEADS * HEAD_DIM), F32)
    return _call(body, name, (2 * HEADS, nq),
                 [mla(dq), mla(dq, True), mla(HEAD_DIM, True), ret(blk), ret(blk), ret(blk)],
                 [mla(HEAD_DIM), mla(HEAD_DIM), ret(blk), ret(HEAD_DIM)],
                 [wide, wide, wide, _sds((nq * HEAD_DIM, HEADS * HEAD_DIM), F32)],
                 [pltpu.VMEM((blk, 1), F32), pltpu.VMEM((blk, 1), F32), pltpu.VMEM((blk, HEAD_DIM), F32),
                  pltpu.VMEM((HEAD_DIM, HEAD_DIM), F32)],
                 [q, k, v, rq, rk, rv], host)


def attn_bwd(name, q, k, v, do, o, lse, blk, scale, host=None):
    t = q.shape[0]
    dq_w = q.shape[1] // HEADS
    nb = t // blk

    def body(q_ref, k_ref, v_ref, do_ref, o_ref, lse_ref, dq_ref, dk_out, dv_out, dk_ref, dv_ref):
        ki = pl.program_id(1)
        kv = k_ref[...]
        vv = v_ref[...]

        @pl.when(ki == 0)
        def _():
            dq_ref[...] = jnp.zeros_like(dq_ref)

        def queries(start, n, diag):
            rows = pl.ds(pl.multiple_of(start, blk), n)
            qv, dov = q_ref[rows, :], do_ref[rows, :]
            s = lax.dot_general(qv, kv, NT, preferred_element_type=F32) * scale
            if diag:
                s = _causal(s)
            p = jnp.exp(s - lse_ref[rows, :][:, :1])
            dp = lax.dot_general(dov, vv, NT, preferred_element_type=F32)
            delta = jnp.sum(dov.astype(F32) * o_ref[rows, :], axis=-1, keepdims=True)
            ds = p * (dp - delta) * scale
            pb, dsb = p.astype(BF16), ds.astype(BF16)
            dv_ref[...] += lax.dot_general(pb, dov, TN, preferred_element_type=F32)
            dk_ref[...] += lax.dot_general(dsb, qv, TN, preferred_element_type=F32)
            dq_ref[rows, :] += jnp.dot(dsb, kv, preferred_element_type=F32)

        dk_ref[...] = jnp.zeros_like(dk_ref)
        dv_ref[...] = jnp.zeros_like(dv_ref)
        queries(ki * blk, blk, True)
        later = nb - 1 - ki

        @pl.when(later % 2 == 1)
        def _():
            queries((ki + 1) * blk, blk, False)

        @pl.loop(0, later // 2)
        def _(j):
            queries((ki + 1 + later % 2 + 2 * j) * blk, 2 * blk, False)

        dk_out[...] = dk_ref[...].astype(dk_out.dtype)
        dv_out[...] = dv_ref[...].astype(dv_out.dtype)

    full = lambda w: pl.BlockSpec((t, w), lambda h, j: (0, h))
    blkd = lambda w: pl.BlockSpec((blk, w), lambda h, j: (j, h))
    return _call(body, name, (HEADS, nb),
                 [full(dq_w), blkd(dq_w), blkd(HEAD_DIM), full(HEAD_DIM), full(HEAD_DIM), full(HEAD_DIM)],
                 [full(dq_w), blkd(dq_w), blkd(HEAD_DIM)],
                 [_sds(q.shape, F32), _sds(k.shape, BF16), _sds(v.shape, BF16)],
                 [pltpu.VMEM((blk, dq_w), F32), pltpu.VMEM((blk, HEAD_DIM), F32)], [q, k, v, do, o, lse], host)


def _chunk_decays(lg, blk):
    row = lax.broadcasted_iota(jnp.int32, (blk, HEAD_DIM), 0).astype(F32)
    return jnp.exp(lg * (row + 1.0)), jnp.exp(lg * (blk - 1.0 - row)), jnp.exp(lg * blk * jnp.ones((1, HEAD_DIM), F32))


def _ret_fwd_step(h, i, blk, q_ref, k_ref, v_ref, o_ref, st_ref, state):
    lg = _log_decay(h)

    @pl.when(i == 0)
    def _():
        state[...] = jnp.zeros_like(state)

    qv, kv, vv = q_ref[...], k_ref[...], v_ref[...]
    before = state[...]
    st_ref[...] = before
    p, _ = _decayed_scores(qv, kv, lg)
    xi, zeta, g_blk = _chunk_decays(lg, blk)
    o_ref[...] = (jnp.dot(p.astype(BF16), vv, preferred_element_type=F32)
                  + jnp.dot(qv, before.astype(BF16), preferred_element_type=F32) * xi)
    kz = (kv.astype(F32) * zeta).astype(BF16)
    state[...] = before * g_blk + lax.dot_general(kz, vv, TN, preferred_element_type=F32)


def ret_bwd(name, q, k, v, do, states, blk, host=None):
    t = q.shape[0]
    nb = t // blk

    def body(q_ref, k_ref, v_ref, do_ref, st_ref, dq_ref, dk_ref, dv_ref, dstate):
        h, i = pl.program_id(0), pl.program_id(1)
        lg = _log_decay(h)

        @pl.when(i == 0)
        def _():
            dstate[...] = jnp.zeros_like(dstate)

        qv, kv, vv, dov = q_ref[...], k_ref[...], v_ref[...], do_ref[...]
        before = st_ref[...].astype(BF16)
        after_grad = dstate[...]
        p, dec = _decayed_scores(qv, kv, lg)
        ds = lax.dot_general(dov, vv, NT, preferred_element_type=F32) * dec
        pb, dsb = p.astype(BF16), ds.astype(BF16)
        xi, zeta, g_blk = _chunk_decays(lg, blk)
        dox = (dov.astype(F32) * xi).astype(BF16)
        kz = (kv.astype(F32) * zeta).astype(BF16)
        agb = after_grad.astype(BF16)
        dv = lax.dot_general(pb, dov, TN, preferred_element_type=F32) + jnp.dot(kz, agb, preferred_element_type=F32)
        dq = (jnp.dot(dsb, kv, preferred_element_type=F32)
              + lax.dot_general(dox, before, NT, preferred_element_type=F32))
        dk = (lax.dot_general(dsb, qv, TN, preferred_element_type=F32)
              + lax.dot_general(vv, agb, NT, preferred_element_type=F32) * zeta)
        for ref, val in ((dq_ref, dq), (dk_ref, dk), (dv_ref, dv)):
            ref[...] = val.astype(ref.dtype)
        dstate[...] = after_grad * g_blk + lax.dot_general(qv, dox, TN, preferred_element_type=F32)

    hspec = pl.BlockSpec((blk, HEAD_DIM), lambda h, i: (nb - 1 - i, h))
    return _call(body, name, (HEADS, nb),
                 [hspec] * 4 + [pl.BlockSpec((HEAD_DIM, HEAD_DIM), lambda h, i: (nb - 1 - i, h))],
                 [hspec] * 3, [_sds(q.shape, BF16)] * 3, [pltpu.VMEM((HEAD_DIM, HEAD_DIM), F32)],
                 [q, k, v, do, states], host)


CHIP_FLIPS = ((1, 0), (0, 1), (1, 1))
CHIP_BARRIER = 1
ALL_BARRIER = 2


def _position():
    return lax.axis_index("x"), lax.axis_index("y"), lax.axis_index("c")


def _chip_peers(pos):
    x, y, c = pos
    return [(x ^ fx, y ^ fy, c) for fx, fy in CHIP_FLIPS]


class _Comm:
    def __init__(self, ins, out_shape, plan, n_remote, n_local, collective_id):
        self.ins, self.out_shape, self.plan = list(ins), list(out_shape), plan
        self.n_remote, self.n_local, self.collective_id = n_remote, n_local, collective_id
        self.result = None

    def scratch(self):
        return [pltpu.SemaphoreType.DMA((self.n_remote,)), pltpu.SemaphoreType.DMA((self.n_remote,)),
                pltpu.SemaphoreType.DMA((self.n_local,))]

    def _copies(self, in_refs, out_refs, sems):
        send_sems, recv_sems, local_sems = sems
        pos = _position()
        p = self.plan(pos, in_refs, out_refs)

        def remote(k, src, dst, dev):
            return pltpu.make_async_remote_copy(src_ref=src, dst_ref=dst, send_sem=send_sems.at[k],
                                                recv_sem=recv_sems.at[k], device_id=dev, device_id_type=MESH)

        local = [pltpu.make_async_copy(s, d, local_sems.at[i]) for i, (s, d) in enumerate(p["local"])]
        out = [remote(k, s, d, dev) for k, (s, d, dev) in enumerate(p["sends"])]
        arrivals = [functools.partial(remote, k, d, d, pos) for k, d in enumerate(p["recvs"])]
        return local, out, arrivals, p["peers"]

    def start(self, in_refs, out_refs, sems):
        local, out, _, peers = self._copies(in_refs, out_refs, sems)
        barrier = pltpu.get_barrier_semaphore()
        for peer in peers:
            pl.semaphore_signal(barrier, inc=1, device_id=peer, device_id_type=MESH)
        pl.semaphore_wait(barrier, len(peers))
        for cp in local + out:
            cp.start()

    def finish(self, in_refs, out_refs, sems):
        local, out, arrivals, _ = self._copies(in_refs, out_refs, sems)
        for make in arrivals:
            make().wait_recv()
        for cp in out:
            cp.wait_send()
        for cp in local:
            cp.wait()

    def run(self, name):
        n_in, n_out = len(self.ins), len(self.out_shape)

        def body(*refs):
            comm = (refs[:n_in], refs[n_in:n_in + n_out], refs[n_in + n_out:])
            self.start(*comm)
            self.finish(*comm)

        hbm = pl.BlockSpec(memory_space=pl.ANY)
        self.result = pl.pallas_call(
            body, name=name, in_specs=[hbm] * n_in, out_specs=[hbm] * n_out, out_shape=self.out_shape,
            scratch_shapes=self.scratch(),
            compiler_params=pltpu.CompilerParams(collective_id=self.collective_id))(*self.ins)
        return self.result


def _half_rows(c, rows):
    r2 = rows // 2
    return pl.ds(pl.multiple_of(c * r2, math.gcd(r2, LANES)), r2)


def _half(ref, c, rows, lead=()):
    return ref.at[(*lead, _half_rows(c, rows))]


def gather_halves(shards):
    def plan(pos, ins, outs):
        x, y, c = pos
        me = 2 * x + y
        p = dict(local=[], sends=[], recvs=[], peers=_chip_peers(pos))
        for a, (src, dst) in enumerate(zip(ins, outs)):
            rows = shards[a].shape[0]
            p["local"].append((src, dst.at[me]))
            for px, py, _ in p["peers"]:
                p["sends"].append((_half(src, c, rows), _half(dst, c, rows, (me,)), (px, py, c)))
                p["recvs"].append(_half(dst, c, rows, (2 * px + py,)))
        return p

    return _Comm(shards, [_sds((N_CHIPS, *s.shape), s.dtype) for s in shards], plan,
                 n_remote=3 * len(shards), n_local=len(shards), collective_id=CHIP_BARRIER)


def chip_exchange(parts):
    def plan(pos, ins, outs):
        x, y, c = pos
        me = 2 * x + y
        p = dict(local=[], sends=[], recvs=[], peers=_chip_peers(pos))
        for src, dst in zip(ins, outs):
            p["local"].append((src.at[me], dst.at[me]))
            for px, py, _ in p["peers"]:
                peer = 2 * px + py
                p["sends"].append((src.at[peer], dst.at[me], (px, py, c)))
                p["recvs"].append(dst.at[peer])
        return p

    return _Comm(parts, [_sds(g.shape, g.dtype) for g in parts], plan, n_remote=3 * len(parts), n_local=len(parts),
                 collective_id=CHIP_BARRIER)


def all_gather_devices(v):
    flips = [(fx, fy, fc) for fx in (0, 1) for fy in (0, 1) for fc in (0, 1)][1:]

    def plan(pos, ins, outs):
        x, y, c = pos
        me = 4 * x + 2 * y + c
        p = dict(local=[(ins[0], outs[0].at[me])], sends=[], recvs=[],
                 peers=[(x ^ fx, y ^ fy, c ^ fc) for fx, fy, fc in flips])
        for px, py, pc in p["peers"]:
            p["sends"].append((ins[0], outs[0].at[me], (px, py, pc)))
            p["recvs"].append(outs[0].at[4 * px + 2 * py + pc])
        return p

    return _Comm([v], [_sds((8, *v.shape), v.dtype)], plan, n_remote=7, n_local=1,
                 collective_id=ALL_BARRIER).run("small_all_gather")[0]


SWAP_CHUNK_BYTES = 3 * 2 ** 19


def _sibling_stream(t, n, value, consume, sbuf, rbuf, send_sems, recv_sems, credits):
    x, y, c = _position()
    sib = (x, y, 1 - c)

    def copy(slot):
        return pltpu.make_async_remote_copy(src_ref=sbuf.at[slot], dst_ref=rbuf.at[slot], send_sem=send_sems.at[slot],
                                            recv_sem=recv_sems.at[slot], device_id=sib, device_id_type=MESH)

    slot = t % 2

    @pl.when(t == 0)
    def _():
        barrier = pltpu.get_barrier_semaphore()
        pl.semaphore_signal(barrier, inc=1, device_id=sib, device_id_type=MESH)
        pl.semaphore_wait(barrier, 1)

    @pl.when(jnp.logical_and(t >= 2, t < n))
    def _():
        copy(slot).wait_send()
        pl.semaphore_wait(credits.at[slot], 1)

    @pl.when(t < n)
    def _():
        sbuf[slot] = value
        copy(slot).start()

    @pl.when(t >= 1)
    def _():
        prev = 1 - slot
        copy(prev).wait_recv()
        consume(sbuf[prev], rbuf[prev])

        @pl.when(t + 1 < n)
        def _():
            pl.semaphore_signal(credits.at[prev], inc=1, device_id=sib, device_id_type=MESH)

    @pl.when(t == n)
    def _():
        copy(1 - slot).wait_send()
        if n > 1:
            copy(slot).wait_send()


def _swap_scratch(rows, cols, dtype):
    return [pltpu.VMEM((2, rows, cols), dtype), pltpu.VMEM((2, rows, cols), dtype),
            pltpu.SemaphoreType.DMA((2,)), pltpu.SemaphoreType.DMA((2,)), pltpu.SemaphoreType.REGULAR((2,))]


def _chunk_rows(rows, cols, dtype, nbytes=SWAP_CHUNK_BYTES):
    return _tile(rows, max(16, nbytes // (cols * jnp.dtype(dtype).itemsize)), 16)


def pair_add(name, g):
    s, r, c_ = g.shape
    r2 = r // 2
    cr = _chunk_rows(r2, c_, g.dtype)
    nj = r2 // cr

    n = s * nj

    def body(core, mine_ref, theirs_ref, o_ref, *scratch):
        def consume(_, got):
            o_ref[...] = (mine_ref[...].astype(F32) + got.astype(F32)).astype(o_ref.dtype)

        _sibling_stream(pl.program_id(0), n, theirs_ref[...], consume, *scratch)

    sent = lambda t: jnp.minimum(t, n - 1)
    used = lambda t: jnp.maximum(t - 1, 0)
    grid_spec = pltpu.PrefetchScalarGridSpec(
        num_scalar_prefetch=1, grid=(n + 1,),
        in_specs=[pl.BlockSpec((None, cr, c_), lambda t, core: (used(t) // nj, core[0] * nj + used(t) % nj, 0)),
                  pl.BlockSpec((None, cr, c_), lambda t, core: (sent(t) // nj, (1 - core[0]) * nj + sent(t) % nj, 0))],
        out_specs=pl.BlockSpec((None, cr, c_), lambda t, core: (used(t) // nj, used(t) % nj, 0)),
        scratch_shapes=_swap_scratch(cr, c_, g.dtype))
    core = lax.axis_index("c").astype(jnp.int32).reshape(1)
    return pl.pallas_call(body, name=name, grid_spec=grid_spec, out_shape=_sds((s, r2, c_), g.dtype),
                          compiler_params=_params("arbitrary", collective_id=SIBLING_BARRIER))(core, g, g)


def _adamw_math(w, g, m, v):
    m = ADAM_B1 * m + (1.0 - ADAM_B1) * g
    v = ADAM_B2 * v + (1.0 - ADAM_B2) * (g * g)
    m_hat = m / (1.0 - ADAM_B1 ** ADAM_STEP)
    v_hat = v / (1.0 - ADAM_B2 ** ADAM_STEP)
    return -ADAM_LR * (m_hat / (jnp.sqrt(v_hat) + ADAM_EPS) + ADAM_WD * w), m, v


def reduce_join(name, p):
    s, r2, c_ = p.shape
    cr = _chunk_rows(r2, c_, F32)
    nj = r2 // cr

    def body(core, p_ref, o_ref, *scratch):
        acc = p_ref[0].astype(F32)
        for i in range(1, s):
            acc = acc + p_ref[i].astype(F32)

        def consume(own, got):
            c = core[0]
            o_ref[c] = own
            o_ref[1 - c] = got

        _sibling_stream(pl.program_id(0), nj, acc, consume, *scratch)

    grid_spec = pltpu.PrefetchScalarGridSpec(
        num_scalar_prefetch=1, grid=(nj + 1,),
        in_specs=[pl.BlockSpec((s, cr, c_), lambda t, core: (0, jnp.minimum(t, nj - 1), 0))],
        out_specs=pl.BlockSpec((2, cr, c_), lambda t, core: (0, jnp.maximum(t - 1, 0), 0)),
        scratch_shapes=_swap_scratch(cr, c_, F32))
    core = lax.axis_index("c").astype(jnp.int32).reshape(1)
    out = pl.pallas_call(body, name=name, grid_spec=grid_spec, out_shape=_sds((2, r2, c_), F32),
                         compiler_params=_params("arbitrary", collective_id=SIBLING_BARRIER))(core, p)
    return out.reshape(2 * r2, c_)


def sibling_fill(name, buf):
    s, r, c_ = buf.shape
    r2 = r // 2
    cr = _chunk_rows(r2, c_, buf.dtype)
    nj = r2 // cr
    n_peers = len(CHIP_FLIPS)

    n = n_peers * nj

    def body(where, in_ref, o_ref, *scratch):
        def consume(_, got):
            o_ref[...] = got

        _sibling_stream(pl.program_id(0), n, in_ref[...], consume, *scratch)

    sent = lambda t: jnp.minimum(t, n - 1)
    used = lambda t: jnp.maximum(t - 1, 0)
    grid_spec = pltpu.PrefetchScalarGridSpec(
        num_scalar_prefetch=1, grid=(n + 1,),
        in_specs=[pl.BlockSpec((None, cr, c_),
                               lambda t, where: (where[sent(t) // nj], where[n_peers] * nj + sent(t) % nj, 0))],
        out_specs=pl.BlockSpec((None, cr, c_),
                               lambda t, where: (where[used(t) // nj], (1 - where[n_peers]) * nj + used(t) % nj, 0)),
        scratch_shapes=_swap_scratch(cr, c_, buf.dtype))
    x, y, c = _position()
    where = jnp.stack([2 * (x ^ fx) + (y ^ fy) for fx, fy in CHIP_FLIPS] + [c]).astype(jnp.int32)
    return pl.pallas_call(body, name=name, grid_spec=grid_spec, out_shape=_sds(buf.shape, buf.dtype),
                          input_output_aliases={1: 0},
                          compiler_params=_params("arbitrary", collective_id=SIBLING_BARRIER))(where, buf)


def sum_slots(name, p, out_dtype):
    s, r, c = p.shape
    tr = _tile(r, 256, 16)

    def body(p_ref, o_ref):
        acc = p_ref[0].astype(F32)
        for i in range(1, s):
            acc = acc + p_ref[i].astype(F32)
        o_ref[...] = acc.astype(o_ref.dtype)

    return pl.pallas_call(
        body, name=name, grid=(r // tr,), in_specs=[pl.BlockSpec((s, tr, c), lambda i: (0, i, 0))],
        out_specs=pl.BlockSpec((tr, c), lambda i: (i, 0)), out_shape=_sds((r, c), out_dtype),
        compiler_params=_params("arbitrary"),
    )(p)


def adamw(name, w, g, m, v):
    r, c = w.shape
    outs, _ = _rowwise(name, lambda _, w, g, m, v: ([*_adamw_math(w, g, m, v), g], []), [w, g, m, v], [],
                       [(c, F32)] * 4, [], _tile(r, 256, 8))
    return outs


RET_SCALE = HEAD_DIM ** -0.5
MLA_SCALE = (HEAD_DIM + MLA_ROPE) ** -0.5
GRAD_DT = BF16
IN_RET = 4 * HEADS * HEAD_DIM
IN_MLA = MLA_Q_RANK + MLA_KV_RANK + MLA_ROPE
IN_MLA_PAD = IN_MLA + 64


def _heads(fn):
    return jnp.concatenate([fn(h) for h in range(HEADS)], axis=1)


def _head(a, h, stride=HEAD_DIM, off=0):
    return a[:, h * stride + off:h * stride + off + HEAD_DIM]


def _group_norm(o):
    rs = [lax.rsqrt(jnp.mean(_head(o, h) * _head(o, h), axis=-1, keepdims=True) + EPS) for h in range(HEADS)]
    return _heads(lambda h: _head(o, h) * rs[h]), rs


class _Alone:
    def host(self, kernel_name):
        return None

    def done(self, kernel_name, w):
        pass

    def grads(self, g):
        pass


def _ffn_fwd(tag, n, w, k, tm, tmk, plan):
    gate, up, down = tag + "_gate", tag + "_up", tag + "_down"
    if "wu" + k in w:
        g, u, a = mm_nn(up, n, w["wg" + k], tm, outs=[BF16] * 3, w2=w["wu" + k],
                        epilogue=lambda g, u: (g, u, _silu(g) * u), host=plan.host(up))
    else:
        (g,) = mm_nn(gate, n, w["wg" + k], tm, out_dtype=BF16, host=plan.host(gate))
        plan.done(gate, w)
        u, a = mm_nn(up, n, w["wu" + k], tm, outs=[BF16] * 2, extras=(g,),
                     epilogue=lambda u, g: (u, _silu(g.astype(F32)) * u), host=plan.host(up))
    plan.done(up, w)
    ff = a.shape[1]
    wd = w["wd" + k]
    f = mm_nn_k(down, a, wd.reshape(ff, wd.shape[2]), tmk, _tile(ff, 1408), host=plan.host(down))
    plan.done(down, w)
    return g, u, a, f


def _ffn_bwd(tag, df, n, g, u, a, wg, wu, wd, tm, tmk, tk, dt, plan):
    ns = wg.shape[2]

    def gate_grads(da, g, u):
        g, u = g.astype(F32), u.astype(F32)
        return da * u * _dsilu(g), da * _silu(g)

    def hosted(kernel_name, call):
        out = call(plan.host(kernel_name))
        plan.done(kernel_name, None)
        return out

    k = tag[-1]
    dg, du = hosted(tag + "_da", lambda h: mm_nt(tag + "_da", [df], [wd], tm, ns, outs=(BF16, BF16),
                                                 epilogue=gate_grads, extras=(g, u), host=h))
    dwg = hosted(tag + "_dwg", lambda h: mm_tn(tag + "_dwg", n, dg, dt, ns, tk, GRAD_DT, shard_cols=True, host=h))
    plan.grads({"wg" + k: dwg})
    dwu = hosted(tag + "_dwu", lambda h: mm_tn(tag + "_dwu", n, du, dt, ns, tk, GRAD_DT, shard_cols=True, host=h))
    plan.grads({"wu" + k: dwu})
    dwd = hosted(tag + "_dwd", lambda h: mm_tn(tag + "_dwd", a, df, ns, dt, tk, GRAD_DT, shard_rows=True, host=h))
    plan.grads({"wd" + k: dwd})
    dn = hosted(tag + "_dn", lambda h: mm_nt_k(tag + "_dn", [dg, du], [wg, wu], tmk, dt, out_dtype=BF16, host=h))
    return dn


def _local_step(x, tgt, meta, w, nw, plan):
    seq, d = x.shape
    t_real = N_META + seq
    tp = -(-t_real // LANES) * LANES
    zpad = jnp.zeros((tp - t_real, d), F32)
    h0 = jnp.concatenate([meta, x, zpad], axis=0)
    tgt_p = jnp.concatenate([jnp.zeros((N_META, d), F32), tgt, zpad], axis=0)
    cos_r, sin_r, cos_m, sa, sb = _rope_tables(tp)
    tm = _tile(tp, 512)
    tmt = _tile(tp, 768, 16)
    tmk = _tile(tp, 1408)
    tk = tp
    blk = tm
    dt = _tile(d, 1024)
    hw = HEADS * HEAD_DIM
    qw = HEADS * MLA_QK_PAD

    (n1,), _ = _rowwise("ffn1_norm", lambda r0, h, g: ([_rms(h, g)], []), [h0], [nw["ffn1_pre_norm"]],
                        [(d, BF16)], [], tm)
    g1, u1, a1, f1 = _ffn_fwd("ffn1", n1, w, "1", tm, tmk, plan)

    def post_ffn1(r0, h, f, post, pre):
        h1 = h + 0.5 * _rms(f, post)
        return [h1, _rms(h1, pre)], []

    (h1, un), _ = _rowwise("mix_norm", post_ffn1, [h0, f1], [nw["ffn1_post_norm"], nw["mix_pre_norm"]],
                           [(d, F32), (d, BF16)], [], tm)
    (proj_r,) = mm_nn("proj_r", un, w["w_r"], tm, n_block=_tile(IN_RET, 1024), host=plan.host("proj_r"))
    plan.done("proj_r", w)
    (proj_c,) = mm_nn("proj_c", un, w["w_c"], tm)

    def split_proj(r0, pr, pc, cr, sr, cm, ta, tb, qn, kvn):
        rq = _heads(lambda h: _rope_r(_head(pr, h), cr, sr))
        rk = _heads(lambda h: _rope_r(_head(pr, h, off=hw), cr, sr) * RET_SCALE)
        rv = pr[:, 2 * hw:3 * hw]
        cqn = _rms(pc[:, :MLA_Q_RANK], qn)
        ckvn = _rms(pc[:, MLA_Q_RANK:MLA_Q_RANK + MLA_KV_RANK], kvn)
        krr = _rope_m(pc[:, MLA_Q_RANK + MLA_KV_RANK:], cm, ta, tb)
        return [rq, rk, rv, cqn, ckvn, krr], []

    (rq, rk, rv, cqn, ckvn, krr), _ = _rowwise(
        "split_proj", split_proj, [proj_r, proj_c, cos_r, sin_r, cos_m, sa, sb],
        [nw["mla_q_norm"], nw["mla_kv_norm"]],
        [(hw, BF16), (hw, BF16), (hw, BF16), (MLA_Q_RANK, BF16), (MLA_KV_RANK, BF16), (LANES, F32)], [], tm)
    (qp,) = mm_nn("q_up", cqn, w["wuq"], tm)
    (kn,) = mm_nn("k_up", ckvn, w["wuk"], tm)
    (vv,) = mm_nn("v_up", ckvn, w["wuv"], tm, out_dtype=BF16)

    def build_qk(r0, qp, kn, krr, cm, ta, tb):
        qc = jnp.concatenate(
            [part for h in range(HEADS)
             for part in (_head(qp, h, MLA_QK_PAD), _rope_m(_head(qp, h, MLA_QK_PAD, HEAD_DIM), cm, ta, tb))], axis=1)
        kc = jnp.concatenate([part for h in range(HEADS) for part in (_head(kn, h), krr)], axis=1)
        return [qc, kc], []

    (qc, kc), _ = _rowwise("build_qk", build_qk, [qp, kn, krr, cos_m, sa, sb], [], [(qw, BF16), (qw, BF16)], [], tm)
    o_m, lse, o_r, ret_states = mix_fwd("mix_fwd", qc, kc, vv, rq, rk, rv, blk, MLA_SCALE, host=plan.host("mix_fwd"))
    plan.done("mix_fwd", w)

    def gate_mix(r0, rg, o_r, o_m, gn):
        y, _ = _group_norm(o_r)
        return [jnp.concatenate([_silu(rg) * (y * gn), o_m], axis=1)], []

    (mixcat,), _ = _rowwise("gate_mix", gate_mix, [(proj_r, hw, 3), o_r, o_m], [nw["ret_group_norm"]],
                            [(2 * hw, BF16)], [], tm)
    (mix,) = mm_nn("mix_out", mixcat, w["w_out"], tm, n_block=dt)

    def post_mix(r0, h, m, post, pre):
        h2 = h + _rms(m, post)
        return [h2, _rms(h2, pre)], []

    (h2, n3), _ = _rowwise("ffn2_norm", post_mix, [h1, mix], [nw["mix_post_norm"], nw["ffn2_pre_norm"]],
                           [(d, F32), (d, BF16)], [], tm)
    g2, u2, a2, f2 = _ffn_fwd("ffn2", n3, w, "2", tm, tmk, plan)

    def loss_head(r0, h, f, t, post):
        h3 = h + 0.5 * _rms(f, post)
        row = r0 + lax.broadcasted_iota(jnp.int32, h3.shape, 0)
        err = jnp.where(row >= N_META, jnp.where(row < t_real, h3 - t, 0.0), 0.0)
        dh3 = err / d
        df, dpost = _rms_bwd(f, post, 0.5 * dh3)
        return [dh3, df], [_colsum(err * err), _colsum(dpost)]

    (dh3, df2), (loss_vec, d_post2) = _rowwise("loss_head", loss_head, [h2, f2, tgt_p], [nw["ffn2_post_norm"]],
                                               [(d, F32), (d, BF16)], [d, d], tm)
    loss = 0.5 * jnp.sum(loss_vec) / d
    dn3 = _ffn_bwd("ffn2", df2, n3, g2, u2, a2, w["wg2"], w["wu2"], w["wd2"], tmt, tmk, tk, dt, plan)

    def back_mix_norm(r0, h, m, dh3, dn, pre, post):
        dx, dpre = _rms_bwd(h, pre, dn)
        dh2 = dh3 + dx
        dm, dpost = _rms_bwd(m, post, dh2)
        return [dh2, dm], [_colsum(dpre), _colsum(dpost)]

    (dh2, dmix), (d_pre2, d_mix_post) = _rowwise(
        "back_mix_norm", back_mix_norm, [h2, mix, dh3, dn3], [nw["ffn2_pre_norm"], nw["mix_post_norm"]],
        [(d, F32), (d, BF16)], [d, d], tm)
    (dmixcat,) = mm_nt("mix_dx", [dmix], [w["w_out"]], tmt, _tile(2 * hw, 512), outs=(BF16,))
    plan.grads(dict(w_out=mm_tn("mix_dw", mixcat, dmix, 2 * hw // N_CHIPS, dt, tk, GRAD_DT, shard_rows=True)))

    def back_gate(r0, dmc, rg, o_r, gn):
        d_ret, d_om = dmc[:, :hw], dmc[:, hw:]
        yh, rs = _group_norm(o_r)
        d_rg = d_ret * (yh * gn) * _dsilu(rg)
        dy = d_ret * _silu(rg)
        gyh = dy * gn
        d_or = _heads(lambda h: rs[h] * (_head(gyh, h) - _head(yh, h) * jnp.mean(_head(gyh, h) * _head(yh, h),
                                                                                    axis=-1, keepdims=True)))
        return [d_or, d_rg, d_om], [_colsum(dy * yh)]

    (d_or, d_rg, d_om), (d_gn,) = _rowwise("back_gate", back_gate, [dmixcat, (proj_r, hw, 3), o_r],
                                           [nw["ret_group_norm"]], [(hw, BF16), (hw, F32), (hw, BF16)], [hw], tm)
    dqc, dkc, dvv = attn_bwd("mla_bwd", qc, kc, vv, d_om, o_m, lse, blk, MLA_SCALE, host=plan.host("mla_bwd"))
    plan.done("mla_bwd", None)
    drq, drk, drv = ret_bwd("ret_bwd", rq, rk, rv, d_or, ret_states, blk)

    def back_qk(r0, dqc, dkc, dvv, cm, ta, tb):
        dkc = dkc.astype(F32)
        dqp = jnp.concatenate(
            [part for h in range(HEADS)
             for part in (_head(dqc, h, MLA_QK_PAD), _rope_m_t(_head(dqc, h, MLA_QK_PAD, HEAD_DIM), cm, ta, tb))],
            axis=1)
        dkn = _heads(lambda h: _head(dkc, h, MLA_QK_PAD))
        dkr = _head(dkc, 0, MLA_QK_PAD, HEAD_DIM)
        for h in range(1, HEADS):
            dkr = dkr + _head(dkc, h, MLA_QK_PAD, HEAD_DIM)
        return [dqp, dkn, _rope_m_t(dkr, cm, ta, tb), dvv], []

    (dqp, dkn, dkr, dvb), _ = _rowwise("back_qk", back_qk, [dqc, dkc, dvv, cos_m, sa, sb], [],
                                       [(qw, BF16), (hw, BF16), (LANES, F32), (hw, BF16)], [], tm)
    (dcqn,) = mm_nt("q_dx", [dqp], [w["wuq"]], tm, MLA_Q_RANK)
    dwuq = mm_tn("q_dw", cqn, dqp, MLA_Q_RANK, _tile(qw, 1024), tk, GRAD_DT)
    (dckvn,) = mm_nt("kv_dx", [dkn, dvb], [w["wuk"], w["wuv"]], tm, MLA_KV_RANK)
    dwuk = mm_tn("k_dw", ckvn, dkn, MLA_KV_RANK, hw, tk, GRAD_DT)
    dwuv = mm_tn("v_dw", ckvn, dvb, MLA_KV_RANK, hw, tk, GRAD_DT)

    def back_proj(r0, drq, drk, drv, d_rg, pc, dcqn, dckvn, dkr, cr, sr, qn, kvn):
        d_q = _heads(lambda h: _rope_r_t(_head(drq, h), cr, sr))
        d_k = _heads(lambda h: _rope_r_t(_head(drk, h), cr, sr) * RET_SCALE)
        dcq, a_q = _rms_bwd(pc[:, :MLA_Q_RANK], qn, dcqn)
        dckv, a_kv = _rms_bwd(pc[:, MLA_Q_RANK:MLA_Q_RANK + MLA_KV_RANK], kvn, dckvn)
        return ([jnp.concatenate([d_q, d_k, drv, d_rg], axis=1), jnp.concatenate([dcq, dckv, dkr], axis=1)],
                [_colsum(a_q), _colsum(a_kv)])

    (dproj_r, dproj_c), (d_qn, d_kvn) = _rowwise(
        "back_proj", back_proj, [drq, drk, drv, d_rg, proj_c, dcqn, dckvn, dkr, cos_r, sin_r],
        [nw["mla_q_norm"], nw["mla_kv_norm"]], [(IN_RET, BF16), (IN_MLA_PAD, BF16)],
        [MLA_Q_RANK, MLA_KV_RANK], tm)
    (dun,) = mm_nt("proj_dx", [dproj_r, dproj_c], [w["w_r"], w["w_c"]], tmt, _tile(d, 512), outs=(BF16,))
    dw_r = mm_tn("proj_dw_r", un, dproj_r, dt, _tile(IN_RET, 1024), tk, GRAD_DT)
    dw_c = mm_tn("proj_dw_c", un, dproj_c, dt, IN_MLA_PAD, tk, GRAD_DT)
    plan.grads(dict(w_r=dw_r, w_c=dw_c, wuq=dwuq, wuk=dwuk, wuv=dwuv))

    def back_ffn1_norm(r0, h, f, dh2, dn, pre, post):
        dx, dpre = _rms_bwd(h, pre, dn)
        dh1 = dh2 + dx
        df, dpost = _rms_bwd(f, post, 0.5 * dh1)
        return [dh1, df], [_colsum(dpre), _colsum(dpost)]

    (dh1, df1), (d_mix_pre, d_post1) = _rowwise(
        "back_ffn1_norm", back_ffn1_norm, [h1, f1, dh2, dun], [nw["mix_pre_norm"], nw["ffn1_post_norm"]],
        [(d, F32), (d, BF16)], [d, d], tm)
    dn1 = _ffn_bwd("ffn1", df1, n1, g1, u1, a1, w["wg1"], w["wu1"], w["wd1"], tmt, tmk, tk, dt, plan)

    def back_input(r0, h, dh1, dn, pre):
        dx, dpre = _rms_bwd(h, pre, dn)
        return [dh1 + dx], [_colsum(dpre)]

    (dh0,), (d_pre1,) = _rowwise("back_input", back_input, [h0, dh1, dn1], [nw["ffn1_pre_norm"]], [(d, F32)], [d], tm)

    small = dict(ffn1_pre_norm=d_pre1, ffn1_post_norm=d_post1, mix_pre_norm=d_mix_pre, ret_group_norm=d_gn,
                 mla_q_norm=d_qn, mla_kv_norm=d_kvn, mix_post_norm=d_mix_post, ffn2_pre_norm=d_pre2,
                 ffn2_post_norm=d_post2)
    return loss, dh0[N_META:t_real], small, dh0[:N_META]


WEIGHTS = ("meta_tokens", "ffn1_pre_norm", "ffn1_w_gate", "ffn1_w_up", "ffn1_w_down", "ffn1_post_norm",
           "mix_pre_norm", "w_in", "ret_group_norm", "mla_q_norm", "mla_w_uq", "mla_kv_norm", "mla_w_uk",
           "mla_w_uv", "w_out", "mix_post_norm", "ffn2_pre_norm", "ffn2_w_gate", "ffn2_w_up", "ffn2_w_down",
           "ffn2_post_norm")
BIG = ("ffn1_w_gate", "ffn1_w_up", "ffn1_w_down", "w_in", "mla_w_uq", "mla_w_uk", "mla_w_uv", "w_out",
       "ffn2_w_gate", "ffn2_w_up", "ffn2_w_down")
NORMS = ("ffn1_pre_norm", "ffn1_post_norm", "mix_pre_norm", "ret_group_norm", "mla_q_norm", "mla_kv_norm",
         "mix_post_norm", "ffn2_pre_norm", "ffn2_post_norm")


def _unshard_cols(g):
    return g.transpose(1, 0, 2).reshape(g.shape[1], -1)


def _shard_cols(a):
    return a.reshape(a.shape[0], N_CHIPS, -1).transpose(1, 0, 2)


def _pack_rows(rows, width):
    rows = [jnp.pad(r, ((0, 0), (0, width - r.shape[1]))) for r in rows]
    n = sum(r.shape[0] for r in rows)
    return jnp.pad(jnp.concatenate(rows, axis=0), ((0, -n % 8), (0, 0)))


def _weight_views(full):
    w = {}
    for n, g in full.items():
        if n == "w_in":
            w_in = _unshard_cols(g)
            w["w_r"] = w_in[:, :IN_RET]
            w["w_c"] = jnp.pad(w_in[:, IN_RET:], ((0, 0), (0, IN_MLA_PAD - IN_MLA)))
        elif n == "mla_w_uq":
            q = _unshard_cols(g).reshape(MLA_Q_RANK, HEADS, HEAD_DIM + MLA_ROPE)
            q = jnp.pad(q, ((0, 0), (0, 0), (0, MLA_QK_PAD - HEAD_DIM - MLA_ROPE)))
            w["wuq"] = q.reshape(MLA_Q_RANK, HEADS * MLA_QK_PAD)
        elif n in ("mla_w_uk", "mla_w_uv"):
            w["wu" + n[-1]] = _unshard_cols(g)
        elif n == "w_out":
            w["w_out"] = g.reshape(-1, g.shape[2])
        else:
            w["w" + n[7] + n[3]] = g
    return w


def _contributions(g):
    ffn = {"g": "gate", "u": "up", "d": "down"}
    c = {f"ffn{n[2]}_w_{ffn[n[1]]}": a for n, a in g.items() if len(n) == 3 and n[2] in "12"}
    if "w_out" in g:
        c["w_out"] = g["w_out"]
    if "w_r" in g:
        dwuq = g["wuq"].reshape(MLA_Q_RANK, HEADS, MLA_QK_PAD)[:, :, :HEAD_DIM + MLA_ROPE]
        c.update(w_in=_shard_cols(jnp.concatenate([g["w_r"], g["w_c"][:, :IN_MLA]], axis=1)),
                 mla_w_uq=_shard_cols(dwuq.reshape(MLA_Q_RANK, -1)), mla_w_uk=_shard_cols(g["wuk"]),
                 mla_w_uv=_shard_cols(g["wuv"]))
    return c


class _Schedule(_Alone):
    FIRST = ("ffn1_w_gate",)
    CARRIED = {"ffn1_gate": ("ffn1_w_up",), "ffn1_up": ("ffn1_w_down",),
               "ffn1_down": ("w_in", "mla_w_uq", "mla_w_uk", "mla_w_uv"),
               "proj_r": ("w_out",), "mix_fwd": ("ffn2_w_gate", "ffn2_w_up"), "ffn2_up": ("ffn2_w_down",)}
    GRAD_HOST = dict(ffn2_w_gate="mla_bwd", ffn2_w_up="mla_bwd", ffn2_w_down="mla_bwd",
                     w_out="mla_bwd", w_in="ffn1_da", mla_w_uq="ffn1_da", mla_w_uk="ffn1_da", mla_w_uv="ffn1_da",
                     ffn1_w_gate="ffn1_dwu", ffn1_w_up="ffn1_dwd", ffn1_w_down="ffn1_dn")

    def __init__(self, shards):
        self.gathers = {k: (gather_halves([shards[n] for n in names]), names) for k, names in self.CARRIED.items()}
        self.waiting = {}
        self.exchanges = {}
        self.grad = {}

    def host(self, kernel_name):
        if kernel_name in self.gathers:
            return self.gathers[kernel_name][0]
        if kernel_name in self.waiting:
            names, sums = zip(*self.waiting.pop(kernel_name))
            self.exchanges[kernel_name] = (chip_exchange(list(sums)), names)
            return self.exchanges[kernel_name][0]
        return None

    def done(self, kernel_name, w):
        if kernel_name in self.gathers:
            comm, names = self.gathers[kernel_name]
            w.update(_weight_views({n: sibling_fill("fill_" + n, b) for n, b in zip(names, comm.result)}))
        elif kernel_name in self.exchanges:
            comm, names = self.exchanges[kernel_name]
            for n, q in zip(names, comm.result):
                self.grad[n] = reduce_join("reduce_join_" + n, q)

    def grads(self, g):
        for n, a in _contributions(g).items():
            self.waiting.setdefault(self.GRAD_HOST[n], []).append((n, pair_add("pair_add_" + n, a)))


def _step(p, m, v, x, loss_target):
    d = x.shape[2]
    names = list(BIG)
    shards = {n: p[n][0].astype(BF16) for n in names}
    first = _Schedule.FIRST
    gathered = gather_halves([shards[n] for n in first] + [p["meta_tokens"]]).run("gather_first")
    filled = [sibling_fill("fill_" + n, b) for n, b in zip(first + ("meta_tokens",), gathered)]
    w = _weight_views(dict(zip(first, filled[:-1])))
    meta = _unshard_cols(filled[-1])
    nw = {n: p[n] for n in NORMS}
    plan = _Schedule(shards)
    loss, grad_x, small, d_meta = _local_step(x[0], loss_target[0], meta, w, nw, plan)
    grads = dict(plan.grad)

    width = max(d, HEADS * HEAD_DIM)
    packed = _pack_rows([small[n] for n in NORMS] + [d_meta], width)
    total = sum_slots("small_sum", all_gather_devices(packed), F32)
    for i, n in enumerate(NORMS):
        grads[n] = total[i:i + 1, :p[n].shape[1]]
    cols = p["meta_tokens"].shape[1]
    chip = 2 * lax.axis_index("x") + lax.axis_index("y")
    grads["meta_tokens"] = lax.dynamic_slice(total[len(NORMS):len(NORMS) + N_META, :d], (0, chip * cols), (N_META, cols))

    delta, new_m, new_v = {}, {}, {}
    for n in names + ["meta_tokens"]:
        shape = p[n].shape
        flat = lambda a: a.reshape(-1, shape[-1])
        out = adamw("adamw_" + n, flat(p[n]), flat(grads[n]), flat(m[n]), flat(v[n]))
        delta[n], new_m[n], new_v[n], grads[n] = (o.reshape(shape) for o in out)
    pk = lambda src: _pack_rows([src[n] for n in NORMS], width)
    out = adamw("adamw_norms", pk(p), pk(grads), pk(m), pk(v))
    for i, n in enumerate(NORMS):
        delta[n], new_m[n], new_v[n] = (o[i:i + 1, :p[n].shape[1]] for o in out[:3])

    loss = lax.psum(loss, ("x", "y", "c"))
    return loss, grad_x[None], grads, delta, new_m, new_v


def kernel(x, meta_tokens, ffn1_pre_norm, ffn1_w_gate, ffn1_w_up, ffn1_w_down, ffn1_post_norm, mix_pre_norm, w_in, ret_group_norm, mla_q_norm, mla_w_uq, mla_kv_norm, mla_w_uk, mla_w_uv, w_out, mix_post_norm, ffn2_pre_norm, ffn2_w_gate, ffn2_w_up, ffn2_w_down, ffn2_post_norm, loss_target, m_meta_tokens, m_ffn1_pre_norm, m_ffn1_w_gate, m_ffn1_w_up, m_ffn1_w_down, m_ffn1_post_norm, m_mix_pre_norm, m_w_in, m_ret_group_norm, m_mla_q_norm, m_mla_w_uq, m_mla_kv_norm, m_mla_w_uk, m_mla_w_uv, m_w_out, m_mix_post_norm, m_ffn2_pre_norm, m_ffn2_w_gate, m_ffn2_w_up, m_ffn2_w_down, m_ffn2_post_norm, v_meta_tokens, v_ffn1_pre_norm, v_ffn1_w_gate, v_ffn1_w_up, v_ffn1_w_down, v_ffn1_post_norm, v_mix_pre_norm, v_w_in, v_ret_group_norm, v_mla_q_norm, v_mla_w_uq, v_mla_kv_norm, v_mla_w_uk, v_mla_w_uv, v_w_out, v_mix_post_norm, v_ffn2_pre_norm, v_ffn2_w_gate, v_ffn2_w_up, v_ffn2_w_down, v_ffn2_post_norm):
    args = locals()
    p = {n: args[n] for n in WEIGHTS}
    m = {n: args["m_" + n] for n in WEIGHTS}
    v = {n: args["v_" + n] for n in WEIGHTS}
    loss, grad_x, grads, delta, new_m, new_v = _step(p, m, v, x, loss_target)
    return (loss, grad_x, *[grads[n] for n in WEIGHTS], *[delta[n] for n in WEIGHTS],
            *[new_m[n] for n in WEIGHTS], *[new_v[n] for n in WEIGHTS])
```

```python
import functools
import math

import jax
import jax.numpy as jnp
import numpy as np
from jax import lax
from jax.experimental import pallas as pl
from jax.experimental.pallas import tpu as pltpu

F32 = jnp.float32
BF16 = jnp.bfloat16

EPS = 1e-6
N_META = 16
HEADS = 8
HEAD_DIM = 128
MLA_ROPE = 64
MLA_QK_PAD = 256
MLA_Q_RANK = 512
MLA_KV_RANK = 256
ROPE_THETA = 10000.0
N_CHIPS = 4
LANES = 128
VMEM_LIMIT = 60 * 2 ** 20

ADAM_LR = 0.001
ADAM_B1 = 0.9
ADAM_B2 = 0.999
ADAM_EPS = 1e-08
ADAM_WD = 0.01
ADAM_STEP = 10

NN = (((1,), (0,)), ((), ()))
NT = (((1,), (1,)), ((), ()))
TN = (((0,), (0,)), ((), ()))
MESH = pl.DeviceIdType.MESH


def _tile(n, pref, align=LANES):
    if n <= pref:
        return n
    best = 0
    for t in range(align, pref + 1, align):
        if n % t == 0:
            best = t
    assert best, (n, pref)
    return best


def _params(*sem, collective_id=None):
    return pltpu.CompilerParams(dimension_semantics=sem, vmem_limit_bytes=VMEM_LIMIT, collective_id=collective_id)


SIBLING_BARRIER = 0


def _sds(shape, dtype):
    return jax.ShapeDtypeStruct(tuple(shape), dtype)


def _rowwise(name, fn, rows, consts, outs, accs, tr):
    rows = [r if isinstance(r, tuple) else (r, r.shape[1], 0) for r in rows]
    t = rows[0][0].shape[0]
    assert t % tr == 0
    n_r, n_c, n_o = len(rows), len(consts), len(outs)

    def body(*refs):
        i = pl.program_id(0)
        r = [x[...] for x in refs[:n_r]]
        c = [x[...] for x in refs[n_r:n_r + n_c]]
        o_refs = refs[n_r + n_c:n_r + n_c + n_o]
        a_refs = refs[n_r + n_c + n_o:]
        o_vals, a_vals = fn(i * tr, *r, *c)
        for ref, v in zip(o_refs, o_vals):
            ref[...] = v.astype(ref.dtype)
        if a_refs:
            @pl.when(i == 0)
            def _():
                for ref, v in zip(a_refs, a_vals):
                    ref[...] = v

            @pl.when(i > 0)
            def _():
                for ref, v in zip(a_refs, a_vals):
                    ref[...] += v

    in_specs = [pl.BlockSpec((tr, w), functools.partial(lambda cb, i: (i, cb), cb)) for _, w, cb in rows]
    in_specs += [pl.BlockSpec(a.shape, lambda i: (0, 0)) for a in consts]
    out_specs = [pl.BlockSpec((tr, w), lambda i: (i, 0)) for w, _ in outs]
    out_specs += [pl.BlockSpec((1, w), lambda i: (0, 0)) for w in accs]
    out_shape = [_sds((t, w), dt) for w, dt in outs] + [_sds((1, w), F32) for w in accs]
    res = pl.pallas_call(
        body, name=name, grid=(t // tr,), in_specs=in_specs, out_specs=out_specs, out_shape=out_shape,
        compiler_params=_params("arbitrary"),
    )(*[a for a, _, _ in rows], *consts)
    return res[:n_o], res[n_o:]


def _rms(x, w):
    r = lax.rsqrt(jnp.mean(x * x, axis=-1, keepdims=True) + EPS)
    return x * r * w


def _rms_bwd(x, w, dy):
    r = lax.rsqrt(jnp.mean(x * x, axis=-1, keepdims=True) + EPS)
    xh = x * r
    gy = dy * w
    dx = r * (gy - xh * jnp.mean(gy * xh, axis=-1, keepdims=True))
    return dx, dy * xh


def _colsum(v):
    return jnp.sum(v, axis=0, keepdims=True)


def _silu(x):
    return x * jax.nn.sigmoid(x)


def _dsilu(x):
    s = jax.nn.sigmoid(x)
    return s * (1.0 + x * (1.0 - s))


def _rope_r(x, cos, sin):
    return x * cos + pltpu.roll(x, 64, 1) * sin


def _rope_r_t(dy, cos, sin):
    return dy * cos + pltpu.roll(dy * sin, 64, 1)


def _rope_m(x, cos, sa, sb):
    return x * cos + pltpu.roll(x, 32, 1) * sa + pltpu.roll(x, 96, 1) * sb


def _rope_m_t(dy, cos, sa, sb):
    return dy * cos + pltpu.roll(dy * sa, 96, 1) + pltpu.roll(dy * sb, 32, 1)


def _rope_tables(t):
    pos = np.arange(t, dtype=np.float32)

    def cs(dim):
        inv = np.float32(ROPE_THETA) ** (-np.arange(0, dim, 2, dtype=np.float32) / np.float32(dim))
        ang = pos[:, None] * inv[None, :]
        return np.cos(ang), np.sin(ang)

    c, s = cs(HEAD_DIM)
    cos_r = np.concatenate([c, c], axis=1)
    sin_r = np.concatenate([-s, s], axis=1)
    c, s = cs(MLA_ROPE)
    z32, z64 = np.zeros_like(s), np.zeros((t, 64), np.float32)
    cos_m = np.concatenate([c, c, z64], axis=1)
    sa = np.concatenate([z32, s, z64], axis=1)
    sb = np.concatenate([-s, z32, z64], axis=1)
    return tuple(jnp.asarray(a, F32) for a in (cos_r, sin_r, cos_m, sa, sb))


def _call(body, name, grid, in_specs, out_specs, out_shape, scratch, operands, host=None):
    sem = ("arbitrary",) * len(grid)
    if host is None:
        return pl.pallas_call(body, name=name, grid=grid, in_specs=in_specs, out_specs=out_specs, out_shape=out_shape,
                              scratch_shapes=scratch, compiler_params=_params(*sem))(*operands)
    n_in, n_out, n_s = len(in_specs), len(out_shape), len(scratch)
    h_in, h_out = len(host.ins), len(host.out_shape)

    def hosted(*refs):
        a = n_in
        b = a + h_in
        c = b + n_out
        d = c + h_out
        e = d + n_s
        ids = [pl.program_id(i) for i in range(len(grid))]
        first = functools.reduce(jnp.logical_and, [i == 0 for i in ids])
        last = functools.reduce(jnp.logical_and, [i == g - 1 for i, g in zip(ids, grid)])
        comm = (refs[a:b], refs[c:d], refs[e:])

        @pl.when(first)
        def _():
            host.start(*comm)

        body(*refs[:a], *refs[b:c], *refs[d:e])

        @pl.when(last)
        def _():
            host.finish(*comm)

    hbm = pl.BlockSpec(memory_space=pl.ANY)
    res = pl.pallas_call(
        hosted, name=name, grid=grid, in_specs=list(in_specs) + [hbm] * h_in, out_specs=list(out_specs) + [hbm] * h_out,
        out_shape=list(out_shape) + list(host.out_shape), scratch_shapes=list(scratch) + host.scratch(),
        compiler_params=_params(*sem, collective_id=host.collective_id))(*operands, *host.ins)
    host.result = res[n_out:]
    return res[:n_out]


def _mm(name, grid, operands, in_specs, dns, out_specs, out_shape, epilogue=None, extras=(), extra_specs=(),
        acc_shape=None, host=None):
    n_p, n_e = len(dns), len(extras)
    nk = grid[2]
    n_o = len(out_shape)
    in_place = nk > 1 and epilogue is None and n_o == 1 and out_shape[0].dtype == F32

    def body(*refs):
        ab = refs[:2 * n_p]
        ex = refs[2 * n_p:2 * n_p + n_e]
        outs = refs[2 * n_p + n_e:2 * n_p + n_e + n_o]

        part = None
        for p in range(n_p):
            d = lax.dot_general(ab[2 * p][...], ab[2 * p + 1][...], dns[p], preferred_element_type=F32)
            part = d if part is None else part + d

        def finish(acc):
            vals = (acc,) if epilogue is None else epilogue(acc, *[e[...] for e in ex])
            for o, v in zip(outs, vals):
                o[...] = v.astype(o.dtype)

        if nk == 1:
            finish(part)
        else:
            acc_ref = outs[0] if in_place else refs[2 * n_p + n_e + n_o]
            k = pl.program_id(2)

            @pl.when(k == 0)
            def _():
                acc_ref[...] = part

            @pl.when(k > 0)
            def _():
                acc_ref[...] += part

            if not in_place:
                @pl.when(k == nk - 1)
                def _():
                    finish(acc_ref[...])

    scratch = [] if nk == 1 or in_place else [pltpu.VMEM(acc_shape, F32)]
    return _call(body, name, grid, list(in_specs) + list(extra_specs), out_specs, out_shape, scratch,
                 [*operands, *extras], host)


def mm_nn(name, x, w, tm, out_dtype=F32, epilogue=None, outs=None, w2=None, n_block=None, extras=(), host=None):
    t, kdim = x.shape
    if w.ndim == 3:
        s, _, ns = w.shape
        n, tn, nb = s * ns, ns, s
        wspec = pl.BlockSpec((None, kdim, ns), lambda j, i, k: (j, 0, 0))
    else:
        n = w.shape[1]
        tn = n_block or n
        nb = n // tn
        wspec = pl.BlockSpec((kdim, tn), lambda j, i, k: (0, j))
    xspec = pl.BlockSpec((tm, kdim), lambda j, i, k: (i, 0))
    ospec = pl.BlockSpec((tm, tn), lambda j, i, k: (i, j))
    outs = outs or [out_dtype]
    grid = (nb, t // tm, 1)
    if w2 is None:
        return _mm(name, grid, [x, w], [xspec, wspec], [NN], [ospec] * len(outs), [_sds((t, n), d) for d in outs],
                   epilogue=epilogue, extras=extras, extra_specs=[ospec] * len(extras), host=host)

    def body(x_ref, w_ref, w2_ref, *o_refs):
        xv = x_ref[...]
        a = jnp.dot(xv, w_ref[...], preferred_element_type=F32)
        b = jnp.dot(xv, w2_ref[...], preferred_element_type=F32)
        for o, v in zip(o_refs, epilogue(a, b)):
            o[...] = v.astype(o.dtype)

    return _call(body, name, grid[:2],
                 [pl.BlockSpec((tm, kdim), lambda j, i: (i, 0)),
                  pl.BlockSpec((None, kdim, tn), lambda j, i: (j, 0, 0)),
                  pl.BlockSpec((None, kdim, tn), lambda j, i: (j, 0, 0))],
                 [pl.BlockSpec((tm, tn), lambda j, i: (i, j))] * len(outs), [_sds((t, n), d) for d in outs], [],
                 [x, w, w2], host)


def mm_nn_k(name, x, w, tm, tk, out_dtype=F32, host=None):
    t, kdim = x.shape
    n = w.shape[1]
    grid = (t // tm, 1, kdim // tk)
    return _mm(name, grid, [x, w],
               [pl.BlockSpec((tm, tk), lambda i, j, k: (i, k)), pl.BlockSpec((tk, n), lambda i, j, k: (k, 0))],
               [NN], [pl.BlockSpec((tm, n), lambda i, j, k: (i, 0))], [_sds((t, n), out_dtype)],
               acc_shape=(tm, n), host=host)[0]


def mm_nt(name, xs, ws, tm, tn, outs=(F32,), epilogue=None, extras=(), host=None):
    t = xs[0].shape[0]
    specs, ops = [], []
    for x, w in zip(xs, ws):
        kdim = x.shape[1]
        specs.append(pl.BlockSpec((tm, kdim), lambda j, i, k: (i, 0)))
        if w.ndim == 3:
            assert tn == w.shape[1]
            n = w.shape[0] * w.shape[1]
            specs.append(pl.BlockSpec((None, tn, kdim), lambda j, i, k: (j, 0, 0)))
        else:
            n = w.shape[0]
            specs.append(pl.BlockSpec((tn, kdim), lambda j, i, k: (j, 0)))
        ops += [x, w]
    ospec = pl.BlockSpec((tm, tn), lambda j, i, k: (i, j))
    return _mm(name, (n // tn, t // tm, 1), ops, specs, [NT] * len(xs), [ospec] * len(outs),
               [_sds((t, n), d) for d in outs], epilogue=epilogue, extras=extras,
               extra_specs=[ospec] * len(extras), host=host)


def mm_nt_k(name, xs, ws, tm, tn, out_dtype=F32, host=None):
    t = xs[0].shape[0]
    s, n, ns = ws[0].shape
    specs, ops = [], []
    for x, w in zip(xs, ws):
        specs.append(pl.BlockSpec((tm, ns), lambda i, j, k: (i, k)))
        specs.append(pl.BlockSpec((None, tn, ns), lambda i, j, k: (k, j, 0)))
        ops += [x, w]
    return _mm(name, (t // tm, n // tn, s), ops, specs, [NT] * len(xs),
               [pl.BlockSpec((tm, tn), lambda i, j, k: (i, j))], [_sds((t, n), out_dtype)], acc_shape=(tm, tn),
               host=host)[0]


def mm_tn(name, x, y, tm, tn, tk, out_dtype, shard_rows=False, shard_cols=False, host=None):
    t, m = x.shape
    n = y.shape[1]
    grid = (m // tm, n // tn, t // tk)
    if shard_cols:
        ospec = pl.BlockSpec((None, tm, tn), lambda i, j, k: (j, i, 0))
        oshape = _sds((n // tn, m, tn), out_dtype)
    elif shard_rows:
        ospec = pl.BlockSpec((None, tm, tn), lambda i, j, k: (i, 0, j))
        oshape = _sds((m // tm, tm, n), out_dtype)
    else:
        ospec = pl.BlockSpec((tm, tn), lambda i, j, k: (i, j))
        oshape = _sds((m, n), out_dtype)
    return _mm(name, grid, [x, y],
               [pl.BlockSpec((tk, tm), lambda i, j, k: (k, i)), pl.BlockSpec((tk, tn), lambda i, j, k: (k, j))],
               [TN], [ospec], [oshape], acc_shape=(tm, tn), host=host)[0]


def _decay_logs():
    return [math.log(1.0 - 2.0 ** (-5.0 - h)) for h in range(HEADS)]


def _log_decay(h):
    lg = jnp.float32(_decay_logs()[0])
    for i in range(1, HEADS):
        lg = jnp.where(h == i, jnp.float32(_decay_logs()[i]), lg)
    return lg


def _decayed_scores(q, k, lg):
    s = lax.dot_general(q, k, NT, preferred_element_type=F32)
    row = lax.broadcasted_iota(jnp.int32, s.shape, 0)
    col = lax.broadcasted_iota(jnp.int32, s.shape, 1)
    dec = jnp.where(col <= row, jnp.exp(jnp.maximum(row - col, 0).astype(F32) * lg), 0.0)
    return s * dec, dec


def _causal(s):
    row = lax.broadcasted_iota(jnp.int32, s.shape, 0)
    col = lax.broadcasted_iota(jnp.int32, s.shape, 1)
    return jnp.where(col <= row, s, -1e30)


def _attn_fwd_step(qi, blk, scale, q_ref, k_ref, v_ref, o_ref, lse_ref, m_ref, l_ref, acc_ref):
    qv = q_ref[...]
    m_ref[...] = jnp.full_like(m_ref, -1e30)
    l_ref[...] = jnp.zeros_like(l_ref)
    acc_ref[...] = jnp.zeros_like(acc_ref)

    def keys(start, n, diag_from):
        rows = pl.ds(pl.multiple_of(start, blk), n)
        s = lax.dot_general(qv, k_ref[rows, :], NT, preferred_element_type=F32) * scale
        if diag_from is not None:
            row = lax.broadcasted_iota(jnp.int32, s.shape, 0)
            col = lax.broadcasted_iota(jnp.int32, s.shape, 1)
            s = jnp.where(col - diag_from <= row, s, -1e30)
        m = m_ref[...]
        m_new = jnp.maximum(m, jnp.max(s, axis=-1, keepdims=True))
        p = jnp.exp(s - m_new)
        alpha = jnp.exp(m - m_new)
        m_ref[...] = m_new
        l_ref[...] = alpha * l_ref[...] + jnp.sum(p, axis=-1, keepdims=True)
        acc_ref[...] = alpha * acc_ref[...] + jnp.dot(p.astype(BF16), v_ref[rows, :], preferred_element_type=F32)

    @pl.loop(0, qi // 2)
    def _(j):
        keys(j * (2 * blk), 2 * blk, None)

    @pl.when(qi % 2 == 1)
    def _():
        keys((qi - 1) * blk, 2 * blk, blk)

    @pl.when(qi % 2 == 0)
    def _():
        keys(qi * blk, blk, 0)

    l = l_ref[...]
    o_ref[...] = acc_ref[...] / l
    lse_ref[...] = jnp.broadcast_to(m_ref[...] + jnp.log(l), (blk, HEAD_DIM))


def mix_fwd(name, q, k, v, rq, rk, rv, blk, scale, host=None):
    t = q.shape[0]
    dq = q.shape[1] // HEADS
    nq = t // blk

    def body(q_ref, k_ref, v_ref, rq_ref, rk_ref, rv_ref, o_ref, lse_ref, ro_ref, st_ref, m_ref, l_ref, acc_ref, state):
        h, i = pl.program_id(0), pl.program_id(1)

        @pl.when(h < HEADS)
        def _():
            _attn_fwd_step(i, blk, scale, q_ref, k_ref, v_ref, o_ref, lse_ref, m_ref, l_ref, acc_ref)

        @pl.when(h >= HEADS)
        def _():
            _ret_fwd_step(h - HEADS, i, blk, rq_ref, rk_ref, rv_ref, ro_ref, st_ref, state)

    mla = lambda w, whole=False: pl.BlockSpec(
        (t if whole else blk, w),
        lambda h, i: (0 if whole else jnp.where(h < HEADS, i, nq - 1), jnp.minimum(h, HEADS - 1)))
    ret = lambda rows: pl.BlockSpec(
        (rows, HEAD_DIM), lambda h, i: (jnp.where(h >= HEADS, i, 0), jnp.maximum(h - HEADS, 0)))
    wide = _sds((t, HEADS * HEAD_DIM), F32)
    return _call(body, name, (2 * HEADS, nq),
                 [mla(dq), mla(dq, True), mla(HEAD_DIM, True), ret(blk), ret(blk), ret(blk)],
                 [mla(HEAD_DIM), mla(HEAD_DIM), ret(blk), ret(HEAD_DIM)],
                 [wide, wide, wide, _sds((nq * HEAD_DIM, HEADS * HEAD_DIM), F32)],
                 [pltpu.VMEM((blk, 1), F32), pltpu.VMEM((blk, 1), F32), pltpu.VMEM((blk, HEAD_DIM), F32),
                  pltpu.VMEM((HEAD_DIM, HEAD_DIM), F32)],
                 [q, k, v, rq, rk, rv], host)


def attn_bwd(name, q, k, v, do, o, lse, blk, scale, host=None):
    t = q.shape[0]
    dq_w = q.shape[1] // HEADS
    nb = t // blk

    def body(q_ref, k_ref, v_ref, do_ref, o_ref, lse_ref, dq_ref, dk_out, dv_out, dk_ref, dv_ref):
        ki = pl.program_id(1)
        kv = k_ref[...]
        vv = v_ref[...]

        @pl.when(ki == 0)
        def _():
            dq_ref[...] = jnp.zeros_like(dq_ref)

        def queries(start, n, diag):
            rows = pl.ds(pl.multiple_of(start, blk), n)
            qv, dov = q_ref[rows, :], do_ref[rows, :]
            s = lax.dot_general(qv, kv, NT, preferred_element_type=F32) * scale
            if diag:
                s = _causal(s)
            p = jnp.exp(s - lse_ref[rows, :][:, :1])
            dp = lax.dot_general(dov, vv, NT, preferred_element_type=F32)
            delta = jnp.sum(dov.astype(F32) * o_ref[rows, :], axis=-1, keepdims=True)
            ds = p * (dp - delta) * scale
            pb, dsb = p.astype(BF16), ds.astype(BF16)
            dv_ref[...] += lax.dot_general(pb, dov, TN, preferred_element_type=F32)
            dk_ref[...] += lax.dot_general(dsb, qv, TN, preferred_element_type=F32)
            dq_ref[rows, :] += jnp.dot(dsb, kv, preferred_element_type=F32)

        dk_ref[...] = jnp.zeros_like(dk_ref)
        dv_ref[...] = jnp.zeros_like(dv_ref)
        queries(ki * blk, blk, True)
        later = nb - 1 - ki

        @pl.when(later % 2 == 1)
        def _():
            queries((ki + 1) * blk, blk, False)

        @pl.loop(0, later // 2)
        def _(j):
            queries((ki + 1 + later % 2 + 2 * j) * blk, 2 * blk, False)

        dk_out[...] = dk_ref[...].astype(dk_out.dtype)
        dv_out[...] = dv_ref[...].astype(dv_out.dtype)

    full = lambda w: pl.BlockSpec((t, w), lambda h, j: (0, h))
    blkd = lambda w: pl.BlockSpec((blk, w), lambda h, j: (j, h))
    return _call(body, name, (HEADS, nb),
                 [full(dq_w), blkd(dq_w), blkd(HEAD_DIM), full(HEAD_DIM), full(HEAD_DIM), full(HEAD_DIM)],
                 [full(dq_w), blkd(dq_w), blkd(HEAD_DIM)],
                 [_sds(q.shape, F32), _sds(k.shape, BF16), _sds(v.shape, BF16)],
                 [pltpu.VMEM((blk, dq_w), F32), pltpu.VMEM((blk, HEAD_DIM), F32)], [q, k, v, do, o, lse], host)


def _chunk_decays(lg, blk):
    row = lax.broadcasted_iota(jnp.int32, (blk, HEAD_DIM), 0).astype(F32)
    return jnp.exp(lg * (row + 1.0)), jnp.exp(lg * (blk - 1.0 - row)), jnp.exp(lg * blk * jnp.ones((1, HEAD_DIM), F32))


def _ret_fwd_step(h, i, blk, q_ref, k_ref, v_ref, o_ref, st_ref, state):
    lg = _log_decay(h)

    @pl.when(i == 0)
    def _():
        state[...] = jnp.zeros_like(state)

    qv, kv, vv = q_ref[...], k_ref[...], v_ref[...]
    before = state[...]
    st_ref[...] = before
    p, _ = _decayed_scores(qv, kv, lg)
    xi, zeta, g_blk = _chunk_decays(lg, blk)
    o_ref[...] = (jnp.dot(p.astype(BF16), vv, preferred_element_type=F32)
                  + jnp.dot(qv, before.astype(BF16), preferred_element_type=F32) * xi)
    kz = (kv.astype(F32) * zeta).astype(BF16)
    state[...] = before * g_blk + lax.dot_general(kz, vv, TN, preferred_element_type=F32)


def ret_bwd(name, q, k, v, do, states, blk, host=None):
    t = q.shape[0]
    nb = t // blk

    def body(q_ref, k_ref, v_ref, do_ref, st_ref, dq_ref, dk_ref, dv_ref, dstate):
        h, i = pl.program_id(0), pl.program_id(1)
        lg = _log_decay(h)

        @pl.when(i == 0)
        def _():
            dstate[...] = jnp.zeros_like(dstate)

        qv, kv, vv, dov = q_ref[...], k_ref[...], v_ref[...], do_ref[...]
        before = st_ref[...].astype(BF16)
        after_grad = dstate[...]
        p, dec = _decayed_scores(qv, kv, lg)
        ds = lax.dot_general(dov, vv, NT, preferred_element_type=F32) * dec
        pb, dsb = p.astype(BF16), ds.astype(BF16)
        xi, zeta, g_blk = _chunk_decays(lg, blk)
        dox = (dov.astype(F32) * xi).astype(BF16)
        kz = (kv.astype(F32) * zeta).astype(BF16)
        agb = after_grad.astype(BF16)
        dv = lax.dot_general(pb, dov, TN, preferred_element_type=F32) + jnp.dot(kz, agb, preferred_element_type=F32)
        dq = (jnp.dot(dsb, kv, preferred_element_type=F32)
              + lax.dot_general(dox, before, NT, preferred_element_type=F32))
        dk = (lax.dot_general(dsb, qv, TN, preferred_element_type=F32)
              + lax.dot_general(vv, agb, NT, preferred_element_type=F32) * zeta)
        for ref, val in ((dq_ref, dq), (dk_ref, dk), (dv_ref, dv)):
            ref[...] = val.astype(ref.dtype)
        dstate[...] = after_grad * g_blk + lax.dot_general(qv, dox, TN, preferred_element_type=F32)

    hspec = pl.BlockSpec((blk, HEAD_DIM), lambda h, i: (nb - 1 - i, h))
    return _call(body, name, (HEADS, nb),
                 [hspec] * 4 + [pl.BlockSpec((HEAD_DIM, HEAD_DIM), lambda h, i: (nb - 1 - i, h))],
                 [hspec] * 3, [_sds(q.shape, BF16)] * 3, [pltpu.VMEM((HEAD_DIM, HEAD_DIM), F32)],
                 [q, k, v, do, states], host)


CHIP_FLIPS = ((1, 0), (0, 1), (1, 1))
CHIP_BARRIER = 1
ALL_BARRIER = 2


def _position():
    return lax.axis_index("x"), lax.axis_index("y"), lax.axis_index("c")


def _chip_peers(pos):
    x, y, c = pos
    return [(x ^ fx, y ^ fy, c) for fx, fy in CHIP_FLIPS]


class _Comm:
    def __init__(self, ins, out_shape, plan, n_remote, n_local, collective_id):
        self.ins, self.out_shape, self.plan = list(ins), list(out_shape), plan
        self.n_remote, self.n_local, self.collective_id = n_remote, n_local, collective_id
        self.result = None

    def scratch(self):
        return [pltpu.SemaphoreType.DMA((self.n_remote,)), pltpu.SemaphoreType.DMA((self.n_remote,)),
                pltpu.SemaphoreType.DMA((max(self.n_local, 1),))]

    def _copies(self, in_refs, out_refs, sems):
        send_sems, recv_sems, local_sems = sems
        pos = _position()
        p = self.plan(pos, in_refs, out_refs)

        def remote(k, src, dst, dev):
            return pltpu.make_async_remote_copy(src_ref=src, dst_ref=dst, send_sem=send_sems.at[k],
                                                recv_sem=recv_sems.at[k], device_id=dev, device_id_type=MESH)

        local = [pltpu.make_async_copy(s, d, local_sems.at[i]) for i, (s, d) in enumerate(p["local"])]
        out = [remote(k, s, d, dev) for k, (s, d, dev) in enumerate(p["sends"])]
        arrivals = [functools.partial(remote, k, d, d, pos) for k, d in enumerate(p["recvs"])]
        return local, out, arrivals, p["peers"]

    def start(self, in_refs, out_refs, sems):
        local, out, _, peers = self._copies(in_refs, out_refs, sems)
        barrier = pltpu.get_barrier_semaphore()
        for peer in peers:
            pl.semaphore_signal(barrier, inc=1, device_id=peer, device_id_type=MESH)
        pl.semaphore_wait(barrier, len(peers))
        for cp in local + out:
            cp.start()

    def finish(self, in_refs, out_refs, sems):
        local, out, arrivals, _ = self._copies(in_refs, out_refs, sems)
        for make in arrivals:
            make().wait_recv()
        for cp in out:
            cp.wait_send()
        for cp in local:
            cp.wait()

    def run(self, name):
        n_in, n_out = len(self.ins), len(self.out_shape)

        def body(*refs):
            comm = (refs[:n_in], refs[n_in:n_in + n_out], refs[n_in + n_out:])
            self.start(*comm)
            self.finish(*comm)

        hbm = pl.BlockSpec(memory_space=pl.ANY)
        self.result = pl.pallas_call(
            body, name=name, in_specs=[hbm] * n_in, out_specs=[hbm] * n_out, out_shape=self.out_shape,
            scratch_shapes=self.scratch(),
            compiler_params=pltpu.CompilerParams(collective_id=self.collective_id))(*self.ins)
        return self.result


def _half_rows(c, rows):
    r2 = rows // 2
    return pl.ds(pl.multiple_of(c * r2, math.gcd(r2, LANES)), r2)


def _half(ref, c, rows, lead=()):
    return ref.at[(*lead, _half_rows(c, rows))]


def gather_halves(shards):
    def plan(pos, ins, outs):
        x, y, c = pos
        me = 2 * x + y
        p = dict(local=[], sends=[], recvs=[], peers=_chip_peers(pos))
        for a, (src, dst) in enumerate(zip(ins, outs)):
            rows = shards[a].shape[0]
            for px, py, _ in p["peers"]:
                p["sends"].append((_half(src, c, rows), _half(dst, c, rows, (me,)), (px, py, c)))
                p["recvs"].append(_half(dst, c, rows, (2 * px + py,)))
        return p

    return _Comm(shards, [_sds((N_CHIPS, *s.shape), s.dtype) for s in shards], plan,
                 n_remote=3 * len(shards), n_local=0, collective_id=CHIP_BARRIER)


def chip_exchange(parts):
    def plan(pos, ins, outs):
        x, y, c = pos
        me = 2 * x + y
        p = dict(local=[], sends=[], recvs=[], peers=_chip_peers(pos))
        for src, dst in zip(ins, outs):
            for px, py, _ in p["peers"]:
                peer = 2 * px + py
                p["sends"].append((src.at[peer], dst.at[me], (px, py, c)))
                p["recvs"].append(dst.at[peer])
        return p

    return _Comm(parts, [_sds(g.shape, g.dtype) for g in parts], plan, n_remote=3 * len(parts), n_local=0,
                 collective_id=CHIP_BARRIER)


def all_gather_devices(v):
    flips = [(fx, fy, fc) for fx in (0, 1) for fy in (0, 1) for fc in (0, 1)][1:]

    def plan(pos, ins, outs):
        x, y, c = pos
        me = 4 * x + 2 * y + c
        p = dict(local=[(ins[0], outs[0].at[me])], sends=[], recvs=[],
                 peers=[(x ^ fx, y ^ fy, c ^ fc) for fx, fy, fc in flips])
        for px, py, pc in p["peers"]:
            p["sends"].append((ins[0], outs[0].at[me], (px, py, pc)))
            p["recvs"].append(outs[0].at[4 * px + 2 * py + pc])
        return p

    return _Comm([v], [_sds((8, *v.shape), v.dtype)], plan, n_remote=7, n_local=1,
                 collective_id=ALL_BARRIER).run("small_all_gather")[0]


SWAP_CHUNK_BYTES = 3 * 2 ** 19


def _sibling_stream(t, n, value, consume, sbuf, rbuf, send_sems, recv_sems, credits):
    x, y, c = _position()
    sib = (x, y, 1 - c)

    def copy(slot):
        return pltpu.make_async_remote_copy(src_ref=sbuf.at[slot], dst_ref=rbuf.at[slot], send_sem=send_sems.at[slot],
                                            recv_sem=recv_sems.at[slot], device_id=sib, device_id_type=MESH)

    slot = t % 2

    @pl.when(t == 0)
    def _():
        barrier = pltpu.get_barrier_semaphore()
        pl.semaphore_signal(barrier, inc=1, device_id=sib, device_id_type=MESH)
        pl.semaphore_wait(barrier, 1)

    @pl.when(jnp.logical_and(t >= 2, t < n))
    def _():
        copy(slot).wait_send()
        pl.semaphore_wait(credits.at[slot], 1)

    @pl.when(t < n)
    def _():
        sbuf[slot] = value
        copy(slot).start()

    @pl.when(t >= 1)
    def _():
        prev = 1 - slot
        copy(prev).wait_recv()
        consume(sbuf[prev], rbuf[prev])

        @pl.when(t + 1 < n)
        def _():
            pl.semaphore_signal(credits.at[prev], inc=1, device_id=sib, device_id_type=MESH)

    @pl.when(t == n)
    def _():
        copy(1 - slot).wait_send()
        if n > 1:
            copy(slot).wait_send()


def _swap_scratch(rows, cols, dtype):
    return [pltpu.VMEM((2, rows, cols), dtype), pltpu.VMEM((2, rows, cols), dtype),
            pltpu.SemaphoreType.DMA((2,)), pltpu.SemaphoreType.DMA((2,)), pltpu.SemaphoreType.REGULAR((2,))]


def _chunk_rows(rows, cols, dtype, nbytes=SWAP_CHUNK_BYTES):
    return _tile(rows, max(16, nbytes // (cols * jnp.dtype(dtype).itemsize)), 16)


def pair_add(name, g):
    s, r, c_ = g.shape
    r2 = r // 2
    cr = _chunk_rows(r2, c_, g.dtype)
    nj = r2 // cr

    n = s * nj

    def body(core, mine_ref, theirs_ref, o_ref, *scratch):
        def consume(_, got):
            o_ref[...] = (mine_ref[...].astype(F32) + got.astype(F32)).astype(o_ref.dtype)

        _sibling_stream(pl.program_id(0), n, theirs_ref[...], consume, *scratch)

    sent = lambda t: jnp.minimum(t, n - 1)
    used = lambda t: jnp.maximum(t - 1, 0)
    grid_spec = pltpu.PrefetchScalarGridSpec(
        num_scalar_prefetch=1, grid=(n + 1,),
        in_specs=[pl.BlockSpec((None, cr, c_), lambda t, core: (used(t) // nj, core[0] * nj + used(t) % nj, 0)),
                  pl.BlockSpec((None, cr, c_), lambda t, core: (sent(t) // nj, (1 - core[0]) * nj + sent(t) % nj, 0))],
        out_specs=pl.BlockSpec((None, cr, c_), lambda t, core: (used(t) // nj, used(t) % nj, 0)),
        scratch_shapes=_swap_scratch(cr, c_, g.dtype))
    core = lax.axis_index("c").astype(jnp.int32).reshape(1)
    return pl.pallas_call(body, name=name, grid_spec=grid_spec, out_shape=_sds((s, r2, c_), g.dtype),
                          compiler_params=_params("arbitrary", collective_id=SIBLING_BARRIER))(core, g, g)


def _adamw_math(w, g, m, v):
    m = ADAM_B1 * m + (1.0 - ADAM_B1) * g
    v = ADAM_B2 * v + (1.0 - ADAM_B2) * (g * g)
    m_hat = m / (1.0 - ADAM_B1 ** ADAM_STEP)
    v_hat = v / (1.0 - ADAM_B2 ** ADAM_STEP)
    return -ADAM_LR * (m_hat / (jnp.sqrt(v_hat) + ADAM_EPS) + ADAM_WD * w), m, v


def reduce_join(name, p, mine):
    s, r2, c_ = p.shape
    cr = _chunk_rows(r2, c_, F32)
    nj = r2 // cr

    def body(where, p_ref, mine_ref, o_ref, acc_ref, *scratch):
        c, me = where[0], where[1]
        for chip in range(s):
            @pl.when(me == chip)
            def _():
                terms = [mine_ref[...] if i == chip else p_ref[i] for i in range(s)]
                acc = terms[0].astype(F32)
                for term in terms[1:]:
                    acc = acc + term.astype(F32)
                acc_ref[...] = acc

        def consume(own, got):
            o_ref[c] = own
            o_ref[1 - c] = got

        _sibling_stream(pl.program_id(0), nj, acc_ref[...], consume, *scratch)

    sent = lambda t: jnp.minimum(t, nj - 1)
    grid_spec = pltpu.PrefetchScalarGridSpec(
        num_scalar_prefetch=1, grid=(nj + 1,),
        in_specs=[pl.BlockSpec((s, cr, c_), lambda t, where: (0, sent(t), 0)),
                  pl.BlockSpec((None, cr, c_), lambda t, where: (where[1], sent(t), 0))],
        out_specs=pl.BlockSpec((2, cr, c_), lambda t, where: (0, jnp.maximum(t - 1, 0), 0)),
        scratch_shapes=[pltpu.VMEM((cr, c_), F32)] + _swap_scratch(cr, c_, F32))
    x, y, c = _position()
    where = jnp.stack([c, 2 * x + y]).astype(jnp.int32)
    out = pl.pallas_call(body, name=name, grid_spec=grid_spec, out_shape=_sds((2, r2, c_), F32),
                         compiler_params=_params("arbitrary", collective_id=SIBLING_BARRIER))(where, p, mine)
    return out.reshape(2 * r2, c_)


def sibling_fill(name, buf, shard):
    s, r, c_ = buf.shape
    r2 = r // 2
    cr = _chunk_rows(r2, c_, buf.dtype)
    nj = r2 // cr
    n_peers = len(CHIP_FLIPS)
    n = n_peers * nj
    n_own = r // cr

    def body(where, in_ref, own_ref, o_ref, *scratch):
        t = pl.program_id(0)

        @pl.when(t <= n)
        def _():
            def consume(_, got):
                o_ref[...] = got

            _sibling_stream(t, n, in_ref[...], consume, *scratch)

        @pl.when(t > n)
        def _():
            o_ref[...] = own_ref[...]

    sent = lambda t: jnp.minimum(t, n - 1)
    used = lambda t: jnp.clip(t - 1, 0, n - 1)
    own = lambda t: jnp.clip(t - (n + 1), 0, n_own - 1)
    grid_spec = pltpu.PrefetchScalarGridSpec(
        num_scalar_prefetch=1, grid=(n + 1 + n_own,),
        in_specs=[pl.BlockSpec((None, cr, c_),
                               lambda t, where: (where[sent(t) // nj], where[n_peers] * nj + sent(t) % nj, 0)),
                  pl.BlockSpec((cr, c_), lambda t, where: (own(t), 0))],
        out_specs=pl.BlockSpec(
            (None, cr, c_),
            lambda t, where: (jnp.where(t <= n, where[used(t) // nj], where[n_peers + 1]),
                              jnp.where(t <= n, (1 - where[n_peers]) * nj + used(t) % nj, own(t)), 0)),
        scratch_shapes=_swap_scratch(cr, c_, buf.dtype))
    x, y, c = _position()
    where = jnp.stack([2 * (x ^ fx) + (y ^ fy) for fx, fy in CHIP_FLIPS] + [c, 2 * x + y]).astype(jnp.int32)
    return pl.pallas_call(body, name=name, grid_spec=grid_spec, out_shape=_sds(buf.shape, buf.dtype),
                          input_output_aliases={1: 0},
                          compiler_params=_params("arbitrary", collective_id=SIBLING_BARRIER))(where, buf, shard)


def sum_slots(name, p, out_dtype):
    s, r, c = p.shape
    tr = _tile(r, 256, 16)

    def body(p_ref, o_ref):
        acc = p_ref[0].astype(F32)
        for i in range(1, s):
            acc = acc + p_ref[i].astype(F32)
        o_ref[...] = acc.astype(o_ref.dtype)

    return pl.pallas_call(
        body, name=name, grid=(r // tr,), in_specs=[pl.BlockSpec((s, tr, c), lambda i: (0, i, 0))],
        out_specs=pl.BlockSpec((tr, c), lambda i: (i, 0)), out_shape=_sds((r, c), out_dtype),
        compiler_params=_params("arbitrary"),
    )(p)


def adamw(name, w, g, m, v):
    r, c = w.shape
    outs, _ = _rowwise(name, lambda _, w, g, m, v: ([*_adamw_math(w, g, m, v), g], []), [w, g, m, v], [],
                       [(c, F32)] * 4, [], _tile(r, 256, 8))
    return outs


RET_SCALE = HEAD_DIM ** -0.5
MLA_SCALE = (HEAD_DIM + MLA_ROPE) ** -0.5
GRAD_DT = BF16
IN_RET = 4 * HEADS * HEAD_DIM
IN_MLA = MLA_Q_RANK + MLA_KV_RANK + MLA_ROPE
IN_MLA_PAD = IN_MLA + 64


def _heads(fn):
    return jnp.concatenate([fn(h) for h in range(HEADS)], axis=1)


def _head(a, h, stride=HEAD_DIM, off=0):
    return a[:, h * stride + off:h * stride + off + HEAD_DIM]


def _group_norm(o):
    rs = [lax.rsqrt(jnp.mean(_head(o, h) * _head(o, h), axis=-1, keepdims=True) + EPS) for h in range(HEADS)]
    return _heads(lambda h: _head(o, h) * rs[h]), rs


class _Alone:
    def host(self, kernel_name):
        return None

    def done(self, kernel_name, w):
        pass

    def grads(self, g):
        pass


def _ffn_fwd(tag, n, w, k, tm, tmk, plan):
    gate, up, down = tag + "_gate", tag + "_up", tag + "_down"
    if "wu" + k in w:
        g, u, a = mm_nn(up, n, w["wg" + k], tm, outs=[BF16] * 3, w2=w["wu" + k],
                        epilogue=lambda g, u: (g, u, _silu(g) * u), host=plan.host(up))
    else:
        (g,) = mm_nn(gate, n, w["wg" + k], tm, out_dtype=BF16, host=plan.host(gate))
        plan.done(gate, w)
        u, a = mm_nn(up, n, w["wu" + k], tm, outs=[BF16] * 2, extras=(g,),
                     epilogue=lambda u, g: (u, _silu(g.astype(F32)) * u), host=plan.host(up))
    plan.done(up, w)
    ff = a.shape[1]
    wd = w["wd" + k]
    f = mm_nn_k(down, a, wd.reshape(ff, wd.shape[2]), tmk, _tile(ff, 1408), host=plan.host(down))
    plan.done(down, w)
    return g, u, a, f


def _ffn_bwd(tag, df, n, g, u, a, wg, wu, wd, tm, tmk, tk, dt, plan):
    ns = wg.shape[2]

    def gate_grads(da, g, u):
        g, u = g.astype(F32), u.astype(F32)
        return da * u * _dsilu(g), da * _silu(g)

    def hosted(kernel_name, call):
        out = call(plan.host(kernel_name))
        plan.done(kernel_name, None)
        return out

    k = tag[-1]
    dg, du = hosted(tag + "_da", lambda h: mm_nt(tag + "_da", [df], [wd], tm, ns, outs=(BF16, BF16),
                                                 epilogue=gate_grads, extras=(g, u), host=h))
    dwg = hosted(tag + "_dwg", lambda h: mm_tn(tag + "_dwg", n, dg, dt, ns, tk, GRAD_DT, shard_cols=True, host=h))
    plan.grads({"wg" + k: dwg})
    dwu = hosted(tag + "_dwu", lambda h: mm_tn(tag + "_dwu", n, du, dt, ns, tk, GRAD_DT, shard_cols=True, host=h))
    plan.grads({"wu" + k: dwu})
    dwd = hosted(tag + "_dwd", lambda h: mm_tn(tag + "_dwd", a, df, ns, dt, tk, GRAD_DT, shard_rows=True, host=h))
    plan.grads({"wd" + k: dwd})
    dn = hosted(tag + "_dn", lambda h: mm_nt_k(tag + "_dn", [dg, du], [wg, wu], tmk, dt, out_dtype=BF16, host=h))
    return dn


def _local_step(x, tgt, meta, w, nw, plan):
    seq, d = x.shape
    t_real = N_META + seq
    tp = -(-t_real // LANES) * LANES
    zpad = jnp.zeros((tp - t_real, d), F32)
    h0 = jnp.concatenate([meta, x, zpad], axis=0)
    tgt_p = jnp.concatenate([jnp.zeros((N_META, d), F32), tgt, zpad], axis=0)
    cos_r, sin_r, cos_m, sa, sb = _rope_tables(tp)
    tm = _tile(tp, 512)
    tmt = _tile(tp, 768, 16)
    tmk = _tile(tp, 1408)
    tk = tp
    blk = tm
    dt = _tile(d, 1024)
    hw = HEADS * HEAD_DIM
    qw = HEADS * MLA_QK_PAD

    (n1,), _ = _rowwise("ffn1_norm", lambda r0, h, g: ([_rms(h, g)], []), [h0], [nw["ffn1_pre_norm"]],
                        [(d, BF16)], [], tm)
    g1, u1, a1, f1 = _ffn_fwd("ffn1", n1, w, "1", tm, tmk, plan)

    def post_ffn1(r0, h, f, post, pre):
        h1 = h + 0.5 * _rms(f, post)
        return [h1, _rms(h1, pre)], []

    (h1, un), _ = _rowwise("mix_norm", post_ffn1, [h0, f1], [nw["ffn1_post_norm"], nw["mix_pre_norm"]],
                           [(d, F32), (d, BF16)], [], tm)
    (proj_r,) = mm_nn("proj_r", un, w["w_r"], tm, n_block=_tile(IN_RET, 1024), host=plan.host("proj_r"))
    plan.done("proj_r", w)
    (proj_c,) = mm_nn("proj_c", un, w["w_c"], tm)

    def split_proj(r0, pr, pc, cr, sr, cm, ta, tb, qn, kvn):
        rq = _heads(lambda h: _rope_r(_head(pr, h), cr, sr))
        rk = _heads(lambda h: _rope_r(_head(pr, h, off=hw), cr, sr) * RET_SCALE)
        rv = pr[:, 2 * hw:3 * hw]
        cqn = _rms(pc[:, :MLA_Q_RANK], qn)
        ckvn = _rms(pc[:, MLA_Q_RANK:MLA_Q_RANK + MLA_KV_RANK], kvn)
        krr = _rope_m(pc[:, MLA_Q_RANK + MLA_KV_RANK:], cm, ta, tb)
        return [rq, rk, rv, cqn, ckvn, krr], []

    (rq, rk, rv, cqn, ckvn, krr), _ = _rowwise(
        "split_proj", split_proj, [proj_r, proj_c, cos_r, sin_r, cos_m, sa, sb],
        [nw["mla_q_norm"], nw["mla_kv_norm"]],
        [(hw, BF16), (hw, BF16), (hw, BF16), (MLA_Q_RANK, BF16), (MLA_KV_RANK, BF16), (LANES, F32)], [], tm)
    (qp,) = mm_nn("q_up", cqn, w["wuq"], tm)
    (kn,) = mm_nn("k_up", ckvn, w["wuk"], tm)
    (vv,) = mm_nn("v_up", ckvn, w["wuv"], tm, out_dtype=BF16)

    def build_qk(r0, qp, kn, krr, cm, ta, tb):
        qc = jnp.concatenate(
            [part for h in range(HEADS)
             for part in (_head(qp, h, MLA_QK_PAD), _rope_m(_head(qp, h, MLA_QK_PAD, HEAD_DIM), cm, ta, tb))], axis=1)
        kc = jnp.concatenate([part for h in range(HEADS) for part in (_head(kn, h), krr)], axis=1)
        return [qc, kc], []

    (qc, kc), _ = _rowwise("build_qk", build_qk, [qp, kn, krr, cos_m, sa, sb], [], [(qw, BF16), (qw, BF16)], [], tm)
    o_m, lse, o_r, ret_states = mix_fwd("mix_fwd", qc, kc, vv, rq, rk, rv, blk, MLA_SCALE, host=plan.host("mix_fwd"))
    plan.done("mix_fwd", w)

    def gate_mix(r0, rg, o_r, o_m, gn):
        y, _ = _group_norm(o_r)
        return [jnp.concatenate([_silu(rg) * (y * gn), o_m], axis=1)], []

    (mixcat,), _ = _rowwise("gate_mix", gate_mix, [(proj_r, hw, 3), o_r, o_m], [nw["ret_group_norm"]],
                            [(2 * hw, BF16)], [], tm)
    (mix,) = mm_nn("mix_out", mixcat, w["w_out"], tm, n_block=dt)

    def post_mix(r0, h, m, post, pre):
        h2 = h + _rms(m, post)
        return [h2, _rms(h2, pre)], []

    (h2, n3), _ = _rowwise("ffn2_norm", post_mix, [h1, mix], [nw["mix_post_norm"], nw["ffn2_pre_norm"]],
                           [(d, F32), (d, BF16)], [], tm)
    g2, u2, a2, f2 = _ffn_fwd("ffn2", n3, w, "2", tm, tmk, plan)

    def loss_head(r0, h, f, t, post):
        h3 = h + 0.5 * _rms(f, post)
        row = r0 + lax.broadcasted_iota(jnp.int32, h3.shape, 0)
        err = jnp.where(row >= N_META, jnp.where(row < t_real, h3 - t, 0.0), 0.0)
        dh3 = err / d
        df, dpost = _rms_bwd(f, post, 0.5 * dh3)
        return [dh3, df], [_colsum(err * err), _colsum(dpost)]

    (dh3, df2), (loss_vec, d_post2) = _rowwise("loss_head", loss_head, [h2, f2, tgt_p], [nw["ffn2_post_norm"]],
                                               [(d, F32), (d, BF16)], [d, d], tm)
    loss = 0.5 * jnp.sum(loss_vec) / d
    dn3 = _ffn_bwd("ffn2", df2, n3, g2, u2, a2, w["wg2"], w["wu2"], w["wd2"], tmt, tmk, tk, dt, plan)

    def back_mix_norm(r0, h, m, dh3, dn, pre, post):
        dx, dpre = _rms_bwd(h, pre, dn)
        dh2 = dh3 + dx
        dm, dpost = _rms_bwd(m, post, dh2)
        return [dh2, dm], [_colsum(dpre), _colsum(dpost)]

    (dh2, dmix), (d_pre2, d_mix_post) = _rowwise(
        "back_mix_norm", back_mix_norm, [h2, mix, dh3, dn3], [nw["ffn2_pre_norm"], nw["mix_post_norm"]],
        [(d, F32), (d, BF16)], [d, d], tm)
    (dmixcat,) = mm_nt("mix_dx", [dmix], [w["w_out"]], tmt, _tile(2 * hw, 512), outs=(BF16,))
    plan.grads(dict(w_out=mm_tn("mix_dw", mixcat, dmix, 2 * hw // N_CHIPS, dt, tk, GRAD_DT, shard_rows=True)))

    def back_gate(r0, dmc, rg, o_r, gn):
        d_ret, d_om = dmc[:, :hw], dmc[:, hw:]
        yh, rs = _group_norm(o_r)
        d_rg = d_ret * (yh * gn) * _dsilu(rg)
        dy = d_ret * _silu(rg)
        gyh = dy * gn
        d_or = _heads(lambda h: rs[h] * (_head(gyh, h) - _head(yh, h) * jnp.mean(_head(gyh, h) * _head(yh, h),
                                                                                    axis=-1, keepdims=True)))
        return [d_or, d_rg, d_om], [_colsum(dy * yh)]

    (d_or, d_rg, d_om), (d_gn,) = _rowwise("back_gate", back_gate, [dmixcat, (proj_r, hw, 3), o_r],
                                           [nw["ret_group_norm"]], [(hw, BF16), (hw, F32), (hw, BF16)], [hw], tm)
    dqc, dkc, dvv = attn_bwd("mla_bwd", qc, kc, vv, d_om, o_m, lse, blk, MLA_SCALE, host=plan.host("mla_bwd"))
    plan.done("mla_bwd", None)
    drq, drk, drv = ret_bwd("ret_bwd", rq, rk, rv, d_or, ret_states, blk)

    def back_qk(r0, dqc, dkc, dvv, cm, ta, tb):
        dkc = dkc.astype(F32)
        dqp = jnp.concatenate(
            [part for h in range(HEADS)
             for part in (_head(dqc, h, MLA_QK_PAD), _rope_m_t(_head(dqc, h, MLA_QK_PAD, HEAD_DIM), cm, ta, tb))],
            axis=1)
        dkn = _heads(lambda h: _head(dkc, h, MLA_QK_PAD))
        dkr = _head(dkc, 0, MLA_QK_PAD, HEAD_DIM)
        for h in range(1, HEADS):
            dkr = dkr + _head(dkc, h, MLA_QK_PAD, HEAD_DIM)
        return [dqp, dkn, _rope_m_t(dkr, cm, ta, tb), dvv], []

    (dqp, dkn, dkr, dvb), _ = _rowwise("back_qk", back_qk, [dqc, dkc, dvv, cos_m, sa, sb], [],
                                       [(qw, BF16), (hw, BF16), (LANES, F32), (hw, BF16)], [], tm)
    (dcqn,) = mm_nt("q_dx", [dqp], [w["wuq"]], tm, MLA_Q_RANK)
    dwuq = mm_tn("q_dw", cqn, dqp, MLA_Q_RANK, _tile(qw, 1024), tk, GRAD_DT)
    (dckvn,) = mm_nt("kv_dx", [dkn, dvb], [w["wuk"], w["wuv"]], tm, MLA_KV_RANK)
    dwuk = mm_tn("k_dw", ckvn, dkn, MLA_KV_RANK, hw, tk, GRAD_DT)
    dwuv = mm_tn("v_dw", ckvn, dvb, MLA_KV_RANK, hw, tk, GRAD_DT)

    def back_proj(r0, drq, drk, drv, d_rg, pc, dcqn, dckvn, dkr, cr, sr, qn, kvn):
        d_q = _heads(lambda h: _rope_r_t(_head(drq, h), cr, sr))
        d_k = _heads(lambda h: _rope_r_t(_head(drk, h), cr, sr) * RET_SCALE)
        dcq, a_q = _rms_bwd(pc[:, :MLA_Q_RANK], qn, dcqn)
        dckv, a_kv = _rms_bwd(pc[:, MLA_Q_RANK:MLA_Q_RANK + MLA_KV_RANK], kvn, dckvn)
        return ([jnp.concatenate([d_q, d_k, drv, d_rg], axis=1), jnp.concatenate([dcq, dckv, dkr], axis=1)],
                [_colsum(a_q), _colsum(a_kv)])

    (dproj_r, dproj_c), (d_qn, d_kvn) = _rowwise(
        "back_proj", back_proj, [drq, drk, drv, d_rg, proj_c, dcqn, dckvn, dkr, cos_r, sin_r],
        [nw["mla_q_norm"], nw["mla_kv_norm"]], [(IN_RET, BF16), (IN_MLA_PAD, BF16)],
        [MLA_Q_RANK, MLA_KV_RANK], tm)
    (dun,) = mm_nt("proj_dx", [dproj_r, dproj_c], [w["w_r"], w["w_c"]], tmt, _tile(d, 512), outs=(BF16,))
    dw_r = mm_tn("proj_dw_r", un, dproj_r, dt, _tile(IN_RET, 1024), tk, GRAD_DT)
    dw_c = mm_tn("proj_dw_c", un, dproj_c, dt, IN_MLA_PAD, tk, GRAD_DT)
    plan.grads(dict(w_r=dw_r, w_c=dw_c, wuq=dwuq, wuk=dwuk, wuv=dwuv))

    def back_ffn1_norm(r0, h, f, dh2, dn, pre, post):
        dx, dpre = _rms_bwd(h, pre, dn)
        dh1 = dh2 + dx
        df, dpost = _rms_bwd(f, post, 0.5 * dh1)
        return [dh1, df], [_colsum(dpre), _colsum(dpost)]

    (dh1, df1), (d_mix_pre, d_post1) = _rowwise(
        "back_ffn1_norm", back_ffn1_norm, [h1, f1, dh2, dun], [nw["mix_pre_norm"], nw["ffn1_post_norm"]],
        [(d, F32), (d, BF16)], [d, d], tm)
    dn1 = _ffn_bwd("ffn1", df1, n1, g1, u1, a1, w["wg1"], w["wu1"], w["wd1"], tmt, tmk, tk, dt, plan)

    def back_input(r0, h, dh1, dn, pre):
        dx, dpre = _rms_bwd(h, pre, dn)
        return [dh1 + dx], [_colsum(dpre)]

    (dh0,), (d_pre1,) = _rowwise("back_input", back_input, [h0, dh1, dn1], [nw["ffn1_pre_norm"]], [(d, F32)], [d], tm)

    small = dict(ffn1_pre_norm=d_pre1, ffn1_post_norm=d_post1, mix_pre_norm=d_mix_pre, ret_group_norm=d_gn,
                 mla_q_norm=d_qn, mla_kv_norm=d_kvn, mix_post_norm=d_mix_post, ffn2_pre_norm=d_pre2,
                 ffn2_post_norm=d_post2)
    return loss, dh0[N_META:t_real], small, dh0[:N_META]


WEIGHTS = ("meta_tokens", "ffn1_pre_norm", "ffn1_w_gate", "ffn1_w_up", "ffn1_w_down", "ffn1_post_norm",
           "mix_pre_norm", "w_in", "ret_group_norm", "mla_q_norm", "mla_w_uq", "mla_kv_norm", "mla_w_uk",
           "mla_w_uv", "w_out", "mix_post_norm", "ffn2_pre_norm", "ffn2_w_gate", "ffn2_w_up", "ffn2_w_down",
           "ffn2_post_norm")
BIG = ("ffn1_w_gate", "ffn1_w_up", "ffn1_w_down", "w_in", "mla_w_uq", "mla_w_uk", "mla_w_uv", "w_out",
       "ffn2_w_gate", "ffn2_w_up", "ffn2_w_down")
NORMS = ("ffn1_pre_norm", "ffn1_post_norm", "mix_pre_norm", "ret_group_norm", "mla_q_norm", "mla_kv_norm",
         "mix_post_norm", "ffn2_pre_norm", "ffn2_post_norm")


def _unshard_cols(g):
    return g.transpose(1, 0, 2).reshape(g.shape[1], -1)


def _shard_cols(a):
    return a.reshape(a.shape[0], N_CHIPS, -1).transpose(1, 0, 2)


def _pack_rows(rows, width):
    rows = [jnp.pad(r, ((0, 0), (0, width - r.shape[1]))) for r in rows]
    n = sum(r.shape[0] for r in rows)
    return jnp.pad(jnp.concatenate(rows, axis=0), ((0, -n % 8), (0, 0)))


def _weight_views(full):
    w = {}
    for n, g in full.items():
        if n == "w_in":
            w_in = _unshard_cols(g)
            w["w_r"] = w_in[:, :IN_RET]
            w["w_c"] = jnp.pad(w_in[:, IN_RET:], ((0, 0), (0, IN_MLA_PAD - IN_MLA)))
        elif n == "mla_w_uq":
            q = _unshard_cols(g).reshape(MLA_Q_RANK, HEADS, HEAD_DIM + MLA_ROPE)
            q = jnp.pad(q, ((0, 0), (0, 0), (0, MLA_QK_PAD - HEAD_DIM - MLA_ROPE)))
            w["wuq"] = q.reshape(MLA_Q_RANK, HEADS * MLA_QK_PAD)
        elif n in ("mla_w_uk", "mla_w_uv"):
            w["wu" + n[-1]] = _unshard_cols(g)
        elif n == "w_out":
            w["w_out"] = g.reshape(-1, g.shape[2])
        else:
            w["w" + n[7] + n[3]] = g
    return w


def _contributions(g):
    ffn = {"g": "gate", "u": "up", "d": "down"}
    c = {f"ffn{n[2]}_w_{ffn[n[1]]}": a for n, a in g.items() if len(n) == 3 and n[2] in "12"}
    if "w_out" in g:
        c["w_out"] = g["w_out"]
    if "w_r" in g:
        dwuq = g["wuq"].reshape(MLA_Q_RANK, HEADS, MLA_QK_PAD)[:, :, :HEAD_DIM + MLA_ROPE]
        c.update(w_in=_shard_cols(jnp.concatenate([g["w_r"], g["w_c"][:, :IN_MLA]], axis=1)),
                 mla_w_uq=_shard_cols(dwuq.reshape(MLA_Q_RANK, -1)), mla_w_uk=_shard_cols(g["wuk"]),
                 mla_w_uv=_shard_cols(g["wuv"]))
    return c


class _Schedule(_Alone):
    FIRST = ("ffn1_w_gate",)
    CARRIED = {"ffn1_gate": ("ffn1_w_up",), "ffn1_up": ("ffn1_w_down",),
               "ffn1_down": ("w_in", "mla_w_uq", "mla_w_uk", "mla_w_uv"),
               "proj_r": ("w_out",), "mix_fwd": ("ffn2_w_gate", "ffn2_w_up"), "ffn2_up": ("ffn2_w_down",)}
    GRAD_HOST = dict(ffn2_w_gate="mla_bwd", ffn2_w_up="mla_bwd", ffn2_w_down="mla_bwd",
                     w_out="mla_bwd", w_in="ffn1_da", mla_w_uq="ffn1_da", mla_w_uk="ffn1_da", mla_w_uv="ffn1_da",
                     ffn1_w_gate="ffn1_dwu", ffn1_w_up="ffn1_dwd", ffn1_w_down="ffn1_dn")

    def __init__(self, shards):
        self.shards = shards
        self.gathers = {k: (gather_halves([shards[n] for n in names]), names) for k, names in self.CARRIED.items()}
        self.waiting = {}
        self.exchanges = {}
        self.grad = {}

    def host(self, kernel_name):
        if kernel_name in self.gathers:
            return self.gathers[kernel_name][0]
        if kernel_name in self.waiting:
            names, sums = zip(*self.waiting.pop(kernel_name))
            self.exchanges[kernel_name] = (chip_exchange(list(sums)), names, sums)
            return self.exchanges[kernel_name][0]
        return None

    def done(self, kernel_name, w):
        if kernel_name in self.gathers:
            comm, names = self.gathers[kernel_name]
            w.update(_weight_views({n: sibling_fill("fill_" + n, b, self.shards[n])
                                    for n, b in zip(names, comm.result)}))
        elif kernel_name in self.exchanges:
            comm, names, sums = self.exchanges[kernel_name]
            for n, q, mine in zip(names, comm.result, sums):
                self.grad[n] = reduce_join("reduce_join_" + n, q, mine)

    def grads(self, g):
        for n, a in _contributions(g).items():
            self.waiting.setdefault(self.GRAD_HOST[n], []).append((n, pair_add("pair_add_" + n, a)))


def _step(p, m, v, x, loss_target):
    d = x.shape[2]
    names = list(BIG)
    shards = {n: p[n][0].astype(BF16) for n in names}
    first = _Schedule.FIRST
    gathered = gather_halves([shards[n] for n in first] + [p["meta_tokens"]]).run("gather_first")
    filled = [sibling_fill("fill_" + n, b, shards.get(n, p["meta_tokens"]))
              for n, b in zip(first + ("meta_tokens",), gathered)]
    w = _weight_views(dict(zip(first, filled[:-1])))
    meta = _unshard_cols(filled[-1])
    nw = {n: p[n] for n in NORMS}
    plan = _Schedule(shards)
    loss, grad_x, small, d_meta = _local_step(x[0], loss_target[0], meta, w, nw, plan)
    grads = dict(plan.grad)

    width = max(d, HEADS * HEAD_DIM)
    packed = _pack_rows([small[n] for n in NORMS] + [d_meta], width)
    total = sum_slots("small_sum", all_gather_devices(packed), F32)
    for i, n in enumerate(NORMS):
        grads[n] = total[i:i + 1, :p[n].shape[1]]
    cols = p["meta_tokens"].shape[1]
    chip = 2 * lax.axis_index("x") + lax.axis_index("y")
    grads["meta_tokens"] = lax.dynamic_slice(total[len(NORMS):len(NORMS) + N_META, :d], (0, chip * cols), (N_META, cols))

    delta, new_m, new_v = {}, {}, {}
    for n in names + ["meta_tokens"]:
        shape = p[n].shape
        flat = lambda a: a.reshape(-1, shape[-1])
        out = adamw("adamw_" + n, flat(p[n]), flat(grads[n]), flat(m[n]), flat(v[n]))
        delta[n], new_m[n], new_v[n], grads[n] = (o.reshape(shape) for o in out)
    pk = lambda src: _pack_rows([src[n] for n in NORMS], width)
    out = adamw("adamw_norms", pk(p), pk(grads), pk(m), pk(v))
    for i, n in enumerate(NORMS):
        delta[n], new_m[n], new_v[n] = (o[i:i + 1, :p[n].shape[1]] for o in out[:3])

    loss = lax.psum(loss, ("x", "y", "c"))
    return loss, grad_x[None], grads, delta, new_m, new_v


def kernel(x, meta_tokens, ffn1_pre_norm, ffn1_w_gate, ffn1_w_up, ffn1_w_down, ffn1_post_norm, mix_pre_norm, w_in, ret_group_norm, mla_q_norm, mla_w_uq, mla_kv_norm, mla_w_uk, mla_w_uv, w_out, mix_post_norm, ffn2_pre_norm, ffn2_w_gate, ffn2_w_up, ffn2_w_down, ffn2_post_norm, loss_target, m_meta_tokens, m_ffn1_pre_norm, m_ffn1_w_gate, m_ffn1_w_up, m_ffn1_w_down, m_ffn1_post_norm, m_mix_pre_norm, m_w_in, m_ret_group_norm, m_mla_q_norm, m_mla_w_uq, m_mla_kv_norm, m_mla_w_uk, m_mla_w_uv, m_w_out, m_mix_post_norm, m_ffn2_pre_norm, m_ffn2_w_gate, m_ffn2_w_up, m_ffn2_w_down, m_ffn2_post_norm, v_meta_tokens, v_ffn1_pre_norm, v_ffn1_w_gate, v_ffn1_w_up, v_ffn1_w_down, v_ffn1_post_norm, v_mix_pre_norm, v_w_in, v_ret_group_norm, v_mla_q_norm, v_mla_w_uq, v_mla_kv_norm, v_mla_w_uk, v_mla_w_uv, v_w_out, v_mix_post_norm, v_ffn2_pre_norm, v_ffn2_w_gate, v_ffn2_w_up, v_ffn2_w_down, v_ffn2_post_norm):
    args = locals()
    p = {n: args[n] for n in WEIGHTS}
    m = {n: args["m_" + n] for n in WEIGHTS}
    v = {n: args["v_" + n] for n in WEIGHTS}
    loss, grad_x, grads, delta, new_m, new_v = _step(p, m, v, x, loss_target)
    return (loss, grad_x, *[grads[n] for n in WEIGHTS], *[delta[n] for n in WEIGHTS],
            *[new_m[n] for n in WEIGHTS], *[new_v[n] for n in WEIGHTS])
```

```python
import functools
import math

import jax
import jax.numpy as jnp
import numpy as np
from jax import lax
from jax.experimental import pallas as pl
from jax.experimental.pallas import tpu as pltpu

F32 = jnp.float32
BF16 = jnp.bfloat16

EPS = 1e-6
N_META = 16
HEADS = 8
HEAD_DIM = 128
MLA_ROPE = 64
MLA_QK_PAD = 256
MLA_Q_RANK = 512
MLA_KV_RANK = 256
ROPE_THETA = 10000.0
N_CHIPS = 4
LANES = 128
VMEM_LIMIT = 60 * 2 ** 20

ADAM_LR = 0.001
ADAM_B1 = 0.9
ADAM_B2 = 0.999
ADAM_EPS = 1e-08
ADAM_WD = 0.01
ADAM_STEP = 10

NN = (((1,), (0,)), ((), ()))
NT = (((1,), (1,)), ((), ()))
TN = (((0,), (0,)), ((), ()))
MESH = pl.DeviceIdType.MESH


def _tile(n, pref, align=LANES):
    if n <= pref:
        return n
    best = 0
    for t in range(align, pref + 1, align):
        if n % t == 0:
            best = t
    assert best, (n, pref)
    return best


def _params(*sem, collective_id=None):
    return pltpu.CompilerParams(dimension_semantics=sem, vmem_limit_bytes=VMEM_LIMIT, collective_id=collective_id)


SIBLING_BARRIER = 0


def _sds(shape, dtype):
    return jax.ShapeDtypeStruct(tuple(shape), dtype)


def _rowwise(name, fn, rows, consts, outs, accs, tr):
    rows = [r if isinstance(r, tuple) else (r, r.shape[1], 0) for r in rows]
    t = rows[0][0].shape[0]
    assert t % tr == 0
    n_r, n_c, n_o = len(rows), len(consts), len(outs)

    def body(*refs):
        i = pl.program_id(0)
        r = [x[...] for x in refs[:n_r]]
        c = [x[...] for x in refs[n_r:n_r + n_c]]
        o_refs = refs[n_r + n_c:n_r + n_c + n_o]
        a_refs = refs[n_r + n_c + n_o:]
        o_vals, a_vals = fn(i * tr, *r, *c)
        for ref, v in zip(o_refs, o_vals):
            ref[...] = v.astype(ref.dtype)
        if a_refs:
            @pl.when(i == 0)
            def _():
                for ref, v in zip(a_refs, a_vals):
                    ref[...] = v

            @pl.when(i > 0)
            def _():
                for ref, v in zip(a_refs, a_vals):
                    ref[...] += v

    in_specs = [pl.BlockSpec((tr, w), functools.partial(lambda cb, i: (i, cb), cb)) for _, w, cb in rows]
    in_specs += [pl.BlockSpec(a.shape, lambda i: (0, 0)) for a in consts]
    out_specs = [pl.BlockSpec((tr, w), lambda i: (i, 0)) for w, _ in outs]
    out_specs += [pl.BlockSpec((1, w), lambda i: (0, 0)) for w in accs]
    out_shape = [_sds((t, w), dt) for w, dt in outs] + [_sds((1, w), F32) for w in accs]
    res = pl.pallas_call(
        body, name=name, grid=(t // tr,), in_specs=in_specs, out_specs=out_specs, out_shape=out_shape,
        compiler_params=_params("arbitrary"),
    )(*[a for a, _, _ in rows], *consts)
    return res[:n_o], res[n_o:]


def _rms(x, w):
    r = lax.rsqrt(jnp.mean(x * x, axis=-1, keepdims=True) + EPS)
    return x * r * w


def _rms_bwd(x, w, dy):
    r = lax.rsqrt(jnp.mean(x * x, axis=-1, keepdims=True) + EPS)
    xh = x * r
    gy = dy * w
    dx = r * (gy - xh * jnp.mean(gy * xh, axis=-1, keepdims=True))
    return dx, dy * xh


def _colsum(v):
    return jnp.sum(v, axis=0, keepdims=True)


def _silu(x):
    return x * jax.nn.sigmoid(x)


def _dsilu(x):
    s = jax.nn.sigmoid(x)
    return s * (1.0 + x * (1.0 - s))


def _rope_r(x, cos, sin):
    return x * cos + pltpu.roll(x, 64, 1) * sin


def _rope_r_t(dy, cos, sin):
    return dy * cos + pltpu.roll(dy * sin, 64, 1)


def _rope_m(x, cos, sa, sb):
    return x * cos + pltpu.roll(x, 32, 1) * sa + pltpu.roll(x, 96, 1) * sb


def _rope_m_t(dy, cos, sa, sb):
    return dy * cos + pltpu.roll(dy * sa, 96, 1) + pltpu.roll(dy * sb, 32, 1)


def _rope_tables(t):
    pos = np.arange(t, dtype=np.float32)

    def cs(dim):
        inv = np.float32(ROPE_THETA) ** (-np.arange(0, dim, 2, dtype=np.float32) / np.float32(dim))
        ang = pos[:, None] * inv[None, :]
        return np.cos(ang), np.sin(ang)

    c, s = cs(HEAD_DIM)
    cos_r = np.concatenate([c, c], axis=1)
    sin_r = np.concatenate([-s, s], axis=1)
    c, s = cs(MLA_ROPE)
    z32, z64 = np.zeros_like(s), np.zeros((t, 64), np.float32)
    cos_m = np.concatenate([c, c, z64], axis=1)
    sa = np.concatenate([z32, s, z64], axis=1)
    sb = np.concatenate([-s, z32, z64], axis=1)
    return tuple(jnp.asarray(a, F32) for a in (cos_r, sin_r, cos_m, sa, sb))


def _call(body, name, grid, in_specs, out_specs, out_shape, scratch, operands, host=None):
    sem = ("arbitrary",) * len(grid)
    if host is None:
        return pl.pallas_call(body, name=name, grid=grid, in_specs=in_specs, out_specs=out_specs, out_shape=out_shape,
                              scratch_shapes=scratch, compiler_params=_params(*sem))(*operands)
    n_in, n_out, n_s = len(in_specs), len(out_shape), len(scratch)
    h_in, h_out = len(host.ins), len(host.out_shape)

    def hosted(*refs):
        a = n_in
        b = a + h_in
        c = b + n_out
        d = c + h_out
        e = d + n_s
        ids = [pl.program_id(i) for i in range(len(grid))]
        first = functools.reduce(jnp.logical_and, [i == 0 for i in ids])
        last = functools.reduce(jnp.logical_and, [i == g - 1 for i, g in zip(ids, grid)])
        comm = (refs[a:b], refs[c:d], refs[e:])

        @pl.when(first)
        def _():
            host.start(*comm)

        body(*refs[:a], *refs[b:c], *refs[d:e])

        @pl.when(last)
        def _():
            host.finish(*comm)

    hbm = pl.BlockSpec(memory_space=pl.ANY)
    res = pl.pallas_call(
        hosted, name=name, grid=grid, in_specs=list(in_specs) + [hbm] * h_in, out_specs=list(out_specs) + [hbm] * h_out,
        out_shape=list(out_shape) + list(host.out_shape), scratch_shapes=list(scratch) + host.scratch(),
        compiler_params=_params(*sem, collective_id=host.collective_id))(*operands, *host.ins)
    host.result = res[n_out:]
    return res[:n_out]


def _mm(name, grid, operands, in_specs, dns, out_specs, out_shape, epilogue=None, extras=(), extra_specs=(),
        acc_shape=None, host=None):
    n_p, n_e = len(dns), len(extras)
    nk = grid[2]
    n_o = len(out_shape)
    in_place = nk > 1 and epilogue is None and n_o == 1 and out_shape[0].dtype == F32

    def body(*refs):
        ab = refs[:2 * n_p]
        ex = refs[2 * n_p:2 * n_p + n_e]
        outs = refs[2 * n_p + n_e:2 * n_p + n_e + n_o]

        part = None
        for p in range(n_p):
            d = lax.dot_general(ab[2 * p][...], ab[2 * p + 1][...], dns[p], preferred_element_type=F32)
            part = d if part is None else part + d

        def finish(acc):
            vals = (acc,) if epilogue is None else epilogue(acc, *[e[...] for e in ex])
            for o, v in zip(outs, vals):
                o[...] = v.astype(o.dtype)

        if nk == 1:
            finish(part)
        else:
            acc_ref = outs[0] if in_place else refs[2 * n_p + n_e + n_o]
            k = pl.program_id(2)

            @pl.when(k == 0)
            def _():
                acc_ref[...] = part

            @pl.when(k > 0)
            def _():
                acc_ref[...] += part

            if not in_place:
                @pl.when(k == nk - 1)
                def _():
                    finish(acc_ref[...])

    scratch = [] if nk == 1 or in_place else [pltpu.VMEM(acc_shape, F32)]
    return _call(body, name, grid, list(in_specs) + list(extra_specs), out_specs, out_shape, scratch,
                 [*operands, *extras], host)


def mm_nn(name, x, w, tm, out_dtype=F32, epilogue=None, outs=None, w2=None, n_block=None, extras=(), host=None):
    t, kdim = x.shape
    if w.ndim == 3:
        s, _, ns = w.shape
        n, tn, nb = s * ns, ns, s
        wspec = pl.BlockSpec((None, kdim, ns), lambda j, i, k: (j, 0, 0))
    else:
        n = w.shape[1]
        tn = n_block or n
        nb = n // tn
        wspec = pl.BlockSpec((kdim, tn), lambda j, i, k: (0, j))
    xspec = pl.BlockSpec((tm, kdim), lambda j, i, k: (i, 0))
    ospec = pl.BlockSpec((tm, tn), lambda j, i, k: (i, j))
    outs = outs or [out_dtype]
    grid = (nb, t // tm, 1)
    if w2 is None:
        return _mm(name, grid, [x, w], [xspec, wspec], [NN], [ospec] * len(outs), [_sds((t, n), d) for d in outs],
                   epilogue=epilogue, extras=extras, extra_specs=[ospec] * len(extras), host=host)

    def body(x_ref, w_ref, w2_ref, *o_refs):
        xv = x_ref[...]
        a = jnp.dot(xv, w_ref[...], preferred_element_type=F32)
        b = jnp.dot(xv, w2_ref[...], preferred_element_type=F32)
        for o, v in zip(o_refs, epilogue(a, b)):
            o[...] = v.astype(o.dtype)

    return _call(body, name, grid[:2],
                 [pl.BlockSpec((tm, kdim), lambda j, i: (i, 0)),
                  pl.BlockSpec((None, kdim, tn), lambda j, i: (j, 0, 0)),
                  pl.BlockSpec((None, kdim, tn), lambda j, i: (j, 0, 0))],
                 [pl.BlockSpec((tm, tn), lambda j, i: (i, j))] * len(outs), [_sds((t, n), d) for d in outs], [],
                 [x, w, w2], host)


def mm_nn_k(name, x, w, tm, tk, out_dtype=F32, host=None):
    t, kdim = x.shape
    n = w.shape[1]
    grid = (t // tm, 1, kdim // tk)
    return _mm(name, grid, [x, w],
               [pl.BlockSpec((tm, tk), lambda i, j, k: (i, k)), pl.BlockSpec((tk, n), lambda i, j, k: (k, 0))],
               [NN], [pl.BlockSpec((tm, n), lambda i, j, k: (i, 0))], [_sds((t, n), out_dtype)],
               acc_shape=(tm, n), host=host)[0]


def mm_nt(name, xs, ws, tm, tn, outs=(F32,), epilogue=None, extras=(), host=None):
    t = xs[0].shape[0]
    specs, ops = [], []
    for x, w in zip(xs, ws):
        kdim = x.shape[1]
        specs.append(pl.BlockSpec((tm, kdim), lambda j, i, k: (i, 0)))
        if w.ndim == 3:
            assert tn == w.shape[1]
            n = w.shape[0] * w.shape[1]
            specs.append(pl.BlockSpec((None, tn, kdim), lambda j, i, k: (j, 0, 0)))
        else:
            n = w.shape[0]
            specs.append(pl.BlockSpec((tn, kdim), lambda j, i, k: (j, 0)))
        ops += [x, w]
    ospec = pl.BlockSpec((tm, tn), lambda j, i, k: (i, j))
    return _mm(name, (n // tn, t // tm, 1), ops, specs, [NT] * len(xs), [ospec] * len(outs),
               [_sds((t, n), d) for d in outs], epilogue=epilogue, extras=extras,
               extra_specs=[ospec] * len(extras), host=host)


def mm_nt_k(name, xs, ws, tm, tn, out_dtype=F32, host=None):
    t = xs[0].shape[0]
    s, n, ns = ws[0].shape
    specs, ops = [], []
    for x, w in zip(xs, ws):
        specs.append(pl.BlockSpec((tm, ns), lambda i, j, k: (i, k)))
        specs.append(pl.BlockSpec((None, tn, ns), lambda i, j, k: (k, j, 0)))
        ops += [x, w]
    return _mm(name, (t // tm, n // tn, s), ops, specs, [NT] * len(xs),
               [pl.BlockSpec((tm, tn), lambda i, j, k: (i, j))], [_sds((t, n), out_dtype)], acc_shape=(tm, tn),
               host=host)[0]


def mm_tn(name, x, y, tm, tn, tk, out_dtype, shard_rows=False, shard_cols=False, host=None):
    t, m = x.shape
    n = y.shape[1]
    grid = (m // tm, n // tn, t // tk)
    if shard_cols:
        ospec = pl.BlockSpec((None, tm, tn), lambda i, j, k: (j, i, 0))
        oshape = _sds((n // tn, m, tn), out_dtype)
    elif shard_rows:
        ospec = pl.BlockSpec((None, tm, tn), lambda i, j, k: (i, 0, j))
        oshape = _sds((m // tm, tm, n), out_dtype)
    else:
        ospec = pl.BlockSpec((tm, tn), lambda i, j, k: (i, j))
        oshape = _sds((m, n), out_dtype)
    return _mm(name, grid, [x, y],
               [pl.BlockSpec((tk, tm), lambda i, j, k: (k, i)), pl.BlockSpec((tk, tn), lambda i, j, k: (k, j))],
               [TN], [ospec], [oshape], acc_shape=(tm, tn), host=host)[0]


def _decay_logs():
    return [math.log(1.0 - 2.0 ** (-5.0 - h)) for h in range(HEADS)]


def _log_decay(h):
    lg = jnp.float32(_decay_logs()[0])
    for i in range(1, HEADS):
        lg = jnp.where(h == i, jnp.float32(_decay_logs()[i]), lg)
    return lg


def _decayed_scores(q, k, lg):
    s = lax.dot_general(q, k, NT, preferred_element_type=F32)
    row = lax.broadcasted_iota(jnp.int32, s.shape, 0)
    col = lax.broadcasted_iota(jnp.int32, s.shape, 1)
    dec = jnp.where(col <= row, jnp.exp(jnp.maximum(row - col, 0).astype(F32) * lg), 0.0)
    return s * dec, dec


def _causal(s):
    row = lax.broadcasted_iota(jnp.int32, s.shape, 0)
    col = lax.broadcasted_iota(jnp.int32, s.shape, 1)
    return jnp.where(col <= row, s, -1e30)


def _attn_fwd_step(qi, blk, scale, q_ref, k_ref, v_ref, o_ref, lse_ref, m_ref, l_ref, acc_ref):
    qv = q_ref[...]
    m_ref[...] = jnp.full_like(m_ref, -1e30)
    l_ref[...] = jnp.zeros_like(l_ref)
    acc_ref[...] = jnp.zeros_like(acc_ref)

    def keys(start, n, diag_from):
        rows = pl.ds(pl.multiple_of(start, blk), n)
        s = lax.dot_general(qv, k_ref[rows, :], NT, preferred_element_type=F32) * scale
        if diag_from is not None:
            row = lax.broadcasted_iota(jnp.int32, s.shape, 0)
            col = lax.broadcasted_iota(jnp.int32, s.shape, 1)
            s = jnp.where(col - diag_from <= row, s, -1e30)
        m = m_ref[...]
        m_new = jnp.maximum(m, jnp.max(s, axis=-1, keepdims=True))
        p = jnp.exp(s - m_new)
        alpha = jnp.exp(m - m_new)
        m_ref[...] = m_new
        l_ref[...] = alpha * l_ref[...] + jnp.sum(p, axis=-1, keepdims=True)
        acc_ref[...] = alpha * acc_ref[...] + jnp.dot(p.astype(BF16), v_ref[rows, :], preferred_element_type=F32)

    @pl.loop(0, qi // 2)
    def _(j):
        keys(j * (2 * blk), 2 * blk, None)

    @pl.when(qi % 2 == 1)
    def _():
        keys((qi - 1) * blk, 2 * blk, blk)

    @pl.when(qi % 2 == 0)
    def _():
        keys(qi * blk, blk, 0)

    l = l_ref[...]
    o_ref[...] = acc_ref[...] / l
    lse_ref[...] = jnp.broadcast_to(m_ref[...] + jnp.log(l), (blk, HEAD_DIM))


def mix_fwd(name, q, k, v, rq, rk, rv, blk, scale, host=None):
    t = q.shape[0]
    dq = q.shape[1] // HEADS
    nq = t // blk

    def body(q_ref, k_ref, v_ref, rq_ref, rk_ref, rv_ref, o_ref, lse_ref, ro_ref, st_ref, m_ref, l_ref, acc_ref, state):
        h, i = pl.program_id(0), pl.program_id(1)

        @pl.when(h < HEADS)
        def _():
            _attn_fwd_step(i, blk, scale, q_ref, k_ref, v_ref, o_ref, lse_ref, m_ref, l_ref, acc_ref)

        @pl.when(h >= HEADS)
        def _():
            _ret_fwd_step(h - HEADS, i, blk, rq_ref, rk_ref, rv_ref, ro_ref, st_ref, state)

    mla = lambda w, whole=False: pl.BlockSpec(
        (t if whole else blk, w),
        lambda h, i: (0 if whole else jnp.where(h < HEADS, i, nq - 1), jnp.minimum(h, HEADS - 1)))
    ret = lambda rows: pl.BlockSpec(
        (rows, HEAD_DIM), lambda h, i: (jnp.where(h >= HEADS, i, 0), jnp.maximum(h - HEADS, 0)))
    wide = _sds((t, HEADS * HEAD_DIM), F32)
    return _call(body, name, (2 * HEADS, nq),
                 [mla(dq), mla(dq, True), mla(HEAD_DIM, True), ret(blk), ret(blk), ret(blk)],
                 [mla(HEAD_DIM), mla(HEAD_DIM), ret(blk), ret(HEAD_DIM)],
                 [wide, wide, wide, _sds((nq * HEAD_DIM, HEADS * HEAD_DIM), F32)],
                 [pltpu.VMEM((blk, 1), F32), pltpu.VMEM((blk, 1), F32), pltpu.VMEM((blk, HEAD_DIM), F32),
                  pltpu.VMEM((HEAD_DIM, HEAD_DIM), F32)],
                 [q, k, v, rq, rk, rv], host)


def attn_bwd(name, q, k, v, do, o, lse, blk, scale, host=None):
    t = q.shape[0]
    dq_w = q.shape[1] // HEADS
    nb = t // blk

    def body(q_ref, k_ref, v_ref, do_ref, o_ref, lse_ref, dq_ref, dk_out, dv_out, dk_ref, dv_ref):
        ki = pl.program_id(1)
        kv = k_ref[...]
        vv = v_ref[...]

        @pl.when(ki == 0)
        def _():
            dq_ref[...] = jnp.zeros_like(dq_ref)

        def queries(start, n, diag):
            rows = pl.ds(pl.multiple_of(start, blk), n)
            qv, dov = q_ref[rows, :], do_ref[rows, :]
            s = lax.dot_general(qv, kv, NT, preferred_element_type=F32) * scale
            if diag:
                s = _causal(s)
            p = jnp.exp(s - lse_ref[rows, :][:, :1])
            dp = lax.dot_general(dov, vv, NT, preferred_element_type=F32)
            delta = jnp.sum(dov.astype(F32) * o_ref[rows, :], axis=-1, keepdims=True)
            ds = p * (dp - delta) * scale
            pb, dsb = p.astype(BF16), ds.astype(BF16)
            dv_ref[...] += lax.dot_general(pb, dov, TN, preferred_element_type=F32)
            dk_ref[...] += lax.dot_general(dsb, qv, TN, preferred_element_type=F32)
            dq_ref[rows, :] += jnp.dot(dsb, kv, preferred_element_type=F32)

        dk_ref[...] = jnp.zeros_like(dk_ref)
        dv_ref[...] = jnp.zeros_like(dv_ref)
        queries(ki * blk, blk, True)
        later = nb - 1 - ki

        @pl.when(later % 2 == 1)
        def _():
            queries((ki + 1) * blk, blk, False)

        @pl.loop(0, later // 2)
        def _(j):
            queries((ki + 1 + later % 2 + 2 * j) * blk, 2 * blk, False)

        dk_out[...] = dk_ref[...].astype(dk_out.dtype)
        dv_out[...] = dv_ref[...].astype(dv_out.dtype)

    full = lambda w: pl.BlockSpec((t, w), lambda h, j: (0, h))
    blkd = lambda w: pl.BlockSpec((blk, w), lambda h, j: (j, h))
    return _call(body, name, (HEADS, nb),
                 [full(dq_w), blkd(dq_w), blkd(HEAD_DIM), full(HEAD_DIM), full(HEAD_DIM), full(HEAD_DIM)],
                 [full(dq_w), blkd(dq_w), blkd(HEAD_DIM)],
                 [_sds(q.shape, F32), _sds(k.shape, BF16), _sds(v.shape, BF16)],
                 [pltpu.VMEM((blk, dq_w), F32), pltpu.VMEM((blk, HEAD_DIM), F32)], [q, k, v, do, o, lse], host)


def _chunk_decays(lg, blk):
    row = lax.broadcasted_iota(jnp.int32, (blk, HEAD_DIM), 0).astype(F32)
    return jnp.exp(lg * (row + 1.0)), jnp.exp(lg * (blk - 1.0 - row)), jnp.exp(lg * blk * jnp.ones((1, HEAD_DIM), F32))


def _ret_fwd_step(h, i, blk, q_ref, k_ref, v_ref, o_ref, st_ref, state):
    lg = _log_decay(h)

    @pl.when(i == 0)
    def _():
        state[...] = jnp.zeros_like(state)

    qv, kv, vv = q_ref[...], k_ref[...], v_ref[...]
    before = state[...]
    st_ref[...] = before
    p, _ = _decayed_scores(qv, kv, lg)
    xi, zeta, g_blk = _chunk_decays(lg, blk)
    o_ref[...] = (jnp.dot(p.astype(BF16), vv, preferred_element_type=F32)
                  + jnp.dot(qv, before.astype(BF16), preferred_element_type=F32) * xi)
    kz = (kv.astype(F32) * zeta).astype(BF16)
    state[...] = before * g_blk + lax.dot_general(kz, vv, TN, preferred_element_type=F32)


def ret_bwd(name, q, k, v, do, states, blk, host=None):
    t = q.shape[0]
    nb = t // blk

    def body(q_ref, k_ref, v_ref, do_ref, st_ref, dq_ref, dk_ref, dv_ref, dstate):
        h, i = pl.program_id(0), pl.program_id(1)
        lg = _log_decay(h)

        @pl.when(i == 0)
        def _():
            dstate[...] = jnp.zeros_like(dstate)

        qv, kv, vv, dov = q_ref[...], k_ref[...], v_ref[...], do_ref[...]
        before = st_ref[...].astype(BF16)
        after_grad = dstate[...]
        p, dec = _decayed_scores(qv, kv, lg)
        ds = lax.dot_general(dov, vv, NT, preferred_element_type=F32) * dec
        pb, dsb = p.astype(BF16), ds.astype(BF16)
        xi, zeta, g_blk = _chunk_decays(lg, blk)
        dox = (dov.astype(F32) * xi).astype(BF16)
        kz = (kv.astype(F32) * zeta).astype(BF16)
        agb = after_grad.astype(BF16)
        dv = lax.dot_general(pb, dov, TN, preferred_element_type=F32) + jnp.dot(kz, agb, preferred_element_type=F32)
        dq = (jnp.dot(dsb, kv, preferred_element_type=F32)
              + lax.dot_general(dox, before, NT, preferred_element_type=F32))
        dk = (lax.dot_general(dsb, qv, TN, preferred_element_type=F32)
              + lax.dot_general(vv, agb, NT, preferred_element_type=F32) * zeta)
        for ref, val in ((dq_ref, dq), (dk_ref, dk), (dv_ref, dv)):
            ref[...] = val.astype(ref.dtype)
        dstate[...] = after_grad * g_blk + lax.dot_general(qv, dox, TN, preferred_element_type=F32)

    hspec = pl.BlockSpec((blk, HEAD_DIM), lambda h, i: (nb - 1 - i, h))
    return _call(body, name, (HEADS, nb),
                 [hspec] * 4 + [pl.BlockSpec((HEAD_DIM, HEAD_DIM), lambda h, i: (nb - 1 - i, h))],
                 [hspec] * 3, [_sds(q.shape, BF16)] * 3, [pltpu.VMEM((HEAD_DIM, HEAD_DIM), F32)],
                 [q, k, v, do, states], host)


CHIP_FLIPS = ((1, 0), (0, 1), (1, 1))
CHIP_BARRIER = 1
ALL_BARRIER = 2


def _position():
    return lax.axis_index("x"), lax.axis_index("y"), lax.axis_index("c")


def _chip_peers(pos):
    x, y, c = pos
    return [(x ^ fx, y ^ fy, c) for fx, fy in CHIP_FLIPS]


class _Comm:
    def __init__(self, ins, out_shape, plan, n_remote, n_local, collective_id):
        self.ins, self.out_shape, self.plan = list(ins), list(out_shape), plan
        self.n_remote, self.n_local, self.collective_id = n_remote, n_local, collective_id
        self.result = None

    def scratch(self):
        return [pltpu.SemaphoreType.DMA((self.n_remote,)), pltpu.SemaphoreType.DMA((self.n_remote,)),
                pltpu.SemaphoreType.DMA((max(self.n_local, 1),))]

    def _copies(self, in_refs, out_refs, sems):
        send_sems, recv_sems, local_sems = sems
        pos = _position()
        p = self.plan(pos, in_refs, out_refs)

        def remote(k, src, dst, dev):
            return pltpu.make_async_remote_copy(src_ref=src, dst_ref=dst, send_sem=send_sems.at[k],
                                                recv_sem=recv_sems.at[k], device_id=dev, device_id_type=MESH)

        local = [pltpu.make_async_copy(s, d, local_sems.at[i]) for i, (s, d) in enumerate(p["local"])]
        out = [remote(k, s, d, dev) for k, (s, d, dev) in enumerate(p["sends"])]
        arrivals = [functools.partial(remote, k, d, d, pos) for k, d in enumerate(p["recvs"])]
        return local, out, arrivals, p["peers"]

    def start(self, in_refs, out_refs, sems):
        local, out, _, peers = self._copies(in_refs, out_refs, sems)
        barrier = pltpu.get_barrier_semaphore()
        for peer in peers:
            pl.semaphore_signal(barrier, inc=1, device_id=peer, device_id_type=MESH)
        pl.semaphore_wait(barrier, len(peers))
        for cp in local + out:
            cp.start()

    def finish(self, in_refs, out_refs, sems):
        local, out, arrivals, _ = self._copies(in_refs, out_refs, sems)
        for make in arrivals:
            make().wait_recv()
        for cp in out:
            cp.wait_send()
        for cp in local:
            cp.wait()

    def run(self, name):
        n_in, n_out = len(self.ins), len(self.out_shape)

        def body(*refs):
            comm = (refs[:n_in], refs[n_in:n_in + n_out], refs[n_in + n_out:])
            self.start(*comm)
            self.finish(*comm)

        hbm = pl.BlockSpec(memory_space=pl.ANY)
        self.result = pl.pallas_call(
            body, name=name, in_specs=[hbm] * n_in, out_specs=[hbm] * n_out, out_shape=self.out_shape,
            scratch_shapes=self.scratch(),
            compiler_params=pltpu.CompilerParams(collective_id=self.collective_id))(*self.ins)
        return self.result


def _half_rows(c, rows):
    r2 = rows // 2
    return pl.ds(pl.multiple_of(c * r2, math.gcd(r2, LANES)), r2)


def _half(ref, c, rows, lead=()):
    return ref.at[(*lead, _half_rows(c, rows))]


def gather_halves(shards):
    def plan(pos, ins, outs):
        x, y, c = pos
        me = 2 * x + y
        p = dict(local=[], sends=[], recvs=[], peers=_chip_peers(pos))
        for a, (src, dst) in enumerate(zip(ins, outs)):
            rows = shards[a].shape[0]
            for px, py, _ in p["peers"]:
                p["sends"].append((_half(src, c, rows), _half(dst, c, rows, (me,)), (px, py, c)))
                p["recvs"].append(_half(dst, c, rows, (2 * px + py,)))
        return p

    return _Comm(shards, [_sds((N_CHIPS, *s.shape), s.dtype) for s in shards], plan,
                 n_remote=3 * len(shards), n_local=0, collective_id=CHIP_BARRIER)


def chip_exchange(parts):
    def plan(pos, ins, outs):
        x, y, c = pos
        me = 2 * x + y
        p = dict(local=[], sends=[], recvs=[], peers=_chip_peers(pos))
        for src, dst in zip(ins, outs):
            for px, py, _ in p["peers"]:
                peer = 2 * px + py
                p["sends"].append((src.at[peer], dst.at[me], (px, py, c)))
                p["recvs"].append(dst.at[peer])
        return p

    return _Comm(parts, [_sds(g.shape, g.dtype) for g in parts], plan, n_remote=3 * len(parts), n_local=0,
                 collective_id=CHIP_BARRIER)


def all_gather_devices(v):
    flips = [(fx, fy, fc) for fx in (0, 1) for fy in (0, 1) for fc in (0, 1)][1:]

    def plan(pos, ins, outs):
        x, y, c = pos
        me = 4 * x + 2 * y + c
        p = dict(local=[(ins[0], outs[0].at[me])], sends=[], recvs=[],
                 peers=[(x ^ fx, y ^ fy, c ^ fc) for fx, fy, fc in flips])
        for px, py, pc in p["peers"]:
            p["sends"].append((ins[0], outs[0].at[me], (px, py, pc)))
            p["recvs"].append(outs[0].at[4 * px + 2 * py + pc])
        return p

    return _Comm([v], [_sds((8, *v.shape), v.dtype)], plan, n_remote=7, n_local=1,
                 collective_id=ALL_BARRIER).run("small_all_gather")[0]


SWAP_CHUNK_BYTES = 3 * 2 ** 19


def _sibling_stream(t, n, value, consume, sbuf, rbuf, send_sems, recv_sems, credits):
    x, y, c = _position()
    sib = (x, y, 1 - c)

    def copy(slot):
        return pltpu.make_async_remote_copy(src_ref=sbuf.at[slot], dst_ref=rbuf.at[slot], send_sem=send_sems.at[slot],
                                            recv_sem=recv_sems.at[slot], device_id=sib, device_id_type=MESH)

    slot = t % 2

    @pl.when(t == 0)
    def _():
        barrier = pltpu.get_barrier_semaphore()
        pl.semaphore_signal(barrier, inc=1, device_id=sib, device_id_type=MESH)
        pl.semaphore_wait(barrier, 1)

    @pl.when(jnp.logical_and(t >= 2, t < n))
    def _():
        copy(slot).wait_send()
        pl.semaphore_wait(credits.at[slot], 1)

    @pl.when(t < n)
    def _():
        sbuf[slot] = value
        copy(slot).start()

    @pl.when(t >= 1)
    def _():
        prev = 1 - slot
        copy(prev).wait_recv()
        consume(sbuf[prev], rbuf[prev])

        @pl.when(t + 1 < n)
        def _():
            pl.semaphore_signal(credits.at[prev], inc=1, device_id=sib, device_id_type=MESH)

    @pl.when(t == n)
    def _():
        copy(1 - slot).wait_send()
        if n > 1:
            copy(slot).wait_send()


def _swap_scratch(rows, cols, dtype):
    return [pltpu.VMEM((2, rows, cols), dtype), pltpu.VMEM((2, rows, cols), dtype),
            pltpu.SemaphoreType.DMA((2,)), pltpu.SemaphoreType.DMA((2,)), pltpu.SemaphoreType.REGULAR((2,))]


def _chunk_rows(rows, cols, dtype, nbytes=SWAP_CHUNK_BYTES):
    return _tile(rows, max(16, nbytes // (cols * jnp.dtype(dtype).itemsize)), 16)


def pair_add(name, g):
    s, r, c_ = g.shape
    r2 = r // 2
    cr = _chunk_rows(r2, c_, g.dtype)
    nj = r2 // cr

    n = s * nj

    def body(core, mine_ref, theirs_ref, o_ref, *scratch):
        def consume(_, got):
            o_ref[...] = (mine_ref[...].astype(F32) + got.astype(F32)).astype(o_ref.dtype)

        _sibling_stream(pl.program_id(0), n, theirs_ref[...], consume, *scratch)

    sent = lambda t: jnp.minimum(t, n - 1)
    used = lambda t: jnp.maximum(t - 1, 0)
    grid_spec = pltpu.PrefetchScalarGridSpec(
        num_scalar_prefetch=1, grid=(n + 1,),
        in_specs=[pl.BlockSpec((None, cr, c_), lambda t, core: (used(t) // nj, core[0] * nj + used(t) % nj, 0)),
                  pl.BlockSpec((None, cr, c_), lambda t, core: (sent(t) // nj, (1 - core[0]) * nj + sent(t) % nj, 0))],
        out_specs=pl.BlockSpec((None, cr, c_), lambda t, core: (used(t) // nj, used(t) % nj, 0)),
        scratch_shapes=_swap_scratch(cr, c_, g.dtype))
    core = lax.axis_index("c").astype(jnp.int32).reshape(1)
    return pl.pallas_call(body, name=name, grid_spec=grid_spec, out_shape=_sds((s, r2, c_), g.dtype),
                          compiler_params=_params("arbitrary", collective_id=SIBLING_BARRIER))(core, g, g)


def _adamw_math(w, g, m, v):
    m = ADAM_B1 * m + (1.0 - ADAM_B1) * g
    v = ADAM_B2 * v + (1.0 - ADAM_B2) * (g * g)
    m_hat = m / (1.0 - ADAM_B1 ** ADAM_STEP)
    v_hat = v / (1.0 - ADAM_B2 ** ADAM_STEP)
    return -ADAM_LR * (m_hat / (jnp.sqrt(v_hat) + ADAM_EPS) + ADAM_WD * w), m, v


def reduce_join(name, p, mine):
    s, r2, c_ = p.shape
    cr = _chunk_rows(r2, c_, F32)
    nj = r2 // cr

    def body(where, p_ref, mine_ref, o_ref, acc_ref, *scratch):
        c, me = where[0], where[1]
        for chip in range(s):
            @pl.when(me == chip)
            def _():
                terms = [mine_ref[...] if i == chip else p_ref[i] for i in range(s)]
                acc = terms[0].astype(F32)
                for term in terms[1:]:
                    acc = acc + term.astype(F32)
                acc_ref[...] = acc

        def consume(own, got):
            o_ref[c] = own
            o_ref[1 - c] = got

        _sibling_stream(pl.program_id(0), nj, acc_ref[...], consume, *scratch)

    sent = lambda t: jnp.minimum(t, nj - 1)
    grid_spec = pltpu.PrefetchScalarGridSpec(
        num_scalar_prefetch=1, grid=(nj + 1,),
        in_specs=[pl.BlockSpec((s, cr, c_), lambda t, where: (0, sent(t), 0)),
                  pl.BlockSpec((None, cr, c_), lambda t, where: (where[1], sent(t), 0))],
        out_specs=pl.BlockSpec((2, cr, c_), lambda t, where: (0, jnp.maximum(t - 1, 0), 0)),
        scratch_shapes=[pltpu.VMEM((cr, c_), F32)] + _swap_scratch(cr, c_, F32))
    x, y, c = _position()
    where = jnp.stack([c, 2 * x + y]).astype(jnp.int32)
    out = pl.pallas_call(body, name=name, grid_spec=grid_spec, out_shape=_sds((2, r2, c_), F32),
                         compiler_params=_params("arbitrary", collective_id=SIBLING_BARRIER))(where, p, mine)
    return out.reshape(2 * r2, c_)


def sibling_fill(name, buf, shard):
    s, r, c_ = buf.shape
    r2 = r // 2
    cr = _chunk_rows(r2, c_, buf.dtype)
    nj = r2 // cr
    n_peers = len(CHIP_FLIPS)
    n = n_peers * nj
    n_own = r // cr

    def body(where, in_ref, own_ref, o_ref, *scratch):
        t = pl.program_id(0)

        @pl.when(t <= n)
        def _():
            def consume(_, got):
                o_ref[...] = got

            _sibling_stream(t, n, in_ref[...], consume, *scratch)

        @pl.when(t > n)
        def _():
            o_ref[...] = own_ref[...]

    sent = lambda t: jnp.minimum(t, n - 1)
    used = lambda t: jnp.clip(t - 1, 0, n - 1)
    own = lambda t: jnp.clip(t - (n + 1), 0, n_own - 1)
    grid_spec = pltpu.PrefetchScalarGridSpec(
        num_scalar_prefetch=1, grid=(n + 1 + n_own,),
        in_specs=[pl.BlockSpec((None, cr, c_),
                               lambda t, where: (where[sent(t) // nj], where[n_peers] * nj + sent(t) % nj, 0)),
                  pl.BlockSpec((cr, c_), lambda t, where: (own(t), 0))],
        out_specs=pl.BlockSpec(
            (None, cr, c_),
            lambda t, where: (jnp.where(t <= n, where[used(t) // nj], where[n_peers + 1]),
                              jnp.where(t <= n, (1 - where[n_peers]) * nj + used(t) % nj, own(t)), 0)),
        scratch_shapes=_swap_scratch(cr, c_, buf.dtype))
    x, y, c = _position()
    where = jnp.stack([2 * (x ^ fx) + (y ^ fy) for fx, fy in CHIP_FLIPS] + [c, 2 * x + y]).astype(jnp.int32)
    return pl.pallas_call(body, name=name, grid_spec=grid_spec, out_shape=_sds(buf.shape, buf.dtype),
                          input_output_aliases={1: 0},
                          compiler_params=_params("arbitrary", collective_id=SIBLING_BARRIER))(where, buf, shard)


def sum_slots(name, p, out_dtype):
    s, r, c = p.shape
    tr = _tile(r, 256, 16)

    def body(p_ref, o_ref):
        acc = p_ref[0].astype(F32)
        for i in range(1, s):
            acc = acc + p_ref[i].astype(F32)
        o_ref[...] = acc.astype(o_ref.dtype)

    return pl.pallas_call(
        body, name=name, grid=(r // tr,), in_specs=[pl.BlockSpec((s, tr, c), lambda i: (0, i, 0))],
        out_specs=pl.BlockSpec((tr, c), lambda i: (i, 0)), out_shape=_sds((r, c), out_dtype),
        compiler_params=_params("arbitrary"),
    )(p)


def adamw(name, w, g, m, v):
    r, c = w.shape
    outs, _ = _rowwise(name, lambda _, w, g, m, v: ([*_adamw_math(w, g, m, v), g], []), [w, g, m, v], [],
                       [(c, F32)] * 4, [], _tile(r, 256, 8))
    return outs


RET_SCALE = HEAD_DIM ** -0.5
MLA_SCALE = (HEAD_DIM + MLA_ROPE) ** -0.5
GRAD_DT = BF16
IN_RET = 4 * HEADS * HEAD_DIM
IN_MLA = MLA_Q_RANK + MLA_KV_RANK + MLA_ROPE
IN_MLA_PAD = IN_MLA + 64


def _heads(fn):
    return jnp.concatenate([fn(h) for h in range(HEADS)], axis=1)


def _head(a, h, stride=HEAD_DIM, off=0):
    return a[:, h * stride + off:h * stride + off + HEAD_DIM]


def _group_norm(o):
    rs = [lax.rsqrt(jnp.mean(_head(o, h) * _head(o, h), axis=-1, keepdims=True) + EPS) for h in range(HEADS)]
    return _heads(lambda h: _head(o, h) * rs[h]), rs


class _Alone:
    def host(self, kernel_name):
        return None

    def done(self, kernel_name, w):
        pass

    def grads(self, g):
        pass


def _ffn_fwd(tag, n, w, k, tm, tmk, plan):
    gate, up, down = tag + "_gate", tag + "_up", tag + "_down"
    if "wu" + k in w:
        g, u, a = mm_nn(up, n, w["wg" + k], tm, outs=[BF16] * 3, w2=w["wu" + k],
                        epilogue=lambda g, u: (g, u, _silu(g) * u), host=plan.host(up))
    else:
        (g,) = mm_nn(gate, n, w["wg" + k], tm, out_dtype=BF16, host=plan.host(gate))
        plan.done(gate, w)
        u, a = mm_nn(up, n, w["wu" + k], tm, outs=[BF16] * 2, extras=(g,),
                     epilogue=lambda u, g: (u, _silu(g.astype(F32)) * u), host=plan.host(up))
    plan.done(up, w)
    ff = a.shape[1]
    wd = w["wd" + k]
    f = mm_nn_k(down, a, wd.reshape(ff, wd.shape[2]), tmk, _tile(ff, 1408), host=plan.host(down))
    plan.done(down, w)
    return g, u, a, f


def _ffn_bwd(tag, df, n, g, u, a, wg, wu, wd, tm, tmk, tk, dt, plan):
    ns = wg.shape[2]

    def gate_grads(da, g, u):
        g, u = g.astype(F32), u.astype(F32)
        return da * u * _dsilu(g), da * _silu(g)

    def hosted(kernel_name, call):
        out = call(plan.host(kernel_name))
        plan.done(kernel_name, None)
        return out

    k = tag[-1]
    dg, du = hosted(tag + "_da", lambda h: mm_nt(tag + "_da", [df], [wd], tm, ns, outs=(BF16, BF16),
                                                 epilogue=gate_grads, extras=(g, u), host=h))
    dwg = hosted(tag + "_dwg", lambda h: mm_tn(tag + "_dwg", n, dg, dt, ns, tk, GRAD_DT, shard_cols=True, host=h))
    plan.grads({"wg" + k: dwg})
    dwu = hosted(tag + "_dwu", lambda h: mm_tn(tag + "_dwu", n, du, dt, ns, tk, GRAD_DT, shard_cols=True, host=h))
    plan.grads({"wu" + k: dwu})
    dwd = hosted(tag + "_dwd", lambda h: mm_tn(tag + "_dwd", a, df, ns, dt, tk, GRAD_DT, shard_rows=True, host=h))
    plan.grads({"wd" + k: dwd})
    dn = hosted(tag + "_dn", lambda h: mm_nt_k(tag + "_dn", [dg, du], [wg, wu], tmk, dt, out_dtype=BF16, host=h))
    return dn


def _local_step(x, tgt, meta, w, nw, plan):
    seq, d = x.shape
    t_real = N_META + seq
    tp = -(-t_real // LANES) * LANES
    zpad = jnp.zeros((tp - t_real, d), F32)
    h0 = jnp.concatenate([meta, x, zpad], axis=0)
    tgt_p = jnp.concatenate([jnp.zeros((N_META, d), F32), tgt, zpad], axis=0)
    cos_r, sin_r, cos_m, sa, sb = _rope_tables(tp)
    tm = _tile(tp, 512)
    tmt = _tile(tp, 768, 16)
    tmk = _tile(tp, 1408)
    tk = tp
    blk = tm
    dt = _tile(d, 1024)
    hw = HEADS * HEAD_DIM
    qw = HEADS * MLA_QK_PAD

    (n1,), _ = _rowwise("ffn1_norm", lambda r0, h, g: ([_rms(h, g)], []), [h0], [nw["ffn1_pre_norm"]],
                        [(d, BF16)], [], tm)
    g1, u1, a1, f1 = _ffn_fwd("ffn1", n1, w, "1", tm, tmk, plan)

    def post_ffn1(r0, h, f, post, pre):
        h1 = h + 0.5 * _rms(f, post)
        return [h1, _rms(h1, pre)], []

    (h1, un), _ = _rowwise("mix_norm", post_ffn1, [h0, f1], [nw["ffn1_post_norm"], nw["mix_pre_norm"]],
                           [(d, F32), (d, BF16)], [], tm)
    (proj_r,) = mm_nn("proj_r", un, w["w_r"], tm, n_block=_tile(IN_RET, 1024), host=plan.host("proj_r"))
    plan.done("proj_r", w)
    (proj_c,) = mm_nn("proj_c", un, w["w_c"], tm)

    def split_proj(r0, pr, pc, cr, sr, cm, ta, tb, qn, kvn):
        rq = _heads(lambda h: _rope_r(_head(pr, h), cr, sr))
        rk = _heads(lambda h: _rope_r(_head(pr, h, off=hw), cr, sr) * RET_SCALE)
        rv = pr[:, 2 * hw:3 * hw]
        cqn = _rms(pc[:, :MLA_Q_RANK], qn)
        ckvn = _rms(pc[:, MLA_Q_RANK:MLA_Q_RANK + MLA_KV_RANK], kvn)
        krr = _rope_m(pc[:, MLA_Q_RANK + MLA_KV_RANK:], cm, ta, tb)
        return [rq, rk, rv, cqn, ckvn, krr], []

    (rq, rk, rv, cqn, ckvn, krr), _ = _rowwise(
        "split_proj", split_proj, [proj_r, proj_c, cos_r, sin_r, cos_m, sa, sb],
        [nw["mla_q_norm"], nw["mla_kv_norm"]],
        [(hw, BF16), (hw, BF16), (hw, BF16), (MLA_Q_RANK, BF16), (MLA_KV_RANK, BF16), (LANES, F32)], [], tm)
    (qp,) = mm_nn("q_up", cqn, w["wuq"], tm)
    (kn,) = mm_nn("k_up", ckvn, w["wuk"], tm)
    (vv,) = mm_nn("v_up", ckvn, w["wuv"], tm, out_dtype=BF16)

    def build_qk(r0, qp, kn, krr, cm, ta, tb):
        qc = jnp.concatenate(
            [part for h in range(HEADS)
             for part in (_head(qp, h, MLA_QK_PAD), _rope_m(_head(qp, h, MLA_QK_PAD, HEAD_DIM), cm, ta, tb))], axis=1)
        kc = jnp.concatenate([part for h in range(HEADS) for part in (_head(kn, h), krr)], axis=1)
        return [qc, kc], []

    (qc, kc), _ = _rowwise("build_qk", build_qk, [qp, kn, krr, cos_m, sa, sb], [], [(qw, BF16), (qw, BF16)], [], tm)
    o_m, lse, o_r, ret_states = mix_fwd("mix_fwd", qc, kc, vv, rq, rk, rv, blk, MLA_SCALE, host=plan.host("mix_fwd"))
    plan.done("mix_fwd", w)

    def gate_mix(r0, rg, o_r, o_m, gn):
        y, _ = _group_norm(o_r)
        return [jnp.concatenate([_silu(rg) * (y * gn), o_m], axis=1)], []

    (mixcat,), _ = _rowwise("gate_mix", gate_mix, [(proj_r, hw, 3), o_r, o_m], [nw["ret_group_norm"]],
                            [(2 * hw, BF16)], [], tm)
    (mix,) = mm_nn("mix_out", mixcat, w["w_out"], tm, n_block=dt)

    def post_mix(r0, h, m, post, pre):
        h2 = h + _rms(m, post)
        return [h2, _rms(h2, pre)], []

    (h2, n3), _ = _rowwise("ffn2_norm", post_mix, [h1, mix], [nw["mix_post_norm"], nw["ffn2_pre_norm"]],
                           [(d, F32), (d, BF16)], [], tm)
    g2, u2, a2, f2 = _ffn_fwd("ffn2", n3, w, "2", tm, tmk, plan)

    def loss_head(r0, h, f, t, post):
        h3 = h + 0.5 * _rms(f, post)
        row = r0 + lax.broadcasted_iota(jnp.int32, h3.shape, 0)
        err = jnp.where(row >= N_META, jnp.where(row < t_real, h3 - t, 0.0), 0.0)
        dh3 = err / d
        df, dpost = _rms_bwd(f, post, 0.5 * dh3)
        return [dh3, df], [_colsum(err * err), _colsum(dpost)]

    (dh3, df2), (loss_vec, d_post2) = _rowwise("loss_head", loss_head, [h2, f2, tgt_p], [nw["ffn2_post_norm"]],
                                               [(d, F32), (d, BF16)], [d, d], tm)
    loss = 0.5 * jnp.sum(loss_vec) / d
    dn3 = _ffn_bwd("ffn2", df2, n3, g2, u2, a2, w["wg2"], w["wu2"], w["wd2"], tmt, tmk, tk, dt, plan)

    def back_mix_norm(r0, h, m, dh3, dn, pre, post):
        dx, dpre = _rms_bwd(h, pre, dn)
        dh2 = dh3 + dx
        dm, dpost = _rms_bwd(m, post, dh2)
        return [dh2, dm], [_colsum(dpre), _colsum(dpost)]

    (dh2, dmix), (d_pre2, d_mix_post) = _rowwise(
        "back_mix_norm", back_mix_norm, [h2, mix, dh3, dn3], [nw["ffn2_pre_norm"], nw["mix_post_norm"]],
        [(d, F32), (d, BF16)], [d, d], tm)
    (dmixcat,) = mm_nt("mix_dx", [dmix], [w["w_out"]], tmt, _tile(2 * hw, 512), outs=(BF16,))
    plan.grads(dict(w_out=mm_tn("mix_dw", mixcat, dmix, 2 * hw // N_CHIPS, dt, tk, GRAD_DT, shard_rows=True)))

    def back_gate(r0, dmc, rg, o_r, gn):
        d_ret, d_om = dmc[:, :hw], dmc[:, hw:]
        yh, rs = _group_norm(o_r)
        d_rg = d_ret * (yh * gn) * _dsilu(rg)
        dy = d_ret * _silu(rg)
        gyh = dy * gn
        d_or = _heads(lambda h: rs[h] * (_head(gyh, h) - _head(yh, h) * jnp.mean(_head(gyh, h) * _head(yh, h),
                                                                                    axis=-1, keepdims=True)))
        return [d_or, d_rg, d_om], [_colsum(dy * yh)]

    (d_or, d_rg, d_om), (d_gn,) = _rowwise("back_gate", back_gate, [dmixcat, (proj_r, hw, 3), o_r],
                                           [nw["ret_group_norm"]], [(hw, BF16), (hw, F32), (hw, BF16)], [hw], tm)
    dqc, dkc, dvv = attn_bwd("mla_bwd", qc, kc, vv, d_om, o_m, lse, blk, MLA_SCALE, host=plan.host("mla_bwd"))
    plan.done("mla_bwd", None)
    drq, drk, drv = ret_bwd("ret_bwd", rq, rk, rv, d_or, ret_states, blk)

    def back_qk(r0, dqc, dkc, dvv, cm, ta, tb):
        dkc = dkc.astype(F32)
        dqp = jnp.concatenate(
            [part for h in range(HEADS)
             for part in (_head(dqc, h, MLA_QK_PAD), _rope_m_t(_head(dqc, h, MLA_QK_PAD, HEAD_DIM), cm, ta, tb))],
            axis=1)
        dkn = _heads(lambda h: _head(dkc, h, MLA_QK_PAD))
        dkr = _head(dkc, 0, MLA_QK_PAD, HEAD_DIM)
        for h in range(1, HEADS):
            dkr = dkr + _head(dkc, h, MLA_QK_PAD, HEAD_DIM)
        return [dqp, dkn, _rope_m_t(dkr, cm, ta, tb), dvv], []

    (dqp, dkn, dkr, dvb), _ = _rowwise("back_qk", back_qk, [dqc, dkc, dvv, cos_m, sa, sb], [],
                                       [(qw, BF16), (hw, BF16), (LANES, F32), (hw, BF16)], [], tm)
    (dcqn,) = mm_nt("q_dx", [dqp], [w["wuq"]], tm, MLA_Q_RANK)
    dwuq = mm_tn("q_dw", cqn, dqp, MLA_Q_RANK, _tile(qw, 1024), tk, GRAD_DT)
    (dckvn,) = mm_nt("kv_dx", [dkn, dvb], [w["wuk"], w["wuv"]], tm, MLA_KV_RANK)
    dwuk = mm_tn("k_dw", ckvn, dkn, MLA_KV_RANK, hw, tk, GRAD_DT)
    dwuv = mm_tn("v_dw", ckvn, dvb, MLA_KV_RANK, hw, tk, GRAD_DT)

    def back_proj(r0, drq, drk, drv, d_rg, pc, dcqn, dckvn, dkr, cr, sr, qn, kvn):
        d_q = _heads(lambda h: _rope_r_t(_head(drq, h), cr, sr))
        d_k = _heads(lambda h: _rope_r_t(_head(drk, h), cr, sr) * RET_SCALE)
        dcq, a_q = _rms_bwd(pc[:, :MLA_Q_RANK], qn, dcqn)
        dckv, a_kv = _rms_bwd(pc[:, MLA_Q_RANK:MLA_Q_RANK + MLA_KV_RANK], kvn, dckvn)
        return ([jnp.concatenate([d_q, d_k, drv, d_rg], axis=1), jnp.concatenate([dcq, dckv, dkr], axis=1)],
                [_colsum(a_q), _colsum(a_kv)])

    (dproj_r, dproj_c), (d_qn, d_kvn) = _rowwise(
        "back_proj", back_proj, [drq, drk, drv, d_rg, proj_c, dcqn, dckvn, dkr, cos_r, sin_r],
        [nw["mla_q_norm"], nw["mla_kv_norm"]], [(IN_RET, BF16), (IN_MLA_PAD, BF16)],
        [MLA_Q_RANK, MLA_KV_RANK], tm)
    (dun,) = mm_nt("proj_dx", [dproj_r, dproj_c], [w["w_r"], w["w_c"]], tmt, _tile(d, 512), outs=(BF16,))
    dw_r = mm_tn("proj_dw_r", un, dproj_r, dt, _tile(IN_RET, 1024), tk, GRAD_DT)
    dw_c = mm_tn("proj_dw_c", un, dproj_c, dt, IN_MLA_PAD, tk, GRAD_DT)
    plan.grads(dict(w_r=dw_r, w_c=dw_c, wuq=dwuq, wuk=dwuk, wuv=dwuv))

    def back_ffn1_norm(r0, h, f, dh2, dn, pre, post):
        dx, dpre = _rms_bwd(h, pre, dn)
        dh1 = dh2 + dx
        df, dpost = _rms_bwd(f, post, 0.5 * dh1)
        return [dh1, df], [_colsum(dpre), _colsum(dpost)]

    (dh1, df1), (d_mix_pre, d_post1) = _rowwise(
        "back_ffn1_norm", back_ffn1_norm, [h1, f1, dh2, dun], [nw["mix_pre_norm"], nw["ffn1_post_norm"]],
        [(d, F32), (d, BF16)], [d, d], tm)
    dn1 = _ffn_bwd("ffn1", df1, n1, g1, u1, a1, w["wg1"], w["wu1"], w["wd1"], tmt, tmk, tk, dt, plan)

    def back_input(r0, h, dh1, dn, pre):
        dx, dpre = _rms_bwd(h, pre, dn)
        return [dh1 + dx], [_colsum(dpre)]

    (dh0,), (d_pre1,) = _rowwise("back_input", back_input, [h0, dh1, dn1], [nw["ffn1_pre_norm"]], [(d, F32)], [d], tm)

    small = dict(ffn1_pre_norm=d_pre1, ffn1_post_norm=d_post1, mix_pre_norm=d_mix_pre, ret_group_norm=d_gn,
                 mla_q_norm=d_qn, mla_kv_norm=d_kvn, mix_post_norm=d_mix_post, ffn2_pre_norm=d_pre2,
                 ffn2_post_norm=d_post2)
    return loss, dh0[N_META:t_real], small, dh0[:N_META]


WEIGHTS = ("meta_tokens", "ffn1_pre_norm", "ffn1_w_gate", "ffn1_w_up", "ffn1_w_down", "ffn1_post_norm",
           "mix_pre_norm", "w_in", "ret_group_norm", "mla_q_norm", "mla_w_uq", "mla_kv_norm", "mla_w_uk",
           "mla_w_uv", "w_out", "mix_post_norm", "ffn2_pre_norm", "ffn2_w_gate", "ffn2_w_up", "ffn2_w_down",
           "ffn2_post_norm")
BIG = ("ffn1_w_gate", "ffn1_w_up", "ffn1_w_down", "w_in", "mla_w_uq", "mla_w_uk", "mla_w_uv", "w_out",
       "ffn2_w_gate", "ffn2_w_up", "ffn2_w_down")
NORMS = ("ffn1_pre_norm", "ffn1_post_norm", "mix_pre_norm", "ret_group_norm", "mla_q_norm", "mla_kv_norm",
         "mix_post_norm", "ffn2_pre_norm", "ffn2_post_norm")


def _unshard_cols(g):
    return g.transpose(1, 0, 2).reshape(g.shape[1], -1)


def _shard_cols(a):
    return a.reshape(a.shape[0], N_CHIPS, -1).transpose(1, 0, 2)


def _pack_rows(rows, width):
    rows = [jnp.pad(r, ((0, 0), (0, width - r.shape[1]))) for r in rows]
    n = sum(r.shape[0] for r in rows)
    return jnp.pad(jnp.concatenate(rows, axis=0), ((0, -n % 8), (0, 0)))


def _weight_views(full):
    w = {}
    for n, g in full.items():
        if n == "w_in":
            w_in = _unshard_cols(g)
            w["w_r"] = w_in[:, :IN_RET]
            w["w_c"] = jnp.pad(w_in[:, IN_RET:], ((0, 0), (0, IN_MLA_PAD - IN_MLA)))
        elif n == "mla_w_uq":
            q = _unshard_cols(g).reshape(MLA_Q_RANK, HEADS, HEAD_DIM + MLA_ROPE)
            q = jnp.pad(q, ((0, 0), (0, 0), (0, MLA_QK_PAD - HEAD_DIM - MLA_ROPE)))
            w["wuq"] = q.reshape(MLA_Q_RANK, HEADS * MLA_QK_PAD)
        elif n in ("mla_w_uk", "mla_w_uv"):
            w["wu" + n[-1]] = _unshard_cols(g)
        elif n == "w_out":
            w["w_out"] = g.reshape(-1, g.shape[2])
        else:
            w["w" + n[7] + n[3]] = g
    return w


def _contributions(g):
    ffn = {"g": "gate", "u": "up", "d": "down"}
    c = {f"ffn{n[2]}_w_{ffn[n[1]]}": a for n, a in g.items() if len(n) == 3 and n[2] in "12"}
    if "w_out" in g:
        c["w_out"] = g["w_out"]
    if "w_r" in g:
        dwuq = g["wuq"].reshape(MLA_Q_RANK, HEADS, MLA_QK_PAD)[:, :, :HEAD_DIM + MLA_ROPE]
        c.update(w_in=_shard_cols(jnp.concatenate([g["w_r"], g["w_c"][:, :IN_MLA]], axis=1)),
                 mla_w_uq=_shard_cols(dwuq.reshape(MLA_Q_RANK, -1)), mla_w_uk=_shard_cols(g["wuk"]),
                 mla_w_uv=_shard_cols(g["wuv"]))
    return c


class _Schedule(_Alone):
    FIRST = ("ffn1_w_gate",)
    CARRIED = {"ffn1_gate": ("ffn1_w_up",), "ffn1_up": ("ffn1_w_down",),
               "ffn1_down": ("w_in", "mla_w_uq", "mla_w_uk", "mla_w_uv"),
               "proj_r": ("w_out",), "mix_fwd": ("ffn2_w_gate", "ffn2_w_up"), "ffn2_up": ("ffn2_w_down",)}
    GRAD_HOST = dict(ffn2_w_gate="mla_bwd", ffn2_w_up="mla_bwd", ffn2_w_down="ffn2_dn",
                     w_out="mla_bwd", w_in="ffn1_da", mla_w_uq="ffn1_da", mla_w_uk="ffn1_da", mla_w_uv="ffn1_da",
                     ffn1_w_gate="ffn1_dwu", ffn1_w_up="ffn1_dwd", ffn1_w_down="ffn1_dn")

    def __init__(self, shards):
        self.shards = shards
        self.gathers = {k: (gather_halves([shards[n] for n in names]), names) for k, names in self.CARRIED.items()}
        self.waiting = {}
        self.exchanges = {}
        self.grad = {}

    def host(self, kernel_name):
        if kernel_name in self.gathers:
            return self.gathers[kernel_name][0]
        if kernel_name in self.waiting:
            names, sums = zip(*self.waiting.pop(kernel_name))
            self.exchanges[kernel_name] = (chip_exchange(list(sums)), names, sums)
            return self.exchanges[kernel_name][0]
        return None

    def done(self, kernel_name, w):
        if kernel_name in self.gathers:
            comm, names = self.gathers[kernel_name]
            w.update(_weight_views({n: sibling_fill("fill_" + n, b, self.shards[n])
                                    for n, b in zip(names, comm.result)}))
        elif kernel_name in self.exchanges:
            comm, names, sums = self.exchanges[kernel_name]
            for n, q, mine in zip(names, comm.result, sums):
                self.grad[n] = reduce_join("reduce_join_" + n, q, mine)

    def grads(self, g):
        for n, a in _contributions(g).items():
            self.waiting.setdefault(self.GRAD_HOST[n], []).append((n, pair_add("pair_add_" + n, a)))


def _step(p, m, v, x, loss_target):
    d = x.shape[2]
    names = list(BIG)
    shards = {n: p[n][0].astype(BF16) for n in names}
    first = _Schedule.FIRST
    gathered = gather_halves([shards[n] for n in first] + [p["meta_tokens"]]).run("gather_first")
    filled = [sibling_fill("fill_" + n, b, shards.get(n, p["meta_tokens"]))
              for n, b in zip(first + ("meta_tokens",), gathered)]
    w = _weight_views(dict(zip(first, filled[:-1])))
    meta = _unshard_cols(filled[-1])
    nw = {n: p[n] for n in NORMS}
    plan = _Schedule(shards)
    loss, grad_x, small, d_meta = _local_step(x[0], loss_target[0], meta, w, nw, plan)
    grads = dict(plan.grad)

    width = max(d, HEADS * HEAD_DIM)
    packed = _pack_rows([small[n] for n in NORMS] + [d_meta], width)
    total = sum_slots("small_sum", all_gather_devices(packed), F32)
    for i, n in enumerate(NORMS):
        grads[n] = total[i:i + 1, :p[n].shape[1]]
    cols = p["meta_tokens"].shape[1]
    chip = 2 * lax.axis_index("x") + lax.axis_index("y")
    grads["meta_tokens"] = lax.dynamic_slice(total[len(NORMS):len(NORMS) + N_META, :d], (0, chip * cols), (N_META, cols))

    delta, new_m, new_v = {}, {}, {}
    for n in names + ["meta_tokens"]:
        shape = p[n].shape
        flat = lambda a: a.reshape(-1, shape[-1])
        out = adamw("adamw_" + n, flat(p[n]), flat(grads[n]), flat(m[n]), flat(v[n]))
        delta[n], new_m[n], new_v[n], grads[n] = (o.reshape(shape) for o in out)
    pk = lambda src: _pack_rows([src[n] for n in NORMS], width)
    out = adamw("adamw_norms", pk(p), pk(grads), pk(m), pk(v))
    for i, n in enumerate(NORMS):
        delta[n], new_m[n], new_v[n] = (o[i:i + 1, :p[n].shape[1]] for o in out[:3])

    loss = lax.psum(loss, ("x", "y", "c"))
    return loss, grad_x[None], grads, delta, new_m, new_v


def kernel(x, meta_tokens, ffn1_pre_norm, ffn1_w_gate, ffn1_w_up, ffn1_w_down, ffn1_post_norm, mix_pre_norm, w_in, ret_group_norm, mla_q_norm, mla_w_uq, mla_kv_norm, mla_w_uk, mla_w_uv, w_out, mix_post_norm, ffn2_pre_norm, ffn2_w_gate, ffn2_w_up, ffn2_w_down, ffn2_post_norm, loss_target, m_meta_tokens, m_ffn1_pre_norm, m_ffn1_w_gate, m_ffn1_w_up, m_ffn1_w_down, m_ffn1_post_norm, m_mix_pre_norm, m_w_in, m_ret_group_norm, m_mla_q_norm, m_mla_w_uq, m_mla_kv_norm, m_mla_w_uk, m_mla_w_uv, m_w_out, m_mix_post_norm, m_ffn2_pre_norm, m_ffn2_w_gate, m_ffn2_w_up, m_ffn2_w_down, m_ffn2_post_norm, v_meta_tokens, v_ffn1_pre_norm, v_ffn1_w_gate, v_ffn1_w_up, v_ffn1_w_down, v_ffn1_post_norm, v_mix_pre_norm, v_w_in, v_ret_group_norm, v_mla_q_norm, v_mla_w_uq, v_mla_kv_norm, v_mla_w_uk, v_mla_w_uv, v_w_out, v_mix_post_norm, v_ffn2_pre_norm, v_ffn2_w_gate, v_ffn2_w_up, v_ffn2_w_down, v_ffn2_post_norm):
    args = locals()
    p = {n: args[n] for n in WEIGHTS}
    m = {n: args["m_" + n] for n in WEIGHTS}
    v = {n: args["v_" + n] for n in WEIGHTS}
    loss, grad_x, grads, delta, new_m, new_v = _step(p, m, v, x, loss_target)
    return (loss, grad_x, *[grads[n] for n in WEIGHTS], *[delta[n] for n in WEIGHTS],
            *[new_m[n] for n in WEIGHTS], *[new_v[n] for n in WEIGHTS])
```

```python
import functools
import math

import jax
import jax.numpy as jnp
import numpy as np
from jax import lax
from jax.experimental import pallas as pl
from jax.experimental.pallas import tpu as pltpu

F32 = jnp.float32
BF16 = jnp.bfloat16

EPS = 1e-6
N_META = 16
HEADS = 8
HEAD_DIM = 128
MLA_ROPE = 64
MLA_QK_PAD = 256
MLA_Q_RANK = 512
MLA_KV_RANK = 256
ROPE_THETA = 10000.0
N_CHIPS = 4
LANES = 128
VMEM_LIMIT = 60 * 2 ** 20

ADAM_LR = 0.001
ADAM_B1 = 0.9
ADAM_B2 = 0.999
ADAM_EPS = 1e-08
ADAM_WD = 0.01
ADAM_STEP = 10

NN = (((1,), (0,)), ((), ()))
NT = (((1,), (1,)), ((), ()))
TN = (((0,), (0,)), ((), ()))
MESH = pl.DeviceIdType.MESH


def _tile(n, pref, align=LANES):
    if n <= pref:
        return n
    best = 0
    for t in range(align, pref + 1, align):
        if n % t == 0:
            best = t
    assert best, (n, pref)
    return best


def _params(*sem, collective_id=None):
    return pltpu.CompilerParams(dimension_semantics=sem, vmem_limit_bytes=VMEM_LIMIT, collective_id=collective_id)


SIBLING_BARRIER = 0


def _sds(shape, dtype):
    return jax.ShapeDtypeStruct(tuple(shape), dtype)


def _rowwise(name, fn, rows, consts, outs, accs, tr):
    rows = [r if isinstance(r, tuple) else (r, r.shape[1], 0) for r in rows]
    t = rows[0][0].shape[0]
    assert t % tr == 0
    n_r, n_c, n_o = len(rows), len(consts), len(outs)

    def body(*refs):
        i = pl.program_id(0)
        r = [x[...] for x in refs[:n_r]]
        c = [x[...] for x in refs[n_r:n_r + n_c]]
        o_refs = refs[n_r + n_c:n_r + n_c + n_o]
        a_refs = refs[n_r + n_c + n_o:]
        o_vals, a_vals = fn(i * tr, *r, *c)
        for ref, v in zip(o_refs, o_vals):
            ref[...] = v.astype(ref.dtype)
        if a_refs:
            @pl.when(i == 0)
            def _():
                for ref, v in zip(a_refs, a_vals):
                    ref[...] = v

            @pl.when(i > 0)
            def _():
                for ref, v in zip(a_refs, a_vals):
                    ref[...] += v

    in_specs = [pl.BlockSpec((tr, w), functools.partial(lambda cb, i: (i, cb), cb)) for _, w, cb in rows]
    in_specs += [pl.BlockSpec(a.shape, lambda i: (0, 0)) for a in consts]
    out_specs = [pl.BlockSpec((tr, w), lambda i: (i, 0)) for w, _ in outs]
    out_specs += [pl.BlockSpec((1, w), lambda i: (0, 0)) for w in accs]
    out_shape = [_sds((t, w), dt) for w, dt in outs] + [_sds((1, w), F32) for w in accs]
    res = pl.pallas_call(
        body, name=name, grid=(t // tr,), in_specs=in_specs, out_specs=out_specs, out_shape=out_shape,
        compiler_params=_params("arbitrary"),
    )(*[a for a, _, _ in rows], *consts)
    return res[:n_o], res[n_o:]


def _rms(x, w):
    r = lax.rsqrt(jnp.mean(x * x, axis=-1, keepdims=True) + EPS)
    return x * r * w


def _rms_bwd(x, w, dy):
    r = lax.rsqrt(jnp.mean(x * x, axis=-1, keepdims=True) + EPS)
    xh = x * r
    gy = dy * w
    dx = r * (gy - xh * jnp.mean(gy * xh, axis=-1, keepdims=True))
    return dx, dy * xh


def _colsum(v):
    return jnp.sum(v, axis=0, keepdims=True)


def _silu(x):
    return x * jax.nn.sigmoid(x)


def _dsilu(x):
    s = jax.nn.sigmoid(x)
    return s * (1.0 + x * (1.0 - s))


def _rope_r(x, cos, sin):
    return x * cos + pltpu.roll(x, 64, 1) * sin


def _rope_r_t(dy, cos, sin):
    return dy * cos + pltpu.roll(dy * sin, 64, 1)


def _rope_m(x, cos, sa, sb):
    return x * cos + pltpu.roll(x, 32, 1) * sa + pltpu.roll(x, 96, 1) * sb


def _rope_m_t(dy, cos, sa, sb):
    return dy * cos + pltpu.roll(dy * sa, 96, 1) + pltpu.roll(dy * sb, 32, 1)


def _rope_tables(t):
    pos = np.arange(t, dtype=np.float32)

    def cs(dim):
        inv = np.float32(ROPE_THETA) ** (-np.arange(0, dim, 2, dtype=np.float32) / np.float32(dim))
        ang = pos[:, None] * inv[None, :]
        return np.cos(ang), np.sin(ang)

    c, s = cs(HEAD_DIM)
    cos_r = np.concatenate([c, c], axis=1)
    sin_r = np.concatenate([-s, s], axis=1)
    c, s = cs(MLA_ROPE)
    z32, z64 = np.zeros_like(s), np.zeros((t, 64), np.float32)
    cos_m = np.concatenate([c, c, z64], axis=1)
    sa = np.concatenate([z32, s, z64], axis=1)
    sb = np.concatenate([-s, z32, z64], axis=1)
    return tuple(jnp.asarray(a, F32) for a in (cos_r, sin_r, cos_m, sa, sb))


def _call(body, name, grid, in_specs, out_specs, out_shape, scratch, operands, host=None):
    sem = ("arbitrary",) * len(grid)
    if host is None:
        return pl.pallas_call(body, name=name, grid=grid, in_specs=in_specs, out_specs=out_specs, out_shape=out_shape,
                              scratch_shapes=scratch, compiler_params=_params(*sem))(*operands)
    n_in, n_out, n_s = len(in_specs), len(out_shape), len(scratch)
    h_in, h_out = len(host.ins), len(host.out_shape)

    def hosted(*refs):
        a = n_in
        b = a + h_in
        c = b + n_out
        d = c + h_out
        e = d + n_s
        ids = [pl.program_id(i) for i in range(len(grid))]
        first = functools.reduce(jnp.logical_and, [i == 0 for i in ids])
        last = functools.reduce(jnp.logical_and, [i == g - 1 for i, g in zip(ids, grid)])
        comm = (refs[a:b], refs[c:d], refs[e:])

        @pl.when(first)
        def _():
            host.start(*comm)

        body(*refs[:a], *refs[b:c], *refs[d:e])

        @pl.when(last)
        def _():
            host.finish(*comm)

    hbm = pl.BlockSpec(memory_space=pl.ANY)
    res = pl.pallas_call(
        hosted, name=name, grid=grid, in_specs=list(in_specs) + [hbm] * h_in, out_specs=list(out_specs) + [hbm] * h_out,
        out_shape=list(out_shape) + list(host.out_shape), scratch_shapes=list(scratch) + host.scratch(),
        compiler_params=_params(*sem, collective_id=host.collective_id))(*operands, *host.ins)
    host.result = res[n_out:]
    return res[:n_out]


def _mm(name, grid, operands, in_specs, dns, out_specs, out_shape, epilogue=None, extras=(), extra_specs=(),
        acc_shape=None, host=None):
    n_p, n_e = len(dns), len(extras)
    nk = grid[2]
    n_o = len(out_shape)
    in_place = nk > 1 and epilogue is None and n_o == 1 and out_shape[0].dtype == F32

    def body(*refs):
        ab = refs[:2 * n_p]
        ex = refs[2 * n_p:2 * n_p + n_e]
        outs = refs[2 * n_p + n_e:2 * n_p + n_e + n_o]

        part = None
        for p in range(n_p):
            d = lax.dot_general(ab[2 * p][...], ab[2 * p + 1][...], dns[p], preferred_element_type=F32)
            part = d if part is None else part + d

        def finish(acc):
            vals = (acc,) if epilogue is None else epilogue(acc, *[e[...] for e in ex])
            for o, v in zip(outs, vals):
                o[...] = v.astype(o.dtype)

        if nk == 1:
            finish(part)
        else:
            acc_ref = outs[0] if in_place else refs[2 * n_p + n_e + n_o]
            k = pl.program_id(2)

            @pl.when(k == 0)
            def _():
                acc_ref[...] = part

            @pl.when(k > 0)
            def _():
                acc_ref[...] += part

            if not in_place:
                @pl.when(k == nk - 1)
                def _():
                    finish(acc_ref[...])

    scratch = [] if nk == 1 or in_place else [pltpu.VMEM(acc_shape, F32)]
    return _call(body, name, grid, list(in_specs) + list(extra_specs), out_specs, out_shape, scratch,
                 [*operands, *extras], host)


def mm_nn(name, x, w, tm, out_dtype=F32, epilogue=None, outs=None, w2=None, n_block=None, extras=(), host=None):
    t, kdim = x.shape
    if w.ndim == 3:
        s, _, ns = w.shape
        n, tn, nb = s * ns, ns, s
        wspec = pl.BlockSpec((None, kdim, ns), lambda j, i, k: (j, 0, 0))
    else:
        n = w.shape[1]
        tn = n_block or n
        nb = n // tn
        wspec = pl.BlockSpec((kdim, tn), lambda j, i, k: (0, j))
    xspec = pl.BlockSpec((tm, kdim), lambda j, i, k: (i, 0))
    ospec = pl.BlockSpec((tm, tn), lambda j, i, k: (i, j))
    outs = outs or [out_dtype]
    grid = (nb, t // tm, 1)
    if w2 is None:
        return _mm(name, grid, [x, w], [xspec, wspec], [NN], [ospec] * len(outs), [_sds((t, n), d) for d in outs],
                   epilogue=epilogue, extras=extras, extra_specs=[ospec] * len(extras), host=host)

    def body(x_ref, w_ref, w2_ref, *o_refs):
        xv = x_ref[...]
        a = jnp.dot(xv, w_ref[...], preferred_element_type=F32)
        b = jnp.dot(xv, w2_ref[...], preferred_element_type=F32)
        for o, v in zip(o_refs, epilogue(a, b)):
            o[...] = v.astype(o.dtype)

    return _call(body, name, grid[:2],
                 [pl.BlockSpec((tm, kdim), lambda j, i: (i, 0)),
                  pl.BlockSpec((None, kdim, tn), lambda j, i: (j, 0, 0)),
                  pl.BlockSpec((None, kdim, tn), lambda j, i: (j, 0, 0))],
                 [pl.BlockSpec((tm, tn), lambda j, i: (i, j))] * len(outs), [_sds((t, n), d) for d in outs], [],
                 [x, w, w2], host)


def mm_nn_k(name, x, w, tm, tk, out_dtype=F32, host=None):
    t, kdim = x.shape
    n = w.shape[1]
    grid = (t // tm, 1, kdim // tk)
    return _mm(name, grid, [x, w],
               [pl.BlockSpec((tm, tk), lambda i, j, k: (i, k)), pl.BlockSpec((tk, n), lambda i, j, k: (k, 0))],
               [NN], [pl.BlockSpec((tm, n), lambda i, j, k: (i, 0))], [_sds((t, n), out_dtype)],
               acc_shape=(tm, n), host=host)[0]


def mm_nt(name, xs, ws, tm, tn, outs=(F32,), epilogue=None, extras=(), host=None):
    t = xs[0].shape[0]
    specs, ops = [], []
    for x, w in zip(xs, ws):
        kdim = x.shape[1]
        specs.append(pl.BlockSpec((tm, kdim), lambda j, i, k: (i, 0)))
        if w.ndim == 3:
            assert tn == w.shape[1]
            n = w.shape[0] * w.shape[1]
            specs.append(pl.BlockSpec((None, tn, kdim), lambda j, i, k: (j, 0, 0)))
        else:
            n = w.shape[0]
            specs.append(pl.BlockSpec((tn, kdim), lambda j, i, k: (j, 0)))
        ops += [x, w]
    ospec = pl.BlockSpec((tm, tn), lambda j, i, k: (i, j))
    return _mm(name, (n // tn, t // tm, 1), ops, specs, [NT] * len(xs), [ospec] * len(outs),
               [_sds((t, n), d) for d in outs], epilogue=epilogue, extras=extras,
               extra_specs=[ospec] * len(extras), host=host)


def mm_nt_k(name, xs, ws, tm, tn, out_dtype=F32, host=None):
    t = xs[0].shape[0]
    s, n, ns = ws[0].shape
    specs, ops = [], []
    for x, w in zip(xs, ws):
        specs.append(pl.BlockSpec((tm, ns), lambda i, j, k: (i, k)))
        specs.append(pl.BlockSpec((None, tn, ns), lambda i, j, k: (k, j, 0)))
        ops += [x, w]
    return _mm(name, (t // tm, n // tn, s), ops, specs, [NT] * len(xs),
               [pl.BlockSpec((tm, tn), lambda i, j, k: (i, j))], [_sds((t, n), out_dtype)], acc_shape=(tm, tn),
               host=host)[0]


def mm_tn(name, x, y, tm, tn, tk, out_dtype, shard_rows=False, shard_cols=False, host=None):
    t, m = x.shape
    n = y.shape[1]
    grid = (m // tm, n // tn, t // tk)
    if shard_cols:
        ospec = pl.BlockSpec((None, tm, tn), lambda i, j, k: (j, i, 0))
        oshape = _sds((n // tn, m, tn), out_dtype)
    elif shard_rows:
        ospec = pl.BlockSpec((None, tm, tn), lambda i, j, k: (i, 0, j))
        oshape = _sds((m // tm, tm, n), out_dtype)
    else:
        ospec = pl.BlockSpec((tm, tn), lambda i, j, k: (i, j))
        oshape = _sds((m, n), out_dtype)
    return _mm(name, grid, [x, y],
               [pl.BlockSpec((tk, tm), lambda i, j, k: (k, i)), pl.BlockSpec((tk, tn), lambda i, j, k: (k, j))],
               [TN], [ospec], [oshape], acc_shape=(tm, tn), host=host)[0]


def _decay_logs():
    return [math.log(1.0 - 2.0 ** (-5.0 - h)) for h in range(HEADS)]


def _log_decay(h):
    lg = jnp.float32(_decay_logs()[0])
    for i in range(1, HEADS):
        lg = jnp.where(h == i, jnp.float32(_decay_logs()[i]), lg)
    return lg


def _decayed_scores(q, k, lg):
    s = lax.dot_general(q, k, NT, preferred_element_type=F32)
    row = lax.broadcasted_iota(jnp.int32, s.shape, 0)
    col = lax.broadcasted_iota(jnp.int32, s.shape, 1)
    dec = jnp.where(col <= row, jnp.exp(jnp.maximum(row - col, 0).astype(F32) * lg), 0.0)
    return s * dec, dec


def _causal(s):
    row = lax.broadcasted_iota(jnp.int32, s.shape, 0)
    col = lax.broadcasted_iota(jnp.int32, s.shape, 1)
    return jnp.where(col <= row, s, -1e30)


def _attn_fwd_step(qi, blk, scale, q_ref, k_ref, v_ref, o_ref, lse_ref, m_ref, l_ref, acc_ref):
    qv = q_ref[...]
    m_ref[...] = jnp.full_like(m_ref, -1e30)
    l_ref[...] = jnp.zeros_like(l_ref)
    acc_ref[...] = jnp.zeros_like(acc_ref)

    def keys(start, n, diag_from):
        rows = pl.ds(pl.multiple_of(start, blk), n)
        s = lax.dot_general(qv, k_ref[rows, :], NT, preferred_element_type=F32) * scale
        if diag_from is not None:
            row = lax.broadcasted_iota(jnp.int32, s.shape, 0)
            col = lax.broadcasted_iota(jnp.int32, s.shape, 1)
            s = jnp.where(col - diag_from <= row, s, -1e30)
        m = m_ref[...]
        m_new = jnp.maximum(m, jnp.max(s, axis=-1, keepdims=True))
        p = jnp.exp(s - m_new)
        alpha = jnp.exp(m - m_new)
        m_ref[...] = m_new
        l_ref[...] = alpha * l_ref[...] + jnp.sum(p, axis=-1, keepdims=True)
        acc_ref[...] = alpha * acc_ref[...] + jnp.dot(p.astype(BF16), v_ref[rows, :], preferred_element_type=F32)

    @pl.loop(0, qi // 2)
    def _(j):
        keys(j * (2 * blk), 2 * blk, None)

    @pl.when(qi % 2 == 1)
    def _():
        keys((qi - 1) * blk, 2 * blk, blk)

    @pl.when(qi % 2 == 0)
    def _():
        keys(qi * blk, blk, 0)

    l = l_ref[...]
    o_ref[...] = acc_ref[...] / l
    lse_ref[...] = jnp.broadcast_to(m_ref[...] + jnp.log(l), (blk, HEAD_DIM))


def mix_fwd(name, q, k, v, rq, rk, rv, blk, scale, host=None):
    t = q.shape[0]
    dq = q.shape[1] // HEADS
    nq = t // blk

    def body(q_ref, k_ref, v_ref, rq_ref, rk_ref, rv_ref, o_ref, lse_ref, ro_ref, st_ref, m_ref, l_ref, acc_ref, state):
        h, i = pl.program_id(0), pl.program_id(1)

        @pl.when(h < HEADS)
        def _():
            _attn_fwd_step(i, blk, scale, q_ref, k_ref, v_ref, o_ref, lse_ref, m_ref, l_ref, acc_ref)

        @pl.when(h >= HEADS)
        def _():
            _ret_fwd_step(h - HEADS, i, blk, rq_ref, rk_ref, rv_ref, ro_ref, st_ref, state)

    mla = lambda w, whole=False: pl.BlockSpec(
        (t if whole else blk, w),
        lambda h, i: (0 if whole else jnp.where(h < HEADS, i, nq - 1), jnp.minimum(h, HEADS - 1)))
    ret = lambda rows: pl.BlockSpec(
        (rows, HEAD_DIM), lambda h, i: (jnp.where(h >= HEADS, i, 0), jnp.maximum(h - HEADS, 0)))
    wide = _sds((t, HEADS * HEAD_DIM), F32)
    return _call(body, name, (2 * HEADS, nq),
                 [mla(dq), mla(dq, True), mla(HEAD_DIM, True), ret(blk), ret(blk), ret(blk)],
                 [mla(HEAD_DIM), mla(HEAD_DIM), ret(blk), ret(HEAD_DIM)],
                 [wide, wide, wide, _sds((nq * HEAD_DIM, HEADS * HEAD_DIM), F32)],
                 [pltpu.VMEM((blk, 1), F32), pltpu.VMEM((blk, 1), F32), pltpu.VMEM((blk, HEAD_DIM), F32),
                  pltpu.VMEM((HEAD_DIM, HEAD_DIM), F32)],
                 [q, k, v, rq, rk, rv], host)


def attn_bwd(name, q, k, v, do, o, lse, blk, scale, host=None):
    t = q.shape[0]
    dq_w = q.shape[1] // HEADS
    nb = t // blk

    def body(q_ref, k_ref, v_ref, do_ref, o_ref, lse_ref, dq_ref, dk_out, dv_out, dk_ref, dv_ref):
        ki = pl.program_id(1)
        kv = k_ref[...]
        vv = v_ref[...]

        @pl.when(ki == 0)
        def _():
            dq_ref[...] = jnp.zeros_like(dq_ref)

        def queries(start, n, diag):
            rows = pl.ds(pl.multiple_of(start, blk), n)
            qv, dov = q_ref[rows, :], do_ref[rows, :]
            s = lax.dot_general(qv, kv, NT, preferred_element_type=F32) * scale
            if diag:
                s = _causal(s)
            p = jnp.exp(s - lse_ref[rows, :][:, :1])
            dp = lax.dot_general(dov, vv, NT, preferred_element_type=F32)
            delta = jnp.sum(dov.astype(F32) * o_ref[rows, :], axis=-1, keepdims=True)
            ds = p * (dp - delta) * scale
            pb, dsb = p.astype(BF16), ds.astype(BF16)
            dv_ref[...] += lax.dot_general(pb, dov, TN, preferred_element_type=F32)
            dk_ref[...] += lax.dot_general(dsb, qv, TN, preferred_element_type=F32)
            dq_ref[rows, :] += jnp.dot(dsb, kv, preferred_element_type=F32)

        dk_ref[...] = jnp.zeros_like(dk_ref)
        dv_ref[...] = jnp.zeros_like(dv_ref)
        queries(ki * blk, blk, True)
        later = nb - 1 - ki

        @pl.when(later % 2 == 1)
        def _():
            queries((ki + 1) * blk, blk, False)

        @pl.loop(0, later // 2)
        def _(j):
            queries((ki + 1 + later % 2 + 2 * j) * blk, 2 * blk, False)

        dk_out[...] = dk_ref[...].astype(dk_out.dtype)
        dv_out[...] = dv_ref[...].astype(dv_out.dtype)

    full = lambda w: pl.BlockSpec((t, w), lambda h, j: (0, h))
    blkd = lambda w: pl.BlockSpec((blk, w), lambda h, j: (j, h))
    return _call(body, name, (HEADS, nb),
                 [full(dq_w), blkd(dq_w), blkd(HEAD_DIM), full(HEAD_DIM), full(HEAD_DIM), full(HEAD_DIM)],
                 [full(dq_w), blkd(dq_w), blkd(HEAD_DIM)],
                 [_sds(q.shape, F32), _sds(k.shape, BF16), _sds(v.shape, BF16)],
                 [pltpu.VMEM((blk, dq_w), F32), pltpu.VMEM((blk, HEAD_DIM), F32)], [q, k, v, do, o, lse], host)


def _chunk_decays(lg, blk):
    row = lax.broadcasted_iota(jnp.int32, (blk, HEAD_DIM), 0).astype(F32)
    return jnp.exp(lg * (row + 1.0)), jnp.exp(lg * (blk - 1.0 - row)), jnp.exp(lg * blk * jnp.ones((1, HEAD_DIM), F32))


def _ret_fwd_step(h, i, blk, q_ref, k_ref, v_ref, o_ref, st_ref, state):
    lg = _log_decay(h)

    @pl.when(i == 0)
    def _():
        state[...] = jnp.zeros_like(state)

    qv, kv, vv = q_ref[...], k_ref[...], v_ref[...]
    before = state[...]
    st_ref[...] = before
    p, _ = _decayed_scores(qv, kv, lg)
    xi, zeta, g_blk = _chunk_decays(lg, blk)
    o_ref[...] = (jnp.dot(p.astype(BF16), vv, preferred_element_type=F32)
                  + jnp.dot(qv, before.astype(BF16), preferred_element_type=F32) * xi)
    kz = (kv.astype(F32) * zeta).astype(BF16)
    state[...] = before * g_blk + lax.dot_general(kz, vv, TN, preferred_element_type=F32)


def ret_bwd(name, q, k, v, do, states, blk, host=None):
    t = q.shape[0]
    nb = t // blk

    def body(q_ref, k_ref, v_ref, do_ref, st_ref, dq_ref, dk_ref, dv_ref, dstate):
        h, i = pl.program_id(0), pl.program_id(1)
        lg = _log_decay(h)

        @pl.when(i == 0)
        def _():
            dstate[...] = jnp.zeros_like(dstate)

        qv, kv, vv, dov = q_ref[...], k_ref[...], v_ref[...], do_ref[...]
        before = st_ref[...].astype(BF16)
        after_grad = dstate[...]
        p, dec = _decayed_scores(qv, kv, lg)
        ds = lax.dot_general(dov, vv, NT, preferred_element_type=F32) * dec
        pb, dsb = p.astype(BF16), ds.astype(BF16)
        xi, zeta, g_blk = _chunk_decays(lg, blk)
        dox = (dov.astype(F32) * xi).astype(BF16)
        kz = (kv.astype(F32) * zeta).astype(BF16)
        agb = after_grad.astype(BF16)
        dv = lax.dot_general(pb, dov, TN, preferred_element_type=F32) + jnp.dot(kz, agb, preferred_element_type=F32)
        dq = (jnp.dot(dsb, kv, preferred_element_type=F32)
              + lax.dot_general(dox, before, NT, preferred_element_type=F32))
        dk = (lax.dot_general(dsb, qv, TN, preferred_element_type=F32)
              + lax.dot_general(vv, agb, NT, preferred_element_type=F32) * zeta)
        for ref, val in ((dq_ref, dq), (dk_ref, dk), (dv_ref, dv)):
            ref[...] = val.astype(ref.dtype)
        dstate[...] = after_grad * g_blk + lax.dot_general(qv, dox, TN, preferred_element_type=F32)

    hspec = pl.BlockSpec((blk, HEAD_DIM), lambda h, i: (nb - 1 - i, h))
    return _call(body, name, (HEADS, nb),
                 [hspec] * 4 + [pl.BlockSpec((HEAD_DIM, HEAD_DIM), lambda h, i: (nb - 1 - i, h))],
                 [hspec] * 3, [_sds(q.shape, BF16)] * 3, [pltpu.VMEM((HEAD_DIM, HEAD_DIM), F32)],
                 [q, k, v, do, states], host)


CHIP_FLIPS = ((1, 0), (0, 1), (1, 1))
CHIP_BARRIER = 1
ALL_BARRIER = 2


def _position():
    return lax.axis_index("x"), lax.axis_index("y"), lax.axis_index("c")


def _chip_peers(pos):
    x, y, c = pos
    return [(x ^ fx, y ^ fy, c) for fx, fy in CHIP_FLIPS]


class _Comm:
    def __init__(self, ins, out_shape, plan, n_remote, n_local, collective_id):
        self.ins, self.out_shape, self.plan = list(ins), list(out_shape), plan
        self.n_remote, self.n_local, self.collective_id = n_remote, n_local, collective_id
        self.result = None

    def scratch(self):
        return [pltpu.SemaphoreType.DMA((self.n_remote,)), pltpu.SemaphoreType.DMA((self.n_remote,)),
                pltpu.SemaphoreType.DMA((max(self.n_local, 1),))]

    def _copies(self, in_refs, out_refs, sems):
        send_sems, recv_sems, local_sems = sems
        pos = _position()
        p = self.plan(pos, in_refs, out_refs)

        def remote(k, src, dst, dev):
            return pltpu.make_async_remote_copy(src_ref=src, dst_ref=dst, send_sem=send_sems.at[k],
                                                recv_sem=recv_sems.at[k], device_id=dev, device_id_type=MESH)

        local = [pltpu.make_async_copy(s, d, local_sems.at[i]) for i, (s, d) in enumerate(p["local"])]
        out = [remote(k, s, d, dev) for k, (s, d, dev) in enumerate(p["sends"])]
        arrivals = [functools.partial(remote, k, d, d, pos) for k, d in enumerate(p["recvs"])]
        return local, out, arrivals, p["peers"]

    def start(self, in_refs, out_refs, sems):
        local, out, _, peers = self._copies(in_refs, out_refs, sems)
        barrier = pltpu.get_barrier_semaphore()
        for peer in peers:
            pl.semaphore_signal(barrier, inc=1, device_id=peer, device_id_type=MESH)
        pl.semaphore_wait(barrier, len(peers))
        for cp in local + out:
            cp.start()

    def finish(self, in_refs, out_refs, sems):
        local, out, arrivals, _ = self._copies(in_refs, out_refs, sems)
        for make in arrivals:
            make().wait_recv()
        for cp in out:
            cp.wait_send()
        for cp in local:
            cp.wait()

    def run(self, name):
        n_in, n_out = len(self.ins), len(self.out_shape)

        def body(*refs):
            comm = (refs[:n_in], refs[n_in:n_in + n_out], refs[n_in + n_out:])
            self.start(*comm)
            self.finish(*comm)

        hbm = pl.BlockSpec(memory_space=pl.ANY)
        self.result = pl.pallas_call(
            body, name=name, in_specs=[hbm] * n_in, out_specs=[hbm] * n_out, out_shape=self.out_shape,
            scratch_shapes=self.scratch(),
            compiler_params=pltpu.CompilerParams(collective_id=self.collective_id))(*self.ins)
        return self.result


def _half_rows(c, rows):
    r2 = rows // 2
    return pl.ds(pl.multiple_of(c * r2, math.gcd(r2, LANES)), r2)


def _half(ref, c, rows, lead=()):
    return ref.at[(*lead, _half_rows(c, rows))]


def gather_halves(shards):
    def plan(pos, ins, outs):
        x, y, c = pos
        me = 2 * x + y
        p = dict(local=[], sends=[], recvs=[], peers=_chip_peers(pos))
        for a, (src, dst) in enumerate(zip(ins, outs)):
            rows = shards[a].shape[0]
            for px, py, _ in p["peers"]:
                p["sends"].append((_half(src, c, rows), _half(dst, c, rows, (me,)), (px, py, c)))
                p["recvs"].append(_half(dst, c, rows, (2 * px + py,)))
        return p

    return _Comm(shards, [_sds((N_CHIPS, *s.shape), s.dtype) for s in shards], plan,
                 n_remote=3 * len(shards), n_local=0, collective_id=CHIP_BARRIER)


def chip_exchange(parts):
    def plan(pos, ins, outs):
        x, y, c = pos
        me = 2 * x + y
        p = dict(local=[], sends=[], recvs=[], peers=_chip_peers(pos))
        for src, dst in zip(ins, outs):
            for px, py, _ in p["peers"]:
                peer = 2 * px + py
                p["sends"].append((src.at[peer], dst.at[me], (px, py, c)))
                p["recvs"].append(dst.at[peer])
        return p

    return _Comm(parts, [_sds(g.shape, g.dtype) for g in parts], plan, n_remote=3 * len(parts), n_local=0,
                 collective_id=CHIP_BARRIER)


def all_gather_devices(v):
    flips = [(fx, fy, fc) for fx in (0, 1) for fy in (0, 1) for fc in (0, 1)][1:]

    def plan(pos, ins, outs):
        x, y, c = pos
        me = 4 * x + 2 * y + c
        p = dict(local=[(ins[0], outs[0].at[me])], sends=[], recvs=[],
                 peers=[(x ^ fx, y ^ fy, c ^ fc) for fx, fy, fc in flips])
        for px, py, pc in p["peers"]:
            p["sends"].append((ins[0], outs[0].at[me], (px, py, pc)))
            p["recvs"].append(outs[0].at[4 * px + 2 * py + pc])
        return p

    return _Comm([v], [_sds((8, *v.shape), v.dtype)], plan, n_remote=7, n_local=1,
                 collective_id=ALL_BARRIER).run("small_all_gather")[0]


SWAP_CHUNK_BYTES = 3 * 2 ** 19


def _sibling_stream(t, n, value, consume, sbuf, rbuf, send_sems, recv_sems, credits):
    x, y, c = _position()
    sib = (x, y, 1 - c)

    def copy(slot):
        return pltpu.make_async_remote_copy(src_ref=sbuf.at[slot], dst_ref=rbuf.at[slot], send_sem=send_sems.at[slot],
                                            recv_sem=recv_sems.at[slot], device_id=sib, device_id_type=MESH)

    slot = t % 2

    @pl.when(t == 0)
    def _():
        barrier = pltpu.get_barrier_semaphore()
        pl.semaphore_signal(barrier, inc=1, device_id=sib, device_id_type=MESH)
        pl.semaphore_wait(barrier, 1)

    @pl.when(jnp.logical_and(t >= 2, t < n))
    def _():
        copy(slot).wait_send()
        pl.semaphore_wait(credits.at[slot], 1)

    @pl.when(t < n)
    def _():
        sbuf[slot] = value
        copy(slot).start()

    @pl.when(t >= 1)
    def _():
        prev = 1 - slot
        copy(prev).wait_recv()
        consume(sbuf[prev], rbuf[prev])

        @pl.when(t + 1 < n)
        def _():
            pl.semaphore_signal(credits.at[prev], inc=1, device_id=sib, device_id_type=MESH)

    @pl.when(t == n)
    def _():
        copy(1 - slot).wait_send()
        if n > 1:
            copy(slot).wait_send()


def _swap_scratch(rows, cols, dtype):
    return [pltpu.VMEM((2, rows, cols), dtype), pltpu.VMEM((2, rows, cols), dtype),
            pltpu.SemaphoreType.DMA((2,)), pltpu.SemaphoreType.DMA((2,)), pltpu.SemaphoreType.REGULAR((2,))]


def _chunk_rows(rows, cols, dtype, nbytes=SWAP_CHUNK_BYTES):
    return _tile(rows, max(16, nbytes // (cols * jnp.dtype(dtype).itemsize)), 16)


def pair_add(name, g):
    s, r, c_ = g.shape
    r2 = r // 2
    cr = _chunk_rows(r2, c_, g.dtype)
    nj = r2 // cr

    n = s * nj

    def body(core, mine_ref, theirs_ref, o_ref, *scratch):
        def consume(_, got):
            o_ref[...] = (mine_ref[...].astype(F32) + got.astype(F32)).astype(o_ref.dtype)

        _sibling_stream(pl.program_id(0), n, theirs_ref[...], consume, *scratch)

    sent = lambda t: jnp.minimum(t, n - 1)
    used = lambda t: jnp.maximum(t - 1, 0)
    grid_spec = pltpu.PrefetchScalarGridSpec(
        num_scalar_prefetch=1, grid=(n + 1,),
        in_specs=[pl.BlockSpec((None, cr, c_), lambda t, core: (used(t) // nj, core[0] * nj + used(t) % nj, 0)),
                  pl.BlockSpec((None, cr, c_), lambda t, core: (sent(t) // nj, (1 - core[0]) * nj + sent(t) % nj, 0))],
        out_specs=pl.BlockSpec((None, cr, c_), lambda t, core: (used(t) // nj, used(t) % nj, 0)),
        scratch_shapes=_swap_scratch(cr, c_, g.dtype))
    core = lax.axis_index("c").astype(jnp.int32).reshape(1)
    return pl.pallas_call(body, name=name, grid_spec=grid_spec, out_shape=_sds((s, r2, c_), g.dtype),
                          compiler_params=_params("arbitrary", collective_id=SIBLING_BARRIER))(core, g, g)


def _adamw_math(w, g, m, v):
    m = ADAM_B1 * m + (1.0 - ADAM_B1) * g
    v = ADAM_B2 * v + (1.0 - ADAM_B2) * (g * g)
    m_hat = m / (1.0 - ADAM_B1 ** ADAM_STEP)
    v_hat = v / (1.0 - ADAM_B2 ** ADAM_STEP)
    return -ADAM_LR * (m_hat / (jnp.sqrt(v_hat) + ADAM_EPS) + ADAM_WD * w), m, v


def reduce_join(name, p, mine):
    s, r2, c_ = p.shape
    cr = _chunk_rows(r2, c_, F32)
    nj = r2 // cr

    def body(where, p_ref, mine_ref, o_ref, acc_ref, *scratch):
        c, me = where[0], where[1]
        for chip in range(s):
            @pl.when(me == chip)
            def _():
                terms = [mine_ref[...] if i == chip else p_ref[i] for i in range(s)]
                acc = terms[0].astype(F32)
                for term in terms[1:]:
                    acc = acc + term.astype(F32)
                acc_ref[...] = acc

        def consume(own, got):
            o_ref[c] = own
            o_ref[1 - c] = got

        _sibling_stream(pl.program_id(0), nj, acc_ref[...], consume, *scratch)

    sent = lambda t: jnp.minimum(t, nj - 1)
    grid_spec = pltpu.PrefetchScalarGridSpec(
        num_scalar_prefetch=1, grid=(nj + 1,),
        in_specs=[pl.BlockSpec((s, cr, c_), lambda t, where: (0, sent(t), 0)),
                  pl.BlockSpec((None, cr, c_), lambda t, where: (where[1], sent(t), 0))],
        out_specs=pl.BlockSpec((2, cr, c_), lambda t, where: (0, jnp.maximum(t - 1, 0), 0)),
        scratch_shapes=[pltpu.VMEM((cr, c_), F32)] + _swap_scratch(cr, c_, F32))
    x, y, c = _position()
    where = jnp.stack([c, 2 * x + y]).astype(jnp.int32)
    out = pl.pallas_call(body, name=name, grid_spec=grid_spec, out_shape=_sds((2, r2, c_), F32),
                         compiler_params=_params("arbitrary", collective_id=SIBLING_BARRIER))(where, p, mine)
    return out.reshape(2 * r2, c_)


def sibling_fill(name, buf, shard):
    s, r, c_ = buf.shape
    r2 = r // 2
    cr = _chunk_rows(r2, c_, buf.dtype)
    nj = r2 // cr
    n_peers = len(CHIP_FLIPS)
    n = n_peers * nj
    n_own = r // cr

    def body(where, in_ref, own_ref, o_ref, *scratch):
        t = pl.program_id(0)

        @pl.when(t <= n)
        def _():
            def consume(_, got):
                o_ref[...] = got

            _sibling_stream(t, n, in_ref[...], consume, *scratch)

        @pl.when(t > n)
        def _():
            o_ref[...] = own_ref[...]

    sent = lambda t: jnp.minimum(t, n - 1)
    used = lambda t: jnp.clip(t - 1, 0, n - 1)
    own = lambda t: jnp.clip(t - (n + 1), 0, n_own - 1)
    grid_spec = pltpu.PrefetchScalarGridSpec(
        num_scalar_prefetch=1, grid=(n + 1 + n_own,),
        in_specs=[pl.BlockSpec((None, cr, c_),
                               lambda t, where: (where[sent(t) // nj], where[n_peers] * nj + sent(t) % nj, 0)),
                  pl.BlockSpec((cr, c_), lambda t, where: (own(t), 0))],
        out_specs=pl.BlockSpec(
            (None, cr, c_),
            lambda t, where: (jnp.where(t <= n, where[used(t) // nj], where[n_peers + 1]),
                              jnp.where(t <= n, (1 - where[n_peers]) * nj + used(t) % nj, own(t)), 0)),
        scratch_shapes=_swap_scratch(cr, c_, buf.dtype))
    x, y, c = _position()
    where = jnp.stack([2 * (x ^ fx) + (y ^ fy) for fx, fy in CHIP_FLIPS] + [c, 2 * x + y]).astype(jnp.int32)
    return pl.pallas_call(body, name=name, grid_spec=grid_spec, out_shape=_sds(buf.shape, buf.dtype),
                          input_output_aliases={1: 0},
                          compiler_params=_params("arbitrary", collective_id=SIBLING_BARRIER))(where, buf, shard)


def sum_slots(name, p, out_dtype):
    s, r, c = p.shape
    tr = _tile(r, 256, 16)

    def body(p_ref, o_ref):
        acc = p_ref[0].astype(F32)
        for i in range(1, s):
            acc = acc + p_ref[i].astype(F32)
        o_ref[...] = acc.astype(o_ref.dtype)

    return pl.pallas_call(
        body, name=name, grid=(r // tr,), in_specs=[pl.BlockSpec((s, tr, c), lambda i: (0, i, 0))],
        out_specs=pl.BlockSpec((tr, c), lambda i: (i, 0)), out_shape=_sds((r, c), out_dtype),
        compiler_params=_params("arbitrary"),
    )(p)


def adamw(name, w, g, m, v):
    r, c = w.shape
    outs, _ = _rowwise(name, lambda _, w, g, m, v: ([*_adamw_math(w, g, m, v), g], []), [w, g, m, v], [],
                       [(c, F32)] * 4, [], _tile(r, 256, 8))
    return outs


RET_SCALE = HEAD_DIM ** -0.5
MLA_SCALE = (HEAD_DIM + MLA_ROPE) ** -0.5
GRAD_DT = BF16
IN_RET = 4 * HEADS * HEAD_DIM
IN_MLA = MLA_Q_RANK + MLA_KV_RANK + MLA_ROPE
IN_MLA_PAD = IN_MLA + 64


def _heads(fn):
    return jnp.concatenate([fn(h) for h in range(HEADS)], axis=1)


def _head(a, h, stride=HEAD_DIM, off=0):
    return a[:, h * stride + off:h * stride + off + HEAD_DIM]


def _group_norm(o):
    rs = [lax.rsqrt(jnp.mean(_head(o, h) * _head(o, h), axis=-1, keepdims=True) + EPS) for h in range(HEADS)]
    return _heads(lambda h: _head(o, h) * rs[h]), rs


class _Alone:
    def host(self, kernel_name):
        return None

    def done(self, kernel_name, w):
        pass

    def grads(self, g):
        pass


def _ffn_fwd(tag, n, w, k, tm, tmk, plan):
    gate, up, down = tag + "_gate", tag + "_up", tag + "_down"
    if "wu" + k in w:
        g, u, a = mm_nn(up, n, w["wg" + k], tm, outs=[BF16] * 3, w2=w["wu" + k],
                        epilogue=lambda g, u: (g, u, _silu(g) * u), host=plan.host(up))
    else:
        (g,) = mm_nn(gate, n, w["wg" + k], tm, out_dtype=BF16, host=plan.host(gate))
        plan.done(gate, w)
        u, a = mm_nn(up, n, w["wu" + k], tm, outs=[BF16] * 2, extras=(g,),
                     epilogue=lambda u, g: (u, _silu(g.astype(F32)) * u), host=plan.host(up))
    plan.done(up, w)
    ff = a.shape[1]
    wd = w["wd" + k]
    f = mm_nn_k(down, a, wd.reshape(ff, wd.shape[2]), tmk, _tile(ff, 1408), host=plan.host(down))
    plan.done(down, w)
    return g, u, a, f


def _ffn_bwd(tag, df, n, g, u, a, wg, wu, wd, tm, tmk, tk, dt, plan):
    ns = wg.shape[2]

    def gate_grads(da, g, u):
        g, u = g.astype(F32), u.astype(F32)
        return da * u * _dsilu(g), da * _silu(g)

    def hosted(kernel_name, call):
        out = call(plan.host(kernel_name))
        plan.done(kernel_name, None)
        return out

    k = tag[-1]
    dg, du = hosted(tag + "_da", lambda h: mm_nt(tag + "_da", [df], [wd], tm, ns, outs=(BF16, BF16),
                                                 epilogue=gate_grads, extras=(g, u), host=h))
    dwg = hosted(tag + "_dwg", lambda h: mm_tn(tag + "_dwg", n, dg, dt, ns, tk, GRAD_DT, shard_cols=True, host=h))
    plan.grads({"wg" + k: dwg})
    dwu = hosted(tag + "_dwu", lambda h: mm_tn(tag + "_dwu", n, du, dt, ns, tk, GRAD_DT, shard_cols=True, host=h))
    plan.grads({"wu" + k: dwu})
    dwd = hosted(tag + "_dwd", lambda h: mm_tn(tag + "_dwd", a, df, ns, dt, tk, GRAD_DT, shard_rows=True, host=h))
    plan.grads({"wd" + k: dwd})
    dn = hosted(tag + "_dn", lambda h: mm_nt_k(tag + "_dn", [dg, du], [wg, wu], tmk, dt, out_dtype=BF16, host=h))
    return dn


def _local_step(x, tgt, meta, w, nw, plan):
    seq, d = x.shape
    t_real = N_META + seq
    tp = -(-t_real // LANES) * LANES
    zpad = jnp.zeros((tp - t_real, d), F32)
    h0 = jnp.concatenate([meta, x, zpad], axis=0)
    tgt_p = jnp.concatenate([jnp.zeros((N_META, d), F32), tgt, zpad], axis=0)
    cos_r, sin_r, cos_m, sa, sb = _rope_tables(tp)
    tm = _tile(tp, 512)
    tmt = _tile(tp, 768, 16)
    tmk = _tile(tp, 1408)
    tk = tp
    blk = tm
    dt = _tile(d, 1024)
    hw = HEADS * HEAD_DIM
    qw = HEADS * MLA_QK_PAD

    (n1,), _ = _rowwise("ffn1_norm", lambda r0, h, g: ([_rms(h, g)], []), [h0], [nw["ffn1_pre_norm"]],
                        [(d, BF16)], [], tm)
    g1, u1, a1, f1 = _ffn_fwd("ffn1", n1, w, "1", tm, tmk, plan)

    def post_ffn1(r0, h, f, post, pre):
        h1 = h + 0.5 * _rms(f, post)
        return [h1, _rms(h1, pre)], []

    (h1, un), _ = _rowwise("mix_norm", post_ffn1, [h0, f1], [nw["ffn1_post_norm"], nw["mix_pre_norm"]],
                           [(d, F32), (d, BF16)], [], tm)
    (proj_r,) = mm_nn("proj_r", un, w["w_r"], tm, n_block=_tile(IN_RET, 1024), host=plan.host("proj_r"))
    plan.done("proj_r", w)
    (proj_c,) = mm_nn("proj_c", un, w["w_c"], tm)

    def split_proj(r0, pr, pc, cr, sr, cm, ta, tb, qn, kvn):
        rq = _heads(lambda h: _rope_r(_head(pr, h), cr, sr))
        rk = _heads(lambda h: _rope_r(_head(pr, h, off=hw), cr, sr) * RET_SCALE)
        rv = pr[:, 2 * hw:3 * hw]
        cqn = _rms(pc[:, :MLA_Q_RANK], qn)
        ckvn = _rms(pc[:, MLA_Q_RANK:MLA_Q_RANK + MLA_KV_RANK], kvn)
        krr = _rope_m(pc[:, MLA_Q_RANK + MLA_KV_RANK:], cm, ta, tb)
        return [rq, rk, rv, cqn, ckvn, krr], []

    (rq, rk, rv, cqn, ckvn, krr), _ = _rowwise(
        "split_proj", split_proj, [proj_r, proj_c, cos_r, sin_r, cos_m, sa, sb],
        [nw["mla_q_norm"], nw["mla_kv_norm"]],
        [(hw, BF16), (hw, BF16), (hw, BF16), (MLA_Q_RANK, BF16), (MLA_KV_RANK, BF16), (LANES, F32)], [], tm)
    (qp,) = mm_nn("q_up", cqn, w["wuq"], tm)
    (kn,) = mm_nn("k_up", ckvn, w["wuk"], tm)
    (vv,) = mm_nn("v_up", ckvn, w["wuv"], tm, out_dtype=BF16)

    def build_qk(r0, qp, kn, krr, cm, ta, tb):
        qc = jnp.concatenate(
            [part for h in range(HEADS)
             for part in (_head(qp, h, MLA_QK_PAD), _rope_m(_head(qp, h, MLA_QK_PAD, HEAD_DIM), cm, ta, tb))], axis=1)
        kc = jnp.concatenate([part for h in range(HEADS) for part in (_head(kn, h), krr)], axis=1)
        return [qc, kc], []

    (qc, kc), _ = _rowwise("build_qk", build_qk, [qp, kn, krr, cos_m, sa, sb], [], [(qw, BF16), (qw, BF16)], [], tm)
    o_m, lse, o_r, ret_states = mix_fwd("mix_fwd", qc, kc, vv, rq, rk, rv, blk, MLA_SCALE, host=plan.host("mix_fwd"))
    plan.done("mix_fwd", w)

    def gate_mix(r0, rg, o_r, o_m, gn):
        y, _ = _group_norm(o_r)
        return [jnp.concatenate([_silu(rg) * (y * gn), o_m], axis=1)], []

    (mixcat,), _ = _rowwise("gate_mix", gate_mix, [(proj_r, hw, 3), o_r, o_m], [nw["ret_group_norm"]],
                            [(2 * hw, BF16)], [], tm)
    (mix,) = mm_nn("mix_out", mixcat, w["w_out"], tm, n_block=dt)

    def post_mix(r0, h, m, post, pre):
        h2 = h + _rms(m, post)
        return [h2, _rms(h2, pre)], []

    (h2, n3), _ = _rowwise("ffn2_norm", post_mix, [h1, mix], [nw["mix_post_norm"], nw["ffn2_pre_norm"]],
                           [(d, F32), (d, BF16)], [], tm)
    g2, u2, a2, f2 = _ffn_fwd("ffn2", n3, w, "2", tm, tmk, plan)

    def loss_head(r0, h, f, t, post):
        h3 = h + 0.5 * _rms(f, post)
        row = r0 + lax.broadcasted_iota(jnp.int32, h3.shape, 0)
        err = jnp.where(row >= N_META, jnp.where(row < t_real, h3 - t, 0.0), 0.0)
        dh3 = err / d
        df, dpost = _rms_bwd(f, post, 0.5 * dh3)
        return [dh3, df], [_colsum(err * err), _colsum(dpost)]

    (dh3, df2), (loss_vec, d_post2) = _rowwise("loss_head", loss_head, [h2, f2, tgt_p], [nw["ffn2_post_norm"]],
                                               [(d, F32), (d, BF16)], [d, d], tm)
    loss = 0.5 * jnp.sum(loss_vec) / d
    dn3 = _ffn_bwd("ffn2", df2, n3, g2, u2, a2, w["wg2"], w["wu2"], w["wd2"], tmt, tmk, tk, dt, plan)

    def back_mix_norm(r0, h, m, dh3, dn, pre, post):
        dx, dpre = _rms_bwd(h, pre, dn)
        dh2 = dh3 + dx
        dm, dpost = _rms_bwd(m, post, dh2)
        return [dh2, dm], [_colsum(dpre), _colsum(dpost)]

    (dh2, dmix), (d_pre2, d_mix_post) = _rowwise(
        "back_mix_norm", back_mix_norm, [h2, mix, dh3, dn3], [nw["ffn2_pre_norm"], nw["mix_post_norm"]],
        [(d, F32), (d, BF16)], [d, d], tm)
    (dmixcat,) = mm_nt("mix_dx", [dmix], [w["w_out"]], tmt, _tile(2 * hw, 512), outs=(BF16,))
    plan.grads(dict(w_out=mm_tn("mix_dw", mixcat, dmix, 2 * hw // N_CHIPS, dt, tk, GRAD_DT, shard_rows=True)))

    def back_gate(r0, dmc, rg, o_r, gn):
        d_ret, d_om = dmc[:, :hw], dmc[:, hw:]
        yh, rs = _group_norm(o_r)
        d_rg = d_ret * (yh * gn) * _dsilu(rg)
        dy = d_ret * _silu(rg)
        gyh = dy * gn
        d_or = _heads(lambda h: rs[h] * (_head(gyh, h) - _head(yh, h) * jnp.mean(_head(gyh, h) * _head(yh, h),
                                                                                    axis=-1, keepdims=True)))
        return [d_or, d_rg, d_om], [_colsum(dy * yh)]

    (d_or, d_rg, d_om), (d_gn,) = _rowwise("back_gate", back_gate, [dmixcat, (proj_r, hw, 3), o_r],
                                           [nw["ret_group_norm"]], [(hw, BF16), (hw, F32), (hw, BF16)], [hw], tm)
    dqc, dkc, dvv = attn_bwd("mla_bwd", qc, kc, vv, d_om, o_m, lse, blk, MLA_SCALE, host=plan.host("mla_bwd"))
    plan.done("mla_bwd", None)
    drq, drk, drv = ret_bwd("ret_bwd", rq, rk, rv, d_or, ret_states, blk)

    def back_qk(r0, dqc, dkc, dvv, cm, ta, tb):
        dkc = dkc.astype(F32)
        dqp = jnp.concatenate(
            [part for h in range(HEADS)
             for part in (_head(dqc, h, MLA_QK_PAD), _rope_m_t(_head(dqc, h, MLA_QK_PAD, HEAD_DIM), cm, ta, tb))],
            axis=1)
        dkn = _heads(lambda h: _head(dkc, h, MLA_QK_PAD))
        dkr = _head(dkc, 0, MLA_QK_PAD, HEAD_DIM)
        for h in range(1, HEADS):
            dkr = dkr + _head(dkc, h, MLA_QK_PAD, HEAD_DIM)
        return [dqp, dkn, _rope_m_t(dkr, cm, ta, tb), dvv], []

    (dqp, dkn, dkr, dvb), _ = _rowwise("back_qk", back_qk, [dqc, dkc, dvv, cos_m, sa, sb], [],
                                       [(qw, BF16), (hw, BF16), (LANES, F32), (hw, BF16)], [], tm)
    (dcqn,) = mm_nt("q_dx", [dqp], [w["wuq"]], tm, MLA_Q_RANK)
    dwuq = mm_tn("q_dw", cqn, dqp, MLA_Q_RANK, _tile(qw, 1024), tk, GRAD_DT)
    (dckvn,) = mm_nt("kv_dx", [dkn, dvb], [w["wuk"], w["wuv"]], tm, MLA_KV_RANK)
    dwuk = mm_tn("k_dw", ckvn, dkn, MLA_KV_RANK, hw, tk, GRAD_DT)
    dwuv = mm_tn("v_dw", ckvn, dvb, MLA_KV_RANK, hw, tk, GRAD_DT)

    def back_proj(r0, drq, drk, drv, d_rg, pc, dcqn, dckvn, dkr, cr, sr, qn, kvn):
        d_q = _heads(lambda h: _rope_r_t(_head(drq, h), cr, sr))
        d_k = _heads(lambda h: _rope_r_t(_head(drk, h), cr, sr) * RET_SCALE)
        dcq, a_q = _rms_bwd(pc[:, :MLA_Q_RANK], qn, dcqn)
        dckv, a_kv = _rms_bwd(pc[:, MLA_Q_RANK:MLA_Q_RANK + MLA_KV_RANK], kvn, dckvn)
        return ([jnp.concatenate([d_q, d_k, drv, d_rg], axis=1), jnp.concatenate([dcq, dckv, dkr], axis=1)],
                [_colsum(a_q), _colsum(a_kv)])

    (dproj_r, dproj_c), (d_qn, d_kvn) = _rowwise(
        "back_proj", back_proj, [drq, drk, drv, d_rg, proj_c, dcqn, dckvn, dkr, cos_r, sin_r],
        [nw["mla_q_norm"], nw["mla_kv_norm"]], [(IN_RET, BF16), (IN_MLA_PAD, BF16)],
        [MLA_Q_RANK, MLA_KV_RANK], tm)
    (dun,) = mm_nt("proj_dx", [dproj_r, dproj_c], [w["w_r"], w["w_c"]], tmt, _tile(d, 512), outs=(BF16,))
    dw_r = mm_tn("proj_dw_r", un, dproj_r, dt, _tile(IN_RET, 1024), tk, GRAD_DT)
    dw_c = mm_tn("proj_dw_c", un, dproj_c, dt, IN_MLA_PAD, tk, GRAD_DT)
    plan.grads(dict(w_r=dw_r, w_c=dw_c, wuq=dwuq, wuk=dwuk, wuv=dwuv))

    def back_ffn1_norm(r0, h, f, dh2, dn, pre, post):
        dx, dpre = _rms_bwd(h, pre, dn)
        dh1 = dh2 + dx
        df, dpost = _rms_bwd(f, post, 0.5 * dh1)
        return [dh1, df], [_colsum(dpre), _colsum(dpost)]

    (dh1, df1), (d_mix_pre, d_post1) = _rowwise(
        "back_ffn1_norm", back_ffn1_norm, [h1, f1, dh2, dun], [nw["mix_pre_norm"], nw["ffn1_post_norm"]],
        [(d, F32), (d, BF16)], [d, d], tm)
    dn1 = _ffn_bwd("ffn1", df1, n1, g1, u1, a1, w["wg1"], w["wu1"], w["wd1"], tmt, tmk, tk, dt, plan)

    def back_input(r0, h, dh1, dn, pre):
        dx, dpre = _rms_bwd(h, pre, dn)
        return [dh1 + dx], [_colsum(dpre)]

    (dh0,), (d_pre1,) = _rowwise("back_input", back_input, [h0, dh1, dn1], [nw["ffn1_pre_norm"]], [(d, F32)], [d], tm)

    small = dict(ffn1_pre_norm=d_pre1, ffn1_post_norm=d_post1, mix_pre_norm=d_mix_pre, ret_group_norm=d_gn,
                 mla_q_norm=d_qn, mla_kv_norm=d_kvn, mix_post_norm=d_mix_post, ffn2_pre_norm=d_pre2,
                 ffn2_post_norm=d_post2)
    return loss, dh0[N_META:t_real], small, dh0[:N_META]


WEIGHTS = ("meta_tokens", "ffn1_pre_norm", "ffn1_w_gate", "ffn1_w_up", "ffn1_w_down", "ffn1_post_norm",
           "mix_pre_norm", "w_in", "ret_group_norm", "mla_q_norm", "mla_w_uq", "mla_kv_norm", "mla_w_uk",
           "mla_w_uv", "w_out", "mix_post_norm", "ffn2_pre_norm", "ffn2_w_gate", "ffn2_w_up", "ffn2_w_down",
           "ffn2_post_norm")
BIG = ("ffn1_w_gate", "ffn1_w_up", "ffn1_w_down", "w_in", "mla_w_uq", "mla_w_uk", "mla_w_uv", "w_out",
       "ffn2_w_gate", "ffn2_w_up", "ffn2_w_down")
NORMS = ("ffn1_pre_norm", "ffn1_post_norm", "mix_pre_norm", "ret_group_norm", "mla_q_norm", "mla_kv_norm",
         "mix_post_norm", "ffn2_pre_norm", "ffn2_post_norm")


def _unshard_cols(g):
    return g.transpose(1, 0, 2).reshape(g.shape[1], -1)


def _shard_cols(a):
    return a.reshape(a.shape[0], N_CHIPS, -1).transpose(1, 0, 2)


def _pack_rows(rows, width):
    rows = [jnp.pad(r, ((0, 0), (0, width - r.shape[1]))) for r in rows]
    n = sum(r.shape[0] for r in rows)
    return jnp.pad(jnp.concatenate(rows, axis=0), ((0, -n % 8), (0, 0)))


def _weight_views(full):
    w = {}
    for n, g in full.items():
        if n == "w_in":
            w_in = _unshard_cols(g)
            w["w_r"] = w_in[:, :IN_RET]
            w["w_c"] = jnp.pad(w_in[:, IN_RET:], ((0, 0), (0, IN_MLA_PAD - IN_MLA)))
        elif n == "mla_w_uq":
            q = _unshard_cols(g).reshape(MLA_Q_RANK, HEADS, HEAD_DIM + MLA_ROPE)
            q = jnp.pad(q, ((0, 0), (0, 0), (0, MLA_QK_PAD - HEAD_DIM - MLA_ROPE)))
            w["wuq"] = q.reshape(MLA_Q_RANK, HEADS * MLA_QK_PAD)
        elif n in ("mla_w_uk", "mla_w_uv"):
            w["wu" + n[-1]] = _unshard_cols(g)
        elif n == "w_out":
            w["w_out"] = g.reshape(-1, g.shape[2])
        else:
            w["w" + n[7] + n[3]] = g
    return w


def _contributions(g):
    ffn = {"g": "gate", "u": "up", "d": "down"}
    c = {f"ffn{n[2]}_w_{ffn[n[1]]}": a for n, a in g.items() if len(n) == 3 and n[2] in "12"}
    if "w_out" in g:
        c["w_out"] = g["w_out"]
    if "w_r" in g:
        dwuq = g["wuq"].reshape(MLA_Q_RANK, HEADS, MLA_QK_PAD)[:, :, :HEAD_DIM + MLA_ROPE]
        c.update(w_in=_shard_cols(jnp.concatenate([g["w_r"], g["w_c"][:, :IN_MLA]], axis=1)),
                 mla_w_uq=_shard_cols(dwuq.reshape(MLA_Q_RANK, -1)), mla_w_uk=_shard_cols(g["wuk"]),
                 mla_w_uv=_shard_cols(g["wuv"]))
    return c


class _Schedule(_Alone):
    FIRST = ("ffn1_w_gate",)
    CARRIED = {"ffn1_gate": ("ffn1_w_up",), "ffn1_up": ("ffn1_w_down",),
               "ffn1_down": ("w_in", "mla_w_uq", "mla_w_uk", "mla_w_uv"),
               "proj_r": ("w_out",), "mix_fwd": ("ffn2_w_gate", "ffn2_w_up"), "ffn2_up": ("ffn2_w_down",)}
    GRAD_HOST = dict(ffn2_w_gate="mla_bwd", ffn2_w_up="mla_bwd", ffn2_w_down="ffn2_dn",
                     w_out="mla_bwd", w_in="ffn1_da", mla_w_uq="ffn1_da", mla_w_uk="ffn1_da", mla_w_uv="ffn1_da",
                     ffn1_w_gate="ffn1_dwu", ffn1_w_up="ffn1_dn", ffn1_w_down="ffn1_dn")

    def __init__(self, shards):
        self.shards = shards
        self.gathers = {k: (gather_halves([shards[n] for n in names]), names) for k, names in self.CARRIED.items()}
        self.waiting = {}
        self.exchanges = {}
        self.grad = {}

    def host(self, kernel_name):
        if kernel_name in self.gathers:
            return self.gathers[kernel_name][0]
        if kernel_name in self.waiting:
            names, sums = zip(*self.waiting.pop(kernel_name))
            self.exchanges[kernel_name] = (chip_exchange(list(sums)), names, sums)
            return self.exchanges[kernel_name][0]
        return None

    def done(self, kernel_name, w):
        if kernel_name in self.gathers:
            comm, names = self.gathers[kernel_name]
            w.update(_weight_views({n: sibling_fill("fill_" + n, b, self.shards[n])
                                    for n, b in zip(names, comm.result)}))
        elif kernel_name in self.exchanges:
            comm, names, sums = self.exchanges[kernel_name]
            for n, q, mine in zip(names, comm.result, sums):
                self.grad[n] = reduce_join("reduce_join_" + n, q, mine)

    def grads(self, g):
        for n, a in _contributions(g).items():
            self.waiting.setdefault(self.GRAD_HOST[n], []).append((n, pair_add("pair_add_" + n, a)))


def _step(p, m, v, x, loss_target):
    d = x.shape[2]
    names = list(BIG)
    shards = {n: p[n][0].astype(BF16) for n in names}
    first = _Schedule.FIRST
    gathered = gather_halves([shards[n] for n in first] + [p["meta_tokens"]]).run("gather_first")
    filled = [sibling_fill("fill_" + n, b, shards.get(n, p["meta_tokens"]))
              for n, b in zip(first + ("meta_tokens",), gathered)]
    w = _weight_views(dict(zip(first, filled[:-1])))
    meta = _unshard_cols(filled[-1])
    nw = {n: p[n] for n in NORMS}
    plan = _Schedule(shards)
    loss, grad_x, small, d_meta = _local_step(x[0], loss_target[0], meta, w, nw, plan)
    grads = dict(plan.grad)

    width = max(d, HEADS * HEAD_DIM)
    packed = _pack_rows([small[n] for n in NORMS] + [d_meta], width)
    total = sum_slots("small_sum", all_gather_devices(packed), F32)
    for i, n in enumerate(NORMS):
        grads[n] = total[i:i + 1, :p[n].shape[1]]
    cols = p["meta_tokens"].shape[1]
    chip = 2 * lax.axis_index("x") + lax.axis_index("y")
    grads["meta_tokens"] = lax.dynamic_slice(total[len(NORMS):len(NORMS) + N_META, :d], (0, chip * cols), (N_META, cols))

    delta, new_m, new_v = {}, {}, {}
    for n in names + ["meta_tokens"]:
        shape = p[n].shape
        flat = lambda a: a.reshape(-1, shape[-1])
        out = adamw("adamw_" + n, flat(p[n]), flat(grads[n]), flat(m[n]), flat(v[n]))
        delta[n], new_m[n], new_v[n], grads[n] = (o.reshape(shape) for o in out)
    pk = lambda src: _pack_rows([src[n] for n in NORMS], width)
    out = adamw("adamw_norms", pk(p), pk(grads), pk(m), pk(v))
    for i, n in enumerate(NORMS):
        delta[n], new_m[n], new_v[n] = (o[i:i + 1, :p[n].shape[1]] for o in out[:3])

    loss = lax.psum(loss, ("x", "y", "c"))
    return loss, grad_x[None], grads, delta, new_m, new_v


def kernel(x, meta_tokens, ffn1_pre_norm, ffn1_w_gate, ffn1_w_up, ffn1_w_down, ffn1_post_norm, mix_pre_norm, w_in, ret_group_norm, mla_q_norm, mla_w_uq, mla_kv_norm, mla_w_uk, mla_w_uv, w_out, mix_post_norm, ffn2_pre_norm, ffn2_w_gate, ffn2_w_up, ffn2_w_down, ffn2_post_norm, loss_target, m_meta_tokens, m_ffn1_pre_norm, m_ffn1_w_gate, m_ffn1_w_up, m_ffn1_w_down, m_ffn1_post_norm, m_mix_pre_norm, m_w_in, m_ret_group_norm, m_mla_q_norm, m_mla_w_uq, m_mla_kv_norm, m_mla_w_uk, m_mla_w_uv, m_w_out, m_mix_post_norm, m_ffn2_pre_norm, m_ffn2_w_gate, m_ffn2_w_up, m_ffn2_w_down, m_ffn2_post_norm, v_meta_tokens, v_ffn1_pre_norm, v_ffn1_w_gate, v_ffn1_w_up, v_ffn1_w_down, v_ffn1_post_norm, v_mix_pre_norm, v_w_in, v_ret_group_norm, v_mla_q_norm, v_mla_w_uq, v_mla_kv_norm, v_mla_w_uk, v_mla_w_uv, v_w_out, v_mix_post_norm, v_ffn2_pre_norm, v_ffn2_w_gate, v_ffn2_w_up, v_ffn2_w_down, v_ffn2_post_norm):
    args = locals()
    p = {n: args[n] for n in WEIGHTS}
    m = {n: args["m_" + n] for n in WEIGHTS}
    v = {n: args["v_" + n] for n in WEIGHTS}
    loss, grad_x, grads, delta, new_m, new_v = _step(p, m, v, x, loss_target)
    return (loss, grad_x, *[grads[n] for n in WEIGHTS], *[delta[n] for n in WEIGHTS],
            *[new_m[n] for n in WEIGHTS], *[new_v[n] for n in WEIGHTS])
```

```python
import functools
import math

import jax
import jax.numpy as jnp
import numpy as np
from jax import lax
from jax.experimental import pallas as pl
from jax.experimental.pallas import tpu as pltpu

F32 = jnp.float32
BF16 = jnp.bfloat16

EPS = 1e-6
N_META = 16
HEADS = 8
HEAD_DIM = 128
MLA_ROPE = 64
MLA_QK_PAD = 256
MLA_Q_RANK = 512
MLA_KV_RANK = 256
ROPE_THETA = 10000.0
N_CHIPS = 4
LANES = 128
VMEM_LIMIT = 60 * 2 ** 20

ADAM_LR = 0.001
ADAM_B1 = 0.9
ADAM_B2 = 0.999
ADAM_EPS = 1e-08
ADAM_WD = 0.01
ADAM_STEP = 10

NN = (((1,), (0,)), ((), ()))
NT = (((1,), (1,)), ((), ()))
TN = (((0,), (0,)), ((), ()))
MESH = pl.DeviceIdType.MESH


def _tile(n, pref, align=LANES):
    if n <= pref:
        return n
    best = 0
    for t in range(align, pref + 1, align):
        if n % t == 0:
            best = t
    assert best, (n, pref)
    return best


def _params(*sem, collective_id=None):
    return pltpu.CompilerParams(dimension_semantics=sem, vmem_limit_bytes=VMEM_LIMIT, collective_id=collective_id)


SIBLING_BARRIER = 0


def _sds(shape, dtype):
    return jax.ShapeDtypeStruct(tuple(shape), dtype)


def _rowwise(name, fn, rows, consts, outs, accs, tr):
    rows = [r if isinstance(r, tuple) else (r, r.shape[1], 0) for r in rows]
    t = rows[0][0].shape[0]
    assert t % tr == 0
    n_r, n_c, n_o = len(rows), len(consts), len(outs)

    def body(*refs):
        i = pl.program_id(0)
        r = [x[...] for x in refs[:n_r]]
        c = [x[...] for x in refs[n_r:n_r + n_c]]
        o_refs = refs[n_r + n_c:n_r + n_c + n_o]
        a_refs = refs[n_r + n_c + n_o:]
        o_vals, a_vals = fn(i * tr, *r, *c)
        for ref, v in zip(o_refs, o_vals):
            ref[...] = v.astype(ref.dtype)
        if a_refs:
            @pl.when(i == 0)
            def _():
                for ref, v in zip(a_refs, a_vals):
                    ref[...] = v

            @pl.when(i > 0)
            def _():
                for ref, v in zip(a_refs, a_vals):
                    ref[...] += v

    in_specs = [pl.BlockSpec((tr, w), functools.partial(lambda cb, i: (i, cb), cb)) for _, w, cb in rows]
    in_specs += [pl.BlockSpec(a.shape, lambda i: (0, 0)) for a in consts]
    out_specs = [pl.BlockSpec((tr, w), lambda i: (i, 0)) for w, _ in outs]
    out_specs += [pl.BlockSpec((1, w), lambda i: (0, 0)) for w in accs]
    out_shape = [_sds((t, w), dt) for w, dt in outs] + [_sds((1, w), F32) for w in accs]
    res = pl.pallas_call(
        body, name=name, grid=(t // tr,), in_specs=in_specs, out_specs=out_specs, out_shape=out_shape,
        compiler_params=_params("arbitrary"),
    )(*[a for a, _, _ in rows], *consts)
    return res[:n_o], res[n_o:]


def _rms(x, w):
    r = lax.rsqrt(jnp.mean(x * x, axis=-1, keepdims=True) + EPS)
    return x * r * w


def _rms_bwd(x, w, dy):
    r = lax.rsqrt(jnp.mean(x * x, axis=-1, keepdims=True) + EPS)
    xh = x * r
    gy = dy * w
    dx = r * (gy - xh * jnp.mean(gy * xh, axis=-1, keepdims=True))
    return dx, dy * xh


def _colsum(v):
    return jnp.sum(v, axis=0, keepdims=True)


def _silu(x):
    return x * jax.nn.sigmoid(x)


def _dsilu(x):
    s = jax.nn.sigmoid(x)
    return s * (1.0 + x * (1.0 - s))


def _rope_r(x, cos, sin):
    return x * cos + pltpu.roll(x, 64, 1) * sin


def _rope_r_t(dy, cos, sin):
    return dy * cos + pltpu.roll(dy * sin, 64, 1)


def _rope_m(x, cos, sa, sb):
    return x * cos + pltpu.roll(x, 32, 1) * sa + pltpu.roll(x, 96, 1) * sb


def _rope_m_t(dy, cos, sa, sb):
    return dy * cos + pltpu.roll(dy * sa, 96, 1) + pltpu.roll(dy * sb, 32, 1)


def _rope_tables(t):
    pos = np.arange(t, dtype=np.float32)

    def cs(dim):
        inv = np.float32(ROPE_THETA) ** (-np.arange(0, dim, 2, dtype=np.float32) / np.float32(dim))
        ang = pos[:, None] * inv[None, :]
        return np.cos(ang), np.sin(ang)

    c, s = cs(HEAD_DIM)
    cos_r = np.concatenate([c, c], axis=1)
    sin_r = np.concatenate([-s, s], axis=1)
    c, s = cs(MLA_ROPE)
    z32, z64 = np.zeros_like(s), np.zeros((t, 64), np.float32)
    cos_m = np.concatenate([c, c, z64], axis=1)
    sa = np.concatenate([z32, s, z64], axis=1)
    sb = np.concatenate([-s, z32, z64], axis=1)
    return tuple(jnp.asarray(a, F32) for a in (cos_r, sin_r, cos_m, sa, sb))


def _call(body, name, grid, in_specs, out_specs, out_shape, scratch, operands, host=None):
    sem = ("arbitrary",) * len(grid)
    if host is None:
        return pl.pallas_call(body, name=name, grid=grid, in_specs=in_specs, out_specs=out_specs, out_shape=out_shape,
                              scratch_shapes=scratch, compiler_params=_params(*sem))(*operands)
    n_in, n_out, n_s = len(in_specs), len(out_shape), len(scratch)
    h_in, h_out = len(host.ins), len(host.out_shape)

    def hosted(*refs):
        a = n_in
        b = a + h_in
        c = b + n_out
        d = c + h_out
        e = d + n_s
        ids = [pl.program_id(i) for i in range(len(grid))]
        first = functools.reduce(jnp.logical_and, [i == 0 for i in ids])
        last = functools.reduce(jnp.logical_and, [i == g - 1 for i, g in zip(ids, grid)])
        comm = (refs[a:b], refs[c:d], refs[e:])

        @pl.when(first)
        def _():
            host.start(*comm)

        body(*refs[:a], *refs[b:c], *refs[d:e])

        @pl.when(last)
        def _():
            host.finish(*comm)

    hbm = pl.BlockSpec(memory_space=pl.ANY)
    res = pl.pallas_call(
        hosted, name=name, grid=grid, in_specs=list(in_specs) + [hbm] * h_in, out_specs=list(out_specs) + [hbm] * h_out,
        out_shape=list(out_shape) + list(host.out_shape), scratch_shapes=list(scratch) + host.scratch(),
        input_output_aliases=host.aliases(n_in, n_out),
        compiler_params=_params(*sem, collective_id=host.collective_id))(*operands, *host.ins)
    host.result = res[n_out:]
    return res[:n_out]


def _mm(name, grid, operands, in_specs, dns, out_specs, out_shape, epilogue=None, extras=(), extra_specs=(),
        acc_shape=None, host=None):
    n_p, n_e = len(dns), len(extras)
    nk = grid[2]
    n_o = len(out_shape)
    in_place = nk > 1 and epilogue is None and n_o == 1 and out_shape[0].dtype == F32

    def body(*refs):
        ab = refs[:2 * n_p]
        ex = refs[2 * n_p:2 * n_p + n_e]
        outs = refs[2 * n_p + n_e:2 * n_p + n_e + n_o]

        part = None
        for p in range(n_p):
            d = lax.dot_general(ab[2 * p][...], ab[2 * p + 1][...], dns[p], preferred_element_type=F32)
            part = d if part is None else part + d

        def finish(acc):
            vals = (acc,) if epilogue is None else epilogue(acc, *[e[...] for e in ex])
            for o, v in zip(outs, vals):
                o[...] = v.astype(o.dtype)

        if nk == 1:
            finish(part)
        else:
            acc_ref = outs[0] if in_place else refs[2 * n_p + n_e + n_o]
            k = pl.program_id(2)

            @pl.when(k == 0)
            def _():
                acc_ref[...] = part

            @pl.when(k > 0)
            def _():
                acc_ref[...] += part

            if not in_place:
                @pl.when(k == nk - 1)
                def _():
                    finish(acc_ref[...])

    scratch = [] if nk == 1 or in_place else [pltpu.VMEM(acc_shape, F32)]
    return _call(body, name, grid, list(in_specs) + list(extra_specs), out_specs, out_shape, scratch,
                 [*operands, *extras], host)


def mm_nn(name, x, w, tm, out_dtype=F32, epilogue=None, outs=None, w2=None, n_block=None, extras=(), host=None):
    t, kdim = x.shape
    if w.ndim == 3:
        s, _, ns = w.shape
        n, tn, nb = s * ns, ns, s
        wspec = pl.BlockSpec((None, kdim, ns), lambda j, i, k: (j, 0, 0))
    else:
        n = w.shape[1]
        tn = n_block or n
        nb = n // tn
        wspec = pl.BlockSpec((kdim, tn), lambda j, i, k: (0, j))
    xspec = pl.BlockSpec((tm, kdim), lambda j, i, k: (i, 0))
    ospec = pl.BlockSpec((tm, tn), lambda j, i, k: (i, j))
    outs = outs or [out_dtype]
    grid = (nb, t // tm, 1)
    if w2 is None:
        return _mm(name, grid, [x, w], [xspec, wspec], [NN], [ospec] * len(outs), [_sds((t, n), d) for d in outs],
                   epilogue=epilogue, extras=extras, extra_specs=[ospec] * len(extras), host=host)

    def body(x_ref, w_ref, w2_ref, *o_refs):
        xv = x_ref[...]
        a = jnp.dot(xv, w_ref[...], preferred_element_type=F32)
        b = jnp.dot(xv, w2_ref[...], preferred_element_type=F32)
        for o, v in zip(o_refs, epilogue(a, b)):
            o[...] = v.astype(o.dtype)

    return _call(body, name, grid[:2],
                 [pl.BlockSpec((tm, kdim), lambda j, i: (i, 0)),
                  pl.BlockSpec((None, kdim, tn), lambda j, i: (j, 0, 0)),
                  pl.BlockSpec((None, kdim, tn), lambda j, i: (j, 0, 0))],
                 [pl.BlockSpec((tm, tn), lambda j, i: (i, j))] * len(outs), [_sds((t, n), d) for d in outs], [],
                 [x, w, w2], host)


def mm_nn_k(name, x, w, tm, tk, out_dtype=F32, host=None):
    t, kdim = x.shape
    n = w.shape[1]
    grid = (t // tm, 1, kdim // tk)
    return _mm(name, grid, [x, w],
               [pl.BlockSpec((tm, tk), lambda i, j, k: (i, k)), pl.BlockSpec((tk, n), lambda i, j, k: (k, 0))],
               [NN], [pl.BlockSpec((tm, n), lambda i, j, k: (i, 0))], [_sds((t, n), out_dtype)],
               acc_shape=(tm, n), host=host)[0]


def mm_nt(name, xs, ws, tm, tn, outs=(F32,), epilogue=None, extras=(), host=None):
    t = xs[0].shape[0]
    specs, ops = [], []
    for x, w in zip(xs, ws):
        kdim = x.shape[1]
        specs.append(pl.BlockSpec((tm, kdim), lambda j, i, k: (i, 0)))
        if w.ndim == 3:
            assert tn == w.shape[1]
            n = w.shape[0] * w.shape[1]
            specs.append(pl.BlockSpec((None, tn, kdim), lambda j, i, k: (j, 0, 0)))
        else:
            n = w.shape[0]
            specs.append(pl.BlockSpec((tn, kdim), lambda j, i, k: (j, 0)))
        ops += [x, w]
    ospec = pl.BlockSpec((tm, tn), lambda j, i, k: (i, j))
    return _mm(name, (n // tn, t // tm, 1), ops, specs, [NT] * len(xs), [ospec] * len(outs),
               [_sds((t, n), d) for d in outs], epilogue=epilogue, extras=extras,
               extra_specs=[ospec] * len(extras), host=host)


def mm_nt_k(name, xs, ws, tm, tn, out_dtype=F32, host=None):
    t = xs[0].shape[0]
    s, n, ns = ws[0].shape
    specs, ops = [], []
    for x, w in zip(xs, ws):
        specs.append(pl.BlockSpec((tm, ns), lambda i, j, k: (i, k)))
        specs.append(pl.BlockSpec((None, tn, ns), lambda i, j, k: (k, j, 0)))
        ops += [x, w]
    return _mm(name, (t // tm, n // tn, s), ops, specs, [NT] * len(xs),
               [pl.BlockSpec((tm, tn), lambda i, j, k: (i, j))], [_sds((t, n), out_dtype)], acc_shape=(tm, tn),
               host=host)[0]


def mm_tn(name, x, y, tm, tn, tk, out_dtype, shard_rows=False, shard_cols=False, host=None):
    t, m = x.shape
    n = y.shape[1]
    grid = (m // tm, n // tn, t // tk)
    if shard_cols:
        ospec = pl.BlockSpec((None, tm, tn), lambda i, j, k: (j, i, 0))
        oshape = _sds((n // tn, m, tn), out_dtype)
    elif shard_rows:
        ospec = pl.BlockSpec((None, tm, tn), lambda i, j, k: (i, 0, j))
        oshape = _sds((m // tm, tm, n), out_dtype)
    else:
        ospec = pl.BlockSpec((tm, tn), lambda i, j, k: (i, j))
        oshape = _sds((m, n), out_dtype)
    return _mm(name, grid, [x, y],
               [pl.BlockSpec((tk, tm), lambda i, j, k: (k, i)), pl.BlockSpec((tk, tn), lambda i, j, k: (k, j))],
               [TN], [ospec], [oshape], acc_shape=(tm, tn), host=host)[0]


def _decay_logs():
    return [math.log(1.0 - 2.0 ** (-5.0 - h)) for h in range(HEADS)]


def _log_decay(h):
    lg = jnp.float32(_decay_logs()[0])
    for i in range(1, HEADS):
        lg = jnp.where(h == i, jnp.float32(_decay_logs()[i]), lg)
    return lg


def _decayed_scores(q, k, lg):
    s = lax.dot_general(q, k, NT, preferred_element_type=F32)
    row = lax.broadcasted_iota(jnp.int32, s.shape, 0)
    col = lax.broadcasted_iota(jnp.int32, s.shape, 1)
    dec = jnp.where(col <= row, jnp.exp(jnp.maximum(row - col, 0).astype(F32) * lg), 0.0)
    return s * dec, dec


def _causal(s):
    row = lax.broadcasted_iota(jnp.int32, s.shape, 0)
    col = lax.broadcasted_iota(jnp.int32, s.shape, 1)
    return jnp.where(col <= row, s, -1e30)


def _attn_fwd_step(qi, blk, scale, q_ref, k_ref, v_ref, o_ref, lse_ref, m_ref, l_ref, acc_ref):
    qv = q_ref[...]
    m_ref[...] = jnp.full_like(m_ref, -1e30)
    l_ref[...] = jnp.zeros_like(l_ref)
    acc_ref[...] = jnp.zeros_like(acc_ref)

    def keys(start, n, diag_from):
        rows = pl.ds(pl.multiple_of(start, blk), n)
        s = lax.dot_general(qv, k_ref[rows, :], NT, preferred_element_type=F32) * scale
        if diag_from is not None:
            row = lax.broadcasted_iota(jnp.int32, s.shape, 0)
            col = lax.broadcasted_iota(jnp.int32, s.shape, 1)
            s = jnp.where(col - diag_from <= row, s, -1e30)
        m = m_ref[...]
        m_new = jnp.maximum(m, jnp.max(s, axis=-1, keepdims=True))
        p = jnp.exp(s - m_new)
        alpha = jnp.exp(m - m_new)
        m_ref[...] = m_new
        l_ref[...] = alpha * l_ref[...] + jnp.sum(p, axis=-1, keepdims=True)
        acc_ref[...] = alpha * acc_ref[...] + jnp.dot(p.astype(BF16), v_ref[rows, :], preferred_element_type=F32)

    @pl.loop(0, qi // 2)
    def _(j):
        keys(j * (2 * blk), 2 * blk, None)

    @pl.when(qi % 2 == 1)
    def _():
        keys((qi - 1) * blk, 2 * blk, blk)

    @pl.when(qi % 2 == 0)
    def _():
        keys(qi * blk, blk, 0)

    l = l_ref[...]
    o_ref[...] = acc_ref[...] / l
    lse_ref[...] = jnp.broadcast_to(m_ref[...] + jnp.log(l), (blk, HEAD_DIM))


def mix_fwd(name, q, k, v, rq, rk, rv, blk, scale, host=None):
    t = q.shape[0]
    dq = q.shape[1] // HEADS
    nq = t // blk

    def body(q_ref, k_ref, v_ref, rq_ref, rk_ref, rv_ref, o_ref, lse_ref, ro_ref, st_ref, m_ref, l_ref, acc_ref, state):
        h, i = pl.program_id(0), pl.program_id(1)

        @pl.when(h < HEADS)
        def _():
            _attn_fwd_step(i, blk, scale, q_ref, k_ref, v_ref, o_ref, lse_ref, m_ref, l_ref, acc_ref)

        @pl.when(h >= HEADS)
        def _():
            _ret_fwd_step(h - HEADS, i, blk, rq_ref, rk_ref, rv_ref, ro_ref, st_ref, state)

    mla = lambda w, whole=False: pl.BlockSpec(
        (t if whole else blk, w),
        lambda h, i: (0 if whole else jnp.where(h < HEADS, i, nq - 1), jnp.minimum(h, HEADS - 1)))
    ret = lambda rows: pl.BlockSpec(
        (rows, HEAD_DIM), lambda h, i: (jnp.where(h >= HEADS, i, 0), jnp.maximum(h - HEADS, 0)))
    wide = _sds((t, HEADS * HEAD_DIM), F32)
    return _call(body, name, (2 * HEADS, nq),
                 [mla(dq), mla(dq, True), mla(HEAD_DIM, True), ret(blk), ret(blk), ret(blk)],
                 [mla(HEAD_DIM), mla(HEAD_DIM), ret(blk), ret(HEAD_DIM)],
                 [wide, wide, wide, _sds((nq * HEAD_DIM, HEADS * HEAD_DIM), F32)],
                 [pltpu.VMEM((blk, 1), F32), pltpu.VMEM((blk, 1), F32), pltpu.VMEM((blk, HEAD_DIM), F32),
                  pltpu.VMEM((HEAD_DIM, HEAD_DIM), F32)],
                 [q, k, v, rq, rk, rv], host)


def attn_bwd(name, q, k, v, do, o, lse, blk, scale, host=None):
    t = q.shape[0]
    dq_w = q.shape[1] // HEADS
    nb = t // blk

    def body(q_ref, k_ref, v_ref, do_ref, o_ref, lse_ref, dq_ref, dk_out, dv_out, dk_ref, dv_ref):
        ki = pl.program_id(1)
        kv = k_ref[...]
        vv = v_ref[...]

        @pl.when(ki == 0)
        def _():
            dq_ref[...] = jnp.zeros_like(dq_ref)

        def queries(start, n, diag):
            rows = pl.ds(pl.multiple_of(start, blk), n)
            qv, dov = q_ref[rows, :], do_ref[rows, :]
            s = lax.dot_general(qv, kv, NT, preferred_element_type=F32) * scale
            if diag:
                s = _causal(s)
            p = jnp.exp(s - lse_ref[rows, :][:, :1])
            dp = lax.dot_general(dov, vv, NT, preferred_element_type=F32)
            delta = jnp.sum(dov.astype(F32) * o_ref[rows, :], axis=-1, keepdims=True)
            ds = p * (dp - delta) * scale
            pb, dsb = p.astype(BF16), ds.astype(BF16)
            dv_ref[...] += lax.dot_general(pb, dov, TN, preferred_element_type=F32)
            dk_ref[...] += lax.dot_general(dsb, qv, TN, preferred_element_type=F32)
            dq_ref[rows, :] += jnp.dot(dsb, kv, preferred_element_type=F32)

        dk_ref[...] = jnp.zeros_like(dk_ref)
        dv_ref[...] = jnp.zeros_like(dv_ref)
        queries(ki * blk, blk, True)
        later = nb - 1 - ki

        @pl.when(later % 2 == 1)
        def _():
            queries((ki + 1) * blk, blk, False)

        @pl.loop(0, later // 2)
        def _(j):
            queries((ki + 1 + later % 2 + 2 * j) * blk, 2 * blk, False)

        dk_out[...] = dk_ref[...].astype(dk_out.dtype)
        dv_out[...] = dv_ref[...].astype(dv_out.dtype)

    full = lambda w: pl.BlockSpec((t, w), lambda h, j: (0, h))
    blkd = lambda w: pl.BlockSpec((blk, w), lambda h, j: (j, h))
    return _call(body, name, (HEADS, nb),
                 [full(dq_w), blkd(dq_w), blkd(HEAD_DIM), full(HEAD_DIM), full(HEAD_DIM), full(HEAD_DIM)],
                 [full(dq_w), blkd(dq_w), blkd(HEAD_DIM)],
                 [_sds(q.shape, F32), _sds(k.shape, BF16), _sds(v.shape, BF16)],
                 [pltpu.VMEM((blk, dq_w), F32), pltpu.VMEM((blk, HEAD_DIM), F32)], [q, k, v, do, o, lse], host)


def _chunk_decays(lg, blk):
    row = lax.broadcasted_iota(jnp.int32, (blk, HEAD_DIM), 0).astype(F32)
    return jnp.exp(lg * (row + 1.0)), jnp.exp(lg * (blk - 1.0 - row)), jnp.exp(lg * blk * jnp.ones((1, HEAD_DIM), F32))


def _ret_fwd_step(h, i, blk, q_ref, k_ref, v_ref, o_ref, st_ref, state):
    lg = _log_decay(h)

    @pl.when(i == 0)
    def _():
        state[...] = jnp.zeros_like(state)

    qv, kv, vv = q_ref[...], k_ref[...], v_ref[...]
    before = state[...]
    st_ref[...] = before
    p, _ = _decayed_scores(qv, kv, lg)
    xi, zeta, g_blk = _chunk_decays(lg, blk)
    o_ref[...] = (jnp.dot(p.astype(BF16), vv, preferred_element_type=F32)
                  + jnp.dot(qv, before.astype(BF16), preferred_element_type=F32) * xi)
    kz = (kv.astype(F32) * zeta).astype(BF16)
    state[...] = before * g_blk + lax.dot_general(kz, vv, TN, preferred_element_type=F32)


def ret_bwd(name, q, k, v, do, states, blk, host=None):
    t = q.shape[0]
    nb = t // blk

    def body(q_ref, k_ref, v_ref, do_ref, st_ref, dq_ref, dk_ref, dv_ref, dstate):
        h, i = pl.program_id(0), pl.program_id(1)
        lg = _log_decay(h)

        @pl.when(i == 0)
        def _():
            dstate[...] = jnp.zeros_like(dstate)

        qv, kv, vv, dov = q_ref[...], k_ref[...], v_ref[...], do_ref[...]
        before = st_ref[...].astype(BF16)
        after_grad = dstate[...]
        p, dec = _decayed_scores(qv, kv, lg)
        ds = lax.dot_general(dov, vv, NT, preferred_element_type=F32) * dec
        pb, dsb = p.astype(BF16), ds.astype(BF16)
        xi, zeta, g_blk = _chunk_decays(lg, blk)
        dox = (dov.astype(F32) * xi).astype(BF16)
        kz = (kv.astype(F32) * zeta).astype(BF16)
        agb = after_grad.astype(BF16)
        dv = lax.dot_general(pb, dov, TN, preferred_element_type=F32) + jnp.dot(kz, agb, preferred_element_type=F32)
        dq = (jnp.dot(dsb, kv, preferred_element_type=F32)
              + lax.dot_general(dox, before, NT, preferred_element_type=F32))
        dk = (lax.dot_general(dsb, qv, TN, preferred_element_type=F32)
              + lax.dot_general(vv, agb, NT, preferred_element_type=F32) * zeta)
        for ref, val in ((dq_ref, dq), (dk_ref, dk), (dv_ref, dv)):
            ref[...] = val.astype(ref.dtype)
        dstate[...] = after_grad * g_blk + lax.dot_general(qv, dox, TN, preferred_element_type=F32)

    hspec = pl.BlockSpec((blk, HEAD_DIM), lambda h, i: (nb - 1 - i, h))
    return _call(body, name, (HEADS, nb),
                 [hspec] * 4 + [pl.BlockSpec((HEAD_DIM, HEAD_DIM), lambda h, i: (nb - 1 - i, h))],
                 [hspec] * 3, [_sds(q.shape, BF16)] * 3, [pltpu.VMEM((HEAD_DIM, HEAD_DIM), F32)],
                 [q, k, v, do, states], host)


CHIP_FLIPS = ((1, 0), (0, 1), (1, 1))
CHIP_BARRIER = 1
ALL_BARRIER = 2


def _position():
    return lax.axis_index("x"), lax.axis_index("y"), lax.axis_index("c")


def _chip_peers(pos):
    x, y, c = pos
    return [(x ^ fx, y ^ fy, c) for fx, fy in CHIP_FLIPS]


class _Comm:
    def __init__(self, ins, out_shape, plan, n_remote, n_local, collective_id, in_place=False):
        self.ins, self.out_shape, self.plan = list(ins), list(out_shape), plan
        self.n_remote, self.n_local, self.collective_id = n_remote, n_local, collective_id
        self.in_place = in_place
        self.result = None

    def aliases(self, first_in, first_out):
        return {first_in + i: first_out + i for i in range(len(self.ins))} if self.in_place else {}

    def scratch(self):
        return [pltpu.SemaphoreType.DMA((self.n_remote,)), pltpu.SemaphoreType.DMA((self.n_remote,)),
                pltpu.SemaphoreType.DMA((max(self.n_local, 1),))]

    def _copies(self, in_refs, out_refs, sems):
        send_sems, recv_sems, local_sems = sems
        pos = _position()
        p = self.plan(pos, in_refs, out_refs)

        def remote(k, src, dst, dev):
            return pltpu.make_async_remote_copy(src_ref=src, dst_ref=dst, send_sem=send_sems.at[k],
                                                recv_sem=recv_sems.at[k], device_id=dev, device_id_type=MESH)

        local = [pltpu.make_async_copy(s, d, local_sems.at[i]) for i, (s, d) in enumerate(p["local"])]
        out = [remote(k, s, d, dev) for k, (s, d, dev) in enumerate(p["sends"])]
        arrivals = [functools.partial(remote, k, d, d, pos) for k, d in enumerate(p["recvs"])]
        return local, out, arrivals, p["peers"]

    def start(self, in_refs, out_refs, sems):
        local, out, _, peers = self._copies(in_refs, out_refs, sems)
        barrier = pltpu.get_barrier_semaphore()
        for peer in peers:
            pl.semaphore_signal(barrier, inc=1, device_id=peer, device_id_type=MESH)
        pl.semaphore_wait(barrier, len(peers))
        for cp in local + out:
            cp.start()

    def finish(self, in_refs, out_refs, sems):
        local, out, arrivals, _ = self._copies(in_refs, out_refs, sems)
        for make in arrivals:
            make().wait_recv()
        for cp in out:
            cp.wait_send()
        for cp in local:
            cp.wait()

    def run(self, name):
        n_in, n_out = len(self.ins), len(self.out_shape)

        def body(*refs):
            comm = (refs[:n_in], refs[n_in:n_in + n_out], refs[n_in + n_out:])
            self.start(*comm)
            self.finish(*comm)

        hbm = pl.BlockSpec(memory_space=pl.ANY)
        self.result = pl.pallas_call(
            body, name=name, in_specs=[hbm] * n_in, out_specs=[hbm] * n_out, out_shape=self.out_shape,
            scratch_shapes=self.scratch(), input_output_aliases=self.aliases(0, 0),
            compiler_params=pltpu.CompilerParams(collective_id=self.collective_id))(*self.ins)
        return self.result


def _half_rows(c, rows):
    r2 = rows // 2
    return pl.ds(pl.multiple_of(c * r2, math.gcd(r2, LANES)), r2)


def _half(ref, c, rows, lead=()):
    return ref.at[(*lead, _half_rows(c, rows))]


def own_slot(shard):
    me = 2 * lax.axis_index("x") + lax.axis_index("y")
    return lax.dynamic_update_slice(lax.empty((N_CHIPS, *shard.shape), shard.dtype), shard[None], (me, 0, 0))


def gather_halves(bufs):
    def plan(pos, ins, outs):
        x, y, c = pos
        me = 2 * x + y
        p = dict(local=[], sends=[], recvs=[], peers=_chip_peers(pos))
        for a, dst in enumerate(outs):
            rows = bufs[a].shape[1]
            for px, py, _ in p["peers"]:
                p["sends"].append((_half(dst, c, rows, (me,)), _half(dst, c, rows, (me,)), (px, py, c)))
                p["recvs"].append(_half(dst, c, rows, (2 * px + py,)))
        return p

    return _Comm(bufs, [_sds(b.shape, b.dtype) for b in bufs], plan, n_remote=3 * len(bufs), n_local=0,
                 collective_id=CHIP_BARRIER, in_place=True)


def chip_exchange(parts):
    def plan(pos, ins, outs):
        x, y, c = pos
        me = 2 * x + y
        p = dict(local=[], sends=[], recvs=[], peers=_chip_peers(pos))
        for src, dst in zip(ins, outs):
            for px, py, _ in p["peers"]:
                peer = 2 * px + py
                p["sends"].append((src.at[peer], dst.at[me], (px, py, c)))
                p["recvs"].append(dst.at[peer])
        return p

    return _Comm(parts, [_sds(g.shape, g.dtype) for g in parts], plan, n_remote=3 * len(parts), n_local=0,
                 collective_id=CHIP_BARRIER)


def all_gather_devices(v):
    flips = [(fx, fy, fc) for fx in (0, 1) for fy in (0, 1) for fc in (0, 1)][1:]

    def plan(pos, ins, outs):
        x, y, c = pos
        me = 4 * x + 2 * y + c
        p = dict(local=[(ins[0], outs[0].at[me])], sends=[], recvs=[],
                 peers=[(x ^ fx, y ^ fy, c ^ fc) for fx, fy, fc in flips])
        for px, py, pc in p["peers"]:
            p["sends"].append((ins[0], outs[0].at[me], (px, py, pc)))
            p["recvs"].append(outs[0].at[4 * px + 2 * py + pc])
        return p

    return _Comm([v], [_sds((8, *v.shape), v.dtype)], plan, n_remote=7, n_local=1,
                 collective_id=ALL_BARRIER).run("small_all_gather")[0]


SWAP_CHUNK_BYTES = 3 * 2 ** 19


def _sibling_stream(t, n, value, consume, sbuf, rbuf, send_sems, recv_sems, credits):
    x, y, c = _position()
    sib = (x, y, 1 - c)

    def copy(slot):
        return pltpu.make_async_remote_copy(src_ref=sbuf.at[slot], dst_ref=rbuf.at[slot], send_sem=send_sems.at[slot],
                                            recv_sem=recv_sems.at[slot], device_id=sib, device_id_type=MESH)

    slot = t % 2

    @pl.when(t == 0)
    def _():
        barrier = pltpu.get_barrier_semaphore()
        pl.semaphore_signal(barrier, inc=1, device_id=sib, device_id_type=MESH)
        pl.semaphore_wait(barrier, 1)

    @pl.when(jnp.logical_and(t >= 2, t < n))
    def _():
        copy(slot).wait_send()
        pl.semaphore_wait(credits.at[slot], 1)

    @pl.when(t < n)
    def _():
        sbuf[slot] = value
        copy(slot).start()

    @pl.when(t >= 1)
    def _():
        prev = 1 - slot
        copy(prev).wait_recv()
        consume(sbuf[prev], rbuf[prev])

        @pl.when(t + 1 < n)
        def _():
            pl.semaphore_signal(credits.at[prev], inc=1, device_id=sib, device_id_type=MESH)

    @pl.when(t == n)
    def _():
        copy(1 - slot).wait_send()
        if n > 1:
            copy(slot).wait_send()


def _swap_scratch(rows, cols, dtype):
    return [pltpu.VMEM((2, rows, cols), dtype), pltpu.VMEM((2, rows, cols), dtype),
            pltpu.SemaphoreType.DMA((2,)), pltpu.SemaphoreType.DMA((2,)), pltpu.SemaphoreType.REGULAR((2,))]


def _chunk_rows(rows, cols, dtype, nbytes=SWAP_CHUNK_BYTES):
    return _tile(rows, max(16, nbytes // (cols * jnp.dtype(dtype).itemsize)), 16)


def pair_add(name, g):
    s, r, c_ = g.shape
    r2 = r // 2
    cr = _chunk_rows(r2, c_, g.dtype)
    nj = r2 // cr

    n = s * nj

    def body(core, mine_ref, theirs_ref, o_ref, *scratch):
        def consume(_, got):
            o_ref[...] = (mine_ref[...].astype(F32) + got.astype(F32)).astype(o_ref.dtype)

        _sibling_stream(pl.program_id(0), n, theirs_ref[...], consume, *scratch)

    sent = lambda t: jnp.minimum(t, n - 1)
    used = lambda t: jnp.maximum(t - 1, 0)
    grid_spec = pltpu.PrefetchScalarGridSpec(
        num_scalar_prefetch=1, grid=(n + 1,),
        in_specs=[pl.BlockSpec((None, cr, c_), lambda t, core: (used(t) // nj, core[0] * nj + used(t) % nj, 0)),
                  pl.BlockSpec((None, cr, c_), lambda t, core: (sent(t) // nj, (1 - core[0]) * nj + sent(t) % nj, 0))],
        out_specs=pl.BlockSpec((None, cr, c_), lambda t, core: (used(t) // nj, used(t) % nj, 0)),
        scratch_shapes=_swap_scratch(cr, c_, g.dtype))
    core = lax.axis_index("c").astype(jnp.int32).reshape(1)
    return pl.pallas_call(body, name=name, grid_spec=grid_spec, out_shape=_sds((s, r2, c_), g.dtype),
                          compiler_params=_params("arbitrary", collective_id=SIBLING_BARRIER))(core, g, g)


def _adamw_math(w, g, m, v):
    m = ADAM_B1 * m + (1.0 - ADAM_B1) * g
    v = ADAM_B2 * v + (1.0 - ADAM_B2) * (g * g)
    m_hat = m / (1.0 - ADAM_B1 ** ADAM_STEP)
    v_hat = v / (1.0 - ADAM_B2 ** ADAM_STEP)
    return -ADAM_LR * (m_hat / (jnp.sqrt(v_hat) + ADAM_EPS) + ADAM_WD * w), m, v


def reduce_join(name, p, mine):
    s, r2, c_ = p.shape
    cr = _chunk_rows(r2, c_, F32)
    nj = r2 // cr

    def body(where, p_ref, mine_ref, o_ref, acc_ref, *scratch):
        c, me = where[0], where[1]
        for chip in range(s):
            @pl.when(me == chip)
            def _():
                terms = [mine_ref[...] if i == chip else p_ref[i] for i in range(s)]
                acc = terms[0].astype(F32)
                for term in terms[1:]:
                    acc = acc + term.astype(F32)
                acc_ref[...] = acc

        def consume(own, got):
            o_ref[c] = own
            o_ref[1 - c] = got

        _sibling_stream(pl.program_id(0), nj, acc_ref[...], consume, *scratch)

    sent = lambda t: jnp.minimum(t, nj - 1)
    grid_spec = pltpu.PrefetchScalarGridSpec(
        num_scalar_prefetch=1, grid=(nj + 1,),
        in_specs=[pl.BlockSpec((s, cr, c_), lambda t, where: (0, sent(t), 0)),
                  pl.BlockSpec((None, cr, c_), lambda t, where: (where[1], sent(t), 0))],
        out_specs=pl.BlockSpec((2, cr, c_), lambda t, where: (0, jnp.maximum(t - 1, 0), 0)),
        scratch_shapes=[pltpu.VMEM((cr, c_), F32)] + _swap_scratch(cr, c_, F32))
    x, y, c = _position()
    where = jnp.stack([c, 2 * x + y]).astype(jnp.int32)
    out = pl.pallas_call(body, name=name, grid_spec=grid_spec, out_shape=_sds((2, r2, c_), F32),
                         compiler_params=_params("arbitrary", collective_id=SIBLING_BARRIER))(where, p, mine)
    return out.reshape(2 * r2, c_)


def sibling_fill(name, buf):
    s, r, c_ = buf.shape
    r2 = r // 2
    cr = _chunk_rows(r2, c_, buf.dtype)
    nj = r2 // cr
    n_peers = len(CHIP_FLIPS)
    n = n_peers * nj

    def body(where, in_ref, o_ref, *scratch):
        def consume(_, got):
            o_ref[...] = got

        _sibling_stream(pl.program_id(0), n, in_ref[...], consume, *scratch)

    sent = lambda t: jnp.minimum(t, n - 1)
    used = lambda t: jnp.maximum(t - 1, 0)
    grid_spec = pltpu.PrefetchScalarGridSpec(
        num_scalar_prefetch=1, grid=(n + 1,),
        in_specs=[pl.BlockSpec((None, cr, c_),
                               lambda t, where: (where[sent(t) // nj], where[n_peers] * nj + sent(t) % nj, 0))],
        out_specs=pl.BlockSpec((None, cr, c_),
                               lambda t, where: (where[used(t) // nj], (1 - where[n_peers]) * nj + used(t) % nj, 0)),
        scratch_shapes=_swap_scratch(cr, c_, buf.dtype))
    x, y, c = _position()
    where = jnp.stack([2 * (x ^ fx) + (y ^ fy) for fx, fy in CHIP_FLIPS] + [c]).astype(jnp.int32)
    return pl.pallas_call(body, name=name, grid_spec=grid_spec, out_shape=_sds(buf.shape, buf.dtype),
                          input_output_aliases={1: 0},
                          compiler_params=_params("arbitrary", collective_id=SIBLING_BARRIER))(where, buf)


def sum_slots(name, p, out_dtype):
    s, r, c = p.shape
    tr = _tile(r, 256, 16)

    def body(p_ref, o_ref):
        acc = p_ref[0].astype(F32)
        for i in range(1, s):
            acc = acc + p_ref[i].astype(F32)
        o_ref[...] = acc.astype(o_ref.dtype)

    return pl.pallas_call(
        body, name=name, grid=(r // tr,), in_specs=[pl.BlockSpec((s, tr, c), lambda i: (0, i, 0))],
        out_specs=pl.BlockSpec((tr, c), lambda i: (i, 0)), out_shape=_sds((r, c), out_dtype),
        compiler_params=_params("arbitrary"),
    )(p)


def adamw(name, w, g, m, v):
    r, c = w.shape
    outs, _ = _rowwise(name, lambda _, w, g, m, v: ([*_adamw_math(w, g, m, v), g], []), [w, g, m, v], [],
                       [(c, F32)] * 4, [], _tile(r, 256, 8))
    return outs


RET_SCALE = HEAD_DIM ** -0.5
MLA_SCALE = (HEAD_DIM + MLA_ROPE) ** -0.5
GRAD_DT = BF16
IN_RET = 4 * HEADS * HEAD_DIM
IN_MLA = MLA_Q_RANK + MLA_KV_RANK + MLA_ROPE
IN_MLA_PAD = IN_MLA + 64


def _heads(fn):
    return jnp.concatenate([fn(h) for h in range(HEADS)], axis=1)


def _head(a, h, stride=HEAD_DIM, off=0):
    return a[:, h * stride + off:h * stride + off + HEAD_DIM]


def _group_norm(o):
    rs = [lax.rsqrt(jnp.mean(_head(o, h) * _head(o, h), axis=-1, keepdims=True) + EPS) for h in range(HEADS)]
    return _heads(lambda h: _head(o, h) * rs[h]), rs


class _Alone:
    def host(self, kernel_name):
        return None

    def done(self, kernel_name, w):
        pass

    def grads(self, g):
        pass


def _ffn_fwd(tag, n, w, k, tm, tmk, plan):
    gate, up, down = tag + "_gate", tag + "_up", tag + "_down"
    if "wu" + k in w:
        g, u, a = mm_nn(up, n, w["wg" + k], tm, outs=[BF16] * 3, w2=w["wu" + k],
                        epilogue=lambda g, u: (g, u, _silu(g) * u), host=plan.host(up))
    else:
        (g,) = mm_nn(gate, n, w["wg" + k], tm, out_dtype=BF16, host=plan.host(gate))
        plan.done(gate, w)
        u, a = mm_nn(up, n, w["wu" + k], tm, outs=[BF16] * 2, extras=(g,),
                     epilogue=lambda u, g: (u, _silu(g.astype(F32)) * u), host=plan.host(up))
    plan.done(up, w)
    ff = a.shape[1]
    wd = w["wd" + k]
    f = mm_nn_k(down, a, wd.reshape(ff, wd.shape[2]), tmk, _tile(ff, 1408), host=plan.host(down))
    plan.done(down, w)
    return g, u, a, f


def _ffn_bwd(tag, df, n, g, u, a, wg, wu, wd, tm, tmk, tk, dt, plan):
    ns = wg.shape[2]

    def gate_grads(da, g, u):
        g, u = g.astype(F32), u.astype(F32)
        return da * u * _dsilu(g), da * _silu(g)

    def hosted(kernel_name, call):
        out = call(plan.host(kernel_name))
        plan.done(kernel_name, None)
        return out

    k = tag[-1]
    dg, du = hosted(tag + "_da", lambda h: mm_nt(tag + "_da", [df], [wd], tm, ns, outs=(BF16, BF16),
                                                 epilogue=gate_grads, extras=(g, u), host=h))
    dwg = hosted(tag + "_dwg", lambda h: mm_tn(tag + "_dwg", n, dg, dt, ns, tk, GRAD_DT, shard_cols=True, host=h))
    plan.grads({"wg" + k: dwg})
    dwu = hosted(tag + "_dwu", lambda h: mm_tn(tag + "_dwu", n, du, dt, ns, tk, GRAD_DT, shard_cols=True, host=h))
    plan.grads({"wu" + k: dwu})
    dwd = hosted(tag + "_dwd", lambda h: mm_tn(tag + "_dwd", a, df, ns, dt, tk, GRAD_DT, shard_rows=True, host=h))
    plan.grads({"wd" + k: dwd})
    dn = hosted(tag + "_dn", lambda h: mm_nt_k(tag + "_dn", [dg, du], [wg, wu], tmk, dt, out_dtype=BF16, host=h))
    return dn


def _local_step(x, tgt, meta, w, nw, plan):
    seq, d = x.shape
    t_real = N_META + seq
    tp = -(-t_real // LANES) * LANES
    zpad = jnp.zeros((tp - t_real, d), F32)
    h0 = jnp.concatenate([meta, x, zpad], axis=0)
    tgt_p = jnp.concatenate([jnp.zeros((N_META, d), F32), tgt, zpad], axis=0)
    cos_r, sin_r, cos_m, sa, sb = _rope_tables(tp)
    tm = _tile(tp, 512)
    tmt = _tile(tp, 768, 16)
    tmk = _tile(tp, 1408)
    tk = tp
    blk = tm
    dt = _tile(d, 1024)
    hw = HEADS * HEAD_DIM
    qw = HEADS * MLA_QK_PAD

    (n1,), _ = _rowwise("ffn1_norm", lambda r0, h, g: ([_rms(h, g)], []), [h0], [nw["ffn1_pre_norm"]],
                        [(d, BF16)], [], tm)
    g1, u1, a1, f1 = _ffn_fwd("ffn1", n1, w, "1", tm, tmk, plan)

    def post_ffn1(r0, h, f, post, pre):
        h1 = h + 0.5 * _rms(f, post)
        return [h1, _rms(h1, pre)], []

    (h1, un), _ = _rowwise("mix_norm", post_ffn1, [h0, f1], [nw["ffn1_post_norm"], nw["mix_pre_norm"]],
                           [(d, F32), (d, BF16)], [], tm)
    (proj_r,) = mm_nn("proj_r", un, w["w_r"], tm, n_block=_tile(IN_RET, 1024), host=plan.host("proj_r"))
    plan.done("proj_r", w)
    (proj_c,) = mm_nn("proj_c", un, w["w_c"], tm)

    def split_proj(r0, pr, pc, cr, sr, cm, ta, tb, qn, kvn):
        rq = _heads(lambda h: _rope_r(_head(pr, h), cr, sr))
        rk = _heads(lambda h: _rope_r(_head(pr, h, off=hw), cr, sr) * RET_SCALE)
        rv = pr[:, 2 * hw:3 * hw]
        cqn = _rms(pc[:, :MLA_Q_RANK], qn)
        ckvn = _rms(pc[:, MLA_Q_RANK:MLA_Q_RANK + MLA_KV_RANK], kvn)
        krr = _rope_m(pc[:, MLA_Q_RANK + MLA_KV_RANK:], cm, ta, tb)
        return [rq, rk, rv, cqn, ckvn, krr], []

    (rq, rk, rv, cqn, ckvn, krr), _ = _rowwise(
        "split_proj", split_proj, [proj_r, proj_c, cos_r, sin_r, cos_m, sa, sb],
        [nw["mla_q_norm"], nw["mla_kv_norm"]],
        [(hw, BF16), (hw, BF16), (hw, BF16), (MLA_Q_RANK, BF16), (MLA_KV_RANK, BF16), (LANES, F32)], [], tm)
    (qp,) = mm_nn("q_up", cqn, w["wuq"], tm)
    (kn,) = mm_nn("k_up", ckvn, w["wuk"], tm)
    (vv,) = mm_nn("v_up", ckvn, w["wuv"], tm, out_dtype=BF16)

    def build_qk(r0, qp, kn, krr, cm, ta, tb):
        qc = jnp.concatenate(
            [part for h in range(HEADS)
             for part in (_head(qp, h, MLA_QK_PAD), _rope_m(_head(qp, h, MLA_QK_PAD, HEAD_DIM), cm, ta, tb))], axis=1)
        kc = jnp.concatenate([part for h in range(HEADS) for part in (_head(kn, h), krr)], axis=1)
        return [qc, kc], []

    (qc, kc), _ = _rowwise("build_qk", build_qk, [qp, kn, krr, cos_m, sa, sb], [], [(qw, BF16), (qw, BF16)], [], tm)
    o_m, lse, o_r, ret_states = mix_fwd("mix_fwd", qc, kc, vv, rq, rk, rv, blk, MLA_SCALE, host=plan.host("mix_fwd"))
    plan.done("mix_fwd", w)

    def gate_mix(r0, rg, o_r, o_m, gn):
        y, _ = _group_norm(o_r)
        return [jnp.concatenate([_silu(rg) * (y * gn), o_m], axis=1)], []

    (mixcat,), _ = _rowwise("gate_mix", gate_mix, [(proj_r, hw, 3), o_r, o_m], [nw["ret_group_norm"]],
                            [(2 * hw, BF16)], [], tm)
    (mix,) = mm_nn("mix_out", mixcat, w["w_out"], tm, n_block=dt)

    def post_mix(r0, h, m, post, pre):
        h2 = h + _rms(m, post)
        return [h2, _rms(h2, pre)], []

    (h2, n3), _ = _rowwise("ffn2_norm", post_mix, [h1, mix], [nw["mix_post_norm"], nw["ffn2_pre_norm"]],
                           [(d, F32), (d, BF16)], [], tm)
    g2, u2, a2, f2 = _ffn_fwd("ffn2", n3, w, "2", tm, tmk, plan)

    def loss_head(r0, h, f, t, post):
        h3 = h + 0.5 * _rms(f, post)
        row = r0 + lax.broadcasted_iota(jnp.int32, h3.shape, 0)
        err = jnp.where(row >= N_META, jnp.where(row < t_real, h3 - t, 0.0), 0.0)
        dh3 = err / d
        df, dpost = _rms_bwd(f, post, 0.5 * dh3)
        return [dh3, df], [_colsum(err * err), _colsum(dpost)]

    (dh3, df2), (loss_vec, d_post2) = _rowwise("loss_head", loss_head, [h2, f2, tgt_p], [nw["ffn2_post_norm"]],
                                               [(d, F32), (d, BF16)], [d, d], tm)
    loss = 0.5 * jnp.sum(loss_vec) / d
    dn3 = _ffn_bwd("ffn2", df2, n3, g2, u2, a2, w["wg2"], w["wu2"], w["wd2"], tmt, tmk, tk, dt, plan)

    def back_mix_norm(r0, h, m, dh3, dn, pre, post):
        dx, dpre = _rms_bwd(h, pre, dn)
        dh2 = dh3 + dx
        dm, dpost = _rms_bwd(m, post, dh2)
        return [dh2, dm], [_colsum(dpre), _colsum(dpost)]

    (dh2, dmix), (d_pre2, d_mix_post) = _rowwise(
        "back_mix_norm", back_mix_norm, [h2, mix, dh3, dn3], [nw["ffn2_pre_norm"], nw["mix_post_norm"]],
        [(d, F32), (d, BF16)], [d, d], tm)
    (dmixcat,) = mm_nt("mix_dx", [dmix], [w["w_out"]], tmt, _tile(2 * hw, 512), outs=(BF16,))
    plan.grads(dict(w_out=mm_tn("mix_dw", mixcat, dmix, 2 * hw // N_CHIPS, dt, tk, GRAD_DT, shard_rows=True)))

    def back_gate(r0, dmc, rg, o_r, gn):
        d_ret, d_om = dmc[:, :hw], dmc[:, hw:]
        yh, rs = _group_norm(o_r)
        d_rg = d_ret * (yh * gn) * _dsilu(rg)
        dy = d_ret * _silu(rg)
        gyh = dy * gn
        d_or = _heads(lambda h: rs[h] * (_head(gyh, h) - _head(yh, h) * jnp.mean(_head(gyh, h) * _head(yh, h),
                                                                                    axis=-1, keepdims=True)))
        return [d_or, d_rg, d_om], [_colsum(dy * yh)]

    (d_or, d_rg, d_om), (d_gn,) = _rowwise("back_gate", back_gate, [dmixcat, (proj_r, hw, 3), o_r],
                                           [nw["ret_group_norm"]], [(hw, BF16), (hw, F32), (hw, BF16)], [hw], tm)
    dqc, dkc, dvv = attn_bwd("mla_bwd", qc, kc, vv, d_om, o_m, lse, blk, MLA_SCALE, host=plan.host("mla_bwd"))
    plan.done("mla_bwd", None)
    drq, drk, drv = ret_bwd("ret_bwd", rq, rk, rv, d_or, ret_states, blk)

    def back_qk(r0, dqc, dkc, dvv, cm, ta, tb):
        dkc = dkc.astype(F32)
        dqp = jnp.concatenate(
            [part for h in range(HEADS)
             for part in (_head(dqc, h, MLA_QK_PAD), _rope_m_t(_head(dqc, h, MLA_QK_PAD, HEAD_DIM), cm, ta, tb))],
            axis=1)
        dkn = _heads(lambda h: _head(dkc, h, MLA_QK_PAD))
        dkr = _head(dkc, 0, MLA_QK_PAD, HEAD_DIM)
        for h in range(1, HEADS):
            dkr = dkr + _head(dkc, h, MLA_QK_PAD, HEAD_DIM)
        return [dqp, dkn, _rope_m_t(dkr, cm, ta, tb), dvv], []

    (dqp, dkn, dkr, dvb), _ = _rowwise("back_qk", back_qk, [dqc, dkc, dvv, cos_m, sa, sb], [],
                                       [(qw, BF16), (hw, BF16), (LANES, F32), (hw, BF16)], [], tm)
    (dcqn,) = mm_nt("q_dx", [dqp], [w["wuq"]], tm, MLA_Q_RANK)
    dwuq = mm_tn("q_dw", cqn, dqp, MLA_Q_RANK, _tile(qw, 1024), tk, GRAD_DT)
    (dckvn,) = mm_nt("kv_dx", [dkn, dvb], [w["wuk"], w["wuv"]], tm, MLA_KV_RANK)
    dwuk = mm_tn("k_dw", ckvn, dkn, MLA_KV_RANK, hw, tk, GRAD_DT)
    dwuv = mm_tn("v_dw", ckvn, dvb, MLA_KV_RANK, hw, tk, GRAD_DT)

    def back_proj(r0, drq, drk, drv, d_rg, pc, dcqn, dckvn, dkr, cr, sr, qn, kvn):
        d_q = _heads(lambda h: _rope_r_t(_head(drq, h), cr, sr))
        d_k = _heads(lambda h: _rope_r_t(_head(drk, h), cr, sr) * RET_SCALE)
        dcq, a_q = _rms_bwd(pc[:, :MLA_Q_RANK], qn, dcqn)
        dckv, a_kv = _rms_bwd(pc[:, MLA_Q_RANK:MLA_Q_RANK + MLA_KV_RANK], kvn, dckvn)
        return ([jnp.concatenate([d_q, d_k, drv, d_rg], axis=1), jnp.concatenate([dcq, dckv, dkr], axis=1)],
                [_colsum(a_q), _colsum(a_kv)])

    (dproj_r, dproj_c), (d_qn, d_kvn) = _rowwise(
        "back_proj", back_proj, [drq, drk, drv, d_rg, proj_c, dcqn, dckvn, dkr, cos_r, sin_r],
        [nw["mla_q_norm"], nw["mla_kv_norm"]], [(IN_RET, BF16), (IN_MLA_PAD, BF16)],
        [MLA_Q_RANK, MLA_KV_RANK], tm)
    (dun,) = mm_nt("proj_dx", [dproj_r, dproj_c], [w["w_r"], w["w_c"]], tmt, _tile(d, 512), outs=(BF16,))
    dw_r = mm_tn("proj_dw_r", un, dproj_r, dt, _tile(IN_RET, 1024), tk, GRAD_DT)
    dw_c = mm_tn("proj_dw_c", un, dproj_c, dt, IN_MLA_PAD, tk, GRAD_DT)
    plan.grads(dict(w_r=dw_r, w_c=dw_c, wuq=dwuq, wuk=dwuk, wuv=dwuv))

    def back_ffn1_norm(r0, h, f, dh2, dn, pre, post):
        dx, dpre = _rms_bwd(h, pre, dn)
        dh1 = dh2 + dx
        df, dpost = _rms_bwd(f, post, 0.5 * dh1)
        return [dh1, df], [_colsum(dpre), _colsum(dpost)]

    (dh1, df1), (d_mix_pre, d_post1) = _rowwise(
        "back_ffn1_norm", back_ffn1_norm, [h1, f1, dh2, dun], [nw["mix_pre_norm"], nw["ffn1_post_norm"]],
        [(d, F32), (d, BF16)], [d, d], tm)
    dn1 = _ffn_bwd("ffn1", df1, n1, g1, u1, a1, w["wg1"], w["wu1"], w["wd1"], tmt, tmk, tk, dt, plan)

    def back_input(r0, h, dh1, dn, pre):
        dx, dpre = _rms_bwd(h, pre, dn)
        return [dh1 + dx], [_colsum(dpre)]

    (dh0,), (d_pre1,) = _rowwise("back_input", back_input, [h0, dh1, dn1], [nw["ffn1_pre_norm"]], [(d, F32)], [d], tm)

    small = dict(ffn1_pre_norm=d_pre1, ffn1_post_norm=d_post1, mix_pre_norm=d_mix_pre, ret_group_norm=d_gn,
                 mla_q_norm=d_qn, mla_kv_norm=d_kvn, mix_post_norm=d_mix_post, ffn2_pre_norm=d_pre2,
                 ffn2_post_norm=d_post2)
    return loss, dh0[N_META:t_real], small, dh0[:N_META]


WEIGHTS = ("meta_tokens", "ffn1_pre_norm", "ffn1_w_gate", "ffn1_w_up", "ffn1_w_down", "ffn1_post_norm",
           "mix_pre_norm", "w_in", "ret_group_norm", "mla_q_norm", "mla_w_uq", "mla_kv_norm", "mla_w_uk",
           "mla_w_uv", "w_out", "mix_post_norm", "ffn2_pre_norm", "ffn2_w_gate", "ffn2_w_up", "ffn2_w_down",
           "ffn2_post_norm")
BIG = ("ffn1_w_gate", "ffn1_w_up", "ffn1_w_down", "w_in", "mla_w_uq", "mla_w_uk", "mla_w_uv", "w_out",
       "ffn2_w_gate", "ffn2_w_up", "ffn2_w_down")
NORMS = ("ffn1_pre_norm", "ffn1_post_norm", "mix_pre_norm", "ret_group_norm", "mla_q_norm", "mla_kv_norm",
         "mix_post_norm", "ffn2_pre_norm", "ffn2_post_norm")


def _unshard_cols(g):
    return g.transpose(1, 0, 2).reshape(g.shape[1], -1)


def _shard_cols(a):
    return a.reshape(a.shape[0], N_CHIPS, -1).transpose(1, 0, 2)


def _pack_rows(rows, width):
    rows = [jnp.pad(r, ((0, 0), (0, width - r.shape[1]))) for r in rows]
    n = sum(r.shape[0] for r in rows)
    return jnp.pad(jnp.concatenate(rows, axis=0), ((0, -n % 8), (0, 0)))


def _weight_views(full):
    w = {}
    for n, g in full.items():
        if n == "w_in":
            w_in = _unshard_cols(g)
            w["w_r"] = w_in[:, :IN_RET]
            w["w_c"] = jnp.pad(w_in[:, IN_RET:], ((0, 0), (0, IN_MLA_PAD - IN_MLA)))
        elif n == "mla_w_uq":
            q = _unshard_cols(g).reshape(MLA_Q_RANK, HEADS, HEAD_DIM + MLA_ROPE)
            q = jnp.pad(q, ((0, 0), (0, 0), (0, MLA_QK_PAD - HEAD_DIM - MLA_ROPE)))
            w["wuq"] = q.reshape(MLA_Q_RANK, HEADS * MLA_QK_PAD)
        elif n in ("mla_w_uk", "mla_w_uv"):
            w["wu" + n[-1]] = _unshard_cols(g)
        elif n == "w_out":
            w["w_out"] = g.reshape(-1, g.shape[2])
        else:
            w["w" + n[7] + n[3]] = g
    return w


def _contributions(g):
    ffn = {"g": "gate", "u": "up", "d": "down"}
    c = {f"ffn{n[2]}_w_{ffn[n[1]]}": a for n, a in g.items() if len(n) == 3 and n[2] in "12"}
    if "w_out" in g:
        c["w_out"] = g["w_out"]
    if "w_r" in g:
        dwuq = g["wuq"].reshape(MLA_Q_RANK, HEADS, MLA_QK_PAD)[:, :, :HEAD_DIM + MLA_ROPE]
        c.update(w_in=_shard_cols(jnp.concatenate([g["w_r"], g["w_c"][:, :IN_MLA]], axis=1)),
                 mla_w_uq=_shard_cols(dwuq.reshape(MLA_Q_RANK, -1)), mla_w_uk=_shard_cols(g["wuk"]),
                 mla_w_uv=_shard_cols(g["wuv"]))
    return c


class _Schedule(_Alone):
    FIRST = ("ffn1_w_gate",)
    CARRIED = {"ffn1_gate": ("ffn1_w_up",), "ffn1_up": ("ffn1_w_down",),
               "ffn1_down": ("w_in", "mla_w_uq", "mla_w_uk", "mla_w_uv"),
               "proj_r": ("w_out",), "mix_fwd": ("ffn2_w_gate", "ffn2_w_up"), "ffn2_up": ("ffn2_w_down",)}
    GRAD_HOST = dict(ffn2_w_gate="mla_bwd", ffn2_w_up="mla_bwd", ffn2_w_down="ffn2_dn",
                     w_out="mla_bwd", w_in="ffn1_da", mla_w_uq="ffn1_da", mla_w_uk="ffn1_da", mla_w_uv="ffn1_da",
                     ffn1_w_gate="ffn1_dwu", ffn1_w_up="ffn1_dn", ffn1_w_down="ffn1_dn")

    def __init__(self, bufs):
        self.gathers = {k: (gather_halves([bufs[n] for n in names]), names) for k, names in self.CARRIED.items()}
        self.waiting = {}
        self.exchanges = {}
        self.grad = {}

    def host(self, kernel_name):
        if kernel_name in self.gathers:
            return self.gathers[kernel_name][0]
        if kernel_name in self.waiting:
            names, sums = zip(*self.waiting.pop(kernel_name))
            self.exchanges[kernel_name] = (chip_exchange(list(sums)), names, sums)
            return self.exchanges[kernel_name][0]
        return None

    def done(self, kernel_name, w):
        if kernel_name in self.gathers:
            comm, names = self.gathers[kernel_name]
            w.update(_weight_views({n: sibling_fill("fill_" + n, b) for n, b in zip(names, comm.result)}))
        elif kernel_name in self.exchanges:
            comm, names, sums = self.exchanges[kernel_name]
            for n, q, mine in zip(names, comm.result, sums):
                self.grad[n] = reduce_join("reduce_join_" + n, q, mine)

    def grads(self, g):
        for n, a in _contributions(g).items():
            self.waiting.setdefault(self.GRAD_HOST[n], []).append((n, pair_add("pair_add_" + n, a)))


def _step(p, m, v, x, loss_target):
    d = x.shape[2]
    names = list(BIG)
    bufs = {n: own_slot(p[n][0].astype(BF16)) for n in names}
    first = _Schedule.FIRST
    gathered = gather_halves([bufs[n] for n in first] + [own_slot(p["meta_tokens"])]).run("gather_first")
    filled = [sibling_fill("fill_" + n, b) for n, b in zip(first + ("meta_tokens",), gathered)]
    w = _weight_views(dict(zip(first, filled[:-1])))
    meta = _unshard_cols(filled[-1])
    nw = {n: p[n] for n in NORMS}
    plan = _Schedule(bufs)
    loss, grad_x, small, d_meta = _local_step(x[0], loss_target[0], meta, w, nw, plan)
    grads = dict(plan.grad)

    width = max(d, HEADS * HEAD_DIM)
    packed = _pack_rows([small[n] for n in NORMS] + [d_meta], width)
    total = sum_slots("small_sum", all_gather_devices(packed), F32)
    for i, n in enumerate(NORMS):
        grads[n] = total[i:i + 1, :p[n].shape[1]]
    cols = p["meta_tokens"].shape[1]
    chip = 2 * lax.axis_index("x") + lax.axis_index("y")
    grads["meta_tokens"] = lax.dynamic_slice(total[len(NORMS):len(NORMS) + N_META, :d], (0, chip * cols), (N_META, cols))

    delta, new_m, new_v = {}, {}, {}
    for n in names + ["meta_tokens"]:
        shape = p[n].shape
        flat = lambda a: a.reshape(-1, shape[-1])
        out = adamw("adamw_" + n, flat(p[n]), flat(grads[n]), flat(m[n]), flat(v[n]))
        delta[n], new_m[n], new_v[n], grads[n] = (o.reshape(shape) for o in out)
    pk = lambda src: _pack_rows([src[n] for n in NORMS], width)
    out = adamw("adamw_norms", pk(p), pk(grads), pk(m), pk(v))
    for i, n in enumerate(NORMS):
        delta[n], new_m[n], new_v[n] = (o[i:i + 1, :p[n].shape[1]] for o in out[:3])

    loss = lax.psum(loss, ("x", "y", "c"))
    return loss, grad_x[None], grads, delta, new_m, new_v


def kernel(x, meta_tokens, ffn1_pre_norm, ffn1_w_gate, ffn1_w_up, ffn1_w_down, ffn1_post_norm, mix_pre_norm, w_in, ret_group_norm, mla_q_norm, mla_w_uq, mla_kv_norm, mla_w_uk, mla_w_uv, w_out, mix_post_norm, ffn2_pre_norm, ffn2_w_gate, ffn2_w_up, ffn2_w_down, ffn2_post_norm, loss_target, m_meta_tokens, m_ffn1_pre_norm, m_ffn1_w_gate, m_ffn1_w_up, m_ffn1_w_down, m_ffn1_post_norm, m_mix_pre_norm, m_w_in, m_ret_group_norm, m_mla_q_norm, m_mla_w_uq, m_mla_kv_norm, m_mla_w_uk, m_mla_w_uv, m_w_out, m_mix_post_norm, m_ffn2_pre_norm, m_ffn2_w_gate, m_ffn2_w_up, m_ffn2_w_down, m_ffn2_post_norm, v_meta_tokens, v_ffn1_pre_norm, v_ffn1_w_gate, v_ffn1_w_up, v_ffn1_w_down, v_ffn1_post_norm, v_mix_pre_norm, v_w_in, v_ret_group_norm, v_mla_q_norm, v_mla_w_uq, v_mla_kv_norm, v_mla_w_uk, v_mla_w_uv, v_w_out, v_mix_post_norm, v_ffn2_pre_norm, v_ffn2_w_gate, v_ffn2_w_up, v_ffn2_w_down, v_ffn2_post_norm):
    args = locals()
    p = {n: args[n] for n in WEIGHTS}
    m = {n: args["m_" + n] for n in WEIGHTS}
    v = {n: args["v_" + n] for n in WEIGHTS}
    loss, grad_x, grads, delta, new_m, new_v = _step(p, m, v, x, loss_target)
    return (loss, grad_x, *[grads[n] for n in WEIGHTS], *[delta[n] for n in WEIGHTS],
            *[new_m[n] for n in WEIGHTS], *[new_v[n] for n in WEIGHTS])
```

```python
import functools
import math

import jax
import jax.numpy as jnp
import numpy as np
from jax import lax
from jax.experimental import pallas as pl
from jax.experimental.pallas import tpu as pltpu

F32 = jnp.float32
BF16 = jnp.bfloat16

EPS = 1e-6
N_META = 16
HEADS = 8
HEAD_DIM = 128
MLA_ROPE = 64
MLA_QK_PAD = 256
MLA_Q_RANK = 512
MLA_KV_RANK = 256
ROPE_THETA = 10000.0
N_CHIPS = 4
LANES = 128
VMEM_LIMIT = 60 * 2 ** 20

ADAM_LR = 0.001
ADAM_B1 = 0.9
ADAM_B2 = 0.999
ADAM_EPS = 1e-08
ADAM_WD = 0.01
ADAM_STEP = 10

NN = (((1,), (0,)), ((), ()))
NT = (((1,), (1,)), ((), ()))
TN = (((0,), (0,)), ((), ()))
MESH = pl.DeviceIdType.MESH


def _tile(n, pref, align=LANES):
    if n <= pref:
        return n
    best = 0
    for t in range(align, pref + 1, align):
        if n % t == 0:
            best = t
    assert best, (n, pref)
    return best


def _params(*sem, collective_id=None):
    return pltpu.CompilerParams(dimension_semantics=sem, vmem_limit_bytes=VMEM_LIMIT, collective_id=collective_id)


SIBLING_BARRIER = 0


def _sds(shape, dtype):
    return jax.ShapeDtypeStruct(tuple(shape), dtype)


def _rowwise(name, fn, rows, consts, outs, accs, tr):
    rows = [r if isinstance(r, tuple) else (r, r.shape[1], 0) for r in rows]
    t = rows[0][0].shape[0]
    assert t % tr == 0
    n_r, n_c, n_o = len(rows), len(consts), len(outs)

    def body(*refs):
        i = pl.program_id(0)
        r = [x[...] for x in refs[:n_r]]
        c = [x[...] for x in refs[n_r:n_r + n_c]]
        o_refs = refs[n_r + n_c:n_r + n_c + n_o]
        a_refs = refs[n_r + n_c + n_o:]
        o_vals, a_vals = fn(i * tr, *r, *c)
        for ref, v in zip(o_refs, o_vals):
            ref[...] = v.astype(ref.dtype)
        if a_refs:
            @pl.when(i == 0)
            def _():
                for ref, v in zip(a_refs, a_vals):
                    ref[...] = v

            @pl.when(i > 0)
            def _():
                for ref, v in zip(a_refs, a_vals):
                    ref[...] += v

    in_specs = [pl.BlockSpec((tr, w), functools.partial(lambda cb, i: (i, cb), cb)) for _, w, cb in rows]
    in_specs += [pl.BlockSpec(a.shape, lambda i: (0, 0)) for a in consts]
    out_specs = [pl.BlockSpec((tr, w), lambda i: (i, 0)) for w, _ in outs]
    out_specs += [pl.BlockSpec((1, w), lambda i: (0, 0)) for w in accs]
    out_shape = [_sds((t, w), dt) for w, dt in outs] + [_sds((1, w), F32) for w in accs]
    res = pl.pallas_call(
        body, name=name, grid=(t // tr,), in_specs=in_specs, out_specs=out_specs, out_shape=out_shape,
        compiler_params=_params("arbitrary"),
    )(*[a for a, _, _ in rows], *consts)
    return res[:n_o], res[n_o:]


def _rms(x, w):
    r = lax.rsqrt(jnp.mean(x * x, axis=-1, keepdims=True) + EPS)
    return x * r * w


def _rms_bwd(x, w, dy):
    r = lax.rsqrt(jnp.mean(x * x, axis=-1, keepdims=True) + EPS)
    xh = x * r
    gy = dy * w
    dx = r * (gy - xh * jnp.mean(gy * xh, axis=-1, keepdims=True))
    return dx, dy * xh


def _colsum(v):
    return jnp.sum(v, axis=0, keepdims=True)


def _silu(x):
    return x * jax.nn.sigmoid(x)


def _dsilu(x):
    s = jax.nn.sigmoid(x)
    return s * (1.0 + x * (1.0 - s))


def _rope_r(x, cos, sin):
    return x * cos + pltpu.roll(x, 64, 1) * sin


def _rope_r_t(dy, cos, sin):
    return dy * cos + pltpu.roll(dy * sin, 64, 1)


def _rope_m(x, cos, sa, sb):
    return x * cos + pltpu.roll(x, 32, 1) * sa + pltpu.roll(x, 96, 1) * sb


def _rope_m_t(dy, cos, sa, sb):
    return dy * cos + pltpu.roll(dy * sa, 96, 1) + pltpu.roll(dy * sb, 32, 1)


def _rope_tables(t):
    pos = np.arange(t, dtype=np.float32)

    def cs(dim):
        inv = np.float32(ROPE_THETA) ** (-np.arange(0, dim, 2, dtype=np.float32) / np.float32(dim))
        ang = pos[:, None] * inv[None, :]
        return np.cos(ang), np.sin(ang)

    c, s = cs(HEAD_DIM)
    cos_r = np.concatenate([c, c], axis=1)
    sin_r = np.concatenate([-s, s], axis=1)
    c, s = cs(MLA_ROPE)
    z32, z64 = np.zeros_like(s), np.zeros((t, 64), np.float32)
    cos_m = np.concatenate([c, c, z64], axis=1)
    sa = np.concatenate([z32, s, z64], axis=1)
    sb = np.concatenate([-s, z32, z64], axis=1)
    return tuple(jnp.asarray(a, F32) for a in (cos_r, sin_r, cos_m, sa, sb))


def _call(body, name, grid, in_specs, out_specs, out_shape, scratch, operands, host=None):
    sem = ("arbitrary",) * len(grid)
    if host is None:
        return pl.pallas_call(body, name=name, grid=grid, in_specs=in_specs, out_specs=out_specs, out_shape=out_shape,
                              scratch_shapes=scratch, compiler_params=_params(*sem))(*operands)
    n_in, n_out, n_s = len(in_specs), len(out_shape), len(scratch)
    h_in, h_out = len(host.ins), len(host.out_shape)

    def hosted(*refs):
        a = n_in
        b = a + h_in
        c = b + n_out
        d = c + h_out
        e = d + n_s
        ids = [pl.program_id(i) for i in range(len(grid))]
        first = functools.reduce(jnp.logical_and, [i == 0 for i in ids])
        last = functools.reduce(jnp.logical_and, [i == g - 1 for i, g in zip(ids, grid)])
        comm = (refs[a:b], refs[c:d], refs[e:])

        @pl.when(first)
        def _():
            host.start(*comm)

        body(*refs[:a], *refs[b:c], *refs[d:e])

        @pl.when(last)
        def _():
            host.finish(*comm)

    hbm = pl.BlockSpec(memory_space=pl.ANY)
    res = pl.pallas_call(
        hosted, name=name, grid=grid, in_specs=list(in_specs) + [hbm] * h_in, out_specs=list(out_specs) + [hbm] * h_out,
        out_shape=list(out_shape) + list(host.out_shape), scratch_shapes=list(scratch) + host.scratch(),
        input_output_aliases=host.aliases(n_in, n_out),
        compiler_params=_params(*sem, collective_id=host.collective_id))(*operands, *host.ins)
    host.result = res[n_out:]
    return res[:n_out]


def _mm(name, grid, operands, in_specs, dns, out_specs, out_shape, epilogue=None, extras=(), extra_specs=(),
        acc_shape=None, host=None):
    n_p, n_e = len(dns), len(extras)
    nk = grid[2]
    n_o = len(out_shape)
    in_place = nk > 1 and epilogue is None and n_o == 1 and out_shape[0].dtype == F32

    def body(*refs):
        ab = refs[:2 * n_p]
        ex = refs[2 * n_p:2 * n_p + n_e]
        outs = refs[2 * n_p + n_e:2 * n_p + n_e + n_o]

        part = None
        for p in range(n_p):
            d = lax.dot_general(ab[2 * p][...], ab[2 * p + 1][...], dns[p], preferred_element_type=F32)
            part = d if part is None else part + d

        def finish(acc):
            vals = (acc,) if epilogue is None else epilogue(acc, *[e[...] for e in ex])
            for o, v in zip(outs, vals):
                o[...] = v.astype(o.dtype)

        if nk == 1:
            finish(part)
        else:
            acc_ref = outs[0] if in_place else refs[2 * n_p + n_e + n_o]
            k = pl.program_id(2)

            @pl.when(k == 0)
            def _():
                acc_ref[...] = part

            @pl.when(k > 0)
            def _():
                acc_ref[...] += part

            if not in_place:
                @pl.when(k == nk - 1)
                def _():
                    finish(acc_ref[...])

    scratch = [] if nk == 1 or in_place else [pltpu.VMEM(acc_shape, F32)]
    return _call(body, name, grid, list(in_specs) + list(extra_specs), out_specs, out_shape, scratch,
                 [*operands, *extras], host)


def mm_nn(name, x, w, tm, out_dtype=F32, epilogue=None, outs=None, w2=None, n_block=None, extras=(), host=None):
    t, kdim = x.shape
    if w.ndim == 3:
        s, _, ns = w.shape
        n, tn, nb = s * ns, ns, s
        wspec = pl.BlockSpec((None, kdim, ns), lambda j, i, k: (j, 0, 0))
    else:
        n = w.shape[1]
        tn = n_block or n
        nb = n // tn
        wspec = pl.BlockSpec((kdim, tn), lambda j, i, k: (0, j))
    xspec = pl.BlockSpec((tm, kdim), lambda j, i, k: (i, 0))
    ospec = pl.BlockSpec((tm, tn), lambda j, i, k: (i, j))
    outs = outs or [out_dtype]
    grid = (nb, t // tm, 1)
    if w2 is None:
        return _mm(name, grid, [x, w], [xspec, wspec], [NN], [ospec] * len(outs), [_sds((t, n), d) for d in outs],
                   epilogue=epilogue, extras=extras, extra_specs=[ospec] * len(extras), host=host)

    def body(x_ref, w_ref, w2_ref, *o_refs):
        xv = x_ref[...]
        a = jnp.dot(xv, w_ref[...], preferred_element_type=F32)
        b = jnp.dot(xv, w2_ref[...], preferred_element_type=F32)
        for o, v in zip(o_refs, epilogue(a, b)):
            o[...] = v.astype(o.dtype)

    return _call(body, name, grid[:2],
                 [pl.BlockSpec((tm, kdim), lambda j, i: (i, 0)),
                  pl.BlockSpec((None, kdim, tn), lambda j, i: (j, 0, 0)),
                  pl.BlockSpec((None, kdim, tn), lambda j, i: (j, 0, 0))],
                 [pl.BlockSpec((tm, tn), lambda j, i: (i, j))] * len(outs), [_sds((t, n), d) for d in outs], [],
                 [x, w, w2], host)


def mm_nn_k(name, x, w, tm, tk, out_dtype=F32, host=None):
    t, kdim = x.shape
    n = w.shape[1]
    grid = (t // tm, 1, kdim // tk)
    return _mm(name, grid, [x, w],
               [pl.BlockSpec((tm, tk), lambda i, j, k: (i, k)), pl.BlockSpec((tk, n), lambda i, j, k: (k, 0))],
               [NN], [pl.BlockSpec((tm, n), lambda i, j, k: (i, 0))], [_sds((t, n), out_dtype)],
               acc_shape=(tm, n), host=host)[0]


def mm_nt(name, xs, ws, tm, tn, outs=(F32,), epilogue=None, extras=(), host=None):
    t = xs[0].shape[0]
    specs, ops = [], []
    for x, w in zip(xs, ws):
        kdim = x.shape[1]
        specs.append(pl.BlockSpec((tm, kdim), lambda j, i, k: (i, 0)))
        if w.ndim == 3:
            assert tn == w.shape[1]
            n = w.shape[0] * w.shape[1]
            specs.append(pl.BlockSpec((None, tn, kdim), lambda j, i, k: (j, 0, 0)))
        else:
            n = w.shape[0]
            specs.append(pl.BlockSpec((tn, kdim), lambda j, i, k: (j, 0)))
        ops += [x, w]
    ospec = pl.BlockSpec((tm, tn), lambda j, i, k: (i, j))
    return _mm(name, (n // tn, t // tm, 1), ops, specs, [NT] * len(xs), [ospec] * len(outs),
               [_sds((t, n), d) for d in outs], epilogue=epilogue, extras=extras,
               extra_specs=[ospec] * len(extras), host=host)


def mm_nt_k(name, xs, ws, tm, tn, out_dtype=F32, host=None):
    t = xs[0].shape[0]
    s, n, ns = ws[0].shape
    specs, ops = [], []
    for x, w in zip(xs, ws):
        specs.append(pl.BlockSpec((tm, ns), lambda i, j, k: (i, k)))
        specs.append(pl.BlockSpec((None, tn, ns), lambda i, j, k: (k, j, 0)))
        ops += [x, w]
    return _mm(name, (t // tm, n // tn, s), ops, specs, [NT] * len(xs),
               [pl.BlockSpec((tm, tn), lambda i, j, k: (i, j))], [_sds((t, n), out_dtype)], acc_shape=(tm, tn),
               host=host)[0]


def mm_tn(name, x, y, tm, tn, tk, out_dtype, shard_rows=False, shard_cols=False, host=None):
    t, m = x.shape
    n = y.shape[1]
    grid = (m // tm, n // tn, t // tk)
    if shard_cols:
        ospec = pl.BlockSpec((None, tm, tn), lambda i, j, k: (j, i, 0))
        oshape = _sds((n // tn, m, tn), out_dtype)
    elif shard_rows:
        ospec = pl.BlockSpec((None, tm, tn), lambda i, j, k: (i, 0, j))
        oshape = _sds((m // tm, tm, n), out_dtype)
    else:
        ospec = pl.BlockSpec((tm, tn), lambda i, j, k: (i, j))
        oshape = _sds((m, n), out_dtype)
    return _mm(name, grid, [x, y],
               [pl.BlockSpec((tk, tm), lambda i, j, k: (k, i)), pl.BlockSpec((tk, tn), lambda i, j, k: (k, j))],
               [TN], [ospec], [oshape], acc_shape=(tm, tn), host=host)[0]


def _decay_logs():
    return [math.log(1.0 - 2.0 ** (-5.0 - h)) for h in range(HEADS)]


def _log_decay(h):
    lg = jnp.float32(_decay_logs()[0])
    for i in range(1, HEADS):
        lg = jnp.where(h == i, jnp.float32(_decay_logs()[i]), lg)
    return lg


def _decayed_scores(q, k, lg):
    s = lax.dot_general(q, k, NT, preferred_element_type=F32)
    row = lax.broadcasted_iota(jnp.int32, s.shape, 0)
    col = lax.broadcasted_iota(jnp.int32, s.shape, 1)
    dec = jnp.where(col <= row, jnp.exp(jnp.maximum(row - col, 0).astype(F32) * lg), 0.0)
    return s * dec, dec


def _causal(s):
    row = lax.broadcasted_iota(jnp.int32, s.shape, 0)
    col = lax.broadcasted_iota(jnp.int32, s.shape, 1)
    return jnp.where(col <= row, s, -1e30)


def _attn_fwd_step(qi, blk, scale, q_ref, k_ref, v_ref, o_ref, lse_ref, m_ref, l_ref, acc_ref):
    qv = q_ref[...]
    m_ref[...] = jnp.full_like(m_ref, -1e30)
    l_ref[...] = jnp.zeros_like(l_ref)
    acc_ref[...] = jnp.zeros_like(acc_ref)

    def keys(start, n, diag_from):
        rows = pl.ds(pl.multiple_of(start, blk), n)
        s = lax.dot_general(qv, k_ref[rows, :], NT, preferred_element_type=F32) * scale
        if diag_from is not None:
            row = lax.broadcasted_iota(jnp.int32, s.shape, 0)
            col = lax.broadcasted_iota(jnp.int32, s.shape, 1)
            s = jnp.where(col - diag_from <= row, s, -1e30)
        m = m_ref[...]
        m_new = jnp.maximum(m, jnp.max(s, axis=-1, keepdims=True))
        p = jnp.exp(s - m_new)
        alpha = jnp.exp(m - m_new)
        m_ref[...] = m_new
        l_ref[...] = alpha * l_ref[...] + jnp.sum(p, axis=-1, keepdims=True)
        acc_ref[...] = alpha * acc_ref[...] + jnp.dot(p.astype(BF16), v_ref[rows, :], preferred_element_type=F32)

    @pl.loop(0, qi // 2)
    def _(j):
        keys(j * (2 * blk), 2 * blk, None)

    @pl.when(qi % 2 == 1)
    def _():
        keys((qi - 1) * blk, 2 * blk, blk)

    @pl.when(qi % 2 == 0)
    def _():
        keys(qi * blk, blk, 0)

    l = l_ref[...]
    o_ref[...] = acc_ref[...] / l
    lse_ref[...] = jnp.broadcast_to(m_ref[...] + jnp.log(l), (blk, HEAD_DIM))


def mix_fwd(name, q, k, v, rq, rk, rv, blk, scale, host=None):
    t = q.shape[0]
    dq = q.shape[1] // HEADS
    nq = t // blk

    def body(q_ref, k_ref, v_ref, rq_ref, rk_ref, rv_ref, o_ref, lse_ref, ro_ref, st_ref, m_ref, l_ref, acc_ref, state):
        h, i = pl.program_id(0), pl.program_id(1)

        @pl.when(h < HEADS)
        def _():
            _attn_fwd_step(i, blk, scale, q_ref, k_ref, v_ref, o_ref, lse_ref, m_ref, l_ref, acc_ref)

        @pl.when(h >= HEADS)
        def _():
            _ret_fwd_step(h - HEADS, i, blk, rq_ref, rk_ref, rv_ref, ro_ref, st_ref, state)

    mla = lambda w, whole=False: pl.BlockSpec(
        (t if whole else blk, w),
        lambda h, i: (0 if whole else jnp.where(h < HEADS, i, nq - 1), jnp.minimum(h, HEADS - 1)))
    ret = lambda rows: pl.BlockSpec(
        (rows, HEAD_DIM), lambda h, i: (jnp.where(h >= HEADS, i, 0), jnp.maximum(h - HEADS, 0)))
    wide = _sds((t, HEADS * HEAD_DIM), F32)
    return _call(body, name, (2 * HEADS, nq),
                 [mla(dq), mla(dq, True), mla(HEAD_DIM, True), ret(blk), ret(blk), ret(blk)],
                 [mla(HEAD_DIM), mla(HEAD_DIM), ret(blk), ret(HEAD_DIM)],
                 [wide, wide, wide, _sds((nq * HEAD_DIM, HEADS * HEAD_DIM), F32)],
                 [pltpu.VMEM((blk, 1), F32), pltpu.VMEM((blk, 1), F32), pltpu.VMEM((blk, HEAD_DIM), F32),
                  pltpu.VMEM((HEAD_DIM, HEAD_DIM), F32)],
                 [q, k, v, rq, rk, rv], host)


def attn_bwd(name, q, k, v, do, o, lse, blk, scale, host=None):
    t = q.shape[0]
    dq_w = q.shape[1] // HEADS
    nb = t // blk

    def body(q_ref, k_ref, v_ref, do_ref, o_ref, lse_ref, dq_ref, dk_out, dv_out, dk_ref, dv_ref):
        ki = pl.program_id(1)
        kv = k_ref[...]
        vv = v_ref[...]

        @pl.when(ki == 0)
        def _():
            dq_ref[...] = jnp.zeros_like(dq_ref)

        def queries(start, n, diag):
            rows = pl.ds(pl.multiple_of(start, blk), n)
            qv, dov = q_ref[rows, :], do_ref[rows, :]
            s = lax.dot_general(qv, kv, NT, preferred_element_type=F32) * scale
            if diag:
                s = _causal(s)
            p = jnp.exp(s - lse_ref[rows, :][:, :1])
            dp = lax.dot_general(dov, vv, NT, preferred_element_type=F32)
            delta = jnp.sum(dov.astype(F32) * o_ref[rows, :], axis=-1, keepdims=True)
            ds = p * (dp - delta) * scale
            pb, dsb = p.astype(BF16), ds.astype(BF16)
            dv_ref[...] += lax.dot_general(pb, dov, TN, preferred_element_type=F32)
            dk_ref[...] += lax.dot_general(dsb, qv, TN, preferred_element_type=F32)
            dq_ref[rows, :] += jnp.dot(dsb, kv, preferred_element_type=F32)

        dk_ref[...] = jnp.zeros_like(dk_ref)
        dv_ref[...] = jnp.zeros_like(dv_ref)
        queries(ki * blk, blk, True)
        later = nb - 1 - ki

        @pl.when(later % 2 == 1)
        def _():
            queries((ki + 1) * blk, blk, False)

        @pl.loop(0, later // 2)
        def _(j):
            queries((ki + 1 + later % 2 + 2 * j) * blk, 2 * blk, False)

        dk_out[...] = dk_ref[...].astype(dk_out.dtype)
        dv_out[...] = dv_ref[...].astype(dv_out.dtype)

    full = lambda w: pl.BlockSpec((t, w), lambda h, j: (0, h))
    blkd = lambda w: pl.BlockSpec((blk, w), lambda h, j: (j, h))
    return _call(body, name, (HEADS, nb),
                 [full(dq_w), blkd(dq_w), blkd(HEAD_DIM), full(HEAD_DIM), full(HEAD_DIM), full(HEAD_DIM)],
                 [full(dq_w), blkd(dq_w), blkd(HEAD_DIM)],
                 [_sds(q.shape, F32), _sds(k.shape, BF16), _sds(v.shape, BF16)],
                 [pltpu.VMEM((blk, dq_w), F32), pltpu.VMEM((blk, HEAD_DIM), F32)], [q, k, v, do, o, lse], host)


def _chunk_decays(lg, blk):
    row = lax.broadcasted_iota(jnp.int32, (blk, HEAD_DIM), 0).astype(F32)
    return jnp.exp(lg * (row + 1.0)), jnp.exp(lg * (blk - 1.0 - row)), jnp.exp(lg * blk * jnp.ones((1, HEAD_DIM), F32))


def _ret_fwd_step(h, i, blk, q_ref, k_ref, v_ref, o_ref, st_ref, state):
    lg = _log_decay(h)

    @pl.when(i == 0)
    def _():
        state[...] = jnp.zeros_like(state)

    qv, kv, vv = q_ref[...], k_ref[...], v_ref[...]
    before = state[...]
    st_ref[...] = before
    p, _ = _decayed_scores(qv, kv, lg)
    xi, zeta, g_blk = _chunk_decays(lg, blk)
    o_ref[...] = (jnp.dot(p.astype(BF16), vv, preferred_element_type=F32)
                  + jnp.dot(qv, before.astype(BF16), preferred_element_type=F32) * xi)
    kz = (kv.astype(F32) * zeta).astype(BF16)
    state[...] = before * g_blk + lax.dot_general(kz, vv, TN, preferred_element_type=F32)


def ret_bwd(name, q, k, v, do, states, blk, host=None):
    t = q.shape[0]
    nb = t // blk

    def body(q_ref, k_ref, v_ref, do_ref, st_ref, dq_ref, dk_ref, dv_ref, dstate):
        h, i = pl.program_id(0), pl.program_id(1)
        lg = _log_decay(h)

        @pl.when(i == 0)
        def _():
            dstate[...] = jnp.zeros_like(dstate)

        qv, kv, vv, dov = q_ref[...], k_ref[...], v_ref[...], do_ref[...]
        before = st_ref[...].astype(BF16)
        after_grad = dstate[...]
        p, dec = _decayed_scores(qv, kv, lg)
        ds = lax.dot_general(dov, vv, NT, preferred_element_type=F32) * dec
        pb, dsb = p.astype(BF16), ds.astype(BF16)
        xi, zeta, g_blk = _chunk_decays(lg, blk)
        dox = (dov.astype(F32) * xi).astype(BF16)
        kz = (kv.astype(F32) * zeta).astype(BF16)
        agb = after_grad.astype(BF16)
        dv = lax.dot_general(pb, dov, TN, preferred_element_type=F32) + jnp.dot(kz, agb, preferred_element_type=F32)
        dq = (jnp.dot(dsb, kv, preferred_element_type=F32)
              + lax.dot_general(dox, before, NT, preferred_element_type=F32))
        dk = (lax.dot_general(dsb, qv, TN, preferred_element_type=F32)
              + lax.dot_general(vv, agb, NT, preferred_element_type=F32) * zeta)
        for ref, val in ((dq_ref, dq), (dk_ref, dk), (dv_ref, dv)):
            ref[...] = val.astype(ref.dtype)
        dstate[...] = after_grad * g_blk + lax.dot_general(qv, dox, TN, preferred_element_type=F32)

    hspec = pl.BlockSpec((blk, HEAD_DIM), lambda h, i: (nb - 1 - i, h))
    return _call(body, name, (HEADS, nb),
                 [hspec] * 4 + [pl.BlockSpec((HEAD_DIM, HEAD_DIM), lambda h, i: (nb - 1 - i, h))],
                 [hspec] * 3, [_sds(q.shape, BF16)] * 3, [pltpu.VMEM((HEAD_DIM, HEAD_DIM), F32)],
                 [q, k, v, do, states], host)


CHIP_FLIPS = ((1, 0), (0, 1), (1, 1))
CHIP_BARRIER = 1
ALL_BARRIER = 2


def _position():
    return lax.axis_index("x"), lax.axis_index("y"), lax.axis_index("c")


def _chip_peers(pos):
    x, y, c = pos
    return [(x ^ fx, y ^ fy, c) for fx, fy in CHIP_FLIPS]


class _Comm:
    def __init__(self, ins, out_shape, plan, n_remote, n_local, collective_id, in_place=False):
        self.ins, self.out_shape, self.plan = list(ins), list(out_shape), plan
        self.n_remote, self.n_local, self.collective_id = n_remote, n_local, collective_id
        self.in_place = in_place
        self.result = None

    def aliases(self, first_in, first_out):
        return {first_in + i: first_out + i for i in range(len(self.ins))} if self.in_place else {}

    def scratch(self):
        return [pltpu.SemaphoreType.DMA((self.n_remote,)), pltpu.SemaphoreType.DMA((self.n_remote,)),
                pltpu.SemaphoreType.DMA((max(self.n_local, 1),))]

    def _copies(self, in_refs, out_refs, sems):
        send_sems, recv_sems, local_sems = sems
        pos = _position()
        p = self.plan(pos, in_refs, out_refs)

        def remote(k, src, dst, dev):
            return pltpu.make_async_remote_copy(src_ref=src, dst_ref=dst, send_sem=send_sems.at[k],
                                                recv_sem=recv_sems.at[k], device_id=dev, device_id_type=MESH)

        local = [pltpu.make_async_copy(s, d, local_sems.at[i]) for i, (s, d) in enumerate(p["local"])]
        out = [remote(k, s, d, dev) for k, (s, d, dev) in enumerate(p["sends"])]
        arrivals = [functools.partial(remote, k, d, d, pos) for k, d in enumerate(p["recvs"])]
        return local, out, arrivals, p["peers"]

    def start(self, in_refs, out_refs, sems):
        local, out, _, peers = self._copies(in_refs, out_refs, sems)
        barrier = pltpu.get_barrier_semaphore()
        for peer in peers:
            pl.semaphore_signal(barrier, inc=1, device_id=peer, device_id_type=MESH)
        pl.semaphore_wait(barrier, len(peers))
        for cp in local + out:
            cp.start()

    def finish(self, in_refs, out_refs, sems):
        local, out, arrivals, _ = self._copies(in_refs, out_refs, sems)
        for make in arrivals:
            make().wait_recv()
        for cp in out:
            cp.wait_send()
        for cp in local:
            cp.wait()

    def run(self, name):
        n_in, n_out = len(self.ins), len(self.out_shape)

        def body(*refs):
            comm = (refs[:n_in], refs[n_in:n_in + n_out], refs[n_in + n_out:])
            self.start(*comm)
            self.finish(*comm)

        hbm = pl.BlockSpec(memory_space=pl.ANY)
        self.result = pl.pallas_call(
            body, name=name, in_specs=[hbm] * n_in, out_specs=[hbm] * n_out, out_shape=self.out_shape,
            scratch_shapes=self.scratch(), input_output_aliases=self.aliases(0, 0),
            compiler_params=pltpu.CompilerParams(collective_id=self.collective_id))(*self.ins)
        return self.result


def _half_rows(c, rows):
    r2 = rows // 2
    return pl.ds(pl.multiple_of(c * r2, math.gcd(r2, LANES)), r2)


def _half(ref, c, rows, lead=()):
    return ref.at[(*lead, _half_rows(c, rows))]


def own_slot(shard):
    me = 2 * lax.axis_index("x") + lax.axis_index("y")
    return lax.dynamic_update_slice(lax.empty((N_CHIPS, *shard.shape), shard.dtype), shard[None], (me, 0, 0))


def gather_halves(bufs):
    def plan(pos, ins, outs):
        x, y, c = pos
        me = 2 * x + y
        p = dict(local=[], sends=[], recvs=[], peers=_chip_peers(pos))
        for a, dst in enumerate(outs):
            rows = bufs[a].shape[1]
            for px, py, _ in p["peers"]:
                p["sends"].append((_half(dst, c, rows, (me,)), _half(dst, c, rows, (me,)), (px, py, c)))
                p["recvs"].append(_half(dst, c, rows, (2 * px + py,)))
        return p

    return _Comm(bufs, [_sds(b.shape, b.dtype) for b in bufs], plan, n_remote=3 * len(bufs), n_local=0,
                 collective_id=CHIP_BARRIER, in_place=True)


def chip_exchange(parts):
    def plan(pos, ins, outs):
        x, y, c = pos
        me = 2 * x + y
        p = dict(local=[], sends=[], recvs=[], peers=_chip_peers(pos))
        for src, dst in zip(ins, outs):
            for px, py, _ in p["peers"]:
                peer = 2 * px + py
                p["sends"].append((src.at[peer], dst.at[me], (px, py, c)))
                p["recvs"].append(dst.at[peer])
        return p

    return _Comm(parts, [_sds(g.shape, g.dtype) for g in parts], plan, n_remote=3 * len(parts), n_local=0,
                 collective_id=CHIP_BARRIER)


def all_gather_devices(v):
    flips = [(fx, fy, fc) for fx in (0, 1) for fy in (0, 1) for fc in (0, 1)][1:]

    def plan(pos, ins, outs):
        x, y, c = pos
        me = 4 * x + 2 * y + c
        p = dict(local=[(ins[0], outs[0].at[me])], sends=[], recvs=[],
                 peers=[(x ^ fx, y ^ fy, c ^ fc) for fx, fy, fc in flips])
        for px, py, pc in p["peers"]:
            p["sends"].append((ins[0], outs[0].at[me], (px, py, pc)))
            p["recvs"].append(outs[0].at[4 * px + 2 * py + pc])
        return p

    return _Comm([v], [_sds((8, *v.shape), v.dtype)], plan, n_remote=7, n_local=1,
                 collective_id=ALL_BARRIER).run("small_all_gather")[0]


SWAP_CHUNK_BYTES = 3 * 2 ** 19


def _sibling_stream(t, n, value, consume, sbuf, rbuf, send_sems, recv_sems, credits):
    x, y, c = _position()
    sib = (x, y, 1 - c)

    def copy(slot):
        return pltpu.make_async_remote_copy(src_ref=sbuf.at[slot], dst_ref=rbuf.at[slot], send_sem=send_sems.at[slot],
                                            recv_sem=recv_sems.at[slot], device_id=sib, device_id_type=MESH)

    slot = t % 2

    @pl.when(t == 0)
    def _():
        barrier = pltpu.get_barrier_semaphore()
        pl.semaphore_signal(barrier, inc=1, device_id=sib, device_id_type=MESH)
        pl.semaphore_wait(barrier, 1)

    @pl.when(jnp.logical_and(t >= 2, t < n))
    def _():
        copy(slot).wait_send()
        pl.semaphore_wait(credits.at[slot], 1)

    @pl.when(t < n)
    def _():
        sbuf[slot] = value
        copy(slot).start()

    @pl.when(t >= 1)
    def _():
        prev = 1 - slot
        copy(prev).wait_recv()
        consume(sbuf[prev], rbuf[prev])

        @pl.when(t + 1 < n)
        def _():
            pl.semaphore_signal(credits.at[prev], inc=1, device_id=sib, device_id_type=MESH)

    @pl.when(t == n)
    def _():
        copy(1 - slot).wait_send()
        if n > 1:
            copy(slot).wait_send()


def _swap_scratch(rows, cols, dtype):
    return [pltpu.VMEM((2, rows, cols), dtype), pltpu.VMEM((2, rows, cols), dtype),
            pltpu.SemaphoreType.DMA((2,)), pltpu.SemaphoreType.DMA((2,)), pltpu.SemaphoreType.REGULAR((2,))]


def _chunk_rows(rows, cols, dtype, nbytes=SWAP_CHUNK_BYTES):
    return _tile(rows, max(16, nbytes // (cols * jnp.dtype(dtype).itemsize)), 16)


def pair_add(name, g):
    s, r, c_ = g.shape
    r2 = r // 2
    cr = _chunk_rows(r2, c_, g.dtype)
    nj = r2 // cr

    n = s * nj

    def body(core, mine_ref, theirs_ref, o_ref, *scratch):
        def consume(_, got):
            o_ref[...] = (mine_ref[...].astype(F32) + got.astype(F32)).astype(o_ref.dtype)

        _sibling_stream(pl.program_id(0), n, theirs_ref[...], consume, *scratch)

    sent = lambda t: jnp.minimum(t, n - 1)
    used = lambda t: jnp.maximum(t - 1, 0)
    grid_spec = pltpu.PrefetchScalarGridSpec(
        num_scalar_prefetch=1, grid=(n + 1,),
        in_specs=[pl.BlockSpec((None, cr, c_), lambda t, core: (used(t) // nj, core[0] * nj + used(t) % nj, 0)),
                  pl.BlockSpec((None, cr, c_), lambda t, core: (sent(t) // nj, (1 - core[0]) * nj + sent(t) % nj, 0))],
        out_specs=pl.BlockSpec((None, cr, c_), lambda t, core: (used(t) // nj, used(t) % nj, 0)),
        scratch_shapes=_swap_scratch(cr, c_, g.dtype))
    core = lax.axis_index("c").astype(jnp.int32).reshape(1)
    return pl.pallas_call(body, name=name, grid_spec=grid_spec, out_shape=_sds((s, r2, c_), g.dtype),
                          compiler_params=_params("arbitrary", collective_id=SIBLING_BARRIER))(core, g, g)


def _adamw_math(w, g, m, v):
    m = ADAM_B1 * m + (1.0 - ADAM_B1) * g
    v = ADAM_B2 * v + (1.0 - ADAM_B2) * (g * g)
    m_hat = m / (1.0 - ADAM_B1 ** ADAM_STEP)
    v_hat = v / (1.0 - ADAM_B2 ** ADAM_STEP)
    return -ADAM_LR * (m_hat / (jnp.sqrt(v_hat) + ADAM_EPS) + ADAM_WD * w), m, v


def reduce_join(name, p, mine):
    s, r2, c_ = p.shape
    cr = _chunk_rows(r2, c_, F32)
    nj = r2 // cr

    def body(where, p_ref, mine_ref, o_ref, acc_ref, *scratch):
        c, me = where[0], where[1]
        for chip in range(s):
            @pl.when(me == chip)
            def _():
                terms = [mine_ref[...] if i == chip else p_ref[i] for i in range(s)]
                acc = terms[0].astype(F32)
                for term in terms[1:]:
                    acc = acc + term.astype(F32)
                acc_ref[...] = acc

        def consume(own, got):
            o_ref[c] = own
            o_ref[1 - c] = got

        _sibling_stream(pl.program_id(0), nj, acc_ref[...], consume, *scratch)

    sent = lambda t: jnp.minimum(t, nj - 1)
    grid_spec = pltpu.PrefetchScalarGridSpec(
        num_scalar_prefetch=1, grid=(nj + 1,),
        in_specs=[pl.BlockSpec((s, cr, c_), lambda t, where: (0, sent(t), 0)),
                  pl.BlockSpec((None, cr, c_), lambda t, where: (where[1], sent(t), 0))],
        out_specs=pl.BlockSpec((2, cr, c_), lambda t, where: (0, jnp.maximum(t - 1, 0), 0)),
        scratch_shapes=[pltpu.VMEM((cr, c_), F32)] + _swap_scratch(cr, c_, F32))
    x, y, c = _position()
    where = jnp.stack([c, 2 * x + y]).astype(jnp.int32)
    out = pl.pallas_call(body, name=name, grid_spec=grid_spec, out_shape=_sds((2, r2, c_), F32),
                         compiler_params=_params("arbitrary", collective_id=SIBLING_BARRIER))(where, p, mine)
    return out.reshape(2 * r2, c_)


def sibling_fill(name, buf):
    s, r, c_ = buf.shape
    r2 = r // 2
    cr = _chunk_rows(r2, c_, buf.dtype)
    nj = r2 // cr
    n_peers = len(CHIP_FLIPS)
    n = n_peers * nj

    def body(where, in_ref, o_ref, *scratch):
        def consume(_, got):
            o_ref[...] = got

        _sibling_stream(pl.program_id(0), n, in_ref[...], consume, *scratch)

    sent = lambda t: jnp.minimum(t, n - 1)
    used = lambda t: jnp.maximum(t - 1, 0)
    grid_spec = pltpu.PrefetchScalarGridSpec(
        num_scalar_prefetch=1, grid=(n + 1,),
        in_specs=[pl.BlockSpec((None, cr, c_),
                               lambda t, where: (where[sent(t) // nj], where[n_peers] * nj + sent(t) % nj, 0))],
        out_specs=pl.BlockSpec((None, cr, c_),
                               lambda t, where: (where[used(t) // nj], (1 - where[n_peers]) * nj + used(t) % nj, 0)),
        scratch_shapes=_swap_scratch(cr, c_, buf.dtype))
    x, y, c = _position()
    where = jnp.stack([2 * (x ^ fx) + (y ^ fy) for fx, fy in CHIP_FLIPS] + [c]).astype(jnp.int32)
    return pl.pallas_call(body, name=name, grid_spec=grid_spec, out_shape=_sds(buf.shape, buf.dtype),
                          input_output_aliases={1: 0},
                          compiler_params=_params("arbitrary", collective_id=SIBLING_BARRIER))(where, buf)


def sum_slots(name, p, out_dtype):
    s, r, c = p.shape
    tr = _tile(r, 256, 16)

    def body(p_ref, o_ref):
        acc = p_ref[0].astype(F32)
        for i in range(1, s):
            acc = acc + p_ref[i].astype(F32)
        o_ref[...] = acc.astype(o_ref.dtype)

    return pl.pallas_call(
        body, name=name, grid=(r // tr,), in_specs=[pl.BlockSpec((s, tr, c), lambda i: (0, i, 0))],
        out_specs=pl.BlockSpec((tr, c), lambda i: (i, 0)), out_shape=_sds((r, c), out_dtype),
        compiler_params=_params("arbitrary"),
    )(p)


def adamw(name, w, g, m, v):
    r, c = w.shape
    outs, _ = _rowwise(name, lambda _, w, g, m, v: ([*_adamw_math(w, g, m, v), g], []), [w, g, m, v], [],
                       [(c, F32)] * 4, [], _tile(r, 512, 8))
    return outs


RET_SCALE = HEAD_DIM ** -0.5
MLA_SCALE = (HEAD_DIM + MLA_ROPE) ** -0.5
GRAD_DT = BF16
IN_RET = 4 * HEADS * HEAD_DIM
IN_MLA = MLA_Q_RANK + MLA_KV_RANK + MLA_ROPE
IN_MLA_PAD = IN_MLA + 64


def _heads(fn):
    return jnp.concatenate([fn(h) for h in range(HEADS)], axis=1)


def _head(a, h, stride=HEAD_DIM, off=0):
    return a[:, h * stride + off:h * stride + off + HEAD_DIM]


def _group_norm(o):
    rs = [lax.rsqrt(jnp.mean(_head(o, h) * _head(o, h), axis=-1, keepdims=True) + EPS) for h in range(HEADS)]
    return _heads(lambda h: _head(o, h) * rs[h]), rs


class _Alone:
    def host(self, kernel_name):
        return None

    def done(self, kernel_name, w):
        pass

    def grads(self, g):
        pass


def _ffn_fwd(tag, n, w, k, tm, tmk, plan):
    gate, up, down = tag + "_gate", tag + "_up", tag + "_down"
    if "wu" + k in w:
        g, u, a = mm_nn(up, n, w["wg" + k], tm, outs=[BF16] * 3, w2=w["wu" + k],
                        epilogue=lambda g, u: (g, u, _silu(g) * u), host=plan.host(up))
    else:
        (g,) = mm_nn(gate, n, w["wg" + k], tm, out_dtype=BF16, host=plan.host(gate))
        plan.done(gate, w)
        u, a = mm_nn(up, n, w["wu" + k], tm, outs=[BF16] * 2, extras=(g,),
                     epilogue=lambda u, g: (u, _silu(g.astype(F32)) * u), host=plan.host(up))
    plan.done(up, w)
    ff = a.shape[1]
    wd = w["wd" + k]
    f = mm_nn_k(down, a, wd.reshape(ff, wd.shape[2]), tmk, _tile(ff, 1408), host=plan.host(down))
    plan.done(down, w)
    return g, u, a, f


def _ffn_bwd(tag, df, n, g, u, a, wg, wu, wd, tm, tmk, tk, dt, plan):
    ns = wg.shape[2]

    def gate_grads(da, g, u):
        g, u = g.astype(F32), u.astype(F32)
        return da * u * _dsilu(g), da * _silu(g)

    def hosted(kernel_name, call):
        out = call(plan.host(kernel_name))
        plan.done(kernel_name, None)
        return out

    k = tag[-1]
    dg, du = hosted(tag + "_da", lambda h: mm_nt(tag + "_da", [df], [wd], tm, ns, outs=(BF16, BF16),
                                                 epilogue=gate_grads, extras=(g, u), host=h))
    dwg = hosted(tag + "_dwg", lambda h: mm_tn(tag + "_dwg", n, dg, dt, ns, tk, GRAD_DT, shard_cols=True, host=h))
    plan.grads({"wg" + k: dwg})
    dwu = hosted(tag + "_dwu", lambda h: mm_tn(tag + "_dwu", n, du, dt, ns, tk, GRAD_DT, shard_cols=True, host=h))
    plan.grads({"wu" + k: dwu})
    dwd = hosted(tag + "_dwd", lambda h: mm_tn(tag + "_dwd", a, df, ns, dt, tk, GRAD_DT, shard_rows=True, host=h))
    plan.grads({"wd" + k: dwd})
    dn = hosted(tag + "_dn", lambda h: mm_nt_k(tag + "_dn", [dg, du], [wg, wu], tmk, dt, out_dtype=BF16, host=h))
    return dn


def _local_step(x, tgt, meta, w, nw, plan):
    seq, d = x.shape
    t_real = N_META + seq
    tp = -(-t_real // LANES) * LANES
    zpad = jnp.zeros((tp - t_real, d), F32)
    h0 = jnp.concatenate([meta, x, zpad], axis=0)
    tgt_p = jnp.concatenate([jnp.zeros((N_META, d), F32), tgt, zpad], axis=0)
    cos_r, sin_r, cos_m, sa, sb = _rope_tables(tp)
    tm = _tile(tp, 512)
    tmt = _tile(tp, 768, 16)
    tmk = _tile(tp, 1408)
    tk = tp
    blk = tm
    dt = _tile(d, 1024)
    hw = HEADS * HEAD_DIM
    qw = HEADS * MLA_QK_PAD

    (n1,), _ = _rowwise("ffn1_norm", lambda r0, h, g: ([_rms(h, g)], []), [h0], [nw["ffn1_pre_norm"]],
                        [(d, BF16)], [], tm)
    g1, u1, a1, f1 = _ffn_fwd("ffn1", n1, w, "1", tm, tmk, plan)

    def post_ffn1(r0, h, f, post, pre):
        h1 = h + 0.5 * _rms(f, post)
        return [h1, _rms(h1, pre)], []

    (h1, un), _ = _rowwise("mix_norm", post_ffn1, [h0, f1], [nw["ffn1_post_norm"], nw["mix_pre_norm"]],
                           [(d, F32), (d, BF16)], [], tm)
    (proj_r,) = mm_nn("proj_r", un, w["w_r"], tm, n_block=_tile(IN_RET, 1024), host=plan.host("proj_r"))
    plan.done("proj_r", w)
    (proj_c,) = mm_nn("proj_c", un, w["w_c"], tm)

    def split_proj(r0, pr, pc, cr, sr, cm, ta, tb, qn, kvn):
        rq = _heads(lambda h: _rope_r(_head(pr, h), cr, sr))
        rk = _heads(lambda h: _rope_r(_head(pr, h, off=hw), cr, sr) * RET_SCALE)
        rv = pr[:, 2 * hw:3 * hw]
        cqn = _rms(pc[:, :MLA_Q_RANK], qn)
        ckvn = _rms(pc[:, MLA_Q_RANK:MLA_Q_RANK + MLA_KV_RANK], kvn)
        krr = _rope_m(pc[:, MLA_Q_RANK + MLA_KV_RANK:], cm, ta, tb)
        return [rq, rk, rv, cqn, ckvn, krr], []

    (rq, rk, rv, cqn, ckvn, krr), _ = _rowwise(
        "split_proj", split_proj, [proj_r, proj_c, cos_r, sin_r, cos_m, sa, sb],
        [nw["mla_q_norm"], nw["mla_kv_norm"]],
        [(hw, BF16), (hw, BF16), (hw, BF16), (MLA_Q_RANK, BF16), (MLA_KV_RANK, BF16), (LANES, F32)], [], tm)
    (qp,) = mm_nn("q_up", cqn, w["wuq"], tm)
    (kn,) = mm_nn("k_up", ckvn, w["wuk"], tm)
    (vv,) = mm_nn("v_up", ckvn, w["wuv"], tm, out_dtype=BF16)

    def build_qk(r0, qp, kn, krr, cm, ta, tb):
        qc = jnp.concatenate(
            [part for h in range(HEADS)
             for part in (_head(qp, h, MLA_QK_PAD), _rope_m(_head(qp, h, MLA_QK_PAD, HEAD_DIM), cm, ta, tb))], axis=1)
        kc = jnp.concatenate([part for h in range(HEADS) for part in (_head(kn, h), krr)], axis=1)
        return [qc, kc], []

    (qc, kc), _ = _rowwise("build_qk", build_qk, [qp, kn, krr, cos_m, sa, sb], [], [(qw, BF16), (qw, BF16)], [], tm)
    o_m, lse, o_r, ret_states = mix_fwd("mix_fwd", qc, kc, vv, rq, rk, rv, blk, MLA_SCALE, host=plan.host("mix_fwd"))
    plan.done("mix_fwd", w)

    def gate_mix(r0, rg, o_r, o_m, gn):
        y, _ = _group_norm(o_r)
        return [jnp.concatenate([_silu(rg) * (y * gn), o_m], axis=1)], []

    (mixcat,), _ = _rowwise("gate_mix", gate_mix, [(proj_r, hw, 3), o_r, o_m], [nw["ret_group_norm"]],
                            [(2 * hw, BF16)], [], tm)
    (mix,) = mm_nn("mix_out", mixcat, w["w_out"], tm, n_block=dt)

    def post_mix(r0, h, m, post, pre):
        h2 = h + _rms(m, post)
        return [h2, _rms(h2, pre)], []

    (h2, n3), _ = _rowwise("ffn2_norm", post_mix, [h1, mix], [nw["mix_post_norm"], nw["ffn2_pre_norm"]],
                           [(d, F32), (d, BF16)], [], tm)
    g2, u2, a2, f2 = _ffn_fwd("ffn2", n3, w, "2", tm, tmk, plan)

    def loss_head(r0, h, f, t, post):
        h3 = h + 0.5 * _rms(f, post)
        row = r0 + lax.broadcasted_iota(jnp.int32, h3.shape, 0)
        err = jnp.where(row >= N_META, jnp.where(row < t_real, h3 - t, 0.0), 0.0)
        dh3 = err / d
        df, dpost = _rms_bwd(f, post, 0.5 * dh3)
        return [dh3, df], [_colsum(err * err), _colsum(dpost)]

    (dh3, df2), (loss_vec, d_post2) = _rowwise("loss_head", loss_head, [h2, f2, tgt_p], [nw["ffn2_post_norm"]],
                                               [(d, F32), (d, BF16)], [d, d], tm)
    loss = 0.5 * jnp.sum(loss_vec) / d
    dn3 = _ffn_bwd("ffn2", df2, n3, g2, u2, a2, w["wg2"], w["wu2"], w["wd2"], tmt, tmk, tk, dt, plan)

    def back_mix_norm(r0, h, m, dh3, dn, pre, post):
        dx, dpre = _rms_bwd(h, pre, dn)
        dh2 = dh3 + dx
        dm, dpost = _rms_bwd(m, post, dh2)
        return [dh2, dm], [_colsum(dpre), _colsum(dpost)]

    (dh2, dmix), (d_pre2, d_mix_post) = _rowwise(
        "back_mix_norm", back_mix_norm, [h2, mix, dh3, dn3], [nw["ffn2_pre_norm"], nw["mix_post_norm"]],
        [(d, F32), (d, BF16)], [d, d], tm)
    (dmixcat,) = mm_nt("mix_dx", [dmix], [w["w_out"]], tmt, _tile(2 * hw, 512), outs=(BF16,))
    plan.grads(dict(w_out=mm_tn("mix_dw", mixcat, dmix, 2 * hw // N_CHIPS, dt, tk, GRAD_DT, shard_rows=True)))

    def back_gate(r0, dmc, rg, o_r, gn):
        d_ret, d_om = dmc[:, :hw], dmc[:, hw:]
        yh, rs = _group_norm(o_r)
        d_rg = d_ret * (yh * gn) * _dsilu(rg)
        dy = d_ret * _silu(rg)
        gyh = dy * gn
        d_or = _heads(lambda h: rs[h] * (_head(gyh, h) - _head(yh, h) * jnp.mean(_head(gyh, h) * _head(yh, h),
                                                                                    axis=-1, keepdims=True)))
        return [d_or, d_rg, d_om], [_colsum(dy * yh)]

    (d_or, d_rg, d_om), (d_gn,) = _rowwise("back_gate", back_gate, [dmixcat, (proj_r, hw, 3), o_r],
                                           [nw["ret_group_norm"]], [(hw, BF16), (hw, F32), (hw, BF16)], [hw], tm)
    dqc, dkc, dvv = attn_bwd("mla_bwd", qc, kc, vv, d_om, o_m, lse, blk, MLA_SCALE, host=plan.host("mla_bwd"))
    plan.done("mla_bwd", None)
    drq, drk, drv = ret_bwd("ret_bwd", rq, rk, rv, d_or, ret_states, blk)

    def back_qk(r0, dqc, dkc, dvv, cm, ta, tb):
        dkc = dkc.astype(F32)
        dqp = jnp.concatenate(
            [part for h in range(HEADS)
             for part in (_head(dqc, h, MLA_QK_PAD), _rope_m_t(_head(dqc, h, MLA_QK_PAD, HEAD_DIM), cm, ta, tb))],
            axis=1)
        dkn = _heads(lambda h: _head(dkc, h, MLA_QK_PAD))
        dkr = _head(dkc, 0, MLA_QK_PAD, HEAD_DIM)
        for h in range(1, HEADS):
            dkr = dkr + _head(dkc, h, MLA_QK_PAD, HEAD_DIM)
        return [dqp, dkn, _rope_m_t(dkr, cm, ta, tb), dvv], []

    (dqp, dkn, dkr, dvb), _ = _rowwise("back_qk", back_qk, [dqc, dkc, dvv, cos_m, sa, sb], [],
                                       [(qw, BF16), (hw, BF16), (LANES, F32), (hw, BF16)], [], tm)
    (dcqn,) = mm_nt("q_dx", [dqp], [w["wuq"]], tm, MLA_Q_RANK)
    dwuq = mm_tn("q_dw", cqn, dqp, MLA_Q_RANK, _tile(qw, 1024), tk, GRAD_DT)
    (dckvn,) = mm_nt("kv_dx", [dkn, dvb], [w["wuk"], w["wuv"]], tm, MLA_KV_RANK)
    dwuk = mm_tn("k_dw", ckvn, dkn, MLA_KV_RANK, hw, tk, GRAD_DT)
    dwuv = mm_tn("v_dw", ckvn, dvb, MLA_KV_RANK, hw, tk, GRAD_DT)

    def back_proj(r0, drq, drk, drv, d_rg, pc, dcqn, dckvn, dkr, cr, sr, qn, kvn):
        d_q = _heads(lambda h: _rope_r_t(_head(drq, h), cr, sr))
        d_k = _heads(lambda h: _rope_r_t(_head(drk, h), cr, sr) * RET_SCALE)
        dcq, a_q = _rms_bwd(pc[:, :MLA_Q_RANK], qn, dcqn)
        dckv, a_kv = _rms_bwd(pc[:, MLA_Q_RANK:MLA_Q_RANK + MLA_KV_RANK], kvn, dckvn)
        return ([jnp.concatenate([d_q, d_k, drv, d_rg], axis=1), jnp.concatenate([dcq, dckv, dkr], axis=1)],
                [_colsum(a_q), _colsum(a_kv)])

    (dproj_r, dproj_c), (d_qn, d_kvn) = _rowwise(
        "back_proj", back_proj, [drq, drk, drv, d_rg, proj_c, dcqn, dckvn, dkr, cos_r, sin_r],
        [nw["mla_q_norm"], nw["mla_kv_norm"]], [(IN_RET, BF16), (IN_MLA_PAD, BF16)],
        [MLA_Q_RANK, MLA_KV_RANK], tm)
    (dun,) = mm_nt("proj_dx", [dproj_r, dproj_c], [w["w_r"], w["w_c"]], tmt, _tile(d, 512), outs=(BF16,))
    dw_r = mm_tn("proj_dw_r", un, dproj_r, dt, _tile(IN_RET, 1024), tk, GRAD_DT)
    dw_c = mm_tn("proj_dw_c", un, dproj_c, dt, IN_MLA_PAD, tk, GRAD_DT)
    plan.grads(dict(w_r=dw_r, w_c=dw_c, wuq=dwuq, wuk=dwuk, wuv=dwuv))

    def back_ffn1_norm(r0, h, f, dh2, dn, pre, post):
        dx, dpre = _rms_bwd(h, pre, dn)
        dh1 = dh2 + dx
        df, dpost = _rms_bwd(f, post, 0.5 * dh1)
        return [dh1, df], [_colsum(dpre), _colsum(dpost)]

    (dh1, df1), (d_mix_pre, d_post1) = _rowwise(
        "back_ffn1_norm", back_ffn1_norm, [h1, f1, dh2, dun], [nw["mix_pre_norm"], nw["ffn1_post_norm"]],
        [(d, F32), (d, BF16)], [d, d], tm)
    dn1 = _ffn_bwd("ffn1", df1, n1, g1, u1, a1, w["wg1"], w["wu1"], w["wd1"], tmt, tmk, tk, dt, plan)

    def back_input(r0, h, dh1, dn, pre):
        dx, dpre = _rms_bwd(h, pre, dn)
        return [dh1 + dx], [_colsum(dpre)]

    (dh0,), (d_pre1,) = _rowwise("back_input", back_input, [h0, dh1, dn1], [nw["ffn1_pre_norm"]], [(d, F32)], [d], tm)

    small = dict(ffn1_pre_norm=d_pre1, ffn1_post_norm=d_post1, mix_pre_norm=d_mix_pre, ret_group_norm=d_gn,
                 mla_q_norm=d_qn, mla_kv_norm=d_kvn, mix_post_norm=d_mix_post, ffn2_pre_norm=d_pre2,
                 ffn2_post_norm=d_post2)
    return loss, dh0[N_META:t_real], small, dh0[:N_META]


WEIGHTS = ("meta_tokens", "ffn1_pre_norm", "ffn1_w_gate", "ffn1_w_up", "ffn1_w_down", "ffn1_post_norm",
           "mix_pre_norm", "w_in", "ret_group_norm", "mla_q_norm", "mla_w_uq", "mla_kv_norm", "mla_w_uk",
           "mla_w_uv", "w_out", "mix_post_norm", "ffn2_pre_norm", "ffn2_w_gate", "ffn2_w_up", "ffn2_w_down",
           "ffn2_post_norm")
BIG = ("ffn1_w_gate", "ffn1_w_up", "ffn1_w_down", "w_in", "mla_w_uq", "mla_w_uk", "mla_w_uv", "w_out",
       "ffn2_w_gate", "ffn2_w_up", "ffn2_w_down")
NORMS = ("ffn1_pre_norm", "ffn1_post_norm", "mix_pre_norm", "ret_group_norm", "mla_q_norm", "mla_kv_norm",
         "mix_post_norm", "ffn2_pre_norm", "ffn2_post_norm")


def _unshard_cols(g):
    return g.transpose(1, 0, 2).reshape(g.shape[1], -1)


def _shard_cols(a):
    return a.reshape(a.shape[0], N_CHIPS, -1).transpose(1, 0, 2)


def _pack_rows(rows, width):
    rows = [jnp.pad(r, ((0, 0), (0, width - r.shape[1]))) for r in rows]
    n = sum(r.shape[0] for r in rows)
    return jnp.pad(jnp.concatenate(rows, axis=0), ((0, -n % 8), (0, 0)))


def _weight_views(full):
    w = {}
    for n, g in full.items():
        if n == "w_in":
            w_in = _unshard_cols(g)
            w["w_r"] = w_in[:, :IN_RET]
            w["w_c"] = jnp.pad(w_in[:, IN_RET:], ((0, 0), (0, IN_MLA_PAD - IN_MLA)))
        elif n == "mla_w_uq":
            q = _unshard_cols(g).reshape(MLA_Q_RANK, HEADS, HEAD_DIM + MLA_ROPE)
            q = jnp.pad(q, ((0, 0), (0, 0), (0, MLA_QK_PAD - HEAD_DIM - MLA_ROPE)))
            w["wuq"] = q.reshape(MLA_Q_RANK, HEADS * MLA_QK_PAD)
        elif n in ("mla_w_uk", "mla_w_uv"):
            w["wu" + n[-1]] = _unshard_cols(g)
        elif n == "w_out":
            w["w_out"] = g.reshape(-1, g.shape[2])
        else:
            w["w" + n[7] + n[3]] = g
    return w


def _contributions(g):
    ffn = {"g": "gate", "u": "up", "d": "down"}
    c = {f"ffn{n[2]}_w_{ffn[n[1]]}": a for n, a in g.items() if len(n) == 3 and n[2] in "12"}
    if "w_out" in g:
        c["w_out"] = g["w_out"]
    if "w_r" in g:
        dwuq = g["wuq"].reshape(MLA_Q_RANK, HEADS, MLA_QK_PAD)[:, :, :HEAD_DIM + MLA_ROPE]
        c.update(w_in=_shard_cols(jnp.concatenate([g["w_r"], g["w_c"][:, :IN_MLA]], axis=1)),
                 mla_w_uq=_shard_cols(dwuq.reshape(MLA_Q_RANK, -1)), mla_w_uk=_shard_cols(g["wuk"]),
                 mla_w_uv=_shard_cols(g["wuv"]))
    return c


class _Schedule(_Alone):
    FIRST = ("ffn1_w_gate",)
    CARRIED = {"ffn1_gate": ("ffn1_w_up",), "ffn1_up": ("ffn1_w_down",),
               "ffn1_down": ("w_in", "mla_w_uq", "mla_w_uk", "mla_w_uv"),
               "proj_r": ("w_out",), "mix_fwd": ("ffn2_w_gate", "ffn2_w_up"), "ffn2_up": ("ffn2_w_down",)}
    GRAD_HOST = dict(ffn2_w_gate="mla_bwd", ffn2_w_up="mla_bwd", ffn2_w_down="ffn2_dn",
                     w_out="mla_bwd", w_in="ffn1_da", mla_w_uq="ffn1_da", mla_w_uk="ffn1_da", mla_w_uv="ffn1_da",
                     ffn1_w_gate="ffn1_dwu", ffn1_w_up="ffn1_dn", ffn1_w_down="ffn1_dn")

    def __init__(self, bufs):
        self.gathers = {k: (gather_halves([bufs[n] for n in names]), names) for k, names in self.CARRIED.items()}
        self.waiting = {}
        self.exchanges = {}
        self.grad = {}

    def host(self, kernel_name):
        if kernel_name in self.gathers:
            return self.gathers[kernel_name][0]
        if kernel_name in self.waiting:
            names, sums = zip(*self.waiting.pop(kernel_name))
            self.exchanges[kernel_name] = (chip_exchange(list(sums)), names, sums)
            return self.exchanges[kernel_name][0]
        return None

    def done(self, kernel_name, w):
        if kernel_name in self.gathers:
            comm, names = self.gathers[kernel_name]
            w.update(_weight_views({n: sibling_fill("fill_" + n, b) for n, b in zip(names, comm.result)}))
        elif kernel_name in self.exchanges:
            comm, names, sums = self.exchanges[kernel_name]
            for n, q, mine in zip(names, comm.result, sums):
                self.grad[n] = reduce_join("reduce_join_" + n, q, mine)

    def grads(self, g):
        for n, a in _contributions(g).items():
            self.waiting.setdefault(self.GRAD_HOST[n], []).append((n, pair_add("pair_add_" + n, a)))


def _step(p, m, v, x, loss_target):
    d = x.shape[2]
    names = list(BIG)
    bufs = {n: own_slot(p[n][0].astype(BF16)) for n in names}
    first = _Schedule.FIRST
    gathered = gather_halves([bufs[n] for n in first] + [own_slot(p["meta_tokens"])]).run("gather_first")
    filled = [sibling_fill("fill_" + n, b) for n, b in zip(first + ("meta_tokens",), gathered)]
    w = _weight_views(dict(zip(first, filled[:-1])))
    meta = _unshard_cols(filled[-1])
    nw = {n: p[n] for n in NORMS}
    plan = _Schedule(bufs)
    loss, grad_x, small, d_meta = _local_step(x[0], loss_target[0], meta, w, nw, plan)
    grads = dict(plan.grad)

    width = max(d, HEADS * HEAD_DIM)
    packed = _pack_rows([small[n] for n in NORMS] + [d_meta], width)
    total = sum_slots("small_sum", all_gather_devices(packed), F32)
    for i, n in enumerate(NORMS):
        grads[n] = total[i:i + 1, :p[n].shape[1]]
    cols = p["meta_tokens"].shape[1]
    chip = 2 * lax.axis_index("x") + lax.axis_index("y")
    grads["meta_tokens"] = lax.dynamic_slice(total[len(NORMS):len(NORMS) + N_META, :d], (0, chip * cols), (N_META, cols))

    delta, new_m, new_v = {}, {}, {}
    for n in names + ["meta_tokens"]:
        shape = p[n].shape
        flat = lambda a: a.reshape(-1, shape[-1])
        out = adamw("adamw_" + n, flat(p[n]), flat(grads[n]), flat(m[n]), flat(v[n]))
        delta[n], new_m[n], new_v[n], grads[n] = (o.reshape(shape) for o in out)
    pk = lambda src: _pack_rows([src[n] for n in NORMS], width)
    out = adamw("adamw_norms", pk(p), pk(grads), pk(m), pk(v))
    for i, n in enumerate(NORMS):
        delta[n], new_m[n], new_v[n] = (o[i:i + 1, :p[n].shape[1]] for o in out[:3])

    loss = lax.psum(loss, ("x", "y", "c"))
    return loss, grad_x[None], grads, delta, new_m, new_v


def kernel(x, meta_tokens, ffn1_pre_norm, ffn1_w_gate, ffn1_w_up, ffn1_w_down, ffn1_post_norm, mix_pre_norm, w_in, ret_group_norm, mla_q_norm, mla_w_uq, mla_kv_norm, mla_w_uk, mla_w_uv, w_out, mix_post_norm, ffn2_pre_norm, ffn2_w_gate, ffn2_w_up, ffn2_w_down, ffn2_post_norm, loss_target, m_meta_tokens, m_ffn1_pre_norm, m_ffn1_w_gate, m_ffn1_w_up, m_ffn1_w_down, m_ffn1_post_norm, m_mix_pre_norm, m_w_in, m_ret_group_norm, m_mla_q_norm, m_mla_w_uq, m_mla_kv_norm, m_mla_w_uk, m_mla_w_uv, m_w_out, m_mix_post_norm, m_ffn2_pre_norm, m_ffn2_w_gate, m_ffn2_w_up, m_ffn2_w_down, m_ffn2_post_norm, v_meta_tokens, v_ffn1_pre_norm, v_ffn1_w_gate, v_ffn1_w_up, v_ffn1_w_down, v_ffn1_post_norm, v_mix_pre_norm, v_w_in, v_ret_group_norm, v_mla_q_norm, v_mla_w_uq, v_mla_kv_norm, v_mla_w_uk, v_mla_w_uv, v_w_out, v_mix_post_norm, v_ffn2_pre_norm, v_ffn2_w_gate, v_ffn2_w_up, v_ffn2_w_down, v_ffn2_post_norm):
    args = locals()
    p = {n: args[n] for n in WEIGHTS}
    m = {n: args["m_" + n] for n in WEIGHTS}
    v = {n: args["v_" + n] for n in WEIGHTS}
    loss, grad_x, grads, delta, new_m, new_v = _step(p, m, v, x, loss_target)
    return (loss, grad_x, *[grads[n] for n in WEIGHTS], *[delta[n] for n in WEIGHTS],
            *[new_m[n] for n in WEIGHTS], *[new_v[n] for n in WEIGHTS])
```

```python
import functools
import math

import jax
import jax.numpy as jnp
import numpy as np
from jax import lax
from jax.experimental import pallas as pl
from jax.experimental.pallas import tpu as pltpu

F32 = jnp.float32
BF16 = jnp.bfloat16

EPS = 1e-6
N_META = 16
HEADS = 8
HEAD_DIM = 128
MLA_ROPE = 64
MLA_QK_PAD = 256
MLA_Q_RANK = 512
MLA_KV_RANK = 256
ROPE_THETA = 10000.0
N_CHIPS = 4
LANES = 128
VMEM_LIMIT = 60 * 2 ** 20

ADAM_LR = 0.001
ADAM_B1 = 0.9
ADAM_B2 = 0.999
ADAM_EPS = 1e-08
ADAM_WD = 0.01
ADAM_STEP = 10

NN = (((1,), (0,)), ((), ()))
NT = (((1,), (1,)), ((), ()))
TN = (((0,), (0,)), ((), ()))
MESH = pl.DeviceIdType.MESH


def _tile(n, pref, align=LANES):
    if n <= pref:
        return n
    best = 0
    for t in range(align, pref + 1, align):
        if n % t == 0:
            best = t
    assert best, (n, pref)
    return best


def _params(*sem, collective_id=None):
    return pltpu.CompilerParams(dimension_semantics=sem, vmem_limit_bytes=VMEM_LIMIT, collective_id=collective_id)


SIBLING_BARRIER = 0


def _sds(shape, dtype):
    return jax.ShapeDtypeStruct(tuple(shape), dtype)


def _rowwise(name, fn, rows, consts, outs, accs, tr, host=None):
    rows = [r if isinstance(r, tuple) else (r, r.shape[1], 0) for r in rows]
    t = rows[0][0].shape[0]
    assert t % tr == 0
    n_r, n_c, n_o = len(rows), len(consts), len(outs)

    def body(*refs):
        i = pl.program_id(0)
        r = [x[...] for x in refs[:n_r]]
        c = [x[...] for x in refs[n_r:n_r + n_c]]
        o_refs = refs[n_r + n_c:n_r + n_c + n_o]
        a_refs = refs[n_r + n_c + n_o:]
        o_vals, a_vals = fn(i * tr, *r, *c)
        for ref, v in zip(o_refs, o_vals):
            ref[...] = v.astype(ref.dtype)
        if a_refs:
            @pl.when(i == 0)
            def _():
                for ref, v in zip(a_refs, a_vals):
                    ref[...] = v

            @pl.when(i > 0)
            def _():
                for ref, v in zip(a_refs, a_vals):
                    ref[...] += v

    in_specs = [pl.BlockSpec((tr, w), functools.partial(lambda cb, i: (i, cb), cb)) for _, w, cb in rows]
    in_specs += [pl.BlockSpec(a.shape, lambda i: (0, 0)) for a in consts]
    out_specs = [pl.BlockSpec((tr, w), lambda i: (i, 0)) for w, _ in outs]
    out_specs += [pl.BlockSpec((1, w), lambda i: (0, 0)) for w in accs]
    out_shape = [_sds((t, w), dt) for w, dt in outs] + [_sds((1, w), F32) for w in accs]
    res = _call(body, name, (t // tr,), in_specs, out_specs, out_shape, [], [a for a, _, _ in rows] + list(consts),
                host=host)
    return res[:n_o], res[n_o:]


def _rms(x, w):
    r = lax.rsqrt(jnp.mean(x * x, axis=-1, keepdims=True) + EPS)
    return x * r * w


def _rms_bwd(x, w, dy):
    r = lax.rsqrt(jnp.mean(x * x, axis=-1, keepdims=True) + EPS)
    xh = x * r
    gy = dy * w
    dx = r * (gy - xh * jnp.mean(gy * xh, axis=-1, keepdims=True))
    return dx, dy * xh


def _colsum(v):
    return jnp.sum(v, axis=0, keepdims=True)


def _silu(x):
    return x * jax.nn.sigmoid(x)


def _dsilu(x):
    s = jax.nn.sigmoid(x)
    return s * (1.0 + x * (1.0 - s))


def _rope_r(x, cos, sin):
    return x * cos + pltpu.roll(x, 64, 1) * sin


def _rope_r_t(dy, cos, sin):
    return dy * cos + pltpu.roll(dy * sin, 64, 1)


def _rope_m(x, cos, sa, sb):
    return x * cos + pltpu.roll(x, 32, 1) * sa + pltpu.roll(x, 96, 1) * sb


def _rope_m_t(dy, cos, sa, sb):
    return dy * cos + pltpu.roll(dy * sa, 96, 1) + pltpu.roll(dy * sb, 32, 1)


def _rope_tables(t):
    pos = np.arange(t, dtype=np.float32)

    def cs(dim):
        inv = np.float32(ROPE_THETA) ** (-np.arange(0, dim, 2, dtype=np.float32) / np.float32(dim))
        ang = pos[:, None] * inv[None, :]
        return np.cos(ang), np.sin(ang)

    c, s = cs(HEAD_DIM)
    cos_r = np.concatenate([c, c], axis=1)
    sin_r = np.concatenate([-s, s], axis=1)
    c, s = cs(MLA_ROPE)
    z32, z64 = np.zeros_like(s), np.zeros((t, 64), np.float32)
    cos_m = np.concatenate([c, c, z64], axis=1)
    sa = np.concatenate([z32, s, z64], axis=1)
    sb = np.concatenate([-s, z32, z64], axis=1)
    return tuple(jnp.asarray(a, F32) for a in (cos_r, sin_r, cos_m, sa, sb))


def _call(body, name, grid, in_specs, out_specs, out_shape, scratch, operands, host=None):
    sem = ("arbitrary",) * len(grid)
    if host is None:
        return pl.pallas_call(body, name=name, grid=grid, in_specs=in_specs, out_specs=out_specs, out_shape=out_shape,
                              scratch_shapes=scratch, compiler_params=_params(*sem))(*operands)
    n_in, n_out, n_s = len(in_specs), len(out_shape), len(scratch)
    h_in, h_out = len(host.ins), len(host.out_shape)

    def hosted(*refs):
        a = n_in
        b = a + h_in
        c = b + n_out
        d = c + h_out
        e = d + n_s
        ids = [pl.program_id(i) for i in range(len(grid))]
        first = functools.reduce(jnp.logical_and, [i == 0 for i in ids])
        last = functools.reduce(jnp.logical_and, [i == g - 1 for i, g in zip(ids, grid)])
        comm = (refs[a:b], refs[c:d], refs[e:])

        @pl.when(first)
        def _():
            host.start(*comm)

        body(*refs[:a], *refs[b:c], *refs[d:e])

        @pl.when(last)
        def _():
            host.finish(*comm)

    hbm = pl.BlockSpec(memory_space=pl.ANY)
    res = pl.pallas_call(
        hosted, name=name, grid=grid, in_specs=list(in_specs) + [hbm] * h_in, out_specs=list(out_specs) + [hbm] * h_out,
        out_shape=list(out_shape) + list(host.out_shape), scratch_shapes=list(scratch) + host.scratch(),
        input_output_aliases=host.aliases(n_in, n_out),
        compiler_params=_params(*sem, collective_id=host.collective_id))(*operands, *host.ins)
    host.result = res[n_out:]
    return res[:n_out]


def _mm(name, grid, operands, in_specs, dns, out_specs, out_shape, epilogue=None, extras=(), extra_specs=(),
        acc_shape=None, host=None):
    n_p, n_e = len(dns), len(extras)
    nk = grid[2]
    n_o = len(out_shape)
    in_place = nk > 1 and epilogue is None and n_o == 1 and out_shape[0].dtype == F32

    def body(*refs):
        ab = refs[:2 * n_p]
        ex = refs[2 * n_p:2 * n_p + n_e]
        outs = refs[2 * n_p + n_e:2 * n_p + n_e + n_o]

        part = None
        for p in range(n_p):
            d = lax.dot_general(ab[2 * p][...], ab[2 * p + 1][...], dns[p], preferred_element_type=F32)
            part = d if part is None else part + d

        def finish(acc):
            vals = (acc,) if epilogue is None else epilogue(acc, *[e[...] for e in ex])
            for o, v in zip(outs, vals):
                o[...] = v.astype(o.dtype)

        if nk == 1:
            finish(part)
        else:
            acc_ref = outs[0] if in_place else refs[2 * n_p + n_e + n_o]
            k = pl.program_id(2)

            @pl.when(k == 0)
            def _():
                acc_ref[...] = part

            @pl.when(k > 0)
            def _():
                acc_ref[...] += part

            if not in_place:
                @pl.when(k == nk - 1)
                def _():
                    finish(acc_ref[...])

    scratch = [] if nk == 1 or in_place else [pltpu.VMEM(acc_shape, F32)]
    return _call(body, name, grid, list(in_specs) + list(extra_specs), out_specs, out_shape, scratch,
                 [*operands, *extras], host)


def mm_nn(name, x, w, tm, out_dtype=F32, epilogue=None, outs=None, w2=None, n_block=None, extras=(), host=None):
    t, kdim = x.shape
    if w.ndim == 3:
        s, _, ns = w.shape
        n, tn, nb = s * ns, ns, s
        wspec = pl.BlockSpec((None, kdim, ns), lambda j, i, k: (j, 0, 0))
    else:
        n = w.shape[1]
        tn = n_block or n
        nb = n // tn
        wspec = pl.BlockSpec((kdim, tn), lambda j, i, k: (0, j))
    xspec = pl.BlockSpec((tm, kdim), lambda j, i, k: (i, 0))
    ospec = pl.BlockSpec((tm, tn), lambda j, i, k: (i, j))
    outs = outs or [out_dtype]
    grid = (nb, t // tm, 1)
    if w2 is None:
        return _mm(name, grid, [x, w], [xspec, wspec], [NN], [ospec] * len(outs), [_sds((t, n), d) for d in outs],
                   epilogue=epilogue, extras=extras, extra_specs=[ospec] * len(extras), host=host)

    def body(x_ref, w_ref, w2_ref, *o_refs):
        xv = x_ref[...]
        a = jnp.dot(xv, w_ref[...], preferred_element_type=F32)
        b = jnp.dot(xv, w2_ref[...], preferred_element_type=F32)
        for o, v in zip(o_refs, epilogue(a, b)):
            o[...] = v.astype(o.dtype)

    return _call(body, name, grid[:2],
                 [pl.BlockSpec((tm, kdim), lambda j, i: (i, 0)),
                  pl.BlockSpec((None, kdim, tn), lambda j, i: (j, 0, 0)),
                  pl.BlockSpec((None, kdim, tn), lambda j, i: (j, 0, 0))],
                 [pl.BlockSpec((tm, tn), lambda j, i: (i, j))] * len(outs), [_sds((t, n), d) for d in outs], [],
                 [x, w, w2], host)


def mm_nn_k(name, x, w, tm, tk, out_dtype=F32, host=None):
    t, kdim = x.shape
    n = w.shape[1]
    grid = (t // tm, 1, kdim // tk)
    return _mm(name, grid, [x, w],
               [pl.BlockSpec((tm, tk), lambda i, j, k: (i, k)), pl.BlockSpec((tk, n), lambda i, j, k: (k, 0))],
               [NN], [pl.BlockSpec((tm, n), lambda i, j, k: (i, 0))], [_sds((t, n), out_dtype)],
               acc_shape=(tm, n), host=host)[0]


def mm_nt(name, xs, ws, tm, tn, outs=(F32,), epilogue=None, extras=(), host=None):
    t = xs[0].shape[0]
    specs, ops = [], []
    for x, w in zip(xs, ws):
        kdim = x.shape[1]
        specs.append(pl.BlockSpec((tm, kdim), lambda j, i, k: (i, 0)))
        if w.ndim == 3:
            assert tn == w.shape[1]
            n = w.shape[0] * w.shape[1]
            specs.append(pl.BlockSpec((None, tn, kdim), lambda j, i, k: (j, 0, 0)))
        else:
            n = w.shape[0]
            specs.append(pl.BlockSpec((tn, kdim), lambda j, i, k: (j, 0)))
        ops += [x, w]
    ospec = pl.BlockSpec((tm, tn), lambda j, i, k: (i, j))
    return _mm(name, (n // tn, t // tm, 1), ops, specs, [NT] * len(xs), [ospec] * len(outs),
               [_sds((t, n), d) for d in outs], epilogue=epilogue, extras=extras,
               extra_specs=[ospec] * len(extras), host=host)


def mm_nt_k(name, xs, ws, tm, tn, out_dtype=F32, host=None):
    t = xs[0].shape[0]
    s, n, ns = ws[0].shape
    specs, ops = [], []
    for x, w in zip(xs, ws):
        specs.append(pl.BlockSpec((tm, ns), lambda i, j, k: (i, k)))
        specs.append(pl.BlockSpec((None, tn, ns), lambda i, j, k: (k, j, 0)))
        ops += [x, w]
    return _mm(name, (t // tm, n // tn, s), ops, specs, [NT] * len(xs),
               [pl.BlockSpec((tm, tn), lambda i, j, k: (i, j))], [_sds((t, n), out_dtype)], acc_shape=(tm, tn),
               host=host)[0]


def mm_tn(name, x, y, tm, tn, tk, out_dtype, shard_rows=False, shard_cols=False, host=None):
    t, m = x.shape
    n = y.shape[1]
    grid = (m // tm, n // tn, t // tk)
    if shard_cols:
        ospec = pl.BlockSpec((None, tm, tn), lambda i, j, k: (j, i, 0))
        oshape = _sds((n // tn, m, tn), out_dtype)
    elif shard_rows:
        ospec = pl.BlockSpec((None, tm, tn), lambda i, j, k: (i, 0, j))
        oshape = _sds((m // tm, tm, n), out_dtype)
    else:
        ospec = pl.BlockSpec((tm, tn), lambda i, j, k: (i, j))
        oshape = _sds((m, n), out_dtype)
    return _mm(name, grid, [x, y],
               [pl.BlockSpec((tk, tm), lambda i, j, k: (k, i)), pl.BlockSpec((tk, tn), lambda i, j, k: (k, j))],
               [TN], [ospec], [oshape], acc_shape=(tm, tn), host=host)[0]


def _decay_logs():
    return [math.log(1.0 - 2.0 ** (-5.0 - h)) for h in range(HEADS)]


def _log_decay(h):
    lg = jnp.float32(_decay_logs()[0])
    for i in range(1, HEADS):
        lg = jnp.where(h == i, jnp.float32(_decay_logs()[i]), lg)
    return lg


def _decayed_scores(q, k, lg):
    s = lax.dot_general(q, k, NT, preferred_element_type=F32)
    row = lax.broadcasted_iota(jnp.int32, s.shape, 0)
    col = lax.broadcasted_iota(jnp.int32, s.shape, 1)
    dec = jnp.where(col <= row, jnp.exp(jnp.maximum(row - col, 0).astype(F32) * lg), 0.0)
    return s * dec, dec


def _causal(s):
    row = lax.broadcasted_iota(jnp.int32, s.shape, 0)
    col = lax.broadcasted_iota(jnp.int32, s.shape, 1)
    return jnp.where(col <= row, s, -1e30)


def _attn_fwd_step(qi, blk, scale, q_ref, k_ref, v_ref, o_ref, lse_ref, m_ref, l_ref, acc_ref):
    qv = q_ref[...]
    m_ref[...] = jnp.full_like(m_ref, -1e30)
    l_ref[...] = jnp.zeros_like(l_ref)
    acc_ref[...] = jnp.zeros_like(acc_ref)

    def keys(start, n, diag_from):
        rows = pl.ds(pl.multiple_of(start, blk), n)
        s = lax.dot_general(qv, k_ref[rows, :], NT, preferred_element_type=F32) * scale
        if diag_from is not None:
            row = lax.broadcasted_iota(jnp.int32, s.shape, 0)
            col = lax.broadcasted_iota(jnp.int32, s.shape, 1)
            s = jnp.where(col - diag_from <= row, s, -1e30)
        m = m_ref[...]
        m_new = jnp.maximum(m, jnp.max(s, axis=-1, keepdims=True))
        p = jnp.exp(s - m_new)
        alpha = jnp.exp(m - m_new)
        m_ref[...] = m_new
        l_ref[...] = alpha * l_ref[...] + jnp.sum(p, axis=-1, keepdims=True)
        acc_ref[...] = alpha * acc_ref[...] + jnp.dot(p.astype(BF16), v_ref[rows, :], preferred_element_type=F32)

    @pl.loop(0, qi // 2)
    def _(j):
        keys(j * (2 * blk), 2 * blk, None)

    @pl.when(qi % 2 == 1)
    def _():
        keys((qi - 1) * blk, 2 * blk, blk)

    @pl.when(qi % 2 == 0)
    def _():
        keys(qi * blk, blk, 0)

    l = l_ref[...]
    o_ref[...] = acc_ref[...] / l
    lse_ref[...] = jnp.broadcast_to(m_ref[...] + jnp.log(l), (blk, HEAD_DIM))


def mix_fwd(name, q, k, v, rq, rk, rv, blk, scale, host=None):
    t = q.shape[0]
    dq = q.shape[1] // HEADS
    nq = t // blk

    def body(q_ref, k_ref, v_ref, rq_ref, rk_ref, rv_ref, o_ref, lse_ref, ro_ref, st_ref, m_ref, l_ref, acc_ref, state):
        h, i = pl.program_id(0), pl.program_id(1)

        @pl.when(h < HEADS)
        def _():
            _attn_fwd_step(i, blk, scale, q_ref, k_ref, v_ref, o_ref, lse_ref, m_ref, l_ref, acc_ref)

        @pl.when(h >= HEADS)
        def _():
            _ret_fwd_step(h - HEADS, i, blk, rq_ref, rk_ref, rv_ref, ro_ref, st_ref, state)

    mla = lambda w, whole=False: pl.BlockSpec(
        (t if whole else blk, w),
        lambda h, i: (0 if whole else jnp.where(h < HEADS, i, nq - 1), jnp.minimum(h, HEADS - 1)))
    ret = lambda rows: pl.BlockSpec(
        (rows, HEAD_DIM), lambda h, i: (jnp.where(h >= HEADS, i, 0), jnp.maximum(h - HEADS, 0)))
    wide = _sds((t, HEADS * HEAD_DIM), F32)
    return _call(body, name, (2 * HEADS, nq),
                 [mla(dq), mla(dq, True), mla(HEAD_DIM, True), ret(blk), ret(blk), ret(blk)],
                 [mla(HEAD_DIM), mla(HEAD_DIM), ret(blk), ret(HEAD_DIM)],
                 [wide, wide, wide, _sds((nq * HEAD_DIM, HEADS * HEAD_DIM), F32)],
                 [pltpu.VMEM((blk, 1), F32), pltpu.VMEM((blk, 1), F32), pltpu.VMEM((blk, HEAD_DIM), F32),
                  pltpu.VMEM((HEAD_DIM, HEAD_DIM), F32)],
                 [q, k, v, rq, rk, rv], host)


def attn_bwd(name, q, k, v, do, o, lse, blk, scale, host=None):
    t = q.shape[0]
    dq_w = q.shape[1] // HEADS
    nb = t // blk

    def body(q_ref, k_ref, v_ref, do_ref, o_ref, lse_ref, dq_ref, dk_out, dv_out, dk_ref, dv_ref):
        ki = pl.program_id(1)
        kv = k_ref[...]
        vv = v_ref[...]

        @pl.when(ki == 0)
        def _():
            dq_ref[...] = jnp.zeros_like(dq_ref)

        def queries(start, n, diag):
            rows = pl.ds(pl.multiple_of(start, blk), n)
            qv, dov = q_ref[rows, :], do_ref[rows, :]
            s = lax.dot_general(qv, kv, NT, preferred_element_type=F32) * scale
            if diag:
                s = _causal(s)
            p = jnp.exp(s - lse_ref[rows, :][:, :1])
            dp = lax.dot_general(dov, vv, NT, preferred_element_type=F32)
            delta = jnp.sum(dov.astype(F32) * o_ref[rows, :], axis=-1, keepdims=True)
            ds = p * (dp - delta) * scale
            pb, dsb = p.astype(BF16), ds.astype(BF16)
            dv_ref[...] += lax.dot_general(pb, dov, TN, preferred_element_type=F32)
            dk_ref[...] += lax.dot_general(dsb, qv, TN, preferred_element_type=F32)
            dq_ref[rows, :] += jnp.dot(dsb, kv, preferred_element_type=F32)

        dk_ref[...] = jnp.zeros_like(dk_ref)
        dv_ref[...] = jnp.zeros_like(dv_ref)
        queries(ki * blk, blk, True)
        later = nb - 1 - ki

        @pl.when(later % 2 == 1)
        def _():
            queries((ki + 1) * blk, blk, False)

        @pl.loop(0, later // 2)
        def _(j):
            queries((ki + 1 + later % 2 + 2 * j) * blk, 2 * blk, False)

        dk_out[...] = dk_ref[...].astype(dk_out.dtype)
        dv_out[...] = dv_ref[...].astype(dv_out.dtype)

    full = lambda w: pl.BlockSpec((t, w), lambda h, j: (0, h))
    blkd = lambda w: pl.BlockSpec((blk, w), lambda h, j: (j, h))
    return _call(body, name, (HEADS, nb),
                 [full(dq_w), blkd(dq_w), blkd(HEAD_DIM), full(HEAD_DIM), full(HEAD_DIM), full(HEAD_DIM)],
                 [full(dq_w), blkd(dq_w), blkd(HEAD_DIM)],
                 [_sds(q.shape, F32), _sds(k.shape, BF16), _sds(v.shape, BF16)],
                 [pltpu.VMEM((blk, dq_w), F32), pltpu.VMEM((blk, HEAD_DIM), F32)], [q, k, v, do, o, lse], host)


def _chunk_decays(lg, blk):
    row = lax.broadcasted_iota(jnp.int32, (blk, HEAD_DIM), 0).astype(F32)
    return jnp.exp(lg * (row + 1.0)), jnp.exp(lg * (blk - 1.0 - row)), jnp.exp(lg * blk * jnp.ones((1, HEAD_DIM), F32))


def _ret_fwd_step(h, i, blk, q_ref, k_ref, v_ref, o_ref, st_ref, state):
    lg = _log_decay(h)

    @pl.when(i == 0)
    def _():
        state[...] = jnp.zeros_like(state)

    qv, kv, vv = q_ref[...], k_ref[...], v_ref[...]
    before = state[...]
    st_ref[...] = before
    p, _ = _decayed_scores(qv, kv, lg)
    xi, zeta, g_blk = _chunk_decays(lg, blk)
    o_ref[...] = (jnp.dot(p.astype(BF16), vv, preferred_element_type=F32)
                  + jnp.dot(qv, before.astype(BF16), preferred_element_type=F32) * xi)
    kz = (kv.astype(F32) * zeta).astype(BF16)
    state[...] = before * g_blk + lax.dot_general(kz, vv, TN, preferred_element_type=F32)


def ret_bwd(name, q, k, v, do, states, blk, host=None):
    t = q.shape[0]
    nb = t // blk

    def body(q_ref, k_ref, v_ref, do_ref, st_ref, dq_ref, dk_ref, dv_ref, dstate):
        h, i = pl.program_id(0), pl.program_id(1)
        lg = _log_decay(h)

        @pl.when(i == 0)
        def _():
            dstate[...] = jnp.zeros_like(dstate)

        qv, kv, vv, dov = q_ref[...], k_ref[...], v_ref[...], do_ref[...]
        before = st_ref[...].astype(BF16)
        after_grad = dstate[...]
        p, dec = _decayed_scores(qv, kv, lg)
        ds = lax.dot_general(dov, vv, NT, preferred_element_type=F32) * dec
        pb, dsb = p.astype(BF16), ds.astype(BF16)
        xi, zeta, g_blk = _chunk_decays(lg, blk)
        dox = (dov.astype(F32) * xi).astype(BF16)
        kz = (kv.astype(F32) * zeta).astype(BF16)
        agb = after_grad.astype(BF16)
        dv = lax.dot_general(pb, dov, TN, preferred_element_type=F32) + jnp.dot(kz, agb, preferred_element_type=F32)
        dq = (jnp.dot(dsb, kv, preferred_element_type=F32)
              + lax.dot_general(dox, before, NT, preferred_element_type=F32))
        dk = (lax.dot_general(dsb, qv, TN, preferred_element_type=F32)
              + lax.dot_general(vv, agb, NT, preferred_element_type=F32) * zeta)
        for ref, val in ((dq_ref, dq), (dk_ref, dk), (dv_ref, dv)):
            ref[...] = val.astype(ref.dtype)
        dstate[...] = after_grad * g_blk + lax.dot_general(qv, dox, TN, preferred_element_type=F32)

    hspec = pl.BlockSpec((blk, HEAD_DIM), lambda h, i: (nb - 1 - i, h))
    return _call(body, name, (HEADS, nb),
                 [hspec] * 4 + [pl.BlockSpec((HEAD_DIM, HEAD_DIM), lambda h, i: (nb - 1 - i, h))],
                 [hspec] * 3, [_sds(q.shape, BF16)] * 3, [pltpu.VMEM((HEAD_DIM, HEAD_DIM), F32)],
                 [q, k, v, do, states], host)


CHIP_FLIPS = ((1, 0), (0, 1), (1, 1))
CHIP_BARRIER = 1
ALL_BARRIER = 2


def _position():
    return lax.axis_index("x"), lax.axis_index("y"), lax.axis_index("c")


def _chip_peers(pos):
    x, y, c = pos
    return [(x ^ fx, y ^ fy, c) for fx, fy in CHIP_FLIPS]


class _Comm:
    def __init__(self, ins, out_shape, plan, n_remote, n_local, collective_id, in_place=False):
        self.ins, self.out_shape, self.plan = list(ins), list(out_shape), plan
        self.n_remote, self.n_local, self.collective_id = n_remote, n_local, collective_id
        self.in_place = in_place
        self.result = None

    def aliases(self, first_in, first_out):
        return {first_in + i: first_out + i for i in range(len(self.ins))} if self.in_place else {}

    def scratch(self):
        return [pltpu.SemaphoreType.DMA((self.n_remote,)), pltpu.SemaphoreType.DMA((self.n_remote,)),
                pltpu.SemaphoreType.DMA((max(self.n_local, 1),))]

    def _copies(self, in_refs, out_refs, sems):
        send_sems, recv_sems, local_sems = sems
        pos = _position()
        p = self.plan(pos, in_refs, out_refs)

        def remote(k, src, dst, dev):
            return pltpu.make_async_remote_copy(src_ref=src, dst_ref=dst, send_sem=send_sems.at[k],
                                                recv_sem=recv_sems.at[k], device_id=dev, device_id_type=MESH)

        local = [pltpu.make_async_copy(s, d, local_sems.at[i]) for i, (s, d) in enumerate(p["local"])]
        out = [remote(k, s, d, dev) for k, (s, d, dev) in enumerate(p["sends"])]
        arrivals = [functools.partial(remote, k, d, d, pos) for k, d in enumerate(p["recvs"])]
        return local, out, arrivals, p["peers"]

    def start(self, in_refs, out_refs, sems):
        local, out, _, peers = self._copies(in_refs, out_refs, sems)
        barrier = pltpu.get_barrier_semaphore()
        for peer in peers:
            pl.semaphore_signal(barrier, inc=1, device_id=peer, device_id_type=MESH)
        pl.semaphore_wait(barrier, len(peers))
        for cp in local + out:
            cp.start()

    def finish(self, in_refs, out_refs, sems):
        local, out, arrivals, _ = self._copies(in_refs, out_refs, sems)
        for make in arrivals:
            make().wait_recv()
        for cp in out:
            cp.wait_send()
        for cp in local:
            cp.wait()

    def run(self, name):
        n_in, n_out = len(self.ins), len(self.out_shape)

        def body(*refs):
            comm = (refs[:n_in], refs[n_in:n_in + n_out], refs[n_in + n_out:])
            self.start(*comm)
            self.finish(*comm)

        hbm = pl.BlockSpec(memory_space=pl.ANY)
        self.result = pl.pallas_call(
            body, name=name, in_specs=[hbm] * n_in, out_specs=[hbm] * n_out, out_shape=self.out_shape,
            scratch_shapes=self.scratch(), input_output_aliases=self.aliases(0, 0),
            compiler_params=pltpu.CompilerParams(collective_id=self.collective_id))(*self.ins)
        return self.result


def _half_rows(c, rows):
    r2 = rows // 2
    return pl.ds(pl.multiple_of(c * r2, math.gcd(r2, LANES)), r2)


def _half(ref, c, rows, lead=()):
    return ref.at[(*lead, _half_rows(c, rows))]


def own_slot(shard):
    me = 2 * lax.axis_index("x") + lax.axis_index("y")
    return lax.dynamic_update_slice(lax.empty((N_CHIPS, *shard.shape), shard.dtype), shard[None], (me, 0, 0))


def gather_halves(bufs):
    def plan(pos, ins, outs):
        x, y, c = pos
        me = 2 * x + y
        p = dict(local=[], sends=[], recvs=[], peers=_chip_peers(pos))
        for a, dst in enumerate(outs):
            rows = bufs[a].shape[1]
            for px, py, _ in p["peers"]:
                p["sends"].append((_half(dst, c, rows, (me,)), _half(dst, c, rows, (me,)), (px, py, c)))
                p["recvs"].append(_half(dst, c, rows, (2 * px + py,)))
        return p

    return _Comm(bufs, [_sds(b.shape, b.dtype) for b in bufs], plan, n_remote=3 * len(bufs), n_local=0,
                 collective_id=CHIP_BARRIER, in_place=True)


def chip_exchange(parts):
    def plan(pos, ins, outs):
        x, y, c = pos
        me = 2 * x + y
        p = dict(local=[], sends=[], recvs=[], peers=_chip_peers(pos))
        for src, dst in zip(ins, outs):
            for px, py, _ in p["peers"]:
                peer = 2 * px + py
                p["sends"].append((src.at[peer], dst.at[me], (px, py, c)))
                p["recvs"].append(dst.at[peer])
        return p

    return _Comm(parts, [_sds(g.shape, g.dtype) for g in parts], plan, n_remote=3 * len(parts), n_local=0,
                 collective_id=CHIP_BARRIER)


def all_gather_devices(v):
    flips = [(fx, fy, fc) for fx in (0, 1) for fy in (0, 1) for fc in (0, 1)][1:]

    def plan(pos, ins, outs):
        x, y, c = pos
        me = 4 * x + 2 * y + c
        p = dict(local=[(ins[0], outs[0].at[me])], sends=[], recvs=[],
                 peers=[(x ^ fx, y ^ fy, c ^ fc) for fx, fy, fc in flips])
        for px, py, pc in p["peers"]:
            p["sends"].append((ins[0], outs[0].at[me], (px, py, pc)))
            p["recvs"].append(outs[0].at[4 * px + 2 * py + pc])
        return p

    return _Comm([v], [_sds((8, *v.shape), v.dtype)], plan, n_remote=7, n_local=1,
                 collective_id=ALL_BARRIER).run("small_all_gather")[0]


SWAP_CHUNK_BYTES = 3 * 2 ** 19


def _sibling_stream(t, n, value, consume, sbuf, rbuf, send_sems, recv_sems, credits):
    x, y, c = _position()
    sib = (x, y, 1 - c)

    def copy(slot):
        return pltpu.make_async_remote_copy(src_ref=sbuf.at[slot], dst_ref=rbuf.at[slot], send_sem=send_sems.at[slot],
                                            recv_sem=recv_sems.at[slot], device_id=sib, device_id_type=MESH)

    slot = t % 2

    @pl.when(t == 0)
    def _():
        barrier = pltpu.get_barrier_semaphore()
        pl.semaphore_signal(barrier, inc=1, device_id=sib, device_id_type=MESH)
        pl.semaphore_wait(barrier, 1)

    @pl.when(jnp.logical_and(t >= 2, t < n))
    def _():
        copy(slot).wait_send()
        pl.semaphore_wait(credits.at[slot], 1)

    @pl.when(t < n)
    def _():
        sbuf[slot] = value
        copy(slot).start()

    @pl.when(t >= 1)
    def _():
        prev = 1 - slot
        copy(prev).wait_recv()
        consume(sbuf[prev], rbuf[prev])

        @pl.when(t + 1 < n)
        def _():
            pl.semaphore_signal(credits.at[prev], inc=1, device_id=sib, device_id_type=MESH)

    @pl.when(t == n)
    def _():
        copy(1 - slot).wait_send()
        if n > 1:
            copy(slot).wait_send()


def _swap_scratch(rows, cols, dtype):
    return [pltpu.VMEM((2, rows, cols), dtype), pltpu.VMEM((2, rows, cols), dtype),
            pltpu.SemaphoreType.DMA((2,)), pltpu.SemaphoreType.DMA((2,)), pltpu.SemaphoreType.REGULAR((2,))]


def _chunk_rows(rows, cols, dtype, nbytes=SWAP_CHUNK_BYTES):
    return _tile(rows, max(16, nbytes // (cols * jnp.dtype(dtype).itemsize)), 16)


def pair_add(name, g):
    s, r, c_ = g.shape
    r2 = r // 2
    cr = _chunk_rows(r2, c_, g.dtype)
    nj = r2 // cr

    n = s * nj

    def body(core, mine_ref, theirs_ref, o_ref, *scratch):
        def consume(_, got):
            o_ref[...] = (mine_ref[...].astype(F32) + got.astype(F32)).astype(o_ref.dtype)

        _sibling_stream(pl.program_id(0), n, theirs_ref[...], consume, *scratch)

    sent = lambda t: jnp.minimum(t, n - 1)
    used = lambda t: jnp.maximum(t - 1, 0)
    grid_spec = pltpu.PrefetchScalarGridSpec(
        num_scalar_prefetch=1, grid=(n + 1,),
        in_specs=[pl.BlockSpec((None, cr, c_), lambda t, core: (used(t) // nj, core[0] * nj + used(t) % nj, 0)),
                  pl.BlockSpec((None, cr, c_), lambda t, core: (sent(t) // nj, (1 - core[0]) * nj + sent(t) % nj, 0))],
        out_specs=pl.BlockSpec((None, cr, c_), lambda t, core: (used(t) // nj, used(t) % nj, 0)),
        scratch_shapes=_swap_scratch(cr, c_, g.dtype))
    core = lax.axis_index("c").astype(jnp.int32).reshape(1)
    return pl.pallas_call(body, name=name, grid_spec=grid_spec, out_shape=_sds((s, r2, c_), g.dtype),
                          compiler_params=_params("arbitrary", collective_id=SIBLING_BARRIER))(core, g, g)


def _adamw_math(w, g, m, v):
    m = ADAM_B1 * m + (1.0 - ADAM_B1) * g
    v = ADAM_B2 * v + (1.0 - ADAM_B2) * (g * g)
    m_hat = m / (1.0 - ADAM_B1 ** ADAM_STEP)
    v_hat = v / (1.0 - ADAM_B2 ** ADAM_STEP)
    return -ADAM_LR * (m_hat / (jnp.sqrt(v_hat) + ADAM_EPS) + ADAM_WD * w), m, v


def reduce_join(name, p, mine):
    s, r2, c_ = p.shape
    cr = _chunk_rows(r2, c_, F32)
    nj = r2 // cr

    def body(where, p_ref, mine_ref, o_ref, acc_ref, *scratch):
        c, me = where[0], where[1]
        for chip in range(s):
            @pl.when(me == chip)
            def _():
                terms = [mine_ref[...] if i == chip else p_ref[i] for i in range(s)]
                acc = terms[0].astype(F32)
                for term in terms[1:]:
                    acc = acc + term.astype(F32)
                acc_ref[...] = acc

        def consume(own, got):
            o_ref[c] = own
            o_ref[1 - c] = got

        _sibling_stream(pl.program_id(0), nj, acc_ref[...], consume, *scratch)

    sent = lambda t: jnp.minimum(t, nj - 1)
    grid_spec = pltpu.PrefetchScalarGridSpec(
        num_scalar_prefetch=1, grid=(nj + 1,),
        in_specs=[pl.BlockSpec((s, cr, c_), lambda t, where: (0, sent(t), 0)),
                  pl.BlockSpec((None, cr, c_), lambda t, where: (where[1], sent(t), 0))],
        out_specs=pl.BlockSpec((2, cr, c_), lambda t, where: (0, jnp.maximum(t - 1, 0), 0)),
        scratch_shapes=[pltpu.VMEM((cr, c_), F32)] + _swap_scratch(cr, c_, F32))
    x, y, c = _position()
    where = jnp.stack([c, 2 * x + y]).astype(jnp.int32)
    out = pl.pallas_call(body, name=name, grid_spec=grid_spec, out_shape=_sds((2, r2, c_), F32),
                         compiler_params=_params("arbitrary", collective_id=SIBLING_BARRIER))(where, p, mine)
    return out.reshape(2 * r2, c_)


def sibling_fill(name, buf):
    s, r, c_ = buf.shape
    r2 = r // 2
    cr = _chunk_rows(r2, c_, buf.dtype)
    nj = r2 // cr
    n_peers = len(CHIP_FLIPS)
    n = n_peers * nj

    def body(where, in_ref, o_ref, *scratch):
        def consume(_, got):
            o_ref[...] = got

        _sibling_stream(pl.program_id(0), n, in_ref[...], consume, *scratch)

    sent = lambda t: jnp.minimum(t, n - 1)
    used = lambda t: jnp.maximum(t - 1, 0)
    grid_spec = pltpu.PrefetchScalarGridSpec(
        num_scalar_prefetch=1, grid=(n + 1,),
        in_specs=[pl.BlockSpec((None, cr, c_),
                               lambda t, where: (where[sent(t) // nj], where[n_peers] * nj + sent(t) % nj, 0))],
        out_specs=pl.BlockSpec((None, cr, c_),
                               lambda t, where: (where[used(t) // nj], (1 - where[n_peers]) * nj + used(t) % nj, 0)),
        scratch_shapes=_swap_scratch(cr, c_, buf.dtype))
    x, y, c = _position()
    where = jnp.stack([2 * (x ^ fx) + (y ^ fy) for fx, fy in CHIP_FLIPS] + [c]).astype(jnp.int32)
    return pl.pallas_call(body, name=name, grid_spec=grid_spec, out_shape=_sds(buf.shape, buf.dtype),
                          input_output_aliases={1: 0},
                          compiler_params=_params("arbitrary", collective_id=SIBLING_BARRIER))(where, buf)


def sum_slots(name, p, out_dtype):
    s, r, c = p.shape
    tr = _tile(r, 256, 16)

    def body(p_ref, o_ref):
        acc = p_ref[0].astype(F32)
        for i in range(1, s):
            acc = acc + p_ref[i].astype(F32)
        o_ref[...] = acc.astype(o_ref.dtype)

    return pl.pallas_call(
        body, name=name, grid=(r // tr,), in_specs=[pl.BlockSpec((s, tr, c), lambda i: (0, i, 0))],
        out_specs=pl.BlockSpec((tr, c), lambda i: (i, 0)), out_shape=_sds((r, c), out_dtype),
        compiler_params=_params("arbitrary"),
    )(p)


def adamw(name, w, g, m, v):
    r, c = w.shape
    outs, _ = _rowwise(name, lambda _, w, g, m, v: ([*_adamw_math(w, g, m, v), g], []), [w, g, m, v], [],
                       [(c, F32)] * 4, [], _tile(r, 256, 8))
    return outs


RET_SCALE = HEAD_DIM ** -0.5
MLA_SCALE = (HEAD_DIM + MLA_ROPE) ** -0.5
GRAD_DT = BF16
IN_RET = 4 * HEADS * HEAD_DIM
IN_MLA = MLA_Q_RANK + MLA_KV_RANK + MLA_ROPE
IN_MLA_PAD = IN_MLA + 64


def _heads(fn):
    return jnp.concatenate([fn(h) for h in range(HEADS)], axis=1)


def _head(a, h, stride=HEAD_DIM, off=0):
    return a[:, h * stride + off:h * stride + off + HEAD_DIM]


def _group_norm(o):
    rs = [lax.rsqrt(jnp.mean(_head(o, h) * _head(o, h), axis=-1, keepdims=True) + EPS) for h in range(HEADS)]
    return _heads(lambda h: _head(o, h) * rs[h]), rs


class _Alone:
    def host(self, kernel_name):
        return None

    def done(self, kernel_name, w):
        pass

    def grads(self, g):
        pass


def _ffn_fwd(tag, n, w, k, tm, tmk, plan):
    gate, up, down = tag + "_gate", tag + "_up", tag + "_down"
    if "wu" + k in w:
        g, u, a = mm_nn(up, n, w["wg" + k], tm, outs=[BF16] * 3, w2=w["wu" + k],
                        epilogue=lambda g, u: (g, u, _silu(g) * u), host=plan.host(up))
    else:
        (g,) = mm_nn(gate, n, w["wg" + k], tm, out_dtype=BF16, host=plan.host(gate))
        plan.done(gate, w)
        u, a = mm_nn(up, n, w["wu" + k], tm, outs=[BF16] * 2, extras=(g,),
                     epilogue=lambda u, g: (u, _silu(g.astype(F32)) * u), host=plan.host(up))
    plan.done(up, w)
    ff = a.shape[1]
    wd = w["wd" + k]
    f = mm_nn_k(down, a, wd.reshape(ff, wd.shape[2]), tmk, _tile(ff, 1408), host=plan.host(down))
    plan.done(down, w)
    return g, u, a, f


def _ffn_bwd(tag, df, n, g, u, a, wg, wu, wd, tm, tmk, tk, dt, plan):
    ns = wg.shape[2]

    def gate_grads(da, g, u):
        g, u = g.astype(F32), u.astype(F32)
        return da * u * _dsilu(g), da * _silu(g)

    def hosted(kernel_name, call):
        out = call(plan.host(kernel_name))
        plan.done(kernel_name, None)
        return out

    k = tag[-1]
    dg, du = hosted(tag + "_da", lambda h: mm_nt(tag + "_da", [df], [wd], tm, ns, outs=(BF16, BF16),
                                                 epilogue=gate_grads, extras=(g, u), host=h))
    dwg = hosted(tag + "_dwg", lambda h: mm_tn(tag + "_dwg", n, dg, dt, ns, tk, GRAD_DT, shard_cols=True, host=h))
    plan.grads({"wg" + k: dwg})
    dwu = hosted(tag + "_dwu", lambda h: mm_tn(tag + "_dwu", n, du, dt, ns, tk, GRAD_DT, shard_cols=True, host=h))
    plan.grads({"wu" + k: dwu})
    dwd = hosted(tag + "_dwd", lambda h: mm_tn(tag + "_dwd", a, df, ns, dt, tk, GRAD_DT, shard_rows=True, host=h))
    plan.grads({"wd" + k: dwd})
    dn = hosted(tag + "_dn", lambda h: mm_nt_k(tag + "_dn", [dg, du], [wg, wu], tmk, dt, out_dtype=BF16, host=h))
    return dn


def _local_step(x, tgt, meta, w, nw, plan):
    seq, d = x.shape
    t_real = N_META + seq
    tp = -(-t_real // LANES) * LANES
    zpad = jnp.zeros((tp - t_real, d), F32)
    h0 = jnp.concatenate([meta, x, zpad], axis=0)
    tgt_p = jnp.concatenate([jnp.zeros((N_META, d), F32), tgt, zpad], axis=0)
    cos_r, sin_r, cos_m, sa, sb = _rope_tables(tp)
    tm = _tile(tp, 512)
    tmt = _tile(tp, 768, 16)
    tmk = _tile(tp, 1408)
    tk = tp
    blk = tm
    dt = _tile(d, 1024)
    hw = HEADS * HEAD_DIM
    qw = HEADS * MLA_QK_PAD

    (n1,), _ = _rowwise("ffn1_norm", lambda r0, h, g: ([_rms(h, g)], []), [h0], [nw["ffn1_pre_norm"]],
                        [(d, BF16)], [], tm, host=plan.host("ffn1_norm"))
    plan.done("ffn1_norm", w)
    g1, u1, a1, f1 = _ffn_fwd("ffn1", n1, w, "1", tm, tmk, plan)

    def post_ffn1(r0, h, f, post, pre):
        h1 = h + 0.5 * _rms(f, post)
        return [h1, _rms(h1, pre)], []

    (h1, un), _ = _rowwise("mix_norm", post_ffn1, [h0, f1], [nw["ffn1_post_norm"], nw["mix_pre_norm"]],
                           [(d, F32), (d, BF16)], [], tm)
    (proj_r,) = mm_nn("proj_r", un, w["w_r"], tm, n_block=_tile(IN_RET, 1024), host=plan.host("proj_r"))
    plan.done("proj_r", w)
    (proj_c,) = mm_nn("proj_c", un, w["w_c"], tm)

    def split_proj(r0, pr, pc, cr, sr, cm, ta, tb, qn, kvn):
        rq = _heads(lambda h: _rope_r(_head(pr, h), cr, sr))
        rk = _heads(lambda h: _rope_r(_head(pr, h, off=hw), cr, sr) * RET_SCALE)
        rv = pr[:, 2 * hw:3 * hw]
        cqn = _rms(pc[:, :MLA_Q_RANK], qn)
        ckvn = _rms(pc[:, MLA_Q_RANK:MLA_Q_RANK + MLA_KV_RANK], kvn)
        krr = _rope_m(pc[:, MLA_Q_RANK + MLA_KV_RANK:], cm, ta, tb)
        return [rq, rk, rv, cqn, ckvn, krr], []

    (rq, rk, rv, cqn, ckvn, krr), _ = _rowwise(
        "split_proj", split_proj, [proj_r, proj_c, cos_r, sin_r, cos_m, sa, sb],
        [nw["mla_q_norm"], nw["mla_kv_norm"]],
        [(hw, BF16), (hw, BF16), (hw, BF16), (MLA_Q_RANK, BF16), (MLA_KV_RANK, BF16), (LANES, F32)], [], tm)
    (qp,) = mm_nn("q_up", cqn, w["wuq"], tm)
    (kn,) = mm_nn("k_up", ckvn, w["wuk"], tm)
    (vv,) = mm_nn("v_up", ckvn, w["wuv"], tm, out_dtype=BF16)

    def build_qk(r0, qp, kn, krr, cm, ta, tb):
        qc = jnp.concatenate(
            [part for h in range(HEADS)
             for part in (_head(qp, h, MLA_QK_PAD), _rope_m(_head(qp, h, MLA_QK_PAD, HEAD_DIM), cm, ta, tb))], axis=1)
        kc = jnp.concatenate([part for h in range(HEADS) for part in (_head(kn, h), krr)], axis=1)
        return [qc, kc], []

    (qc, kc), _ = _rowwise("build_qk", build_qk, [qp, kn, krr, cos_m, sa, sb], [], [(qw, BF16), (qw, BF16)], [], tm)
    o_m, lse, o_r, ret_states = mix_fwd("mix_fwd", qc, kc, vv, rq, rk, rv, blk, MLA_SCALE, host=plan.host("mix_fwd"))
    plan.done("mix_fwd", w)

    def gate_mix(r0, rg, o_r, o_m, gn):
        y, _ = _group_norm(o_r)
        return [jnp.concatenate([_silu(rg) * (y * gn), o_m], axis=1)], []

    (mixcat,), _ = _rowwise("gate_mix", gate_mix, [(proj_r, hw, 3), o_r, o_m], [nw["ret_group_norm"]],
                            [(2 * hw, BF16)], [], tm)
    (mix,) = mm_nn("mix_out", mixcat, w["w_out"], tm, n_block=dt)

    def post_mix(r0, h, m, post, pre):
        h2 = h + _rms(m, post)
        return [h2, _rms(h2, pre)], []

    (h2, n3), _ = _rowwise("ffn2_norm", post_mix, [h1, mix], [nw["mix_post_norm"], nw["ffn2_pre_norm"]],
                           [(d, F32), (d, BF16)], [], tm)
    g2, u2, a2, f2 = _ffn_fwd("ffn2", n3, w, "2", tm, tmk, plan)

    def loss_head(r0, h, f, t, post):
        h3 = h + 0.5 * _rms(f, post)
        row = r0 + lax.broadcasted_iota(jnp.int32, h3.shape, 0)
        err = jnp.where(row >= N_META, jnp.where(row < t_real, h3 - t, 0.0), 0.0)
        dh3 = err / d
        df, dpost = _rms_bwd(f, post, 0.5 * dh3)
        return [dh3, df], [_colsum(err * err), _colsum(dpost)]

    (dh3, df2), (loss_vec, d_post2) = _rowwise("loss_head", loss_head, [h2, f2, tgt_p], [nw["ffn2_post_norm"]],
                                               [(d, F32), (d, BF16)], [d, d], tm)
    loss = 0.5 * jnp.sum(loss_vec) / d
    dn3 = _ffn_bwd("ffn2", df2, n3, g2, u2, a2, w["wg2"], w["wu2"], w["wd2"], tmt, tmk, tk, dt, plan)

    def back_mix_norm(r0, h, m, dh3, dn, pre, post):
        dx, dpre = _rms_bwd(h, pre, dn)
        dh2 = dh3 + dx
        dm, dpost = _rms_bwd(m, post, dh2)
        return [dh2, dm], [_colsum(dpre), _colsum(dpost)]

    (dh2, dmix), (d_pre2, d_mix_post) = _rowwise(
        "back_mix_norm", back_mix_norm, [h2, mix, dh3, dn3], [nw["ffn2_pre_norm"], nw["mix_post_norm"]],
        [(d, F32), (d, BF16)], [d, d], tm)
    (dmixcat,) = mm_nt("mix_dx", [dmix], [w["w_out"]], tmt, _tile(2 * hw, 512), outs=(BF16,))
    plan.grads(dict(w_out=mm_tn("mix_dw", mixcat, dmix, 2 * hw // N_CHIPS, dt, tk, GRAD_DT, shard_rows=True)))

    def back_gate(r0, dmc, rg, o_r, gn):
        d_ret, d_om = dmc[:, :hw], dmc[:, hw:]
        yh, rs = _group_norm(o_r)
        d_rg = d_ret * (yh * gn) * _dsilu(rg)
        dy = d_ret * _silu(rg)
        gyh = dy * gn
        d_or = _heads(lambda h: rs[h] * (_head(gyh, h) - _head(yh, h) * jnp.mean(_head(gyh, h) * _head(yh, h),
                                                                                    axis=-1, keepdims=True)))
        return [d_or, d_rg, d_om], [_colsum(dy * yh)]

    (d_or, d_rg, d_om), (d_gn,) = _rowwise("back_gate", back_gate, [dmixcat, (proj_r, hw, 3), o_r],
                                           [nw["ret_group_norm"]], [(hw, BF16), (hw, F32), (hw, BF16)], [hw], tm)
    dqc, dkc, dvv = attn_bwd("mla_bwd", qc, kc, vv, d_om, o_m, lse, blk, MLA_SCALE, host=plan.host("mla_bwd"))
    plan.done("mla_bwd", None)
    drq, drk, drv = ret_bwd("ret_bwd", rq, rk, rv, d_or, ret_states, blk)

    def back_qk(r0, dqc, dkc, dvv, cm, ta, tb):
        dkc = dkc.astype(F32)
        dqp = jnp.concatenate(
            [part for h in range(HEADS)
             for part in (_head(dqc, h, MLA_QK_PAD), _rope_m_t(_head(dqc, h, MLA_QK_PAD, HEAD_DIM), cm, ta, tb))],
            axis=1)
        dkn = _heads(lambda h: _head(dkc, h, MLA_QK_PAD))
        dkr = _head(dkc, 0, MLA_QK_PAD, HEAD_DIM)
        for h in range(1, HEADS):
            dkr = dkr + _head(dkc, h, MLA_QK_PAD, HEAD_DIM)
        return [dqp, dkn, _rope_m_t(dkr, cm, ta, tb), dvv], []

    (dqp, dkn, dkr, dvb), _ = _rowwise("back_qk", back_qk, [dqc, dkc, dvv, cos_m, sa, sb], [],
                                       [(qw, BF16), (hw, BF16), (LANES, F32), (hw, BF16)], [], tm)
    (dcqn,) = mm_nt("q_dx", [dqp], [w["wuq"]], tm, MLA_Q_RANK)
    dwuq = mm_tn("q_dw", cqn, dqp, MLA_Q_RANK, _tile(qw, 1024), tk, GRAD_DT)
    (dckvn,) = mm_nt("kv_dx", [dkn, dvb], [w["wuk"], w["wuv"]], tm, MLA_KV_RANK)
    dwuk = mm_tn("k_dw", ckvn, dkn, MLA_KV_RANK, hw, tk, GRAD_DT)
    dwuv = mm_tn("v_dw", ckvn, dvb, MLA_KV_RANK, hw, tk, GRAD_DT)

    def back_proj(r0, drq, drk, drv, d_rg, pc, dcqn, dckvn, dkr, cr, sr, qn, kvn):
        d_q = _heads(lambda h: _rope_r_t(_head(drq, h), cr, sr))
        d_k = _heads(lambda h: _rope_r_t(_head(drk, h), cr, sr) * RET_SCALE)
        dcq, a_q = _rms_bwd(pc[:, :MLA_Q_RANK], qn, dcqn)
        dckv, a_kv = _rms_bwd(pc[:, MLA_Q_RANK:MLA_Q_RANK + MLA_KV_RANK], kvn, dckvn)
        return ([jnp.concatenate([d_q, d_k, drv, d_rg], axis=1), jnp.concatenate([dcq, dckv, dkr], axis=1)],
                [_colsum(a_q), _colsum(a_kv)])

    (dproj_r, dproj_c), (d_qn, d_kvn) = _rowwise(
        "back_proj", back_proj, [drq, drk, drv, d_rg, proj_c, dcqn, dckvn, dkr, cos_r, sin_r],
        [nw["mla_q_norm"], nw["mla_kv_norm"]], [(IN_RET, BF16), (IN_MLA_PAD, BF16)],
        [MLA_Q_RANK, MLA_KV_RANK], tm)
    (dun,) = mm_nt("proj_dx", [dproj_r, dproj_c], [w["w_r"], w["w_c"]], tmt, _tile(d, 512), outs=(BF16,))
    dw_r = mm_tn("proj_dw_r", un, dproj_r, dt, _tile(IN_RET, 1024), tk, GRAD_DT)
    dw_c = mm_tn("proj_dw_c", un, dproj_c, dt, IN_MLA_PAD, tk, GRAD_DT)
    plan.grads(dict(w_r=dw_r, w_c=dw_c, wuq=dwuq, wuk=dwuk, wuv=dwuv))

    def back_ffn1_norm(r0, h, f, dh2, dn, pre, post):
        dx, dpre = _rms_bwd(h, pre, dn)
        dh1 = dh2 + dx
        df, dpost = _rms_bwd(f, post, 0.5 * dh1)
        return [dh1, df], [_colsum(dpre), _colsum(dpost)]

    (dh1, df1), (d_mix_pre, d_post1) = _rowwise(
        "back_ffn1_norm", back_ffn1_norm, [h1, f1, dh2, dun], [nw["mix_pre_norm"], nw["ffn1_post_norm"]],
        [(d, F32), (d, BF16)], [d, d], tm)
    dn1 = _ffn_bwd("ffn1", df1, n1, g1, u1, a1, w["wg1"], w["wu1"], w["wd1"], tmt, tmk, tk, dt, plan)

    def back_input(r0, h, dh1, dn, pre):
        dx, dpre = _rms_bwd(h, pre, dn)
        return [dh1 + dx], [_colsum(dpre)]

    (dh0,), (d_pre1,) = _rowwise("back_input", back_input, [h0, dh1, dn1], [nw["ffn1_pre_norm"]], [(d, F32)], [d], tm)

    small = dict(ffn1_pre_norm=d_pre1, ffn1_post_norm=d_post1, mix_pre_norm=d_mix_pre, ret_group_norm=d_gn,
                 mla_q_norm=d_qn, mla_kv_norm=d_kvn, mix_post_norm=d_mix_post, ffn2_pre_norm=d_pre2,
                 ffn2_post_norm=d_post2)
    return loss, dh0[N_META:t_real], small, dh0[:N_META]


WEIGHTS = ("meta_tokens", "ffn1_pre_norm", "ffn1_w_gate", "ffn1_w_up", "ffn1_w_down", "ffn1_post_norm",
           "mix_pre_norm", "w_in", "ret_group_norm", "mla_q_norm", "mla_w_uq", "mla_kv_norm", "mla_w_uk",
           "mla_w_uv", "w_out", "mix_post_norm", "ffn2_pre_norm", "ffn2_w_gate", "ffn2_w_up", "ffn2_w_down",
           "ffn2_post_norm")
BIG = ("ffn1_w_gate", "ffn1_w_up", "ffn1_w_down", "w_in", "mla_w_uq", "mla_w_uk", "mla_w_uv", "w_out",
       "ffn2_w_gate", "ffn2_w_up", "ffn2_w_down")
NORMS = ("ffn1_pre_norm", "ffn1_post_norm", "mix_pre_norm", "ret_group_norm", "mla_q_norm", "mla_kv_norm",
         "mix_post_norm", "ffn2_pre_norm", "ffn2_post_norm")


def _unshard_cols(g):
    return g.transpose(1, 0, 2).reshape(g.shape[1], -1)


def _shard_cols(a):
    return a.reshape(a.shape[0], N_CHIPS, -1).transpose(1, 0, 2)


def _pack_rows(rows, width):
    rows = [jnp.pad(r, ((0, 0), (0, width - r.shape[1]))) for r in rows]
    n = sum(r.shape[0] for r in rows)
    return jnp.pad(jnp.concatenate(rows, axis=0), ((0, -n % 8), (0, 0)))


def _weight_views(full):
    w = {}
    for n, g in full.items():
        if n == "w_in":
            w_in = _unshard_cols(g)
            w["w_r"] = w_in[:, :IN_RET]
            w["w_c"] = jnp.pad(w_in[:, IN_RET:], ((0, 0), (0, IN_MLA_PAD - IN_MLA)))
        elif n == "mla_w_uq":
            q = _unshard_cols(g).reshape(MLA_Q_RANK, HEADS, HEAD_DIM + MLA_ROPE)
            q = jnp.pad(q, ((0, 0), (0, 0), (0, MLA_QK_PAD - HEAD_DIM - MLA_ROPE)))
            w["wuq"] = q.reshape(MLA_Q_RANK, HEADS * MLA_QK_PAD)
        elif n in ("mla_w_uk", "mla_w_uv"):
            w["wu" + n[-1]] = _unshard_cols(g)
        elif n == "w_out":
            w["w_out"] = g.reshape(-1, g.shape[2])
        else:
            w["w" + n[7] + n[3]] = g
    return w


def _contributions(g):
    ffn = {"g": "gate", "u": "up", "d": "down"}
    c = {f"ffn{n[2]}_w_{ffn[n[1]]}": a for n, a in g.items() if len(n) == 3 and n[2] in "12"}
    if "w_out" in g:
        c["w_out"] = g["w_out"]
    if "w_r" in g:
        dwuq = g["wuq"].reshape(MLA_Q_RANK, HEADS, MLA_QK_PAD)[:, :, :HEAD_DIM + MLA_ROPE]
        c.update(w_in=_shard_cols(jnp.concatenate([g["w_r"], g["w_c"][:, :IN_MLA]], axis=1)),
                 mla_w_uq=_shard_cols(dwuq.reshape(MLA_Q_RANK, -1)), mla_w_uk=_shard_cols(g["wuk"]),
                 mla_w_uv=_shard_cols(g["wuv"]))
    return c


class _Schedule(_Alone):
    FIRST = ()
    CARRIED = {"ffn1_norm": ("ffn1_w_gate",), "ffn1_gate": ("ffn1_w_up",), "ffn1_up": ("ffn1_w_down",),
               "ffn1_down": ("w_in", "mla_w_uq", "mla_w_uk", "mla_w_uv"),
               "proj_r": ("w_out",), "mix_fwd": ("ffn2_w_gate", "ffn2_w_up"), "ffn2_up": ("ffn2_w_down",)}
    GRAD_HOST = dict(ffn2_w_gate="mla_bwd", ffn2_w_up="mla_bwd", ffn2_w_down="ffn2_dn",
                     w_out="mla_bwd", w_in="ffn1_da", mla_w_uq="ffn1_da", mla_w_uk="ffn1_da", mla_w_uv="ffn1_da",
                     ffn1_w_gate="ffn1_dwu", ffn1_w_up="ffn1_dn", ffn1_w_down="ffn1_dn")

    def __init__(self, bufs):
        self.gathers = {k: (gather_halves([bufs[n] for n in names]), names) for k, names in self.CARRIED.items()}
        self.waiting = {}
        self.exchanges = {}
        self.grad = {}

    def host(self, kernel_name):
        if kernel_name in self.gathers:
            return self.gathers[kernel_name][0]
        if kernel_name in self.waiting:
            names, sums = zip(*self.waiting.pop(kernel_name))
            self.exchanges[kernel_name] = (chip_exchange(list(sums)), names, sums)
            return self.exchanges[kernel_name][0]
        return None

    def done(self, kernel_name, w):
        if kernel_name in self.gathers:
            comm, names = self.gathers[kernel_name]
            w.update(_weight_views({n: sibling_fill("fill_" + n, b) for n, b in zip(names, comm.result)}))
        elif kernel_name in self.exchanges:
            comm, names, sums = self.exchanges[kernel_name]
            for n, q, mine in zip(names, comm.result, sums):
                self.grad[n] = reduce_join("reduce_join_" + n, q, mine)

    def grads(self, g):
        for n, a in _contributions(g).items():
            self.waiting.setdefault(self.GRAD_HOST[n], []).append((n, pair_add("pair_add_" + n, a)))


def _step(p, m, v, x, loss_target):
    d = x.shape[2]
    names = list(BIG)
    bufs = {n: own_slot(p[n][0].astype(BF16)) for n in names}
    first = _Schedule.FIRST
    gathered = gather_halves([bufs[n] for n in first] + [own_slot(p["meta_tokens"])]).run("gather_first")
    filled = [sibling_fill("fill_" + n, b) for n, b in zip(first + ("meta_tokens",), gathered)]
    w = _weight_views(dict(zip(first, filled[:-1])))
    meta = _unshard_cols(filled[-1])
    nw = {n: p[n] for n in NORMS}
    plan = _Schedule(bufs)
    loss, grad_x, small, d_meta = _local_step(x[0], loss_target[0], meta, w, nw, plan)
    grads = dict(plan.grad)

    width = max(d, HEADS * HEAD_DIM)
    packed = _pack_rows([small[n] for n in NORMS] + [d_meta], width)
    total = sum_slots("small_sum", all_gather_devices(packed), F32)
    for i, n in enumerate(NORMS):
        grads[n] = total[i:i + 1, :p[n].shape[1]]
    cols = p["meta_tokens"].shape[1]
    chip = 2 * lax.axis_index("x") + lax.axis_index("y")
    grads["meta_tokens"] = lax.dynamic_slice(total[len(NORMS):len(NORMS) + N_META, :d], (0, chip * cols), (N_META, cols))

    delta, new_m, new_v = {}, {}, {}
    for n in names + ["meta_tokens"]:
        shape = p[n].shape
        flat = lambda a: a.reshape(-1, shape[-1])
        out = adamw("adamw_" + n, flat(p[n]), flat(grads[n]), flat(m[n]), flat(v[n]))
        delta[n], new_m[n], new_v[n], grads[n] = (o.reshape(shape) for o in out)
    pk = lambda src: _pack_rows([src[n] for n in NORMS], width)
    out = adamw("adamw_norms", pk(p), pk(grads), pk(m), pk(v))
    for i, n in enumerate(NORMS):
        delta[n], new_m[n], new_v[n] = (o[i:i + 1, :p[n].shape[1]] for o in out[:3])

    loss = lax.psum(loss, ("x", "y", "c"))
    return loss, grad_x[None], grads, delta, new_m, new_v


def kernel(x, meta_tokens, ffn1_pre_norm, ffn1_w_gate, ffn1_w_up, ffn1_w_down, ffn1_post_norm, mix_pre_norm, w_in, ret_group_norm, mla_q_norm, mla_w_uq, mla_kv_norm, mla_w_uk, mla_w_uv, w_out, mix_post_norm, ffn2_pre_norm, ffn2_w_gate, ffn2_w_up, ffn2_w_down, ffn2_post_norm, loss_target, m_meta_tokens, m_ffn1_pre_norm, m_ffn1_w_gate, m_ffn1_w_up, m_ffn1_w_down, m_ffn1_post_norm, m_mix_pre_norm, m_w_in, m_ret_group_norm, m_mla_q_norm, m_mla_w_uq, m_mla_kv_norm, m_mla_w_uk, m_mla_w_uv, m_w_out, m_mix_post_norm, m_ffn2_pre_norm, m_ffn2_w_gate, m_ffn2_w_up, m_ffn2_w_down, m_ffn2_post_norm, v_meta_tokens, v_ffn1_pre_norm, v_ffn1_w_gate, v_ffn1_w_up, v_ffn1_w_down, v_ffn1_post_norm, v_mix_pre_norm, v_w_in, v_ret_group_norm, v_mla_q_norm, v_mla_w_uq, v_mla_kv_norm, v_mla_w_uk, v_mla_w_uv, v_w_out, v_mix_post_norm, v_ffn2_pre_norm, v_ffn2_w_gate, v_ffn2_w_up, v_ffn2_w_down, v_ffn2_post_norm):
    args = locals()
    p = {n: args[n] for n in WEIGHTS}
    m = {n: args["m_" + n] for n in WEIGHTS}
    v = {n: args["v_" + n] for n in WEIGHTS}
    loss, grad_x, grads, delta, new_m, new_v = _step(p, m, v, x, loss_target)
    return (loss, grad_x, *[grads[n] for n in WEIGHTS], *[delta[n] for n in WEIGHTS],
            *[new_m[n] for n in WEIGHTS], *[new_v[n] for n in WEIGHTS])
```
